```python
import jax, jax.numpy as jnp
from jax import lax
import numpy as np

D_MODEL = 1024
BATCH = 8
SEQ = 4096
DEPTH = 2

CHUNK = 64
N_PREV_CHUNKS = 8
BAND = (N_PREV_CHUNKS + 1) * CHUNK
HEAD_DIM = 64
H_A = 8
H_B = 8
W_A = H_A * HEAD_DIM
W_B = H_B * HEAD_DIM
W_MIX = W_A + W_B
W_IN = 3 * W_MIX
REL_CLIP = 128
SB_BLOCK = 128
D_FF = 2816
CONV_WIDTH = 3
N_MOD = 6
EPS = 1e-6

kernel_name = "hybrid_chunked_stickbreaking_convffn"


def rms_norm(x):
    xf = x.astype(jnp.float32)
    y = xf * lax.rsqrt(jnp.mean(xf * xf, axis=-1, keepdims=True) + EPS)
    return y.astype(x.dtype)


def modulate(h, shift, scale):
    return h * (1 + scale[:, None, :]) + shift[:, None, :]


def rel_bias_band(rel_bias):
    q_pos = N_PREV_CHUNKS * CHUNK + jnp.arange(CHUNK)
    k_pos = jnp.arange(BAND)
    dist = jnp.clip(q_pos[:, None] - k_pos[None, :], -REL_CLIP, REL_CLIP) + REL_CLIP
    return rel_bias[:, dist]


def chunked_rel_attention(q, k, v, rel_bias):
    b, s, h, dh = q.shape
    n_chunks = s // CHUNK
    qc = q.reshape(b, n_chunks, CHUNK, h, dh)
    pad = ((0, 0), (N_PREV_CHUNKS, 0), (0, 0), (0, 0), (0, 0))
    kp = jnp.pad(k.reshape(b, n_chunks, CHUNK, h, dh), pad)
    vp = jnp.pad(v.reshape(b, n_chunks, CHUNK, h, dh), pad)
    band_idx = jnp.arange(n_chunks)[:, None] + jnp.arange(N_PREV_CHUNKS + 1)[None, :]
    kb = kp[:, band_idx].reshape(b, n_chunks, BAND, h, dh)
    vb = vp[:, band_idx].reshape(b, n_chunks, BAND, h, dh)
    scores = jnp.einsum("bcqhd,bckhd->bhcqk", qc, kb).astype(jnp.float32) * (dh ** -0.5)
    scores = scores + rel_bias_band(rel_bias).astype(jnp.float32)[None, :, None]
    key_chunk = (jnp.arange(n_chunks)[:, None] - N_PREV_CHUNKS
                 + (jnp.arange(BAND) // CHUNK)[None, :])
    scores = jnp.where((key_chunk >= 0)[None, None, :, None, :], scores, -jnp.inf)
    probs = jax.nn.softmax(scores, axis=-1).astype(v.dtype)
    out = jnp.einsum("bhcqk,bckhd->bcqhd", probs, vb)
    return out.reshape(b, s, h * dh)


def stick_breaking_attention(q, k, v):
    b, s, h, dh = q.shape
    scale = dh ** -0.5
    outs = []
    for blk in range(s // SB_BLOCK):
        q_start = blk * SB_BLOCK
        k_end = q_start + SB_BLOCK
        logits = jnp.einsum("bqhd,bkhd->bhqk", q[:, q_start:k_end],
                            k[:, :k_end]).astype(jnp.float32) * scale
        strict = (q_start + jnp.arange(SB_BLOCK))[:, None] > jnp.arange(k_end)[None, :]
        log_beta = jax.nn.log_sigmoid(logits)
        log_keep = jnp.where(strict, jax.nn.log_sigmoid(-logits), 0.0)
        log_tail = lax.cumsum(log_keep, axis=3, reverse=True) - log_keep
        weights = jnp.where(strict, jnp.exp(log_beta + log_tail), 0.0).astype(v.dtype)
        outs.append(jnp.einsum("bhqk,bkhd->bqhd", weights, v[:, :k_end]))
    return jnp.concatenate(outs, axis=1).reshape(b, s, h * dh)


def causal_depthwise_conv(u, w, bias):
    s = u.shape[1]
    up = jnp.pad(u, ((0, 0), (CONV_WIDTH - 1, 0), (0, 0)))
    y = w[0] * up[:, 0:s]
    for i in range(1, CONV_WIDTH):
        y = y + w[i] * up[:, i:i + s]
    return y + bias


def hybrid_layer(x, c_act, w_ada, b_ada, w_in, rel_bias, g_a, g_b, w_out,
                 w_up, conv_w, conv_b, w_down):
    b, s, _ = x.shape
    mod = c_act @ w_ada + b_ada
    shift_mix, scale_mix, gate_mix, shift_ffn, scale_ffn, gate_ffn = jnp.split(mod, N_MOD, axis=-1)

    h = modulate(rms_norm(x), shift_mix, scale_mix)
    proj = h @ w_in
    cuts = [W_A, 2 * W_A, 3 * W_A, 3 * W_A + W_B, 3 * W_A + 2 * W_B]
    qa, ka, va, qb, kb, vb = jnp.split(proj, cuts, axis=-1)
    oa = chunked_rel_attention(qa.reshape(b, s, H_A, HEAD_DIM), ka.reshape(b, s, H_A, HEAD_DIM),
                               va.reshape(b, s, H_A, HEAD_DIM), rel_bias)
    ob = stick_breaking_attention(qb.reshape(b, s, H_B, HEAD_DIM), kb.reshape(b, s, H_B, HEAD_DIM),
                                  vb.reshape(b, s, H_B, HEAD_DIM))
    mixed = jnp.concatenate([rms_norm(oa) * g_a, rms_norm(ob) * g_b], axis=-1) @ w_out
    x = x + gate_mix[:, None, :] * mixed

    h = modulate(rms_norm(x), shift_ffn, scale_ffn)
    gate, val = jnp.split(causal_depthwise_conv(h @ w_up, conv_w, conv_b), 2, axis=-1)
    x = x + gate_ffn[:, None, :] * ((jax.nn.silu(gate) * val) @ w_down)
    return x


def _fwd_setup_inputs(seed: int = 0) -> dict:
    key = jax.random.key(seed)
    ks = jax.random.split(key, 14)
    f32 = jnp.float32
    nrm = lambda k, shape: jax.random.normal(k, shape, dtype=f32)
    return {
        "x": nrm(ks[0], (BATCH, SEQ, D_MODEL)),
        "c": nrm(ks[1], (BATCH, D_MODEL)),
        "w_ada": nrm(ks[2], (DEPTH, D_MODEL, N_MOD * D_MODEL)) * D_MODEL ** -0.5,
        "b_ada": nrm(ks[3], (DEPTH, N_MOD * D_MODEL)) * 0.02,
        "w_in": nrm(ks[4], (DEPTH, D_MODEL, W_IN)) * D_MODEL ** -0.5,
        "rel_bias": nrm(ks[5], (DEPTH, H_A, 2 * REL_CLIP + 1)) * 0.5,
        "g_a": 1.0 + 0.1 * nrm(ks[6], (DEPTH, W_A)),
        "g_b": 1.0 + 0.1 * nrm(ks[7], (DEPTH, W_B)),
        "w_out": nrm(ks[8], (DEPTH, W_MIX, D_MODEL)) * W_MIX ** -0.5,
        "w_up": nrm(ks[9], (DEPTH, D_MODEL, 2 * D_FF)) * D_MODEL ** -0.5,
        "conv_w": nrm(ks[10], (DEPTH, CONV_WIDTH, 2 * D_FF)) * CONV_WIDTH ** -0.5,
        "conv_b": nrm(ks[11], (DEPTH, 2 * D_FF)) * 0.02,
        "w_down": nrm(ks[12], (DEPTH, D_FF, D_MODEL)) * D_FF ** -0.5,
        "final_g": 1.0 + 0.1 * nrm(ks[13], (D_MODEL,)),
    }


def _fwd_reference(x, c, w_ada, b_ada, w_in, rel_bias, g_a, g_b, w_out, w_up, conv_w,
              conv_b, w_down, final_g):
    c_act = jax.nn.silu(c)
    for l in range(DEPTH):
        x = hybrid_layer(x, c_act, w_ada[l], b_ada[l], w_in[l], rel_bias[l], g_a[l], g_b[l],
                         w_out[l], w_up[l], conv_w[l], conv_b[l], w_down[l])
    return rms_norm(x) * final_g


import jax as _jax
import jax.numpy as _jnp

TWIN_FORMAT = 'train_step'
FWD_PARAMS = ['x', 'c', 'w_ada', 'b_ada', 'w_in', 'rel_bias', 'g_a', 'g_b', 'w_out', 'w_up', 'conv_w', 'conv_b', 'w_down', 'final_g']
TWIN_WEIGHTS = ['w_ada', 'b_ada', 'w_in', 'rel_bias', 'g_a', 'g_b', 'w_out', 'w_up', 'conv_w', 'conv_b', 'w_down', 'final_g']
TWIN_DIFF_INPUT = 'x'
TWIN_INPUTS = ['x', 'c', 'w_ada', 'b_ada', 'w_in', 'rel_bias', 'g_a', 'g_b', 'w_out', 'w_up', 'conv_w', 'conv_b', 'w_down', 'final_g', 'loss_target', 'm_w_ada', 'm_b_ada', 'm_w_in', 'm_rel_bias', 'm_g_a', 'm_g_b', 'm_w_out', 'm_w_up', 'm_conv_w', 'm_conv_b', 'm_w_down', 'm_final_g', 'v_w_ada', 'v_b_ada', 'v_w_in', 'v_rel_bias', 'v_g_a', 'v_g_b', 'v_w_out', 'v_w_up', 'v_conv_w', 'v_conv_b', 'v_w_down', 'v_final_g']
TWIN_OUTPUTS = ['loss', 'grad_x', 'grad_w_ada', 'grad_b_ada', 'grad_w_in', 'grad_rel_bias', 'grad_g_a', 'grad_g_b', 'grad_w_out', 'grad_w_up', 'grad_conv_w', 'grad_conv_b', 'grad_w_down', 'grad_final_g', 'delta_w_ada', 'delta_b_ada', 'delta_w_in', 'delta_rel_bias', 'delta_g_a', 'delta_g_b', 'delta_w_out', 'delta_w_up', 'delta_conv_w', 'delta_conv_b', 'delta_w_down', 'delta_final_g', 'new_m_w_ada', 'new_m_b_ada', 'new_m_w_in', 'new_m_rel_bias', 'new_m_g_a', 'new_m_g_b', 'new_m_w_out', 'new_m_w_up', 'new_m_conv_w', 'new_m_conv_b', 'new_m_w_down', 'new_m_final_g', 'new_v_w_ada', 'new_v_b_ada', 'new_v_w_in', 'new_v_rel_bias', 'new_v_g_a', 'new_v_g_b', 'new_v_w_out', 'new_v_w_up', 'new_v_conv_w', 'new_v_conv_b', 'new_v_w_down', 'new_v_final_g']
TWIN_LEAF_KINDS = {'loss': 'loss', 'grad_x': 'grad_x', 'grad_w_ada': 'grad_w', 'grad_b_ada': 'grad_w', 'grad_w_in': 'grad_w', 'grad_rel_bias': 'grad_w', 'grad_g_a': 'grad_w', 'grad_g_b': 'grad_w', 'grad_w_out': 'grad_w', 'grad_w_up': 'grad_w', 'grad_conv_w': 'grad_w', 'grad_conv_b': 'grad_w', 'grad_w_down': 'grad_w', 'grad_final_g': 'grad_w', 'delta_w_ada': 'delta_w', 'delta_b_ada': 'delta_w', 'delta_w_in': 'delta_w', 'delta_rel_bias': 'delta_w', 'delta_g_a': 'delta_w', 'delta_g_b': 'delta_w', 'delta_w_out': 'delta_w', 'delta_w_up': 'delta_w', 'delta_conv_w': 'delta_w', 'delta_conv_b': 'delta_w', 'delta_w_down': 'delta_w', 'delta_final_g': 'delta_w', 'new_m_w_ada': 'new_m', 'new_m_b_ada': 'new_m', 'new_m_w_in': 'new_m', 'new_m_rel_bias': 'new_m', 'new_m_g_a': 'new_m', 'new_m_g_b': 'new_m', 'new_m_w_out': 'new_m', 'new_m_w_up': 'new_m', 'new_m_conv_w': 'new_m', 'new_m_conv_b': 'new_m', 'new_m_w_down': 'new_m', 'new_m_final_g': 'new_m', 'new_v_w_ada': 'new_v', 'new_v_b_ada': 'new_v', 'new_v_w_in': 'new_v', 'new_v_rel_bias': 'new_v', 'new_v_g_a': 'new_v', 'new_v_g_b': 'new_v', 'new_v_w_out': 'new_v', 'new_v_w_up': 'new_v', 'new_v_conv_w': 'new_v', 'new_v_conv_b': 'new_v', 'new_v_w_down': 'new_v', 'new_v_final_g': 'new_v'}


def _forward(args):
    return _fwd_reference(*[args[k] for k in FWD_PARAMS])


def _output_shape():
    out = _jax.eval_shape(lambda: _forward(_fwd_setup_inputs(0)))
    return out.shape, out.dtype

N_MICROBATCH = 1
ADAM_LR = 0.001
ADAM_B1 = 0.9
ADAM_B2 = 0.999
ADAM_EPS = 1e-08
ADAM_WD = 0.01
ADAM_STEP = 10
PER_EXAMPLE_BATCH_AXIS = {'x': 0, 'c': 0, 'loss_target': 0}
SHARED_INPUTS = []
_WEIGHT_DTYPES = {'w_ada': _jnp.float32, 'b_ada': _jnp.float32, 'w_in': _jnp.float32, 'rel_bias': _jnp.float32, 'g_a': _jnp.float32, 'g_b': _jnp.float32, 'w_out': _jnp.float32, 'w_up': _jnp.float32, 'conv_w': _jnp.float32, 'conv_b': _jnp.float32, 'w_down': _jnp.float32, 'final_g': _jnp.float32}
MOMENT_SCALE = {'w_ada': 4.937254e-01, 'b_ada': 8.592082e-01, 'w_in': 3.753227e-01, 'rel_bias': 1.550148e-02, 'g_a': 7.742828e-01, 'g_b': 5.030268e-01, 'w_out': 7.000143e-01, 'w_up': 1.513246e-01, 'conv_w': 1.650612e-01, 'conv_b': 1.960036e-01, 'w_down': 2.737713e-01, 'final_g': 3.386836e+01}


def _to_microbatches(a, axis):
    t = _jnp.moveaxis(a, axis, 0)
    t = t.reshape((N_MICROBATCH, t.shape[0] // N_MICROBATCH) + t.shape[1:])
    return _jnp.moveaxis(t, 1, axis + 1)


def setup_inputs(seed: int = 0) -> dict:
    inp = _fwd_setup_inputs(seed)
    key = _jax.random.fold_in(_jax.random.key(seed), 7919)
    shape, _ = _output_shape()
    out = dict(inp)
    out["loss_target"] = _jax.random.normal(_jax.random.fold_in(key, 0), shape, _jnp.float32)
    for i, name in enumerate(TWIN_WEIGHTS):
        w = inp[name].astype(_jnp.float32)
        if MOMENT_SCALE is None:
            s = _jnp.sqrt(_jnp.mean(_jnp.square(w)) + 1e-30)
        else:
            s = MOMENT_SCALE[name]
        km, kv = _jax.random.split(_jax.random.fold_in(key, i + 1))
        out[name] = w
        out["m_" + name] = s * _jax.random.normal(km, w.shape, _jnp.float32)
        out["v_" + name] = (s * s) * _jax.random.uniform(kv, w.shape, _jnp.float32, 0.5, 1.5)
    if N_MICROBATCH > 1:
        for name, axis in PER_EXAMPLE_BATCH_AXIS.items():
            out[name] = _to_microbatches(out[name], axis)
    return {'x': out['x'], 'c': out['c'], 'w_ada': out['w_ada'], 'b_ada': out['b_ada'], 'w_in': out['w_in'], 'rel_bias': out['rel_bias'], 'g_a': out['g_a'], 'g_b': out['g_b'], 'w_out': out['w_out'], 'w_up': out['w_up'], 'conv_w': out['conv_w'], 'conv_b': out['conv_b'], 'w_down': out['w_down'], 'final_g': out['final_g'], 'loss_target': out['loss_target'], 'm_w_ada': out['m_w_ada'], 'm_b_ada': out['m_b_ada'], 'm_w_in': out['m_w_in'], 'm_rel_bias': out['m_rel_bias'], 'm_g_a': out['m_g_a'], 'm_g_b': out['m_g_b'], 'm_w_out': out['m_w_out'], 'm_w_up': out['m_w_up'], 'm_conv_w': out['m_conv_w'], 'm_conv_b': out['m_conv_b'], 'm_w_down': out['m_w_down'], 'm_final_g': out['m_final_g'], 'v_w_ada': out['v_w_ada'], 'v_b_ada': out['v_b_ada'], 'v_w_in': out['v_w_in'], 'v_rel_bias': out['v_rel_bias'], 'v_g_a': out['v_g_a'], 'v_g_b': out['v_g_b'], 'v_w_out': out['v_w_out'], 'v_w_up': out['v_w_up'], 'v_conv_w': out['v_conv_w'], 'v_conv_b': out['v_conv_b'], 'v_w_down': out['v_w_down'], 'v_final_g': out['v_final_g']}


def _loss(weights, diff, rest, loss_target):
    with _jax.named_scope("forward"):
        args = {**rest, TWIN_DIFF_INPUT: diff, **{k: w.astype(_WEIGHT_DTYPES[k]) for k, w in weights.items()}}
        y = _forward(args)
    with _jax.named_scope("loss_head"):
        err = _jnp.square(y.astype(_jnp.float32) - loss_target)
        return 0.5 * _jnp.sum(_jnp.mean(err, axis=-1)) if err.ndim else 0.5 * err


def _adamw(w, g, m, v):
    m = ADAM_B1 * m + (1.0 - ADAM_B1) * g
    v = ADAM_B2 * v + (1.0 - ADAM_B2) * _jnp.square(g)
    m_hat = m / (1.0 - ADAM_B1 ** ADAM_STEP)
    v_hat = v / (1.0 - ADAM_B2 ** ADAM_STEP)
    delta = -ADAM_LR * (m_hat / (_jnp.sqrt(v_hat) + ADAM_EPS) + ADAM_WD * w)
    return delta, m, v


def reference(x, c, w_ada, b_ada, w_in, rel_bias, g_a, g_b, w_out, w_up, conv_w, conv_b, w_down, final_g, loss_target, m_w_ada, m_b_ada, m_w_in, m_rel_bias, m_g_a, m_g_b, m_w_out, m_w_up, m_conv_w, m_conv_b, m_w_down, m_final_g, v_w_ada, v_b_ada, v_w_in, v_rel_bias, v_g_a, v_g_b, v_w_out, v_w_up, v_conv_w, v_conv_b, v_w_down, v_final_g):
    given = dict(x=x, c=c, w_ada=w_ada, b_ada=b_ada, w_in=w_in, rel_bias=rel_bias, g_a=g_a, g_b=g_b, w_out=w_out, w_up=w_up, conv_w=conv_w, conv_b=conv_b, w_down=w_down, final_g=final_g, loss_target=loss_target, m_w_ada=m_w_ada, m_b_ada=m_b_ada, m_w_in=m_w_in, m_rel_bias=m_rel_bias, m_g_a=m_g_a, m_g_b=m_g_b, m_w_out=m_w_out, m_w_up=m_w_up, m_conv_w=m_conv_w, m_conv_b=m_conv_b, m_w_down=m_w_down, m_final_g=m_final_g, v_w_ada=v_w_ada, v_b_ada=v_b_ada, v_w_in=v_w_in, v_rel_bias=v_rel_bias, v_g_a=v_g_a, v_g_b=v_g_b, v_w_out=v_w_out, v_w_up=v_w_up, v_conv_w=v_conv_w, v_conv_b=v_conv_b, v_w_down=v_w_down, v_final_g=v_final_g)
    weights = {n: given[n] for n in TWIN_WEIGHTS}
    shared = {n: given[n] for n in SHARED_INPUTS}
    per_example = {n: given[n] for n in ['x', 'c']}
    grad_fn = _jax.value_and_grad(_loss, argnums=(0, 1))

    def one_microbatch(ex, loss_target):
        ex = dict(ex)
        diff = ex.pop(TWIN_DIFF_INPUT)
        return grad_fn(weights, diff, {**shared, **ex}, loss_target)

    if N_MICROBATCH == 1:
        loss, (grad_w, grad_x) = one_microbatch(per_example, given["loss_target"])
    else:
        def body(carry, xs):
            loss_sum, grad_sum = carry
            l_k, (gw_k, gx_k) = one_microbatch(xs[0], xs[1])
            with _jax.named_scope("update"):
                return (loss_sum + l_k, _jax.tree.map(_jnp.add, grad_sum, gw_k)), gx_k

        init = (_jnp.zeros((), _jnp.float32), _jax.tree.map(_jnp.zeros_like, weights))
        (loss, grad_w), grad_x = _jax.lax.scan(body, init, (per_example, given["loss_target"]))
    with _jax.named_scope("update"):
        delta_w, new_m, new_v = {}, {}, {}
        for n in TWIN_WEIGHTS:
            delta_w[n], new_m[n], new_v[n] = _adamw(weights[n], grad_w[n], given["m_" + n], given["v_" + n])
    return (loss, grad_x, *[grad_w[n] for n in TWIN_WEIGHTS], *[delta_w[n] for n in TWIN_WEIGHTS],
            *[new_m[n] for n in TWIN_WEIGHTS], *[new_v[n] for n in TWIN_WEIGHTS])
```

```python
import jax
import jax.numpy as jnp
from jax import lax
from jax.experimental import pallas as pl
from jax.experimental.pallas import tpu as pltpu

F32, BF16 = jnp.float32, jnp.bfloat16
MESH_ID = pl.DeviceIdType.MESH
NDEV = 8
D = 1024
HD = 64
WG = 512
NT = 6
GU = 704
DFF = 2816
CHUNK, NPREV, REL_CLIP = 64, 8, 128
BAND = (NPREV + 1) * CHUNK
EPS = 1e-6
NEG = -1e30
TQA = 256
KWA = TQA + NPREV * CHUNK
TQB, TKB = 256, 128
LR, B1, B2, AEPS, WD, STEP = 0.001, 0.9, 0.999, 1e-08, 0.01, 10
VMEM_MB = 56


def _call(body, **kw):
    return pl.pallas_call(body, **kw)


def _params(n_axes):
    return pltpu.CompilerParams(dimension_semantics=("arbitrary",) * n_axes, vmem_limit_bytes=VMEM_MB << 20)


def _dot(a, b):
    return jnp.dot(a, b, preferred_element_type=F32)


def _dot_nt(a, b):
    return lax.dot_general(a, b, (((1,), (1,)), ((), ())), preferred_element_type=F32)


def _dot_tn(a, b):
    return lax.dot_general(a, b, (((0,), (0,)), ((), ())), preferred_element_type=F32)


def _dot2(x, u):
    hi = x.astype(BF16)
    lo = (x - hi.astype(F32)).astype(BF16)
    return _dot(hi, u) + _dot(lo, u)


def _rms(x):
    return lax.rsqrt(jnp.mean(x * x, axis=-1, keepdims=True) + EPS)


def _rms_bwd(dn, n, r):
    return r * (dn - n * jnp.mean(dn * n, axis=-1, keepdims=True))


def _colsum(x):
    return jnp.sum(x, axis=0, keepdims=True)


def _sigmoid(x):
    return 1.0 / (1.0 + jnp.exp(-x))


def _sds(shape, dtype):
    return jax.ShapeDtypeStruct(shape, dtype)


def _place():
    x, y, c = lax.axis_index("x"), lax.axis_index("y"), lax.axis_index("c")
    return x, y, c, 4 * x + 2 * y + c


def _peer(x, y, c, k):
    px = 1 - x if k & 4 else x
    py = 1 - y if k & 2 else y
    pc = 1 - c if k & 1 else c
    return (px, py, pc), 4 * px + 2 * py + pc


def _small_allgather(v, name):
    rows, cols = v.shape

    def body(v_ref, out_ref, send_sems, recv_sems, local_sem):
        x, y, c, me = _place()
        mine = pltpu.make_async_copy(v_ref, out_ref.at[me], local_sem)
        mine.start()
        sends = []
        for k in range(1, NDEV):
            peer, _ = _peer(x, y, c, k)
            cp = pltpu.make_async_remote_copy(v_ref, out_ref.at[me], send_sems.at[k - 1], recv_sems.at[k - 1],
                                              device_id=peer, device_id_type=MESH_ID)
            cp.start()
            sends.append(cp)
        for k in range(1, NDEV):
            peer, pidx = _peer(x, y, c, k)
            pltpu.make_async_remote_copy(v_ref, out_ref.at[pidx], send_sems.at[k - 1], recv_sems.at[k - 1],
                                         device_id=peer, device_id_type=MESH_ID).wait_recv()
        for cp in sends:
            cp.wait_send()
        mine.wait()

    return _call(
        body, name=name,
        out_shape=_sds((NDEV, rows, cols), F32),
        in_specs=[pl.BlockSpec(memory_space=pltpu.VMEM)],
        out_specs=pl.BlockSpec(memory_space=pltpu.VMEM),
        scratch_shapes=[pltpu.SemaphoreType.DMA((NDEV - 1,)), pltpu.SemaphoreType.DMA((NDEV - 1,)),
                        pltpu.SemaphoreType.DMA],
    )(v)


def _shard_view(ref, kind, p):
    if kind == "in":
        return ref.at[:, pl.ds(pl.multiple_of(p * 384, 128), 384)]
    if kind == "out":
        return ref.at[pl.ds(pl.multiple_of(p * 128, 128), 128), :]
    if kind == "up":
        return ref.at[p]
    if kind == "down":
        return ref.at[pl.ds(pl.multiple_of(p * 352, 16), 352), :]
    raise ValueError(kind)


_KINDS = ("in", "out", "up", "down")
_FULL_SHAPES = {"in": (D, 3 * D), "out": (D, D), "up": (NDEV, D, GU), "down": (DFF, D)}
_SHARD_SHAPES = {"in": (D, 384), "out": (128, D), "up": (D, GU), "down": (352, D)}


def _weights_allgather(shards):
    n = len(_KINDS) * 2

    def body(*refs):
        ins, outs = refs[:4], refs[4:4 + n]
        send_sems, recv_sems, local_sems = refs[4 + n:]
        x, y, c, me = _place()
        started = []
        for t, kind in enumerate(_KINDS):
            for l in range(2):
                j = 2 * t + l
                src = ins[t].at[l]
                mine = pltpu.make_async_copy(src, _shard_view(outs[j], kind, me), local_sems.at[j])
                mine.start()
                started.append(mine)
        sends = []
        for k in range(1, NDEV):
            peer, _ = _peer(x, y, c, k)
            for t, kind in enumerate(_KINDS):
                for l in range(2):
                    j = 2 * t + l
                    cp = pltpu.make_async_remote_copy(ins[t].at[l], _shard_view(outs[j], kind, me),
                                                      send_sems.at[j, k - 1], recv_sems.at[j, k - 1],
                                                      device_id=peer, device_id_type=MESH_ID)
                    cp.start()
                    sends.append(cp)
        for k in range(1, NDEV):
            peer, pidx = _peer(x, y, c, k)
            for t, kind in enumerate(_KINDS):
                for l in range(2):
                    j = 2 * t + l
                    pltpu.make_async_remote_copy(ins[t].at[l], _shard_view(outs[j], kind, pidx),
                                                 send_sems.at[j, k - 1], recv_sems.at[j, k - 1],
                                                 device_id=peer, device_id_type=MESH_ID).wait_recv()
        for cp in sends:
            cp.wait_send()
        for cp in started:
            cp.wait()

    out_shape = [_sds(_FULL_SHAPES[kind], BF16) for kind in _KINDS for _ in range(2)]
    any_spec = pl.BlockSpec(memory_space=pl.ANY)
    outs = _call(
        body, name="weights_allgather",
        out_shape=out_shape,
        in_specs=[any_spec] * 4, out_specs=[any_spec] * n,
        scratch_shapes=[pltpu.SemaphoreType.DMA((n, NDEV - 1)), pltpu.SemaphoreType.DMA((n, NDEV - 1)),
                        pltpu.SemaphoreType.DMA((n,))],
    )(*shards)
    return {kind: (outs[2 * t], outs[2 * t + 1]) for t, kind in enumerate(_KINDS)}


def _grads_exchange(grads):
    n = len(_KINDS) * 2

    def body(*refs):
        ins, outs = refs[:n], refs[n:n + 4]
        send_sems, recv_sems, local_sems = refs[n + 4:]
        x, y, c, me = _place()
        started = []
        for t, kind in enumerate(_KINDS):
            for l in range(2):
                j = 2 * t + l
                mine = pltpu.make_async_copy(_shard_view(ins[j], kind, me), outs[t].at[me, l], local_sems.at[j])
                mine.start()
                started.append(mine)
        sends = []
        for k in range(1, NDEV):
            peer, pidx = _peer(x, y, c, k)
            for t, kind in enumerate(_KINDS):
                for l in range(2):
                    j = 2 * t + l
                    cp = pltpu.make_async_remote_copy(_shard_view(ins[j], kind, pidx), outs[t].at[me, l],
                                                      send_sems.at[j, k - 1], recv_sems.at[j, k - 1],
                                                      device_id=peer, device_id_type=MESH_ID)
                    cp.start()
                    sends.append(cp)
        for k in range(1, NDEV):
            peer, pidx = _peer(x, y, c, k)
            for t, kind in enumerate(_KINDS):
                for l in range(2):
                    j = 2 * t + l
                    pltpu.make_async_remote_copy(_shard_view(ins[j], kind, me), outs[t].at[pidx, l],
                                                 send_sems.at[j, k - 1], recv_sems.at[j, k - 1],
                                                 device_id=peer, device_id_type=MESH_ID).wait_recv()
        for cp in sends:
            cp.wait_send()
        for cp in started:
            cp.wait()

    out_shape = [_sds((NDEV, 2) + _SHARD_SHAPES[kind], BF16) for kind in _KINDS]
    any_spec = pl.BlockSpec(memory_space=pl.ANY)
    flat = [grads[kind][l] for kind in _KINDS for l in range(2)]
    outs = _call(
        body, name="grads_exchange",
        out_shape=out_shape,
        in_specs=[any_spec] * n, out_specs=[any_spec] * 4,
        scratch_shapes=[pltpu.SemaphoreType.DMA((n, NDEV - 1)), pltpu.SemaphoreType.DMA((n, NDEV - 1)),
                        pltpu.SemaphoreType.DMA((n,))],
    )(*flat)
    return dict(zip(_KINDS, outs))


def _mod_fwd(c_all, w_ada, b_sl):
    def body(c_ref, w_ref, b_ref, mod_ref, cact_ref):
        cv = c_ref[...]
        ca = cv * _sigmoid(cv)
        cact_ref[...] = ca
        mod_ref[0] = _dot(ca.astype(BF16), w_ref[0].astype(BF16)) + b_ref[0]

    return _call(
        body, name="mod_fwd", grid=(2,),
        in_specs=[pl.BlockSpec((NDEV, D), lambda l: (0, 0)), pl.BlockSpec((1, D, 768), lambda l: (l, 0, 0)),
                  pl.BlockSpec((1, 1, 768), lambda l: (l, 0, 0))],
        out_specs=[pl.BlockSpec((1, NDEV, 768), lambda l: (l, 0, 0)), pl.BlockSpec((NDEV, D), lambda l: (0, 0))],
        out_shape=[_sds((2, NDEV, 768), F32), _sds((NDEV, D), F32)],
        compiler_params=_params(1),
    )(c_all, w_ada, b_sl)


def _nm_matmul(x, shift, scale, w, *, two_d, n, groups, out_dtype, tm, name):
    s = x.shape[0]

    def body(x_ref, sh_ref, sc_ref, w_ref, h_ref, o_ref):
        @pl.when(pl.program_id(1) == 0)
        def _():
            xv = x_ref[...]
            h_ref[...] = ((xv * _rms(xv)) * (1.0 + sc_ref[...]) + sh_ref[...]).astype(BF16)
        wv = w_ref[...] if two_d else w_ref[0]
        o_ref[0] = _dot(h_ref[...], wv).astype(out_dtype)

    vec = pl.BlockSpec((1, D), lambda i, g: (0, 0))
    w_spec = pl.BlockSpec((D, n), lambda i, g: (0, g)) if two_d else pl.BlockSpec((1, D, n), lambda i, g: (g, 0, 0))
    return _call(
        body, name=name, grid=(s // tm, groups),
        in_specs=[pl.BlockSpec((tm, D), lambda i, g: (i, 0)), vec, vec, w_spec],
        out_specs=[pl.BlockSpec((tm, D), lambda i, g: (i, 0)), pl.BlockSpec((1, tm, n), lambda i, g: (g, i, 0))],
        out_shape=[_sds((s, D), BF16), _sds((groups, s, n), out_dtype)],
        compiler_params=_params(2),
    )(x, shift, scale, w)


def _bias_table(rel_bias):
    q_pos = NPREV * CHUNK + jnp.arange(CHUNK)
    k_pos = jnp.arange(BAND)
    dist = jnp.clip(q_pos[:, None] - k_pos[None, :], -REL_CLIP, REL_CLIP) + REL_CLIP
    band = rel_bias[:, dist]
    rows = []
    for i in range(TQA // CHUNK):
        rows.append(jnp.pad(band, ((0, 0), (0, 0), (CHUNK * i, KWA - BAND - CHUNK * i)), constant_values=NEG))
    return jnp.concatenate(rows, axis=1)


def _attn_a_cases(qi, tile):
    @pl.when(qi == 0)
    def _():
        tile(TQA, 2 * TQA, 0)

    @pl.when(qi == 1)
    def _():
        tile(2 * TQA, TQA, 0)

    @pl.when(qi >= 2)
    def _():
        tile(KWA, 0, pl.multiple_of((qi - 2) * TQA, TQA))


def _attn_a_specs(s):
    q_spec = pl.BlockSpec((1, TQA, 128), lambda hp, qi: (0, qi, hp))
    k_spec = pl.BlockSpec((1, s, 128), lambda hp, qi: (1, 0, hp))
    v_spec = pl.BlockSpec((1, s, 128), lambda hp, qi: (2, 0, hp))
    b_spec = pl.BlockSpec((2, TQA, KWA), lambda hp, qi: (hp, 0, 0))
    return q_spec, k_spec, v_spec, b_spec


def _attn_a_fwd(proj, bias_tab):
    s = proj.shape[1]

    def body(q_ref, k_ref, v_ref, b_ref, o_ref):
        def tile(nk, off, kstart):
            for h in range(2):
                sl = slice(HD * h, HD * (h + 1))
                q = q_ref[0, :, sl]
                k = k_ref[0, pl.ds(kstart, nk), sl]
                v = v_ref[0, pl.ds(kstart, nk), sl]
                sc = _dot_nt(q, k) * 0.125 + b_ref[h, :, off:off + nk]
                p = jnp.exp(sc - jnp.max(sc, axis=-1, keepdims=True))
                den = jnp.sum(p, axis=-1, keepdims=True)
                o_ref[:, sl] = _dot(p.astype(BF16), v) / den

        _attn_a_cases(pl.program_id(1), tile)

    q_spec, k_spec, v_spec, b_spec = _attn_a_specs(s)
    return _call(
        body, name="attn_a_fwd", grid=(4, s // TQA),
        in_specs=[q_spec, k_spec, v_spec, b_spec],
        out_specs=pl.BlockSpec((TQA, 128), lambda hp, qi: (qi, hp)),
        out_shape=_sds((s, WG), F32),
        compiler_params=_params(2),
    )(proj, proj, proj, bias_tab)


def _sb_logits(q, k):
    lg = _dot_nt(q, k) * 0.125
    sp = jnp.maximum(lg, 0.0) + jnp.log(1.0 + jnp.exp(-jnp.abs(lg)))
    return lg - sp, -sp


def _attn_b_specs(s):
    q_spec = pl.BlockSpec((1, TQB, 128), lambda hp, qi: (3, qi, hp))
    k_spec = pl.BlockSpec((1, s, 128), lambda hp, qi: (4, 0, hp))
    v_spec = pl.BlockSpec((1, s, 128), lambda hp, qi: (5, 0, hp))
    return q_spec, k_spec, v_spec


def _attn_b_fwd(proj):
    s = proj.shape[1]
    ndiag = TQB // TKB

    def body(q_ref, k_ref, v_ref, o_ref, ls_ref):
        q0 = pl.program_id(1) * TQB
        rows = lax.broadcasted_iota(jnp.int32, (TQB, TKB), 0)
        cols = lax.broadcasted_iota(jnp.int32, (TQB, TKB), 1)
        uj = lax.broadcasted_iota(jnp.int32, (TKB, TKB), 0)
        us = lax.broadcasted_iota(jnp.int32, (TKB, TKB), 1)
        suffix = jnp.where(uj >= us, 1.0, 0.0).astype(BF16)
        for h in range(2):
            sl = slice(HD * h, HD * (h + 1))
            q = q_ref[0, :, sl]

            def tile(kstart, diag, carry, acc):
                k = k_ref[0, pl.ds(kstart, TKB), sl]
                v = v_ref[0, pl.ds(kstart, TKB), sl]
                lb, lk = _sb_logits(q, k)
                if diag is not None:
                    strict = rows > cols + diag * TKB
                    lk = jnp.where(strict, lk, 0.0)
                csum = _dot2(lk, suffix) + carry
                w = jnp.exp(lb + csum - lk)
                if diag is not None:
                    w = jnp.where(strict, w, 0.0)
                return csum[:, 0:1], acc + _dot(w.astype(BF16), v)

            state = (jnp.zeros((TQB, 1), F32), jnp.zeros((TQB, HD), F32))
            for d in range(ndiag - 1, -1, -1):
                state = tile(pl.multiple_of(q0 + d * TKB, TKB), d, *state)
            nfull = q0 // TKB

            def step(i, st):
                return tile(pl.multiple_of((nfull - 1 - i) * TKB, TKB), None, *st)

            carry, acc = lax.fori_loop(0, nfull, step, state)
            o_ref[:, sl] = acc
            ls_ref[0, :, sl] = jnp.broadcast_to(carry, (TQB, HD))

    q_spec, k_spec, v_spec = _attn_b_specs(s)
    return _call(
        body, name="attn_b_fwd", grid=(4, s // TQB),
        in_specs=[q_spec, k_spec, v_spec],
        out_specs=[pl.BlockSpec((TQB, 128), lambda hp, qi: (qi, hp)),
                   pl.BlockSpec((1, TQB, 128), lambda hp, qi: (hp, qi, 0))],
        out_shape=[_sds((s, WG), F32), _sds((4, s, 128), F32)],
        compiler_params=_params(2),
    )(proj, proj, proj)


def _mix_out(oa, ob, g, w_out, x, gate, tm):
    s = x.shape[0]

    def body(oa_ref, ob_ref, g_ref, w_ref, x_ref, gate_ref, nab_ref, mixed_ref, x2_ref):
        a, b = oa_ref[...], ob_ref[...]
        nab_ref[:, :WG] = (a * _rms(a) * g_ref[:, :WG]).astype(BF16)
        nab_ref[:, WG:] = (b * _rms(b) * g_ref[:, WG:]).astype(BF16)
        mixed = _dot(nab_ref[...], w_ref[...])
        mixed_ref[...] = mixed
        x2_ref[...] = x_ref[...] + gate_ref[...] * mixed

    row = pl.BlockSpec((tm, D), lambda i: (i, 0))
    half = pl.BlockSpec((tm, WG), lambda i: (i, 0))
    vec = pl.BlockSpec((1, D), lambda i: (0, 0))
    return _call(
        body, name="mix_out", grid=(s // tm,),
        in_specs=[half, half, vec, pl.BlockSpec((D, D), lambda i: (0, 0)), row, vec],
        out_specs=[row, row, row],
        out_shape=[_sds((s, D), BF16), _sds((s, D), F32), _sds((s, D), F32)],
        compiler_params=_params(1),
    )(oa, ob, g, w_out, x, gate)


def _conv_taps(u, halo, first, tm):
    rows = lax.broadcasted_iota(jnp.int32, (tm, 1), 0)
    keep = jnp.where(first, 0.0, 1.0)
    h1 = halo[7:8, :] * keep
    h2 = halo[6:7, :] * keep
    um1 = jnp.where(rows == 0, h1, pltpu.roll(u, 1, 0))
    um2 = jnp.where(rows == 0, h2, jnp.where(rows == 1, h1, pltpu.roll(u, 2, 0)))
    return um1, um2


def _conv_specs(tm):
    u_spec = pl.BlockSpec((2, 1, tm, GU), lambda p, i: (0, p, i, 0))
    halo_spec = pl.BlockSpec((2, 1, 8, GU), lambda p, i: (0, p, jnp.maximum(i * (tm // 8) - 1, 0), 0))
    cw_spec = pl.BlockSpec((2, 1, 3, GU), lambda p, i: (0, p, 0, 0))
    cb_spec = pl.BlockSpec((2, 1, 1, GU), lambda p, i: (0, p, 0, 0))
    return u_spec, halo_spec, cw_spec, cb_spec


def _conv_act(u, conv_w, conv_b, tm):
    s = u.shape[2]

    def body(u_ref, halo_ref, cw_ref, cb_ref, a_ref):
        first = pl.program_id(1) == 0
        ys = []
        for side in range(2):
            uv = u_ref[side, 0]
            um1, um2 = _conv_taps(uv, halo_ref[side, 0], first, tm)
            cw = cw_ref[side, 0]
            ys.append(cw[2:3] * uv + cw[1:2] * um1 + cw[0:1] * um2 + cb_ref[side, 0])
        a_ref[0] = (ys[0] * _sigmoid(ys[0]) * ys[1]).astype(BF16)

    u_spec, halo_spec, cw_spec, cb_spec = _conv_specs(tm)
    return _call(
        body, name="conv_act", grid=(4, s // tm),
        in_specs=[u_spec, halo_spec, cw_spec, cb_spec],
        out_specs=pl.BlockSpec((1, tm, GU), lambda p, i: (p, i, 0)),
        out_shape=_sds((4, s, GU), BF16),
        compiler_params=_params(2),
    )(u, u, conv_w, conv_b)


def _down(a, w_down, x2, gate, tm):
    s = x2.shape[0]

    def body(a_ref, w_ref, x_ref, gate_ref, ffn_ref, x3_ref):
        p = pl.program_id(1)
        part = _dot(a_ref[0], w_ref[0])

        @pl.when(p == 0)
        def _():
            ffn_ref[...] = part

        @pl.when(p > 0)
        def _():
            ffn_ref[...] += part

        @pl.when(p == 3)
        def _():
            x3_ref[...] = x_ref[...] + gate_ref[...] * ffn_ref[...]

    row = pl.BlockSpec((tm, D), lambda i, p: (i, 0))
    return _call(
        body, name="down", grid=(s // tm, 4),
        in_specs=[pl.BlockSpec((1, tm, GU), lambda i, p: (p, i, 0)), pl.BlockSpec((1, GU, D), lambda i, p: (p, 0, 0)),
                  row, pl.BlockSpec((1, D), lambda i, p: (0, 0))],
        out_specs=[row, row],
        out_shape=[_sds((s, D), F32), _sds((s, D), F32)],
        compiler_params=_params(2),
    )(a, w_down, x2, gate)


def _final_loss(x, g, target, tm):
    s = x.shape[0]

    def body(x_ref, g_ref, t_ref, loss_ref, dx_ref, dg_ref):
        @pl.when(pl.program_id(0) == 0)
        def _():
            loss_ref[...] = jnp.zeros_like(loss_ref)
            dg_ref[...] = jnp.zeros_like(dg_ref)
        xv = x_ref[...]
        r = _rms(xv)
        nrm = xv * r
        err = nrm * g_ref[...] - t_ref[...]
        loss_ref[...] += (0.5 / D) * jnp.sum(jnp.sum(err * err, axis=-1, keepdims=True), axis=0, keepdims=True)
        dy = err * (1.0 / D)
        dg_ref[...] += _colsum(dy * nrm)
        dx_ref[...] = _rms_bwd(dy * g_ref[...], nrm, r)

    row = pl.BlockSpec((tm, D), lambda i: (i, 0))
    vec = pl.BlockSpec((1, D), lambda i: (0, 0))
    return _call(
        body, name="final_loss", grid=(s // tm,),
        in_specs=[row, vec, row],
        out_specs=[pl.BlockSpec((1, 1), lambda i: (0, 0)), row, vec],
        out_shape=[_sds((1, 1), F32), _sds((s, D), F32), _sds((1, D), F32)],
        compiler_params=_params(1),
    )(x, g, target)


def _down_bwd(dx3, gate, ffn, w_down, tm):
    s = dx3.shape[0]

    def body(dx_ref, gate_ref, ffn_ref, w_ref, dgate_ref, dff_ref, da_ref):
        i, p = pl.program_id(0), pl.program_id(1)

        @pl.when((i == 0) & (p == 0))
        def _():
            dgate_ref[...] = jnp.zeros_like(dgate_ref)

        @pl.when(p == 0)
        def _():
            dxv = dx_ref[...]
            dgate_ref[...] += _colsum(dxv * ffn_ref[...])
            dff_ref[...] = (dxv * gate_ref[...]).astype(BF16)

        da_ref[0] = _dot_nt(dff_ref[...], w_ref[0])

    row = pl.BlockSpec((tm, D), lambda i, p: (i, 0))
    vec = pl.BlockSpec((1, D), lambda i, p: (0, 0))
    return _call(
        body, name="down_bwd", grid=(s // tm, 4),
        in_specs=[row, vec, row, pl.BlockSpec((1, GU, D), lambda i, p: (p, 0, 0))],
        out_specs=[vec, row, pl.BlockSpec((1, tm, GU), lambda i, p: (p, i, 0))],
        out_shape=[_sds((1, D), F32), _sds((s, D), BF16), _sds((4, s, GU), F32)],
        compiler_params=_params(2),
    )(dx3, gate, ffn, w_down)


def _conv_act_bwd(u, conv_w, conv_b, da, tm):
    s = u.shape[2]

    def body(u_ref, halo_ref, cw_ref, cb_ref, da_ref, dy_ref, dcw_ref, dcb_ref):
        first = pl.program_id(1) == 0

        @pl.when(first)
        def _():
            dcw_ref[...] = jnp.zeros_like(dcw_ref)
            dcb_ref[...] = jnp.zeros_like(dcb_ref)

        taps, ys = [], []
        for side in range(2):
            uv = u_ref[side, 0]
            um1, um2 = _conv_taps(uv, halo_ref[side, 0], first, tm)
            cw = cw_ref[side, 0]
            taps.append((um2, um1, uv))
            ys.append(cw[2:3] * uv + cw[1:2] * um1 + cw[0:1] * um2 + cb_ref[side, 0])
        dav = da_ref[0]
        sg = _sigmoid(ys[0])
        dys = (dav * ys[1] * (sg * (1.0 + ys[0] * (1.0 - sg))), dav * (ys[0] * sg))
        for side in range(2):
            dy_ref[side, 0] = dys[side]
            dcb_ref[side, 0] += _colsum(dys[side])
            for j in range(3):
                dcw_ref[side, 0, j:j + 1, :] += _colsum(dys[side] * taps[side][j])

    u_spec, halo_spec, cw_spec, cb_spec = _conv_specs(tm)
    return _call(
        body, name="conv_act_bwd", grid=(4, s // tm),
        in_specs=[u_spec, halo_spec, cw_spec, cb_spec, pl.BlockSpec((1, tm, GU), lambda p, i: (p, i, 0))],
        out_specs=[u_spec, cw_spec, cb_spec],
        out_shape=[_sds((2, 4, s, GU), F32), _sds((2, 4, 3, GU), F32), _sds((2, 4, 1, GU), F32)],
        compiler_params=_params(2),
    )(u, u, conv_w, conv_b, da)


def _conv_transpose(dy, conv_w, tm):
    s = dy.shape[1]
    nt = s // tm

    def body(dy_ref, halo_ref, cw_ref, du_ref):
        keep = jnp.where(pl.program_id(1) == nt - 1, 0.0, 1.0)
        dv = dy_ref[0]
        rows = lax.broadcasted_iota(jnp.int32, (tm, 1), 0)
        h0 = halo_ref[0, 0:1, :] * keep
        h1 = halo_ref[0, 1:2, :] * keep
        dp1 = jnp.where(rows == tm - 1, h0, pltpu.roll(dv, tm - 1, 0))
        dp2 = jnp.where(rows == tm - 1, h1, jnp.where(rows == tm - 2, h0, pltpu.roll(dv, tm - 2, 0)))
        cw = cw_ref[0]
        du_ref[0] = (cw[2:3] * dv + cw[1:2] * dp1 + cw[0:1] * dp2).astype(BF16)

    blk = pl.BlockSpec((1, tm, GU), lambda g, i: (g, i, 0))
    return _call(
        body, name="conv_transpose", grid=(NDEV, nt),
        in_specs=[blk, pl.BlockSpec((1, 8, GU), lambda g, i: (g, jnp.minimum((i + 1) * (tm // 8), s // 8 - 1), 0)),
                  pl.BlockSpec((1, 3, GU), lambda g, i: (g, 0, 0))],
        out_specs=blk,
        out_shape=_sds((NDEV, s, GU), BF16),
        compiler_params=_params(2),
    )(dy, dy, conv_w)


def _wgrad(a3, b3, *, out_two_d, tk, name):
    ga, s, ka = a3.shape
    gb, _, nb = b3.shape
    groups = max(ga, gb)
    nk = s // tk

    def body(a_ref, b_ref, o_ref, acc):
        k = pl.program_id(1)

        @pl.when(k == 0)
        def _():
            acc[...] = jnp.zeros_like(acc)

        acc[...] += _dot_tn(a_ref[0], b_ref[0])

        @pl.when(k == nk - 1)
        def _():
            if out_two_d:
                o_ref[...] = acc[...].astype(BF16)
            else:
                o_ref[0] = acc[...].astype(BF16)

    a_spec = pl.BlockSpec((1, tk, ka), (lambda g, k: (g, k, 0)) if ga > 1 else (lambda g, k: (0, k, 0)))
    b_spec = pl.BlockSpec((1, tk, nb), (lambda g, k: (g, k, 0)) if gb > 1 else (lambda g, k: (0, k, 0)))
    if out_two_d:
        o_spec, o_shape = pl.BlockSpec((ka, nb), lambda g, k: (0, g)), _sds((ka, groups * nb), BF16)
    else:
        o_spec, o_shape = pl.BlockSpec((1, ka, nb), lambda g, k: (g, 0, 0)), _sds((groups, ka, nb), BF16)
    return _call(
        body, name=name, grid=(groups, nk),
        in_specs=[a_spec, b_spec], out_specs=o_spec, out_shape=o_shape,
        scratch_shapes=[pltpu.VMEM((ka, nb), F32)],
        compiler_params=_params(2),
    )(a3, b3)


def _dgrad_norm_bwd(d3, w, x_in, scale, dx_up, *, two_d, tm, name):
    groups, s, n = d3.shape

    def body(d_ref, w_ref, x_ref, sc_ref, up_ref, dx_ref, dsc_ref, dsh_ref, acc):
        i, g = pl.program_id(0), pl.program_id(1)

        @pl.when((i == 0) & (g == 0))
        def _():
            dsc_ref[...] = jnp.zeros_like(dsc_ref)
            dsh_ref[...] = jnp.zeros_like(dsh_ref)

        part = _dot_nt(d_ref[0], w_ref[...] if two_d else w_ref[0])

        @pl.when(g == 0)
        def _():
            acc[...] = part

        @pl.when(g > 0)
        def _():
            acc[...] += part

        @pl.when(g == groups - 1)
        def _():
            dh = acc[...]
            xv = x_ref[...]
            r = _rms(xv)
            nrm = xv * r
            dsh_ref[...] += _colsum(dh)
            dsc_ref[...] += _colsum(dh * nrm)
            dx_ref[...] = up_ref[...] + _rms_bwd(dh * (1.0 + sc_ref[...]), nrm, r)

    row = pl.BlockSpec((tm, D), lambda i, g: (i, 0))
    vec = pl.BlockSpec((1, D), lambda i, g: (0, 0))
    w_spec = pl.BlockSpec((D, n), lambda i, g: (0, g)) if two_d else pl.BlockSpec((1, D, n), lambda i, g: (g, 0, 0))
    return _call(
        body, name=name, grid=(s // tm, groups),
        in_specs=[pl.BlockSpec((1, tm, n), lambda i, g: (g, i, 0)), w_spec, row, vec, row],
        out_specs=[row, vec, vec],
        out_shape=[_sds((s, D), F32), _sds((1, D), F32), _sds((1, D), F32)],
        scratch_shapes=[pltpu.VMEM((tm, D), F32)],
        compiler_params=_params(2),
    )(d3, w, x_in, scale, dx_up)


def _mix_out_bwd(dx2, mixed, gate, w_out, oa, ob, g, tm):
    s = dx2.shape[0]

    def body(dx_ref, mixed_ref, gate_ref, w_ref, oa_ref, ob_ref, g_ref, dgate_ref, dm_ref, doa_ref, dob_ref, dg_ref):
        @pl.when(pl.program_id(0) == 0)
        def _():
            dgate_ref[...] = jnp.zeros_like(dgate_ref)
            dg_ref[...] = jnp.zeros_like(dg_ref)
        dxv = dx_ref[...]
        dgate_ref[...] += _colsum(dxv * mixed_ref[...])
        dm_ref[...] = (dxv * gate_ref[...]).astype(BF16)
        dnab = _dot_nt(dm_ref[...], w_ref[...])
        for o_ref, do_ref, sl in ((oa_ref, doa_ref, slice(0, WG)), (ob_ref, dob_ref, slice(WG, D))):
            ov = o_ref[...]
            r = _rms(ov)
            nrm = ov * r
            dn = dnab[:, sl]
            dg_ref[:, sl] += _colsum(dn * nrm)
            do_ref[...] = _rms_bwd(dn * g_ref[:, sl], nrm, r)

    row = pl.BlockSpec((tm, D), lambda i: (i, 0))
    half = pl.BlockSpec((tm, WG), lambda i: (i, 0))
    vec = pl.BlockSpec((1, D), lambda i: (0, 0))
    return _call(
        body, name="mix_out_bwd", grid=(s // tm,),
        in_specs=[row, row, vec, pl.BlockSpec((D, D), lambda i: (0, 0)), half, half, vec],
        out_specs=[vec, row, half, half, vec],
        out_shape=[_sds((1, D), F32), _sds((s, D), BF16), _sds((s, WG), F32), _sds((s, WG), F32), _sds((1, D), F32)],
        compiler_params=_params(1),
    )(dx2, mixed, gate, w_out, oa, ob, g)


def _attn_a_bwd(proj, bias_tab, d_oa):
    s = proj.shape[1]
    nq = s // TQA

    def body(q_ref, k_ref, v_ref, b_ref, do_ref, dq_ref, dk_ref, dv_ref, db_ref, dk_acc, dv_acc):
        qi = pl.program_id(1)

        @pl.when(qi == 0)
        def _():
            dk_acc[...] = jnp.zeros_like(dk_acc)
            dv_acc[...] = jnp.zeros_like(dv_acc)
            db_ref[...] = jnp.zeros_like(db_ref)

        def tile(nk, off, kstart):
            for h in range(2):
                sl = slice(HD * h, HD * (h + 1))
                q = q_ref[0, :, sl]
                k = k_ref[0, pl.ds(kstart, nk), sl]
                v = v_ref[0, pl.ds(kstart, nk), sl]
                do = do_ref[:, sl].astype(BF16)
                sc = _dot_nt(q, k) * 0.125 + b_ref[h, :, off:off + nk]
                p = jnp.exp(sc - jnp.max(sc, axis=-1, keepdims=True))
                p = p / jnp.sum(p, axis=-1, keepdims=True)
                dp = _dot_nt(do, v)
                ds = p * (dp - jnp.sum(dp * p, axis=-1, keepdims=True))
                db_ref[h, :, off:off + nk] += ds
                dsb = (ds * 0.125).astype(BF16)
                dq_ref[:, sl] = _dot(dsb, k).astype(BF16)
                dk_acc[pl.ds(kstart, nk), sl] += _dot_tn(dsb, q)
                dv_acc[pl.ds(kstart, nk), sl] += _dot_tn(p.astype(BF16), do)

        _attn_a_cases(qi, tile)

        @pl.when(qi == nq - 1)
        def _():
            dk_ref[...] = dk_acc[...].astype(BF16)
            dv_ref[...] = dv_acc[...].astype(BF16)

    q_spec, k_spec, v_spec, b_spec = _attn_a_specs(s)
    blk = pl.BlockSpec((TQA, 128), lambda hp, qi: (qi, hp))
    col = pl.BlockSpec((s, 128), lambda hp, qi: (0, hp))
    return _call(
        body, name="attn_a_bwd", grid=(4, nq),
        in_specs=[q_spec, k_spec, v_spec, b_spec, blk],
        out_specs=[blk, col, col, b_spec],
        out_shape=[_sds((s, WG), BF16), _sds((s, WG), BF16), _sds((s, WG), BF16), _sds((NDEV, TQA, KWA), F32)],
        scratch_shapes=[pltpu.VMEM((s, 128), F32), pltpu.VMEM((s, 128), F32)],
        compiler_params=_params(2),
    )(proj, proj, proj, bias_tab, d_oa)


def _attn_b_bwd(proj, lsum, d_ob):
    s = proj.shape[1]
    nq = s // TQB
    ndiag = TQB // TKB

    def body(q_ref, k_ref, v_ref, ls_ref, do_ref, dq_ref, dk_ref, dv_ref, dk_acc, dv_acc):
        qi = pl.program_id(1)
        q0 = qi * TQB

        @pl.when(qi == 0)
        def _():
            dk_acc[...] = jnp.zeros_like(dk_acc)
            dv_acc[...] = jnp.zeros_like(dv_acc)

        rows = lax.broadcasted_iota(jnp.int32, (TQB, TKB), 0)
        cols = lax.broadcasted_iota(jnp.int32, (TQB, TKB), 1)
        uj = lax.broadcasted_iota(jnp.int32, (TKB, TKB), 0)
        us = lax.broadcasted_iota(jnp.int32, (TKB, TKB), 1)
        prefix = jnp.where(uj <= us, 1.0, 0.0).astype(BF16)
        for h in range(2):
            sl = slice(HD * h, HD * (h + 1))
            q = q_ref[0, :, sl]
            do = do_ref[:, sl].astype(BF16)
            ltot = ls_ref[0, :, HD * h:HD * h + 1]

            def tile(kstart, diag, cl, cg, dq):
                k = k_ref[0, pl.ds(kstart, TKB), sl]
                v = v_ref[0, pl.ds(kstart, TKB), sl]
                lb, lk = _sb_logits(q, k)
                if diag is not None:
                    strict = rows > cols + diag * TKB
                    lk = jnp.where(strict, lk, 0.0)
                pre = _dot2(lk, prefix) + cl
                a = jnp.exp(lb + ltot - pre)
                if diag is not None:
                    a = jnp.where(strict, a, 0.0)
                gz = _dot_nt(do, v) * a
                pg = _dot2(gz, prefix) + cg
                sig = jnp.exp(lb)
                dl = gz * (1.0 - sig) - (pg - gz) * sig
                if diag is not None:
                    dl = jnp.where(strict, dl, 0.0)
                dlb = (dl * 0.125).astype(BF16)
                dk_acc[pl.ds(kstart, TKB), sl] += _dot_tn(dlb, q)
                dv_acc[pl.ds(kstart, TKB), sl] += _dot_tn(a.astype(BF16), do)
                return pre[:, TKB - 1:TKB], pg[:, TKB - 1:TKB], dq + _dot(dlb, k)

            state = (jnp.zeros((TQB, 1), F32), jnp.zeros((TQB, 1), F32), jnp.zeros((TQB, HD), F32))

            def step(i, st):
                return tile(pl.multiple_of(i * TKB, TKB), None, *st)

            state = lax.fori_loop(0, q0 // TKB, step, state)
            for d in range(ndiag):
                state = tile(pl.multiple_of(q0 + d * TKB, TKB), d, *state)
            dq_ref[:, sl] = state[2].astype(BF16)

        @pl.when(qi == nq - 1)
        def _():
            dk_ref[...] = dk_acc[...].astype(BF16)
            dv_ref[...] = dv_acc[...].astype(BF16)

    q_spec, k_spec, v_spec = _attn_b_specs(s)
    blk = pl.BlockSpec((TQB, 128), lambda hp, qi: (qi, hp))
    col = pl.BlockSpec((s, 128), lambda hp, qi: (0, hp))
    return _call(
        body, name="attn_b_bwd", grid=(4, nq),
        in_specs=[q_spec, k_spec, v_spec, pl.BlockSpec((1, TQB, 128), lambda hp, qi: (hp, qi, 0)), blk],
        out_specs=[blk, col, col],
        out_shape=[_sds((s, WG), BF16)] * 3,
        scratch_shapes=[pltpu.VMEM((s, 128), F32), pltpu.VMEM((s, 128), F32)],
        compiler_params=_params(2),
    )(proj, proj, proj, lsum, d_ob)


def _bias_fold(dtab):
    nbin = 384
    rb = 8

    def body(t_ref, o_ref):
        @pl.when(pl.program_id(0) == 0)
        def _():
            o_ref[...] = jnp.zeros_like(o_ref)
        kk = lax.broadcasted_iota(jnp.int32, (KWA, nbin), 0)
        bins = lax.broadcasted_iota(jnp.int32, (KWA, nbin), 1)
        acc = jnp.zeros((NDEV, nbin), F32)
        for j in range(rb):
            r = pl.program_id(0) * rb + j
            hit = (jnp.clip(NPREV * CHUNK + r - kk, -REL_CLIP, REL_CLIP) + REL_CLIP) == bins
            onehot = jnp.where(hit, 1.0, 0.0).astype(BF16)
            row = t_ref[:, j, :]
            hi = row.astype(BF16)
            mid = (row - hi.astype(F32)).astype(BF16)
            lo = (row - hi.astype(F32) - mid.astype(F32)).astype(BF16)
            acc = acc + _dot(hi, onehot) + _dot(mid, onehot) + _dot(lo, onehot)
        o_ref[...] += acc

    return _call(
        body, name="bias_fold", grid=(TQA // rb,),
        in_specs=[pl.BlockSpec((NDEV, rb, KWA), lambda i: (0, i, 0))],
        out_specs=pl.BlockSpec((NDEV, nbin), lambda i: (0, 0)),
        out_shape=_sds((NDEV, nbin), F32),
        compiler_params=_params(1),
    )(dtab)


def _adamw(w, g, m, v):
    m = B1 * m + (1.0 - B1) * g
    v = B2 * v + (1.0 - B2) * (g * g)
    m_hat = m / (1.0 - B1 ** STEP)
    v_hat = v / (1.0 - B2 ** STEP)
    delta = -LR * (m_hat / (jnp.sqrt(v_hat) + AEPS) + WD * w)
    return delta, m, v


def _adamw_big(recv, w, m, v):
    _, _, rows, cols = recv.shape
    tr = rows if rows <= 512 else 256

    def body(r_ref, w_ref, m_ref, v_ref, g_ref, d_ref, nm_ref, nv_ref):
        g = r_ref[0, 0].astype(F32)
        for p in range(1, NDEV):
            g = g + r_ref[p, 0].astype(F32)
        delta, nm, nv = _adamw(w_ref[0], g, m_ref[0], v_ref[0])
        g_ref[0], d_ref[0], nm_ref[0], nv_ref[0] = g, delta, nm, nv

    blk = pl.BlockSpec((1, tr, cols), lambda l, i: (l, i, 0))
    return _call(
        body, name="adamw_big", grid=(2, rows // tr),
        in_specs=[pl.BlockSpec((NDEV, 1, tr, cols), lambda l, i: (0, l, i, 0)), blk, blk, blk],
        out_specs=[blk] * 4,
        out_shape=[_sds((2, rows, cols), F32)] * 4,
        compiler_params=_params(2),
    )(recv, w, m, v)


def _adamw_w_ada(cact_t, dmod, w, m, v):
    tr = 256

    def body(c_ref, dm_ref, w_ref, m_ref, v_ref, g_ref, d_ref, nm_ref, nv_ref):
        g = c_ref[:, 0:1] * dm_ref[0, 0:1, :]
        for b in range(1, NDEV):
            g = g + c_ref[:, b:b + 1] * dm_ref[0, b:b + 1, :]
        delta, nm, nv = _adamw(w_ref[0], g, m_ref[0], v_ref[0])
        g_ref[0], d_ref[0], nm_ref[0], nv_ref[0] = g, delta, nm, nv

    blk = pl.BlockSpec((1, tr, 768), lambda l, i: (l, i, 0))
    return _call(
        body, name="adamw_w_ada", grid=(2, D // tr),
        in_specs=[pl.BlockSpec((tr, NDEV), lambda l, i: (i, 0)), pl.BlockSpec((1, NDEV, 768), lambda l, i: (l, 0, 0)),
                  blk, blk, blk],
        out_specs=[blk] * 4,
        out_shape=[_sds((2, D, 768), F32)] * 4,
        compiler_params=_params(2),
    )(cact_t, dmod, w, m, v)


def _adamw_small(gath, w, m, v):
    rows = gath.shape[1]

    def body(r_ref, w_ref, m_ref, v_ref, g_ref, d_ref, nm_ref, nv_ref):
        g = r_ref[0]
        for p in range(1, NDEV):
            g = g + r_ref[p]
        delta, nm, nv = _adamw(w_ref[...], g, m_ref[...], v_ref[...])
        g_ref[...], d_ref[...], nm_ref[...], nv_ref[...] = g, delta, nm, nv

    blk = pl.BlockSpec((rows, D), lambda i: (0, 0))
    return _call(
        body, name="adamw_small", grid=(1,),
        in_specs=[pl.BlockSpec((NDEV, rows, D), lambda i: (0, 0, 0)), blk, blk, blk],
        out_specs=[blk] * 4,
        out_shape=[_sds((rows, D), F32)] * 4,
        compiler_params=_params(1),
    )(gath, w, m, v)


_PACK = (("b_ada", 2 * 6 * D), ("rel_bias", 2 * 8 * 257), ("g_a", 2 * WG), ("g_b", 2 * WG),
         ("conv_b", 2 * 2 * DFF), ("final_g", D), ("conv_w", 2 * NDEV * 3 * GU))


def _pack(parts):
    rows = []
    for name, size in _PACK:
        flat = parts[name].reshape(-1).astype(F32)
        assert flat.shape[0] == size, (name, flat.shape)
        rows.append(jnp.pad(flat, (0, -size % D)))
    out = jnp.concatenate(rows).reshape(-1, D)
    return jnp.pad(out, ((0, -out.shape[0] % 8), (0, 0)))


def _unpack(packed):
    flat = packed.reshape(-1)
    out, pos = {}, 0
    for name, size in _PACK:
        out[name] = flat[pos:pos + size]
        pos += size + (-size % D)
    return out


def kernel(x, c, w_ada, b_ada, w_in, rel_bias, g_a, g_b, w_out, w_up, conv_w, conv_b, w_down, final_g, loss_target, m_w_ada, m_b_ada, m_w_in, m_rel_bias, m_g_a, m_g_b, m_w_out, m_w_up, m_conv_w, m_conv_b, m_w_down, m_final_g, v_w_ada, v_b_ada, v_w_in, v_rel_bias, v_g_a, v_g_b, v_w_out, v_w_up, v_conv_w, v_conv_b, v_w_down, v_final_g):
    s = x.shape[1]
    assert s % TQA == 0 and s >= KWA and s % 512 == 0
    tm = 512
    me = 4 * lax.axis_index("x") + 2 * lax.axis_index("y") + lax.axis_index("c")
    xs = x.reshape(s, D)
    target = loss_target.reshape(s, D)

    full = _weights_allgather([w_in.astype(BF16), w_out.astype(BF16), w_up.astype(BF16), w_down.astype(BF16)])

    c_all = _small_allgather(jnp.pad(c, ((0, 7), (0, 0))), "gather_c")[:, 0, :]
    b_sl = lax.dynamic_slice(b_ada, (0, me * 768), (2, 768)).reshape(2, 1, 768)
    mod_part, cact = _mod_fwd(c_all, w_ada, b_sl)
    mod_all = _small_allgather(mod_part.reshape(2 * NDEV, 768), "gather_mod")
    mod_all = mod_all.reshape(NDEV, 2, NDEV, 768)
    mod_me = lax.dynamic_index_in_dim(mod_all, me, axis=2, keepdims=False)
    mod = jnp.transpose(mod_me, (1, 0, 2)).reshape(2, 6, 1, D)

    cw_all = _small_allgather(jnp.pad(conv_w.reshape(2 * 3, GU), ((0, 2), (0, 0))), "gather_conv_w")
    cw_all = cw_all[:, :6, :].reshape(NDEV, 2, 3, GU)

    saved = []
    xl = xs
    for l in range(2):
        sh_mix, sc_mix, gt_mix, sh_ffn, sc_ffn, gt_ffn = (mod[l, j] for j in range(6))
        cw = cw_all[:, l].reshape(2, 4, 3, GU)
        cb = conv_b[l].reshape(2, 4, 1, GU)
        gvec = jnp.concatenate([g_a[l], g_b[l]]).reshape(1, D)
        tab = _bias_table(rel_bias[l])
        wd4 = full["down"][l].reshape(4, GU, D)

        h1, proj = _nm_matmul(xl, sh_mix, sc_mix, full["in"][l], two_d=True, n=WG, groups=NT, out_dtype=BF16, tm=tm,
                              name="norm_proj")
        oa = _attn_a_fwd(proj, tab)
        ob, lsum = _attn_b_fwd(proj)
        nab, mixed, x2 = _mix_out(oa, ob, gvec, full["out"][l], xl, gt_mix, tm)
        h2, u = _nm_matmul(x2, sh_ffn, sc_ffn, full["up"][l], two_d=False, n=GU, groups=NDEV, out_dtype=F32, tm=tm,
                           name="norm_up")
        u = u.reshape(2, 4, s, GU)
        a = _conv_act(u, cw, cb, tm)
        ffn, x3 = _down(a, wd4, x2, gt_ffn, tm)
        saved.append(dict(x=xl, h1=h1, proj=proj, oa=oa, ob=ob, lsum=lsum, nab=nab, mixed=mixed, x2=x2, h2=h2, u=u,
                          a=a, ffn=ffn, cw=cw, cb=cb, gvec=gvec, tab=tab, wd4=wd4))
        xl = x3

    loss_part, dx, d_final_g = _final_loss(xl, final_g.reshape(1, D), target, tm)
    loss = lax.psum(loss_part[0, 0], ("x", "y", "c"))

    grads = {kind: [None, None] for kind in _KINDS}
    small = {"b_ada": [None, None], "rel_bias": [None, None], "g_a": [None, None], "g_b": [None, None],
             "conv_b": [None, None], "conv_w": [None, None]}
    for l in (1, 0):
        sv = saved[l]
        sh_mix, sc_mix, gt_mix, sh_ffn, sc_ffn, gt_ffn = (mod[l, j] for j in range(6))
        d_gt_ffn, dff, da = _down_bwd(dx, gt_ffn, sv["ffn"], sv["wd4"], tm)
        grads["down"][l] = _wgrad(sv["a"], dff.reshape(1, s, D), out_two_d=False, tk=tm, name="wgrad_down").reshape(DFF, D)
        dy, d_cw, d_cb = _conv_act_bwd(sv["u"], sv["cw"], sv["cb"], da, tm)
        du = _conv_transpose(dy.reshape(NDEV, s, GU), sv["cw"].reshape(NDEV, 3, GU), tm)
        grads["up"][l] = _wgrad(sv["h2"].reshape(1, s, D), du, out_two_d=False, tk=tm, name="wgrad_up")
        dx2, d_sc_ffn, d_sh_ffn = _dgrad_norm_bwd(du, full["up"][l], sv["x2"], sc_ffn, dx, two_d=False, tm=tm,
                                                  name="dgrad_up")
        d_gt_mix, dmixed, d_oa, d_ob, d_g = _mix_out_bwd(dx2, sv["mixed"], gt_mix, full["out"][l], sv["oa"], sv["ob"],
                                                         sv["gvec"], tm)
        grads["out"][l] = _wgrad(sv["nab"].reshape(1, s, D), dmixed.reshape(1, s, D), out_two_d=False, tk=tm,
                                 name="wgrad_out").reshape(D, D)
        dqa, dka, dva, d_tab = _attn_a_bwd(sv["proj"], sv["tab"], d_oa)
        dqb, dkb, dvb = _attn_b_bwd(sv["proj"], sv["lsum"], d_ob)
        dproj = jnp.stack([dqa, dka, dva, dqb, dkb, dvb])
        grads["in"][l] = _wgrad(sv["h1"].reshape(1, s, D), dproj, out_two_d=True, tk=tm, name="wgrad_in")
        dx, d_sc_mix, d_sh_mix = _dgrad_norm_bwd(dproj, full["in"][l], sv["x"], sc_mix, dx2, two_d=True, tm=tm,
                                                 name="dgrad_in")
        small["b_ada"][l] = jnp.concatenate([d_sh_mix, d_sc_mix, d_gt_mix, d_sh_ffn, d_sc_ffn, d_gt_ffn], axis=1)
        small["rel_bias"][l] = _bias_fold(d_tab)[:, :2 * REL_CLIP + 1]
        small["g_a"][l], small["g_b"][l] = d_g[:, :WG], d_g[:, WG:]
        small["conv_b"][l] = d_cb
        small["conv_w"][l] = d_cw.reshape(NDEV, 3, GU)
    grad_x = dx.reshape(1, s, D)

    recv = _grads_exchange(grads)
    big = {}
    for kind, (w, m, v) in (("in", (w_in, m_w_in, v_w_in)), ("out", (w_out, m_w_out, v_w_out)),
                            ("up", (w_up, m_w_up, v_w_up)), ("down", (w_down, m_w_down, v_w_down))):
        big[kind] = _adamw_big(recv[kind], w, m, v)

    contrib = {k: jnp.stack(vs) for k, vs in small.items()}
    contrib["final_g"] = d_final_g
    gath = _small_allgather(_pack(contrib), "gather_small_grads")

    def place_conv_w(t):
        return lax.dynamic_update_slice(jnp.zeros((2, NDEV, 3, GU), F32), t.reshape(2, 1, 3, GU), (0, me, 0, 0))

    def packed_params(b, rb, ga, gb, cb_, fg, cw_):
        return _pack({"b_ada": b, "rel_bias": rb, "g_a": ga, "g_b": gb, "conv_b": cb_, "final_g": fg,
                      "conv_w": place_conv_w(cw_)})

    sm = _adamw_small(gath,
                      packed_params(b_ada, rel_bias, g_a, g_b, conv_b, final_g, conv_w),
                      packed_params(m_b_ada, m_rel_bias, m_g_a, m_g_b, m_conv_b, m_final_g, m_conv_w),
                      packed_params(v_b_ada, v_rel_bias, v_g_a, v_g_b, v_conv_b, v_final_g, v_conv_w))
    sm = [_unpack(t) for t in sm]

    dmod_all = gath[:, :12, :].reshape(NDEV, 2, 6 * D)
    dmod_sl = jnp.transpose(lax.dynamic_slice(dmod_all, (0, 0, me * 768), (NDEV, 2, 768)), (1, 0, 2))
    ada = _adamw_w_ada(cact.T, dmod_sl, w_ada, m_w_ada, v_w_ada)

    def small_out(j, name):
        t = sm[j][name]
        if name == "b_ada":
            return t.reshape(2, 6 * D)
        if name == "rel_bias":
            return t.reshape(2, 8, 257)
        if name in ("g_a", "g_b"):
            return t.reshape(2, WG)
        if name == "conv_b":
            return t.reshape(2, 2 * DFF)
        if name == "final_g":
            return t.reshape(D)
        t = t.reshape(2, NDEV, 3, GU)
        return lax.dynamic_index_in_dim(t, me, axis=1, keepdims=False)

    def group(j):
        return (ada[j], small_out(j, "b_ada"), big["in"][j], small_out(j, "rel_bias"), small_out(j, "g_a"),
                small_out(j, "g_b"), big["out"][j], big["up"][j], small_out(j, "conv_w"), small_out(j, "conv_b"),
                big["down"][j], small_out(j, "final_g"))

    return (loss, grad_x, *group(0), *group(1), *group(2), *group(3))
```

```python
import jax
import jax.numpy as jnp
from jax import lax
from jax.experimental import pallas as pl
from jax.experimental.pallas import tpu as pltpu

F32, BF16 = jnp.float32, jnp.bfloat16
MESH_ID = pl.DeviceIdType.MESH
NDEV = 8
D = 1024
HD = 64
WG = 512
NT = 6
GU = 704
DFF = 2816
CHUNK, NPREV, REL_CLIP = 64, 8, 128
BAND = (NPREV + 1) * CHUNK
EPS = 1e-6
NEG = -1e30
TQA = 256
KWA = TQA + NPREV * CHUNK
TABW = 1024
TAB0 = TABW - KWA
N_FAR = TAB0 + NPREV * CHUNK - REL_CLIP + 1
N_NEAR = REL_CLIP + CHUNK - 1
TQB, TKB = 512, 128
KSTEP = 4
LR, B1, B2, AEPS, WD, STEP = 0.001, 0.9, 0.999, 1e-08, 0.01, 10
VMEM_MB = 56


def _call(body, **kw):
    return pl.pallas_call(body, **kw)


def _params(n_axes):
    return pltpu.CompilerParams(dimension_semantics=("arbitrary",) * n_axes, vmem_limit_bytes=VMEM_MB << 20)


def _dot(a, b):
    return jnp.dot(a, b, preferred_element_type=F32)


def _dot_nt(a, b):
    return lax.dot_general(a, b, (((1,), (1,)), ((), ())), preferred_element_type=F32)


def _dot_tn(a, b):
    return lax.dot_general(a, b, (((0,), (0,)), ((), ())), preferred_element_type=F32)


def _dot2(x, u):
    hi = x.astype(BF16)
    lo = (x - hi.astype(F32)).astype(BF16)
    return _dot(hi, u) + _dot(lo, u)


def _rms(x):
    return lax.rsqrt(jnp.mean(x * x, axis=-1, keepdims=True) + EPS)


def _rms_bwd(dn, n, r):
    return r * (dn - n * jnp.mean(dn * n, axis=-1, keepdims=True))


def _colsum(x):
    return jnp.sum(x, axis=0, keepdims=True)


def _sigmoid(x):
    return 1.0 / (1.0 + jnp.exp(-x))


def _sds(shape, dtype):
    return jax.ShapeDtypeStruct(shape, dtype)


def _place():
    x, y, c = lax.axis_index("x"), lax.axis_index("y"), lax.axis_index("c")
    return x, y, c, 4 * x + 2 * y + c


def _peer(x, y, c, k):
    px = 1 - x if k & 4 else x
    py = 1 - y if k & 2 else y
    pc = 1 - c if k & 1 else c
    return (px, py, pc), 4 * px + 2 * py + pc


def _small_allgather(v, name):
    rows, cols = v.shape

    def body(v_ref, out_ref, send_sems, recv_sems, local_sem):
        x, y, c, me = _place()
        mine = pltpu.make_async_copy(v_ref, out_ref.at[me], local_sem)
        mine.start()
        sends = []
        for k in range(1, NDEV):
            peer, _ = _peer(x, y, c, k)
            cp = pltpu.make_async_remote_copy(v_ref, out_ref.at[me], send_sems.at[k - 1], recv_sems.at[k - 1],
                                              device_id=peer, device_id_type=MESH_ID)
            cp.start()
            sends.append(cp)
        for k in range(1, NDEV):
            peer, pidx = _peer(x, y, c, k)
            pltpu.make_async_remote_copy(v_ref, out_ref.at[pidx], send_sems.at[k - 1], recv_sems.at[k - 1],
                                         device_id=peer, device_id_type=MESH_ID).wait_recv()
        for cp in sends:
            cp.wait_send()
        mine.wait()

    return _call(
        body, name=name,
        out_shape=_sds((NDEV, rows, cols), F32),
        in_specs=[pl.BlockSpec(memory_space=pltpu.VMEM)],
        out_specs=pl.BlockSpec(memory_space=pltpu.VMEM),
        scratch_shapes=[pltpu.SemaphoreType.DMA((NDEV - 1,)), pltpu.SemaphoreType.DMA((NDEV - 1,)),
                        pltpu.SemaphoreType.DMA],
    )(v)


def _shard_view(ref, kind, p):
    if kind == "in":
        return ref.at[:, pl.ds(pl.multiple_of(p * 384, 128), 384)]
    if kind == "out":
        return ref.at[pl.ds(pl.multiple_of(p * 128, 128), 128), :]
    if kind == "up":
        return ref.at[p]
    if kind == "down":
        return ref.at[pl.ds(pl.multiple_of(p * 352, 16), 352), :]
    raise ValueError(kind)


_KINDS = ("in", "out", "up", "down")
_FULL_SHAPES = {"in": (D, 3 * D), "out": (D, D), "up": (NDEV, D, GU), "down": (DFF, D)}
_SHARD_SHAPES = {"in": (D, 384), "out": (128, D), "up": (D, GU), "down": (352, D)}


def _weights_allgather(shards):
    n = len(_KINDS) * 2

    def body(*refs):
        ins, outs = refs[:4], refs[4:4 + n]
        send_sems, recv_sems, local_sems = refs[4 + n:]
        x, y, c, me = _place()
        started = []
        for t, kind in enumerate(_KINDS):
            for l in range(2):
                j = 2 * t + l
                src = ins[t].at[l]
                mine = pltpu.make_async_copy(src, _shard_view(outs[j], kind, me), local_sems.at[j])
                mine.start()
                started.append(mine)
        sends = []
        for k in range(1, NDEV):
            peer, _ = _peer(x, y, c, k)
            for t, kind in enumerate(_KINDS):
                for l in range(2):
                    j = 2 * t + l
                    cp = pltpu.make_async_remote_copy(ins[t].at[l], _shard_view(outs[j], kind, me),
                                                      send_sems.at[j, k - 1], recv_sems.at[j, k - 1],
                                                      device_id=peer, device_id_type=MESH_ID)
                    cp.start()
                    sends.append(cp)
        for k in range(1, NDEV):
            peer, pidx = _peer(x, y, c, k)
            for t, kind in enumerate(_KINDS):
                for l in range(2):
                    j = 2 * t + l
                    pltpu.make_async_remote_copy(ins[t].at[l], _shard_view(outs[j], kind, pidx),
                                                 send_sems.at[j, k - 1], recv_sems.at[j, k - 1],
                                                 device_id=peer, device_id_type=MESH_ID).wait_recv()
        for cp in sends:
            cp.wait_send()
        for cp in started:
            cp.wait()

    out_shape = [_sds(_FULL_SHAPES[kind], BF16) for kind in _KINDS for _ in range(2)]
    any_spec = pl.BlockSpec(memory_space=pl.ANY)
    outs = _call(
        body, name="weights_allgather",
        out_shape=out_shape,
        in_specs=[any_spec] * 4, out_specs=[any_spec] * n,
        scratch_shapes=[pltpu.SemaphoreType.DMA((n, NDEV - 1)), pltpu.SemaphoreType.DMA((n, NDEV - 1)),
                        pltpu.SemaphoreType.DMA((n,))],
    )(*shards)
    return {kind: (outs[2 * t], outs[2 * t + 1]) for t, kind in enumerate(_KINDS)}


def _grads_exchange(grads):
    n = len(_KINDS) * 2

    def body(*refs):
        ins, outs = refs[:n], refs[n:n + 4]
        send_sems, recv_sems, local_sems = refs[n + 4:]
        x, y, c, me = _place()
        started = []
        for t, kind in enumerate(_KINDS):
            for l in range(2):
                j = 2 * t + l
                mine = pltpu.make_async_copy(_shard_view(ins[j], kind, me), outs[t].at[me, l], local_sems.at[j])
                mine.start()
                started.append(mine)
        sends = []
        for k in range(1, NDEV):
            peer, pidx = _peer(x, y, c, k)
            for t, kind in enumerate(_KINDS):
                for l in range(2):
                    j = 2 * t + l
                    cp = pltpu.make_async_remote_copy(_shard_view(ins[j], kind, pidx), outs[t].at[me, l],
                                                      send_sems.at[j, k - 1], recv_sems.at[j, k - 1],
                                                      device_id=peer, device_id_type=MESH_ID)
                    cp.start()
                    sends.append(cp)
        for k in range(1, NDEV):
            peer, pidx = _peer(x, y, c, k)
            for t, kind in enumerate(_KINDS):
                for l in range(2):
                    j = 2 * t + l
                    pltpu.make_async_remote_copy(_shard_view(ins[j], kind, me), outs[t].at[pidx, l],
                                                 send_sems.at[j, k - 1], recv_sems.at[j, k - 1],
                                                 device_id=peer, device_id_type=MESH_ID).wait_recv()
        for cp in sends:
            cp.wait_send()
        for cp in started:
            cp.wait()

    out_shape = [_sds((NDEV, 2) + _SHARD_SHAPES[kind], BF16) for kind in _KINDS]
    any_spec = pl.BlockSpec(memory_space=pl.ANY)
    flat = [grads[kind][l] for kind in _KINDS for l in range(2)]
    outs = _call(
        body, name="grads_exchange",
        out_shape=out_shape,
        in_specs=[any_spec] * n, out_specs=[any_spec] * 4,
        scratch_shapes=[pltpu.SemaphoreType.DMA((n, NDEV - 1)), pltpu.SemaphoreType.DMA((n, NDEV - 1)),
                        pltpu.SemaphoreType.DMA((n,))],
    )(*flat)
    return dict(zip(_KINDS, outs))


def _mod_fwd(c_all, w_ada, b_sl):
    def body(c_ref, w_ref, b_ref, mod_ref, cact_ref):
        cv = c_ref[...]
        ca = cv * _sigmoid(cv)
        cact_ref[...] = ca
        mod_ref[0] = _dot(ca.astype(BF16), w_ref[0].astype(BF16)) + b_ref[0]

    return _call(
        body, name="mod_fwd", grid=(2,),
        in_specs=[pl.BlockSpec((NDEV, D), lambda l: (0, 0)), pl.BlockSpec((1, D, 768), lambda l: (l, 0, 0)),
                  pl.BlockSpec((1, 1, 768), lambda l: (l, 0, 0))],
        out_specs=[pl.BlockSpec((1, NDEV, 768), lambda l: (l, 0, 0)), pl.BlockSpec((NDEV, D), lambda l: (0, 0))],
        out_shape=[_sds((2, NDEV, 768), F32), _sds((NDEV, D), F32)],
        compiler_params=_params(1),
    )(c_all, w_ada, b_sl)


def _nm_matmul(x, shift, scale, w, *, two_d, n, groups, out_dtype, tm, name):
    s = x.shape[0]

    def body(x_ref, sh_ref, sc_ref, w_ref, h_ref, o_ref):
        @pl.when(pl.program_id(1) == 0)
        def _():
            xv = x_ref[...]
            h_ref[...] = ((xv * _rms(xv)) * (1.0 + sc_ref[...]) + sh_ref[...]).astype(BF16)
        wv = w_ref[...] if two_d else w_ref[0]
        o_ref[0] = _dot(h_ref[...], wv).astype(out_dtype)

    vec = pl.BlockSpec((1, D), lambda i, g: (0, 0))
    w_spec = pl.BlockSpec((D, n), lambda i, g: (0, g)) if two_d else pl.BlockSpec((1, D, n), lambda i, g: (g, 0, 0))
    return _call(
        body, name=name, grid=(s // tm, groups),
        in_specs=[pl.BlockSpec((tm, D), lambda i, g: (i, 0)), vec, vec, w_spec],
        out_specs=[pl.BlockSpec((tm, D), lambda i, g: (i, 0)), pl.BlockSpec((1, tm, n), lambda i, g: (g, i, 0))],
        out_shape=[_sds((s, D), BF16), _sds((groups, s, n), out_dtype)],
        compiler_params=_params(2),
    )(x, shift, scale, w)


def _bias_table(rel_bias):
    far = jnp.broadcast_to(rel_bias[:, 2 * REL_CLIP:], (NDEV, N_FAR))
    near = rel_bias[:, 2 * REL_CLIP - 1:REL_CLIP - CHUNK:-1]
    fpad = jnp.concatenate([far, near, jnp.zeros((NDEV, TABW - N_FAR - N_NEAR), F32)], axis=1)

    def body(f_ref, o_ref):
        t = pltpu.roll(jnp.broadcast_to(f_ref[0], (TQA, TABW)), 0, 1, stride=1, stride_axis=0)[:, TAB0:]
        rows = lax.broadcasted_iota(jnp.int32, (TQA, KWA), 0)
        cols = lax.broadcasted_iota(jnp.int32, (TQA, KWA), 1)
        first = jnp.bitwise_and(rows, -CHUNK)
        o_ref[0] = jnp.where((cols >= first) & (cols < first + BAND), t, NEG)

    return _call(
        body, name="bias_table", grid=(NDEV,),
        in_specs=[pl.BlockSpec((1, 1, TABW), lambda h: (h, 0, 0))],
        out_specs=pl.BlockSpec((1, TQA, KWA), lambda h: (h, 0, 0)),
        out_shape=_sds((NDEV, TQA, KWA), F32),
        compiler_params=_params(1),
    )(fpad.reshape(NDEV, 1, TABW))


def _attn_a_cases(qi, tile):
    @pl.when(qi == 0)
    def _():
        tile(TQA, 2 * TQA, 0)

    @pl.when(qi == 1)
    def _():
        tile(2 * TQA, TQA, 0)

    @pl.when(qi >= 2)
    def _():
        tile(KWA, 0, pl.multiple_of((qi - 2) * TQA, TQA))


def _attn_a_specs(s):
    q_spec = pl.BlockSpec((1, TQA, 128), lambda hp, qi: (0, qi, hp))
    k_spec = pl.BlockSpec((1, s, 128), lambda hp, qi: (1, 0, hp))
    v_spec = pl.BlockSpec((1, s, 128), lambda hp, qi: (2, 0, hp))
    b_spec = pl.BlockSpec((2, TQA, KWA), lambda hp, qi: (hp, 0, 0))
    return q_spec, k_spec, v_spec, b_spec


def _attn_a_fwd(proj, bias_tab):
    s = proj.shape[1]

    def body(q_ref, k_ref, v_ref, b_ref, o_ref):
        def tile(nk, off, kstart):
            for h in range(2):
                sl = slice(HD * h, HD * (h + 1))
                q = q_ref[0, :, sl]
                k = k_ref[0, pl.ds(kstart, nk), sl]
                v = v_ref[0, pl.ds(kstart, nk), sl]
                sc = _dot_nt(q, k) * 0.125 + b_ref[h, :, off:off + nk]
                p = jnp.exp(sc - jnp.max(sc, axis=-1, keepdims=True))
                den = jnp.sum(p, axis=-1, keepdims=True)
                o_ref[:, sl] = _dot(p.astype(BF16), v) / den

        _attn_a_cases(pl.program_id(1), tile)

    q_spec, k_spec, v_spec, b_spec = _attn_a_specs(s)
    return _call(
        body, name="attn_a_fwd", grid=(4, s // TQA),
        in_specs=[q_spec, k_spec, v_spec, b_spec],
        out_specs=pl.BlockSpec((TQA, 128), lambda hp, qi: (qi, hp)),
        out_shape=_sds((s, WG), F32),
        compiler_params=_params(2),
    )(proj, proj, proj, bias_tab)


def _sb_logits(q, k):
    lg = _dot_nt(q, k) * 0.125
    sp = jnp.maximum(lg, 0.0) + jnp.log(1.0 + jnp.exp(-jnp.abs(lg)))
    return lg - sp, -sp


def _attn_b_specs(s):
    q_spec = pl.BlockSpec((1, TQB, 128), lambda hp, qi: (3, qi, hp))
    k_spec = pl.BlockSpec((1, s, 128), lambda hp, qi: (4, 0, hp))
    v_spec = pl.BlockSpec((1, s, 128), lambda hp, qi: (5, 0, hp))
    return q_spec, k_spec, v_spec


def _attn_b_fwd(proj):
    s = proj.shape[1]
    ndiag = TQB // TKB

    def body(q_ref, k_ref, v_ref, o_ref, ls_ref):
        q0 = pl.program_id(1) * TQB
        rows = lax.broadcasted_iota(jnp.int32, (TQB, TKB), 0)
        cols = lax.broadcasted_iota(jnp.int32, (TQB, TKB), 1)
        uj = lax.broadcasted_iota(jnp.int32, (TKB, TKB), 0)
        us = lax.broadcasted_iota(jnp.int32, (TKB, TKB), 1)
        suffix = jnp.where(uj >= us, 1.0, 0.0).astype(BF16)
        heads = [slice(HD * h, HD * (h + 1)) for h in range(2)]
        qs = [q_ref[0, :, sl] for sl in heads]

        def tile(h, kstart, diag, carry, acc):
            k = k_ref[0, pl.ds(kstart, TKB), heads[h]]
            v = v_ref[0, pl.ds(kstart, TKB), heads[h]]
            lb, lk = _sb_logits(qs[h], k)
            if diag is not None:
                strict = rows > cols + diag * TKB
                lk = jnp.where(strict, lk, 0.0)
            csum = _dot2(lk, suffix) + carry
            w = jnp.exp(lb + csum - lk)
            if diag is not None:
                w = jnp.where(strict, w, 0.0)
            return csum[:, 0:1], acc + _dot(w.astype(BF16), v)

        state = [jnp.zeros((TQB, 1), F32), jnp.zeros((TQB, HD), F32)] * 2
        for d in range(ndiag - 1, -1, -1):
            for h in range(2):
                state[2 * h:2 * h + 2] = tile(h, pl.multiple_of(q0 + d * TKB, TKB), d, *state[2 * h:2 * h + 2])
        nsteps = q0 // (KSTEP * TKB)

        def step(i, st):
            st = list(st)
            base = (nsteps - 1 - i) * (KSTEP * TKB)
            for sub in range(KSTEP - 1, -1, -1):
                for h in range(2):
                    st[2 * h:2 * h + 2] = tile(h, pl.multiple_of(base + sub * TKB, TKB), None, *st[2 * h:2 * h + 2])
            return tuple(st)

        state = lax.fori_loop(0, nsteps, step, tuple(state))
        for h in range(2):
            o_ref[:, heads[h]] = state[2 * h + 1]
            ls_ref[0, :, heads[h]] = jnp.broadcast_to(state[2 * h], (TQB, HD))

    q_spec, k_spec, v_spec = _attn_b_specs(s)
    return _call(
        body, name="attn_b_fwd", grid=(4, s // TQB),
        in_specs=[q_spec, k_spec, v_spec],
        out_specs=[pl.BlockSpec((TQB, 128), lambda hp, qi: (qi, hp)),
                   pl.BlockSpec((1, TQB, 128), lambda hp, qi: (hp, qi, 0))],
        out_shape=[_sds((s, WG), F32), _sds((4, s, 128), F32)],
        compiler_params=_params(2),
    )(proj, proj, proj)


def _mix_out(oa, ob, g, w_out, x, gate, tm):
    s = x.shape[0]

    def body(oa_ref, ob_ref, g_ref, w_ref, x_ref, gate_ref, nab_ref, mixed_ref, x2_ref):
        a, b = oa_ref[...], ob_ref[...]
        nab_ref[:, :WG] = (a * _rms(a) * g_ref[:, :WG]).astype(BF16)
        nab_ref[:, WG:] = (b * _rms(b) * g_ref[:, WG:]).astype(BF16)
        mixed = _dot(nab_ref[...], w_ref[...])
        mixed_ref[...] = mixed
        x2_ref[...] = x_ref[...] + gate_ref[...] * mixed

    row = pl.BlockSpec((tm, D), lambda i: (i, 0))
    half = pl.BlockSpec((tm, WG), lambda i: (i, 0))
    vec = pl.BlockSpec((1, D), lambda i: (0, 0))
    return _call(
        body, name="mix_out", grid=(s // tm,),
        in_specs=[half, half, vec, pl.BlockSpec((D, D), lambda i: (0, 0)), row, vec],
        out_specs=[row, row, row],
        out_shape=[_sds((s, D), BF16), _sds((s, D), F32), _sds((s, D), F32)],
        compiler_params=_params(1),
    )(oa, ob, g, w_out, x, gate)


def _conv_taps(u, halo, first, tm):
    rows = lax.broadcasted_iota(jnp.int32, (tm, 1), 0)
    keep = jnp.where(first, 0.0, 1.0)
    h1 = halo[7:8, :] * keep
    h2 = halo[6:7, :] * keep
    um1 = jnp.where(rows == 0, h1, pltpu.roll(u, 1, 0))
    um2 = jnp.where(rows == 0, h2, jnp.where(rows == 1, h1, pltpu.roll(u, 2, 0)))
    return um1, um2


def _conv_specs(tm):
    u_spec = pl.BlockSpec((2, 1, tm, GU), lambda p, i: (0, p, i, 0))
    halo_spec = pl.BlockSpec((2, 1, 8, GU), lambda p, i: (0, p, jnp.maximum(i * (tm // 8) - 1, 0), 0))
    cw_spec = pl.BlockSpec((2, 1, 3, GU), lambda p, i: (0, p, 0, 0))
    cb_spec = pl.BlockSpec((2, 1, 1, GU), lambda p, i: (0, p, 0, 0))
    return u_spec, halo_spec, cw_spec, cb_spec


def _conv_act(u, conv_w, conv_b, tm):
    s = u.shape[2]

    def body(u_ref, halo_ref, cw_ref, cb_ref, a_ref):
        first = pl.program_id(1) == 0
        ys = []
        for side in range(2):
            uv = u_ref[side, 0]
            um1, um2 = _conv_taps(uv, halo_ref[side, 0], first, tm)
            cw = cw_ref[side, 0]
            ys.append(cw[2:3] * uv + cw[1:2] * um1 + cw[0:1] * um2 + cb_ref[side, 0])
        a_ref[0] = (ys[0] * _sigmoid(ys[0]) * ys[1]).astype(BF16)

    u_spec, halo_spec, cw_spec, cb_spec = _conv_specs(tm)
    return _call(
        body, name="conv_act", grid=(4, s // tm),
        in_specs=[u_spec, halo_spec, cw_spec, cb_spec],
        out_specs=pl.BlockSpec((1, tm, GU), lambda p, i: (p, i, 0)),
        out_shape=_sds((4, s, GU), BF16),
        compiler_params=_params(2),
    )(u, u, conv_w, conv_b)


def _down(a, w_down, x2, gate, tm):
    s = x2.shape[0]

    def body(a_ref, w_ref, x_ref, gate_ref, ffn_ref, x3_ref):
        p = pl.program_id(1)
        part = _dot(a_ref[0], w_ref[0])

        @pl.when(p == 0)
        def _():
            ffn_ref[...] = part

        @pl.when(p > 0)
        def _():
            ffn_ref[...] += part

        @pl.when(p == 3)
        def _():
            x3_ref[...] = x_ref[...] + gate_ref[...] * ffn_ref[...]

    row = pl.BlockSpec((tm, D), lambda i, p: (i, 0))
    return _call(
        body, name="down", grid=(s // tm, 4),
        in_specs=[pl.BlockSpec((1, tm, GU), lambda i, p: (p, i, 0)), pl.BlockSpec((1, GU, D), lambda i, p: (p, 0, 0)),
                  row, pl.BlockSpec((1, D), lambda i, p: (0, 0))],
        out_specs=[row, row],
        out_shape=[_sds((s, D), F32), _sds((s, D), F32)],
        compiler_params=_params(2),
    )(a, w_down, x2, gate)


def _final_loss(x, g, target, tm):
    s = x.shape[0]

    def body(x_ref, g_ref, t_ref, loss_ref, dx_ref, dg_ref):
        @pl.when(pl.program_id(0) == 0)
        def _():
            loss_ref[...] = jnp.zeros_like(loss_ref)
            dg_ref[...] = jnp.zeros_like(dg_ref)
        xv = x_ref[...]
        r = _rms(xv)
        nrm = xv * r
        err = nrm * g_ref[...] - t_ref[...]
        loss_ref[...] += (0.5 / D) * jnp.sum(jnp.sum(err * err, axis=-1, keepdims=True), axis=0, keepdims=True)
        dy = err * (1.0 / D)
        dg_ref[...] += _colsum(dy * nrm)
        dx_ref[...] = _rms_bwd(dy * g_ref[...], nrm, r)

    row = pl.BlockSpec((tm, D), lambda i: (i, 0))
    vec = pl.BlockSpec((1, D), lambda i: (0, 0))
    return _call(
        body, name="final_loss", grid=(s // tm,),
        in_specs=[row, vec, row],
        out_specs=[pl.BlockSpec((1, 1), lambda i: (0, 0)), row, vec],
        out_shape=[_sds((1, 1), F32), _sds((s, D), F32), _sds((1, D), F32)],
        compiler_params=_params(1),
    )(x, g, target)


def _down_bwd(dx3, gate, ffn, w_down, tm):
    s = dx3.shape[0]

    def body(dx_ref, gate_ref, ffn_ref, w_ref, dgate_ref, dff_ref, da_ref):
        i, p = pl.program_id(0), pl.program_id(1)

        @pl.when((i == 0) & (p == 0))
        def _():
            dgate_ref[...] = jnp.zeros_like(dgate_ref)

        @pl.when(p == 0)
        def _():
            dxv = dx_ref[...]
            dgate_ref[...] += _colsum(dxv * ffn_ref[...])
            dff_ref[...] = (dxv * gate_ref[...]).astype(BF16)

        da_ref[0] = _dot_nt(dff_ref[...], w_ref[0])

    row = pl.BlockSpec((tm, D), lambda i, p: (i, 0))
    vec = pl.BlockSpec((1, D), lambda i, p: (0, 0))
    return _call(
        body, name="down_bwd", grid=(s // tm, 4),
        in_specs=[row, vec, row, pl.BlockSpec((1, GU, D), lambda i, p: (p, 0, 0))],
        out_specs=[vec, row, pl.BlockSpec((1, tm, GU), lambda i, p: (p, i, 0))],
        out_shape=[_sds((1, D), F32), _sds((s, D), BF16), _sds((4, s, GU), F32)],
        compiler_params=_params(2),
    )(dx3, gate, ffn, w_down)


def _conv_act_bwd(u, conv_w, conv_b, da, tm):
    s = u.shape[2]

    def body(u_ref, halo_ref, cw_ref, cb_ref, da_ref, dy_ref, dcw_ref, dcb_ref):
        first = pl.program_id(1) == 0

        @pl.when(first)
        def _():
            dcw_ref[...] = jnp.zeros_like(dcw_ref)
            dcb_ref[...] = jnp.zeros_like(dcb_ref)

        taps, ys = [], []
        for side in range(2):
            uv = u_ref[side, 0]
            um1, um2 = _conv_taps(uv, halo_ref[side, 0], first, tm)
            cw = cw_ref[side, 0]
            taps.append((um2, um1, uv))
            ys.append(cw[2:3] * uv + cw[1:2] * um1 + cw[0:1] * um2 + cb_ref[side, 0])
        dav = da_ref[0]
        sg = _sigmoid(ys[0])
        dys = (dav * ys[1] * (sg * (1.0 + ys[0] * (1.0 - sg))), dav * (ys[0] * sg))
        for side in range(2):
            dy_ref[side, 0] = dys[side]
            dcb_ref[side, 0] += _colsum(dys[side])
            for j in range(3):
                dcw_ref[side, 0, j:j + 1, :] += _colsum(dys[side] * taps[side][j])

    u_spec, halo_spec, cw_spec, cb_spec = _conv_specs(tm)
    return _call(
        body, name="conv_act_bwd", grid=(4, s // tm),
        in_specs=[u_spec, halo_spec, cw_spec, cb_spec, pl.BlockSpec((1, tm, GU), lambda p, i: (p, i, 0))],
        out_specs=[u_spec, cw_spec, cb_spec],
        out_shape=[_sds((2, 4, s, GU), F32), _sds((2, 4, 3, GU), F32), _sds((2, 4, 1, GU), F32)],
        compiler_params=_params(2),
    )(u, u, conv_w, conv_b, da)


def _conv_transpose(dy, conv_w, tm):
    s = dy.shape[1]
    nt = s // tm

    def body(dy_ref, halo_ref, cw_ref, du_ref):
        keep = jnp.where(pl.program_id(1) == nt - 1, 0.0, 1.0)
        dv = dy_ref[0]
        rows = lax.broadcasted_iota(jnp.int32, (tm, 1), 0)
        h0 = halo_ref[0, 0:1, :] * keep
        h1 = halo_ref[0, 1:2, :] * keep
        dp1 = jnp.where(rows == tm - 1, h0, pltpu.roll(dv, tm - 1, 0))
        dp2 = jnp.where(rows == tm - 1, h1, jnp.where(rows == tm - 2, h0, pltpu.roll(dv, tm - 2, 0)))
        cw = cw_ref[0]
        du_ref[0] = (cw[2:3] * dv + cw[1:2] * dp1 + cw[0:1] * dp2).astype(BF16)

    blk = pl.BlockSpec((1, tm, GU), lambda g, i: (g, i, 0))
    return _call(
        body, name="conv_transpose", grid=(NDEV, nt),
        in_specs=[blk, pl.BlockSpec((1, 8, GU), lambda g, i: (g, jnp.minimum((i + 1) * (tm // 8), s // 8 - 1), 0)),
                  pl.BlockSpec((1, 3, GU), lambda g, i: (g, 0, 0))],
        out_specs=blk,
        out_shape=_sds((NDEV, s, GU), BF16),
        compiler_params=_params(2),
    )(dy, dy, conv_w)


def _wgrad(a3, b3, *, out_two_d, tk, name):
    ga, s, ka = a3.shape
    gb, _, nb = b3.shape
    groups = max(ga, gb)
    nk = s // tk

    def body(a_ref, b_ref, o_ref, acc):
        k = pl.program_id(1)

        @pl.when(k == 0)
        def _():
            acc[...] = jnp.zeros_like(acc)

        acc[...] += _dot_tn(a_ref[0], b_ref[0])

        @pl.when(k == nk - 1)
        def _():
            if out_two_d:
                o_ref[...] = acc[...].astype(BF16)
            else:
                o_ref[0] = acc[...].astype(BF16)

    a_spec = pl.BlockSpec((1, tk, ka), (lambda g, k: (g, k, 0)) if ga > 1 else (lambda g, k: (0, k, 0)))
    b_spec = pl.BlockSpec((1, tk, nb), (lambda g, k: (g, k, 0)) if gb > 1 else (lambda g, k: (0, k, 0)))
    if out_two_d:
        o_spec, o_shape = pl.BlockSpec((ka, nb), lambda g, k: (0, g)), _sds((ka, groups * nb), BF16)
    else:
        o_spec, o_shape = pl.BlockSpec((1, ka, nb), lambda g, k: (g, 0, 0)), _sds((groups, ka, nb), BF16)
    return _call(
        body, name=name, grid=(groups, nk),
        in_specs=[a_spec, b_spec], out_specs=o_spec, out_shape=o_shape,
        scratch_shapes=[pltpu.VMEM((ka, nb), F32)],
        compiler_params=_params(2),
    )(a3, b3)


def _wgrad_in(h1, dparts, tk):
    s = h1.shape[0]
    nk = s // tk

    def body(a_ref, *refs):
        d_refs, o_ref, acc = refs[:NT], refs[NT], refs[NT + 1]
        k = pl.program_id(0)

        @pl.when(k == 0)
        def _():
            acc[...] = jnp.zeros_like(acc)

        for j in range(NT):
            acc[:, WG * j:WG * (j + 1)] += _dot_tn(a_ref[...], d_refs[j][...])

        @pl.when(k == nk - 1)
        def _():
            o_ref[...] = acc[...].astype(BF16)

    return _call(
        body, name="wgrad_in", grid=(nk,),
        in_specs=[pl.BlockSpec((tk, D), lambda k: (k, 0))] + [pl.BlockSpec((tk, WG), lambda k: (k, 0))] * NT,
        out_specs=pl.BlockSpec((D, NT * WG), lambda k: (0, 0)),
        out_shape=_sds((D, NT * WG), BF16),
        scratch_shapes=[pltpu.VMEM((D, NT * WG), F32)],
        compiler_params=_params(1),
    )(h1, *dparts)


def _dgrad_in(dparts, w, x_in, scale, dx_up, tm):
    s = x_in.shape[0]

    def body(*refs):
        d_refs = refs[:NT]
        w_ref, x_ref, sc_ref, up_ref, dx_ref, dsc_ref, dsh_ref = refs[NT:]

        @pl.when(pl.program_id(0) == 0)
        def _():
            dsc_ref[...] = jnp.zeros_like(dsc_ref)
            dsh_ref[...] = jnp.zeros_like(dsh_ref)

        dh = _dot_nt(d_refs[0][...], w_ref[:, 0:WG])
        for j in range(1, NT):
            dh = dh + _dot_nt(d_refs[j][...], w_ref[:, WG * j:WG * (j + 1)])
        xv = x_ref[...]
        r = _rms(xv)
        nrm = xv * r
        dsh_ref[...] += _colsum(dh)
        dsc_ref[...] += _colsum(dh * nrm)
        dx_ref[...] = up_ref[...] + _rms_bwd(dh * (1.0 + sc_ref[...]), nrm, r)

    row = pl.BlockSpec((tm, D), lambda i: (i, 0))
    vec = pl.BlockSpec((1, D), lambda i: (0, 0))
    return _call(
        body, name="dgrad_in", grid=(s // tm,),
        in_specs=[pl.BlockSpec((tm, WG), lambda i: (i, 0))] * NT + [pl.BlockSpec((D, NT * WG), lambda i: (0, 0)), row, vec, row],
        out_specs=[row, vec, vec],
        out_shape=[_sds((s, D), F32), _sds((1, D), F32), _sds((1, D), F32)],
        compiler_params=_params(1),
    )(*dparts, w, x_in, scale, dx_up)


def _dgrad_norm_bwd(d3, w, x_in, scale, dx_up, *, two_d, tm, name):
    groups, s, n = d3.shape

    def body(d_ref, w_ref, x_ref, sc_ref, up_ref, dx_ref, dsc_ref, dsh_ref, acc):
        i, g = pl.program_id(0), pl.program_id(1)

        @pl.when((i == 0) & (g == 0))
        def _():
            dsc_ref[...] = jnp.zeros_like(dsc_ref)
            dsh_ref[...] = jnp.zeros_like(dsh_ref)

        part = _dot_nt(d_ref[0], w_ref[...] if two_d else w_ref[0])

        @pl.when(g == 0)
        def _():
            acc[...] = part

        @pl.when(g > 0)
        def _():
            acc[...] += part

        @pl.when(g == groups - 1)
        def _():
            dh = acc[...]
            xv = x_ref[...]
            r = _rms(xv)
            nrm = xv * r
            dsh_ref[...] += _colsum(dh)
            dsc_ref[...] += _colsum(dh * nrm)
            dx_ref[...] = up_ref[...] + _rms_bwd(dh * (1.0 + sc_ref[...]), nrm, r)

    row = pl.BlockSpec((tm, D), lambda i, g: (i, 0))
    vec = pl.BlockSpec((1, D), lambda i, g: (0, 0))
    w_spec = pl.BlockSpec((D, n), lambda i, g: (0, g)) if two_d else pl.BlockSpec((1, D, n), lambda i, g: (g, 0, 0))
    return _call(
        body, name=name, grid=(s // tm, groups),
        in_specs=[pl.BlockSpec((1, tm, n), lambda i, g: (g, i, 0)), w_spec, row, vec, row],
        out_specs=[row, vec, vec],
        out_shape=[_sds((s, D), F32), _sds((1, D), F32), _sds((1, D), F32)],
        scratch_shapes=[pltpu.VMEM((tm, D), F32)],
        compiler_params=_params(2),
    )(d3, w, x_in, scale, dx_up)


def _mix_out_bwd(dx2, mixed, gate, w_out, oa, ob, g, tm):
    s = dx2.shape[0]

    def body(dx_ref, mixed_ref, gate_ref, w_ref, oa_ref, ob_ref, g_ref, dgate_ref, dm_ref, doa_ref, dob_ref, dg_ref):
        @pl.when(pl.program_id(0) == 0)
        def _():
            dgate_ref[...] = jnp.zeros_like(dgate_ref)
            dg_ref[...] = jnp.zeros_like(dg_ref)
        dxv = dx_ref[...]
        dgate_ref[...] += _colsum(dxv * mixed_ref[...])
        dm_ref[...] = (dxv * gate_ref[...]).astype(BF16)
        dnab = _dot_nt(dm_ref[...], w_ref[...])
        for o_ref, do_ref, sl in ((oa_ref, doa_ref, slice(0, WG)), (ob_ref, dob_ref, slice(WG, D))):
            ov = o_ref[...]
            r = _rms(ov)
            nrm = ov * r
            dn = dnab[:, sl]
            dg_ref[:, sl] += _colsum(dn * nrm)
            do_ref[...] = _rms_bwd(dn * g_ref[:, sl], nrm, r)

    row = pl.BlockSpec((tm, D), lambda i: (i, 0))
    half = pl.BlockSpec((tm, WG), lambda i: (i, 0))
    vec = pl.BlockSpec((1, D), lambda i: (0, 0))
    return _call(
        body, name="mix_out_bwd", grid=(s // tm,),
        in_specs=[row, row, vec, pl.BlockSpec((D, D), lambda i: (0, 0)), half, half, vec],
        out_specs=[vec, row, half, half, vec],
        out_shape=[_sds((1, D), F32), _sds((s, D), BF16), _sds((s, WG), F32), _sds((s, WG), F32), _sds((1, D), F32)],
        compiler_params=_params(1),
    )(dx2, mixed, gate, w_out, oa, ob, g)


def _attn_a_bwd(proj, bias_tab, d_oa):
    s = proj.shape[1]
    nq = s // TQA

    def body(q_ref, k_ref, v_ref, b_ref, do_ref, dq_ref, dk_ref, dv_ref, db_ref, dk_acc, dv_acc):
        qi = pl.program_id(1)

        @pl.when(qi == 0)
        def _():
            dk_acc[...] = jnp.zeros_like(dk_acc)
            dv_acc[...] = jnp.zeros_like(dv_acc)
            db_ref[...] = jnp.zeros_like(db_ref)

        def tile(nk, off, kstart):
            for h in range(2):
                sl = slice(HD * h, HD * (h + 1))
                q = q_ref[0, :, sl]
                k = k_ref[0, pl.ds(kstart, nk), sl]
                v = v_ref[0, pl.ds(kstart, nk), sl]
                do = do_ref[:, sl].astype(BF16)
                sc = _dot_nt(q, k) * 0.125 + b_ref[h, :, off:off + nk]
                p = jnp.exp(sc - jnp.max(sc, axis=-1, keepdims=True))
                p = p / jnp.sum(p, axis=-1, keepdims=True)
                dp = _dot_nt(do, v)
                ds = p * (dp - jnp.sum(dp * p, axis=-1, keepdims=True))
                db_ref[h, :, off:off + nk] += ds
                dsb = (ds * 0.125).astype(BF16)
                dq_ref[:, sl] = _dot(dsb, k).astype(BF16)
                dk_acc[pl.ds(kstart, nk), sl] += _dot_tn(dsb, q)
                dv_acc[pl.ds(kstart, nk), sl] += _dot_tn(p.astype(BF16), do)

        _attn_a_cases(qi, tile)

        @pl.when(qi == nq - 1)
        def _():
            dk_ref[...] = dk_acc[...].astype(BF16)
            dv_ref[...] = dv_acc[...].astype(BF16)

    q_spec, k_spec, v_spec, b_spec = _attn_a_specs(s)
    blk = pl.BlockSpec((TQA, 128), lambda hp, qi: (qi, hp))
    col = pl.BlockSpec((s, 128), lambda hp, qi: (0, hp))
    return _call(
        body, name="attn_a_bwd", grid=(4, nq),
        in_specs=[q_spec, k_spec, v_spec, b_spec, blk],
        out_specs=[blk, col, col, b_spec],
        out_shape=[_sds((s, WG), BF16), _sds((s, WG), BF16), _sds((s, WG), BF16), _sds((NDEV, TQA, KWA), F32)],
        scratch_shapes=[pltpu.VMEM((s, 128), F32), pltpu.VMEM((s, 128), F32)],
        compiler_params=_params(2),
    )(proj, proj, proj, bias_tab, d_oa)


def _attn_b_bwd(proj, lsum, d_ob):
    s = proj.shape[1]
    nq = s // TQB
    ndiag = TQB // TKB

    def body(q_ref, k_ref, v_ref, ls_ref, do_ref, dq_ref, dk_ref, dv_ref, dk_acc, dv_acc):
        qi = pl.program_id(1)
        q0 = qi * TQB

        @pl.when(qi == 0)
        def _():
            dk_acc[...] = jnp.zeros_like(dk_acc)
            dv_acc[...] = jnp.zeros_like(dv_acc)

        rows = lax.broadcasted_iota(jnp.int32, (TQB, TKB), 0)
        cols = lax.broadcasted_iota(jnp.int32, (TQB, TKB), 1)
        uj = lax.broadcasted_iota(jnp.int32, (TKB, TKB), 0)
        us = lax.broadcasted_iota(jnp.int32, (TKB, TKB), 1)
        prefix = jnp.where(uj <= us, 1.0, 0.0).astype(BF16)
        heads = [slice(HD * h, HD * (h + 1)) for h in range(2)]
        qs = [q_ref[0, :, sl] for sl in heads]
        dos = [do_ref[:, sl].astype(BF16) for sl in heads]
        ltots = [ls_ref[0, :, HD * h:HD * h + 1] for h in range(2)]

        def tile(h, kstart, diag, cl, cg, dq):
            sl = heads[h]
            k = k_ref[0, pl.ds(kstart, TKB), sl]
            v = v_ref[0, pl.ds(kstart, TKB), sl]
            lb, lk = _sb_logits(qs[h], k)
            if diag is not None:
                strict = rows > cols + diag * TKB
                lk = jnp.where(strict, lk, 0.0)
            pre = _dot2(lk, prefix) + cl
            a = jnp.exp(lb + ltots[h] - pre)
            if diag is not None:
                a = jnp.where(strict, a, 0.0)
            gz = _dot_nt(dos[h], v) * a
            pg = _dot2(gz, prefix) + cg
            sig = jnp.exp(lb)
            dl = gz * (1.0 - sig) - (pg - gz) * sig
            if diag is not None:
                dl = jnp.where(strict, dl, 0.0)
            dlb = (dl * 0.125).astype(BF16)
            dk_acc[pl.ds(kstart, TKB), sl] += _dot_tn(dlb, qs[h])
            dv_acc[pl.ds(kstart, TKB), sl] += _dot_tn(a.astype(BF16), dos[h])
            return pre[:, TKB - 1:TKB], pg[:, TKB - 1:TKB], dq + _dot(dlb, k)

        state = [jnp.zeros((TQB, 1), F32), jnp.zeros((TQB, 1), F32), jnp.zeros((TQB, HD), F32)] * 2

        def step(i, st):
            st = list(st)
            base = i * (KSTEP * TKB)
            for sub in range(KSTEP):
                for h in range(2):
                    st[3 * h:3 * h + 3] = tile(h, pl.multiple_of(base + sub * TKB, TKB), None, *st[3 * h:3 * h + 3])
            return tuple(st)

        state = list(lax.fori_loop(0, q0 // (KSTEP * TKB), step, tuple(state)))
        for d in range(ndiag):
            for h in range(2):
                state[3 * h:3 * h + 3] = tile(h, pl.multiple_of(q0 + d * TKB, TKB), d, *state[3 * h:3 * h + 3])
        for h in range(2):
            dq_ref[:, heads[h]] = state[3 * h + 2].astype(BF16)

        @pl.when(qi == nq - 1)
        def _():
            dk_ref[...] = dk_acc[...].astype(BF16)
            dv_ref[...] = dv_acc[...].astype(BF16)

    q_spec, k_spec, v_spec = _attn_b_specs(s)
    blk = pl.BlockSpec((TQB, 128), lambda hp, qi: (qi, hp))
    col = pl.BlockSpec((s, 128), lambda hp, qi: (0, hp))
    return _call(
        body, name="attn_b_bwd", grid=(4, nq),
        in_specs=[q_spec, k_spec, v_spec, pl.BlockSpec((1, TQB, 128), lambda hp, qi: (hp, qi, 0)), blk],
        out_specs=[blk, col, col],
        out_shape=[_sds((s, WG), BF16)] * 3,
        scratch_shapes=[pltpu.VMEM((s, 128), F32), pltpu.VMEM((s, 128), F32)],
        compiler_params=_params(2),
    )(proj, proj, proj, lsum, d_ob)


def _bias_fold(dtab):
    def body(t_ref, d_ref, far_ref):
        acc = jnp.zeros((NDEV, TABW), F32)
        zpad = jnp.zeros((NDEV, TAB0), F32)
        for r in range(TQA):
            row = jnp.concatenate([zpad, t_ref[:, r, :]], axis=1)
            acc = acc + (pltpu.roll(row, TABW - r, 1) if r else row)
        d_ref[...] = acc
        lane = lax.broadcasted_iota(jnp.int32, (NDEV, TABW), 1)
        far = jnp.sum(jnp.where(lane < N_FAR, acc, 0.0), axis=1, keepdims=True)
        far_ref[...] = jnp.broadcast_to(far, (NDEV, 128))

    d_fpad, d_far = _call(
        body, name="bias_fold", grid=(1,),
        in_specs=[pl.BlockSpec((NDEV, TQA, KWA), lambda i: (0, 0, 0))],
        out_specs=[pl.BlockSpec((NDEV, TABW), lambda i: (0, 0)), pl.BlockSpec((NDEV, 128), lambda i: (0, 0))],
        out_shape=[_sds((NDEV, TABW), F32), _sds((NDEV, 128), F32)],
        compiler_params=_params(1),
    )(dtab)
    d_near = d_fpad[:, N_FAR:N_FAR + N_NEAR][:, ::-1]
    return jnp.concatenate([jnp.zeros((NDEV, REL_CLIP - CHUNK + 1), F32), d_near, d_far[:, :1]], axis=1)


def _adamw(w, g, m, v):
    m = B1 * m + (1.0 - B1) * g
    v = B2 * v + (1.0 - B2) * (g * g)
    m_hat = m / (1.0 - B1 ** STEP)
    v_hat = v / (1.0 - B2 ** STEP)
    delta = -LR * (m_hat / (jnp.sqrt(v_hat) + AEPS) + WD * w)
    return delta, m, v


def _adamw_big(recv, w, m, v):
    _, _, rows, cols = recv.shape
    tr = rows if rows <= 512 else 256

    def body(r_ref, w_ref, m_ref, v_ref, g_ref, d_ref, nm_ref, nv_ref):
        g = r_ref[0, 0].astype(F32)
        for p in range(1, NDEV):
            g = g + r_ref[p, 0].astype(F32)
        delta, nm, nv = _adamw(w_ref[0], g, m_ref[0], v_ref[0])
        g_ref[0], d_ref[0], nm_ref[0], nv_ref[0] = g, delta, nm, nv

    blk = pl.BlockSpec((1, tr, cols), lambda l, i: (l, i, 0))
    return _call(
        body, name="adamw_big", grid=(2, rows // tr),
        in_specs=[pl.BlockSpec((NDEV, 1, tr, cols), lambda l, i: (0, l, i, 0)), blk, blk, blk],
        out_specs=[blk] * 4,
        out_shape=[_sds((2, rows, cols), F32)] * 4,
        compiler_params=_params(2),
    )(recv, w, m, v)


def _adamw_w_ada(cact_t, dmod, w, m, v):
    tr = 256

    def body(c_ref, dm_ref, w_ref, m_ref, v_ref, g_ref, d_ref, nm_ref, nv_ref):
        g = c_ref[:, 0:1] * dm_ref[0, 0:1, :]
        for b in range(1, NDEV):
            g = g + c_ref[:, b:b + 1] * dm_ref[0, b:b + 1, :]
        delta, nm, nv = _adamw(w_ref[0], g, m_ref[0], v_ref[0])
        g_ref[0], d_ref[0], nm_ref[0], nv_ref[0] = g, delta, nm, nv

    blk = pl.BlockSpec((1, tr, 768), lambda l, i: (l, i, 0))
    return _call(
        body, name="adamw_w_ada", grid=(2, D // tr),
        in_specs=[pl.BlockSpec((tr, NDEV), lambda l, i: (i, 0)), pl.BlockSpec((1, NDEV, 768), lambda l, i: (l, 0, 0)),
                  blk, blk, blk],
        out_specs=[blk] * 4,
        out_shape=[_sds((2, D, 768), F32)] * 4,
        compiler_params=_params(2),
    )(cact_t, dmod, w, m, v)


def _adamw_small(gath, w, m, v):
    rows = gath.shape[1]

    def body(r_ref, w_ref, m_ref, v_ref, g_ref, d_ref, nm_ref, nv_ref):
        g = r_ref[0]
        for p in range(1, NDEV):
            g = g + r_ref[p]
        delta, nm, nv = _adamw(w_ref[...], g, m_ref[...], v_ref[...])
        g_ref[...], d_ref[...], nm_ref[...], nv_ref[...] = g, delta, nm, nv

    blk = pl.BlockSpec((rows, D), lambda i: (0, 0))
    return _call(
        body, name="adamw_small", grid=(1,),
        in_specs=[pl.BlockSpec((NDEV, rows, D), lambda i: (0, 0, 0)), blk, blk, blk],
        out_specs=[blk] * 4,
        out_shape=[_sds((rows, D), F32)] * 4,
        compiler_params=_params(1),
    )(gath, w, m, v)


_PACK = (("b_ada", 2 * 6 * D), ("rel_bias", 2 * 8 * 257), ("g_a", 2 * WG), ("g_b", 2 * WG),
         ("conv_b", 2 * 2 * DFF), ("final_g", D), ("conv_w", 2 * NDEV * 3 * GU))


def _pack(parts):
    rows = []
    for name, size in _PACK:
        flat = parts[name].reshape(-1).astype(F32)
        assert flat.shape[0] == size, (name, flat.shape)
        rows.append(jnp.pad(flat, (0, -size % D)))
    out = jnp.concatenate(rows).reshape(-1, D)
    return jnp.pad(out, ((0, -out.shape[0] % 8), (0, 0)))


def _unpack(packed):
    flat = packed.reshape(-1)
    out, pos = {}, 0
    for name, size in _PACK:
        out[name] = flat[pos:pos + size]
        pos += size + (-size % D)
    return out


def kernel(x, c, w_ada, b_ada, w_in, rel_bias, g_a, g_b, w_out, w_up, conv_w, conv_b, w_down, final_g, loss_target, m_w_ada, m_b_ada, m_w_in, m_rel_bias, m_g_a, m_g_b, m_w_out, m_w_up, m_conv_w, m_conv_b, m_w_down, m_final_g, v_w_ada, v_b_ada, v_w_in, v_rel_bias, v_g_a, v_g_b, v_w_out, v_w_up, v_conv_w, v_conv_b, v_w_down, v_final_g):
    s = x.shape[1]
    assert s % TQA == 0 and s >= KWA and s % 512 == 0
    tm = 512
    me = 4 * lax.axis_index("x") + 2 * lax.axis_index("y") + lax.axis_index("c")
    xs = x.reshape(s, D)
    target = loss_target.reshape(s, D)

    full = _weights_allgather([w_in.astype(BF16), w_out.astype(BF16), w_up.astype(BF16), w_down.astype(BF16)])

    c_all = _small_allgather(jnp.pad(c, ((0, 7), (0, 0))), "gather_c")[:, 0, :]
    b_sl = lax.dynamic_slice(b_ada, (0, me * 768), (2, 768)).reshape(2, 1, 768)
    mod_part, cact = _mod_fwd(c_all, w_ada, b_sl)
    mod_all = _small_allgather(mod_part.reshape(2 * NDEV, 768), "gather_mod")
    mod_all = mod_all.reshape(NDEV, 2, NDEV, 768)
    mod_me = lax.dynamic_index_in_dim(mod_all, me, axis=2, keepdims=False)
    mod = jnp.transpose(mod_me, (1, 0, 2)).reshape(2, 6, 1, D)

    cw_all = _small_allgather(jnp.pad(conv_w.reshape(2 * 3, GU), ((0, 2), (0, 0))), "gather_conv_w")
    cw_all = cw_all[:, :6, :].reshape(NDEV, 2, 3, GU)

    saved = []
    xl = xs
    for l in range(2):
        sh_mix, sc_mix, gt_mix, sh_ffn, sc_ffn, gt_ffn = (mod[l, j] for j in range(6))
        cw = cw_all[:, l].reshape(2, 4, 3, GU)
        cb = conv_b[l].reshape(2, 4, 1, GU)
        gvec = jnp.concatenate([g_a[l], g_b[l]]).reshape(1, D)
        tab = _bias_table(rel_bias[l])
        wd4 = full["down"][l].reshape(4, GU, D)

        h1, proj = _nm_matmul(xl, sh_mix, sc_mix, full["in"][l], two_d=True, n=WG, groups=NT, out_dtype=BF16, tm=tm,
                              name="norm_proj")
        oa = _attn_a_fwd(proj, tab)
        ob, lsum = _attn_b_fwd(proj)
        nab, mixed, x2 = _mix_out(oa, ob, gvec, full["out"][l], xl, gt_mix, tm)
        h2, u = _nm_matmul(x2, sh_ffn, sc_ffn, full["up"][l], two_d=False, n=GU, groups=NDEV, out_dtype=F32, tm=tm,
                           name="norm_up")
        u = u.reshape(2, 4, s, GU)
        a = _conv_act(u, cw, cb, tm)
        ffn, x3 = _down(a, wd4, x2, gt_ffn, tm)
        saved.append(dict(x=xl, h1=h1, proj=proj, oa=oa, ob=ob, lsum=lsum, nab=nab, mixed=mixed, x2=x2, h2=h2, u=u,
                          a=a, ffn=ffn, cw=cw, cb=cb, gvec=gvec, tab=tab, wd4=wd4))
        xl = x3

    loss_part, dx, d_final_g = _final_loss(xl, final_g.reshape(1, D), target, tm)
    loss = lax.psum(loss_part[0, 0], ("x", "y", "c"))

    grads = {kind: [None, None] for kind in _KINDS}
    small = {"b_ada": [None, None], "rel_bias": [None, None], "g_a": [None, None], "g_b": [None, None],
             "conv_b": [None, None], "conv_w": [None, None]}
    for l in (1, 0):
        sv = saved[l]
        sh_mix, sc_mix, gt_mix, sh_ffn, sc_ffn, gt_ffn = (mod[l, j] for j in range(6))
        d_gt_ffn, dff, da = _down_bwd(dx, gt_ffn, sv["ffn"], sv["wd4"], tm)
        grads["down"][l] = _wgrad(sv["a"], dff.reshape(1, s, D), out_two_d=False, tk=tm, name="wgrad_down").reshape(DFF, D)
        dy, d_cw, d_cb = _conv_act_bwd(sv["u"], sv["cw"], sv["cb"], da, tm)
        du = _conv_transpose(dy.reshape(NDEV, s, GU), sv["cw"].reshape(NDEV, 3, GU), tm)
        grads["up"][l] = _wgrad(sv["h2"].reshape(1, s, D), du, out_two_d=False, tk=tm, name="wgrad_up")
        dx2, d_sc_ffn, d_sh_ffn = _dgrad_norm_bwd(du, full["up"][l], sv["x2"], sc_ffn, dx, two_d=False, tm=tm,
                                                  name="dgrad_up")
        d_gt_mix, dmixed, d_oa, d_ob, d_g = _mix_out_bwd(dx2, sv["mixed"], gt_mix, full["out"][l], sv["oa"], sv["ob"],
                                                         sv["gvec"], tm)
        grads["out"][l] = _wgrad(sv["nab"].reshape(1, s, D), dmixed.reshape(1, s, D), out_two_d=False, tk=tm,
                                 name="wgrad_out").reshape(D, D)
        dqa, dka, dva, d_tab = _attn_a_bwd(sv["proj"], sv["tab"], d_oa)
        dqb, dkb, dvb = _attn_b_bwd(sv["proj"], sv["lsum"], d_ob)
        dparts = (dqa, dka, dva, dqb, dkb, dvb)
        grads["in"][l] = _wgrad_in(sv["h1"], dparts, tm)
        dx, d_sc_mix, d_sh_mix = _dgrad_in(dparts, full["in"][l], sv["x"], sc_mix, dx2, tm)
        small["b_ada"][l] = jnp.concatenate([d_sh_mix, d_sc_mix, d_gt_mix, d_sh_ffn, d_sc_ffn, d_gt_ffn], axis=1)
        small["rel_bias"][l] = _bias_fold(d_tab)
        small["g_a"][l], small["g_b"][l] = d_g[:, :WG], d_g[:, WG:]
        small["conv_b"][l] = d_cb
        small["conv_w"][l] = d_cw.reshape(NDEV, 3, GU)
    grad_x = dx.reshape(1, s, D)

    recv = _grads_exchange(grads)
    big = {}
    for kind, (w, m, v) in (("in", (w_in, m_w_in, v_w_in)), ("out", (w_out, m_w_out, v_w_out)),
                            ("up", (w_up, m_w_up, v_w_up)), ("down", (w_down, m_w_down, v_w_down))):
        big[kind] = _adamw_big(recv[kind], w, m, v)

    contrib = {k: jnp.stack(vs) for k, vs in small.items()}
    contrib["final_g"] = d_final_g
    gath = _small_allgather(_pack(contrib), "gather_small_grads")

    def place_conv_w(t):
        return lax.dynamic_update_slice(jnp.zeros((2, NDEV, 3, GU), F32), t.reshape(2, 1, 3, GU), (0, me, 0, 0))

    def packed_params(b, rb, ga, gb, cb_, fg, cw_):
        return _pack({"b_ada": b, "rel_bias": rb, "g_a": ga, "g_b": gb, "conv_b": cb_, "final_g": fg,
                      "conv_w": place_conv_w(cw_)})

    sm = _adamw_small(gath,
                      packed_params(b_ada, rel_bias, g_a, g_b, conv_b, final_g, conv_w),
                      packed_params(m_b_ada, m_rel_bias, m_g_a, m_g_b, m_conv_b, m_final_g, m_conv_w),
                      packed_params(v_b_ada, v_rel_bias, v_g_a, v_g_b, v_conv_b, v_final_g, v_conv_w))
    sm = [_unpack(t) for t in sm]

    dmod_all = gath[:, :12, :].reshape(NDEV, 2, 6 * D)
    dmod_sl = jnp.transpose(lax.dynamic_slice(dmod_all, (0, 0, me * 768), (NDEV, 2, 768)), (1, 0, 2))
    ada = _adamw_w_ada(cact.T, dmod_sl, w_ada, m_w_ada, v_w_ada)

    def small_out(j, name):
        t = sm[j][name]
        if name == "b_ada":
            return t.reshape(2, 6 * D)
        if name == "rel_bias":
            return t.reshape(2, 8, 257)
        if name in ("g_a", "g_b"):
            return t.reshape(2, WG)
        if name == "conv_b":
            return t.reshape(2, 2 * DFF)
        if name == "final_g":
            return t.reshape(D)
        t = t.reshape(2, NDEV, 3, GU)
        return lax.dynamic_index_in_dim(t, me, axis=1, keepdims=False)

    def group(j):
        return (ada[j], small_out(j, "b_ada"), big["in"][j], small_out(j, "rel_bias"), small_out(j, "g_a"),
                small_out(j, "g_b"), big["out"][j], big["up"][j], small_out(j, "conv_w"), small_out(j, "conv_b"),
                big["down"][j], small_out(j, "final_g"))

    return (loss, grad_x, *group(0), *group(1), *group(2), *group(3))
```

```python
import jax
import jax.numpy as jnp
from jax import lax
from jax.experimental import pallas as pl
from jax.experimental.pallas import tpu as pltpu

F32, BF16 = jnp.float32, jnp.bfloat16
MESH_ID = pl.DeviceIdType.MESH
NDEV = 8
D = 1024
HD = 64
WG = 512
NT = 6
GU = 704
DFF = 2816
CHUNK, NPREV, REL_CLIP = 64, 8, 128
BAND = (NPREV + 1) * CHUNK
EPS = 1e-6
NEG = -1e30
TQA = 256
KWA = TQA + NPREV * CHUNK
TABW = 1024
TAB0 = TABW - KWA
N_FAR = TAB0 + NPREV * CHUNK - REL_CLIP + 1
N_NEAR = REL_CLIP + CHUNK - 1
TQB, TKB = 512, 128
KSTEP = 4
LR, B1, B2, AEPS, WD, STEP = 0.001, 0.9, 0.999, 1e-08, 0.01, 10
VMEM_MB = 56


def _call(body, **kw):
    return pl.pallas_call(body, **kw)


def _params(n_axes):
    return pltpu.CompilerParams(dimension_semantics=("arbitrary",) * n_axes, vmem_limit_bytes=VMEM_MB << 20)


def _dot(a, b):
    return jnp.dot(a, b, preferred_element_type=F32)


def _dot_nt(a, b):
    return lax.dot_general(a, b, (((1,), (1,)), ((), ())), preferred_element_type=F32)


def _dot_tn(a, b):
    return lax.dot_general(a, b, (((0,), (0,)), ((), ())), preferred_element_type=F32)


def _dot2(x, u):
    hi = x.astype(BF16)
    lo = (x - hi.astype(F32)).astype(BF16)
    return _dot(hi, u) + _dot(lo, u)


def _rms(x):
    return lax.rsqrt(jnp.mean(x * x, axis=-1, keepdims=True) + EPS)


def _rms_bwd(dn, n, r):
    return r * (dn - n * jnp.mean(dn * n, axis=-1, keepdims=True))


def _colsum(x):
    return jnp.sum(x, axis=0, keepdims=True)


def _sigmoid(x):
    return 1.0 / (1.0 + jnp.exp(-x))


def _sds(shape, dtype):
    return jax.ShapeDtypeStruct(shape, dtype)


def _place():
    x, y, c = lax.axis_index("x"), lax.axis_index("y"), lax.axis_index("c")
    return x, y, c, 4 * x + 2 * y + c


def _peer(x, y, c, k):
    px = 1 - x if k & 4 else x
    py = 1 - y if k & 2 else y
    pc = 1 - c if k & 1 else c
    return (px, py, pc), 4 * px + 2 * py + pc


def _small_allgather(v, name):
    rows, cols = v.shape

    def body(v_ref, out_ref, send_sems, recv_sems, local_sem):
        x, y, c, me = _place()
        mine = pltpu.make_async_copy(v_ref, out_ref.at[me], local_sem)
        mine.start()
        sends = []
        for k in range(1, NDEV):
            peer, _ = _peer(x, y, c, k)
            cp = pltpu.make_async_remote_copy(v_ref, out_ref.at[me], send_sems.at[k - 1], recv_sems.at[k - 1],
                                              device_id=peer, device_id_type=MESH_ID)
            cp.start()
            sends.append(cp)
        for k in range(1, NDEV):
            peer, pidx = _peer(x, y, c, k)
            pltpu.make_async_remote_copy(v_ref, out_ref.at[pidx], send_sems.at[k - 1], recv_sems.at[k - 1],
                                         device_id=peer, device_id_type=MESH_ID).wait_recv()
        for cp in sends:
            cp.wait_send()
        mine.wait()

    return _call(
        body, name=name,
        out_shape=_sds((NDEV, rows, cols), F32),
        in_specs=[pl.BlockSpec(memory_space=pltpu.VMEM)],
        out_specs=pl.BlockSpec(memory_space=pltpu.VMEM),
        scratch_shapes=[pltpu.SemaphoreType.DMA((NDEV - 1,)), pltpu.SemaphoreType.DMA((NDEV - 1,)),
                        pltpu.SemaphoreType.DMA],
    )(v)


def _shard_view(ref, kind, p):
    if kind == "in":
        return ref.at[:, pl.ds(pl.multiple_of(p * 384, 128), 384)]
    if kind == "out":
        return ref.at[pl.ds(pl.multiple_of(p * 128, 128), 128), :]
    if kind == "up":
        return ref.at[p]
    if kind == "down":
        return ref.at[pl.ds(pl.multiple_of(p * 352, 16), 352), :]
    raise ValueError(kind)


_KINDS = ("in", "out", "up", "down")
_FULL_SHAPES = {"in": (D, 3 * D), "out": (D, D), "up": (NDEV, D, GU), "down": (DFF, D)}
_SHARD_SHAPES = {"in": (D, 384), "out": (128, D), "up": (D, GU), "down": (352, D)}


_HBM = pl.BlockSpec(memory_space=pltpu.HBM)
_SEM = pl.BlockSpec(memory_space=pltpu.SEMAPHORE)
_EFFECT = pltpu.SideEffectType.DATAFLOW_SIDE_EFFECTING
_SEM_SHAPES = (pltpu.SemaphoreType.DMA((NDEV - 1,)), pltpu.SemaphoreType.DMA((NDEV - 1,)), pltpu.SemaphoreType.DMA(()))


def _hbm(a):
    return pltpu.with_memory_space_constraint(a, pltpu.HBM)


def _exchange_copies(kind, gather, src, land, sems):
    send_sems, recv_sems, local_sem = sems
    x, y, c, me = _place()

    def ends(p_dst, p_from):
        if gather:
            return src, _shard_view(land, kind, me), _shard_view(land, kind, p_from)
        return _shard_view(src, kind, p_dst), land.at[me], land.at[p_from]

    s_me, d_me, _ = ends(me, me)
    local = pltpu.make_async_copy(s_me, d_me, local_sem)
    sends, arrivals = [], []
    for k in range(1, NDEV):
        peer, pidx = _peer(x, y, c, k)
        s_k, d_k, from_k = ends(pidx, pidx)
        sends.append(pltpu.make_async_remote_copy(s_k, d_k, send_sems.at[k - 1], recv_sems.at[k - 1],
                                                  device_id=peer, device_id_type=MESH_ID))
        arrivals.append(pltpu.make_async_remote_copy(s_k, from_k, send_sems.at[k - 1], recv_sems.at[k - 1],
                                                     device_id=peer, device_id_type=MESH_ID))
    return local, sends, arrivals


def _exchange_start(kinds, gather, srcs, name, with_token=False):
    n = len(kinds)
    lands = [lax.empty(_FULL_SHAPES[kd] if gather else (NDEV,) + _SHARD_SHAPES[kd], BF16) for kd in kinds]

    def body(*refs):
        ins, sems = refs[:2 * n], refs[2 * n:5 * n]
        for j, kd in enumerate(kinds):
            local, sends, _ = _exchange_copies(kd, gather, ins[j], ins[n + j], sems[3 * j:3 * j + 3])
            local.start()
            for cp in sends:
                cp.start()
        if with_token:
            token = refs[7 * n]
            token[...] = jnp.zeros_like(token)

    out_shape = list(_SEM_SHAPES) * n
    out_shape += [pltpu.HBM(a.shape, a.dtype) for a in srcs] + [pltpu.HBM(a.shape, a.dtype) for a in lands]
    out_specs = [_SEM] * (3 * n) + [_HBM] * (2 * n)
    if with_token:
        out_shape.append(_sds((8, 128), F32))
        out_specs.append(pl.BlockSpec(memory_space=pltpu.VMEM))
    outs = _call(
        body, name=name, out_shape=out_shape,
        in_specs=[_HBM] * (2 * n), out_specs=out_specs,
        input_output_aliases={i: 3 * n + i for i in range(2 * n)},
        compiler_params=pltpu.CompilerParams(has_side_effects=_EFFECT),
    )(*[_hbm(a) for a in srcs], *[_hbm(a) for a in lands])
    per_tensor = [(tuple(outs[3 * j:3 * j + 3]), outs[3 * n + j], outs[4 * n + j]) for j in range(n)]
    return (per_tensor, outs[5 * n]) if with_token else per_tensor


def _exchange_wait(kinds, gather, started, after, name):
    n = len(kinds)

    def body(*refs):
        ins, sems = refs[:2 * n], refs[2 * n:5 * n]
        for j, kd in enumerate(kinds):
            local, sends, arrivals = _exchange_copies(kd, gather, ins[j], ins[n + j], sems[3 * j:3 * j + 3])
            local.wait()
            for cp in arrivals:
                cp.wait_recv()
            for cp in sends:
                cp.wait_send()

    srcs = [st[1] for st in started]
    lands = [st[2] for st in started]
    sems = [sm for st in started for sm in st[0]]
    outs = _call(
        body, name=name,
        out_shape=[pltpu.HBM(a.shape, a.dtype) for a in srcs] + [pltpu.HBM(a.shape, a.dtype) for a in lands],
        in_specs=[_HBM] * (2 * n) + [_SEM] * (3 * n) + [pl.BlockSpec(memory_space=pl.ANY)],
        out_specs=[_HBM] * (2 * n),
        input_output_aliases={i: i for i in range(2 * n)},
        compiler_params=pltpu.CompilerParams(has_side_effects=_EFFECT),
    )(*srcs, *lands, *sems, after)
    return outs[n:]


def _mod_fwd(c_all, w_ada, b_sl):
    def body(c_ref, w_ref, b_ref, mod_ref, cact_ref):
        cv = c_ref[...]
        ca = cv * _sigmoid(cv)
        cact_ref[...] = ca
        mod_ref[0] = _dot(ca.astype(BF16), w_ref[0].astype(BF16)) + b_ref[0]

    return _call(
        body, name="mod_fwd", grid=(2,),
        in_specs=[pl.BlockSpec((NDEV, D), lambda l: (0, 0)), pl.BlockSpec((1, D, 768), lambda l: (l, 0, 0)),
                  pl.BlockSpec((1, 1, 768), lambda l: (l, 0, 0))],
        out_specs=[pl.BlockSpec((1, NDEV, 768), lambda l: (l, 0, 0)), pl.BlockSpec((NDEV, D), lambda l: (0, 0))],
        out_shape=[_sds((2, NDEV, 768), F32), _sds((NDEV, D), F32)],
        compiler_params=_params(1),
    )(c_all, w_ada, b_sl)


def _nm_matmul(x, shift, scale, w, *, two_d, n, groups, out_dtype, tm, name):
    s = x.shape[0]

    def body(x_ref, sh_ref, sc_ref, w_ref, h_ref, o_ref):
        @pl.when(pl.program_id(1) == 0)
        def _():
            xv = x_ref[...]
            h_ref[...] = ((xv * _rms(xv)) * (1.0 + sc_ref[...]) + sh_ref[...]).astype(BF16)
        wv = w_ref[...] if two_d else w_ref[0]
        o_ref[0] = _dot(h_ref[...], wv).astype(out_dtype)

    vec = pl.BlockSpec((1, D), lambda i, g: (0, 0))
    w_spec = pl.BlockSpec((D, n), lambda i, g: (0, g)) if two_d else pl.BlockSpec((1, D, n), lambda i, g: (g, 0, 0))
    return _call(
        body, name=name, grid=(s // tm, groups),
        in_specs=[pl.BlockSpec((tm, D), lambda i, g: (i, 0)), vec, vec, w_spec],
        out_specs=[pl.BlockSpec((tm, D), lambda i, g: (i, 0)), pl.BlockSpec((1, tm, n), lambda i, g: (g, i, 0))],
        out_shape=[_sds((s, D), BF16), _sds((groups, s, n), out_dtype)],
        compiler_params=_params(2),
    )(x, shift, scale, w)


def _bias_table(rel_bias):
    far = jnp.broadcast_to(rel_bias[:, 2 * REL_CLIP:], (NDEV, N_FAR))
    near = rel_bias[:, 2 * REL_CLIP - 1:REL_CLIP - CHUNK:-1]
    fpad = jnp.concatenate([far, near, jnp.zeros((NDEV, TABW - N_FAR - N_NEAR), F32)], axis=1)

    def body(f_ref, o_ref):
        t = pltpu.roll(jnp.broadcast_to(f_ref[0], (TQA, TABW)), 0, 1, stride=1, stride_axis=0)[:, TAB0:]
        rows = lax.broadcasted_iota(jnp.int32, (TQA, KWA), 0)
        cols = lax.broadcasted_iota(jnp.int32, (TQA, KWA), 1)
        first = jnp.bitwise_and(rows, -CHUNK)
        o_ref[0] = jnp.where((cols >= first) & (cols < first + BAND), t, NEG)

    return _call(
        body, name="bias_table", grid=(NDEV,),
        in_specs=[pl.BlockSpec((1, 1, TABW), lambda h: (h, 0, 0))],
        out_specs=pl.BlockSpec((1, TQA, KWA), lambda h: (h, 0, 0)),
        out_shape=_sds((NDEV, TQA, KWA), F32),
        compiler_params=_params(1),
    )(fpad.reshape(NDEV, 1, TABW))


def _attn_a_cases(qi, tile):
    @pl.when(qi == 0)
    def _():
        tile(TQA, 2 * TQA, 0)

    @pl.when(qi == 1)
    def _():
        tile(2 * TQA, TQA, 0)

    @pl.when(qi >= 2)
    def _():
        tile(KWA, 0, pl.multiple_of((qi - 2) * TQA, TQA))


def _attn_a_specs(s):
    q_spec = pl.BlockSpec((1, TQA, 128), lambda hp, qi: (0, qi, hp))
    k_spec = pl.BlockSpec((1, s, 128), lambda hp, qi: (1, 0, hp))
    v_spec = pl.BlockSpec((1, s, 128), lambda hp, qi: (2, 0, hp))
    b_spec = pl.BlockSpec((2, TQA, KWA), lambda hp, qi: (hp, 0, 0))
    return q_spec, k_spec, v_spec, b_spec


def _attn_a_fwd(proj, bias_tab):
    s = proj.shape[1]

    def body(q_ref, k_ref, v_ref, b_ref, o_ref):
        def tile(nk, off, kstart):
            for h in range(2):
                sl = slice(HD * h, HD * (h + 1))
                q = q_ref[0, :, sl]
                k = k_ref[0, pl.ds(kstart, nk), sl]
                v = v_ref[0, pl.ds(kstart, nk), sl]
                sc = _dot_nt(q, k) * 0.125 + b_ref[h, :, off:off + nk]
                p = jnp.exp(sc - jnp.max(sc, axis=-1, keepdims=True))
                den = jnp.sum(p, axis=-1, keepdims=True)
                o_ref[:, sl] = _dot(p.astype(BF16), v) / den

        _attn_a_cases(pl.program_id(1), tile)

    q_spec, k_spec, v_spec, b_spec = _attn_a_specs(s)
    return _call(
        body, name="attn_a_fwd", grid=(4, s // TQA),
        in_specs=[q_spec, k_spec, v_spec, b_spec],
        out_specs=pl.BlockSpec((TQA, 128), lambda hp, qi: (qi, hp)),
        out_shape=_sds((s, WG), F32),
        compiler_params=_params(2),
    )(proj, proj, proj, bias_tab)


def _sb_logits(q, k):
    lg = _dot_nt(q, k) * 0.125
    sp = jnp.maximum(lg, 0.0) + jnp.log(1.0 + jnp.exp(-jnp.abs(lg)))
    return lg - sp, -sp


def _attn_b_specs(s):
    q_spec = pl.BlockSpec((1, TQB, 128), lambda hp, qi: (3, qi, hp))
    k_spec = pl.BlockSpec((1, s, 128), lambda hp, qi: (4, 0, hp))
    v_spec = pl.BlockSpec((1, s, 128), lambda hp, qi: (5, 0, hp))
    return q_spec, k_spec, v_spec


def _attn_b_fwd(proj):
    s = proj.shape[1]
    ndiag = TQB // TKB

    def body(q_ref, k_ref, v_ref, o_ref, ls_ref):
        q0 = pl.program_id(1) * TQB
        rows = lax.broadcasted_iota(jnp.int32, (TQB, TKB), 0)
        cols = lax.broadcasted_iota(jnp.int32, (TQB, TKB), 1)
        uj = lax.broadcasted_iota(jnp.int32, (TKB, TKB), 0)
        us = lax.broadcasted_iota(jnp.int32, (TKB, TKB), 1)
        suffix = jnp.where(uj >= us, 1.0, 0.0).astype(BF16)
        heads = [slice(HD * h, HD * (h + 1)) for h in range(2)]
        qs = [q_ref[0, :, sl] for sl in heads]

        def tile(h, kstart, diag, carry, acc):
            k = k_ref[0, pl.ds(kstart, TKB), heads[h]]
            v = v_ref[0, pl.ds(kstart, TKB), heads[h]]
            lb, lk = _sb_logits(qs[h], k)
            if diag is not None:
                strict = rows > cols + diag * TKB
                lk = jnp.where(strict, lk, 0.0)
            csum = _dot2(lk, suffix) + carry
            w = jnp.exp(lb + csum - lk)
            if diag is not None:
                w = jnp.where(strict, w, 0.0)
            return csum[:, 0:1], acc + _dot(w.astype(BF16), v)

        state = [jnp.zeros((TQB, 1), F32), jnp.zeros((TQB, HD), F32)] * 2
        for d in range(ndiag - 1, -1, -1):
            for h in range(2):
                state[2 * h:2 * h + 2] = tile(h, pl.multiple_of(q0 + d * TKB, TKB), d, *state[2 * h:2 * h + 2])
        nsteps = q0 // (KSTEP * TKB)

        def step(i, st):
            st = list(st)
            base = (nsteps - 1 - i) * (KSTEP * TKB)
            for sub in range(KSTEP - 1, -1, -1):
                for h in range(2):
                    st[2 * h:2 * h + 2] = tile(h, pl.multiple_of(base + sub * TKB, TKB), None, *st[2 * h:2 * h + 2])
            return tuple(st)

        state = lax.fori_loop(0, nsteps, step, tuple(state))
        for h in range(2):
            o_ref[:, heads[h]] = state[2 * h + 1]
            ls_ref[0, :, heads[h]] = jnp.broadcast_to(state[2 * h], (TQB, HD))

    q_spec, k_spec, v_spec = _attn_b_specs(s)
    return _call(
        body, name="attn_b_fwd", grid=(4, s // TQB),
        in_specs=[q_spec, k_spec, v_spec],
        out_specs=[pl.BlockSpec((TQB, 128), lambda hp, qi: (qi, hp)),
                   pl.BlockSpec((1, TQB, 128), lambda hp, qi: (hp, qi, 0))],
        out_shape=[_sds((s, WG), F32), _sds((4, s, 128), F32)],
        compiler_params=_params(2),
    )(proj, proj, proj)


def _mix_out(oa, ob, g, w_out, x, gate, tm):
    s = x.shape[0]

    def body(oa_ref, ob_ref, g_ref, w_ref, x_ref, gate_ref, nab_ref, mixed_ref, x2_ref):
        a, b = oa_ref[...], ob_ref[...]
        nab_ref[:, :WG] = (a * _rms(a) * g_ref[:, :WG]).astype(BF16)
        nab_ref[:, WG:] = (b * _rms(b) * g_ref[:, WG:]).astype(BF16)
        mixed = _dot(nab_ref[...], w_ref[...])
        mixed_ref[...] = mixed
        x2_ref[...] = x_ref[...] + gate_ref[...] * mixed

    row = pl.BlockSpec((tm, D), lambda i: (i, 0))
    half = pl.BlockSpec((tm, WG), lambda i: (i, 0))
    vec = pl.BlockSpec((1, D), lambda i: (0, 0))
    return _call(
        body, name="mix_out", grid=(s // tm,),
        in_specs=[half, half, vec, pl.BlockSpec((D, D), lambda i: (0, 0)), row, vec],
        out_specs=[row, row, row],
        out_shape=[_sds((s, D), BF16), _sds((s, D), F32), _sds((s, D), F32)],
        compiler_params=_params(1),
    )(oa, ob, g, w_out, x, gate)


def _conv_taps(u, halo, first, tm):
    rows = lax.broadcasted_iota(jnp.int32, (tm, 1), 0)
    keep = jnp.where(first, 0.0, 1.0)
    h1 = halo[7:8, :] * keep
    h2 = halo[6:7, :] * keep
    um1 = jnp.where(rows == 0, h1, pltpu.roll(u, 1, 0))
    um2 = jnp.where(rows == 0, h2, jnp.where(rows == 1, h1, pltpu.roll(u, 2, 0)))
    return um1, um2


def _conv_specs(tm):
    u_spec = pl.BlockSpec((2, 1, tm, GU), lambda p, i: (0, p, i, 0))
    halo_spec = pl.BlockSpec((2, 1, 8, GU), lambda p, i: (0, p, jnp.maximum(i * (tm // 8) - 1, 0), 0))
    cw_spec = pl.BlockSpec((2, 1, 3, GU), lambda p, i: (0, p, 0, 0))
    cb_spec = pl.BlockSpec((2, 1, 1, GU), lambda p, i: (0, p, 0, 0))
    return u_spec, halo_spec, cw_spec, cb_spec


def _conv_act(u, conv_w, conv_b, tm):
    s = u.shape[2]

    def body(u_ref, halo_ref, cw_ref, cb_ref, a_ref):
        first = pl.program_id(1) == 0
        ys = []
        for side in range(2):
            uv = u_ref[side, 0]
            um1, um2 = _conv_taps(uv, halo_ref[side, 0], first, tm)
            cw = cw_ref[side, 0]
            ys.append(cw[2:3] * uv + cw[1:2] * um1 + cw[0:1] * um2 + cb_ref[side, 0])
        a_ref[0] = (ys[0] * _sigmoid(ys[0]) * ys[1]).astype(BF16)

    u_spec, halo_spec, cw_spec, cb_spec = _conv_specs(tm)
    return _call(
        body, name="conv_act", grid=(4, s // tm),
        in_specs=[u_spec, halo_spec, cw_spec, cb_spec],
        out_specs=pl.BlockSpec((1, tm, GU), lambda p, i: (p, i, 0)),
        out_shape=_sds((4, s, GU), BF16),
        compiler_params=_params(2),
    )(u, u, conv_w, conv_b)


def _down(a, w_down, x2, gate, tm):
    s = x2.shape[0]

    def body(a_ref, w_ref, x_ref, gate_ref, ffn_ref, x3_ref):
        p = pl.program_id(1)
        part = _dot(a_ref[0], w_ref[0])

        @pl.when(p == 0)
        def _():
            ffn_ref[...] = part

        @pl.when(p > 0)
        def _():
            ffn_ref[...] += part

        @pl.when(p == 3)
        def _():
            x3_ref[...] = x_ref[...] + gate_ref[...] * ffn_ref[...]

    row = pl.BlockSpec((tm, D), lambda i, p: (i, 0))
    return _call(
        body, name="down", grid=(s // tm, 4),
        in_specs=[pl.BlockSpec((1, tm, GU), lambda i, p: (p, i, 0)), pl.BlockSpec((1, GU, D), lambda i, p: (p, 0, 0)),
                  row, pl.BlockSpec((1, D), lambda i, p: (0, 0))],
        out_specs=[row, row],
        out_shape=[_sds((s, D), F32), _sds((s, D), F32)],
        compiler_params=_params(2),
    )(a, w_down, x2, gate)


def _final_loss(x, g, target, tm):
    s = x.shape[0]

    def body(x_ref, g_ref, t_ref, loss_ref, dx_ref, dg_ref):
        @pl.when(pl.program_id(0) == 0)
        def _():
            loss_ref[...] = jnp.zeros_like(loss_ref)
            dg_ref[...] = jnp.zeros_like(dg_ref)
        xv = x_ref[...]
        r = _rms(xv)
        nrm = xv * r
        err = nrm * g_ref[...] - t_ref[...]
        loss_ref[...] += (0.5 / D) * jnp.sum(jnp.sum(err * err, axis=-1, keepdims=True), axis=0, keepdims=True)
        dy = err * (1.0 / D)
        dg_ref[...] += _colsum(dy * nrm)
        dx_ref[...] = _rms_bwd(dy * g_ref[...], nrm, r)

    row = pl.BlockSpec((tm, D), lambda i: (i, 0))
    vec = pl.BlockSpec((1, D), lambda i: (0, 0))
    return _call(
        body, name="final_loss", grid=(s // tm,),
        in_specs=[row, vec, row],
        out_specs=[pl.BlockSpec((1, 1), lambda i: (0, 0)), row, vec],
        out_shape=[_sds((1, 1), F32), _sds((s, D), F32), _sds((1, D), F32)],
        compiler_params=_params(1),
    )(x, g, target)


def _down_bwd(dx3, gate, ffn, w_down, tm):
    s = dx3.shape[0]

    def body(dx_ref, gate_ref, ffn_ref, w_ref, dgate_ref, dff_ref, da_ref):
        i, p = pl.program_id(0), pl.program_id(1)

        @pl.when((i == 0) & (p == 0))
        def _():
            dgate_ref[...] = jnp.zeros_like(dgate_ref)

        @pl.when(p == 0)
        def _():
            dxv = dx_ref[...]
            dgate_ref[...] += _colsum(dxv * ffn_ref[...])
            dff_ref[...] = (dxv * gate_ref[...]).astype(BF16)

        da_ref[0] = _dot_nt(dff_ref[...], w_ref[0])

    row = pl.BlockSpec((tm, D), lambda i, p: (i, 0))
    vec = pl.BlockSpec((1, D), lambda i, p: (0, 0))
    return _call(
        body, name="down_bwd", grid=(s // tm, 4),
        in_specs=[row, vec, row, pl.BlockSpec((1, GU, D), lambda i, p: (p, 0, 0))],
        out_specs=[vec, row, pl.BlockSpec((1, tm, GU), lambda i, p: (p, i, 0))],
        out_shape=[_sds((1, D), F32), _sds((s, D), BF16), _sds((4, s, GU), F32)],
        compiler_params=_params(2),
    )(dx3, gate, ffn, w_down)


def _conv_act_bwd(u, conv_w, conv_b, da, tm):
    s = u.shape[2]

    def body(u_ref, halo_ref, cw_ref, cb_ref, da_ref, dy_ref, dcw_ref, dcb_ref):
        first = pl.program_id(1) == 0

        @pl.when(first)
        def _():
            dcw_ref[...] = jnp.zeros_like(dcw_ref)
            dcb_ref[...] = jnp.zeros_like(dcb_ref)

        taps, ys = [], []
        for side in range(2):
            uv = u_ref[side, 0]
            um1, um2 = _conv_taps(uv, halo_ref[side, 0], first, tm)
            cw = cw_ref[side, 0]
            taps.append((um2, um1, uv))
            ys.append(cw[2:3] * uv + cw[1:2] * um1 + cw[0:1] * um2 + cb_ref[side, 0])
        dav = da_ref[0]
        sg = _sigmoid(ys[0])
        dys = (dav * ys[1] * (sg * (1.0 + ys[0] * (1.0 - sg))), dav * (ys[0] * sg))
        for side in range(2):
            dy_ref[side, 0] = dys[side]
            dcb_ref[side, 0] += _colsum(dys[side])
            for j in range(3):
                dcw_ref[side, 0, j:j + 1, :] += _colsum(dys[side] * taps[side][j])

    u_spec, halo_spec, cw_spec, cb_spec = _conv_specs(tm)
    return _call(
        body, name="conv_act_bwd", grid=(4, s // tm),
        in_specs=[u_spec, halo_spec, cw_spec, cb_spec, pl.BlockSpec((1, tm, GU), lambda p, i: (p, i, 0))],
        out_specs=[u_spec, cw_spec, cb_spec],
        out_shape=[_sds((2, 4, s, GU), F32), _sds((2, 4, 3, GU), F32), _sds((2, 4, 1, GU), F32)],
        compiler_params=_params(2),
    )(u, u, conv_w, conv_b, da)


def _conv_transpose(dy, conv_w, tm):
    s = dy.shape[1]
    nt = s // tm

    def body(dy_ref, halo_ref, cw_ref, du_ref):
        keep = jnp.where(pl.program_id(1) == nt - 1, 0.0, 1.0)
        dv = dy_ref[0]
        rows = lax.broadcasted_iota(jnp.int32, (tm, 1), 0)
        h0 = halo_ref[0, 0:1, :] * keep
        h1 = halo_ref[0, 1:2, :] * keep
        dp1 = jnp.where(rows == tm - 1, h0, pltpu.roll(dv, tm - 1, 0))
        dp2 = jnp.where(rows == tm - 1, h1, jnp.where(rows == tm - 2, h0, pltpu.roll(dv, tm - 2, 0)))
        cw = cw_ref[0]
        du_ref[0] = (cw[2:3] * dv + cw[1:2] * dp1 + cw[0:1] * dp2).astype(BF16)

    blk = pl.BlockSpec((1, tm, GU), lambda g, i: (g, i, 0))
    return _call(
        body, name="conv_transpose", grid=(NDEV, nt),
        in_specs=[blk, pl.BlockSpec((1, 8, GU), lambda g, i: (g, jnp.minimum((i + 1) * (tm // 8), s // 8 - 1), 0)),
                  pl.BlockSpec((1, 3, GU), lambda g, i: (g, 0, 0))],
        out_specs=blk,
        out_shape=_sds((NDEV, s, GU), BF16),
        compiler_params=_params(2),
    )(dy, dy, conv_w)


def _wgrad(a3, b3, *, tk, name):
    ga, s, ka = a3.shape
    gb, _, nb = b3.shape
    groups = max(ga, gb)
    nk = s // tk

    def body(a_ref, b_ref, o_ref, acc):
        k = pl.program_id(1)

        @pl.when(k == 0)
        def _():
            acc[...] = jnp.zeros_like(acc)

        acc[...] += _dot_tn(a_ref[0], b_ref[0])

        @pl.when(k == nk - 1)
        def _():
            o_ref[0] = acc[...].astype(BF16)

    a_spec = pl.BlockSpec((1, tk, ka), (lambda g, k: (g, k, 0)) if ga > 1 else (lambda g, k: (0, k, 0)))
    b_spec = pl.BlockSpec((1, tk, nb), (lambda g, k: (g, k, 0)) if gb > 1 else (lambda g, k: (0, k, 0)))
    return _call(
        body, name=name, grid=(groups, nk),
        in_specs=[a_spec, b_spec], out_specs=pl.BlockSpec((1, ka, nb), lambda g, k: (g, 0, 0)),
        out_shape=_sds((groups, ka, nb), BF16),
        scratch_shapes=[pltpu.VMEM((ka, nb), F32)],
        compiler_params=_params(2),
    )(a3, b3)


def _wgrad_in(h1, dparts, tk):
    s = h1.shape[0]
    nk = s // tk

    def body(a_ref, *refs):
        d_refs, o_ref, acc = refs[:NT], refs[NT], refs[NT + 1]
        k = pl.program_id(0)

        @pl.when(k == 0)
        def _():
            acc[...] = jnp.zeros_like(acc)

        for j in range(NT):
            acc[:, WG * j:WG * (j + 1)] += _dot_tn(a_ref[...], d_refs[j][...])

        @pl.when(k == nk - 1)
        def _():
            o_ref[...] = acc[...].astype(BF16)

    return _call(
        body, name="wgrad_in", grid=(nk,),
        in_specs=[pl.BlockSpec((tk, D), lambda k: (k, 0))] + [pl.BlockSpec((tk, WG), lambda k: (k, 0))] * NT,
        out_specs=pl.BlockSpec((D, NT * WG), lambda k: (0, 0)),
        out_shape=_sds((D, NT * WG), BF16),
        scratch_shapes=[pltpu.VMEM((D, NT * WG), F32)],
        compiler_params=_params(1),
    )(h1, *dparts)


def _dgrad_in(dparts, w, x_in, scale, dx_up, tm):
    s = x_in.shape[0]

    def body(*refs):
        d_refs = refs[:NT]
        w_ref, x_ref, sc_ref, up_ref, dx_ref, dsc_ref, dsh_ref = refs[NT:]

        @pl.when(pl.program_id(0) == 0)
        def _():
            dsc_ref[...] = jnp.zeros_like(dsc_ref)
            dsh_ref[...] = jnp.zeros_like(dsh_ref)

        dh = _dot_nt(d_refs[0][...], w_ref[:, 0:WG])
        for j in range(1, NT):
            dh = dh + _dot_nt(d_refs[j][...], w_ref[:, WG * j:WG * (j + 1)])
        xv = x_ref[...]
        r = _rms(xv)
        nrm = xv * r
        dsh_ref[...] += _colsum(dh)
        dsc_ref[...] += _colsum(dh * nrm)
        dx_ref[...] = up_ref[...] + _rms_bwd(dh * (1.0 + sc_ref[...]), nrm, r)

    row = pl.BlockSpec((tm, D), lambda i: (i, 0))
    vec = pl.BlockSpec((1, D), lambda i: (0, 0))
    return _call(
        body, name="dgrad_in", grid=(s // tm,),
        in_specs=[pl.BlockSpec((tm, WG), lambda i: (i, 0))] * NT + [pl.BlockSpec((D, NT * WG), lambda i: (0, 0)), row, vec, row],
        out_specs=[row, vec, vec],
        out_shape=[_sds((s, D), F32), _sds((1, D), F32), _sds((1, D), F32)],
        compiler_params=_params(1),
    )(*dparts, w, x_in, scale, dx_up)


def _dgrad_norm_bwd(d3, w, x_in, scale, dx_up, *, tm, name):
    groups, s, n = d3.shape

    def body(d_ref, w_ref, x_ref, sc_ref, up_ref, dx_ref, dsc_ref, dsh_ref, acc):
        i, g = pl.program_id(0), pl.program_id(1)

        @pl.when((i == 0) & (g == 0))
        def _():
            dsc_ref[...] = jnp.zeros_like(dsc_ref)
            dsh_ref[...] = jnp.zeros_like(dsh_ref)

        part = _dot_nt(d_ref[0], w_ref[0])

        @pl.when(g == 0)
        def _():
            acc[...] = part

        @pl.when(g > 0)
        def _():
            acc[...] += part

        @pl.when(g == groups - 1)
        def _():
            dh = acc[...]
            xv = x_ref[...]
            r = _rms(xv)
            nrm = xv * r
            dsh_ref[...] += _colsum(dh)
            dsc_ref[...] += _colsum(dh * nrm)
            dx_ref[...] = up_ref[...] + _rms_bwd(dh * (1.0 + sc_ref[...]), nrm, r)

    row = pl.BlockSpec((tm, D), lambda i, g: (i, 0))
    vec = pl.BlockSpec((1, D), lambda i, g: (0, 0))
    return _call(
        body, name=name, grid=(s // tm, groups),
        in_specs=[pl.BlockSpec((1, tm, n), lambda i, g: (g, i, 0)), pl.BlockSpec((1, D, n), lambda i, g: (g, 0, 0)),
                  row, vec, row],
        out_specs=[row, vec, vec],
        out_shape=[_sds((s, D), F32), _sds((1, D), F32), _sds((1, D), F32)],
        scratch_shapes=[pltpu.VMEM((tm, D), F32)],
        compiler_params=_params(2),
    )(d3, w, x_in, scale, dx_up)


def _mix_out_bwd(dx2, mixed, gate, w_out, oa, ob, g, tm):
    s = dx2.shape[0]

    def body(dx_ref, mixed_ref, gate_ref, w_ref, oa_ref, ob_ref, g_ref, dgate_ref, dm_ref, doa_ref, dob_ref, dg_ref):
        @pl.when(pl.program_id(0) == 0)
        def _():
            dgate_ref[...] = jnp.zeros_like(dgate_ref)
            dg_ref[...] = jnp.zeros_like(dg_ref)
        dxv = dx_ref[...]
        dgate_ref[...] += _colsum(dxv * mixed_ref[...])
        dm_ref[...] = (dxv * gate_ref[...]).astype(BF16)
        dnab = _dot_nt(dm_ref[...], w_ref[...])
        for o_ref, do_ref, sl in ((oa_ref, doa_ref, slice(0, WG)), (ob_ref, dob_ref, slice(WG, D))):
            ov = o_ref[...]
            r = _rms(ov)
            nrm = ov * r
            dn = dnab[:, sl]
            dg_ref[:, sl] += _colsum(dn * nrm)
            do_ref[...] = _rms_bwd(dn * g_ref[:, sl], nrm, r)

    row = pl.BlockSpec((tm, D), lambda i: (i, 0))
    half = pl.BlockSpec((tm, WG), lambda i: (i, 0))
    vec = pl.BlockSpec((1, D), lambda i: (0, 0))
    return _call(
        body, name="mix_out_bwd", grid=(s // tm,),
        in_specs=[row, row, vec, pl.BlockSpec((D, D), lambda i: (0, 0)), half, half, vec],
        out_specs=[vec, row, half, half, vec],
        out_shape=[_sds((1, D), F32), _sds((s, D), BF16), _sds((s, WG), F32), _sds((s, WG), F32), _sds((1, D), F32)],
        compiler_params=_params(1),
    )(dx2, mixed, gate, w_out, oa, ob, g)


def _attn_a_bwd(proj, bias_tab, d_oa):
    s = proj.shape[1]
    nq = s // TQA

    def body(q_ref, k_ref, v_ref, b_ref, do_ref, dq_ref, dk_ref, dv_ref, db_ref, dk_acc, dv_acc):
        qi = pl.program_id(1)

        @pl.when(qi == 0)
        def _():
            dk_acc[...] = jnp.zeros_like(dk_acc)
            dv_acc[...] = jnp.zeros_like(dv_acc)
            db_ref[...] = jnp.zeros_like(db_ref)

        def tile(nk, off, kstart):
            for h in range(2):
                sl = slice(HD * h, HD * (h + 1))
                q = q_ref[0, :, sl]
                k = k_ref[0, pl.ds(kstart, nk), sl]
                v = v_ref[0, pl.ds(kstart, nk), sl]
                do = do_ref[:, sl].astype(BF16)
                sc = _dot_nt(q, k) * 0.125 + b_ref[h, :, off:off + nk]
                p = jnp.exp(sc - jnp.max(sc, axis=-1, keepdims=True))
                p = p / jnp.sum(p, axis=-1, keepdims=True)
                dp = _dot_nt(do, v)
                ds = p * (dp - jnp.sum(dp * p, axis=-1, keepdims=True))
                db_ref[h, :, off:off + nk] += ds
                dsb = (ds * 0.125).astype(BF16)
                dq_ref[:, sl] = _dot(dsb, k).astype(BF16)
                dk_acc[pl.ds(kstart, nk), sl] += _dot_tn(dsb, q)
                dv_acc[pl.ds(kstart, nk), sl] += _dot_tn(p.astype(BF16), do)

        _attn_a_cases(qi, tile)

        @pl.when(qi == nq - 1)
        def _():
            dk_ref[...] = dk_acc[...].astype(BF16)
            dv_ref[...] = dv_acc[...].astype(BF16)

    q_spec, k_spec, v_spec, b_spec = _attn_a_specs(s)
    blk = pl.BlockSpec((TQA, 128), lambda hp, qi: (qi, hp))
    col = pl.BlockSpec((s, 128), lambda hp, qi: (0, hp))
    return _call(
        body, name="attn_a_bwd", grid=(4, nq),
        in_specs=[q_spec, k_spec, v_spec, b_spec, blk],
        out_specs=[blk, col, col, b_spec],
        out_shape=[_sds((s, WG), BF16), _sds((s, WG), BF16), _sds((s, WG), BF16), _sds((NDEV, TQA, KWA), F32)],
        scratch_shapes=[pltpu.VMEM((s, 128), F32), pltpu.VMEM((s, 128), F32)],
        compiler_params=_params(2),
    )(proj, proj, proj, bias_tab, d_oa)


def _attn_b_bwd(proj, lsum, d_ob):
    s = proj.shape[1]
    nq = s // TQB
    ndiag = TQB // TKB

    def body(q_ref, k_ref, v_ref, ls_ref, do_ref, dq_ref, dk_ref, dv_ref, dk_acc, dv_acc):
        qi = pl.program_id(1)
        q0 = qi * TQB

        @pl.when(qi == 0)
        def _():
            dk_acc[...] = jnp.zeros_like(dk_acc)
            dv_acc[...] = jnp.zeros_like(dv_acc)

        rows = lax.broadcasted_iota(jnp.int32, (TQB, TKB), 0)
        cols = lax.broadcasted_iota(jnp.int32, (TQB, TKB), 1)
        uj = lax.broadcasted_iota(jnp.int32, (TKB, TKB), 0)
        us = lax.broadcasted_iota(jnp.int32, (TKB, TKB), 1)
        prefix = jnp.where(uj <= us, 1.0, 0.0).astype(BF16)
        heads = [slice(HD * h, HD * (h + 1)) for h in range(2)]
        qs = [q_ref[0, :, sl] for sl in heads]
        dos = [do_ref[:, sl].astype(BF16) for sl in heads]
        ltots = [ls_ref[0, :, HD * h:HD * h + 1] for h in range(2)]

        def tile(h, kstart, diag, cl, cg, dq):
            sl = heads[h]
            k = k_ref[0, pl.ds(kstart, TKB), sl]
            v = v_ref[0, pl.ds(kstart, TKB), sl]
            lb, lk = _sb_logits(qs[h], k)
            if diag is not None:
                strict = rows > cols + diag * TKB
                lk = jnp.where(strict, lk, 0.0)
            pre = _dot2(lk, prefix) + cl
            a = jnp.exp(lb + ltots[h] - pre)
            if diag is not None:
                a = jnp.where(strict, a, 0.0)
            gz = _dot_nt(dos[h], v) * a
            pg = _dot2(gz, prefix) + cg
            sig = jnp.exp(lb)
            dl = gz * (1.0 - sig) - (pg - gz) * sig
            if diag is not None:
                dl = jnp.where(strict, dl, 0.0)
            dlb = (dl * 0.125).astype(BF16)
            dk_acc[pl.ds(kstart, TKB), sl] += _dot_tn(dlb, qs[h])
            dv_acc[pl.ds(kstart, TKB), sl] += _dot_tn(a.astype(BF16), dos[h])
            return pre[:, TKB - 1:TKB], pg[:, TKB - 1:TKB], dq + _dot(dlb, k)

        state = [jnp.zeros((TQB, 1), F32), jnp.zeros((TQB, 1), F32), jnp.zeros((TQB, HD), F32)] * 2

        def step(i, st):
            st = list(st)
            base = i * (KSTEP * TKB)
            for sub in range(KSTEP):
                for h in range(2):
                    st[3 * h:3 * h + 3] = tile(h, pl.multiple_of(base + sub * TKB, TKB), None, *st[3 * h:3 * h + 3])
            return tuple(st)

        state = list(lax.fori_loop(0, q0 // (KSTEP * TKB), step, tuple(state)))
        for d in range(ndiag):
            for h in range(2):
                state[3 * h:3 * h + 3] = tile(h, pl.multiple_of(q0 + d * TKB, TKB), d, *state[3 * h:3 * h + 3])
        for h in range(2):
            dq_ref[:, heads[h]] = state[3 * h + 2].astype(BF16)

        @pl.when(qi == nq - 1)
        def _():
            dk_ref[...] = dk_acc[...].astype(BF16)
            dv_ref[...] = dv_acc[...].astype(BF16)

    q_spec, k_spec, v_spec = _attn_b_specs(s)
    blk = pl.BlockSpec((TQB, 128), lambda hp, qi: (qi, hp))
    col = pl.BlockSpec((s, 128), lambda hp, qi: (0, hp))
    return _call(
        body, name="attn_b_bwd", grid=(4, nq),
        in_specs=[q_spec, k_spec, v_spec, pl.BlockSpec((1, TQB, 128), lambda hp, qi: (hp, qi, 0)), blk],
        out_specs=[blk, col, col],
        out_shape=[_sds((s, WG), BF16)] * 3,
        scratch_shapes=[pltpu.VMEM((s, 128), F32), pltpu.VMEM((s, 128), F32)],
        compiler_params=_params(2),
    )(proj, proj, proj, lsum, d_ob)


def _bias_fold(dtab):
    def body(t_ref, d_ref, far_ref):
        acc = jnp.zeros((NDEV, TABW), F32)
        zpad = jnp.zeros((NDEV, TAB0), F32)
        for r in range(TQA):
            row = jnp.concatenate([zpad, t_ref[:, r, :]], axis=1)
            acc = acc + (pltpu.roll(row, TABW - r, 1) if r else row)
        d_ref[...] = acc
        lane = lax.broadcasted_iota(jnp.int32, (NDEV, TABW), 1)
        far = jnp.sum(jnp.where(lane < N_FAR, acc, 0.0), axis=1, keepdims=True)
        far_ref[...] = jnp.broadcast_to(far, (NDEV, 128))

    d_fpad, d_far = _call(
        body, name="bias_fold", grid=(1,),
        in_specs=[pl.BlockSpec((NDEV, TQA, KWA), lambda i: (0, 0, 0))],
        out_specs=[pl.BlockSpec((NDEV, TABW), lambda i: (0, 0)), pl.BlockSpec((NDEV, 128), lambda i: (0, 0))],
        out_shape=[_sds((NDEV, TABW), F32), _sds((NDEV, 128), F32)],
        compiler_params=_params(1),
    )(dtab)
    d_near = d_fpad[:, N_FAR:N_FAR + N_NEAR][:, ::-1]
    return jnp.concatenate([jnp.zeros((NDEV, REL_CLIP - CHUNK + 1), F32), d_near, d_far[:, :1]], axis=1)


def _adamw(w, g, m, v):
    m = B1 * m + (1.0 - B1) * g
    v = B2 * v + (1.0 - B2) * (g * g)
    m_hat = m / (1.0 - B1 ** STEP)
    v_hat = v / (1.0 - B2 ** STEP)
    delta = -LR * (m_hat / (jnp.sqrt(v_hat) + AEPS) + WD * w)
    return delta, m, v


def _adamw_big(recv0, recv1, w, m, v):
    _, rows, cols = recv0.shape
    tr = rows if rows <= 512 else 256
    nt = rows // tr

    def body(r0_ref, r1_ref, w_ref, m_ref, v_ref, g_ref, d_ref, nm_ref, nv_ref):
        def update(r_ref):
            g = r_ref[0].astype(F32)
            for p in range(1, NDEV):
                g = g + r_ref[p].astype(F32)
            delta, nm, nv = _adamw(w_ref[0], g, m_ref[0], v_ref[0])
            g_ref[0], d_ref[0], nm_ref[0], nv_ref[0] = g, delta, nm, nv

        @pl.when(pl.program_id(0) == 0)
        def _():
            update(r0_ref)

        @pl.when(pl.program_id(0) == 1)
        def _():
            update(r1_ref)

    blk = pl.BlockSpec((1, tr, cols), lambda l, i: (l, i, 0))
    r0_spec = pl.BlockSpec((NDEV, tr, cols), lambda l, i: (0, jnp.where(l == 0, i, nt - 1), 0))
    r1_spec = pl.BlockSpec((NDEV, tr, cols), lambda l, i: (0, jnp.where(l == 1, i, 0), 0))
    return _call(
        body, name="adamw_big", grid=(2, nt),
        in_specs=[r0_spec, r1_spec, blk, blk, blk],
        out_specs=[blk] * 4,
        out_shape=[_sds((2, rows, cols), F32)] * 4,
        compiler_params=_params(2),
    )(recv0, recv1, w, m, v)


def _adamw_w_ada(cact_t, dmod, w, m, v):
    tr = 256

    def body(c_ref, dm_ref, w_ref, m_ref, v_ref, g_ref, d_ref, nm_ref, nv_ref):
        g = c_ref[:, 0:1] * dm_ref[0, 0:1, :]
        for b in range(1, NDEV):
            g = g + c_ref[:, b:b + 1] * dm_ref[0, b:b + 1, :]
        delta, nm, nv = _adamw(w_ref[0], g, m_ref[0], v_ref[0])
        g_ref[0], d_ref[0], nm_ref[0], nv_ref[0] = g, delta, nm, nv

    blk = pl.BlockSpec((1, tr, 768), lambda l, i: (l, i, 0))
    return _call(
        body, name="adamw_w_ada", grid=(2, D // tr),
        in_specs=[pl.BlockSpec((tr, NDEV), lambda l, i: (i, 0)), pl.BlockSpec((1, NDEV, 768), lambda l, i: (l, 0, 0)),
                  blk, blk, blk],
        out_specs=[blk] * 4,
        out_shape=[_sds((2, D, 768), F32)] * 4,
        compiler_params=_params(2),
    )(cact_t, dmod, w, m, v)


def _adamw_small(gath, w, m, v):
    rows = gath.shape[1]

    def body(r_ref, w_ref, m_ref, v_ref, g_ref, d_ref, nm_ref, nv_ref):
        g = r_ref[0]
        for p in range(1, NDEV):
            g = g + r_ref[p]
        delta, nm, nv = _adamw(w_ref[...], g, m_ref[...], v_ref[...])
        g_ref[...], d_ref[...], nm_ref[...], nv_ref[...] = g, delta, nm, nv

    blk = pl.BlockSpec((rows, D), lambda i: (0, 0))
    return _call(
        body, name="adamw_small", grid=(1,),
        in_specs=[pl.BlockSpec((NDEV, rows, D), lambda i: (0, 0, 0)), blk, blk, blk],
        out_specs=[blk] * 4,
        out_shape=[_sds((rows, D), F32)] * 4,
        compiler_params=_params(1),
    )(gath, w, m, v)


_PACK = (("b_ada", 2 * 6 * D), ("rel_bias", 2 * 8 * 257), ("g_a", 2 * WG), ("g_b", 2 * WG),
         ("conv_b", 2 * 2 * DFF), ("final_g", D), ("conv_w", 2 * NDEV * 3 * GU))


def _pack(parts):
    rows = []
    for name, size in _PACK:
        flat = parts[name].reshape(-1).astype(F32)
        assert flat.shape[0] == size, (name, flat.shape)
        rows.append(jnp.pad(flat, (0, -size % D)))
    out = jnp.concatenate(rows).reshape(-1, D)
    return jnp.pad(out, ((0, -out.shape[0] % 8), (0, 0)))


def _unpack(packed):
    flat = packed.reshape(-1)
    out, pos = {}, 0
    for name, size in _PACK:
        out[name] = flat[pos:pos + size]
        pos += size + (-size % D)
    return out


def kernel(x, c, w_ada, b_ada, w_in, rel_bias, g_a, g_b, w_out, w_up, conv_w, conv_b, w_down, final_g, loss_target, m_w_ada, m_b_ada, m_w_in, m_rel_bias, m_g_a, m_g_b, m_w_out, m_w_up, m_conv_w, m_conv_b, m_w_down, m_final_g, v_w_ada, v_b_ada, v_w_in, v_rel_bias, v_g_a, v_g_b, v_w_out, v_w_up, v_conv_w, v_conv_b, v_w_down, v_final_g):
    s = x.shape[1]
    assert s % TQA == 0 and s >= KWA and s % 512 == 0
    tm = 512
    me = 4 * lax.axis_index("x") + 2 * lax.axis_index("y") + lax.axis_index("c")
    xs = x.reshape(s, D)
    target = loss_target.reshape(s, D)

    shards = {"in": w_in, "out": w_out, "up": w_up, "down": w_down}
    order = [(kind, l) for l in range(2) for kind in _KINDS]
    gather_started, token = _exchange_start([kind for kind, _ in order], True,
                                            [shards[kind][l].astype(BF16) for kind, l in order],
                                            "weights_gather_start", with_token=True)
    gather_started = dict(zip(order, gather_started))

    def gathered(kind, l, after):
        return _exchange_wait([kind], True, [gather_started[kind, l]], after, f"weights_gather_wait_{kind}{l}")[0]

    c_all = _small_allgather(jnp.pad(c + token[0, 0], ((0, 7), (0, 0))), "gather_c")[:, 0, :]
    b_sl = lax.dynamic_slice(b_ada, (0, me * 768), (2, 768)).reshape(2, 1, 768)
    mod_part, cact = _mod_fwd(c_all, w_ada, b_sl)
    mod_all = _small_allgather(mod_part.reshape(2 * NDEV, 768), "gather_mod")
    mod_all = mod_all.reshape(NDEV, 2, NDEV, 768)
    mod_me = lax.dynamic_index_in_dim(mod_all, me, axis=2, keepdims=False)
    mod = jnp.transpose(mod_me, (1, 0, 2)).reshape(2, 6, 1, D)

    cw_all = _small_allgather(jnp.pad(conv_w.reshape(2 * 3, GU), ((0, 2), (0, 0))), "gather_conv_w")
    cw_all = cw_all[:, :6, :].reshape(NDEV, 2, 3, GU)

    saved = []
    xl = xs
    for l in range(2):
        sh_mix, sc_mix, gt_mix, sh_ffn, sc_ffn, gt_ffn = (mod[l, j] for j in range(6))
        cw = cw_all[:, l].reshape(2, 4, 3, GU)
        cb = conv_b[l].reshape(2, 4, 1, GU)
        gvec = jnp.concatenate([g_a[l], g_b[l]]).reshape(1, D)
        tab = _bias_table(rel_bias[l])

        wi = gathered("in", l, xl if l else mod)
        h1, proj = _nm_matmul(xl, sh_mix, sc_mix, wi, two_d=True, n=WG, groups=NT, out_dtype=BF16, tm=tm,
                              name="norm_proj")
        oa = _attn_a_fwd(proj, tab)
        ob, lsum = _attn_b_fwd(proj)
        wo = gathered("out", l, ob)
        nab, mixed, x2 = _mix_out(oa, ob, gvec, wo, xl, gt_mix, tm)
        wu = gathered("up", l, x2)
        h2, u = _nm_matmul(x2, sh_ffn, sc_ffn, wu, two_d=False, n=GU, groups=NDEV, out_dtype=F32, tm=tm,
                           name="norm_up")
        u = u.reshape(2, 4, s, GU)
        a = _conv_act(u, cw, cb, tm)
        wd4 = gathered("down", l, a).reshape(4, GU, D)
        ffn, x3 = _down(a, wd4, x2, gt_ffn, tm)
        saved.append(dict(x=xl, h1=h1, proj=proj, oa=oa, ob=ob, lsum=lsum, nab=nab, mixed=mixed, x2=x2, h2=h2, u=u,
                          a=a, ffn=ffn, cw=cw, cb=cb, gvec=gvec, tab=tab, wd4=wd4, wi=wi, wo=wo, wu=wu))
        xl = x3

    loss_part, dx, d_final_g = _final_loss(xl, final_g.reshape(1, D), target, tm)
    loss = lax.psum(loss_part[0, 0], ("x", "y", "c"))

    sent = {}
    small = {"b_ada": [None, None], "rel_bias": [None, None], "g_a": [None, None], "g_b": [None, None],
             "conv_b": [None, None], "conv_w": [None, None]}

    def send(kind, l, grad):
        sent[kind, l] = _exchange_start([kind], False, [grad], f"grads_start_{kind}{l}")[0]

    for l in (1, 0):
        sv = saved[l]
        sh_mix, sc_mix, gt_mix, sh_ffn, sc_ffn, gt_ffn = (mod[l, j] for j in range(6))
        d_gt_ffn, dff, da = _down_bwd(dx, gt_ffn, sv["ffn"], sv["wd4"], tm)
        send("down", l, _wgrad(sv["a"], dff.reshape(1, s, D), tk=tm, name="wgrad_down").reshape(DFF, D))
        dy, d_cw, d_cb = _conv_act_bwd(sv["u"], sv["cw"], sv["cb"], da, tm)
        du = _conv_transpose(dy.reshape(NDEV, s, GU), sv["cw"].reshape(NDEV, 3, GU), tm)
        send("up", l, _wgrad(sv["h2"].reshape(1, s, D), du, tk=tm, name="wgrad_up"))
        dx2, d_sc_ffn, d_sh_ffn = _dgrad_norm_bwd(du, sv["wu"], sv["x2"], sc_ffn, dx, tm=tm, name="dgrad_up")
        d_gt_mix, dmixed, d_oa, d_ob, d_g = _mix_out_bwd(dx2, sv["mixed"], gt_mix, sv["wo"], sv["oa"], sv["ob"],
                                                         sv["gvec"], tm)
        send("out", l, _wgrad(sv["nab"].reshape(1, s, D), dmixed.reshape(1, s, D), tk=tm,
                              name="wgrad_out").reshape(D, D))
        dqa, dka, dva, d_tab = _attn_a_bwd(sv["proj"], sv["tab"], d_oa)
        dqb, dkb, dvb = _attn_b_bwd(sv["proj"], sv["lsum"], d_ob)
        dparts = (dqa, dka, dva, dqb, dkb, dvb)
        send("in", l, _wgrad_in(sv["h1"], dparts, tm))
        dx, d_sc_mix, d_sh_mix = _dgrad_in(dparts, sv["wi"], sv["x"], sc_mix, dx2, tm)
        small["b_ada"][l] = jnp.concatenate([d_sh_mix, d_sc_mix, d_gt_mix, d_sh_ffn, d_sc_ffn, d_gt_ffn], axis=1)
        small["rel_bias"][l] = _bias_fold(d_tab)
        small["g_a"][l], small["g_b"][l] = d_g[:, :WG], d_g[:, WG:]
        small["conv_b"][l] = d_cb
        small["conv_w"][l] = d_cw.reshape(NDEV, 3, GU)
    grad_x = dx.reshape(1, s, D)

    big = {}
    for kind, (w, m, v) in (("down", (w_down, m_w_down, v_w_down)), ("up", (w_up, m_w_up, v_w_up)),
                            ("out", (w_out, m_w_out, v_w_out)), ("in", (w_in, m_w_in, v_w_in))):
        recv0, recv1 = _exchange_wait([kind, kind], False, [sent[kind, 0], sent[kind, 1]], dx, f"grads_wait_{kind}")
        big[kind] = _adamw_big(recv0, recv1, w, m, v)

    contrib = {k: jnp.stack(vs) for k, vs in small.items()}
    contrib["final_g"] = d_final_g
    gath = _small_allgather(_pack(contrib), "gather_small_grads")

    def place_conv_w(t):
        return lax.dynamic_update_slice(jnp.zeros((2, NDEV, 3, GU), F32), t.reshape(2, 1, 3, GU), (0, me, 0, 0))

    def packed_params(b, rb, ga, gb, cb_, fg, cw_):
        return _pack({"b_ada": b, "rel_bias": rb, "g_a": ga, "g_b": gb, "conv_b": cb_, "final_g": fg,
                      "conv_w": place_conv_w(cw_)})

    sm = _adamw_small(gath,
                      packed_params(b_ada, rel_bias, g_a, g_b, conv_b, final_g, conv_w),
                      packed_params(m_b_ada, m_rel_bias, m_g_a, m_g_b, m_conv_b, m_final_g, m_conv_w),
                      packed_params(v_b_ada, v_rel_bias, v_g_a, v_g_b, v_conv_b, v_final_g, v_conv_w))
    sm = [_unpack(t) for t in sm]

    dmod_all = gath[:, :12, :].reshape(NDEV, 2, 6 * D)
    dmod_sl = jnp.transpose(lax.dynamic_slice(dmod_all, (0, 0, me * 768), (NDEV, 2, 768)), (1, 0, 2))
    ada = _adamw_w_ada(cact.T, dmod_sl, w_ada, m_w_ada, v_w_ada)

    def small_out(j, name):
        t = sm[j][name]
        if name == "b_ada":
            return t.reshape(2, 6 * D)
        if name == "rel_bias":
            return t.reshape(2, 8, 257)
        if name in ("g_a", "g_b"):
            return t.reshape(2, WG)
        if name == "conv_b":
            return t.reshape(2, 2 * DFF)
        if name == "final_g":
            return t.reshape(D)
        t = t.reshape(2, NDEV, 3, GU)
        return lax.dynamic_index_in_dim(t, me, axis=1, keepdims=False)

    def group(j):
        return (ada[j], small_out(j, "b_ada"), big["in"][j], small_out(j, "rel_bias"), small_out(j, "g_a"),
                small_out(j, "g_b"), big["out"][j], big["up"][j], small_out(j, "conv_w"), small_out(j, "conv_b"),
                big["down"][j], small_out(j, "final_g"))

    return (loss, grad_x, *group(0), *group(1), *group(2), *group(3))
```

```python
import jax
import jax.numpy as jnp
from jax import lax
from jax.experimental import pallas as pl
from jax.experimental.pallas import tpu as pltpu

F32, BF16 = jnp.float32, jnp.bfloat16
MESH_ID = pl.DeviceIdType.MESH
NDEV = 8
D = 1024
HD = 64
WG = 512
NT = 6
GU = 704
DFF = 2816
CHUNK, NPREV, REL_CLIP = 64, 8, 128
BAND = (NPREV + 1) * CHUNK
EPS = 1e-6
NEG = -1e30
TQA = 256
KWA = TQA + NPREV * CHUNK
TABW = 1024
TAB0 = TABW - KWA
N_FAR = TAB0 + NPREV * CHUNK - REL_CLIP + 1
N_NEAR = REL_CLIP + CHUNK - 1
TQB, TKB = 512, 128
KSTEP = 4
LR, B1, B2, AEPS, WD, STEP = 0.001, 0.9, 0.999, 1e-08, 0.01, 10
VMEM_MB = 56


def _call(body, **kw):
    return pl.pallas_call(body, **kw)


def _params(n_axes):
    return pltpu.CompilerParams(dimension_semantics=("arbitrary",) * n_axes, vmem_limit_bytes=VMEM_MB << 20)


def _dot(a, b):
    return jnp.dot(a, b, preferred_element_type=F32)


def _dot_nt(a, b):
    return lax.dot_general(a, b, (((1,), (1,)), ((), ())), preferred_element_type=F32)


def _dot_tn(a, b):
    return lax.dot_general(a, b, (((0,), (0,)), ((), ())), preferred_element_type=F32)


def _dot2(x, u):
    hi = x.astype(BF16)
    lo = (x - hi.astype(F32)).astype(BF16)
    return _dot(hi, u) + _dot(lo, u)


def _rms(x):
    return lax.rsqrt(jnp.mean(x * x, axis=-1, keepdims=True) + EPS)


def _rms_bwd(dn, n, r):
    return r * (dn - n * jnp.mean(dn * n, axis=-1, keepdims=True))


def _colsum(x):
    return jnp.sum(x, axis=0, keepdims=True)


def _sigmoid(x):
    return 1.0 / (1.0 + jnp.exp(-x))


def _sds(shape, dtype):
    return jax.ShapeDtypeStruct(shape, dtype)


def _place():
    x, y, c = lax.axis_index("x"), lax.axis_index("y"), lax.axis_index("c")
    return x, y, c, 4 * x + 2 * y + c


def _peer(x, y, c, k):
    px = 1 - x if k & 4 else x
    py = 1 - y if k & 2 else y
    pc = 1 - c if k & 1 else c
    return (px, py, pc), 4 * px + 2 * py + pc


def _small_allgather(v, name):
    rows, cols = v.shape

    def body(v_ref, out_ref, send_sems, recv_sems, local_sem):
        x, y, c, me = _place()
        mine = pltpu.make_async_copy(v_ref, out_ref.at[me], local_sem)
        mine.start()
        sends = []
        for k in range(1, NDEV):
            peer, _ = _peer(x, y, c, k)
            cp = pltpu.make_async_remote_copy(v_ref, out_ref.at[me], send_sems.at[k - 1], recv_sems.at[k - 1],
                                              device_id=peer, device_id_type=MESH_ID)
            cp.start()
            sends.append(cp)
        for k in range(1, NDEV):
            peer, pidx = _peer(x, y, c, k)
            pltpu.make_async_remote_copy(v_ref, out_ref.at[pidx], send_sems.at[k - 1], recv_sems.at[k - 1],
                                         device_id=peer, device_id_type=MESH_ID).wait_recv()
        for cp in sends:
            cp.wait_send()
        mine.wait()

    return _call(
        body, name=name,
        out_shape=_sds((NDEV, rows, cols), F32),
        in_specs=[pl.BlockSpec(memory_space=pltpu.VMEM)],
        out_specs=pl.BlockSpec(memory_space=pltpu.VMEM),
        scratch_shapes=[pltpu.SemaphoreType.DMA((NDEV - 1,)), pltpu.SemaphoreType.DMA((NDEV - 1,)),
                        pltpu.SemaphoreType.DMA],
    )(v)


def _shard_view(ref, kind, p):
    if kind == "in":
        return ref.at[:, pl.ds(pl.multiple_of(p * 384, 128), 384)]
    if kind == "out":
        return ref.at[pl.ds(pl.multiple_of(p * 128, 128), 128), :]
    if kind == "up":
        return ref.at[p]
    if kind == "down":
        return ref.at[pl.ds(pl.multiple_of(p * 352, 16), 352), :]
    raise ValueError(kind)


_KINDS = ("in", "out", "up", "down")
_FULL_SHAPES = {"in": (D, 3 * D), "out": (D, D), "up": (NDEV, D, GU), "down": (DFF, D)}
_SHARD_SHAPES = {"in": (D, 384), "out": (128, D), "up": (D, GU), "down": (352, D)}


_HBM = pl.BlockSpec(memory_space=pltpu.HBM)
_SEM = pl.BlockSpec(memory_space=pltpu.SEMAPHORE)
_EFFECT = pltpu.SideEffectType.DATAFLOW_SIDE_EFFECTING
_SEM_SHAPES = (pltpu.SemaphoreType.DMA((NDEV - 1,)), pltpu.SemaphoreType.DMA((NDEV - 1,)), pltpu.SemaphoreType.DMA(()))


def _hbm(a):
    return pltpu.with_memory_space_constraint(a, pltpu.HBM)


def _exchange_copies(kind, gather, src, land, sems):
    send_sems, recv_sems, local_sem = sems
    x, y, c, me = _place()

    def ends(p_dst, p_from):
        if gather:
            return src, _shard_view(land, kind, me), _shard_view(land, kind, p_from)
        return _shard_view(src, kind, p_dst), land.at[me], land.at[p_from]

    s_me, d_me, _ = ends(me, me)
    local = pltpu.make_async_copy(s_me, d_me, local_sem)
    sends, arrivals = [], []
    for k in range(1, NDEV):
        peer, pidx = _peer(x, y, c, k)
        s_k, d_k, from_k = ends(pidx, pidx)
        sends.append(pltpu.make_async_remote_copy(s_k, d_k, send_sems.at[k - 1], recv_sems.at[k - 1],
                                                  device_id=peer, device_id_type=MESH_ID))
        arrivals.append(pltpu.make_async_remote_copy(s_k, from_k, send_sems.at[k - 1], recv_sems.at[k - 1],
                                                     device_id=peer, device_id_type=MESH_ID))
    return local, sends, arrivals


def _exchange_start(kinds, gather, srcs, name, with_token=False):
    n = len(kinds)
    lands = [lax.empty(_FULL_SHAPES[kd] if gather else (NDEV,) + _SHARD_SHAPES[kd], BF16) for kd in kinds]

    def body(*refs):
        ins, sems = refs[:2 * n], refs[2 * n:5 * n]
        for j, kd in enumerate(kinds):
            local, sends, _ = _exchange_copies(kd, gather, ins[j], ins[n + j], sems[3 * j:3 * j + 3])
            local.start()
            for cp in sends:
                cp.start()
        if with_token:
            token = refs[7 * n]
            token[...] = jnp.zeros_like(token)

    out_shape = list(_SEM_SHAPES) * n
    out_shape += [pltpu.HBM(a.shape, a.dtype) for a in srcs] + [pltpu.HBM(a.shape, a.dtype) for a in lands]
    out_specs = [_SEM] * (3 * n) + [_HBM] * (2 * n)
    if with_token:
        out_shape.append(_sds((8, 128), F32))
        out_specs.append(pl.BlockSpec(memory_space=pltpu.VMEM))
    outs = _call(
        body, name=name, out_shape=out_shape,
        in_specs=[_HBM] * (2 * n), out_specs=out_specs,
        input_output_aliases={i: 3 * n + i for i in range(2 * n)},
        compiler_params=pltpu.CompilerParams(has_side_effects=_EFFECT),
    )(*[_hbm(a) for a in srcs], *[_hbm(a) for a in lands])
    per_tensor = [(tuple(outs[3 * j:3 * j + 3]), outs[3 * n + j], outs[4 * n + j]) for j in range(n)]
    return (per_tensor, outs[5 * n]) if with_token else per_tensor


def _exchange_wait(kinds, gather, started, after, name):
    n = len(kinds)

    def body(*refs):
        ins, sems = refs[:2 * n], refs[2 * n:5 * n]
        for j, kd in enumerate(kinds):
            local, sends, arrivals = _exchange_copies(kd, gather, ins[j], ins[n + j], sems[3 * j:3 * j + 3])
            local.wait()
            for cp in arrivals:
                cp.wait_recv()
            for cp in sends:
                cp.wait_send()

    srcs = [st[1] for st in started]
    lands = [st[2] for st in started]
    sems = [sm for st in started for sm in st[0]]
    outs = _call(
        body, name=name,
        out_shape=[pltpu.HBM(a.shape, a.dtype) for a in srcs] + [pltpu.HBM(a.shape, a.dtype) for a in lands],
        in_specs=[_HBM] * (2 * n) + [_SEM] * (3 * n) + [pl.BlockSpec(memory_space=pl.ANY)],
        out_specs=[_HBM] * (2 * n),
        input_output_aliases={i: i for i in range(2 * n)},
        compiler_params=pltpu.CompilerParams(has_side_effects=_EFFECT),
    )(*srcs, *lands, *sems, after)
    return outs[n:]


def _mod_fwd(c_all, w_ada, b_sl):
    def body(c_ref, w_ref, b_ref, mod_ref, cact_ref):
        cv = c_ref[...]
        ca = cv * _sigmoid(cv)
        cact_ref[...] = ca
        mod_ref[0] = _dot(ca.astype(BF16), w_ref[0].astype(BF16)) + b_ref[0]

    return _call(
        body, name="mod_fwd", grid=(2,),
        in_specs=[pl.BlockSpec((NDEV, D), lambda l: (0, 0)), pl.BlockSpec((1, D, 768), lambda l: (l, 0, 0)),
                  pl.BlockSpec((1, 1, 768), lambda l: (l, 0, 0))],
        out_specs=[pl.BlockSpec((1, NDEV, 768), lambda l: (l, 0, 0)), pl.BlockSpec((NDEV, D), lambda l: (0, 0))],
        out_shape=[_sds((2, NDEV, 768), F32), _sds((NDEV, D), F32)],
        compiler_params=_params(1),
    )(c_all, w_ada, b_sl)


def _nm_matmul(x, shift, scale, w, *, two_d, n, groups, out_dtype, tm, name):
    s = x.shape[0]

    def body(x_ref, sh_ref, sc_ref, w_ref, h_ref, o_ref):
        @pl.when(pl.program_id(1) == 0)
        def _():
            xv = x_ref[...]
            h_ref[...] = ((xv * _rms(xv)) * (1.0 + sc_ref[...]) + sh_ref[...]).astype(BF16)
        wv = w_ref[...] if two_d else w_ref[0]
        o_ref[0] = _dot(h_ref[...], wv).astype(out_dtype)

    vec = pl.BlockSpec((1, D), lambda i, g: (0, 0))
    w_spec = pl.BlockSpec((D, n), lambda i, g: (0, g)) if two_d else pl.BlockSpec((1, D, n), lambda i, g: (g, 0, 0))
    return _call(
        body, name=name, grid=(s // tm, groups),
        in_specs=[pl.BlockSpec((tm, D), lambda i, g: (i, 0)), vec, vec, w_spec],
        out_specs=[pl.BlockSpec((tm, D), lambda i, g: (i, 0)), pl.BlockSpec((1, tm, n), lambda i, g: (g, i, 0))],
        out_shape=[_sds((s, D), BF16), _sds((groups, s, n), out_dtype)],
        compiler_params=_params(2),
    )(x, shift, scale, w)


def _bias_table(rel_bias):
    far = jnp.broadcast_to(rel_bias[:, 2 * REL_CLIP:], (NDEV, N_FAR))
    near = rel_bias[:, 2 * REL_CLIP - 1:REL_CLIP - CHUNK:-1]
    fpad = jnp.concatenate([far, near, jnp.zeros((NDEV, TABW - N_FAR - N_NEAR), F32)], axis=1)

    def body(f_ref, o_ref):
        t = pltpu.roll(jnp.broadcast_to(f_ref[0], (TQA, TABW)), 0, 1, stride=1, stride_axis=0)[:, TAB0:]
        rows = lax.broadcasted_iota(jnp.int32, (TQA, KWA), 0)
        cols = lax.broadcasted_iota(jnp.int32, (TQA, KWA), 1)
        first = jnp.bitwise_and(rows, -CHUNK)
        o_ref[0] = jnp.where((cols >= first) & (cols < first + BAND), t, NEG)

    return _call(
        body, name="bias_table", grid=(NDEV,),
        in_specs=[pl.BlockSpec((1, 1, TABW), lambda h: (h, 0, 0))],
        out_specs=pl.BlockSpec((1, TQA, KWA), lambda h: (h, 0, 0)),
        out_shape=_sds((NDEV, TQA, KWA), F32),
        compiler_params=_params(1),
    )(fpad.reshape(NDEV, 1, TABW))


def _attn_a_cases(qi, tile):
    @pl.when(qi == 0)
    def _():
        tile(TQA, 2 * TQA, 0)

    @pl.when(qi == 1)
    def _():
        tile(2 * TQA, TQA, 0)

    @pl.when(qi >= 2)
    def _():
        tile(KWA, 0, pl.multiple_of((qi - 2) * TQA, TQA))


def _attn_a_specs(s):
    q_spec = pl.BlockSpec((1, TQA, 128), lambda hp, qi: (0, qi, hp))
    k_spec = pl.BlockSpec((1, s, 128), lambda hp, qi: (1, 0, hp))
    v_spec = pl.BlockSpec((1, s, 128), lambda hp, qi: (2, 0, hp))
    b_spec = pl.BlockSpec((2, TQA, KWA), lambda hp, qi: (hp, 0, 0))
    return q_spec, k_spec, v_spec, b_spec


def _attn_a_fwd(proj, bias_tab):
    s = proj.shape[1]

    def body(q_ref, k_ref, v_ref, b_ref, o_ref):
        def tile(nk, off, kstart):
            for h in range(2):
                sl = slice(HD * h, HD * (h + 1))
                q = q_ref[0, :, sl]
                k = k_ref[0, pl.ds(kstart, nk), sl]
                v = v_ref[0, pl.ds(kstart, nk), sl]
                sc = _dot_nt(q, k) * 0.125 + b_ref[h, :, off:off + nk]
                p = jnp.exp(sc - jnp.max(sc, axis=-1, keepdims=True))
                den = jnp.sum(p, axis=-1, keepdims=True)
                o_ref[:, sl] = _dot(p.astype(BF16), v) / den

        _attn_a_cases(pl.program_id(1), tile)

    q_spec, k_spec, v_spec, b_spec = _attn_a_specs(s)
    return _call(
        body, name="attn_a_fwd", grid=(4, s // TQA),
        in_specs=[q_spec, k_spec, v_spec, b_spec],
        out_specs=pl.BlockSpec((TQA, 128), lambda hp, qi: (qi, hp)),
        out_shape=_sds((s, WG), F32),
        compiler_params=_params(2),
    )(proj, proj, proj, bias_tab)


def _sb_logits(q, k):
    lg = _dot_nt(q, k) * 0.125
    sp = jnp.maximum(lg, 0.0) + jnp.log(1.0 + jnp.exp(-jnp.abs(lg)))
    return lg - sp, -sp


def _attn_b_specs(s):
    q_spec = pl.BlockSpec((1, TQB, 128), lambda hp, qi: (3, qi, hp))
    k_spec = pl.BlockSpec((1, s, 128), lambda hp, qi: (4, 0, hp))
    v_spec = pl.BlockSpec((1, s, 128), lambda hp, qi: (5, 0, hp))
    return q_spec, k_spec, v_spec


def _attn_b_fwd(proj):
    s = proj.shape[1]
    ndiag = TQB // TKB

    def body(q_ref, k_ref, v_ref, o_ref, ls_ref):
        q0 = pl.program_id(1) * TQB
        rows = lax.broadcasted_iota(jnp.int32, (TQB, TKB), 0)
        cols = lax.broadcasted_iota(jnp.int32, (TQB, TKB), 1)
        uj = lax.broadcasted_iota(jnp.int32, (TKB, TKB), 0)
        us = lax.broadcasted_iota(jnp.int32, (TKB, TKB), 1)
        suffix = jnp.where(uj >= us, 1.0, 0.0).astype(BF16)
        heads = [slice(HD * h, HD * (h + 1)) for h in range(2)]
        qs = [q_ref[0, :, sl] for sl in heads]

        def tile(h, kstart, diag, carry, acc):
            k = k_ref[0, pl.ds(kstart, TKB), heads[h]]
            v = v_ref[0, pl.ds(kstart, TKB), heads[h]]
            lb, lk = _sb_logits(qs[h], k)
            if diag is not None:
                strict = rows > cols + diag * TKB
                lk = jnp.where(strict, lk, 0.0)
            csum = _dot2(lk, suffix) + carry
            w = jnp.exp(lb + csum - lk)
            if diag is not None:
                w = jnp.where(strict, w, 0.0)
            return csum[:, 0:1], acc + _dot(w.astype(BF16), v)

        state = [jnp.zeros((TQB, 1), F32), jnp.zeros((TQB, HD), F32)] * 2
        for d in range(ndiag - 1, -1, -1):
            for h in range(2):
                state[2 * h:2 * h + 2] = tile(h, pl.multiple_of(q0 + d * TKB, TKB), d, *state[2 * h:2 * h + 2])
        nsteps = q0 // (KSTEP * TKB)

        def step(i, st):
            st = list(st)
            base = (nsteps - 1 - i) * (KSTEP * TKB)
            for sub in range(KSTEP - 1, -1, -1):
                for h in range(2):
                    st[2 * h:2 * h + 2] = tile(h, pl.multiple_of(base + sub * TKB, TKB), None, *st[2 * h:2 * h + 2])
            return tuple(st)

        state = lax.fori_loop(0, nsteps, step, tuple(state))
        for h in range(2):
            o_ref[:, heads[h]] = state[2 * h + 1]
            ls_ref[0, :, heads[h]] = jnp.broadcast_to(state[2 * h], (TQB, HD))

    q_spec, k_spec, v_spec = _attn_b_specs(s)
    return _call(
        body, name="attn_b_fwd", grid=(4, s // TQB),
        in_specs=[q_spec, k_spec, v_spec],
        out_specs=[pl.BlockSpec((TQB, 128), lambda hp, qi: (qi, hp)),
                   pl.BlockSpec((1, TQB, 128), lambda hp, qi: (hp, qi, 0))],
        out_shape=[_sds((s, WG), F32), _sds((4, s, 128), F32)],
        compiler_params=_params(2),
    )(proj, proj, proj)


def _mix_out(oa, ob, g, w_out, x, gate, tm):
    s = x.shape[0]

    def body(oa_ref, ob_ref, g_ref, w_ref, x_ref, gate_ref, nab_ref, mixed_ref, x2_ref):
        a, b = oa_ref[...], ob_ref[...]
        nab_ref[:, :WG] = (a * _rms(a) * g_ref[:, :WG]).astype(BF16)
        nab_ref[:, WG:] = (b * _rms(b) * g_ref[:, WG:]).astype(BF16)
        mixed = _dot(nab_ref[...], w_ref[...])
        mixed_ref[...] = mixed
        x2_ref[...] = x_ref[...] + gate_ref[...] * mixed

    row = pl.BlockSpec((tm, D), lambda i: (i, 0))
    half = pl.BlockSpec((tm, WG), lambda i: (i, 0))
    vec = pl.BlockSpec((1, D), lambda i: (0, 0))
    return _call(
        body, name="mix_out", grid=(s // tm,),
        in_specs=[half, half, vec, pl.BlockSpec((D, D), lambda i: (0, 0)), row, vec],
        out_specs=[row, row, row],
        out_shape=[_sds((s, D), BF16), _sds((s, D), F32), _sds((s, D), F32)],
        compiler_params=_params(1),
    )(oa, ob, g, w_out, x, gate)


def _conv_taps(u, halo, first, tm):
    rows = lax.broadcasted_iota(jnp.int32, (tm, 1), 0)
    keep = jnp.where(first, 0.0, 1.0)
    h1 = halo[7:8, :] * keep
    h2 = halo[6:7, :] * keep
    um1 = jnp.where(rows == 0, h1, pltpu.roll(u, 1, 0))
    um2 = jnp.where(rows == 0, h2, jnp.where(rows == 1, h1, pltpu.roll(u, 2, 0)))
    return um1, um2


def _conv_specs(tm):
    u_spec = pl.BlockSpec((2, 1, tm, GU), lambda p, i: (0, p, i, 0))
    halo_spec = pl.BlockSpec((2, 1, 8, GU), lambda p, i: (0, p, jnp.maximum(i * (tm // 8) - 1, 0), 0))
    cw_spec = pl.BlockSpec((2, 1, 3, GU), lambda p, i: (0, p, 0, 0))
    cb_spec = pl.BlockSpec((2, 1, 1, GU), lambda p, i: (0, p, 0, 0))
    return u_spec, halo_spec, cw_spec, cb_spec


def _conv_act(u, conv_w, conv_b, tm):
    s = u.shape[2]

    def body(u_ref, halo_ref, cw_ref, cb_ref, a_ref):
        first = pl.program_id(1) == 0
        ys = []
        for side in range(2):
            uv = u_ref[side, 0]
            um1, um2 = _conv_taps(uv, halo_ref[side, 0], first, tm)
            cw = cw_ref[side, 0]
            ys.append(cw[2:3] * uv + cw[1:2] * um1 + cw[0:1] * um2 + cb_ref[side, 0])
        a_ref[0] = (ys[0] * _sigmoid(ys[0]) * ys[1]).astype(BF16)

    u_spec, halo_spec, cw_spec, cb_spec = _conv_specs(tm)
    return _call(
        body, name="conv_act", grid=(4, s // tm),
        in_specs=[u_spec, halo_spec, cw_spec, cb_spec],
        out_specs=pl.BlockSpec((1, tm, GU), lambda p, i: (p, i, 0)),
        out_shape=_sds((4, s, GU), BF16),
        compiler_params=_params(2),
    )(u, u, conv_w, conv_b)


def _down(a, w_down, x2, gate, tm):
    s = x2.shape[0]

    def body(a_ref, w_ref, x_ref, gate_ref, ffn_ref, x3_ref):
        p = pl.program_id(1)
        part = _dot(a_ref[0], w_ref[0])

        @pl.when(p == 0)
        def _():
            ffn_ref[...] = part

        @pl.when(p > 0)
        def _():
            ffn_ref[...] += part

        @pl.when(p == 3)
        def _():
            x3_ref[...] = x_ref[...] + gate_ref[...] * ffn_ref[...]

    row = pl.BlockSpec((tm, D), lambda i, p: (i, 0))
    return _call(
        body, name="down", grid=(s // tm, 4),
        in_specs=[pl.BlockSpec((1, tm, GU), lambda i, p: (p, i, 0)), pl.BlockSpec((1, GU, D), lambda i, p: (p, 0, 0)),
                  row, pl.BlockSpec((1, D), lambda i, p: (0, 0))],
        out_specs=[row, row],
        out_shape=[_sds((s, D), F32), _sds((s, D), F32)],
        compiler_params=_params(2),
    )(a, w_down, x2, gate)


def _final_loss(x, g, target, tm):
    s = x.shape[0]

    def body(x_ref, g_ref, t_ref, loss_ref, dx_ref, dg_ref):
        @pl.when(pl.program_id(0) == 0)
        def _():
            loss_ref[...] = jnp.zeros_like(loss_ref)
            dg_ref[...] = jnp.zeros_like(dg_ref)
        xv = x_ref[...]
        r = _rms(xv)
        nrm = xv * r
        err = nrm * g_ref[...] - t_ref[...]
        loss_ref[...] += (0.5 / D) * jnp.sum(jnp.sum(err * err, axis=-1, keepdims=True), axis=0, keepdims=True)
        dy = err * (1.0 / D)
        dg_ref[...] += _colsum(dy * nrm)
        dx_ref[...] = _rms_bwd(dy * g_ref[...], nrm, r)

    row = pl.BlockSpec((tm, D), lambda i: (i, 0))
    vec = pl.BlockSpec((1, D), lambda i: (0, 0))
    return _call(
        body, name="final_loss", grid=(s // tm,),
        in_specs=[row, vec, row],
        out_specs=[pl.BlockSpec((1, 1), lambda i: (0, 0)), row, vec],
        out_shape=[_sds((1, 1), F32), _sds((s, D), F32), _sds((1, D), F32)],
        compiler_params=_params(1),
    )(x, g, target)


def _down_bwd(dx3, gate, ffn, w_down, tm):
    s = dx3.shape[0]

    def body(dx_ref, gate_ref, ffn_ref, w_ref, dgate_ref, dff_ref, da_ref):
        i, p = pl.program_id(0), pl.program_id(1)

        @pl.when((i == 0) & (p == 0))
        def _():
            dgate_ref[...] = jnp.zeros_like(dgate_ref)

        @pl.when(p == 0)
        def _():
            dxv = dx_ref[...]
            dgate_ref[...] += _colsum(dxv * ffn_ref[...])
            dff_ref[...] = (dxv * gate_ref[...]).astype(BF16)

        da_ref[0] = _dot_nt(dff_ref[...], w_ref[0])

    row = pl.BlockSpec((tm, D), lambda i, p: (i, 0))
    vec = pl.BlockSpec((1, D), lambda i, p: (0, 0))
    return _call(
        body, name="down_bwd", grid=(s // tm, 4),
        in_specs=[row, vec, row, pl.BlockSpec((1, GU, D), lambda i, p: (p, 0, 0))],
        out_specs=[vec, row, pl.BlockSpec((1, tm, GU), lambda i, p: (p, i, 0))],
        out_shape=[_sds((1, D), F32), _sds((s, D), BF16), _sds((4, s, GU), F32)],
        compiler_params=_params(2),
    )(dx3, gate, ffn, w_down)


def _conv_act_bwd(u, conv_w, conv_b, da, tm):
    s = u.shape[2]

    def body(u_ref, halo_ref, cw_ref, cb_ref, da_ref, dy_ref, dcw_ref, dcb_ref):
        first = pl.program_id(1) == 0

        @pl.when(first)
        def _():
            dcw_ref[...] = jnp.zeros_like(dcw_ref)
            dcb_ref[...] = jnp.zeros_like(dcb_ref)

        taps, ys = [], []
        for side in range(2):
            uv = u_ref[side, 0]
            um1, um2 = _conv_taps(uv, halo_ref[side, 0], first, tm)
            cw = cw_ref[side, 0]
            taps.append((um2, um1, uv))
            ys.append(cw[2:3] * uv + cw[1:2] * um1 + cw[0:1] * um2 + cb_ref[side, 0])
        dav = da_ref[0]
        sg = _sigmoid(ys[0])
        dys = (dav * ys[1] * (sg * (1.0 + ys[0] * (1.0 - sg))), dav * (ys[0] * sg))
        for side in range(2):
            dy_ref[side, 0] = dys[side]
            dcb_ref[side, 0] += _colsum(dys[side])
            for j in range(3):
                dcw_ref[side, 0, j:j + 1, :] += _colsum(dys[side] * taps[side][j])

    u_spec, halo_spec, cw_spec, cb_spec = _conv_specs(tm)
    return _call(
        body, name="conv_act_bwd", grid=(4, s // tm),
        in_specs=[u_spec, halo_spec, cw_spec, cb_spec, pl.BlockSpec((1, tm, GU), lambda p, i: (p, i, 0))],
        out_specs=[u_spec, cw_spec, cb_spec],
        out_shape=[_sds((2, 4, s, GU), F32), _sds((2, 4, 3, GU), F32), _sds((2, 4, 1, GU), F32)],
        compiler_params=_params(2),
    )(u, u, conv_w, conv_b, da)


def _conv_transpose(dy, conv_w, tm):
    s = dy.shape[1]
    nt = s // tm

    def body(dy_ref, halo_ref, cw_ref, du_ref):
        keep = jnp.where(pl.program_id(1) == nt - 1, 0.0, 1.0)
        dv = dy_ref[0]
        rows = lax.broadcasted_iota(jnp.int32, (tm, 1), 0)
        h0 = halo_ref[0, 0:1, :] * keep
        h1 = halo_ref[0, 1:2, :] * keep
        dp1 = jnp.where(rows == tm - 1, h0, pltpu.roll(dv, tm - 1, 0))
        dp2 = jnp.where(rows == tm - 1, h1, jnp.where(rows == tm - 2, h0, pltpu.roll(dv, tm - 2, 0)))
        cw = cw_ref[0]
        du_ref[0] = (cw[2:3] * dv + cw[1:2] * dp1 + cw[0:1] * dp2).astype(BF16)

    blk = pl.BlockSpec((1, tm, GU), lambda g, i: (g, i, 0))
    return _call(
        body, name="conv_transpose", grid=(NDEV, nt),
        in_specs=[blk, pl.BlockSpec((1, 8, GU), lambda g, i: (g, jnp.minimum((i + 1) * (tm // 8), s // 8 - 1), 0)),
                  pl.BlockSpec((1, 3, GU), lambda g, i: (g, 0, 0))],
        out_specs=blk,
        out_shape=_sds((NDEV, s, GU), BF16),
        compiler_params=_params(2),
    )(dy, dy, conv_w)


def _wgrad(a3, b3, *, tk, name):
    ga, s, ka = a3.shape
    gb, _, nb = b3.shape
    groups = max(ga, gb)
    nk = s // tk

    def body(a_ref, b_ref, o_ref, acc):
        k = pl.program_id(1)

        @pl.when(k == 0)
        def _():
            acc[...] = jnp.zeros_like(acc)

        acc[...] += _dot_tn(a_ref[0], b_ref[0])

        @pl.when(k == nk - 1)
        def _():
            o_ref[0] = acc[...].astype(BF16)

    a_spec = pl.BlockSpec((1, tk, ka), (lambda g, k: (g, k, 0)) if ga > 1 else (lambda g, k: (0, k, 0)))
    b_spec = pl.BlockSpec((1, tk, nb), (lambda g, k: (g, k, 0)) if gb > 1 else (lambda g, k: (0, k, 0)))
    return _call(
        body, name=name, grid=(groups, nk),
        in_specs=[a_spec, b_spec], out_specs=pl.BlockSpec((1, ka, nb), lambda g, k: (g, 0, 0)),
        out_shape=_sds((groups, ka, nb), BF16),
        scratch_shapes=[pltpu.VMEM((ka, nb), F32)],
        compiler_params=_params(2),
    )(a3, b3)


def _wgrad_in(h1, dparts, tk):
    s = h1.shape[0]
    nk = s // tk

    def body(a_ref, *refs):
        d_refs, o_ref, acc = refs[:NT], refs[NT], refs[NT + 1]
        k = pl.program_id(0)

        @pl.when(k == 0)
        def _():
            acc[...] = jnp.zeros_like(acc)

        for j in range(NT):
            acc[:, WG * j:WG * (j + 1)] += _dot_tn(a_ref[...], d_refs[j][...])

        @pl.when(k == nk - 1)
        def _():
            o_ref[...] = acc[...].astype(BF16)

    return _call(
        body, name="wgrad_in", grid=(nk,),
        in_specs=[pl.BlockSpec((tk, D), lambda k: (k, 0))] + [pl.BlockSpec((tk, WG), lambda k: (k, 0))] * NT,
        out_specs=pl.BlockSpec((D, NT * WG), lambda k: (0, 0)),
        out_shape=_sds((D, NT * WG), BF16),
        scratch_shapes=[pltpu.VMEM((D, NT * WG), F32)],
        compiler_params=_params(1),
    )(h1, *dparts)


def _dgrad_in(dparts, w, x_in, scale, dx_up, tm):
    s = x_in.shape[0]

    def body(*refs):
        d_refs = refs[:NT]
        w_ref, x_ref, sc_ref, up_ref, dx_ref, dsc_ref, dsh_ref = refs[NT:]

        @pl.when(pl.program_id(0) == 0)
        def _():
            dsc_ref[...] = jnp.zeros_like(dsc_ref)
            dsh_ref[...] = jnp.zeros_like(dsh_ref)

        dh = _dot_nt(d_refs[0][...], w_ref[:, 0:WG])
        for j in range(1, NT):
            dh = dh + _dot_nt(d_refs[j][...], w_ref[:, WG * j:WG * (j + 1)])
        xv = x_ref[...]
        r = _rms(xv)
        nrm = xv * r
        dsh_ref[...] += _colsum(dh)
        dsc_ref[...] += _colsum(dh * nrm)
        dx_ref[...] = up_ref[...] + _rms_bwd(dh * (1.0 + sc_ref[...]), nrm, r)

    row = pl.BlockSpec((tm, D), lambda i: (i, 0))
    vec = pl.BlockSpec((1, D), lambda i: (0, 0))
    return _call(
        body, name="dgrad_in", grid=(s // tm,),
        in_specs=[pl.BlockSpec((tm, WG), lambda i: (i, 0))] * NT + [pl.BlockSpec((D, NT * WG), lambda i: (0, 0)), row, vec, row],
        out_specs=[row, vec, vec],
        out_shape=[_sds((s, D), F32), _sds((1, D), F32), _sds((1, D), F32)],
        compiler_params=_params(1),
    )(*dparts, w, x_in, scale, dx_up)


def _dgrad_norm_bwd(d3, w, x_in, scale, dx_up, *, tm, name):
    groups, s, n = d3.shape

    def body(d_ref, w_ref, x_ref, sc_ref, up_ref, dx_ref, dsc_ref, dsh_ref, acc):
        i, g = pl.program_id(0), pl.program_id(1)

        @pl.when((i == 0) & (g == 0))
        def _():
            dsc_ref[...] = jnp.zeros_like(dsc_ref)
            dsh_ref[...] = jnp.zeros_like(dsh_ref)

        part = _dot_nt(d_ref[0], w_ref[0])

        @pl.when(g == 0)
        def _():
            acc[...] = part

        @pl.when(g > 0)
        def _():
            acc[...] += part

        @pl.when(g == groups - 1)
        def _():
            dh = acc[...]
            xv = x_ref[...]
            r = _rms(xv)
            nrm = xv * r
            dsh_ref[...] += _colsum(dh)
            dsc_ref[...] += _colsum(dh * nrm)
            dx_ref[...] = up_ref[...] + _rms_bwd(dh * (1.0 + sc_ref[...]), nrm, r)

    row = pl.BlockSpec((tm, D), lambda i, g: (i, 0))
    vec = pl.BlockSpec((1, D), lambda i, g: (0, 0))
    return _call(
        body, name=name, grid=(s // tm, groups),
        in_specs=[pl.BlockSpec((1, tm, n), lambda i, g: (g, i, 0)), pl.BlockSpec((1, D, n), lambda i, g: (g, 0, 0)),
                  row, vec, row],
        out_specs=[row, vec, vec],
        out_shape=[_sds((s, D), F32), _sds((1, D), F32), _sds((1, D), F32)],
        scratch_shapes=[pltpu.VMEM((tm, D), F32)],
        compiler_params=_params(2),
    )(d3, w, x_in, scale, dx_up)


def _mix_out_bwd(dx2, mixed, gate, w_out, oa, ob, g, tm):
    s = dx2.shape[0]

    def body(dx_ref, mixed_ref, gate_ref, w_ref, oa_ref, ob_ref, g_ref, dgate_ref, dm_ref, doa_ref, dob_ref, dg_ref):
        @pl.when(pl.program_id(0) == 0)
        def _():
            dgate_ref[...] = jnp.zeros_like(dgate_ref)
            dg_ref[...] = jnp.zeros_like(dg_ref)
        dxv = dx_ref[...]
        dgate_ref[...] += _colsum(dxv * mixed_ref[...])
        dm_ref[...] = (dxv * gate_ref[...]).astype(BF16)
        dnab = _dot_nt(dm_ref[...], w_ref[...])
        for o_ref, do_ref, sl in ((oa_ref, doa_ref, slice(0, WG)), (ob_ref, dob_ref, slice(WG, D))):
            ov = o_ref[...]
            r = _rms(ov)
            nrm = ov * r
            dn = dnab[:, sl]
            dg_ref[:, sl] += _colsum(dn * nrm)
            do_ref[...] = _rms_bwd(dn * g_ref[:, sl], nrm, r)

    row = pl.BlockSpec((tm, D), lambda i: (i, 0))
    half = pl.BlockSpec((tm, WG), lambda i: (i, 0))
    vec = pl.BlockSpec((1, D), lambda i: (0, 0))
    return _call(
        body, name="mix_out_bwd", grid=(s // tm,),
        in_specs=[row, row, vec, pl.BlockSpec((D, D), lambda i: (0, 0)), half, half, vec],
        out_specs=[vec, row, half, half, vec],
        out_shape=[_sds((1, D), F32), _sds((s, D), BF16), _sds((s, WG), F32), _sds((s, WG), F32), _sds((1, D), F32)],
        compiler_params=_params(1),
    )(dx2, mixed, gate, w_out, oa, ob, g)


def _attn_a_bwd(proj, bias_tab, d_oa):
    s = proj.shape[1]
    nq = s // TQA

    def body(q_ref, k_ref, v_ref, b_ref, do_ref, dq_ref, dk_ref, dv_ref, db_ref, dk_acc, dv_acc):
        qi = pl.program_id(1)

        @pl.when(qi == 0)
        def _():
            dk_acc[...] = jnp.zeros_like(dk_acc)
            dv_acc[...] = jnp.zeros_like(dv_acc)
            db_ref[...] = jnp.zeros_like(db_ref)

        def tile(nk, off, kstart):
            for h in range(2):
                sl = slice(HD * h, HD * (h + 1))
                q = q_ref[0, :, sl]
                k = k_ref[0, pl.ds(kstart, nk), sl]
                v = v_ref[0, pl.ds(kstart, nk), sl]
                do = do_ref[:, sl].astype(BF16)
                sc = _dot_nt(q, k) * 0.125 + b_ref[h, :, off:off + nk]
                p = jnp.exp(sc - jnp.max(sc, axis=-1, keepdims=True))
                p = p / jnp.sum(p, axis=-1, keepdims=True)
                dp = _dot_nt(do, v)
                ds = p * (dp - jnp.sum(dp * p, axis=-1, keepdims=True))
                db_ref[h, :, off:off + nk] += ds
                dsb = (ds * 0.125).astype(BF16)
                dq_ref[:, sl] = _dot(dsb, k).astype(BF16)
                dk_acc[pl.ds(kstart, nk), sl] += _dot_tn(dsb, q)
                dv_acc[pl.ds(kstart, nk), sl] += _dot_tn(p.astype(BF16), do)

        _attn_a_cases(qi, tile)

        @pl.when(qi == nq - 1)
        def _():
            dk_ref[...] = dk_acc[...].astype(BF16)
            dv_ref[...] = dv_acc[...].astype(BF16)

    q_spec, k_spec, v_spec, b_spec = _attn_a_specs(s)
    blk = pl.BlockSpec((TQA, 128), lambda hp, qi: (qi, hp))
    col = pl.BlockSpec((s, 128), lambda hp, qi: (0, hp))
    return _call(
        body, name="attn_a_bwd", grid=(4, nq),
        in_specs=[q_spec, k_spec, v_spec, b_spec, blk],
        out_specs=[blk, col, col, b_spec],
        out_shape=[_sds((s, WG), BF16), _sds((s, WG), BF16), _sds((s, WG), BF16), _sds((NDEV, TQA, KWA), F32)],
        scratch_shapes=[pltpu.VMEM((s, 128), F32), pltpu.VMEM((s, 128), F32)],
        compiler_params=_params(2),
    )(proj, proj, proj, bias_tab, d_oa)


def _attn_b_bwd(proj, lsum, d_ob):
    s = proj.shape[1]
    nq = s // TQB
    ndiag = TQB // TKB

    def body(q_ref, k_ref, v_ref, ls_ref, do_ref, dq_ref, dk_ref, dv_ref, dk_acc, dv_acc):
        qi = pl.program_id(1)
        q0 = qi * TQB

        @pl.when(qi == 0)
        def _():
            dk_acc[...] = jnp.zeros_like(dk_acc)
            dv_acc[...] = jnp.zeros_like(dv_acc)

        rows = lax.broadcasted_iota(jnp.int32, (TQB, TKB), 0)
        cols = lax.broadcasted_iota(jnp.int32, (TQB, TKB), 1)
        uj = lax.broadcasted_iota(jnp.int32, (TKB, TKB), 0)
        us = lax.broadcasted_iota(jnp.int32, (TKB, TKB), 1)
        prefix = jnp.where(uj <= us, 1.0, 0.0).astype(BF16)
        heads = [slice(HD * h, HD * (h + 1)) for h in range(2)]
        qs = [q_ref[0, :, sl] for sl in heads]
        dos = [do_ref[:, sl].astype(BF16) for sl in heads]
        ltots = [ls_ref[0, :, HD * h:HD * h + 1] for h in range(2)]

        def tile(h, kstart, diag, cl, cg, dq):
            sl = heads[h]
            k = k_ref[0, pl.ds(kstart, TKB), sl]
            v = v_ref[0, pl.ds(kstart, TKB), sl]
            lb, lk = _sb_logits(qs[h], k)
            if diag is not None:
                strict = rows > cols + diag * TKB
                lk = jnp.where(strict, lk, 0.0)
            pre = _dot2(lk, prefix) + cl
            a = jnp.exp(lb + ltots[h] - pre)
            if diag is not None:
                a = jnp.where(strict, a, 0.0)
            gz = _dot_nt(dos[h], v) * a
            pg = _dot2(gz, prefix) + cg
            sig = jnp.exp(lb)
            dl = gz * (1.0 - sig) - (pg - gz) * sig
            if diag is not None:
                dl = jnp.where(strict, dl, 0.0)
            dlb = (dl * 0.125).astype(BF16)
            dk_acc[pl.ds(kstart, TKB), sl] += _dot_tn(dlb, qs[h])
            dv_acc[pl.ds(kstart, TKB), sl] += _dot_tn(a.astype(BF16), dos[h])
            return pre[:, TKB - 1:TKB], pg[:, TKB - 1:TKB], dq + _dot(dlb, k)

        state = [jnp.zeros((TQB, 1), F32), jnp.zeros((TQB, 1), F32), jnp.zeros((TQB, HD), F32)] * 2

        def step(i, st):
            st = list(st)
            base = i * (KSTEP * TKB)
            for sub in range(KSTEP):
                for h in range(2):
                    st[3 * h:3 * h + 3] = tile(h, pl.multiple_of(base + sub * TKB, TKB), None, *st[3 * h:3 * h + 3])
            return tuple(st)

        state = list(lax.fori_loop(0, q0 // (KSTEP * TKB), step, tuple(state)))
        for d in range(ndiag):
            for h in range(2):
                state[3 * h:3 * h + 3] = tile(h, pl.multiple_of(q0 + d * TKB, TKB), d, *state[3 * h:3 * h + 3])
        for h in range(2):
            dq_ref[:, heads[h]] = state[3 * h + 2].astype(BF16)

        @pl.when(qi == nq - 1)
        def _():
            dk_ref[...] = dk_acc[...].astype(BF16)
            dv_ref[...] = dv_acc[...].astype(BF16)

    q_spec, k_spec, v_spec = _attn_b_specs(s)
    blk = pl.BlockSpec((TQB, 128), lambda hp, qi: (qi, hp))
    col = pl.BlockSpec((s, 128), lambda hp, qi: (0, hp))
    return _call(
        body, name="attn_b_bwd", grid=(4, nq),
        in_specs=[q_spec, k_spec, v_spec, pl.BlockSpec((1, TQB, 128), lambda hp, qi: (hp, qi, 0)), blk],
        out_specs=[blk, col, col],
        out_shape=[_sds((s, WG), BF16)] * 3,
        scratch_shapes=[pltpu.VMEM((s, 128), F32), pltpu.VMEM((s, 128), F32)],
        compiler_params=_params(2),
    )(proj, proj, proj, lsum, d_ob)


def _bias_fold(dtab):
    def body(t_ref, d_ref, far_ref):
        acc = jnp.zeros((NDEV, TABW), F32)
        zpad = jnp.zeros((NDEV, TAB0), F32)
        for r in range(TQA):
            row = jnp.concatenate([zpad, t_ref[:, r, :]], axis=1)
            acc = acc + (pltpu.roll(row, TABW - r, 1) if r else row)
        d_ref[...] = acc
        lane = lax.broadcasted_iota(jnp.int32, (NDEV, TABW), 1)
        far = jnp.sum(jnp.where(lane < N_FAR, acc, 0.0), axis=1, keepdims=True)
        far_ref[...] = jnp.broadcast_to(far, (NDEV, 128))

    d_fpad, d_far = _call(
        body, name="bias_fold", grid=(1,),
        in_specs=[pl.BlockSpec((NDEV, TQA, KWA), lambda i: (0, 0, 0))],
        out_specs=[pl.BlockSpec((NDEV, TABW), lambda i: (0, 0)), pl.BlockSpec((NDEV, 128), lambda i: (0, 0))],
        out_shape=[_sds((NDEV, TABW), F32), _sds((NDEV, 128), F32)],
        compiler_params=_params(1),
    )(dtab)
    d_near = d_fpad[:, N_FAR:N_FAR + N_NEAR][:, ::-1]
    return jnp.concatenate([jnp.zeros((NDEV, REL_CLIP - CHUNK + 1), F32), d_near, d_far[:, :1]], axis=1)


def _adamw(w, g, m, v):
    m = B1 * m + (1.0 - B1) * g
    v = B2 * v + (1.0 - B2) * (g * g)
    m_hat = m / (1.0 - B1 ** STEP)
    v_hat = v / (1.0 - B2 ** STEP)
    delta = -LR * (m_hat / (jnp.sqrt(v_hat) + AEPS) + WD * w)
    return delta, m, v


def _adamw_big(recv0, recv1, w, m, v):
    _, rows, cols = recv0.shape
    tr = rows if rows <= 512 else 256
    nt = rows // tr

    def body(r0_ref, r1_ref, w_ref, m_ref, v_ref, g_ref, d_ref, nm_ref, nv_ref):
        def update(r_ref):
            g = r_ref[0].astype(F32)
            for p in range(1, NDEV):
                g = g + r_ref[p].astype(F32)
            delta, nm, nv = _adamw(w_ref[0], g, m_ref[0], v_ref[0])
            g_ref[0], d_ref[0], nm_ref[0], nv_ref[0] = g, delta, nm, nv

        @pl.when(pl.program_id(0) == 0)
        def _():
            update(r0_ref)

        @pl.when(pl.program_id(0) == 1)
        def _():
            update(r1_ref)

    blk = pl.BlockSpec((1, tr, cols), lambda l, i: (l, i, 0))
    r0_spec = pl.BlockSpec((NDEV, tr, cols), lambda l, i: (0, jnp.where(l == 0, i, nt - 1), 0))
    r1_spec = pl.BlockSpec((NDEV, tr, cols), lambda l, i: (0, jnp.where(l == 1, i, 0), 0))
    return _call(
        body, name="adamw_big", grid=(2, nt),
        in_specs=[r0_spec, r1_spec, blk, blk, blk],
        out_specs=[blk] * 4,
        out_shape=[_sds((2, rows, cols), F32)] * 4,
        compiler_params=_params(2),
    )(recv0, recv1, w, m, v)


def _adamw_w_ada(cact_t, dmod, w, m, v):
    tr = 256

    def body(c_ref, dm_ref, w_ref, m_ref, v_ref, g_ref, d_ref, nm_ref, nv_ref):
        g = c_ref[:, 0:1] * dm_ref[0, 0:1, :]
        for b in range(1, NDEV):
            g = g + c_ref[:, b:b + 1] * dm_ref[0, b:b + 1, :]
        delta, nm, nv = _adamw(w_ref[0], g, m_ref[0], v_ref[0])
        g_ref[0], d_ref[0], nm_ref[0], nv_ref[0] = g, delta, nm, nv

    blk = pl.BlockSpec((1, tr, 768), lambda l, i: (l, i, 0))
    return _call(
        body, name="adamw_w_ada", grid=(2, D // tr),
        in_specs=[pl.BlockSpec((tr, NDEV), lambda l, i: (i, 0)), pl.BlockSpec((1, NDEV, 768), lambda l, i: (l, 0, 0)),
                  blk, blk, blk],
        out_specs=[blk] * 4,
        out_shape=[_sds((2, D, 768), F32)] * 4,
        compiler_params=_params(2),
    )(cact_t, dmod, w, m, v)


def _adamw_small(gath, w, m, v):
    rows = gath.shape[1]

    def body(r_ref, w_ref, m_ref, v_ref, g_ref, d_ref, nm_ref, nv_ref):
        g = r_ref[0]
        for p in range(1, NDEV):
            g = g + r_ref[p]
        delta, nm, nv = _adamw(w_ref[...], g, m_ref[...], v_ref[...])
        g_ref[...], d_ref[...], nm_ref[...], nv_ref[...] = g, delta, nm, nv

    blk = pl.BlockSpec((rows, D), lambda i: (0, 0))
    return _call(
        body, name="adamw_small", grid=(1,),
        in_specs=[pl.BlockSpec((NDEV, rows, D), lambda i: (0, 0, 0)), blk, blk, blk],
        out_specs=[blk] * 4,
        out_shape=[_sds((rows, D), F32)] * 4,
        compiler_params=_params(1),
    )(gath, w, m, v)


_PACK = (("b_ada", 2 * 6 * D), ("rel_bias", 2 * 8 * 257), ("g_a", 2 * WG), ("g_b", 2 * WG),
         ("conv_b", 2 * 2 * DFF), ("final_g", D), ("conv_w", 2 * NDEV * 3 * GU))


def _pack(parts):
    rows = []
    for name, size in _PACK:
        flat = parts[name].reshape(-1).astype(F32)
        assert flat.shape[0] == size, (name, flat.shape)
        rows.append(jnp.pad(flat, (0, -size % D)))
    out = jnp.concatenate(rows).reshape(-1, D)
    return jnp.pad(out, ((0, -out.shape[0] % 8), (0, 0)))


def _unpack(packed):
    flat = packed.reshape(-1)
    out, pos = {}, 0
    for name, size in _PACK:
        out[name] = flat[pos:pos + size]
        pos += size + (-size % D)
    return out


def kernel(x, c, w_ada, b_ada, w_in, rel_bias, g_a, g_b, w_out, w_up, conv_w, conv_b, w_down, final_g, loss_target, m_w_ada, m_b_ada, m_w_in, m_rel_bias, m_g_a, m_g_b, m_w_out, m_w_up, m_conv_w, m_conv_b, m_w_down, m_final_g, v_w_ada, v_b_ada, v_w_in, v_rel_bias, v_g_a, v_g_b, v_w_out, v_w_up, v_conv_w, v_conv_b, v_w_down, v_final_g):
    s = x.shape[1]
    assert s % TQA == 0 and s >= KWA and s % 512 == 0
    tm = 512
    me = 4 * lax.axis_index("x") + 2 * lax.axis_index("y") + lax.axis_index("c")
    xs = x.reshape(s, D)
    target = loss_target.reshape(s, D)

    first = jnp.concatenate([c, jnp.pad(conv_w.reshape(2 * 3, GU), ((0, 1), (0, D - GU)))])
    first_all = _small_allgather(first, "gather_c_conv_w")
    c_all = first_all[:, 0, :]
    cw_all = first_all[:, 1:7, :GU].reshape(NDEV, 2, 3, GU)

    b_sl = lax.dynamic_slice(b_ada, (0, me * 768), (2, 768)).reshape(2, 1, 768)
    mod_part, cact = _mod_fwd(c_all, w_ada, b_sl)
    mod_all = _small_allgather(mod_part.reshape(2 * NDEV, 768), "gather_mod")

    shards = {"in": w_in, "out": w_out, "up": w_up, "down": w_down}
    order = [(kind, l) for l in range(2) for kind in _KINDS]
    mod_all, *srcs = lax.optimization_barrier((mod_all, *[shards[kind][l].astype(BF16) for kind, l in order]))
    gather_started = dict(zip(order, _exchange_start([kind for kind, _ in order], True, srcs, "weights_gather_start")))

    def gathered(kind, l, after):
        return _exchange_wait([kind], True, [gather_started[kind, l]], after, f"weights_gather_wait_{kind}{l}")[0]

    mod_all = mod_all.reshape(NDEV, 2, NDEV, 768)
    mod_me = lax.dynamic_index_in_dim(mod_all, me, axis=2, keepdims=False)
    mod = jnp.transpose(mod_me, (1, 0, 2)).reshape(2, 6, 1, D)

    saved = []
    xl = xs
    for l in range(2):
        sh_mix, sc_mix, gt_mix, sh_ffn, sc_ffn, gt_ffn = (mod[l, j] for j in range(6))
        cw = cw_all[:, l].reshape(2, 4, 3, GU)
        cb = conv_b[l].reshape(2, 4, 1, GU)
        gvec = jnp.concatenate([g_a[l], g_b[l]]).reshape(1, D)
        tab = _bias_table(rel_bias[l])

        wi = gathered("in", l, xl if l else mod)
        h1, proj = _nm_matmul(xl, sh_mix, sc_mix, wi, two_d=True, n=WG, groups=NT, out_dtype=BF16, tm=tm,
                              name="norm_proj")
        oa = _attn_a_fwd(proj, tab)
        ob, lsum = _attn_b_fwd(proj)
        wo = gathered("out", l, ob)
        nab, mixed, x2 = _mix_out(oa, ob, gvec, wo, xl, gt_mix, tm)
        wu = gathered("up", l, x2)
        h2, u = _nm_matmul(x2, sh_ffn, sc_ffn, wu, two_d=False, n=GU, groups=NDEV, out_dtype=F32, tm=tm,
                           name="norm_up")
        u = u.reshape(2, 4, s, GU)
        a = _conv_act(u, cw, cb, tm)
        wd4 = gathered("down", l, a).reshape(4, GU, D)
        ffn, x3 = _down(a, wd4, x2, gt_ffn, tm)
        saved.append(dict(x=xl, h1=h1, proj=proj, oa=oa, ob=ob, lsum=lsum, nab=nab, mixed=mixed, x2=x2, h2=h2, u=u,
                          a=a, ffn=ffn, cw=cw, cb=cb, gvec=gvec, tab=tab, wd4=wd4, wi=wi, wo=wo, wu=wu))
        xl = x3

    loss_part, dx, d_final_g = _final_loss(xl, final_g.reshape(1, D), target, tm)
    loss = lax.psum(loss_part[0, 0], ("x", "y", "c"))

    sent = {}
    small = {"b_ada": [None, None], "rel_bias": [None, None], "g_a": [None, None], "g_b": [None, None],
             "conv_b": [None, None], "conv_w": [None, None]}

    def send(kind, l, grad, then):
        started, token = _exchange_start([kind], False, [grad], f"grads_start_{kind}{l}", with_token=True)
        sent[kind, l] = started[0]
        return lax.optimization_barrier((then, token))[0]

    for l in (1, 0):
        sv = saved[l]
        sh_mix, sc_mix, gt_mix, sh_ffn, sc_ffn, gt_ffn = (mod[l, j] for j in range(6))
        d_gt_ffn, dff, da = _down_bwd(dx, gt_ffn, sv["ffn"], sv["wd4"], tm)
        da = send("down", l, _wgrad(sv["a"], dff.reshape(1, s, D), tk=tm, name="wgrad_down").reshape(DFF, D), da)
        dy, d_cw, d_cb = _conv_act_bwd(sv["u"], sv["cw"], sv["cb"], da, tm)
        du = _conv_transpose(dy.reshape(NDEV, s, GU), sv["cw"].reshape(NDEV, 3, GU), tm)
        du = send("up", l, _wgrad(sv["h2"].reshape(1, s, D), du, tk=tm, name="wgrad_up"), du)
        dx2, d_sc_ffn, d_sh_ffn = _dgrad_norm_bwd(du, sv["wu"], sv["x2"], sc_ffn, dx, tm=tm, name="dgrad_up")
        d_gt_mix, dmixed, d_oa, d_ob, d_g = _mix_out_bwd(dx2, sv["mixed"], gt_mix, sv["wo"], sv["oa"], sv["ob"],
                                                         sv["gvec"], tm)
        d_oa = send("out", l, _wgrad(sv["nab"].reshape(1, s, D), dmixed.reshape(1, s, D), tk=tm,
                                     name="wgrad_out").reshape(D, D), d_oa)
        dqa, dka, dva, d_tab = _attn_a_bwd(sv["proj"], sv["tab"], d_oa)
        dqb, dkb, dvb = _attn_b_bwd(sv["proj"], sv["lsum"], d_ob)
        dparts = (dqa, dka, dva, dqb, dkb, dvb)
        dx2 = send("in", l, _wgrad_in(sv["h1"], dparts, tm), dx2)
        dx, d_sc_mix, d_sh_mix = _dgrad_in(dparts, sv["wi"], sv["x"], sc_mix, dx2, tm)
        small["b_ada"][l] = jnp.concatenate([d_sh_mix, d_sc_mix, d_gt_mix, d_sh_ffn, d_sc_ffn, d_gt_ffn], axis=1)
        small["rel_bias"][l] = _bias_fold(d_tab)
        small["g_a"][l], small["g_b"][l] = d_g[:, :WG], d_g[:, WG:]
        small["conv_b"][l] = d_cb
        small["conv_w"][l] = d_cw.reshape(NDEV, 3, GU)
    grad_x = dx.reshape(1, s, D)

    big = {}
    for kind, (w, m, v) in (("down", (w_down, m_w_down, v_w_down)), ("up", (w_up, m_w_up, v_w_up)),
                            ("out", (w_out, m_w_out, v_w_out)), ("in", (w_in, m_w_in, v_w_in))):
        recv0, recv1 = _exchange_wait([kind, kind], False, [sent[kind, 0], sent[kind, 1]], dx, f"grads_wait_{kind}")
        big[kind] = _adamw_big(recv0, recv1, w, m, v)

    contrib = {k: jnp.stack(vs) for k, vs in small.items()}
    contrib["final_g"] = d_final_g
    gath = _small_allgather(_pack(contrib), "gather_small_grads")

    def place_conv_w(t):
        return lax.dynamic_update_slice(jnp.zeros((2, NDEV, 3, GU), F32), t.reshape(2, 1, 3, GU), (0, me, 0, 0))

    def packed_params(b, rb, ga, gb, cb_, fg, cw_):
        return _pack({"b_ada": b, "rel_bias": rb, "g_a": ga, "g_b": gb, "conv_b": cb_, "final_g": fg,
                      "conv_w": place_conv_w(cw_)})

    sm = _adamw_small(gath,
                      packed_params(b_ada, rel_bias, g_a, g_b, conv_b, final_g, conv_w),
                      packed_params(m_b_ada, m_rel_bias, m_g_a, m_g_b, m_conv_b, m_final_g, m_conv_w),
                      packed_params(v_b_ada, v_rel_bias, v_g_a, v_g_b, v_conv_b, v_final_g, v_conv_w))
    sm = [_unpack(t) for t in sm]

    dmod_all = gath[:, :12, :].reshape(NDEV, 2, 6 * D)
    dmod_sl = jnp.transpose(lax.dynamic_slice(dmod_all, (0, 0, me * 768), (NDEV, 2, 768)), (1, 0, 2))
    ada = _adamw_w_ada(cact.T, dmod_sl, w_ada, m_w_ada, v_w_ada)

    def small_out(j, name):
        t = sm[j][name]
        if name == "b_ada":
            return t.reshape(2, 6 * D)
        if name == "rel_bias":
            return t.reshape(2, 8, 257)
        if name in ("g_a", "g_b"):
            return t.reshape(2, WG)
        if name == "conv_b":
            return t.reshape(2, 2 * DFF)
        if name == "final_g":
            return t.reshape(D)
        t = t.reshape(2, NDEV, 3, GU)
        return lax.dynamic_index_in_dim(t, me, axis=1, keepdims=False)

    def group(j):
        return (ada[j], small_out(j, "b_ada"), big["in"][j], small_out(j, "rel_bias"), small_out(j, "g_a"),
                small_out(j, "g_b"), big["out"][j], big["up"][j], small_out(j, "conv_w"), small_out(j, "conv_b"),
                big["down"][j], small_out(j, "final_g"))

    return (loss, grad_x, *group(0), *group(1), *group(2), *group(3))
```

```python
import jax
import jax.numpy as jnp
from jax import lax
from jax.experimental import pallas as pl
from jax.experimental.pallas import tpu as pltpu

F32, BF16 = jnp.float32, jnp.bfloat16
MESH_ID = pl.DeviceIdType.MESH
NDEV = 8
D = 1024
HD = 64
WG = 512
NT = 6
GU = 704
DFF = 2816
CHUNK, NPREV, REL_CLIP = 64, 8, 128
BAND = (NPREV + 1) * CHUNK
EPS = 1e-6
NEG = -1e30
TQA = 256
KWA = TQA + NPREV * CHUNK
TABW = 1024
TAB0 = TABW - KWA
N_FAR = TAB0 + NPREV * CHUNK - REL_CLIP + 1
N_NEAR = REL_CLIP + CHUNK - 1
TQB, TKB = 512, 128
KSTEP = 4
LR, B1, B2, AEPS, WD, STEP = 0.001, 0.9, 0.999, 1e-08, 0.01, 10
VMEM_MB = 56


def _call(body, **kw):
    return pl.pallas_call(body, **kw)


def _call_after(body, after, **kw):
    n_in = len(kw["in_specs"])
    kw["in_specs"] = list(kw["in_specs"]) + [pl.BlockSpec(memory_space=pl.ANY)]

    def tied(*refs):
        body(*refs[:n_in], *refs[n_in + 1:])

    call = _call(tied, **kw)
    return lambda *args: call(*args, after)


def _params(n_axes):
    return pltpu.CompilerParams(dimension_semantics=("arbitrary",) * n_axes, vmem_limit_bytes=VMEM_MB << 20)


def _dot(a, b):
    return jnp.dot(a, b, preferred_element_type=F32)


def _dot_nt(a, b):
    return lax.dot_general(a, b, (((1,), (1,)), ((), ())), preferred_element_type=F32)


def _dot_tn(a, b):
    return lax.dot_general(a, b, (((0,), (0,)), ((), ())), preferred_element_type=F32)


def _dot2(x, u):
    hi = x.astype(BF16)
    lo = (x - hi.astype(F32)).astype(BF16)
    return _dot(hi, u) + _dot(lo, u)


def _dot2_tn(x, u):
    hi = x.astype(BF16)
    lo = (x - hi.astype(F32)).astype(BF16)
    return _dot_tn(hi, u) + _dot_tn(lo, u)


def _rms(x):
    return lax.rsqrt(jnp.mean(x * x, axis=-1, keepdims=True) + EPS)


def _rms_bwd(dn, n, r):
    return r * (dn - n * jnp.mean(dn * n, axis=-1, keepdims=True))


def _colsum(x):
    return jnp.sum(x, axis=0, keepdims=True)


def _sigmoid(x):
    return 1.0 / (1.0 + jnp.exp(-x))


def _sds(shape, dtype):
    return jax.ShapeDtypeStruct(shape, dtype)


def _place():
    x, y, c = lax.axis_index("x"), lax.axis_index("y"), lax.axis_index("c")
    return x, y, c, 4 * x + 2 * y + c


def _peer(x, y, c, k):
    px = 1 - x if k & 4 else x
    py = 1 - y if k & 2 else y
    pc = 1 - c if k & 1 else c
    return (px, py, pc), 4 * px + 2 * py + pc


def _small_allgather(v, name):
    rows, cols = v.shape

    def body(v_ref, out_ref, send_sems, recv_sems, local_sem):
        x, y, c, me = _place()
        mine = pltpu.make_async_copy(v_ref, out_ref.at[me], local_sem)
        mine.start()
        sends = []
        for k in range(1, NDEV):
            peer, _ = _peer(x, y, c, k)
            cp = pltpu.make_async_remote_copy(v_ref, out_ref.at[me], send_sems.at[k - 1], recv_sems.at[k - 1],
                                              device_id=peer, device_id_type=MESH_ID)
            cp.start()
            sends.append(cp)
        for k in range(1, NDEV):
            peer, pidx = _peer(x, y, c, k)
            pltpu.make_async_remote_copy(v_ref, out_ref.at[pidx], send_sems.at[k - 1], recv_sems.at[k - 1],
                                         device_id=peer, device_id_type=MESH_ID).wait_recv()
        for cp in sends:
            cp.wait_send()
        mine.wait()

    return _call(
        body, name=name,
        out_shape=_sds((NDEV, rows, cols), F32),
        in_specs=[pl.BlockSpec(memory_space=pltpu.VMEM)],
        out_specs=pl.BlockSpec(memory_space=pltpu.VMEM),
        scratch_shapes=[pltpu.SemaphoreType.DMA((NDEV - 1,)), pltpu.SemaphoreType.DMA((NDEV - 1,)),
                        pltpu.SemaphoreType.DMA],
    )(v)


def _shard_view(ref, kind, p):
    if kind == "in":
        return ref.at[:, pl.ds(pl.multiple_of(p * 384, 128), 384)]
    if kind == "out":
        return ref.at[pl.ds(pl.multiple_of(p * 128, 128), 128), :]
    if kind == "up":
        return ref.at[p]
    if kind == "down":
        return ref.at[pl.ds(pl.multiple_of(p * 352, 16), 352), :]
    raise ValueError(kind)


_KINDS = ("in", "out", "up", "down")
_FULL_SHAPES = {"in": (D, 3 * D), "out": (D, D), "up": (NDEV, D, GU), "down": (DFF, D)}
_SHARD_SHAPES = {"in": (D, 384), "out": (128, D), "up": (D, GU), "down": (352, D)}


_HBM = pl.BlockSpec(memory_space=pltpu.HBM)
_SEM = pl.BlockSpec(memory_space=pltpu.SEMAPHORE)
_EFFECT = pltpu.SideEffectType.DATAFLOW_SIDE_EFFECTING
_SEM_SHAPES = (pltpu.SemaphoreType.DMA((NDEV - 1,)), pltpu.SemaphoreType.DMA((NDEV - 1,)), pltpu.SemaphoreType.DMA(()))


def _hbm(a):
    return pltpu.with_memory_space_constraint(a, pltpu.HBM)


def _exchange_copies(kind, gather, src, land, sems):
    send_sems, recv_sems, local_sem = sems
    x, y, c, me = _place()

    def ends(p_dst, p_from):
        if gather:
            return src, _shard_view(land, kind, me), _shard_view(land, kind, p_from)
        return _shard_view(src, kind, p_dst), land.at[me], land.at[p_from]

    s_me, d_me, _ = ends(me, me)
    local = pltpu.make_async_copy(s_me, d_me, local_sem)
    sends, arrivals = [], []
    for k in range(1, NDEV):
        peer, pidx = _peer(x, y, c, k)
        s_k, d_k, from_k = ends(pidx, pidx)
        sends.append(pltpu.make_async_remote_copy(s_k, d_k, send_sems.at[k - 1], recv_sems.at[k - 1],
                                                  device_id=peer, device_id_type=MESH_ID))
        arrivals.append(pltpu.make_async_remote_copy(s_k, from_k, send_sems.at[k - 1], recv_sems.at[k - 1],
                                                     device_id=peer, device_id_type=MESH_ID))
    return local, sends, arrivals


def _exchange_start(kinds, gather, srcs, name, with_token=False):
    n = len(kinds)
    lands = [lax.empty(_FULL_SHAPES[kd] if gather else (NDEV,) + _SHARD_SHAPES[kd], BF16) for kd in kinds]

    def body(*refs):
        ins, sems = refs[:2 * n], refs[2 * n:5 * n]
        for j, kd in enumerate(kinds):
            local, sends, _ = _exchange_copies(kd, gather, ins[j], ins[n + j], sems[3 * j:3 * j + 3])
            local.start()
            for cp in sends:
                cp.start()
        if with_token:
            token = refs[7 * n]
            token[...] = jnp.zeros_like(token)

    out_shape = list(_SEM_SHAPES) * n
    out_shape += [pltpu.HBM(a.shape, a.dtype) for a in srcs] + [pltpu.HBM(a.shape, a.dtype) for a in lands]
    out_specs = [_SEM] * (3 * n) + [_HBM] * (2 * n)
    if with_token:
        out_shape.append(_sds((8, 128), F32))
        out_specs.append(pl.BlockSpec(memory_space=pltpu.VMEM))
    outs = _call(
        body, name=name, out_shape=out_shape,
        in_specs=[_HBM] * (2 * n), out_specs=out_specs,
        input_output_aliases={i: 3 * n + i for i in range(2 * n)},
        compiler_params=pltpu.CompilerParams(has_side_effects=_EFFECT),
    )(*[_hbm(a) for a in srcs], *[_hbm(a) for a in lands])
    per_tensor = [(tuple(outs[3 * j:3 * j + 3]), outs[3 * n + j], outs[4 * n + j]) for j in range(n)]
    return (per_tensor, outs[5 * n]) if with_token else per_tensor


def _exchange_wait(kinds, gather, started, after, name):
    n = len(kinds)

    def body(*refs):
        ins, sems = refs[:2 * n], refs[2 * n:5 * n]
        for j, kd in enumerate(kinds):
            local, sends, arrivals = _exchange_copies(kd, gather, ins[j], ins[n + j], sems[3 * j:3 * j + 3])
            local.wait()
            for cp in arrivals:
                cp.wait_recv()
            for cp in sends:
                cp.wait_send()

    srcs = [st[1] for st in started]
    lands = [st[2] for st in started]
    sems = [sm for st in started for sm in st[0]]
    outs = _call(
        body, name=name,
        out_shape=[pltpu.HBM(a.shape, a.dtype) for a in srcs] + [pltpu.HBM(a.shape, a.dtype) for a in lands],
        in_specs=[_HBM] * (2 * n) + [_SEM] * (3 * n) + [pl.BlockSpec(memory_space=pl.ANY)],
        out_specs=[_HBM] * (2 * n),
        input_output_aliases={i: i for i in range(2 * n)},
        compiler_params=pltpu.CompilerParams(has_side_effects=_EFFECT),
    )(*srcs, *lands, *sems, after)
    return outs[n:]


def _mod_fwd(c_all, w_ada, b_sl):
    def body(c_ref, w_ref, b_ref, mod_ref, cact_ref):
        cv = c_ref[...]
        ca = cv * _sigmoid(cv)
        cact_ref[...] = ca
        mod_ref[0] = _dot(ca.astype(BF16), w_ref[0].astype(BF16)) + b_ref[0]

    return _call(
        body, name="mod_fwd", grid=(2,),
        in_specs=[pl.BlockSpec((NDEV, D), lambda l: (0, 0)), pl.BlockSpec((1, D, 768), lambda l: (l, 0, 0)),
                  pl.BlockSpec((1, 1, 768), lambda l: (l, 0, 0))],
        out_specs=[pl.BlockSpec((1, NDEV, 768), lambda l: (l, 0, 0)), pl.BlockSpec((NDEV, D), lambda l: (0, 0))],
        out_shape=[_sds((2, NDEV, 768), F32), _sds((NDEV, D), F32)],
        compiler_params=_params(1),
    )(c_all, w_ada, b_sl)


def _nm_matmul(x, shift, scale, w, *, two_d, n, groups, out_dtype, tm, name):
    s = x.shape[0]

    def body(x_ref, sh_ref, sc_ref, w_ref, h_ref, o_ref):
        @pl.when(pl.program_id(1) == 0)
        def _():
            xv = x_ref[...]
            h_ref[...] = ((xv * _rms(xv)) * (1.0 + sc_ref[...]) + sh_ref[...]).astype(BF16)
        wv = w_ref[...] if two_d else w_ref[0]
        o_ref[0] = _dot(h_ref[...], wv).astype(out_dtype)

    vec = pl.BlockSpec((1, D), lambda i, g: (0, 0))
    w_spec = pl.BlockSpec((D, n), lambda i, g: (0, g)) if two_d else pl.BlockSpec((1, D, n), lambda i, g: (g, 0, 0))
    return _call(
        body, name=name, grid=(s // tm, groups),
        in_specs=[pl.BlockSpec((tm, D), lambda i, g: (i, 0)), vec, vec, w_spec],
        out_specs=[pl.BlockSpec((tm, D), lambda i, g: (i, 0)), pl.BlockSpec((1, tm, n), lambda i, g: (g, i, 0))],
        out_shape=[_sds((s, D), BF16), _sds((groups, s, n), out_dtype)],
        compiler_params=_params(2),
    )(x, shift, scale, w)


def _bias_table(rel_bias):
    far = jnp.broadcast_to(rel_bias[:, 2 * REL_CLIP:], (NDEV, N_FAR))
    near = rel_bias[:, 2 * REL_CLIP - 1:REL_CLIP - CHUNK:-1]
    fpad = jnp.concatenate([far, near, jnp.zeros((NDEV, TABW - N_FAR - N_NEAR), F32)], axis=1)

    def body(f_ref, o_ref):
        t = pltpu.roll(jnp.broadcast_to(f_ref[0], (TQA, TABW)), 0, 1, stride=1, stride_axis=0)[:, TAB0:]
        rows = lax.broadcasted_iota(jnp.int32, (TQA, KWA), 0)
        cols = lax.broadcasted_iota(jnp.int32, (TQA, KWA), 1)
        first = jnp.bitwise_and(rows, -CHUNK)
        o_ref[0] = jnp.where((cols >= first) & (cols < first + BAND), t, NEG)

    return _call(
        body, name="bias_table", grid=(NDEV,),
        in_specs=[pl.BlockSpec((1, 1, TABW), lambda h: (h, 0, 0))],
        out_specs=pl.BlockSpec((1, TQA, KWA), lambda h: (h, 0, 0)),
        out_shape=_sds((NDEV, TQA, KWA), F32),
        compiler_params=_params(1),
    )(fpad.reshape(NDEV, 1, TABW))


def _attn_a_cases(qi, tile):
    @pl.when(qi == 0)
    def _():
        tile(TQA, 2 * TQA, 0)

    @pl.when(qi == 1)
    def _():
        tile(2 * TQA, TQA, 0)

    @pl.when(qi >= 2)
    def _():
        tile(KWA, 0, pl.multiple_of((qi - 2) * TQA, TQA))


def _attn_a_specs(s):
    q_spec = pl.BlockSpec((1, TQA, 128), lambda hp, qi: (0, qi, hp))
    k_spec = pl.BlockSpec((1, s, 128), lambda hp, qi: (1, 0, hp))
    v_spec = pl.BlockSpec((1, s, 128), lambda hp, qi: (2, 0, hp))
    b_spec = pl.BlockSpec((2, TQA, KWA), lambda hp, qi: (hp, 0, 0))
    return q_spec, k_spec, v_spec, b_spec


def _attn_a_fwd(proj, bias_tab):
    s = proj.shape[1]

    def body(q_ref, k_ref, v_ref, b_ref, o_ref):
        def tile(nk, off, kstart):
            for h in range(2):
                sl = slice(HD * h, HD * (h + 1))
                q = q_ref[0, :, sl]
                k = k_ref[0, pl.ds(kstart, nk), sl]
                v = v_ref[0, pl.ds(kstart, nk), sl]
                sc = _dot_nt(q, k) * 0.125 + b_ref[h, :, off:off + nk]
                p = jnp.exp(sc - jnp.max(sc, axis=-1, keepdims=True))
                den = jnp.sum(p, axis=-1, keepdims=True)
                o_ref[:, sl] = _dot(p.astype(BF16), v) / den

        _attn_a_cases(pl.program_id(1), tile)

    q_spec, k_spec, v_spec, b_spec = _attn_a_specs(s)
    return _call(
        body, name="attn_a_fwd", grid=(4, s // TQA),
        in_specs=[q_spec, k_spec, v_spec, b_spec],
        out_specs=pl.BlockSpec((TQA, 128), lambda hp, qi: (qi, hp)),
        out_shape=_sds((s, WG), F32),
        compiler_params=_params(2),
    )(proj, proj, proj, bias_tab)


def _sb_logits(q_scaled, k):
    lg = _dot_nt(q_scaled, k)
    sp = jnp.maximum(lg, 0.0) + jnp.log(1.0 + jnp.exp(-jnp.abs(lg)))
    return lg - sp, sp


def _attn_b_specs(s):
    q_spec = pl.BlockSpec((1, TQB, 128), lambda hp, qi: (3, qi, hp))
    k_spec = pl.BlockSpec((1, s, 128), lambda hp, qi: (4, 0, hp))
    v_spec = pl.BlockSpec((1, s, 128), lambda hp, qi: (5, 0, hp))
    return q_spec, k_spec, v_spec


def _attn_b_fwd(proj):
    s = proj.shape[1]
    ndiag = TQB // TKB

    def body(q_ref, k_ref, v_ref, o_ref, ls_ref):
        q0 = pl.program_id(1) * TQB
        rows = lax.broadcasted_iota(jnp.int32, (TQB, TKB), 0)
        cols = lax.broadcasted_iota(jnp.int32, (TQB, TKB), 1)
        uj = lax.broadcasted_iota(jnp.int32, (TKB, TKB), 0)
        us = lax.broadcasted_iota(jnp.int32, (TKB, TKB), 1)
        suffix = jnp.where(uj >= us, 1.0, 0.0).astype(BF16)
        heads = [slice(HD * h, HD * (h + 1)) for h in range(2)]
        qs = [(q_ref[0, :, sl].astype(F32) * 0.125).astype(BF16) for sl in heads]

        def tile(h, kstart, diag, carry, acc):
            k = k_ref[0, pl.ds(kstart, TKB), heads[h]]
            v = v_ref[0, pl.ds(kstart, TKB), heads[h]]
            lb, sp = _sb_logits(qs[h], k)
            if diag is not None:
                strict = rows > cols + diag * TKB
                sp = jnp.where(strict, sp, 0.0)
            csum = _dot2(sp, suffix) + carry
            w = jnp.exp(lb - csum + sp)
            if diag is not None:
                w = jnp.where(strict, w, 0.0)
            return csum[:, 0:1], acc + _dot(w.astype(BF16), v)

        state = [jnp.zeros((TQB, 1), F32), jnp.zeros((TQB, HD), F32)] * 2
        for d in range(ndiag - 1, -1, -1):
            for h in range(2):
                state[2 * h:2 * h + 2] = tile(h, pl.multiple_of(q0 + d * TKB, TKB), d, *state[2 * h:2 * h + 2])
        nsteps = q0 // (KSTEP * TKB)

        def step(i, st):
            st = list(st)
            base = (nsteps - 1 - i) * (KSTEP * TKB)
            for sub in range(KSTEP - 1, -1, -1):
                for h in range(2):
                    st[2 * h:2 * h + 2] = tile(h, pl.multiple_of(base + sub * TKB, TKB), None, *st[2 * h:2 * h + 2])
            return tuple(st)

        state = lax.fori_loop(0, nsteps, step, tuple(state))
        for h in range(2):
            o_ref[:, heads[h]] = state[2 * h + 1]
            ls_ref[0, :, heads[h]] = jnp.broadcast_to(state[2 * h], (TQB, HD))

    q_spec, k_spec, v_spec = _attn_b_specs(s)
    return _call(
        body, name="attn_b_fwd", grid=(4, s // TQB),
        in_specs=[q_spec, k_spec, v_spec],
        out_specs=[pl.BlockSpec((TQB, 128), lambda hp, qi: (qi, hp)),
                   pl.BlockSpec((1, TQB, 128), lambda hp, qi: (hp, qi, 0))],
        out_shape=[_sds((s, WG), F32), _sds((4, s, 128), F32)],
        compiler_params=_params(2),
    )(proj, proj, proj)


def _mix_out(oa, ob, g, w_out, x, gate, tm):
    s = x.shape[0]

    def body(oa_ref, ob_ref, g_ref, w_ref, x_ref, gate_ref, nab_ref, mixed_ref, x2_ref):
        a, b = oa_ref[...], ob_ref[...]
        nab_ref[:, :WG] = (a * _rms(a) * g_ref[:, :WG]).astype(BF16)
        nab_ref[:, WG:] = (b * _rms(b) * g_ref[:, WG:]).astype(BF16)
        mixed = _dot(nab_ref[...], w_ref[...])
        mixed_ref[...] = mixed
        x2_ref[...] = x_ref[...] + gate_ref[...] * mixed

    row = pl.BlockSpec((tm, D), lambda i: (i, 0))
    half = pl.BlockSpec((tm, WG), lambda i: (i, 0))
    vec = pl.BlockSpec((1, D), lambda i: (0, 0))
    return _call(
        body, name="mix_out", grid=(s // tm,),
        in_specs=[half, half, vec, pl.BlockSpec((D, D), lambda i: (0, 0)), row, vec],
        out_specs=[row, row, row],
        out_shape=[_sds((s, D), BF16), _sds((s, D), F32), _sds((s, D), F32)],
        compiler_params=_params(1),
    )(oa, ob, g, w_out, x, gate)


def _conv_taps(u, halo, first, tm):
    rows = lax.broadcasted_iota(jnp.int32, (tm, 1), 0)
    keep = jnp.where(first, 0.0, 1.0)
    h1 = halo[7:8, :] * keep
    h2 = halo[6:7, :] * keep
    um1 = jnp.where(rows == 0, h1, pltpu.roll(u, 1, 0))
    um2 = jnp.where(rows == 0, h2, jnp.where(rows == 1, h1, pltpu.roll(u, 2, 0)))
    return um1, um2


def _conv_specs(tm):
    u_spec = pl.BlockSpec((2, 1, tm, GU), lambda p, i: (0, p, i, 0))
    halo_spec = pl.BlockSpec((2, 1, 8, GU), lambda p, i: (0, p, jnp.maximum(i * (tm // 8) - 1, 0), 0))
    cw_spec = pl.BlockSpec((2, 1, 3, GU), lambda p, i: (0, p, 0, 0))
    cb_spec = pl.BlockSpec((2, 1, 1, GU), lambda p, i: (0, p, 0, 0))
    return u_spec, halo_spec, cw_spec, cb_spec


def _conv_act(u, conv_w, conv_b, tm):
    s = u.shape[2]

    def body(u_ref, halo_ref, cw_ref, cb_ref, a_ref):
        first = pl.program_id(1) == 0
        ys = []
        for side in range(2):
            uv = u_ref[side, 0]
            um1, um2 = _conv_taps(uv, halo_ref[side, 0], first, tm)
            cw = cw_ref[side, 0]
            ys.append(cw[2:3] * uv + cw[1:2] * um1 + cw[0:1] * um2 + cb_ref[side, 0])
        a_ref[0] = (ys[0] * _sigmoid(ys[0]) * ys[1]).astype(BF16)

    u_spec, halo_spec, cw_spec, cb_spec = _conv_specs(tm)
    return _call(
        body, name="conv_act", grid=(4, s // tm),
        in_specs=[u_spec, halo_spec, cw_spec, cb_spec],
        out_specs=pl.BlockSpec((1, tm, GU), lambda p, i: (p, i, 0)),
        out_shape=_sds((4, s, GU), BF16),
        compiler_params=_params(2),
    )(u, u, conv_w, conv_b)


def _down(a, w_down, x2, gate, tm):
    s = x2.shape[0]

    def body(a_ref, w_ref, x_ref, gate_ref, ffn_ref, x3_ref):
        p = pl.program_id(1)
        part = _dot(a_ref[0], w_ref[0])

        @pl.when(p == 0)
        def _():
            ffn_ref[...] = part

        @pl.when(p > 0)
        def _():
            ffn_ref[...] += part

        @pl.when(p == 3)
        def _():
            x3_ref[...] = x_ref[...] + gate_ref[...] * ffn_ref[...]

    row = pl.BlockSpec((tm, D), lambda i, p: (i, 0))
    return _call(
        body, name="down", grid=(s // tm, 4),
        in_specs=[pl.BlockSpec((1, tm, GU), lambda i, p: (p, i, 0)), pl.BlockSpec((1, GU, D), lambda i, p: (p, 0, 0)),
                  row, pl.BlockSpec((1, D), lambda i, p: (0, 0))],
        out_specs=[row, row],
        out_shape=[_sds((s, D), F32), _sds((s, D), F32)],
        compiler_params=_params(2),
    )(a, w_down, x2, gate)


def _final_loss(x, g, target, tm):
    s = x.shape[0]

    def body(x_ref, g_ref, t_ref, loss_ref, dx_ref, dg_ref):
        @pl.when(pl.program_id(0) == 0)
        def _():
            loss_ref[...] = jnp.zeros_like(loss_ref)
            dg_ref[...] = jnp.zeros_like(dg_ref)
        xv = x_ref[...]
        r = _rms(xv)
        nrm = xv * r
        err = nrm * g_ref[...] - t_ref[...]
        loss_ref[...] += (0.5 / D) * jnp.sum(jnp.sum(err * err, axis=-1, keepdims=True), axis=0, keepdims=True)
        dy = err * (1.0 / D)
        dg_ref[...] += _colsum(dy * nrm)
        dx_ref[...] = _rms_bwd(dy * g_ref[...], nrm, r)

    row = pl.BlockSpec((tm, D), lambda i: (i, 0))
    vec = pl.BlockSpec((1, D), lambda i: (0, 0))
    return _call(
        body, name="final_loss", grid=(s // tm,),
        in_specs=[row, vec, row],
        out_specs=[pl.BlockSpec((1, 1), lambda i: (0, 0)), row, vec],
        out_shape=[_sds((1, 1), F32), _sds((s, D), F32), _sds((1, D), F32)],
        compiler_params=_params(1),
    )(x, g, target)


def _down_bwd(dx3, gate, ffn, w_down, tm):
    s = dx3.shape[0]

    def body(dx_ref, gate_ref, ffn_ref, w_ref, dgate_ref, dff_ref, da_ref):
        i, p = pl.program_id(0), pl.program_id(1)

        @pl.when((i == 0) & (p == 0))
        def _():
            dgate_ref[...] = jnp.zeros_like(dgate_ref)

        @pl.when(p == 0)
        def _():
            dxv = dx_ref[...]
            dgate_ref[...] += _colsum(dxv * ffn_ref[...])
            dff_ref[...] = (dxv * gate_ref[...]).astype(BF16)

        da_ref[0] = _dot_nt(dff_ref[...], w_ref[0])

    row = pl.BlockSpec((tm, D), lambda i, p: (i, 0))
    vec = pl.BlockSpec((1, D), lambda i, p: (0, 0))
    return _call(
        body, name="down_bwd", grid=(s // tm, 4),
        in_specs=[row, vec, row, pl.BlockSpec((1, GU, D), lambda i, p: (p, 0, 0))],
        out_specs=[vec, row, pl.BlockSpec((1, tm, GU), lambda i, p: (p, i, 0))],
        out_shape=[_sds((1, D), F32), _sds((s, D), BF16), _sds((4, s, GU), F32)],
        compiler_params=_params(2),
    )(dx3, gate, ffn, w_down)


def _conv_act_bwd(u, conv_w, conv_b, da, tm, after):
    s = u.shape[2]

    def body(u_ref, halo_ref, cw_ref, cb_ref, da_ref, dy_ref, dcw_ref, dcb_ref):
        first = pl.program_id(1) == 0

        @pl.when(first)
        def _():
            dcw_ref[...] = jnp.zeros_like(dcw_ref)
            dcb_ref[...] = jnp.zeros_like(dcb_ref)

        taps, ys = [], []
        for side in range(2):
            uv = u_ref[side, 0]
            um1, um2 = _conv_taps(uv, halo_ref[side, 0], first, tm)
            cw = cw_ref[side, 0]
            taps.append((um2, um1, uv))
            ys.append(cw[2:3] * uv + cw[1:2] * um1 + cw[0:1] * um2 + cb_ref[side, 0])
        dav = da_ref[0]
        sg = _sigmoid(ys[0])
        dys = (dav * ys[1] * (sg * (1.0 + ys[0] * (1.0 - sg))), dav * (ys[0] * sg))
        for side in range(2):
            dy_ref[side, 0] = dys[side]
            dcb_ref[side, 0] += _colsum(dys[side])
            for j in range(3):
                dcw_ref[side, 0, j:j + 1, :] += _colsum(dys[side] * taps[side][j])

    u_spec, halo_spec, cw_spec, cb_spec = _conv_specs(tm)
    return _call_after(
        body, after, name="conv_act_bwd", grid=(4, s // tm),
        in_specs=[u_spec, halo_spec, cw_spec, cb_spec, pl.BlockSpec((1, tm, GU), lambda p, i: (p, i, 0))],
        out_specs=[u_spec, cw_spec, cb_spec],
        out_shape=[_sds((2, 4, s, GU), F32), _sds((2, 4, 3, GU), F32), _sds((2, 4, 1, GU), F32)],
        compiler_params=_params(2),
    )(u, u, conv_w, conv_b, da)


def _conv_transpose(dy, conv_w, tm):
    s = dy.shape[1]
    nt = s // tm

    def body(dy_ref, halo_ref, cw_ref, du_ref):
        keep = jnp.where(pl.program_id(1) == nt - 1, 0.0, 1.0)
        dv = dy_ref[0]
        rows = lax.broadcasted_iota(jnp.int32, (tm, 1), 0)
        h0 = halo_ref[0, 0:1, :] * keep
        h1 = halo_ref[0, 1:2, :] * keep
        dp1 = jnp.where(rows == tm - 1, h0, pltpu.roll(dv, tm - 1, 0))
        dp2 = jnp.where(rows == tm - 1, h1, jnp.where(rows == tm - 2, h0, pltpu.roll(dv, tm - 2, 0)))
        cw = cw_ref[0]
        du_ref[0] = (cw[2:3] * dv + cw[1:2] * dp1 + cw[0:1] * dp2).astype(BF16)

    blk = pl.BlockSpec((1, tm, GU), lambda g, i: (g, i, 0))
    return _call(
        body, name="conv_transpose", grid=(NDEV, nt),
        in_specs=[blk, pl.BlockSpec((1, 8, GU), lambda g, i: (g, jnp.minimum((i + 1) * (tm // 8), s // 8 - 1), 0)),
                  pl.BlockSpec((1, 3, GU), lambda g, i: (g, 0, 0))],
        out_specs=blk,
        out_shape=_sds((NDEV, s, GU), BF16),
        compiler_params=_params(2),
    )(dy, dy, conv_w)


def _wgrad(a3, b3, *, tk, name):
    ga, s, ka = a3.shape
    gb, _, nb = b3.shape
    groups = max(ga, gb)
    nk = s // tk

    def body(a_ref, b_ref, o_ref, acc):
        k = pl.program_id(1)

        @pl.when(k == 0)
        def _():
            acc[...] = jnp.zeros_like(acc)

        acc[...] += _dot_tn(a_ref[0], b_ref[0])

        @pl.when(k == nk - 1)
        def _():
            o_ref[0] = acc[...].astype(BF16)

    a_spec = pl.BlockSpec((1, tk, ka), (lambda g, k: (g, k, 0)) if ga > 1 else (lambda g, k: (0, k, 0)))
    b_spec = pl.BlockSpec((1, tk, nb), (lambda g, k: (g, k, 0)) if gb > 1 else (lambda g, k: (0, k, 0)))
    return _call(
        body, name=name, grid=(groups, nk),
        in_specs=[a_spec, b_spec], out_specs=pl.BlockSpec((1, ka, nb), lambda g, k: (g, 0, 0)),
        out_shape=_sds((groups, ka, nb), BF16),
        scratch_shapes=[pltpu.VMEM((ka, nb), F32)],
        compiler_params=_params(2),
    )(a3, b3)


def _wgrad_in(h1, dparts, tk):
    s = h1.shape[0]
    nk = s // tk

    def body(a_ref, *refs):
        d_refs, o_ref, acc = refs[:NT], refs[NT], refs[NT + 1]
        k = pl.program_id(0)

        @pl.when(k == 0)
        def _():
            acc[...] = jnp.zeros_like(acc)

        for j in range(NT):
            acc[:, WG * j:WG * (j + 1)] += _dot_tn(a_ref[...], d_refs[j][...])

        @pl.when(k == nk - 1)
        def _():
            o_ref[...] = acc[...].astype(BF16)

    return _call(
        body, name="wgrad_in", grid=(nk,),
        in_specs=[pl.BlockSpec((tk, D), lambda k: (k, 0))] + [pl.BlockSpec((tk, WG), lambda k: (k, 0))] * NT,
        out_specs=pl.BlockSpec((D, NT * WG), lambda k: (0, 0)),
        out_shape=_sds((D, NT * WG), BF16),
        scratch_shapes=[pltpu.VMEM((D, NT * WG), F32)],
        compiler_params=_params(1),
    )(h1, *dparts)


def _dgrad_in(dparts, w, x_in, scale, dx_up, tm, after):
    s = x_in.shape[0]

    def body(*refs):
        d_refs = refs[:NT]
        w_ref, x_ref, sc_ref, up_ref, dx_ref, dsc_ref, dsh_ref = refs[NT:]

        @pl.when(pl.program_id(0) == 0)
        def _():
            dsc_ref[...] = jnp.zeros_like(dsc_ref)
            dsh_ref[...] = jnp.zeros_like(dsh_ref)

        dh = _dot_nt(d_refs[0][...], w_ref[:, 0:WG])
        for j in range(1, NT):
            dh = dh + _dot_nt(d_refs[j][...], w_ref[:, WG * j:WG * (j + 1)])
        xv = x_ref[...]
        r = _rms(xv)
        nrm = xv * r
        dsh_ref[...] += _colsum(dh)
        dsc_ref[...] += _colsum(dh * nrm)
        dx_ref[...] = up_ref[...] + _rms_bwd(dh * (1.0 + sc_ref[...]), nrm, r)

    row = pl.BlockSpec((tm, D), lambda i: (i, 0))
    vec = pl.BlockSpec((1, D), lambda i: (0, 0))
    return _call_after(
        body, after, name="dgrad_in", grid=(s // tm,),
        in_specs=[pl.BlockSpec((tm, WG), lambda i: (i, 0))] * NT + [pl.BlockSpec((D, NT * WG), lambda i: (0, 0)), row, vec, row],
        out_specs=[row, vec, vec],
        out_shape=[_sds((s, D), F32), _sds((1, D), F32), _sds((1, D), F32)],
        compiler_params=_params(1),
    )(*dparts, w, x_in, scale, dx_up)


def _dgrad_norm_bwd(d3, w, x_in, scale, dx_up, *, tm, name, after):
    groups, s, n = d3.shape

    def body(d_ref, w_ref, x_ref, sc_ref, up_ref, dx_ref, dsc_ref, dsh_ref, acc):
        i, g = pl.program_id(0), pl.program_id(1)

        @pl.when((i == 0) & (g == 0))
        def _():
            dsc_ref[...] = jnp.zeros_like(dsc_ref)
            dsh_ref[...] = jnp.zeros_like(dsh_ref)

        part = _dot_nt(d_ref[0], w_ref[0])

        @pl.when(g == 0)
        def _():
            acc[...] = part

        @pl.when(g > 0)
        def _():
            acc[...] += part

        @pl.when(g == groups - 1)
        def _():
            dh = acc[...]
            xv = x_ref[...]
            r = _rms(xv)
            nrm = xv * r
            dsh_ref[...] += _colsum(dh)
            dsc_ref[...] += _colsum(dh * nrm)
            dx_ref[...] = up_ref[...] + _rms_bwd(dh * (1.0 + sc_ref[...]), nrm, r)

    row = pl.BlockSpec((tm, D), lambda i, g: (i, 0))
    vec = pl.BlockSpec((1, D), lambda i, g: (0, 0))
    return _call_after(
        body, after, name=name, grid=(s // tm, groups),
        in_specs=[pl.BlockSpec((1, tm, n), lambda i, g: (g, i, 0)), pl.BlockSpec((1, D, n), lambda i, g: (g, 0, 0)),
                  row, vec, row],
        out_specs=[row, vec, vec],
        out_shape=[_sds((s, D), F32), _sds((1, D), F32), _sds((1, D), F32)],
        scratch_shapes=[pltpu.VMEM((tm, D), F32)],
        compiler_params=_params(2),
    )(d3, w, x_in, scale, dx_up)


def _mix_out_bwd(dx2, mixed, gate, w_out, oa, ob, g, tm):
    s = dx2.shape[0]

    def body(dx_ref, mixed_ref, gate_ref, w_ref, oa_ref, ob_ref, g_ref, dgate_ref, dm_ref, doa_ref, dob_ref, dg_ref):
        @pl.when(pl.program_id(0) == 0)
        def _():
            dgate_ref[...] = jnp.zeros_like(dgate_ref)
            dg_ref[...] = jnp.zeros_like(dg_ref)
        dxv = dx_ref[...]
        dgate_ref[...] += _colsum(dxv * mixed_ref[...])
        dm_ref[...] = (dxv * gate_ref[...]).astype(BF16)
        dnab = _dot_nt(dm_ref[...], w_ref[...])
        for o_ref, do_ref, sl in ((oa_ref, doa_ref, slice(0, WG)), (ob_ref, dob_ref, slice(WG, D))):
            ov = o_ref[...]
            r = _rms(ov)
            nrm = ov * r
            dn = dnab[:, sl]
            dg_ref[:, sl] += _colsum(dn * nrm)
            do_ref[...] = _rms_bwd(dn * g_ref[:, sl], nrm, r)

    row = pl.BlockSpec((tm, D), lambda i: (i, 0))
    half = pl.BlockSpec((tm, WG), lambda i: (i, 0))
    vec = pl.BlockSpec((1, D), lambda i: (0, 0))
    return _call(
        body, name="mix_out_bwd", grid=(s // tm,),
        in_specs=[row, row, vec, pl.BlockSpec((D, D), lambda i: (0, 0)), half, half, vec],
        out_specs=[vec, row, half, half, vec],
        out_shape=[_sds((1, D), F32), _sds((s, D), BF16), _sds((s, WG), F32), _sds((s, WG), F32), _sds((1, D), F32)],
        compiler_params=_params(1),
    )(dx2, mixed, gate, w_out, oa, ob, g)


def _attn_a_bwd(proj, bias_tab, d_oa, after):
    s = proj.shape[1]
    nq = s // TQA

    def body(q_ref, k_ref, v_ref, b_ref, do_ref, dq_ref, dk_ref, dv_ref, db_ref, dk_acc, dv_acc):
        qi = pl.program_id(1)

        @pl.when(qi == 0)
        def _():
            dk_acc[...] = jnp.zeros_like(dk_acc)
            dv_acc[...] = jnp.zeros_like(dv_acc)
            db_ref[...] = jnp.zeros_like(db_ref)

        def tile(nk, off, kstart):
            for h in range(2):
                sl = slice(HD * h, HD * (h + 1))
                q = q_ref[0, :, sl]
                k = k_ref[0, pl.ds(kstart, nk), sl]
                v = v_ref[0, pl.ds(kstart, nk), sl]
                do = do_ref[:, sl].astype(BF16)
                sc = _dot_nt(q, k) * 0.125 + b_ref[h, :, off:off + nk]
                p = jnp.exp(sc - jnp.max(sc, axis=-1, keepdims=True))
                p = p / jnp.sum(p, axis=-1, keepdims=True)
                dp = _dot_nt(do, v)
                ds = p * (dp - jnp.sum(dp * p, axis=-1, keepdims=True))
                db_ref[h, :, off:off + nk] += ds
                dsb = (ds * 0.125).astype(BF16)
                dq_ref[:, sl] = _dot(dsb, k).astype(BF16)
                dk_acc[pl.ds(kstart, nk), sl] += _dot_tn(dsb, q)
                dv_acc[pl.ds(kstart, nk), sl] += _dot_tn(p.astype(BF16), do)

        _attn_a_cases(qi, tile)

        @pl.when(qi == nq - 1)
        def _():
            dk_ref[...] = dk_acc[...].astype(BF16)
            dv_ref[...] = dv_acc[...].astype(BF16)

    q_spec, k_spec, v_spec, b_spec = _attn_a_specs(s)
    blk = pl.BlockSpec((TQA, 128), lambda hp, qi: (qi, hp))
    col = pl.BlockSpec((s, 128), lambda hp, qi: (0, hp))
    return _call_after(
        body, after, name="attn_a_bwd", grid=(4, nq),
        in_specs=[q_spec, k_spec, v_spec, b_spec, blk],
        out_specs=[blk, col, col, b_spec],
        out_shape=[_sds((s, WG), BF16), _sds((s, WG), BF16), _sds((s, WG), BF16), _sds((NDEV, TQA, KWA), F32)],
        scratch_shapes=[pltpu.VMEM((s, 128), F32), pltpu.VMEM((s, 128), F32)],
        compiler_params=_params(2),
    )(proj, proj, proj, bias_tab, d_oa)


def _attn_b_bwd(proj, lsum, d_ob):
    s = proj.shape[1]
    nq = s // TQB
    nkb = s // TKB
    ndiag = TQB // TKB

    def body(q_ref, k_ref, v_ref, ls_ref, do_ref, dq_ref, dk_ref, dv_ref, dkt_acc, dvt_acc):
        qi = pl.program_id(1)
        q0 = qi * TQB

        @pl.when(qi == 0)
        def _():
            dkt_acc[...] = jnp.zeros_like(dkt_acc)
            dvt_acc[...] = jnp.zeros_like(dvt_acc)

        rows = lax.broadcasted_iota(jnp.int32, (TQB, TKB), 0)
        cols = lax.broadcasted_iota(jnp.int32, (TQB, TKB), 1)
        uj = lax.broadcasted_iota(jnp.int32, (TKB, TKB), 0)
        us = lax.broadcasted_iota(jnp.int32, (TKB, TKB), 1)
        prefix = jnp.where(uj <= us, 1.0, 0.0).astype(BF16)
        heads = [slice(HD * h, HD * (h + 1)) for h in range(2)]
        qs = [(q_ref[0, :, sl].astype(F32) * 0.125).astype(BF16) for sl in heads]
        dos = [do_ref[:, sl].astype(BF16) for sl in heads]
        ei = lax.broadcasted_iota(jnp.int32, (TQB, TQB), 0)
        ej = lax.broadcasted_iota(jnp.int32, (TQB, TQB), 1)
        eye = jnp.where(ei == ej, 1.0, 0.0).astype(BF16)
        qts = [_dot_tn(x, eye).astype(BF16) for x in qs]
        dots = [_dot_tn(x, eye).astype(BF16) for x in dos]
        stots = [ls_ref[0, :, HD * h:HD * h + 1] for h in range(2)]

        def tile(h, kb, diag, cl, cg, dq):
            sl = heads[h]
            kstart = pl.multiple_of(kb * TKB, TKB)
            k = k_ref[0, pl.ds(kstart, TKB), sl]
            v = v_ref[0, pl.ds(kstart, TKB), sl]
            lb, sp = _sb_logits(qs[h], k)
            if diag is not None:
                strict = rows > cols + diag * TKB
                sp = jnp.where(strict, sp, 0.0)
            pre = _dot2(sp, prefix) + cl
            a = jnp.exp(lb - stots[h] + pre)
            if diag is not None:
                a = jnp.where(strict, a, 0.0)
            gz = _dot_nt(dos[h], v) * a
            pg = _dot2(gz, prefix) + cg
            dl = gz - pg * jnp.exp(lb)
            if diag is not None:
                dl = jnp.where(strict, dl, 0.0)
            dlb = dl.astype(BF16)
            dkt_acc[h * nkb + kb] += _dot(qts[h], dlb)
            dvt_acc[h * nkb + kb] += _dot(dots[h], a.astype(BF16))
            return pre[:, TKB - 1:TKB], pg[:, TKB - 1:TKB], dq + _dot(dlb, k)

        state = [jnp.zeros((TQB, 1), F32), jnp.zeros((TQB, 1), F32), jnp.zeros((TQB, HD), F32)] * 2

        def step(i, st):
            st = list(st)
            for sub in range(KSTEP):
                for h in range(2):
                    st[3 * h:3 * h + 3] = tile(h, i * KSTEP + sub, None, *st[3 * h:3 * h + 3])
            return tuple(st)

        state = list(lax.fori_loop(0, q0 // (KSTEP * TKB), step, tuple(state)))
        for d in range(ndiag):
            for h in range(2):
                state[3 * h:3 * h + 3] = tile(h, q0 // TKB + d, d, *state[3 * h:3 * h + 3])
        for h in range(2):
            dq_ref[:, heads[h]] = (state[3 * h + 2] * 0.125).astype(BF16)

        @pl.when(qi == nq - 1)
        def _():
            eye64 = eye[:HD, :HD]
            for h in range(2):
                for kb in range(nkb):
                    rows_kb = slice(kb * TKB, (kb + 1) * TKB)
                    dk_ref[rows_kb, heads[h]] = _dot2_tn(dkt_acc[h * nkb + kb], eye64).astype(BF16)
                    dv_ref[rows_kb, heads[h]] = _dot2_tn(dvt_acc[h * nkb + kb], eye64).astype(BF16)

    q_spec, k_spec, v_spec = _attn_b_specs(s)
    blk = pl.BlockSpec((TQB, 128), lambda hp, qi: (qi, hp))
    col = pl.BlockSpec((s, 128), lambda hp, qi: (0, hp))
    return _call(
        body, name="attn_b_bwd", grid=(4, nq),
        in_specs=[q_spec, k_spec, v_spec, pl.BlockSpec((1, TQB, 128), lambda hp, qi: (hp, qi, 0)), blk],
        out_specs=[blk, col, col],
        out_shape=[_sds((s, WG), BF16)] * 3,
        scratch_shapes=[pltpu.VMEM((2 * nkb, HD, TKB), F32), pltpu.VMEM((2 * nkb, HD, TKB), F32)],
        compiler_params=_params(2),
    )(proj, proj, proj, lsum, d_ob)


def _bias_fold(dtab):
    def body(t_ref, d_ref, far_ref):
        acc = jnp.zeros((NDEV, TABW), F32)
        zpad = jnp.zeros((NDEV, TAB0), F32)
        for r in range(TQA):
            row = jnp.concatenate([zpad, t_ref[:, r, :]], axis=1)
            acc = acc + (pltpu.roll(row, TABW - r, 1) if r else row)
        d_ref[...] = acc
        lane = lax.broadcasted_iota(jnp.int32, (NDEV, TABW), 1)
        far = jnp.sum(jnp.where(lane < N_FAR, acc, 0.0), axis=1, keepdims=True)
        far_ref[...] = jnp.broadcast_to(far, (NDEV, 128))

    d_fpad, d_far = _call(
        body, name="bias_fold", grid=(1,),
        in_specs=[pl.BlockSpec((NDEV, TQA, KWA), lambda i: (0, 0, 0))],
        out_specs=[pl.BlockSpec((NDEV, TABW), lambda i: (0, 0)), pl.BlockSpec((NDEV, 128), lambda i: (0, 0))],
        out_shape=[_sds((NDEV, TABW), F32), _sds((NDEV, 128), F32)],
        compiler_params=_params(1),
    )(dtab)
    d_near = d_fpad[:, N_FAR:N_FAR + N_NEAR][:, ::-1]
    return jnp.concatenate([jnp.zeros((NDEV, REL_CLIP - CHUNK + 1), F32), d_near, d_far[:, :1]], axis=1)


def _adamw(w, g, m, v):
    m = B1 * m + (1.0 - B1) * g
    v = B2 * v + (1.0 - B2) * (g * g)
    m_hat = m / (1.0 - B1 ** STEP)
    v_hat = v / (1.0 - B2 ** STEP)
    delta = -LR * (m_hat / (jnp.sqrt(v_hat) + AEPS) + WD * w)
    return delta, m, v


def _adamw_big(recv0, recv1, w, m, v):
    _, rows, cols = recv0.shape
    tr = rows if rows <= 512 else 256
    nt = rows // tr

    def body(r0_ref, r1_ref, w_ref, m_ref, v_ref, g_ref, d_ref, nm_ref, nv_ref):
        def update(r_ref):
            g = r_ref[0].astype(F32)
            for p in range(1, NDEV):
                g = g + r_ref[p].astype(F32)
            delta, nm, nv = _adamw(w_ref[0], g, m_ref[0], v_ref[0])
            g_ref[0], d_ref[0], nm_ref[0], nv_ref[0] = g, delta, nm, nv

        @pl.when(pl.program_id(0) == 0)
        def _():
            update(r0_ref)

        @pl.when(pl.program_id(0) == 1)
        def _():
            update(r1_ref)

    blk = pl.BlockSpec((1, tr, cols), lambda l, i: (l, i, 0))
    r0_spec = pl.BlockSpec((NDEV, tr, cols), lambda l, i: (0, jnp.where(l == 0, i, nt - 1), 0))
    r1_spec = pl.BlockSpec((NDEV, tr, cols), lambda l, i: (0, jnp.where(l == 1, i, 0), 0))
    return _call(
        body, name="adamw_big", grid=(2, nt),
        in_specs=[r0_spec, r1_spec, blk, blk, blk],
        out_specs=[blk] * 4,
        out_shape=[_sds((2, rows, cols), F32)] * 4,
        compiler_params=_params(2),
    )(recv0, recv1, w, m, v)


def _adamw_w_ada(cact_t, dmod, w, m, v):
    tr = 256

    def body(c_ref, dm_ref, w_ref, m_ref, v_ref, g_ref, d_ref, nm_ref, nv_ref):
        g = c_ref[:, 0:1] * dm_ref[0, 0:1, :]
        for b in range(1, NDEV):
            g = g + c_ref[:, b:b + 1] * dm_ref[0, b:b + 1, :]
        delta, nm, nv = _adamw(w_ref[0], g, m_ref[0], v_ref[0])
        g_ref[0], d_ref[0], nm_ref[0], nv_ref[0] = g, delta, nm, nv

    blk = pl.BlockSpec((1, tr, 768), lambda l, i: (l, i, 0))
    return _call(
        body, name="adamw_w_ada", grid=(2, D // tr),
        in_specs=[pl.BlockSpec((tr, NDEV), lambda l, i: (i, 0)), pl.BlockSpec((1, NDEV, 768), lambda l, i: (l, 0, 0)),
                  blk, blk, blk],
        out_specs=[blk] * 4,
        out_shape=[_sds((2, D, 768), F32)] * 4,
        compiler_params=_params(2),
    )(cact_t, dmod, w, m, v)


def _adamw_small(gath, w, m, v):
    rows = gath.shape[1]

    def body(r_ref, w_ref, m_ref, v_ref, g_ref, d_ref, nm_ref, nv_ref):
        g = r_ref[0]
        for p in range(1, NDEV):
            g = g + r_ref[p]
        delta, nm, nv = _adamw(w_ref[...], g, m_ref[...], v_ref[...])
        g_ref[...], d_ref[...], nm_ref[...], nv_ref[...] = g, delta, nm, nv

    blk = pl.BlockSpec((rows, D), lambda i: (0, 0))
    return _call(
        body, name="adamw_small", grid=(1,),
        in_specs=[pl.BlockSpec((NDEV, rows, D), lambda i: (0, 0, 0)), blk, blk, blk],
        out_specs=[blk] * 4,
        out_shape=[_sds((rows, D), F32)] * 4,
        compiler_params=_params(1),
    )(gath, w, m, v)


_PACK = (("b_ada", 2 * 6 * D), ("rel_bias", 2 * 8 * 257), ("g_a", 2 * WG), ("g_b", 2 * WG),
         ("conv_b", 2 * 2 * DFF), ("final_g", D), ("conv_w", 2 * NDEV * 3 * GU))


def _pack(parts):
    rows = []
    for name, size in _PACK:
        flat = parts[name].reshape(-1).astype(F32)
        assert flat.shape[0] == size, (name, flat.shape)
        rows.append(jnp.pad(flat, (0, -size % D)))
    out = jnp.concatenate(rows).reshape(-1, D)
    return jnp.pad(out, ((0, -out.shape[0] % 8), (0, 0)))


def _unpack(packed):
    flat = packed.reshape(-1)
    out, pos = {}, 0
    for name, size in _PACK:
        out[name] = flat[pos:pos + size]
        pos += size + (-size % D)
    return out


def kernel(x, c, w_ada, b_ada, w_in, rel_bias, g_a, g_b, w_out, w_up, conv_w, conv_b, w_down, final_g, loss_target, m_w_ada, m_b_ada, m_w_in, m_rel_bias, m_g_a, m_g_b, m_w_out, m_w_up, m_conv_w, m_conv_b, m_w_down, m_final_g, v_w_ada, v_b_ada, v_w_in, v_rel_bias, v_g_a, v_g_b, v_w_out, v_w_up, v_conv_w, v_conv_b, v_w_down, v_final_g):
    s = x.shape[1]
    assert s % TQA == 0 and s >= KWA and s % 512 == 0
    tm = 512
    me = 4 * lax.axis_index("x") + 2 * lax.axis_index("y") + lax.axis_index("c")
    xs = x.reshape(s, D)
    target = loss_target.reshape(s, D)

    first = jnp.concatenate([c, jnp.pad(conv_w.reshape(2 * 3, GU), ((0, 1), (0, D - GU)))])
    first_all = _small_allgather(first, "gather_c_conv_w")
    c_all = first_all[:, 0, :]
    cw_all = first_all[:, 1:7, :GU].reshape(NDEV, 2, 3, GU)

    b_sl = lax.dynamic_slice(b_ada, (0, me * 768), (2, 768)).reshape(2, 1, 768)
    mod_part, cact = _mod_fwd(c_all, w_ada, b_sl)
    mod_all = _small_allgather(mod_part.reshape(2 * NDEV, 768), "gather_mod")

    shards = {"in": w_in, "out": w_out, "up": w_up, "down": w_down}
    order = [(kind, l) for l in range(2) for kind in _KINDS]
    mod_all, *srcs = lax.optimization_barrier((mod_all, *[shards[kind][l].astype(BF16) for kind, l in order]))
    gather_started = dict(zip(order, _exchange_start([kind for kind, _ in order], True, srcs, "weights_gather_start")))

    def gathered(kind, l, after):
        return _exchange_wait([kind], True, [gather_started[kind, l]], after, f"weights_gather_wait_{kind}{l}")[0]

    mod_all = mod_all.reshape(NDEV, 2, NDEV, 768)
    mod_me = lax.dynamic_index_in_dim(mod_all, me, axis=2, keepdims=False)
    mod = jnp.transpose(mod_me, (1, 0, 2)).reshape(2, 6, 1, D)

    saved = []
    xl = xs
    for l in range(2):
        sh_mix, sc_mix, gt_mix, sh_ffn, sc_ffn, gt_ffn = (mod[l, j] for j in range(6))
        cw = cw_all[:, l].reshape(2, 4, 3, GU)
        cb = conv_b[l].reshape(2, 4, 1, GU)
        gvec = jnp.concatenate([g_a[l], g_b[l]]).reshape(1, D)
        tab = _bias_table(rel_bias[l])

        wi = gathered("in", l, xl if l else mod)
        h1, proj = _nm_matmul(xl, sh_mix, sc_mix, wi, two_d=True, n=WG, groups=NT, out_dtype=BF16, tm=tm,
                              name="norm_proj")
        oa = _attn_a_fwd(proj, tab)
        ob, lsum = _attn_b_fwd(proj)
        wo = gathered("out", l, ob)
        nab, mixed, x2 = _mix_out(oa, ob, gvec, wo, xl, gt_mix, tm)
        wu = gathered("up", l, x2)
        h2, u = _nm_matmul(x2, sh_ffn, sc_ffn, wu, two_d=False, n=GU, groups=NDEV, out_dtype=F32, tm=tm,
                           name="norm_up")
        u = u.reshape(2, 4, s, GU)
        a = _conv_act(u, cw, cb, tm)
        wd4 = gathered("down", l, a).reshape(4, GU, D)
        ffn, x3 = _down(a, wd4, x2, gt_ffn, tm)
        saved.append(dict(x=xl, h1=h1, proj=proj, oa=oa, ob=ob, lsum=lsum, nab=nab, mixed=mixed, x2=x2, h2=h2, u=u,
                          a=a, ffn=ffn, cw=cw, cb=cb, gvec=gvec, tab=tab, wd4=wd4, wi=wi, wo=wo, wu=wu))
        xl = x3

    loss_part, dx, d_final_g = _final_loss(xl, final_g.reshape(1, D), target, tm)
    loss = lax.psum(loss_part[0, 0], ("x", "y", "c"))

    sent = {}
    small = {"b_ada": [None, None], "rel_bias": [None, None], "g_a": [None, None], "g_b": [None, None],
             "conv_b": [None, None], "conv_w": [None, None]}

    def send(kind, l, grad):
        started, token = _exchange_start([kind], False, [grad], f"grads_start_{kind}{l}", with_token=True)
        sent[kind, l] = started[0]
        return token

    for l in (1, 0):
        sv = saved[l]
        sh_mix, sc_mix, gt_mix, sh_ffn, sc_ffn, gt_ffn = (mod[l, j] for j in range(6))
        d_gt_ffn, dff, da = _down_bwd(dx, gt_ffn, sv["ffn"], sv["wd4"], tm)
        tok = send("down", l, _wgrad(sv["a"], dff.reshape(1, s, D), tk=tm, name="wgrad_down").reshape(DFF, D))
        dy, d_cw, d_cb = _conv_act_bwd(sv["u"], sv["cw"], sv["cb"], da, tm, tok)
        du = _conv_transpose(dy.reshape(NDEV, s, GU), sv["cw"].reshape(NDEV, 3, GU), tm)
        tok = send("up", l, _wgrad(sv["h2"].reshape(1, s, D), du, tk=tm, name="wgrad_up"))
        dx2, d_sc_ffn, d_sh_ffn = _dgrad_norm_bwd(du, sv["wu"], sv["x2"], sc_ffn, dx, tm=tm, name="dgrad_up", after=tok)
        d_gt_mix, dmixed, d_oa, d_ob, d_g = _mix_out_bwd(dx2, sv["mixed"], gt_mix, sv["wo"], sv["oa"], sv["ob"],
                                                         sv["gvec"], tm)
        tok = send("out", l, _wgrad(sv["nab"].reshape(1, s, D), dmixed.reshape(1, s, D), tk=tm,
                                    name="wgrad_out").reshape(D, D))
        dqa, dka, dva, d_tab = _attn_a_bwd(sv["proj"], sv["tab"], d_oa, tok)
        dqb, dkb, dvb = _attn_b_bwd(sv["proj"], sv["lsum"], d_ob)
        dparts = (dqa, dka, dva, dqb, dkb, dvb)
        tok = send("in", l, _wgrad_in(sv["h1"], dparts, tm))
        dx, d_sc_mix, d_sh_mix = _dgrad_in(dparts, sv["wi"], sv["x"], sc_mix, dx2, tm, tok)
        small["b_ada"][l] = jnp.concatenate([d_sh_mix, d_sc_mix, d_gt_mix, d_sh_ffn, d_sc_ffn, d_gt_ffn], axis=1)
        small["rel_bias"][l] = _bias_fold(d_tab)
        small["g_a"][l], small["g_b"][l] = d_g[:, :WG], d_g[:, WG:]
        small["conv_b"][l] = d_cb
        small["conv_w"][l] = d_cw.reshape(NDEV, 3, GU)
    grad_x = dx.reshape(1, s, D)

    big = {}
    for kind, (w, m, v) in (("down", (w_down, m_w_down, v_w_down)), ("up", (w_up, m_w_up, v_w_up)),
                            ("out", (w_out, m_w_out, v_w_out)), ("in", (w_in, m_w_in, v_w_in))):
        recv0, recv1 = _exchange_wait([kind, kind], False, [sent[kind, 0], sent[kind, 1]], dx, f"grads_wait_{kind}")
        big[kind] = _adamw_big(recv0, recv1, w, m, v)

    contrib = {k: jnp.stack(vs) for k, vs in small.items()}
    contrib["final_g"] = d_final_g
    gath = _small_allgather(_pack(contrib), "gather_small_grads")

    def place_conv_w(t):
        return lax.dynamic_update_slice(jnp.zeros((2, NDEV, 3, GU), F32), t.reshape(2, 1, 3, GU), (0, me, 0, 0))

    def packed_params(b, rb, ga, gb, cb_, fg, cw_):
        return _pack({"b_ada": b, "rel_bias": rb, "g_a": ga, "g_b": gb, "conv_b": cb_, "final_g": fg,
                      "conv_w": place_conv_w(cw_)})

    sm = _adamw_small(gath,
                      packed_params(b_ada, rel_bias, g_a, g_b, conv_b, final_g, conv_w),
                      packed_params(m_b_ada, m_rel_bias, m_g_a, m_g_b, m_conv_b, m_final_g, m_conv_w),
                      packed_params(v_b_ada, v_rel_bias, v_g_a, v_g_b, v_conv_b, v_final_g, v_conv_w))
    sm = [_unpack(t) for t in sm]

    dmod_all = gath[:, :12, :].reshape(NDEV, 2, 6 * D)
    dmod_sl = jnp.transpose(lax.dynamic_slice(dmod_all, (0, 0, me * 768), (NDEV, 2, 768)), (1, 0, 2))
    ada = _adamw_w_ada(cact.T, dmod_sl, w_ada, m_w_ada, v_w_ada)

    def small_out(j, name):
        t = sm[j][name]
        if name == "b_ada":
            return t.reshape(2, 6 * D)
        if name == "rel_bias":
            return t.reshape(2, 8, 257)
        if name in ("g_a", "g_b"):
            return t.reshape(2, WG)
        if name == "conv_b":
            return t.reshape(2, 2 * DFF)
        if name == "final_g":
            return t.reshape(D)
        t = t.reshape(2, NDEV, 3, GU)
        return lax.dynamic_index_in_dim(t, me, axis=1, keepdims=False)

    def group(j):
        return (ada[j], small_out(j, "b_ada"), big["in"][j], small_out(j, "rel_bias"), small_out(j, "g_a"),
                small_out(j, "g_b"), big["out"][j], big["up"][j], small_out(j, "conv_w"), small_out(j, "conv_b"),
                big["down"][j], small_out(j, "final_g"))

    return (loss, grad_x, *group(0), *group(1), *group(2), *group(3))
```

```python
import jax
import jax.numpy as jnp
from jax import lax
from jax.experimental import pallas as pl
from jax.experimental.pallas import tpu as pltpu

F32, BF16 = jnp.float32, jnp.bfloat16
MESH_ID = pl.DeviceIdType.MESH
NDEV = 8
D = 1024
HD = 64
WG = 512
NT = 6
GU = 704
DFF = 2816
CHUNK, NPREV, REL_CLIP = 64, 8, 128
BAND = (NPREV + 1) * CHUNK
EPS = 1e-6
NEG = -1e30
TQA = 256
KWA = TQA + NPREV * CHUNK
TABW = 1024
TAB0 = TABW - KWA
N_FAR = TAB0 + NPREV * CHUNK - REL_CLIP + 1
N_NEAR = REL_CLIP + CHUNK - 1
TQB, TKB = 512, 128
KSTEP = 4
LR, B1, B2, AEPS, WD, STEP = 0.001, 0.9, 0.999, 1e-08, 0.01, 10
VMEM_MB = 56


def _call(body, **kw):
    return pl.pallas_call(body, **kw)


def _call_after(body, after, **kw):
    n_in = len(kw["in_specs"])
    kw["in_specs"] = list(kw["in_specs"]) + [pl.BlockSpec(memory_space=pl.ANY)]

    def tied(*refs):
        body(*refs[:n_in], *refs[n_in + 1:])

    call = _call(tied, **kw)
    return lambda *args: call(*args, after)


def _params(n_axes):
    return pltpu.CompilerParams(dimension_semantics=("arbitrary",) * n_axes, vmem_limit_bytes=VMEM_MB << 20)


def _dot(a, b):
    return jnp.dot(a, b, preferred_element_type=F32)


def _dot_nt(a, b):
    return lax.dot_general(a, b, (((1,), (1,)), ((), ())), preferred_element_type=F32)


def _dot_tn(a, b):
    return lax.dot_general(a, b, (((0,), (0,)), ((), ())), preferred_element_type=F32)


def _dot2(x, u):
    hi = x.astype(BF16)
    lo = (x - hi.astype(F32)).astype(BF16)
    return _dot(hi, u) + _dot(lo, u)


def _dot2_tn(x, u):
    hi = x.astype(BF16)
    lo = (x - hi.astype(F32)).astype(BF16)
    return _dot_tn(hi, u) + _dot_tn(lo, u)


def _rms(x):
    return lax.rsqrt(jnp.mean(x * x, axis=-1, keepdims=True) + EPS)


def _rms_bwd(dn, n, r):
    return r * (dn - n * jnp.mean(dn * n, axis=-1, keepdims=True))


def _colsum(x):
    return jnp.sum(x, axis=0, keepdims=True)


def _sigmoid(x):
    return 1.0 / (1.0 + jnp.exp(-x))


def _sds(shape, dtype):
    return jax.ShapeDtypeStruct(shape, dtype)


def _place():
    x, y, c = lax.axis_index("x"), lax.axis_index("y"), lax.axis_index("c")
    return x, y, c, 4 * x + 2 * y + c


def _peer(x, y, c, k):
    px = 1 - x if k & 4 else x
    py = 1 - y if k & 2 else y
    pc = 1 - c if k & 1 else c
    return (px, py, pc), 4 * px + 2 * py + pc


def _small_allgather(v, name):
    rows, cols = v.shape

    def body(v_ref, out_ref, send_sems, recv_sems, local_sem):
        x, y, c, me = _place()
        mine = pltpu.make_async_copy(v_ref, out_ref.at[me], local_sem)
        mine.start()
        sends = []
        for k in range(1, NDEV):
            peer, _ = _peer(x, y, c, k)
            cp = pltpu.make_async_remote_copy(v_ref, out_ref.at[me], send_sems.at[k - 1], recv_sems.at[k - 1],
                                              device_id=peer, device_id_type=MESH_ID)
            cp.start()
            sends.append(cp)
        for k in range(1, NDEV):
            peer, pidx = _peer(x, y, c, k)
            pltpu.make_async_remote_copy(v_ref, out_ref.at[pidx], send_sems.at[k - 1], recv_sems.at[k - 1],
                                         device_id=peer, device_id_type=MESH_ID).wait_recv()
        for cp in sends:
            cp.wait_send()
        mine.wait()

    return _call(
        body, name=name,
        out_shape=_sds((NDEV, rows, cols), F32),
        in_specs=[pl.BlockSpec(memory_space=pltpu.VMEM)],
        out_specs=pl.BlockSpec(memory_space=pltpu.VMEM),
        scratch_shapes=[pltpu.SemaphoreType.DMA((NDEV - 1,)), pltpu.SemaphoreType.DMA((NDEV - 1,)),
                        pltpu.SemaphoreType.DMA],
    )(v)


def _shard_view(ref, kind, p):
    if kind == "in":
        return ref.at[:, pl.ds(pl.multiple_of(p * 384, 128), 384)]
    if kind == "out":
        return ref.at[pl.ds(pl.multiple_of(p * 128, 128), 128), :]
    if kind == "up":
        return ref.at[p]
    if kind == "down":
        return ref.at[pl.ds(pl.multiple_of(p * 352, 16), 352), :]
    raise ValueError(kind)


_KINDS = ("in", "out", "up", "down")
_FULL_SHAPES = {"in": (D, 3 * D), "out": (D, D), "up": (NDEV, D, GU), "down": (DFF, D)}
_SHARD_SHAPES = {"in": (D, 384), "out": (128, D), "up": (D, GU), "down": (352, D)}


_HBM = pl.BlockSpec(memory_space=pltpu.HBM)
_SEM = pl.BlockSpec(memory_space=pltpu.SEMAPHORE)
_EFFECT = pltpu.SideEffectType.DATAFLOW_SIDE_EFFECTING
_SEM_SHAPES = (pltpu.SemaphoreType.DMA((NDEV - 1,)), pltpu.SemaphoreType.DMA((NDEV - 1,)), pltpu.SemaphoreType.DMA(()))


def _hbm(a):
    return pltpu.with_memory_space_constraint(a, pltpu.HBM)


def _exchange_copies(kind, gather, src, land, sems):
    send_sems, recv_sems, local_sem = sems
    x, y, c, me = _place()

    def ends(p_dst, p_from):
        if gather:
            return src, _shard_view(land, kind, me), _shard_view(land, kind, p_from)
        return _shard_view(src, kind, p_dst), land.at[me], land.at[p_from]

    s_me, d_me, _ = ends(me, me)
    local = pltpu.make_async_copy(s_me, d_me, local_sem)
    sends, arrivals = [], []
    for k in range(1, NDEV):
        peer, pidx = _peer(x, y, c, k)
        s_k, d_k, from_k = ends(pidx, pidx)
        sends.append(pltpu.make_async_remote_copy(s_k, d_k, send_sems.at[k - 1], recv_sems.at[k - 1],
                                                  device_id=peer, device_id_type=MESH_ID))
        arrivals.append(pltpu.make_async_remote_copy(s_k, from_k, send_sems.at[k - 1], recv_sems.at[k - 1],
                                                     device_id=peer, device_id_type=MESH_ID))
    return local, sends, arrivals


def _exchange_start(kinds, gather, srcs, name, with_token=False):
    n = len(kinds)
    lands = [lax.empty(_FULL_SHAPES[kd] if gather else (NDEV,) + _SHARD_SHAPES[kd], BF16) for kd in kinds]

    def body(*refs):
        ins, sems = refs[:2 * n], refs[2 * n:5 * n]
        for j, kd in enumerate(kinds):
            local, sends, _ = _exchange_copies(kd, gather, ins[j], ins[n + j], sems[3 * j:3 * j + 3])
            local.start()
            for cp in sends:
                cp.start()
        if with_token:
            token = refs[7 * n]
            token[...] = jnp.zeros_like(token)

    out_shape = list(_SEM_SHAPES) * n
    out_shape += [pltpu.HBM(a.shape, a.dtype) for a in srcs] + [pltpu.HBM(a.shape, a.dtype) for a in lands]
    out_specs = [_SEM] * (3 * n) + [_HBM] * (2 * n)
    if with_token:
        out_shape.append(_sds((8, 128), F32))
        out_specs.append(pl.BlockSpec(memory_space=pltpu.VMEM))
    outs = _call(
        body, name=name, out_shape=out_shape,
        in_specs=[_HBM] * (2 * n), out_specs=out_specs,
        input_output_aliases={i: 3 * n + i for i in range(2 * n)},
        compiler_params=pltpu.CompilerParams(has_side_effects=_EFFECT),
    )(*[_hbm(a) for a in srcs], *[_hbm(a) for a in lands])
    per_tensor = [(tuple(outs[3 * j:3 * j + 3]), outs[3 * n + j], outs[4 * n + j]) for j in range(n)]
    return (per_tensor, outs[5 * n]) if with_token else per_tensor


def _exchange_wait(kinds, gather, started, after, name):
    n = len(kinds)

    def body(*refs):
        ins, sems = refs[:2 * n], refs[2 * n:5 * n]
        for j, kd in enumerate(kinds):
            local, sends, arrivals = _exchange_copies(kd, gather, ins[j], ins[n + j], sems[3 * j:3 * j + 3])
            local.wait()
            for cp in arrivals:
                cp.wait_recv()
            for cp in sends:
                cp.wait_send()

    srcs = [st[1] for st in started]
    lands = [st[2] for st in started]
    sems = [sm for st in started for sm in st[0]]
    outs = _call(
        body, name=name,
        out_shape=[pltpu.HBM(a.shape, a.dtype) for a in srcs] + [pltpu.HBM(a.shape, a.dtype) for a in lands],
        in_specs=[_HBM] * (2 * n) + [_SEM] * (3 * n) + [pl.BlockSpec(memory_space=pl.ANY)],
        out_specs=[_HBM] * (2 * n),
        input_output_aliases={i: i for i in range(2 * n)},
        compiler_params=pltpu.CompilerParams(has_side_effects=_EFFECT),
    )(*srcs, *lands, *sems, after)
    return outs[n:]


def _mod_fwd(c_all, w_ada, b_sl):
    def body(c_ref, w_ref, b_ref, mod_ref, cact_ref):
        cv = c_ref[...]
        ca = cv * _sigmoid(cv)
        cact_ref[...] = ca
        mod_ref[0] = _dot(ca.astype(BF16), w_ref[0].astype(BF16)) + b_ref[0]

    return _call(
        body, name="mod_fwd", grid=(2,),
        in_specs=[pl.BlockSpec((NDEV, D), lambda l: (0, 0)), pl.BlockSpec((1, D, 768), lambda l: (l, 0, 0)),
                  pl.BlockSpec((1, 1, 768), lambda l: (l, 0, 0))],
        out_specs=[pl.BlockSpec((1, NDEV, 768), lambda l: (l, 0, 0)), pl.BlockSpec((NDEV, D), lambda l: (0, 0))],
        out_shape=[_sds((2, NDEV, 768), F32), _sds((NDEV, D), F32)],
        compiler_params=_params(1),
    )(c_all, w_ada, b_sl)


def _nm_matmul(x, shift, scale, w, *, two_d, n, groups, out_dtype, tm, name):
    s = x.shape[0]

    def body(x_ref, sh_ref, sc_ref, w_ref, h_ref, o_ref):
        @pl.when(pl.program_id(1) == 0)
        def _():
            xv = x_ref[...]
            h_ref[...] = ((xv * _rms(xv)) * (1.0 + sc_ref[...]) + sh_ref[...]).astype(BF16)
        wv = w_ref[...] if two_d else w_ref[0]
        o_ref[0] = _dot(h_ref[...], wv).astype(out_dtype)

    vec = pl.BlockSpec((1, D), lambda i, g: (0, 0))
    w_spec = pl.BlockSpec((D, n), lambda i, g: (0, g)) if two_d else pl.BlockSpec((1, D, n), lambda i, g: (g, 0, 0))
    return _call(
        body, name=name, grid=(s // tm, groups),
        in_specs=[pl.BlockSpec((tm, D), lambda i, g: (i, 0)), vec, vec, w_spec],
        out_specs=[pl.BlockSpec((tm, D), lambda i, g: (i, 0)), pl.BlockSpec((1, tm, n), lambda i, g: (g, i, 0))],
        out_shape=[_sds((s, D), BF16), _sds((groups, s, n), out_dtype)],
        compiler_params=_params(2),
    )(x, shift, scale, w)


def _bias_table(rel_bias):
    far = jnp.broadcast_to(rel_bias[:, 2 * REL_CLIP:], (NDEV, N_FAR))
    near = rel_bias[:, 2 * REL_CLIP - 1:REL_CLIP - CHUNK:-1]
    fpad = jnp.concatenate([far, near, jnp.zeros((NDEV, TABW - N_FAR - N_NEAR), F32)], axis=1)

    def body(f_ref, o_ref):
        t = pltpu.roll(jnp.broadcast_to(f_ref[0], (TQA, TABW)), 0, 1, stride=1, stride_axis=0)[:, TAB0:]
        rows = lax.broadcasted_iota(jnp.int32, (TQA, KWA), 0)
        cols = lax.broadcasted_iota(jnp.int32, (TQA, KWA), 1)
        first = jnp.bitwise_and(rows, -CHUNK)
        o_ref[0] = jnp.where((cols >= first) & (cols < first + BAND), t, NEG)

    return _call(
        body, name="bias_table", grid=(NDEV,),
        in_specs=[pl.BlockSpec((1, 1, TABW), lambda h: (h, 0, 0))],
        out_specs=pl.BlockSpec((1, TQA, KWA), lambda h: (h, 0, 0)),
        out_shape=_sds((NDEV, TQA, KWA), F32),
        compiler_params=_params(1),
    )(fpad.reshape(NDEV, 1, TABW))


def _attn_a_cases(qi, tile):
    @pl.when(qi == 0)
    def _():
        tile(TQA, 2 * TQA, 0)

    @pl.when(qi == 1)
    def _():
        tile(2 * TQA, TQA, 0)

    @pl.when(qi >= 2)
    def _():
        tile(KWA, 0, pl.multiple_of((qi - 2) * TQA, TQA))


def _attn_a_specs(s):
    q_spec = pl.BlockSpec((1, TQA, 128), lambda hp, qi: (0, qi, hp))
    k_spec = pl.BlockSpec((1, s, 128), lambda hp, qi: (1, 0, hp))
    v_spec = pl.BlockSpec((1, s, 128), lambda hp, qi: (2, 0, hp))
    b_spec = pl.BlockSpec((2, TQA, KWA), lambda hp, qi: (hp, 0, 0))
    return q_spec, k_spec, v_spec, b_spec


def _attn_a_fwd(proj, bias_tab):
    s = proj.shape[1]

    def body(q_ref, k_ref, v_ref, b_ref, o_ref):
        def tile(nk, off, kstart):
            for h in range(2):
                sl = slice(HD * h, HD * (h + 1))
                q = q_ref[0, :, sl]
                k = k_ref[0, pl.ds(kstart, nk), sl]
                v = v_ref[0, pl.ds(kstart, nk), sl]
                sc = _dot_nt(q, k) * 0.125 + b_ref[h, :, off:off + nk]
                p = jnp.exp(sc - jnp.max(sc, axis=-1, keepdims=True))
                den = jnp.sum(p, axis=-1, keepdims=True)
                o_ref[:, sl] = _dot(p.astype(BF16), v) / den

        _attn_a_cases(pl.program_id(1), tile)

    q_spec, k_spec, v_spec, b_spec = _attn_a_specs(s)
    return _call(
        body, name="attn_a_fwd", grid=(4, s // TQA),
        in_specs=[q_spec, k_spec, v_spec, b_spec],
        out_specs=pl.BlockSpec((TQA, 128), lambda hp, qi: (qi, hp)),
        out_shape=_sds((s, WG), F32),
        compiler_params=_params(2),
    )(proj, proj, proj, bias_tab)


def _sb_logits(q_scaled, k):
    lg = _dot_nt(q_scaled, k)
    sp = jnp.maximum(lg, 0.0) + jnp.log(1.0 + jnp.exp(-jnp.abs(lg)))
    return lg - sp, sp


def _attn_b_specs(s):
    q_spec = pl.BlockSpec((1, TQB, 128), lambda hp, qi: (3, qi, hp))
    k_spec = pl.BlockSpec((1, s, 128), lambda hp, qi: (4, 0, hp))
    v_spec = pl.BlockSpec((1, s, 128), lambda hp, qi: (5, 0, hp))
    return q_spec, k_spec, v_spec


def _attn_b_fwd(proj):
    s = proj.shape[1]
    ndiag = TQB // TKB

    def body(q_ref, k_ref, v_ref, o_ref, ls_ref):
        q0 = pl.program_id(1) * TQB
        rows = lax.broadcasted_iota(jnp.int32, (TQB, TKB), 0)
        cols = lax.broadcasted_iota(jnp.int32, (TQB, TKB), 1)
        uj = lax.broadcasted_iota(jnp.int32, (TKB, TKB), 0)
        us = lax.broadcasted_iota(jnp.int32, (TKB, TKB), 1)
        suffix = jnp.where(uj >= us, 1.0, 0.0).astype(BF16)
        heads = [slice(HD * h, HD * (h + 1)) for h in range(2)]
        qs = [(q_ref[0, :, sl].astype(F32) * 0.125).astype(BF16) for sl in heads]

        def tile(h, kstart, diag, carry, acc):
            k = k_ref[0, pl.ds(kstart, TKB), heads[h]]
            v = v_ref[0, pl.ds(kstart, TKB), heads[h]]
            lb, sp = _sb_logits(qs[h], k)
            if diag is not None:
                strict = rows > cols + diag * TKB
                sp = jnp.where(strict, sp, 0.0)
            csum = _dot2(sp, suffix) + carry
            w = jnp.exp(lb - csum + sp)
            if diag is not None:
                w = jnp.where(strict, w, 0.0)
            return csum[:, 0:1], acc + _dot(w.astype(BF16), v)

        state = [jnp.zeros((TQB, 1), F32), jnp.zeros((TQB, HD), F32)] * 2
        for d in range(ndiag - 1, -1, -1):
            for h in range(2):
                state[2 * h:2 * h + 2] = tile(h, pl.multiple_of(q0 + d * TKB, TKB), d, *state[2 * h:2 * h + 2])
        nsteps = q0 // (KSTEP * TKB)

        def step(i, st):
            st = list(st)
            base = (nsteps - 1 - i) * (KSTEP * TKB)
            for sub in range(KSTEP - 1, -1, -1):
                for h in range(2):
                    st[2 * h:2 * h + 2] = tile(h, pl.multiple_of(base + sub * TKB, TKB), None, *st[2 * h:2 * h + 2])
            return tuple(st)

        state = lax.fori_loop(0, nsteps, step, tuple(state))
        for h in range(2):
            o_ref[:, heads[h]] = state[2 * h + 1]
            ls_ref[0, :, heads[h]] = jnp.broadcast_to(state[2 * h], (TQB, HD))

    q_spec, k_spec, v_spec = _attn_b_specs(s)
    return _call(
        body, name="attn_b_fwd", grid=(4, s // TQB),
        in_specs=[q_spec, k_spec, v_spec],
        out_specs=[pl.BlockSpec((TQB, 128), lambda hp, qi: (qi, hp)),
                   pl.BlockSpec((1, TQB, 128), lambda hp, qi: (hp, qi, 0))],
        out_shape=[_sds((s, WG), F32), _sds((4, s, 128), F32)],
        compiler_params=_params(2),
    )(proj, proj, proj)


def _mix_out(oa, ob, g, w_out, x, gate, tm):
    s = x.shape[0]

    def body(oa_ref, ob_ref, g_ref, w_ref, x_ref, gate_ref, nab_ref, mixed_ref, x2_ref):
        a, b = oa_ref[...], ob_ref[...]
        nab_ref[:, :WG] = (a * _rms(a) * g_ref[:, :WG]).astype(BF16)
        nab_ref[:, WG:] = (b * _rms(b) * g_ref[:, WG:]).astype(BF16)
        mixed = _dot(nab_ref[...], w_ref[...])
        mixed_ref[...] = mixed
        x2_ref[...] = x_ref[...] + gate_ref[...] * mixed

    row = pl.BlockSpec((tm, D), lambda i: (i, 0))
    half = pl.BlockSpec((tm, WG), lambda i: (i, 0))
    vec = pl.BlockSpec((1, D), lambda i: (0, 0))
    return _call(
        body, name="mix_out", grid=(s // tm,),
        in_specs=[half, half, vec, pl.BlockSpec((D, D), lambda i: (0, 0)), row, vec],
        out_specs=[row, row, row],
        out_shape=[_sds((s, D), BF16), _sds((s, D), F32), _sds((s, D), F32)],
        compiler_params=_params(1),
    )(oa, ob, g, w_out, x, gate)


HALO = 16


def _conv_taps(u, halo, first, tm):
    rows = lax.broadcasted_iota(jnp.int32, (tm, 1), 0)
    keep = jnp.where(first, 0.0, 1.0)
    h1 = halo[HALO - 1:HALO, :] * keep
    h2 = halo[HALO - 2:HALO - 1, :] * keep
    um1 = jnp.where(rows == 0, h1, pltpu.roll(u, 1, 0))
    um2 = jnp.where(rows == 0, h2, jnp.where(rows == 1, h1, pltpu.roll(u, 2, 0)))
    return um1, um2


def _conv_specs(tm):
    u_spec = pl.BlockSpec((2, 1, tm, GU), lambda p, i: (0, p, i, 0))
    halo_spec = pl.BlockSpec((2, 1, HALO, GU), lambda p, i: (0, p, jnp.maximum(i * (tm // HALO) - 1, 0), 0))
    cw_spec = pl.BlockSpec((2, 1, 3, GU), lambda p, i: (0, p, 0, 0))
    cb_spec = pl.BlockSpec((2, 1, 1, GU), lambda p, i: (0, p, 0, 0))
    return u_spec, halo_spec, cw_spec, cb_spec


def _conv_act(u, conv_w, conv_b, tm):
    s = u.shape[2]

    def body(u_ref, halo_ref, cw_ref, cb_ref, a_ref):
        first = pl.program_id(1) == 0
        ys = []
        for side in range(2):
            uv = u_ref[side, 0].astype(F32)
            um1, um2 = _conv_taps(uv, halo_ref[side, 0].astype(F32), first, tm)
            cw = cw_ref[side, 0]
            ys.append(cw[2:3] * uv + cw[1:2] * um1 + cw[0:1] * um2 + cb_ref[side, 0])
        a_ref[0] = (ys[0] * _sigmoid(ys[0]) * ys[1]).astype(BF16)

    u_spec, halo_spec, cw_spec, cb_spec = _conv_specs(tm)
    return _call(
        body, name="conv_act", grid=(4, s // tm),
        in_specs=[u_spec, halo_spec, cw_spec, cb_spec],
        out_specs=pl.BlockSpec((1, tm, GU), lambda p, i: (p, i, 0)),
        out_shape=_sds((4, s, GU), BF16),
        compiler_params=_params(2),
    )(u, u, conv_w, conv_b)


def _down(a, w_down, x2, gate, tm):
    s = x2.shape[0]

    def body(a_ref, w_ref, x_ref, gate_ref, ffn_ref, x3_ref):
        p = pl.program_id(1)
        part = _dot(a_ref[0], w_ref[0])

        @pl.when(p == 0)
        def _():
            ffn_ref[...] = part

        @pl.when(p > 0)
        def _():
            ffn_ref[...] += part

        @pl.when(p == 3)
        def _():
            x3_ref[...] = x_ref[...] + gate_ref[...] * ffn_ref[...]

    row = pl.BlockSpec((tm, D), lambda i, p: (i, 0))
    return _call(
        body, name="down", grid=(s // tm, 4),
        in_specs=[pl.BlockSpec((1, tm, GU), lambda i, p: (p, i, 0)), pl.BlockSpec((1, GU, D), lambda i, p: (p, 0, 0)),
                  row, pl.BlockSpec((1, D), lambda i, p: (0, 0))],
        out_specs=[row, row],
        out_shape=[_sds((s, D), F32), _sds((s, D), F32)],
        compiler_params=_params(2),
    )(a, w_down, x2, gate)


def _final_loss(x, g, target, tm):
    s = x.shape[0]

    def body(x_ref, g_ref, t_ref, loss_ref, dx_ref, dg_ref):
        @pl.when(pl.program_id(0) == 0)
        def _():
            loss_ref[...] = jnp.zeros_like(loss_ref)
            dg_ref[...] = jnp.zeros_like(dg_ref)
        xv = x_ref[...]
        r = _rms(xv)
        nrm = xv * r
        err = nrm * g_ref[...] - t_ref[...]
        loss_ref[...] += (0.5 / D) * jnp.sum(jnp.sum(err * err, axis=-1, keepdims=True), axis=0, keepdims=True)
        dy = err * (1.0 / D)
        dg_ref[...] += _colsum(dy * nrm)
        dx_ref[...] = _rms_bwd(dy * g_ref[...], nrm, r)

    row = pl.BlockSpec((tm, D), lambda i: (i, 0))
    vec = pl.BlockSpec((1, D), lambda i: (0, 0))
    return _call(
        body, name="final_loss", grid=(s // tm,),
        in_specs=[row, vec, row],
        out_specs=[pl.BlockSpec((1, 1), lambda i: (0, 0)), row, vec],
        out_shape=[_sds((1, 1), F32), _sds((s, D), F32), _sds((1, D), F32)],
        compiler_params=_params(1),
    )(x, g, target)


def _down_bwd(dx3, gate, ffn, w_down, tm):
    s = dx3.shape[0]

    def body(dx_ref, gate_ref, ffn_ref, w_ref, dgate_ref, dff_ref, da_ref):
        i, p = pl.program_id(0), pl.program_id(1)

        @pl.when((i == 0) & (p == 0))
        def _():
            dgate_ref[...] = jnp.zeros_like(dgate_ref)

        @pl.when(p == 0)
        def _():
            dxv = dx_ref[...]
            dgate_ref[...] += _colsum(dxv * ffn_ref[...])
            dff_ref[...] = (dxv * gate_ref[...]).astype(BF16)

        da_ref[0] = _dot_nt(dff_ref[...], w_ref[0]).astype(BF16)

    row = pl.BlockSpec((tm, D), lambda i, p: (i, 0))
    vec = pl.BlockSpec((1, D), lambda i, p: (0, 0))
    return _call(
        body, name="down_bwd", grid=(s // tm, 4),
        in_specs=[row, vec, row, pl.BlockSpec((1, GU, D), lambda i, p: (p, 0, 0))],
        out_specs=[vec, row, pl.BlockSpec((1, tm, GU), lambda i, p: (p, i, 0))],
        out_shape=[_sds((1, D), F32), _sds((s, D), BF16), _sds((4, s, GU), BF16)],
        compiler_params=_params(2),
    )(dx3, gate, ffn, w_down)


def _conv_act_bwd(u, conv_w, conv_b, da, tm, after):
    s = u.shape[2]

    def body(u_ref, halo_ref, cw_ref, cb_ref, da_ref, dy_ref, dcw_ref, dcb_ref):
        first = pl.program_id(1) == 0

        @pl.when(first)
        def _():
            dcw_ref[...] = jnp.zeros_like(dcw_ref)
            dcb_ref[...] = jnp.zeros_like(dcb_ref)

        taps, ys = [], []
        for side in range(2):
            uv = u_ref[side, 0].astype(F32)
            um1, um2 = _conv_taps(uv, halo_ref[side, 0].astype(F32), first, tm)
            cw = cw_ref[side, 0]
            taps.append((um2, um1, uv))
            ys.append(cw[2:3] * uv + cw[1:2] * um1 + cw[0:1] * um2 + cb_ref[side, 0])
        dav = da_ref[0].astype(F32)
        sg = _sigmoid(ys[0])
        dys = (dav * ys[1] * (sg * (1.0 + ys[0] * (1.0 - sg))), dav * (ys[0] * sg))
        for side in range(2):
            dy_ref[side, 0] = dys[side].astype(BF16)
            dcb_ref[side, 0] += _colsum(dys[side])
            for j in range(3):
                dcw_ref[side, 0, j:j + 1, :] += _colsum(dys[side] * taps[side][j])

    u_spec, halo_spec, cw_spec, cb_spec = _conv_specs(tm)
    return _call_after(
        body, after, name="conv_act_bwd", grid=(4, s // tm),
        in_specs=[u_spec, halo_spec, cw_spec, cb_spec, pl.BlockSpec((1, tm, GU), lambda p, i: (p, i, 0))],
        out_specs=[u_spec, cw_spec, cb_spec],
        out_shape=[_sds((2, 4, s, GU), BF16), _sds((2, 4, 3, GU), F32), _sds((2, 4, 1, GU), F32)],
        compiler_params=_params(2),
    )(u, u, conv_w, conv_b, da)


def _conv_transpose(dy, conv_w, tm):
    s = dy.shape[1]
    nt = s // tm

    def body(dy_ref, halo_ref, cw_ref, du_ref):
        keep = jnp.where(pl.program_id(1) == nt - 1, 0.0, 1.0)
        dv = dy_ref[0].astype(F32)
        rows = lax.broadcasted_iota(jnp.int32, (tm, 1), 0)
        h0 = halo_ref[0, 0:1, :].astype(F32) * keep
        h1 = halo_ref[0, 1:2, :].astype(F32) * keep
        dp1 = jnp.where(rows == tm - 1, h0, pltpu.roll(dv, tm - 1, 0))
        dp2 = jnp.where(rows == tm - 1, h1, jnp.where(rows == tm - 2, h0, pltpu.roll(dv, tm - 2, 0)))
        cw = cw_ref[0]
        du_ref[0] = (cw[2:3] * dv + cw[1:2] * dp1 + cw[0:1] * dp2).astype(BF16)

    blk = pl.BlockSpec((1, tm, GU), lambda g, i: (g, i, 0))
    return _call(
        body, name="conv_transpose", grid=(NDEV, nt),
        in_specs=[blk, pl.BlockSpec((1, HALO, GU),
                                    lambda g, i: (g, jnp.minimum((i + 1) * (tm // HALO), s // HALO - 1), 0)),
                  pl.BlockSpec((1, 3, GU), lambda g, i: (g, 0, 0))],
        out_specs=blk,
        out_shape=_sds((NDEV, s, GU), BF16),
        compiler_params=_params(2),
    )(dy, dy, conv_w)


def _wgrad(a3, b3, *, tk, name):
    ga, s, ka = a3.shape
    gb, _, nb = b3.shape
    groups = max(ga, gb)
    nk = s // tk

    def body(a_ref, b_ref, o_ref, acc):
        k = pl.program_id(1)

        @pl.when(k == 0)
        def _():
            acc[...] = jnp.zeros_like(acc)

        acc[...] += _dot_tn(a_ref[0], b_ref[0])

        @pl.when(k == nk - 1)
        def _():
            o_ref[0] = acc[...].astype(BF16)

    a_spec = pl.BlockSpec((1, tk, ka), (lambda g, k: (g, k, 0)) if ga > 1 else (lambda g, k: (0, k, 0)))
    b_spec = pl.BlockSpec((1, tk, nb), (lambda g, k: (g, k, 0)) if gb > 1 else (lambda g, k: (0, k, 0)))
    return _call(
        body, name=name, grid=(groups, nk),
        in_specs=[a_spec, b_spec], out_specs=pl.BlockSpec((1, ka, nb), lambda g, k: (g, 0, 0)),
        out_shape=_sds((groups, ka, nb), BF16),
        scratch_shapes=[pltpu.VMEM((ka, nb), F32)],
        compiler_params=_params(2),
    )(a3, b3)


def _wgrad_in(h1, dparts, tk, after):
    s = h1.shape[0]
    nk = s // tk

    def body(a_ref, *refs):
        d_refs, o_ref, acc = refs[:NT], refs[NT], refs[NT + 1]
        k = pl.program_id(0)

        @pl.when(k == 0)
        def _():
            acc[...] = jnp.zeros_like(acc)

        for j in range(NT):
            acc[:, WG * j:WG * (j + 1)] += _dot_tn(a_ref[...], d_refs[j][...])

        @pl.when(k == nk - 1)
        def _():
            o_ref[...] = acc[...].astype(BF16)

    kw = dict(
        name="wgrad_in", grid=(nk,),
        in_specs=[pl.BlockSpec((tk, D), lambda k: (k, 0))] + [pl.BlockSpec((tk, WG), lambda k: (k, 0))] * NT,
        out_specs=pl.BlockSpec((D, NT * WG), lambda k: (0, 0)),
        out_shape=_sds((D, NT * WG), BF16),
        scratch_shapes=[pltpu.VMEM((D, NT * WG), F32)],
        compiler_params=_params(1),
    )
    call = _call(body, **kw) if after is None else _call_after(body, after, **kw)
    return call(h1, *dparts)


def _dgrad_in(dparts, w, x_in, scale, dx_up, tm, after):
    s = x_in.shape[0]

    def body(*refs):
        d_refs = refs[:NT]
        w_ref, x_ref, sc_ref, up_ref, dx_ref, dsc_ref, dsh_ref = refs[NT:]

        @pl.when(pl.program_id(0) == 0)
        def _():
            dsc_ref[...] = jnp.zeros_like(dsc_ref)
            dsh_ref[...] = jnp.zeros_like(dsh_ref)

        dh = _dot_nt(d_refs[0][...], w_ref[:, 0:WG])
        for j in range(1, NT):
            dh = dh + _dot_nt(d_refs[j][...], w_ref[:, WG * j:WG * (j + 1)])
        xv = x_ref[...]
        r = _rms(xv)
        nrm = xv * r
        dsh_ref[...] += _colsum(dh)
        dsc_ref[...] += _colsum(dh * nrm)
        dx_ref[...] = up_ref[...] + _rms_bwd(dh * (1.0 + sc_ref[...]), nrm, r)

    row = pl.BlockSpec((tm, D), lambda i: (i, 0))
    vec = pl.BlockSpec((1, D), lambda i: (0, 0))
    return _call_after(
        body, after, name="dgrad_in", grid=(s // tm,),
        in_specs=[pl.BlockSpec((tm, WG), lambda i: (i, 0))] * NT + [pl.BlockSpec((D, NT * WG), lambda i: (0, 0)), row, vec, row],
        out_specs=[row, vec, vec],
        out_shape=[_sds((s, D), F32), _sds((1, D), F32), _sds((1, D), F32)],
        compiler_params=_params(1),
    )(*dparts, w, x_in, scale, dx_up)


def _dgrad_norm_bwd(d3, w, x_in, scale, dx_up, *, tm, name, after):
    groups, s, n = d3.shape

    def body(d_ref, w_ref, x_ref, sc_ref, up_ref, dx_ref, dsc_ref, dsh_ref, acc):
        i, g = pl.program_id(0), pl.program_id(1)

        @pl.when((i == 0) & (g == 0))
        def _():
            dsc_ref[...] = jnp.zeros_like(dsc_ref)
            dsh_ref[...] = jnp.zeros_like(dsh_ref)

        part = _dot_nt(d_ref[0], w_ref[0])

        @pl.when(g == 0)
        def _():
            acc[...] = part

        @pl.when(g > 0)
        def _():
            acc[...] += part

        @pl.when(g == groups - 1)
        def _():
            dh = acc[...]
            xv = x_ref[...]
            r = _rms(xv)
            nrm = xv * r
            dsh_ref[...] += _colsum(dh)
            dsc_ref[...] += _colsum(dh * nrm)
            dx_ref[...] = up_ref[...] + _rms_bwd(dh * (1.0 + sc_ref[...]), nrm, r)

    row = pl.BlockSpec((tm, D), lambda i, g: (i, 0))
    vec = pl.BlockSpec((1, D), lambda i, g: (0, 0))
    return _call_after(
        body, after, name=name, grid=(s // tm, groups),
        in_specs=[pl.BlockSpec((1, tm, n), lambda i, g: (g, i, 0)), pl.BlockSpec((1, D, n), lambda i, g: (g, 0, 0)),
                  row, vec, row],
        out_specs=[row, vec, vec],
        out_shape=[_sds((s, D), F32), _sds((1, D), F32), _sds((1, D), F32)],
        scratch_shapes=[pltpu.VMEM((tm, D), F32)],
        compiler_params=_params(2),
    )(d3, w, x_in, scale, dx_up)


def _mix_out_bwd(dx2, mixed, gate, w_out, oa, ob, g, tm):
    s = dx2.shape[0]

    def body(dx_ref, mixed_ref, gate_ref, w_ref, oa_ref, ob_ref, g_ref, dgate_ref, dm_ref, doa_ref, dob_ref, dg_ref):
        @pl.when(pl.program_id(0) == 0)
        def _():
            dgate_ref[...] = jnp.zeros_like(dgate_ref)
            dg_ref[...] = jnp.zeros_like(dg_ref)
        dxv = dx_ref[...]
        dgate_ref[...] += _colsum(dxv * mixed_ref[...])
        dm_ref[...] = (dxv * gate_ref[...]).astype(BF16)
        dnab = _dot_nt(dm_ref[...], w_ref[...])
        for o_ref, do_ref, sl in ((oa_ref, doa_ref, slice(0, WG)), (ob_ref, dob_ref, slice(WG, D))):
            ov = o_ref[...]
            r = _rms(ov)
            nrm = ov * r
            dn = dnab[:, sl]
            dg_ref[:, sl] += _colsum(dn * nrm)
            do_ref[...] = _rms_bwd(dn * g_ref[:, sl], nrm, r)

    row = pl.BlockSpec((tm, D), lambda i: (i, 0))
    half = pl.BlockSpec((tm, WG), lambda i: (i, 0))
    vec = pl.BlockSpec((1, D), lambda i: (0, 0))
    return _call(
        body, name="mix_out_bwd", grid=(s // tm,),
        in_specs=[row, row, vec, pl.BlockSpec((D, D), lambda i: (0, 0)), half, half, vec],
        out_specs=[vec, row, half, half, vec],
        out_shape=[_sds((1, D), F32), _sds((s, D), BF16), _sds((s, WG), F32), _sds((s, WG), F32), _sds((1, D), F32)],
        compiler_params=_params(1),
    )(dx2, mixed, gate, w_out, oa, ob, g)


def _attn_a_bwd(proj, bias_tab, d_oa, after):
    s = proj.shape[1]
    nq = s // TQA

    def body(q_ref, k_ref, v_ref, b_ref, do_ref, dq_ref, dk_ref, dv_ref, db_ref, dk_acc, dv_acc):
        qi = pl.program_id(1)

        @pl.when(qi == 0)
        def _():
            dk_acc[...] = jnp.zeros_like(dk_acc)
            dv_acc[...] = jnp.zeros_like(dv_acc)
            db_ref[...] = jnp.zeros_like(db_ref)

        def tile(nk, off, kstart):
            for h in range(2):
                sl = slice(HD * h, HD * (h + 1))
                q = q_ref[0, :, sl]
                k = k_ref[0, pl.ds(kstart, nk), sl]
                v = v_ref[0, pl.ds(kstart, nk), sl]
                do = do_ref[:, sl].astype(BF16)
                sc = _dot_nt(q, k) * 0.125 + b_ref[h, :, off:off + nk]
                p = jnp.exp(sc - jnp.max(sc, axis=-1, keepdims=True))
                p = p / jnp.sum(p, axis=-1, keepdims=True)
                dp = _dot_nt(do, v)
                ds = p * (dp - jnp.sum(dp * p, axis=-1, keepdims=True))
                db_ref[h, :, off:off + nk] += ds
                dsb = (ds * 0.125).astype(BF16)
                dq_ref[:, sl] = _dot(dsb, k).astype(BF16)
                dk_acc[pl.ds(kstart, nk), sl] += _dot_tn(dsb, q)
                dv_acc[pl.ds(kstart, nk), sl] += _dot_tn(p.astype(BF16), do)

        _attn_a_cases(qi, tile)

        @pl.when(qi == nq - 1)
        def _():
            dk_ref[...] = dk_acc[...].astype(BF16)
            dv_ref[...] = dv_acc[...].astype(BF16)

    q_spec, k_spec, v_spec, b_spec = _attn_a_specs(s)
    blk = pl.BlockSpec((TQA, 128), lambda hp, qi: (qi, hp))
    col = pl.BlockSpec((s, 128), lambda hp, qi: (0, hp))
    return _call_after(
        body, after, name="attn_a_bwd", grid=(4, nq),
        in_specs=[q_spec, k_spec, v_spec, b_spec, blk],
        out_specs=[blk, col, col, b_spec],
        out_shape=[_sds((s, WG), BF16), _sds((s, WG), BF16), _sds((s, WG), BF16), _sds((NDEV, TQA, KWA), F32)],
        scratch_shapes=[pltpu.VMEM((s, 128), F32), pltpu.VMEM((s, 128), F32)],
        compiler_params=_params(2),
    )(proj, proj, proj, bias_tab, d_oa)


def _attn_b_bwd(proj, lsum, d_ob):
    s = proj.shape[1]
    nq = s // TQB
    nkb = s // TKB
    ndiag = TQB // TKB

    def body(q_ref, k_ref, v_ref, ls_ref, do_ref, dq_ref, dk_ref, dv_ref, dkt_acc, dvt_acc):
        qi = pl.program_id(1)
        q0 = qi * TQB

        @pl.when(qi == 0)
        def _():
            dkt_acc[...] = jnp.zeros_like(dkt_acc)
            dvt_acc[...] = jnp.zeros_like(dvt_acc)

        rows = lax.broadcasted_iota(jnp.int32, (TQB, TKB), 0)
        cols = lax.broadcasted_iota(jnp.int32, (TQB, TKB), 1)
        uj = lax.broadcasted_iota(jnp.int32, (TKB, TKB), 0)
        us = lax.broadcasted_iota(jnp.int32, (TKB, TKB), 1)
        prefix = jnp.where(uj <= us, 1.0, 0.0).astype(BF16)
        heads = [slice(HD * h, HD * (h + 1)) for h in range(2)]
        qs = [(q_ref[0, :, sl].astype(F32) * 0.125).astype(BF16) for sl in heads]
        dos = [do_ref[:, sl].astype(BF16) for sl in heads]
        ei = lax.broadcasted_iota(jnp.int32, (TQB, TQB), 0)
        ej = lax.broadcasted_iota(jnp.int32, (TQB, TQB), 1)
        eye = jnp.where(ei == ej, 1.0, 0.0).astype(BF16)
        qts = [_dot_tn(x, eye).astype(BF16) for x in qs]
        dots = [_dot_tn(x, eye).astype(BF16) for x in dos]
        stots = [ls_ref[0, :, HD * h:HD * h + 1] for h in range(2)]

        def tile(h, kb, diag, cl, cg, dq):
            sl = heads[h]
            kstart = pl.multiple_of(kb * TKB, TKB)
            k = k_ref[0, pl.ds(kstart, TKB), sl]
            v = v_ref[0, pl.ds(kstart, TKB), sl]
            lb, sp = _sb_logits(qs[h], k)
            if diag is not None:
                strict = rows > cols + diag * TKB
                sp = jnp.where(strict, sp, 0.0)
            pre = _dot2(sp, prefix) + cl
            a = jnp.exp(lb - stots[h] + pre)
            if diag is not None:
                a = jnp.where(strict, a, 0.0)
            gz = _dot_nt(dos[h], v) * a
            pg = _dot2(gz, prefix) + cg
            dl = gz - pg * jnp.exp(lb)
            if diag is not None:
                dl = jnp.where(strict, dl, 0.0)
            dlb = dl.astype(BF16)
            dkt_acc[h * nkb + kb] += _dot(qts[h], dlb)
            dvt_acc[h * nkb + kb] += _dot(dots[h], a.astype(BF16))
            return pre[:, TKB - 1:TKB], pg[:, TKB - 1:TKB], dq + _dot(dlb, k)

        state = [jnp.zeros((TQB, 1), F32), jnp.zeros((TQB, 1), F32), jnp.zeros((TQB, HD), F32)] * 2

        def step(i, st):
            st = list(st)
            for sub in range(KSTEP):
                for h in range(2):
                    st[3 * h:3 * h + 3] = tile(h, i * KSTEP + sub, None, *st[3 * h:3 * h + 3])
            return tuple(st)

        state = list(lax.fori_loop(0, q0 // (KSTEP * TKB), step, tuple(state)))
        for d in range(ndiag):
            for h in range(2):
                state[3 * h:3 * h + 3] = tile(h, q0 // TKB + d, d, *state[3 * h:3 * h + 3])
        for h in range(2):
            dq_ref[:, heads[h]] = (state[3 * h + 2] * 0.125).astype(BF16)

        @pl.when(qi == nq - 1)
        def _():
            eye64 = eye[:HD, :HD]
            for h in range(2):
                for kb in range(nkb):
                    rows_kb = slice(kb * TKB, (kb + 1) * TKB)
                    dk_ref[rows_kb, heads[h]] = _dot2_tn(dkt_acc[h * nkb + kb], eye64).astype(BF16)
                    dv_ref[rows_kb, heads[h]] = _dot2_tn(dvt_acc[h * nkb + kb], eye64).astype(BF16)

    q_spec, k_spec, v_spec = _attn_b_specs(s)
    blk = pl.BlockSpec((TQB, 128), lambda hp, qi: (qi, hp))
    col = pl.BlockSpec((s, 128), lambda hp, qi: (0, hp))
    return _call(
        body, name="attn_b_bwd", grid=(4, nq),
        in_specs=[q_spec, k_spec, v_spec, pl.BlockSpec((1, TQB, 128), lambda hp, qi: (hp, qi, 0)), blk],
        out_specs=[blk, col, col],
        out_shape=[_sds((s, WG), BF16)] * 3,
        scratch_shapes=[pltpu.VMEM((2 * nkb, HD, TKB), F32), pltpu.VMEM((2 * nkb, HD, TKB), F32)],
        compiler_params=_params(2),
    )(proj, proj, proj, lsum, d_ob)


def _bias_fold(dtab):
    def body(t_ref, d_ref, far_ref):
        acc = jnp.zeros((NDEV, TABW), F32)
        zpad = jnp.zeros((NDEV, TAB0), F32)
        for r in range(TQA):
            row = jnp.concatenate([zpad, t_ref[:, r, :]], axis=1)
            acc = acc + (pltpu.roll(row, TABW - r, 1) if r else row)
        d_ref[...] = acc
        lane = lax.broadcasted_iota(jnp.int32, (NDEV, TABW), 1)
        far = jnp.sum(jnp.where(lane < N_FAR, acc, 0.0), axis=1, keepdims=True)
        far_ref[...] = jnp.broadcast_to(far, (NDEV, 128))

    d_fpad, d_far = _call(
        body, name="bias_fold", grid=(1,),
        in_specs=[pl.BlockSpec((NDEV, TQA, KWA), lambda i: (0, 0, 0))],
        out_specs=[pl.BlockSpec((NDEV, TABW), lambda i: (0, 0)), pl.BlockSpec((NDEV, 128), lambda i: (0, 0))],
        out_shape=[_sds((NDEV, TABW), F32), _sds((NDEV, 128), F32)],
        compiler_params=_params(1),
    )(dtab)
    d_near = d_fpad[:, N_FAR:N_FAR + N_NEAR][:, ::-1]
    return jnp.concatenate([jnp.zeros((NDEV, REL_CLIP - CHUNK + 1), F32), d_near, d_far[:, :1]], axis=1)


def _adamw(w, g, m, v):
    m = B1 * m + (1.0 - B1) * g
    v = B2 * v + (1.0 - B2) * (g * g)
    m_hat = m / (1.0 - B1 ** STEP)
    v_hat = v / (1.0 - B2 ** STEP)
    delta = -LR * (m_hat / (jnp.sqrt(v_hat) + AEPS) + WD * w)
    return delta, m, v


def _adamw_big(recv0, recv1, w, m, v):
    _, rows, cols = recv0.shape
    tr = rows if rows <= 512 else 256
    nt = rows // tr

    def body(r0_ref, r1_ref, w_ref, m_ref, v_ref, g_ref, d_ref, nm_ref, nv_ref):
        def update(r_ref):
            g = r_ref[0].astype(F32)
            for p in range(1, NDEV):
                g = g + r_ref[p].astype(F32)
            delta, nm, nv = _adamw(w_ref[0], g, m_ref[0], v_ref[0])
            g_ref[0], d_ref[0], nm_ref[0], nv_ref[0] = g, delta, nm, nv

        @pl.when(pl.program_id(0) == 0)
        def _():
            update(r0_ref)

        @pl.when(pl.program_id(0) == 1)
        def _():
            update(r1_ref)

    blk = pl.BlockSpec((1, tr, cols), lambda l, i: (l, i, 0))
    r0_spec = pl.BlockSpec((NDEV, tr, cols), lambda l, i: (0, jnp.where(l == 0, i, nt - 1), 0))
    r1_spec = pl.BlockSpec((NDEV, tr, cols), lambda l, i: (0, jnp.where(l == 1, i, 0), 0))
    return _call(
        body, name="adamw_big", grid=(2, nt),
        in_specs=[r0_spec, r1_spec, blk, blk, blk],
        out_specs=[blk] * 4,
        out_shape=[_sds((2, rows, cols), F32)] * 4,
        compiler_params=_params(2),
    )(recv0, recv1, w, m, v)


def _adamw_w_ada(cact_t, dmod, w, m, v):
    tr = 256

    def body(c_ref, dm_ref, w_ref, m_ref, v_ref, g_ref, d_ref, nm_ref, nv_ref):
        g = c_ref[:, 0:1] * dm_ref[0, 0:1, :]
        for b in range(1, NDEV):
            g = g + c_ref[:, b:b + 1] * dm_ref[0, b:b + 1, :]
        delta, nm, nv = _adamw(w_ref[0], g, m_ref[0], v_ref[0])
        g_ref[0], d_ref[0], nm_ref[0], nv_ref[0] = g, delta, nm, nv

    blk = pl.BlockSpec((1, tr, 768), lambda l, i: (l, i, 0))
    return _call(
        body, name="adamw_w_ada", grid=(2, D // tr),
        in_specs=[pl.BlockSpec((tr, NDEV), lambda l, i: (i, 0)), pl.BlockSpec((1, NDEV, 768), lambda l, i: (l, 0, 0)),
                  blk, blk, blk],
        out_specs=[blk] * 4,
        out_shape=[_sds((2, D, 768), F32)] * 4,
        compiler_params=_params(2),
    )(cact_t, dmod, w, m, v)


def _adamw_small(gath, w, m, v):
    rows = gath.shape[1]

    def body(r_ref, w_ref, m_ref, v_ref, g_ref, d_ref, nm_ref, nv_ref):
        g = r_ref[0]
        for p in range(1, NDEV):
            g = g + r_ref[p]
        delta, nm, nv = _adamw(w_ref[...], g, m_ref[...], v_ref[...])
        g_ref[...], d_ref[...], nm_ref[...], nv_ref[...] = g, delta, nm, nv

    blk = pl.BlockSpec((rows, D), lambda i: (0, 0))
    return _call(
        body, name="adamw_small", grid=(1,),
        in_specs=[pl.BlockSpec((NDEV, rows, D), lambda i: (0, 0, 0)), blk, blk, blk],
        out_specs=[blk] * 4,
        out_shape=[_sds((rows, D), F32)] * 4,
        compiler_params=_params(1),
    )(gath, w, m, v)


_PACK = (("b_ada", 2 * 6 * D), ("rel_bias", 2 * 8 * 257), ("g_a", 2 * WG), ("g_b", 2 * WG),
         ("conv_b", 2 * 2 * DFF), ("final_g", D), ("conv_w", 2 * NDEV * 3 * GU))


def _pack(parts):
    rows = []
    for name, size in _PACK:
        flat = parts[name].reshape(-1).astype(F32)
        assert flat.shape[0] == size, (name, flat.shape)
        rows.append(jnp.pad(flat, (0, -size % D)))
    out = jnp.concatenate(rows).reshape(-1, D)
    return jnp.pad(out, ((0, -out.shape[0] % 8), (0, 0)))


def _unpack(packed):
    flat = packed.reshape(-1)
    out, pos = {}, 0
    for name, size in _PACK:
        out[name] = flat[pos:pos + size]
        pos += size + (-size % D)
    return out


def kernel(x, c, w_ada, b_ada, w_in, rel_bias, g_a, g_b, w_out, w_up, conv_w, conv_b, w_down, final_g, loss_target, m_w_ada, m_b_ada, m_w_in, m_rel_bias, m_g_a, m_g_b, m_w_out, m_w_up, m_conv_w, m_conv_b, m_w_down, m_final_g, v_w_ada, v_b_ada, v_w_in, v_rel_bias, v_g_a, v_g_b, v_w_out, v_w_up, v_conv_w, v_conv_b, v_w_down, v_final_g):
    s = x.shape[1]
    assert s % TQA == 0 and s >= KWA and s % 512 == 0
    tm = 512
    tmm = min(1024, s)
    me = 4 * lax.axis_index("x") + 2 * lax.axis_index("y") + lax.axis_index("c")
    xs = x.reshape(s, D)
    target = loss_target.reshape(s, D)

    first = jnp.concatenate([c, jnp.pad(conv_w.reshape(2 * 3, GU), ((0, 1), (0, D - GU)))])
    first_all = _small_allgather(first, "gather_c_conv_w")
    c_all = first_all[:, 0, :]
    cw_all = first_all[:, 1:7, :GU].reshape(NDEV, 2, 3, GU)

    b_sl = lax.dynamic_slice(b_ada, (0, me * 768), (2, 768)).reshape(2, 1, 768)
    mod_part, cact = _mod_fwd(c_all, w_ada, b_sl)
    mod_all = _small_allgather(mod_part.reshape(2 * NDEV, 768), "gather_mod")

    shards = {"in": w_in, "out": w_out, "up": w_up, "down": w_down}
    order = [(kind, l) for l in range(2) for kind in _KINDS]
    mod_all, *srcs = lax.optimization_barrier((mod_all, *[shards[kind][l].astype(BF16) for kind, l in order]))
    gather_started = dict(zip(order, _exchange_start([kind for kind, _ in order], True, srcs, "weights_gather_start")))

    def gathered(kind, l, after):
        return _exchange_wait([kind], True, [gather_started[kind, l]], after, f"weights_gather_wait_{kind}{l}")[0]

    mod_all = mod_all.reshape(NDEV, 2, NDEV, 768)
    mod_me = lax.dynamic_index_in_dim(mod_all, me, axis=2, keepdims=False)
    mod = jnp.transpose(mod_me, (1, 0, 2)).reshape(2, 6, 1, D)

    saved = []
    xl = xs
    for l in range(2):
        sh_mix, sc_mix, gt_mix, sh_ffn, sc_ffn, gt_ffn = (mod[l, j] for j in range(6))
        cw = cw_all[:, l].reshape(2, 4, 3, GU)
        cb = conv_b[l].reshape(2, 4, 1, GU)
        gvec = jnp.concatenate([g_a[l], g_b[l]]).reshape(1, D)
        tab = _bias_table(rel_bias[l])

        wi = gathered("in", l, xl if l else mod)
        h1, proj = _nm_matmul(xl, sh_mix, sc_mix, wi, two_d=True, n=WG, groups=NT, out_dtype=BF16, tm=tmm,
                              name="norm_proj")
        oa = _attn_a_fwd(proj, tab)
        ob, lsum = _attn_b_fwd(proj)
        wo = gathered("out", l, ob)
        nab, mixed, x2 = _mix_out(oa, ob, gvec, wo, xl, gt_mix, tm)
        wu = gathered("up", l, x2)
        h2, u = _nm_matmul(x2, sh_ffn, sc_ffn, wu, two_d=False, n=GU, groups=NDEV, out_dtype=BF16, tm=tmm,
                           name="norm_up")
        u = u.reshape(2, 4, s, GU)
        a = _conv_act(u, cw, cb, tm)
        wd4 = gathered("down", l, a).reshape(4, GU, D)
        ffn, x3 = _down(a, wd4, x2, gt_ffn, tmm)
        saved.append(dict(x=xl, h1=h1, proj=proj, oa=oa, ob=ob, lsum=lsum, nab=nab, mixed=mixed, x2=x2, h2=h2, u=u,
                          a=a, ffn=ffn, cw=cw, cb=cb, gvec=gvec, tab=tab, wd4=wd4, wi=wi, wo=wo, wu=wu))
        xl = x3

    loss_part, dx, d_final_g = _final_loss(xl, final_g.reshape(1, D), target, tm)
    loss = lax.psum(loss_part[0, 0], ("x", "y", "c"))

    sent = {}
    small = {"b_ada": [None, None], "rel_bias": [None, None], "g_a": [None, None], "g_b": [None, None],
             "conv_b": [None, None], "conv_w": [None, None]}

    def send(kind, l, grad):
        started, token = _exchange_start([kind], False, [grad], f"grads_start_{kind}{l}", with_token=True)
        sent[kind, l] = started[0]
        return token

    for l in (1, 0):
        sv = saved[l]
        sh_mix, sc_mix, gt_mix, sh_ffn, sc_ffn, gt_ffn = (mod[l, j] for j in range(6))
        d_gt_ffn, dff, da = _down_bwd(dx, gt_ffn, sv["ffn"], sv["wd4"], tmm)
        tok = send("down", l, _wgrad(sv["a"], dff.reshape(1, s, D), tk=tmm, name="wgrad_down").reshape(DFF, D))
        dy, d_cw, d_cb = _conv_act_bwd(sv["u"], sv["cw"], sv["cb"], da, tm, tok)
        du = _conv_transpose(dy.reshape(NDEV, s, GU), sv["cw"].reshape(NDEV, 3, GU), tm)
        tok = send("up", l, _wgrad(sv["h2"].reshape(1, s, D), du, tk=tmm, name="wgrad_up"))
        dx2, d_sc_ffn, d_sh_ffn = _dgrad_norm_bwd(du, sv["wu"], sv["x2"], sc_ffn, dx, tm=tmm, name="dgrad_up", after=tok)
        d_gt_mix, dmixed, d_oa, d_ob, d_g = _mix_out_bwd(dx2, sv["mixed"], gt_mix, sv["wo"], sv["oa"], sv["ob"],
                                                         sv["gvec"], tm)
        tok = send("out", l, _wgrad(sv["nab"].reshape(1, s, D), dmixed.reshape(1, s, D), tk=tmm,
                                    name="wgrad_out").reshape(D, D))
        dqa, dka, dva, d_tab = _attn_a_bwd(sv["proj"], sv["tab"], d_oa, tok)
        dqb, dkb, dvb = _attn_b_bwd(sv["proj"], sv["lsum"], d_ob)
        dparts = (dqa, dka, dva, dqb, dkb, dvb)
        if l:
            tok = send("in", l, _wgrad_in(sv["h1"], dparts, tmm, None))
        dx, d_sc_mix, d_sh_mix = _dgrad_in(dparts, sv["wi"], sv["x"], sc_mix, dx2, tm, tok)
        small["b_ada"][l] = jnp.concatenate([d_sh_mix, d_sc_mix, d_gt_mix, d_sh_ffn, d_sc_ffn, d_gt_ffn], axis=1)
        small["rel_bias"][l] = _bias_fold(d_tab)
        small["g_a"][l], small["g_b"][l] = d_g[:, :WG], d_g[:, WG:]
        small["conv_b"][l] = d_cb
        small["conv_w"][l] = d_cw.reshape(NDEV, 3, GU)
    grad_x = dx.reshape(1, s, D)

    contrib = {k: jnp.stack(vs) for k, vs in small.items()}
    contrib["final_g"] = d_final_g
    gath = _small_allgather(_pack(contrib), "gather_small_grads")
    send("in", 0, _wgrad_in(saved[0]["h1"], dparts, tmm, gath))

    def place_conv_w(t):
        return lax.dynamic_update_slice(jnp.zeros((2, NDEV, 3, GU), F32), t.reshape(2, 1, 3, GU), (0, me, 0, 0))

    def packed_params(b, rb, ga, gb, cb_, fg, cw_):
        return _pack({"b_ada": b, "rel_bias": rb, "g_a": ga, "g_b": gb, "conv_b": cb_, "final_g": fg,
                      "conv_w": place_conv_w(cw_)})

    sm = _adamw_small(gath,
                      packed_params(b_ada, rel_bias, g_a, g_b, conv_b, final_g, conv_w),
                      packed_params(m_b_ada, m_rel_bias, m_g_a, m_g_b, m_conv_b, m_final_g, m_conv_w),
                      packed_params(v_b_ada, v_rel_bias, v_g_a, v_g_b, v_conv_b, v_final_g, v_conv_w))
    sm = [_unpack(t) for t in sm]

    dmod_all = gath[:, :12, :].reshape(NDEV, 2, 6 * D)
    dmod_sl = jnp.transpose(lax.dynamic_slice(dmod_all, (0, 0, me * 768), (NDEV, 2, 768)), (1, 0, 2))
    ada = _adamw_w_ada(cact.T, dmod_sl, w_ada, m_w_ada, v_w_ada)

    big = {}
    for kind, (w, m, v), after in (("down", (w_down, m_w_down, v_w_down), dx), ("up", (w_up, m_w_up, v_w_up), dx),
                                   ("out", (w_out, m_w_out, v_w_out), dx), ("in", (w_in, m_w_in, v_w_in), ada[0])):
        recv0, recv1 = _exchange_wait([kind, kind], False, [sent[kind, 0], sent[kind, 1]], after, f"grads_wait_{kind}")
        big[kind] = _adamw_big(recv0, recv1, w, m, v)

    def small_out(j, name):
        t = sm[j][name]
        if name == "b_ada":
            return t.reshape(2, 6 * D)
        if name == "rel_bias":
            return t.reshape(2, 8, 257)
        if name in ("g_a", "g_b"):
            return t.reshape(2, WG)
        if name == "conv_b":
            return t.reshape(2, 2 * DFF)
        if name == "final_g":
            return t.reshape(D)
        t = t.reshape(2, NDEV, 3, GU)
        return lax.dynamic_index_in_dim(t, me, axis=1, keepdims=False)

    def group(j):
        return (ada[j], small_out(j, "b_ada"), big["in"][j], small_out(j, "rel_bias"), small_out(j, "g_a"),
                small_out(j, "g_b"), big["out"][j], big["up"][j], small_out(j, "conv_w"), small_out(j, "conv_b"),
                big["down"][j], small_out(j, "final_g"))

    return (loss, grad_x, *group(0), *group(1), *group(2), *group(3))
```

```python
import jax
import jax.numpy as jnp
from jax import lax
from jax.experimental import pallas as pl
from jax.experimental.pallas import tpu as pltpu

F32, BF16 = jnp.float32, jnp.bfloat16
MESH_ID = pl.DeviceIdType.MESH
NDEV = 8
D = 1024
HD = 64
WG = 512
NT = 6
GU = 704
DFF = 2816
CHUNK, NPREV, REL_CLIP = 64, 8, 128
BAND = (NPREV + 1) * CHUNK
EPS = 1e-6
NEG = -1e30
TQA = 256
KWA = TQA + NPREV * CHUNK
TABW = 1024
TAB0 = TABW - KWA
N_FAR = TAB0 + NPREV * CHUNK - REL_CLIP + 1
N_NEAR = REL_CLIP + CHUNK - 1
TQB, TKB = 512, 128
KSTEP = 4
LR, B1, B2, AEPS, WD, STEP = 0.001, 0.9, 0.999, 1e-08, 0.01, 10
VMEM_MB = 56


def _call(body, **kw):
    return pl.pallas_call(body, **kw)


def _call_after(body, after, **kw):
    n_in = len(kw["in_specs"])
    kw["in_specs"] = list(kw["in_specs"]) + [pl.BlockSpec(memory_space=pl.ANY)]

    def tied(*refs):
        body(*refs[:n_in], *refs[n_in + 1:])

    call = _call(tied, **kw)
    return lambda *args: call(*args, after)


def _params(n_axes):
    return pltpu.CompilerParams(dimension_semantics=("arbitrary",) * n_axes, vmem_limit_bytes=VMEM_MB << 20)


def _dot(a, b):
    return jnp.dot(a, b, preferred_element_type=F32)


def _dot_nt(a, b):
    return lax.dot_general(a, b, (((1,), (1,)), ((), ())), preferred_element_type=F32)


def _dot_tn(a, b):
    return lax.dot_general(a, b, (((0,), (0,)), ((), ())), preferred_element_type=F32)


def _dot2(x, u):
    hi = x.astype(BF16)
    lo = (x - hi.astype(F32)).astype(BF16)
    return _dot(hi, u) + _dot(lo, u)


def _dot2_deep(x, uu):
    hi = x.astype(BF16)
    lo = (x - hi.astype(F32)).astype(BF16)
    return _dot(jnp.concatenate([hi, lo], axis=1), uu)


def _dot2_tn(x, u):
    hi = x.astype(BF16)
    lo = (x - hi.astype(F32)).astype(BF16)
    return _dot_tn(hi, u) + _dot_tn(lo, u)


def _rms(x):
    return lax.rsqrt(jnp.mean(x * x, axis=-1, keepdims=True) + EPS)


def _rms_bwd(dn, n, r):
    return r * (dn - n * jnp.mean(dn * n, axis=-1, keepdims=True))


def _colsum(x):
    return jnp.sum(x, axis=0, keepdims=True)


def _sigmoid(x):
    return 1.0 / (1.0 + jnp.exp(-x))


def _sds(shape, dtype):
    return jax.ShapeDtypeStruct(shape, dtype)


def _place():
    x, y, c = lax.axis_index("x"), lax.axis_index("y"), lax.axis_index("c")
    return x, y, c, 4 * x + 2 * y + c


def _peer(x, y, c, k):
    px = 1 - x if k & 4 else x
    py = 1 - y if k & 2 else y
    pc = 1 - c if k & 1 else c
    return (px, py, pc), 4 * px + 2 * py + pc


def _small_allgather(v, name):
    rows, cols = v.shape

    def body(v_ref, out_ref, send_sems, recv_sems, local_sem):
        x, y, c, me = _place()
        mine = pltpu.make_async_copy(v_ref, out_ref.at[me], local_sem)
        mine.start()
        sends = []
        for k in range(1, NDEV):
            peer, _ = _peer(x, y, c, k)
            cp = pltpu.make_async_remote_copy(v_ref, out_ref.at[me], send_sems.at[k - 1], recv_sems.at[k - 1],
                                              device_id=peer, device_id_type=MESH_ID)
            cp.start()
            sends.append(cp)
        for k in range(1, NDEV):
            peer, pidx = _peer(x, y, c, k)
            pltpu.make_async_remote_copy(v_ref, out_ref.at[pidx], send_sems.at[k - 1], recv_sems.at[k - 1],
                                         device_id=peer, device_id_type=MESH_ID).wait_recv()
        for cp in sends:
            cp.wait_send()
        mine.wait()

    return _call(
        body, name=name,
        out_shape=_sds((NDEV, rows, cols), F32),
        in_specs=[pl.BlockSpec(memory_space=pltpu.VMEM)],
        out_specs=pl.BlockSpec(memory_space=pltpu.VMEM),
        scratch_shapes=[pltpu.SemaphoreType.DMA((NDEV - 1,)), pltpu.SemaphoreType.DMA((NDEV - 1,)),
                        pltpu.SemaphoreType.DMA],
    )(v)


def _shard_view(ref, kind, p):
    if kind == "in":
        return ref.at[:, pl.ds(pl.multiple_of(p * 384, 128), 384)]
    if kind == "out":
        return ref.at[pl.ds(pl.multiple_of(p * 128, 128), 128), :]
    if kind == "up":
        return ref.at[p]
    if kind == "down":
        return ref.at[pl.ds(pl.multiple_of(p * 352, 16), 352), :]
    raise ValueError(kind)


_KINDS = ("in", "out", "up", "down")
_FULL_SHAPES = {"in": (D, 3 * D), "out": (D, D), "up": (NDEV, D, GU), "down": (DFF, D)}
_SHARD_SHAPES = {"in": (D, 384), "out": (128, D), "up": (D, GU), "down": (352, D)}


_HBM = pl.BlockSpec(memory_space=pltpu.HBM)
_SEM = pl.BlockSpec(memory_space=pltpu.SEMAPHORE)
_EFFECT = pltpu.SideEffectType.DATAFLOW_SIDE_EFFECTING
_SEM_SHAPES = (pltpu.SemaphoreType.DMA((NDEV - 1,)), pltpu.SemaphoreType.DMA((NDEV - 1,)), pltpu.SemaphoreType.DMA(()))


def _hbm(a):
    return pltpu.with_memory_space_constraint(a, pltpu.HBM)


def _exchange_copies(kind, gather, src, land, sems):
    send_sems, recv_sems, local_sem = sems
    x, y, c, me = _place()

    def ends(p_dst, p_from):
        if gather:
            return src, _shard_view(land, kind, me), _shard_view(land, kind, p_from)
        return _shard_view(src, kind, p_dst), land.at[me], land.at[p_from]

    s_me, d_me, _ = ends(me, me)
    local = pltpu.make_async_copy(s_me, d_me, local_sem)
    sends, arrivals = [], []
    for k in range(1, NDEV):
        peer, pidx = _peer(x, y, c, k)
        s_k, d_k, from_k = ends(pidx, pidx)
        sends.append(pltpu.make_async_remote_copy(s_k, d_k, send_sems.at[k - 1], recv_sems.at[k - 1],
                                                  device_id=peer, device_id_type=MESH_ID))
        arrivals.append(pltpu.make_async_remote_copy(s_k, from_k, send_sems.at[k - 1], recv_sems.at[k - 1],
                                                     device_id=peer, device_id_type=MESH_ID))
    return local, sends, arrivals


def _exchange_start(kinds, gather, srcs, name, with_token=False):
    n = len(kinds)
    lands = [lax.empty(_FULL_SHAPES[kd] if gather else (NDEV,) + _SHARD_SHAPES[kd], BF16) for kd in kinds]

    def body(*refs):
        ins, sems = refs[:2 * n], refs[2 * n:5 * n]
        for j, kd in enumerate(kinds):
            local, sends, _ = _exchange_copies(kd, gather, ins[j], ins[n + j], sems[3 * j:3 * j + 3])
            local.start()
            for cp in sends:
                cp.start()
        if with_token:
            token = refs[7 * n]
            token[...] = jnp.zeros_like(token)

    out_shape = list(_SEM_SHAPES) * n
    out_shape += [pltpu.HBM(a.shape, a.dtype) for a in srcs] + [pltpu.HBM(a.shape, a.dtype) for a in lands]
    out_specs = [_SEM] * (3 * n) + [_HBM] * (2 * n)
    if with_token:
        out_shape.append(_sds((8, 128), F32))
        out_specs.append(pl.BlockSpec(memory_space=pltpu.VMEM))
    outs = _call(
        body, name=name, out_shape=out_shape,
        in_specs=[_HBM] * (2 * n), out_specs=out_specs,
        input_output_aliases={i: 3 * n + i for i in range(2 * n)},
        compiler_params=pltpu.CompilerParams(has_side_effects=_EFFECT),
    )(*[_hbm(a) for a in srcs], *[_hbm(a) for a in lands])
    per_tensor = [(tuple(outs[3 * j:3 * j + 3]), outs[3 * n + j], outs[4 * n + j]) for j in range(n)]
    return (per_tensor, outs[5 * n]) if with_token else per_tensor


def _exchange_wait(kinds, gather, started, after, name):
    n = len(kinds)

    def body(*refs):
        ins, sems = refs[:2 * n], refs[2 * n:5 * n]
        for j, kd in enumerate(kinds):
            local, sends, arrivals = _exchange_copies(kd, gather, ins[j], ins[n + j], sems[3 * j:3 * j + 3])
            local.wait()
            for cp in arrivals:
                cp.wait_recv()
            for cp in sends:
                cp.wait_send()

    srcs = [st[1] for st in started]
    lands = [st[2] for st in started]
    sems = [sm for st in started for sm in st[0]]
    outs = _call(
        body, name=name,
        out_shape=[pltpu.HBM(a.shape, a.dtype) for a in srcs] + [pltpu.HBM(a.shape, a.dtype) for a in lands],
        in_specs=[_HBM] * (2 * n) + [_SEM] * (3 * n) + [pl.BlockSpec(memory_space=pl.ANY)],
        out_specs=[_HBM] * (2 * n),
        input_output_aliases={i: i for i in range(2 * n)},
        compiler_params=pltpu.CompilerParams(has_side_effects=_EFFECT),
    )(*srcs, *lands, *sems, after)
    return outs[n:]


def _mod_fwd(c_all, w_ada, b_sl):
    def body(c_ref, w_ref, b_ref, mod_ref, cact_ref):
        cv = c_ref[...]
        ca = cv * _sigmoid(cv)
        cact_ref[...] = ca
        mod_ref[0] = _dot(ca.astype(BF16), w_ref[0].astype(BF16)) + b_ref[0]

    return _call(
        body, name="mod_fwd", grid=(2,),
        in_specs=[pl.BlockSpec((NDEV, D), lambda l: (0, 0)), pl.BlockSpec((1, D, 768), lambda l: (l, 0, 0)),
                  pl.BlockSpec((1, 1, 768), lambda l: (l, 0, 0))],
        out_specs=[pl.BlockSpec((1, NDEV, 768), lambda l: (l, 0, 0)), pl.BlockSpec((NDEV, D), lambda l: (0, 0))],
        out_shape=[_sds((2, NDEV, 768), F32), _sds((NDEV, D), F32)],
        compiler_params=_params(1),
    )(c_all, w_ada, b_sl)


def _nm_matmul(x, shift, scale, w, *, two_d, n, groups, out_dtype, tm, name):
    s = x.shape[0]

    def body(x_ref, sh_ref, sc_ref, w_ref, h_ref, o_ref):
        @pl.when(pl.program_id(1) == 0)
        def _():
            xv = x_ref[...]
            h_ref[...] = ((xv * _rms(xv)) * (1.0 + sc_ref[...]) + sh_ref[...]).astype(BF16)
        wv = w_ref[...] if two_d else w_ref[0]
        o_ref[0] = _dot(h_ref[...], wv).astype(out_dtype)

    vec = pl.BlockSpec((1, D), lambda i, g: (0, 0))
    w_spec = pl.BlockSpec((D, n), lambda i, g: (0, g)) if two_d else pl.BlockSpec((1, D, n), lambda i, g: (g, 0, 0))
    return _call(
        body, name=name, grid=(s // tm, groups),
        in_specs=[pl.BlockSpec((tm, D), lambda i, g: (i, 0)), vec, vec, w_spec],
        out_specs=[pl.BlockSpec((tm, D), lambda i, g: (i, 0)), pl.BlockSpec((1, tm, n), lambda i, g: (g, i, 0))],
        out_shape=[_sds((s, D), BF16), _sds((groups, s, n), out_dtype)],
        compiler_params=_params(2),
    )(x, shift, scale, w)


def _bias_table(rel_bias):
    far = jnp.broadcast_to(rel_bias[:, 2 * REL_CLIP:], (NDEV, N_FAR))
    near = rel_bias[:, 2 * REL_CLIP - 1:REL_CLIP - CHUNK:-1]
    fpad = jnp.concatenate([far, near, jnp.zeros((NDEV, TABW - N_FAR - N_NEAR), F32)], axis=1)

    def body(f_ref, o_ref):
        t = pltpu.roll(jnp.broadcast_to(f_ref[0], (TQA, TABW)), 0, 1, stride=1, stride_axis=0)[:, TAB0:]
        rows = lax.broadcasted_iota(jnp.int32, (TQA, KWA), 0)
        cols = lax.broadcasted_iota(jnp.int32, (TQA, KWA), 1)
        first = jnp.bitwise_and(rows, -CHUNK)
        o_ref[0] = jnp.where((cols >= first) & (cols < first + BAND), t, NEG)

    return _call(
        body, name="bias_table", grid=(NDEV,),
        in_specs=[pl.BlockSpec((1, 1, TABW), lambda h: (h, 0, 0))],
        out_specs=pl.BlockSpec((1, TQA, KWA), lambda h: (h, 0, 0)),
        out_shape=_sds((NDEV, TQA, KWA), F32),
        compiler_params=_params(1),
    )(fpad.reshape(NDEV, 1, TABW))


def _attn_a_cases(qi, tile):
    @pl.when(qi == 0)
    def _():
        tile(TQA, 2 * TQA, 0)

    @pl.when(qi == 1)
    def _():
        tile(2 * TQA, TQA, 0)

    @pl.when(qi >= 2)
    def _():
        tile(KWA, 0, pl.multiple_of((qi - 2) * TQA, TQA))


def _attn_a_specs(s):
    q_spec = pl.BlockSpec((1, TQA, 128), lambda hp, qi: (0, qi, hp))
    k_spec = pl.BlockSpec((1, s, 128), lambda hp, qi: (1, 0, hp))
    v_spec = pl.BlockSpec((1, s, 128), lambda hp, qi: (2, 0, hp))
    b_spec = pl.BlockSpec((2, TQA, KWA), lambda hp, qi: (hp, 0, 0))
    return q_spec, k_spec, v_spec, b_spec


def _attn_a_fwd(proj, bias_tab):
    s = proj.shape[1]

    def body(q_ref, k_ref, v_ref, b_ref, o_ref):
        def tile(nk, off, kstart):
            for h in range(2):
                sl = slice(HD * h, HD * (h + 1))
                q = q_ref[0, :, sl]
                k = k_ref[0, pl.ds(kstart, nk), sl]
                v = v_ref[0, pl.ds(kstart, nk), sl]
                sc = _dot_nt(q, k) * 0.125 + b_ref[h, :, off:off + nk]
                p = jnp.exp(sc - jnp.max(sc, axis=-1, keepdims=True))
                den = jnp.sum(p, axis=-1, keepdims=True)
                o_ref[:, sl] = _dot(p.astype(BF16), v) / den

        _attn_a_cases(pl.program_id(1), tile)

    q_spec, k_spec, v_spec, b_spec = _attn_a_specs(s)
    return _call(
        body, name="attn_a_fwd", grid=(4, s // TQA),
        in_specs=[q_spec, k_spec, v_spec, b_spec],
        out_specs=pl.BlockSpec((TQA, 128), lambda hp, qi: (qi, hp)),
        out_shape=_sds((s, WG), F32),
        compiler_params=_params(2),
    )(proj, proj, proj, bias_tab)


def _sb_terms(lg):
    sp = jnp.maximum(lg, 0.0) + jnp.log(1.0 + jnp.exp(-jnp.abs(lg)))
    return lg - sp, sp


def _attn_b_specs(s):
    q_spec = pl.BlockSpec((1, TQB, 128), lambda hp, qi: (3, qi, hp))
    k_spec = pl.BlockSpec((1, s, 128), lambda hp, qi: (4, 0, hp))
    v_spec = pl.BlockSpec((1, s, 128), lambda hp, qi: (5, 0, hp))
    return q_spec, k_spec, v_spec


def _attn_b_fwd(proj):
    s = proj.shape[1]
    ndiag = TQB // TKB

    def body(q_ref, k_ref, v_ref, o_ref, ls_ref):
        q0 = pl.program_id(1) * TQB
        rows = lax.broadcasted_iota(jnp.int32, (TQB, TKB), 0)
        cols = lax.broadcasted_iota(jnp.int32, (TQB, TKB), 1)
        uj = lax.broadcasted_iota(jnp.int32, (TKB, TKB), 0)
        us = lax.broadcasted_iota(jnp.int32, (TKB, TKB), 1)
        suffix = jnp.where(uj >= us, 1.0, 0.0).astype(BF16)
        heads = [slice(HD * h, HD * (h + 1)) for h in range(2)]
        qs = [(q_ref[0, :, sl].astype(F32) * 0.125).astype(BF16) for sl in heads]

        def tile(h, kstart, diag, carry, acc):
            k = k_ref[0, pl.ds(kstart, TKB), heads[h]]
            v = v_ref[0, pl.ds(kstart, TKB), heads[h]]
            lb, sp = _sb_terms(_dot_nt(qs[h], k))
            if diag is not None:
                strict = rows > cols + diag * TKB
                sp = jnp.where(strict, sp, 0.0)
            csum = _dot2(sp, suffix) + carry
            w = jnp.exp(lb - csum + sp)
            if diag is not None:
                w = jnp.where(strict, w, 0.0)
            return csum[:, 0:1], acc + _dot(w.astype(BF16), v)

        state = [jnp.zeros((TQB, 1), F32), jnp.zeros((TQB, HD), F32)] * 2
        for d in range(ndiag - 1, -1, -1):
            for h in range(2):
                state[2 * h:2 * h + 2] = tile(h, pl.multiple_of(q0 + d * TKB, TKB), d, *state[2 * h:2 * h + 2])
        nsteps = q0 // (KSTEP * TKB)

        def step(i, st):
            st = list(st)
            base = (nsteps - 1 - i) * (KSTEP * TKB)
            for sub in range(KSTEP - 1, -1, -1):
                for h in range(2):
                    st[2 * h:2 * h + 2] = tile(h, pl.multiple_of(base + sub * TKB, TKB), None, *st[2 * h:2 * h + 2])
            return tuple(st)

        state = lax.fori_loop(0, nsteps, step, tuple(state))
        for h in range(2):
            o_ref[:, heads[h]] = state[2 * h + 1]
            ls_ref[0, :, heads[h]] = jnp.broadcast_to(state[2 * h], (TQB, HD))

    q_spec, k_spec, v_spec = _attn_b_specs(s)
    return _call(
        body, name="attn_b_fwd", grid=(4, s // TQB),
        in_specs=[q_spec, k_spec, v_spec],
        out_specs=[pl.BlockSpec((TQB, 128), lambda hp, qi: (qi, hp)),
                   pl.BlockSpec((1, TQB, 128), lambda hp, qi: (hp, qi, 0))],
        out_shape=[_sds((s, WG), F32), _sds((4, s, 128), F32)],
        compiler_params=_params(2),
    )(proj, proj, proj)


def _mix_out(oa, ob, g, w_out, x, gate, tm):
    s = x.shape[0]

    def body(oa_ref, ob_ref, g_ref, w_ref, x_ref, gate_ref, nab_ref, mixed_ref, x2_ref):
        a, b = oa_ref[...], ob_ref[...]
        nab_ref[:, :WG] = (a * _rms(a) * g_ref[:, :WG]).astype(BF16)
        nab_ref[:, WG:] = (b * _rms(b) * g_ref[:, WG:]).astype(BF16)
        mixed = _dot(nab_ref[...], w_ref[...])
        mixed_ref[...] = mixed
        x2_ref[...] = x_ref[...] + gate_ref[...] * mixed

    row = pl.BlockSpec((tm, D), lambda i: (i, 0))
    half = pl.BlockSpec((tm, WG), lambda i: (i, 0))
    vec = pl.BlockSpec((1, D), lambda i: (0, 0))
    return _call(
        body, name="mix_out", grid=(s // tm,),
        in_specs=[half, half, vec, pl.BlockSpec((D, D), lambda i: (0, 0)), row, vec],
        out_specs=[row, row, row],
        out_shape=[_sds((s, D), BF16), _sds((s, D), F32), _sds((s, D), F32)],
        compiler_params=_params(1),
    )(oa, ob, g, w_out, x, gate)


HALO = 16


def _conv_taps(u, halo, first, tm):
    rows = lax.broadcasted_iota(jnp.int32, (tm, 1), 0)
    keep = jnp.where(first, 0.0, 1.0)
    h1 = halo[HALO - 1:HALO, :] * keep
    h2 = halo[HALO - 2:HALO - 1, :] * keep
    um1 = jnp.where(rows == 0, h1, pltpu.roll(u, 1, 0))
    um2 = jnp.where(rows == 0, h2, jnp.where(rows == 1, h1, pltpu.roll(u, 2, 0)))
    return um1, um2


def _conv_specs(tm):
    u_spec = pl.BlockSpec((2, 1, tm, GU), lambda p, i: (0, p, i, 0))
    halo_spec = pl.BlockSpec((2, 1, HALO, GU), lambda p, i: (0, p, jnp.maximum(i * (tm // HALO) - 1, 0), 0))
    cw_spec = pl.BlockSpec((2, 1, 3, GU), lambda p, i: (0, p, 0, 0))
    cb_spec = pl.BlockSpec((2, 1, 1, GU), lambda p, i: (0, p, 0, 0))
    return u_spec, halo_spec, cw_spec, cb_spec


def _conv_act(u, conv_w, conv_b, tm):
    s = u.shape[2]

    def body(u_ref, halo_ref, cw_ref, cb_ref, a_ref):
        first = pl.program_id(1) == 0
        ys = []
        for side in range(2):
            uv = u_ref[side, 0].astype(F32)
            um1, um2 = _conv_taps(uv, halo_ref[side, 0].astype(F32), first, tm)
            cw = cw_ref[side, 0]
            ys.append(cw[2:3] * uv + cw[1:2] * um1 + cw[0:1] * um2 + cb_ref[side, 0])
        a_ref[0] = (ys[0] * _sigmoid(ys[0]) * ys[1]).astype(BF16)

    u_spec, halo_spec, cw_spec, cb_spec = _conv_specs(tm)
    return _call(
        body, name="conv_act", grid=(4, s // tm),
        in_specs=[u_spec, halo_spec, cw_spec, cb_spec],
        out_specs=pl.BlockSpec((1, tm, GU), lambda p, i: (p, i, 0)),
        out_shape=_sds((4, s, GU), BF16),
        compiler_params=_params(2),
    )(u, u, conv_w, conv_b)


def _down(a, w_down, x2, gate, tm):
    s = x2.shape[0]

    def body(a_ref, w_ref, x_ref, gate_ref, ffn_ref, x3_ref):
        p = pl.program_id(1)
        part = _dot(a_ref[0], w_ref[0])

        @pl.when(p == 0)
        def _():
            ffn_ref[...] = part

        @pl.when(p > 0)
        def _():
            ffn_ref[...] += part

        @pl.when(p == 3)
        def _():
            x3_ref[...] = x_ref[...] + gate_ref[...] * ffn_ref[...]

    row = pl.BlockSpec((tm, D), lambda i, p: (i, 0))
    return _call(
        body, name="down", grid=(s // tm, 4),
        in_specs=[pl.BlockSpec((1, tm, GU), lambda i, p: (p, i, 0)), pl.BlockSpec((1, GU, D), lambda i, p: (p, 0, 0)),
                  row, pl.BlockSpec((1, D), lambda i, p: (0, 0))],
        out_specs=[row, row],
        out_shape=[_sds((s, D), F32), _sds((s, D), F32)],
        compiler_params=_params(2),
    )(a, w_down, x2, gate)


def _final_loss(x, g, target, tm):
    s = x.shape[0]

    def body(x_ref, g_ref, t_ref, loss_ref, dx_ref, dg_ref):
        @pl.when(pl.program_id(0) == 0)
        def _():
            loss_ref[...] = jnp.zeros_like(loss_ref)
            dg_ref[...] = jnp.zeros_like(dg_ref)
        xv = x_ref[...]
        r = _rms(xv)
        nrm = xv * r
        err = nrm * g_ref[...] - t_ref[...]
        loss_ref[...] += (0.5 / D) * jnp.sum(jnp.sum(err * err, axis=-1, keepdims=True), axis=0, keepdims=True)
        dy = err * (1.0 / D)
        dg_ref[...] += _colsum(dy * nrm)
        dx_ref[...] = _rms_bwd(dy * g_ref[...], nrm, r)

    row = pl.BlockSpec((tm, D), lambda i: (i, 0))
    vec = pl.BlockSpec((1, D), lambda i: (0, 0))
    return _call(
        body, name="final_loss", grid=(s // tm,),
        in_specs=[row, vec, row],
        out_specs=[pl.BlockSpec((1, 1), lambda i: (0, 0)), row, vec],
        out_shape=[_sds((1, 1), F32), _sds((s, D), F32), _sds((1, D), F32)],
        compiler_params=_params(1),
    )(x, g, target)


def _down_bwd(dx3, gate, ffn, w_down, tm):
    s = dx3.shape[0]

    def body(dx_ref, gate_ref, ffn_ref, w_ref, dgate_ref, dff_ref, da_ref):
        i, p = pl.program_id(0), pl.program_id(1)

        @pl.when((i == 0) & (p == 0))
        def _():
            dgate_ref[...] = jnp.zeros_like(dgate_ref)

        @pl.when(p == 0)
        def _():
            dxv = dx_ref[...]
            dgate_ref[...] += _colsum(dxv * ffn_ref[...])
            dff_ref[...] = (dxv * gate_ref[...]).astype(BF16)

        da_ref[0] = _dot_nt(dff_ref[...], w_ref[0]).astype(BF16)

    row = pl.BlockSpec((tm, D), lambda i, p: (i, 0))
    vec = pl.BlockSpec((1, D), lambda i, p: (0, 0))
    return _call(
        body, name="down_bwd", grid=(s // tm, 4),
        in_specs=[row, vec, row, pl.BlockSpec((1, GU, D), lambda i, p: (p, 0, 0))],
        out_specs=[vec, row, pl.BlockSpec((1, tm, GU), lambda i, p: (p, i, 0))],
        out_shape=[_sds((1, D), F32), _sds((s, D), BF16), _sds((4, s, GU), BF16)],
        compiler_params=_params(2),
    )(dx3, gate, ffn, w_down)


def _conv_act_bwd(u, conv_w, conv_b, da, tm, after):
    s = u.shape[2]

    def body(u_ref, halo_ref, cw_ref, cb_ref, da_ref, dy_ref, dcw_ref, dcb_ref):
        first = pl.program_id(1) == 0

        @pl.when(first)
        def _():
            dcw_ref[...] = jnp.zeros_like(dcw_ref)
            dcb_ref[...] = jnp.zeros_like(dcb_ref)

        taps, ys = [], []
        for side in range(2):
            uv = u_ref[side, 0].astype(F32)
            um1, um2 = _conv_taps(uv, halo_ref[side, 0].astype(F32), first, tm)
            cw = cw_ref[side, 0]
            taps.append((um2, um1, uv))
            ys.append(cw[2:3] * uv + cw[1:2] * um1 + cw[0:1] * um2 + cb_ref[side, 0])
        dav = da_ref[0].astype(F32)
        sg = _sigmoid(ys[0])
        dys = (dav * ys[1] * (sg * (1.0 + ys[0] * (1.0 - sg))), dav * (ys[0] * sg))
        for side in range(2):
            dy_ref[side, 0] = dys[side].astype(BF16)
            dcb_ref[side, 0] += _colsum(dys[side])
            for j in range(3):
                dcw_ref[side, 0, j:j + 1, :] += _colsum(dys[side] * taps[side][j])

    u_spec, halo_spec, cw_spec, cb_spec = _conv_specs(tm)
    return _call_after(
        body, after, name="conv_act_bwd", grid=(4, s // tm),
        in_specs=[u_spec, halo_spec, cw_spec, cb_spec, pl.BlockSpec((1, tm, GU), lambda p, i: (p, i, 0))],
        out_specs=[u_spec, cw_spec, cb_spec],
        out_shape=[_sds((2, 4, s, GU), BF16), _sds((2, 4, 3, GU), F32), _sds((2, 4, 1, GU), F32)],
        compiler_params=_params(2),
    )(u, u, conv_w, conv_b, da)


def _conv_transpose(dy, conv_w, tm):
    s = dy.shape[1]
    nt = s // tm

    def body(dy_ref, halo_ref, cw_ref, du_ref):
        keep = jnp.where(pl.program_id(1) == nt - 1, 0.0, 1.0)
        dv = dy_ref[0].astype(F32)
        rows = lax.broadcasted_iota(jnp.int32, (tm, 1), 0)
        h0 = halo_ref[0, 0:1, :].astype(F32) * keep
        h1 = halo_ref[0, 1:2, :].astype(F32) * keep
        dp1 = jnp.where(rows == tm - 1, h0, pltpu.roll(dv, tm - 1, 0))
        dp2 = jnp.where(rows == tm - 1, h1, jnp.where(rows == tm - 2, h0, pltpu.roll(dv, tm - 2, 0)))
        cw = cw_ref[0]
        du_ref[0] = (cw[2:3] * dv + cw[1:2] * dp1 + cw[0:1] * dp2).astype(BF16)

    blk = pl.BlockSpec((1, tm, GU), lambda g, i: (g, i, 0))
    return _call(
        body, name="conv_transpose", grid=(NDEV, nt),
        in_specs=[blk, pl.BlockSpec((1, HALO, GU),
                                    lambda g, i: (g, jnp.minimum((i + 1) * (tm // HALO), s // HALO - 1), 0)),
                  pl.BlockSpec((1, 3, GU), lambda g, i: (g, 0, 0))],
        out_specs=blk,
        out_shape=_sds((NDEV, s, GU), BF16),
        compiler_params=_params(2),
    )(dy, dy, conv_w)


def _wgrad(a3, b3, *, tk, name):
    ga, s, ka = a3.shape
    gb, _, nb = b3.shape
    groups = max(ga, gb)
    nk = s // tk

    def body(a_ref, b_ref, o_ref, acc):
        k = pl.program_id(1)

        @pl.when(k == 0)
        def _():
            acc[...] = jnp.zeros_like(acc)

        acc[...] += _dot_tn(a_ref[0], b_ref[0])

        @pl.when(k == nk - 1)
        def _():
            o_ref[0] = acc[...].astype(BF16)

    a_spec = pl.BlockSpec((1, tk, ka), (lambda g, k: (g, k, 0)) if ga > 1 else (lambda g, k: (0, k, 0)))
    b_spec = pl.BlockSpec((1, tk, nb), (lambda g, k: (g, k, 0)) if gb > 1 else (lambda g, k: (0, k, 0)))
    return _call(
        body, name=name, grid=(groups, nk),
        in_specs=[a_spec, b_spec], out_specs=pl.BlockSpec((1, ka, nb), lambda g, k: (g, 0, 0)),
        out_shape=_sds((groups, ka, nb), BF16),
        scratch_shapes=[pltpu.VMEM((ka, nb), F32)],
        compiler_params=_params(2),
    )(a3, b3)


def _wgrad_in(h1, dparts, tk, after):
    s = h1.shape[0]
    nk = s // tk

    def body(a_ref, *refs):
        d_refs, o_ref, acc = refs[:NT], refs[NT], refs[NT + 1]
        k = pl.program_id(0)

        @pl.when(k == 0)
        def _():
            acc[...] = jnp.zeros_like(acc)

        for j in range(NT):
            acc[:, WG * j:WG * (j + 1)] += _dot_tn(a_ref[...], d_refs[j][...])

        @pl.when(k == nk - 1)
        def _():
            o_ref[...] = acc[...].astype(BF16)

    kw = dict(
        name="wgrad_in", grid=(nk,),
        in_specs=[pl.BlockSpec((tk, D), lambda k: (k, 0))] + [pl.BlockSpec((tk, WG), lambda k: (k, 0))] * NT,
        out_specs=pl.BlockSpec((D, NT * WG), lambda k: (0, 0)),
        out_shape=_sds((D, NT * WG), BF16),
        scratch_shapes=[pltpu.VMEM((D, NT * WG), F32)],
        compiler_params=_params(1),
    )
    call = _call(body, **kw) if after is None else _call_after(body, after, **kw)
    return call(h1, *dparts)


def _dgrad_in(dparts, w, x_in, scale, dx_up, tm, after):
    s = x_in.shape[0]

    def body(*refs):
        d_refs = refs[:NT]
        w_ref, x_ref, sc_ref, up_ref, dx_ref, dsc_ref, dsh_ref = refs[NT:]

        @pl.when(pl.program_id(0) == 0)
        def _():
            dsc_ref[...] = jnp.zeros_like(dsc_ref)
            dsh_ref[...] = jnp.zeros_like(dsh_ref)

        dh = _dot_nt(d_refs[0][...], w_ref[:, 0:WG])
        for j in range(1, NT):
            dh = dh + _dot_nt(d_refs[j][...], w_ref[:, WG * j:WG * (j + 1)])
        xv = x_ref[...]
        r = _rms(xv)
        nrm = xv * r
        dsh_ref[...] += _colsum(dh)
        dsc_ref[...] += _colsum(dh * nrm)
        dx_ref[...] = up_ref[...] + _rms_bwd(dh * (1.0 + sc_ref[...]), nrm, r)

    row = pl.BlockSpec((tm, D), lambda i: (i, 0))
    vec = pl.BlockSpec((1, D), lambda i: (0, 0))
    return _call_after(
        body, after, name="dgrad_in", grid=(s // tm,),
        in_specs=[pl.BlockSpec((tm, WG), lambda i: (i, 0))] * NT + [pl.BlockSpec((D, NT * WG), lambda i: (0, 0)), row, vec, row],
        out_specs=[row, vec, vec],
        out_shape=[_sds((s, D), F32), _sds((1, D), F32), _sds((1, D), F32)],
        compiler_params=_params(1),
    )(*dparts, w, x_in, scale, dx_up)


def _dgrad_norm_bwd(d3, w, x_in, scale, dx_up, *, tm, name, after):
    groups, s, n = d3.shape

    def body(d_ref, w_ref, x_ref, sc_ref, up_ref, dx_ref, dsc_ref, dsh_ref, acc):
        i, g = pl.program_id(0), pl.program_id(1)

        @pl.when((i == 0) & (g == 0))
        def _():
            dsc_ref[...] = jnp.zeros_like(dsc_ref)
            dsh_ref[...] = jnp.zeros_like(dsh_ref)

        part = _dot_nt(d_ref[0], w_ref[0])

        @pl.when(g == 0)
        def _():
            acc[...] = part

        @pl.when(g > 0)
        def _():
            acc[...] += part

        @pl.when(g == groups - 1)
        def _():
            dh = acc[...]
            xv = x_ref[...]
            r = _rms(xv)
            nrm = xv * r
            dsh_ref[...] += _colsum(dh)
            dsc_ref[...] += _colsum(dh * nrm)
            dx_ref[...] = up_ref[...] + _rms_bwd(dh * (1.0 + sc_ref[...]), nrm, r)

    row = pl.BlockSpec((tm, D), lambda i, g: (i, 0))
    vec = pl.BlockSpec((1, D), lambda i, g: (0, 0))
    return _call_after(
        body, after, name=name, grid=(s // tm, groups),
        in_specs=[pl.BlockSpec((1, tm, n), lambda i, g: (g, i, 0)), pl.BlockSpec((1, D, n), lambda i, g: (g, 0, 0)),
                  row, vec, row],
        out_specs=[row, vec, vec],
        out_shape=[_sds((s, D), F32), _sds((1, D), F32), _sds((1, D), F32)],
        scratch_shapes=[pltpu.VMEM((tm, D), F32)],
        compiler_params=_params(2),
    )(d3, w, x_in, scale, dx_up)


def _mix_out_bwd(dx2, mixed, gate, w_out, oa, ob, g, tm):
    s = dx2.shape[0]

    def body(dx_ref, mixed_ref, gate_ref, w_ref, oa_ref, ob_ref, g_ref, dgate_ref, dm_ref, doa_ref, dob_ref, dg_ref):
        @pl.when(pl.program_id(0) == 0)
        def _():
            dgate_ref[...] = jnp.zeros_like(dgate_ref)
            dg_ref[...] = jnp.zeros_like(dg_ref)
        dxv = dx_ref[...]
        dgate_ref[...] += _colsum(dxv * mixed_ref[...])
        dm_ref[...] = (dxv * gate_ref[...]).astype(BF16)
        dnab = _dot_nt(dm_ref[...], w_ref[...])
        for o_ref, do_ref, sl in ((oa_ref, doa_ref, slice(0, WG)), (ob_ref, dob_ref, slice(WG, D))):
            ov = o_ref[...]
            r = _rms(ov)
            nrm = ov * r
            dn = dnab[:, sl]
            dg_ref[:, sl] += _colsum(dn * nrm)
            do_ref[...] = _rms_bwd(dn * g_ref[:, sl], nrm, r)

    row = pl.BlockSpec((tm, D), lambda i: (i, 0))
    half = pl.BlockSpec((tm, WG), lambda i: (i, 0))
    vec = pl.BlockSpec((1, D), lambda i: (0, 0))
    return _call(
        body, name="mix_out_bwd", grid=(s // tm,),
        in_specs=[row, row, vec, pl.BlockSpec((D, D), lambda i: (0, 0)), half, half, vec],
        out_specs=[vec, row, half, half, vec],
        out_shape=[_sds((1, D), F32), _sds((s, D), BF16), _sds((s, WG), F32), _sds((s, WG), F32), _sds((1, D), F32)],
        compiler_params=_params(1),
    )(dx2, mixed, gate, w_out, oa, ob, g)


def _attn_a_bwd(proj, bias_tab, d_oa, after):
    s = proj.shape[1]
    nq = s // TQA

    def body(q_ref, k_ref, v_ref, b_ref, do_ref, dq_ref, dk_ref, dv_ref, db_ref, dk_acc, dv_acc):
        qi = pl.program_id(1)

        @pl.when(qi == 0)
        def _():
            dk_acc[...] = jnp.zeros_like(dk_acc)
            dv_acc[...] = jnp.zeros_like(dv_acc)
            db_ref[...] = jnp.zeros_like(db_ref)

        def tile(nk, off, kstart):
            for h in range(2):
                sl = slice(HD * h, HD * (h + 1))
                q = q_ref[0, :, sl]
                k = k_ref[0, pl.ds(kstart, nk), sl]
                v = v_ref[0, pl.ds(kstart, nk), sl]
                do = do_ref[:, sl].astype(BF16)
                sc = _dot_nt(q, k) * 0.125 + b_ref[h, :, off:off + nk]
                p = jnp.exp(sc - jnp.max(sc, axis=-1, keepdims=True))
                p = p / jnp.sum(p, axis=-1, keepdims=True)
                dp = _dot_nt(do, v)
                ds = p * (dp - jnp.sum(dp * p, axis=-1, keepdims=True))
                db_ref[h, :, off:off + nk] += ds
                dsb = (ds * 0.125).astype(BF16)
                dq_ref[:, sl] = _dot(dsb, k).astype(BF16)
                dk_acc[pl.ds(kstart, nk), sl] += _dot_tn(dsb, q)
                dv_acc[pl.ds(kstart, nk), sl] += _dot_tn(p.astype(BF16), do)

        _attn_a_cases(qi, tile)

        @pl.when(qi == nq - 1)
        def _():
            dk_ref[...] = dk_acc[...].astype(BF16)
            dv_ref[...] = dv_acc[...].astype(BF16)

    q_spec, k_spec, v_spec, b_spec = _attn_a_specs(s)
    blk = pl.BlockSpec((TQA, 128), lambda hp, qi: (qi, hp))
    col = pl.BlockSpec((s, 128), lambda hp, qi: (0, hp))
    return _call_after(
        body, after, name="attn_a_bwd", grid=(4, nq),
        in_specs=[q_spec, k_spec, v_spec, b_spec, blk],
        out_specs=[blk, col, col, b_spec],
        out_shape=[_sds((s, WG), BF16), _sds((s, WG), BF16), _sds((s, WG), BF16), _sds((NDEV, TQA, KWA), F32)],
        scratch_shapes=[pltpu.VMEM((s, 128), F32), pltpu.VMEM((s, 128), F32)],
        compiler_params=_params(2),
    )(proj, proj, proj, bias_tab, d_oa)


def _attn_b_bwd(proj, lsum, d_ob):
    s = proj.shape[1]
    nq = s // TQB
    nkb = s // TKB
    ndiag = TQB // TKB

    def body(q_ref, k_ref, v_ref, ls_ref, do_ref, dq_ref, dk_ref, dv_ref, dkt_acc, dvt_acc):
        qi = pl.program_id(1)
        q0 = qi * TQB

        @pl.when(qi == 0)
        def _():
            dkt_acc[...] = jnp.zeros_like(dkt_acc)
            dvt_acc[...] = jnp.zeros_like(dvt_acc)

        rows = lax.broadcasted_iota(jnp.int32, (TQB, TKB), 0)
        cols = lax.broadcasted_iota(jnp.int32, (TQB, TKB), 1)
        uj = jnp.bitwise_and(lax.broadcasted_iota(jnp.int32, (2 * TKB, TKB), 0), TKB - 1)
        us = lax.broadcasted_iota(jnp.int32, (2 * TKB, TKB), 1)
        prefix = jnp.where(uj <= us, 1.0, 0.0).astype(BF16)
        heads = [slice(HD * h, HD * (h + 1)) for h in range(2)]
        qs = [(q_ref[0, :, sl].astype(F32) * 0.125).astype(BF16) for sl in heads]
        dos = [do_ref[:, sl].astype(BF16) for sl in heads]
        ei = lax.broadcasted_iota(jnp.int32, (TQB, TQB), 0)
        ej = lax.broadcasted_iota(jnp.int32, (TQB, TQB), 1)
        eye = jnp.where(ei == ej, 1.0, 0.0).astype(BF16)
        qts = [_dot_tn(x, eye).astype(BF16) for x in qs]
        dots = [_dot_tn(x, eye).astype(BF16) for x in dos]
        stots = [ls_ref[0, :, HD * h:HD * h + 1] for h in range(2)]
        qds = [jnp.concatenate([qs[h], dos[h]], axis=1) for h in range(2)]
        zkv = jnp.zeros((TKB, HD), BF16)

        def tile(h, kb, diag, cl, cg, dq):
            sl = heads[h]
            kstart = pl.multiple_of(kb * TKB, TKB)
            k = k_ref[0, pl.ds(kstart, TKB), sl]
            v = v_ref[0, pl.ds(kstart, TKB), sl]
            kv = jnp.concatenate([jnp.concatenate([k, zkv], axis=1), jnp.concatenate([zkv, v], axis=1)], axis=0)
            both = _dot_nt(qds[h], kv)
            lb, sp = _sb_terms(both[:, :TKB])
            if diag is not None:
                strict = rows > cols + diag * TKB
                sp = jnp.where(strict, sp, 0.0)
            pre = _dot2_deep(sp, prefix) + cl
            a = jnp.exp(lb - stots[h] + pre)
            if diag is not None:
                a = jnp.where(strict, a, 0.0)
            gz = both[:, TKB:] * a
            pg = _dot2_deep(gz, prefix) + cg
            dl = gz - pg * jnp.exp(lb)
            if diag is not None:
                dl = jnp.where(strict, dl, 0.0)
            dlb = dl.astype(BF16)
            dkt_acc[h * nkb + kb] += _dot(qts[h], dlb)
            dvt_acc[h * nkb + kb] += _dot(dots[h], a.astype(BF16))
            return pre[:, TKB - 1:TKB], pg[:, TKB - 1:TKB], dq + _dot(dlb, k)

        state = [jnp.zeros((TQB, 1), F32), jnp.zeros((TQB, 1), F32), jnp.zeros((TQB, HD), F32)] * 2

        def step(i, st):
            st = list(st)
            for sub in range(KSTEP):
                for h in range(2):
                    st[3 * h:3 * h + 3] = tile(h, i * KSTEP + sub, None, *st[3 * h:3 * h + 3])
            return tuple(st)

        state = list(lax.fori_loop(0, q0 // (KSTEP * TKB), step, tuple(state)))
        for d in range(ndiag):
            for h in range(2):
                state[3 * h:3 * h + 3] = tile(h, q0 // TKB + d, d, *state[3 * h:3 * h + 3])
        for h in range(2):
            dq_ref[:, heads[h]] = (state[3 * h + 2] * 0.125).astype(BF16)

        @pl.when(qi == nq - 1)
        def _():
            eye64 = eye[:HD, :HD]
            for h in range(2):
                for kb in range(nkb):
                    rows_kb = slice(kb * TKB, (kb + 1) * TKB)
                    dk_ref[rows_kb, heads[h]] = _dot2_tn(dkt_acc[h * nkb + kb], eye64).astype(BF16)
                    dv_ref[rows_kb, heads[h]] = _dot2_tn(dvt_acc[h * nkb + kb], eye64).astype(BF16)

    q_spec, k_spec, v_spec = _attn_b_specs(s)
    blk = pl.BlockSpec((TQB, 128), lambda hp, qi: (qi, hp))
    col = pl.BlockSpec((s, 128), lambda hp, qi: (0, hp))
    return _call(
        body, name="attn_b_bwd", grid=(4, nq),
        in_specs=[q_spec, k_spec, v_spec, pl.BlockSpec((1, TQB, 128), lambda hp, qi: (hp, qi, 0)), blk],
        out_specs=[blk, col, col],
        out_shape=[_sds((s, WG), BF16)] * 3,
        scratch_shapes=[pltpu.VMEM((2 * nkb, HD, TKB), F32), pltpu.VMEM((2 * nkb, HD, TKB), F32)],
        compiler_params=_params(2),
    )(proj, proj, proj, lsum, d_ob)


def _bias_fold(dtab):
    def body(t_ref, d_ref, far_ref):
        acc = jnp.zeros((NDEV, TABW), F32)
        zpad = jnp.zeros((NDEV, TAB0), F32)
        for r in range(TQA):
            row = jnp.concatenate([zpad, t_ref[:, r, :]], axis=1)
            acc = acc + (pltpu.roll(row, TABW - r, 1) if r else row)
        d_ref[...] = acc
        lane = lax.broadcasted_iota(jnp.int32, (NDEV, TABW), 1)
        far = jnp.sum(jnp.where(lane < N_FAR, acc, 0.0), axis=1, keepdims=True)
        far_ref[...] = jnp.broadcast_to(far, (NDEV, 128))

    d_fpad, d_far = _call(
        body, name="bias_fold", grid=(1,),
        in_specs=[pl.BlockSpec((NDEV, TQA, KWA), lambda i: (0, 0, 0))],
        out_specs=[pl.BlockSpec((NDEV, TABW), lambda i: (0, 0)), pl.BlockSpec((NDEV, 128), lambda i: (0, 0))],
        out_shape=[_sds((NDEV, TABW), F32), _sds((NDEV, 128), F32)],
        compiler_params=_params(1),
    )(dtab)
    d_near = d_fpad[:, N_FAR:N_FAR + N_NEAR][:, ::-1]
    return jnp.concatenate([jnp.zeros((NDEV, REL_CLIP - CHUNK + 1), F32), d_near, d_far[:, :1]], axis=1)


def _adamw(w, g, m, v):
    m = B1 * m + (1.0 - B1) * g
    v = B2 * v + (1.0 - B2) * (g * g)
    m_hat = m / (1.0 - B1 ** STEP)
    v_hat = v / (1.0 - B2 ** STEP)
    delta = -LR * (m_hat / (jnp.sqrt(v_hat) + AEPS) + WD * w)
    return delta, m, v


def _adamw_big(recv0, recv1, w, m, v):
    _, rows, cols = recv0.shape
    tr = rows if rows <= 512 else 256
    nt = rows // tr

    def body(r0_ref, r1_ref, w_ref, m_ref, v_ref, g_ref, d_ref, nm_ref, nv_ref):
        def update(r_ref):
            g = r_ref[0].astype(F32)
            for p in range(1, NDEV):
                g = g + r_ref[p].astype(F32)
            delta, nm, nv = _adamw(w_ref[0], g, m_ref[0], v_ref[0])
            g_ref[0], d_ref[0], nm_ref[0], nv_ref[0] = g, delta, nm, nv

        @pl.when(pl.program_id(0) == 0)
        def _():
            update(r0_ref)

        @pl.when(pl.program_id(0) == 1)
        def _():
            update(r1_ref)

    blk = pl.BlockSpec((1, tr, cols), lambda l, i: (l, i, 0))
    r0_spec = pl.BlockSpec((NDEV, tr, cols), lambda l, i: (0, jnp.where(l == 0, i, nt - 1), 0))
    r1_spec = pl.BlockSpec((NDEV, tr, cols), lambda l, i: (0, jnp.where(l == 1, i, 0), 0))
    return _call(
        body, name="adamw_big", grid=(2, nt),
        in_specs=[r0_spec, r1_spec, blk, blk, blk],
        out_specs=[blk] * 4,
        out_shape=[_sds((2, rows, cols), F32)] * 4,
        compiler_params=_params(2),
    )(recv0, recv1, w, m, v)


def _adamw_w_ada(cact_t, dmod, w, m, v):
    tr = 256

    def body(c_ref, dm_ref, w_ref, m_ref, v_ref, g_ref, d_ref, nm_ref, nv_ref):
        g = c_ref[:, 0:1] * dm_ref[0, 0:1, :]
        for b in range(1, NDEV):
            g = g + c_ref[:, b:b + 1] * dm_ref[0, b:b + 1, :]
        delta, nm, nv = _adamw(w_ref[0], g, m_ref[0], v_ref[0])
        g_ref[0], d_ref[0], nm_ref[0], nv_ref[0] = g, delta, nm, nv

    blk = pl.BlockSpec((1, tr, 768), lambda l, i: (l, i, 0))
    return _call(
        body, name="adamw_w_ada", grid=(2, D // tr),
        in_specs=[pl.BlockSpec((tr, NDEV), lambda l, i: (i, 0)), pl.BlockSpec((1, NDEV, 768), lambda l, i: (l, 0, 0)),
                  blk, blk, blk],
        out_specs=[blk] * 4,
        out_shape=[_sds((2, D, 768), F32)] * 4,
        compiler_params=_params(2),
    )(cact_t, dmod, w, m, v)


def _adamw_small(gath, w, m, v):
    rows = gath.shape[1]

    def body(r_ref, w_ref, m_ref, v_ref, g_ref, d_ref, nm_ref, nv_ref):
        g = r_ref[0]
        for p in range(1, NDEV):
            g = g + r_ref[p]
        delta, nm, nv = _adamw(w_ref[...], g, m_ref[...], v_ref[...])
        g_ref[...], d_ref[...], nm_ref[...], nv_ref[...] = g, delta, nm, nv

    blk = pl.BlockSpec((rows, D), lambda i: (0, 0))
    return _call(
        body, name="adamw_small", grid=(1,),
        in_specs=[pl.BlockSpec((NDEV, rows, D), lambda i: (0, 0, 0)), blk, blk, blk],
        out_specs=[blk] * 4,
        out_shape=[_sds((rows, D), F32)] * 4,
        compiler_params=_params(1),
    )(gath, w, m, v)


_PACK = (("b_ada", 2 * 6 * D), ("rel_bias", 2 * 8 * 257), ("g_a", 2 * WG), ("g_b", 2 * WG),
         ("conv_b", 2 * 2 * DFF), ("final_g", D), ("conv_w", 2 * NDEV * 3 * GU))


def _pack(parts):
    rows = []
    for name, size in _PACK:
        flat = parts[name].reshape(-1).astype(F32)
        assert flat.shape[0] == size, (name, flat.shape)
        rows.append(jnp.pad(flat, (0, -size % D)))
    out = jnp.concatenate(rows).reshape(-1, D)
    return jnp.pad(out, ((0, -out.shape[0] % 8), (0, 0)))


def _unpack(packed):
    flat = packed.reshape(-1)
    out, pos = {}, 0
    for name, size in _PACK:
        out[name] = flat[pos:pos + size]
        pos += size + (-size % D)
    return out


def kernel(x, c, w_ada, b_ada, w_in, rel_bias, g_a, g_b, w_out, w_up, conv_w, conv_b, w_down, final_g, loss_target, m_w_ada, m_b_ada, m_w_in, m_rel_bias, m_g_a, m_g_b, m_w_out, m_w_up, m_conv_w, m_conv_b, m_w_down, m_final_g, v_w_ada, v_b_ada, v_w_in, v_rel_bias, v_g_a, v_g_b, v_w_out, v_w_up, v_conv_w, v_conv_b, v_w_down, v_final_g):
    s = x.shape[1]
    assert s % TQA == 0 and s >= KWA and s % 512 == 0
    tm = 512
    tmm = min(1024, s)
    me = 4 * lax.axis_index("x") + 2 * lax.axis_index("y") + lax.axis_index("c")
    xs = x.reshape(s, D)
    target = loss_target.reshape(s, D)

    first = jnp.concatenate([c, jnp.pad(conv_w.reshape(2 * 3, GU), ((0, 1), (0, D - GU)))])
    first_all = _small_allgather(first, "gather_c_conv_w")
    c_all = first_all[:, 0, :]
    cw_all = first_all[:, 1:7, :GU].reshape(NDEV, 2, 3, GU)

    b_sl = lax.dynamic_slice(b_ada, (0, me * 768), (2, 768)).reshape(2, 1, 768)
    mod_part, cact = _mod_fwd(c_all, w_ada, b_sl)
    mod_all = _small_allgather(mod_part.reshape(2 * NDEV, 768), "gather_mod")

    shards = {"in": w_in, "out": w_out, "up": w_up, "down": w_down}
    order = [(kind, l) for l in range(2) for kind in _KINDS]
    mod_all, *srcs = lax.optimization_barrier((mod_all, *[shards[kind][l].astype(BF16) for kind, l in order]))
    gather_started = dict(zip(order, _exchange_start([kind for kind, _ in order], True, srcs, "weights_gather_start")))

    def gathered(kind, l, after):
        return _exchange_wait([kind], True, [gather_started[kind, l]], after, f"weights_gather_wait_{kind}{l}")[0]

    mod_all = mod_all.reshape(NDEV, 2, NDEV, 768)
    mod_me = lax.dynamic_index_in_dim(mod_all, me, axis=2, keepdims=False)
    mod = jnp.transpose(mod_me, (1, 0, 2)).reshape(2, 6, 1, D)

    saved = []
    xl = xs
    for l in range(2):
        sh_mix, sc_mix, gt_mix, sh_ffn, sc_ffn, gt_ffn = (mod[l, j] for j in range(6))
        cw = cw_all[:, l].reshape(2, 4, 3, GU)
        cb = conv_b[l].reshape(2, 4, 1, GU)
        gvec = jnp.concatenate([g_a[l], g_b[l]]).reshape(1, D)
        tab = _bias_table(rel_bias[l])

        wi = gathered("in", l, xl if l else mod)
        h1, proj = _nm_matmul(xl, sh_mix, sc_mix, wi, two_d=True, n=WG, groups=NT, out_dtype=BF16, tm=tmm,
                              name="norm_proj")
        oa = _attn_a_fwd(proj, tab)
        ob, lsum = _attn_b_fwd(proj)
        wo = gathered("out", l, ob)
        nab, mixed, x2 = _mix_out(oa, ob, gvec, wo, xl, gt_mix, tm)
        wu = gathered("up", l, x2)
        h2, u = _nm_matmul(x2, sh_ffn, sc_ffn, wu, two_d=False, n=GU, groups=NDEV, out_dtype=BF16, tm=tmm,
                           name="norm_up")
        u = u.reshape(2, 4, s, GU)
        a = _conv_act(u, cw, cb, tm)
        wd4 = gathered("down", l, a).reshape(4, GU, D)
        ffn, x3 = _down(a, wd4, x2, gt_ffn, tmm)
        saved.append(dict(x=xl, h1=h1, proj=proj, oa=oa, ob=ob, lsum=lsum, nab=nab, mixed=mixed, x2=x2, h2=h2, u=u,
                          a=a, ffn=ffn, cw=cw, cb=cb, gvec=gvec, tab=tab, wd4=wd4, wi=wi, wo=wo, wu=wu))
        xl = x3

    loss_part, dx, d_final_g = _final_loss(xl, final_g.reshape(1, D), target, tm)
    loss = lax.psum(loss_part[0, 0], ("x", "y", "c"))

    sent = {}
    small = {"b_ada": [None, None], "rel_bias": [None, None], "g_a": [None, None], "g_b": [None, None],
             "conv_b": [None, None], "conv_w": [None, None]}

    def send(kind, l, grad):
        started, token = _exchange_start([kind], False, [grad], f"grads_start_{kind}{l}", with_token=True)
        sent[kind, l] = started[0]
        return token

    for l in (1, 0):
        sv = saved[l]
        sh_mix, sc_mix, gt_mix, sh_ffn, sc_ffn, gt_ffn = (mod[l, j] for j in range(6))
        d_gt_ffn, dff, da = _down_bwd(dx, gt_ffn, sv["ffn"], sv["wd4"], tmm)
        tok = send("down", l, _wgrad(sv["a"], dff.reshape(1, s, D), tk=tmm, name="wgrad_down").reshape(DFF, D))
        dy, d_cw, d_cb = _conv_act_bwd(sv["u"], sv["cw"], sv["cb"], da, tm, tok)
        du = _conv_transpose(dy.reshape(NDEV, s, GU), sv["cw"].reshape(NDEV, 3, GU), tm)
        tok = send("up", l, _wgrad(sv["h2"].reshape(1, s, D), du, tk=tmm, name="wgrad_up"))
        dx2, d_sc_ffn, d_sh_ffn = _dgrad_norm_bwd(du, sv["wu"], sv["x2"], sc_ffn, dx, tm=tmm, name="dgrad_up", after=tok)
        d_gt_mix, dmixed, d_oa, d_ob, d_g = _mix_out_bwd(dx2, sv["mixed"], gt_mix, sv["wo"], sv["oa"], sv["ob"],
                                                         sv["gvec"], tm)
        tok = send("out", l, _wgrad(sv["nab"].reshape(1, s, D), dmixed.reshape(1, s, D), tk=tmm,
                                    name="wgrad_out").reshape(D, D))
        dqa, dka, dva, d_tab = _attn_a_bwd(sv["proj"], sv["tab"], d_oa, tok)
        dqb, dkb, dvb = _attn_b_bwd(sv["proj"], sv["lsum"], d_ob)
        dparts = (dqa, dka, dva, dqb, dkb, dvb)
        if l:
            tok = send("in", l, _wgrad_in(sv["h1"], dparts, tmm, None))
        dx, d_sc_mix, d_sh_mix = _dgrad_in(dparts, sv["wi"], sv["x"], sc_mix, dx2, tm, tok)
        small["b_ada"][l] = jnp.concatenate([d_sh_mix, d_sc_mix, d_gt_mix, d_sh_ffn, d_sc_ffn, d_gt_ffn], axis=1)
        small["rel_bias"][l] = _bias_fold(d_tab)
        small["g_a"][l], small["g_b"][l] = d_g[:, :WG], d_g[:, WG:]
        small["conv_b"][l] = d_cb
        small["conv_w"][l] = d_cw.reshape(NDEV, 3, GU)
    grad_x = dx.reshape(1, s, D)

    contrib = {k: jnp.stack(vs) for k, vs in small.items()}
    contrib["final_g"] = d_final_g
    gath = _small_allgather(_pack(contrib), "gather_small_grads")
    tok = send("in", 0, _wgrad_in(saved[0]["h1"], dparts, tmm, gath))

    def place_conv_w(t):
        return lax.dynamic_update_slice(jnp.zeros((2, NDEV, 3, GU), F32), t.reshape(2, 1, 3, GU), (0, me, 0, 0))

    def packed_params(b, rb, ga, gb, cb_, fg, cw_):
        return _pack({"b_ada": b, "rel_bias": rb, "g_a": ga, "g_b": gb, "conv_b": cb_, "final_g": fg,
                      "conv_w": place_conv_w(cw_)})

    sm = _adamw_small(gath,
                      packed_params(b_ada, rel_bias, g_a, g_b, conv_b, final_g, conv_w),
                      packed_params(m_b_ada, m_rel_bias, m_g_a, m_g_b, m_conv_b, m_final_g, m_conv_w),
                      packed_params(v_b_ada, v_rel_bias, v_g_a, v_g_b, v_conv_b, v_final_g, v_conv_w))
    sm = [_unpack(t) for t in sm]

    dmod_all = gath[:, :12, :].reshape(NDEV, 2, 6 * D)
    dmod_sl = jnp.transpose(lax.dynamic_slice(dmod_all, (0, 0, me * 768), (NDEV, 2, 768)), (1, 0, 2))
    ada = _adamw_w_ada(cact.T, dmod_sl, w_ada, m_w_ada, v_w_ada)

    big = {}
    for kind, (w, m, v) in (("down", (w_down, m_w_down, v_w_down)), ("up", (w_up, m_w_up, v_w_up)),
                            ("out", (w_out, m_w_out, v_w_out)), ("in", (w_in, m_w_in, v_w_in))):
        recv0, recv1 = _exchange_wait([kind, kind], False, [sent[kind, 0], sent[kind, 1]], tok, f"grads_wait_{kind}")
        big[kind] = _adamw_big(recv0, recv1, w, m, v)
        tok = big[kind][0]

    def small_out(j, name):
        t = sm[j][name]
        if name == "b_ada":
            return t.reshape(2, 6 * D)
        if name == "rel_bias":
            return t.reshape(2, 8, 257)
        if name in ("g_a", "g_b"):
            return t.reshape(2, WG)
        if name == "conv_b":
            return t.reshape(2, 2 * DFF)
        if name == "final_g":
            return t.reshape(D)
        t = t.reshape(2, NDEV, 3, GU)
        return lax.dynamic_index_in_dim(t, me, axis=1, keepdims=False)

    def group(j):
        return (ada[j], small_out(j, "b_ada"), big["in"][j], small_out(j, "rel_bias"), small_out(j, "g_a"),
                small_out(j, "g_b"), big["out"][j], big["up"][j], small_out(j, "conv_w"), small_out(j, "conv_b"),
                big["down"][j], small_out(j, "final_g"))

    return (loss, grad_x, *group(0), *group(1), *group(2), *group(3))
```

```python
import jax
import jax.numpy as jnp
from jax import lax
from jax.experimental import pallas as pl
from jax.experimental.pallas import tpu as pltpu

F32, BF16 = jnp.float32, jnp.bfloat16
MESH_ID = pl.DeviceIdType.MESH
NDEV = 8
D = 1024
HD = 64
WG = 512
NT = 6
GU = 704
DFF = 2816
CHUNK, NPREV, REL_CLIP = 64, 8, 128
BAND = (NPREV + 1) * CHUNK
EPS = 1e-6
NEG = -1e30
TQA = 256
KWA = TQA + NPREV * CHUNK
TABW = 1024
TAB0 = TABW - KWA
N_FAR = TAB0 + NPREV * CHUNK - REL_CLIP + 1
N_NEAR = REL_CLIP + CHUNK - 1
TQB, TKB = 512, 128
KSTEP = 4
LR, B1, B2, AEPS, WD, STEP = 0.001, 0.9, 0.999, 1e-08, 0.01, 10
VMEM_MB = 56


def _call(body, **kw):
    return pl.pallas_call(body, **kw)


def _call_after(body, after, **kw):
    n_in = len(kw["in_specs"])
    kw["in_specs"] = list(kw["in_specs"]) + [pl.BlockSpec(memory_space=pl.ANY)]

    def tied(*refs):
        body(*refs[:n_in], *refs[n_in + 1:])

    call = _call(tied, **kw)
    return lambda *args: call(*args, after)


def _params(n_axes):
    return pltpu.CompilerParams(dimension_semantics=("arbitrary",) * n_axes, vmem_limit_bytes=VMEM_MB << 20)


def _dot(a, b):
    return jnp.dot(a, b, preferred_element_type=F32)


def _dot_nt(a, b):
    return lax.dot_general(a, b, (((1,), (1,)), ((), ())), preferred_element_type=F32)


def _dot_tn(a, b):
    return lax.dot_general(a, b, (((0,), (0,)), ((), ())), preferred_element_type=F32)


def _dot2(x, u):
    hi = x.astype(BF16)
    lo = (x - hi.astype(F32)).astype(BF16)
    return _dot(hi, u) + _dot(lo, u)


def _dot2_deep(x, uu):
    hi = x.astype(BF16)
    lo = (x - hi.astype(F32)).astype(BF16)
    return _dot(jnp.concatenate([hi, lo], axis=1), uu)


def _dot2_tn(x, u):
    hi = x.astype(BF16)
    lo = (x - hi.astype(F32)).astype(BF16)
    return _dot_tn(hi, u) + _dot_tn(lo, u)


def _rms(x):
    return lax.rsqrt(jnp.mean(x * x, axis=-1, keepdims=True) + EPS)


def _rms_bwd(dn, n, r):
    return r * (dn - n * jnp.mean(dn * n, axis=-1, keepdims=True))


def _colsum(x):
    return jnp.sum(x, axis=0, keepdims=True)


def _sigmoid(x):
    return 1.0 / (1.0 + jnp.exp(-x))


def _sds(shape, dtype):
    return jax.ShapeDtypeStruct(shape, dtype)


def _place():
    x, y, c = lax.axis_index("x"), lax.axis_index("y"), lax.axis_index("c")
    return x, y, c, 4 * x + 2 * y + c


def _peer(x, y, c, k):
    px = 1 - x if k & 4 else x
    py = 1 - y if k & 2 else y
    pc = 1 - c if k & 1 else c
    return (px, py, pc), 4 * px + 2 * py + pc


def _small_allgather(v, name):
    rows, cols = v.shape

    def body(v_ref, out_ref, send_sems, recv_sems, local_sem):
        x, y, c, me = _place()
        mine = pltpu.make_async_copy(v_ref, out_ref.at[me], local_sem)
        mine.start()
        sends = []
        for k in range(1, NDEV):
            peer, _ = _peer(x, y, c, k)
            cp = pltpu.make_async_remote_copy(v_ref, out_ref.at[me], send_sems.at[k - 1], recv_sems.at[k - 1],
                                              device_id=peer, device_id_type=MESH_ID)
            cp.start()
            sends.append(cp)
        for k in range(1, NDEV):
            peer, pidx = _peer(x, y, c, k)
            pltpu.make_async_remote_copy(v_ref, out_ref.at[pidx], send_sems.at[k - 1], recv_sems.at[k - 1],
                                         device_id=peer, device_id_type=MESH_ID).wait_recv()
        for cp in sends:
            cp.wait_send()
        mine.wait()

    return _call(
        body, name=name,
        out_shape=_sds((NDEV, rows, cols), F32),
        in_specs=[pl.BlockSpec(memory_space=pltpu.VMEM)],
        out_specs=pl.BlockSpec(memory_space=pltpu.VMEM),
        scratch_shapes=[pltpu.SemaphoreType.DMA((NDEV - 1,)), pltpu.SemaphoreType.DMA((NDEV - 1,)),
                        pltpu.SemaphoreType.DMA],
    )(v)


def _shard_view(ref, kind, p):
    if kind == "in":
        return ref.at[:, pl.ds(pl.multiple_of(p * 384, 128), 384)]
    if kind == "out":
        return ref.at[pl.ds(pl.multiple_of(p * 128, 128), 128), :]
    if kind == "up":
        return ref.at[p]
    if kind == "down":
        return ref.at[pl.ds(pl.multiple_of(p * 352, 16), 352), :]
    raise ValueError(kind)


_KINDS = ("in", "out", "up", "down")
_FULL_SHAPES = {"in": (D, 3 * D), "out": (D, D), "up": (NDEV, GU, D), "down": (DFF, D)}
_SHARD_SHAPES = {"in": (D, 384), "out": (128, D), "up": (GU, D), "down": (352, D)}


_HBM = pl.BlockSpec(memory_space=pltpu.HBM)
_SEM = pl.BlockSpec(memory_space=pltpu.SEMAPHORE)
_EFFECT = pltpu.SideEffectType.DATAFLOW_SIDE_EFFECTING
_SEM_SHAPES = (pltpu.SemaphoreType.DMA((NDEV - 1,)), pltpu.SemaphoreType.DMA((NDEV - 1,)), pltpu.SemaphoreType.DMA(()))


def _hbm(a):
    return pltpu.with_memory_space_constraint(a, pltpu.HBM)


def _exchange_copies(kind, gather, src, land, sems):
    send_sems, recv_sems, local_sem = sems
    x, y, c, me = _place()

    def ends(p_dst, p_from):
        if gather:
            return src, _shard_view(land, kind, me), _shard_view(land, kind, p_from)
        return _shard_view(src, kind, p_dst), land.at[me], land.at[p_from]

    s_me, d_me, _ = ends(me, me)
    local = pltpu.make_async_copy(s_me, d_me, local_sem)
    sends, arrivals = [], []
    for k in range(1, NDEV):
        peer, pidx = _peer(x, y, c, k)
        s_k, d_k, from_k = ends(pidx, pidx)
        sends.append(pltpu.make_async_remote_copy(s_k, d_k, send_sems.at[k - 1], recv_sems.at[k - 1],
                                                  device_id=peer, device_id_type=MESH_ID))
        arrivals.append(pltpu.make_async_remote_copy(s_k, from_k, send_sems.at[k - 1], recv_sems.at[k - 1],
                                                     device_id=peer, device_id_type=MESH_ID))
    return local, sends, arrivals


def _exchange_start(kinds, gather, srcs, name, with_token=False):
    n = len(kinds)
    lands = [lax.empty(_FULL_SHAPES[kd] if gather else (NDEV,) + _SHARD_SHAPES[kd], BF16) for kd in kinds]

    def body(*refs):
        ins, sems = refs[:2 * n], refs[2 * n:5 * n]
        for j, kd in enumerate(kinds):
            local, sends, _ = _exchange_copies(kd, gather, ins[j], ins[n + j], sems[3 * j:3 * j + 3])
            local.start()
            for cp in sends:
                cp.start()
        if with_token:
            token = refs[7 * n]
            token[...] = jnp.zeros_like(token)

    out_shape = list(_SEM_SHAPES) * n
    out_shape += [pltpu.HBM(a.shape, a.dtype) for a in srcs] + [pltpu.HBM(a.shape, a.dtype) for a in lands]
    out_specs = [_SEM] * (3 * n) + [_HBM] * (2 * n)
    if with_token:
        out_shape.append(_sds((8, 128), F32))
        out_specs.append(pl.BlockSpec(memory_space=pltpu.VMEM))
    outs = _call(
        body, name=name, out_shape=out_shape,
        in_specs=[_HBM] * (2 * n), out_specs=out_specs,
        input_output_aliases={i: 3 * n + i for i in range(2 * n)},
        compiler_params=pltpu.CompilerParams(has_side_effects=_EFFECT),
    )(*[_hbm(a) for a in srcs], *[_hbm(a) for a in lands])
    per_tensor = [(tuple(outs[3 * j:3 * j + 3]), outs[3 * n + j], outs[4 * n + j]) for j in range(n)]
    return (per_tensor, outs[5 * n]) if with_token else per_tensor


def _exchange_wait(kinds, gather, started, after, name):
    n = len(kinds)

    def body(*refs):
        ins, sems = refs[:2 * n], refs[2 * n:5 * n]
        for j, kd in enumerate(kinds):
            local, sends, arrivals = _exchange_copies(kd, gather, ins[j], ins[n + j], sems[3 * j:3 * j + 3])
            local.wait()
            for cp in arrivals:
                cp.wait_recv()
            for cp in sends:
                cp.wait_send()

    srcs = [st[1] for st in started]
    lands = [st[2] for st in started]
    sems = [sm for st in started for sm in st[0]]
    outs = _call(
        body, name=name,
        out_shape=[pltpu.HBM(a.shape, a.dtype) for a in srcs] + [pltpu.HBM(a.shape, a.dtype) for a in lands],
        in_specs=[_HBM] * (2 * n) + [_SEM] * (3 * n) + [pl.BlockSpec(memory_space=pl.ANY)],
        out_specs=[_HBM] * (2 * n),
        input_output_aliases={i: i for i in range(2 * n)},
        compiler_params=pltpu.CompilerParams(has_side_effects=_EFFECT),
    )(*srcs, *lands, *sems, after)
    return outs[n:]


def _mod_fwd(c_all, w_ada, b_sl):
    def body(c_ref, w_ref, b_ref, mod_ref, cact_ref):
        cv = c_ref[...]
        ca = cv * _sigmoid(cv)
        cact_ref[...] = ca
        mod_ref[0] = _dot(ca.astype(BF16), w_ref[0].astype(BF16)) + b_ref[0]

    return _call(
        body, name="mod_fwd", grid=(2,),
        in_specs=[pl.BlockSpec((NDEV, D), lambda l: (0, 0)), pl.BlockSpec((1, D, 768), lambda l: (l, 0, 0)),
                  pl.BlockSpec((1, 1, 768), lambda l: (l, 0, 0))],
        out_specs=[pl.BlockSpec((1, NDEV, 768), lambda l: (l, 0, 0)), pl.BlockSpec((NDEV, D), lambda l: (0, 0))],
        out_shape=[_sds((2, NDEV, 768), F32), _sds((NDEV, D), F32)],
        compiler_params=_params(1),
    )(c_all, w_ada, b_sl)


def _nm_matmul(x, shift, scale, w, *, two_d, n, groups, out_dtype, tm, name):
    s = x.shape[0]

    def body(x_ref, sh_ref, sc_ref, w_ref, h_ref, o_ref):
        @pl.when(pl.program_id(1) == 0)
        def _():
            xv = x_ref[...]
            h_ref[...] = ((xv * _rms(xv)) * (1.0 + sc_ref[...]) + sh_ref[...]).astype(BF16)
        if two_d:
            o_ref[0] = _dot(h_ref[...], w_ref[...]).astype(out_dtype)
        else:
            o_ref[0] = _dot_nt(h_ref[...], w_ref[0]).astype(out_dtype)

    vec = pl.BlockSpec((1, D), lambda i, g: (0, 0))
    w_spec = pl.BlockSpec((D, n), lambda i, g: (0, g)) if two_d else pl.BlockSpec((1, n, D), lambda i, g: (g, 0, 0))
    return _call(
        body, name=name, grid=(s // tm, groups),
        in_specs=[pl.BlockSpec((tm, D), lambda i, g: (i, 0)), vec, vec, w_spec],
        out_specs=[pl.BlockSpec((tm, D), lambda i, g: (i, 0)), pl.BlockSpec((1, tm, n), lambda i, g: (g, i, 0))],
        out_shape=[_sds((s, D), BF16), _sds((groups, s, n), out_dtype)],
        compiler_params=_params(2),
    )(x, shift, scale, w)


def _bias_table(rel_bias):
    far = jnp.broadcast_to(rel_bias[:, 2 * REL_CLIP:], (NDEV, N_FAR))
    near = rel_bias[:, 2 * REL_CLIP - 1:REL_CLIP - CHUNK:-1]
    fpad = jnp.concatenate([far, near, jnp.zeros((NDEV, TABW - N_FAR - N_NEAR), F32)], axis=1)

    def body(f_ref, o_ref):
        t = pltpu.roll(jnp.broadcast_to(f_ref[0], (TQA, TABW)), 0, 1, stride=1, stride_axis=0)[:, TAB0:]
        rows = lax.broadcasted_iota(jnp.int32, (TQA, KWA), 0)
        cols = lax.broadcasted_iota(jnp.int32, (TQA, KWA), 1)
        first = jnp.bitwise_and(rows, -CHUNK)
        o_ref[0] = jnp.where((cols >= first) & (cols < first + BAND), t, NEG)

    return _call(
        body, name="bias_table", grid=(NDEV,),
        in_specs=[pl.BlockSpec((1, 1, TABW), lambda h: (h, 0, 0))],
        out_specs=pl.BlockSpec((1, TQA, KWA), lambda h: (h, 0, 0)),
        out_shape=_sds((NDEV, TQA, KWA), F32),
        compiler_params=_params(1),
    )(fpad.reshape(NDEV, 1, TABW))


def _attn_a_cases(qi, tile):
    @pl.when(qi == 0)
    def _():
        tile(TQA, 2 * TQA, 0)

    @pl.when(qi == 1)
    def _():
        tile(2 * TQA, TQA, 0)

    @pl.when(qi >= 2)
    def _():
        tile(KWA, 0, pl.multiple_of((qi - 2) * TQA, TQA))


def _attn_a_specs(s):
    q_spec = pl.BlockSpec((1, TQA, 128), lambda hp, qi: (0, qi, hp))
    k_spec = pl.BlockSpec((1, s, 128), lambda hp, qi: (1, 0, hp))
    v_spec = pl.BlockSpec((1, s, 128), lambda hp, qi: (2, 0, hp))
    b_spec = pl.BlockSpec((2, TQA, KWA), lambda hp, qi: (hp, 0, 0))
    return q_spec, k_spec, v_spec, b_spec


def _attn_a_fwd(proj, bias_tab):
    s = proj.shape[1]

    def body(q_ref, k_ref, v_ref, b_ref, o_ref):
        def tile(nk, off, kstart):
            for h in range(2):
                sl = slice(HD * h, HD * (h + 1))
                q = q_ref[0, :, sl]
                k = k_ref[0, pl.ds(kstart, nk), sl]
                v = v_ref[0, pl.ds(kstart, nk), sl]
                sc = _dot_nt(q, k) * 0.125 + b_ref[h, :, off:off + nk]
                p = jnp.exp(sc - jnp.max(sc, axis=-1, keepdims=True))
                den = jnp.sum(p, axis=-1, keepdims=True)
                o_ref[:, sl] = _dot(p.astype(BF16), v) / den

        _attn_a_cases(pl.program_id(1), tile)

    q_spec, k_spec, v_spec, b_spec = _attn_a_specs(s)
    return _call(
        body, name="attn_a_fwd", grid=(4, s // TQA),
        in_specs=[q_spec, k_spec, v_spec, b_spec],
        out_specs=pl.BlockSpec((TQA, 128), lambda hp, qi: (qi, hp)),
        out_shape=_sds((s, WG), F32),
        compiler_params=_params(2),
    )(proj, proj, proj, bias_tab)


def _sb_terms(lg):
    sp = jnp.maximum(lg, 0.0) + jnp.log(1.0 + jnp.exp(-jnp.abs(lg)))
    return lg - sp, sp


def _attn_b_specs(s):
    q_spec = pl.BlockSpec((1, TQB, 128), lambda hp, qi: (3, qi, hp))
    k_spec = pl.BlockSpec((1, s, 128), lambda hp, qi: (4, 0, hp))
    v_spec = pl.BlockSpec((1, s, 128), lambda hp, qi: (5, 0, hp))
    return q_spec, k_spec, v_spec


def _attn_b_fwd(proj):
    s = proj.shape[1]
    ndiag = TQB // TKB

    def body(q_ref, k_ref, v_ref, o_ref, ls_ref):
        q0 = pl.program_id(1) * TQB
        rows = lax.broadcasted_iota(jnp.int32, (TQB, TKB), 0)
        cols = lax.broadcasted_iota(jnp.int32, (TQB, TKB), 1)
        uj = lax.broadcasted_iota(jnp.int32, (TKB, TKB), 0)
        us = lax.broadcasted_iota(jnp.int32, (TKB, TKB), 1)
        suffix = jnp.where(uj >= us, 1.0, 0.0).astype(BF16)
        heads = [slice(HD * h, HD * (h + 1)) for h in range(2)]
        qs = [(q_ref[0, :, sl].astype(F32) * 0.125).astype(BF16) for sl in heads]

        def tile(h, kstart, diag, carry, acc):
            k = k_ref[0, pl.ds(kstart, TKB), heads[h]]
            v = v_ref[0, pl.ds(kstart, TKB), heads[h]]
            lb, sp = _sb_terms(_dot_nt(qs[h], k))
            if diag is not None:
                strict = rows > cols + diag * TKB
                sp = jnp.where(strict, sp, 0.0)
            csum = _dot2(sp, suffix) + carry
            w = jnp.exp(lb - csum + sp)
            if diag is not None:
                w = jnp.where(strict, w, 0.0)
            return csum[:, 0:1], acc + _dot(w.astype(BF16), v)

        state = [jnp.zeros((TQB, 1), F32), jnp.zeros((TQB, HD), F32)] * 2
        for d in range(ndiag - 1, -1, -1):
            for h in range(2):
                state[2 * h:2 * h + 2] = tile(h, pl.multiple_of(q0 + d * TKB, TKB), d, *state[2 * h:2 * h + 2])
        nsteps = q0 // (KSTEP * TKB)

        def step(i, st):
            st = list(st)
            base = (nsteps - 1 - i) * (KSTEP * TKB)
            for sub in range(KSTEP - 1, -1, -1):
                for h in range(2):
                    st[2 * h:2 * h + 2] = tile(h, pl.multiple_of(base + sub * TKB, TKB), None, *st[2 * h:2 * h + 2])
            return tuple(st)

        state = lax.fori_loop(0, nsteps, step, tuple(state))
        for h in range(2):
            o_ref[:, heads[h]] = state[2 * h + 1]
            ls_ref[0, :, heads[h]] = jnp.broadcast_to(state[2 * h], (TQB, HD))

    q_spec, k_spec, v_spec = _attn_b_specs(s)
    return _call(
        body, name="attn_b_fwd", grid=(4, s // TQB),
        in_specs=[q_spec, k_spec, v_spec],
        out_specs=[pl.BlockSpec((TQB, 128), lambda hp, qi: (qi, hp)),
                   pl.BlockSpec((1, TQB, 128), lambda hp, qi: (hp, qi, 0))],
        out_shape=[_sds((s, WG), F32), _sds((4, s, 128), F32)],
        compiler_params=_params(2),
    )(proj, proj, proj)


def _mix_out(oa, ob, g, w_out, x, gate, tm):
    s = x.shape[0]

    def body(oa_ref, ob_ref, g_ref, w_ref, x_ref, gate_ref, nab_ref, mixed_ref, x2_ref):
        a, b = oa_ref[...], ob_ref[...]
        nab_ref[:, :WG] = (a * _rms(a) * g_ref[:, :WG]).astype(BF16)
        nab_ref[:, WG:] = (b * _rms(b) * g_ref[:, WG:]).astype(BF16)
        mixed = _dot(nab_ref[...], w_ref[...])
        mixed_ref[...] = mixed
        x2_ref[...] = x_ref[...] + gate_ref[...] * mixed

    row = pl.BlockSpec((tm, D), lambda i: (i, 0))
    half = pl.BlockSpec((tm, WG), lambda i: (i, 0))
    vec = pl.BlockSpec((1, D), lambda i: (0, 0))
    return _call(
        body, name="mix_out", grid=(s // tm,),
        in_specs=[half, half, vec, pl.BlockSpec((D, D), lambda i: (0, 0)), row, vec],
        out_specs=[row, row, row],
        out_shape=[_sds((s, D), BF16), _sds((s, D), F32), _sds((s, D), F32)],
        compiler_params=_params(1),
    )(oa, ob, g, w_out, x, gate)


HALO = 16


def _conv_taps(u, halo, first, tm):
    rows = lax.broadcasted_iota(jnp.int32, (tm, 1), 0)
    keep = jnp.where(first, 0.0, 1.0)
    h1 = halo[HALO - 1:HALO, :] * keep
    h2 = halo[HALO - 2:HALO - 1, :] * keep
    um1 = jnp.where(rows == 0, h1, pltpu.roll(u, 1, 0))
    um2 = jnp.where(rows == 0, h2, jnp.where(rows == 1, h1, pltpu.roll(u, 2, 0)))
    return um1, um2


def _conv_specs(tm):
    u_spec = pl.BlockSpec((2, 1, tm, GU), lambda p, i: (0, p, i, 0))
    halo_spec = pl.BlockSpec((2, 1, HALO, GU), lambda p, i: (0, p, jnp.maximum(i * (tm // HALO) - 1, 0), 0))
    cw_spec = pl.BlockSpec((2, 1, 3, GU), lambda p, i: (0, p, 0, 0))
    cb_spec = pl.BlockSpec((2, 1, 1, GU), lambda p, i: (0, p, 0, 0))
    return u_spec, halo_spec, cw_spec, cb_spec


def _conv_act(u, conv_w, conv_b, tm):
    s = u.shape[2]

    def body(u_ref, halo_ref, cw_ref, cb_ref, a_ref):
        first = pl.program_id(1) == 0
        ys = []
        for side in range(2):
            uv = u_ref[side, 0].astype(F32)
            um1, um2 = _conv_taps(uv, halo_ref[side, 0].astype(F32), first, tm)
            cw = cw_ref[side, 0]
            ys.append(cw[2:3] * uv + cw[1:2] * um1 + cw[0:1] * um2 + cb_ref[side, 0])
        a_ref[0] = (ys[0] * _sigmoid(ys[0]) * ys[1]).astype(BF16)

    u_spec, halo_spec, cw_spec, cb_spec = _conv_specs(tm)
    return _call(
        body, name="conv_act", grid=(4, s // tm),
        in_specs=[u_spec, halo_spec, cw_spec, cb_spec],
        out_specs=pl.BlockSpec((1, tm, GU), lambda p, i: (p, i, 0)),
        out_shape=_sds((4, s, GU), BF16),
        compiler_params=_params(2),
    )(u, u, conv_w, conv_b)


def _down(a, w_down, x2, gate, tm):
    s = x2.shape[0]

    def body(a_ref, w_ref, x_ref, gate_ref, ffn_ref, x3_ref):
        p = pl.program_id(1)
        part = _dot(a_ref[0], w_ref[0])

        @pl.when(p == 0)
        def _():
            ffn_ref[...] = part

        @pl.when(p > 0)
        def _():
            ffn_ref[...] += part

        @pl.when(p == 3)
        def _():
            x3_ref[...] = x_ref[...] + gate_ref[...] * ffn_ref[...]

    row = pl.BlockSpec((tm, D), lambda i, p: (i, 0))
    return _call(
        body, name="down", grid=(s // tm, 4),
        in_specs=[pl.BlockSpec((1, tm, GU), lambda i, p: (p, i, 0)), pl.BlockSpec((1, GU, D), lambda i, p: (p, 0, 0)),
                  row, pl.BlockSpec((1, D), lambda i, p: (0, 0))],
        out_specs=[row, row],
        out_shape=[_sds((s, D), F32), _sds((s, D), F32)],
        compiler_params=_params(2),
    )(a, w_down, x2, gate)


def _final_loss(x, g, target, tm):
    s = x.shape[0]

    def body(x_ref, g_ref, t_ref, loss_ref, dx_ref, dg_ref):
        @pl.when(pl.program_id(0) == 0)
        def _():
            loss_ref[...] = jnp.zeros_like(loss_ref)
            dg_ref[...] = jnp.zeros_like(dg_ref)
        xv = x_ref[...]
        r = _rms(xv)
        nrm = xv * r
        err = nrm * g_ref[...] - t_ref[...]
        loss_ref[...] += (0.5 / D) * jnp.sum(jnp.sum(err * err, axis=-1, keepdims=True), axis=0, keepdims=True)
        dy = err * (1.0 / D)
        dg_ref[...] += _colsum(dy * nrm)
        dx_ref[...] = _rms_bwd(dy * g_ref[...], nrm, r)

    row = pl.BlockSpec((tm, D), lambda i: (i, 0))
    vec = pl.BlockSpec((1, D), lambda i: (0, 0))
    return _call(
        body, name="final_loss", grid=(s // tm,),
        in_specs=[row, vec, row],
        out_specs=[pl.BlockSpec((1, 1), lambda i: (0, 0)), row, vec],
        out_shape=[_sds((1, 1), F32), _sds((s, D), F32), _sds((1, D), F32)],
        compiler_params=_params(1),
    )(x, g, target)


def _down_bwd(dx3, gate, ffn, w_down, tm):
    s = dx3.shape[0]

    def body(dx_ref, gate_ref, ffn_ref, w_ref, dgate_ref, dff_ref, da_ref):
        i, p = pl.program_id(0), pl.program_id(1)

        @pl.when((i == 0) & (p == 0))
        def _():
            dgate_ref[...] = jnp.zeros_like(dgate_ref)

        @pl.when(p == 0)
        def _():
            dxv = dx_ref[...]
            dgate_ref[...] += _colsum(dxv * ffn_ref[...])
            dff_ref[...] = (dxv * gate_ref[...]).astype(BF16)

        da_ref[0] = _dot_nt(dff_ref[...], w_ref[0]).astype(BF16)

    row = pl.BlockSpec((tm, D), lambda i, p: (i, 0))
    vec = pl.BlockSpec((1, D), lambda i, p: (0, 0))
    return _call(
        body, name="down_bwd", grid=(s // tm, 4),
        in_specs=[row, vec, row, pl.BlockSpec((1, GU, D), lambda i, p: (p, 0, 0))],
        out_specs=[vec, row, pl.BlockSpec((1, tm, GU), lambda i, p: (p, i, 0))],
        out_shape=[_sds((1, D), F32), _sds((s, D), BF16), _sds((4, s, GU), BF16)],
        compiler_params=_params(2),
    )(dx3, gate, ffn, w_down)


def _conv_act_bwd(u, conv_w, conv_b, da, tm, after):
    s = u.shape[2]

    def body(u_ref, halo_ref, cw_ref, cb_ref, da_ref, dy_ref, dcw_ref, dcb_ref):
        first = pl.program_id(1) == 0

        @pl.when(first)
        def _():
            dcw_ref[...] = jnp.zeros_like(dcw_ref)
            dcb_ref[...] = jnp.zeros_like(dcb_ref)

        taps, ys = [], []
        for side in range(2):
            uv = u_ref[side, 0].astype(F32)
            um1, um2 = _conv_taps(uv, halo_ref[side, 0].astype(F32), first, tm)
            cw = cw_ref[side, 0]
            taps.append((um2, um1, uv))
            ys.append(cw[2:3] * uv + cw[1:2] * um1 + cw[0:1] * um2 + cb_ref[side, 0])
        dav = da_ref[0].astype(F32)
        sg = _sigmoid(ys[0])
        dys = (dav * ys[1] * (sg * (1.0 + ys[0] * (1.0 - sg))), dav * (ys[0] * sg))
        for side in range(2):
            dy_ref[side, 0] = dys[side].astype(BF16)
            dcb_ref[side, 0] += _colsum(dys[side])
            for j in range(3):
                dcw_ref[side, 0, j:j + 1, :] += _colsum(dys[side] * taps[side][j])

    u_spec, halo_spec, cw_spec, cb_spec = _conv_specs(tm)
    return _call_after(
        body, after, name="conv_act_bwd", grid=(4, s // tm),
        in_specs=[u_spec, halo_spec, cw_spec, cb_spec, pl.BlockSpec((1, tm, GU), lambda p, i: (p, i, 0))],
        out_specs=[u_spec, cw_spec, cb_spec],
        out_shape=[_sds((2, 4, s, GU), BF16), _sds((2, 4, 3, GU), F32), _sds((2, 4, 1, GU), F32)],
        compiler_params=_params(2),
    )(u, u, conv_w, conv_b, da)


def _conv_transpose(dy, conv_w, tm):
    s = dy.shape[1]
    nt = s // tm

    def body(dy_ref, halo_ref, cw_ref, du_ref):
        keep = jnp.where(pl.program_id(1) == nt - 1, 0.0, 1.0)
        dv = dy_ref[0].astype(F32)
        rows = lax.broadcasted_iota(jnp.int32, (tm, 1), 0)
        h0 = halo_ref[0, 0:1, :].astype(F32) * keep
        h1 = halo_ref[0, 1:2, :].astype(F32) * keep
        dp1 = jnp.where(rows == tm - 1, h0, pltpu.roll(dv, tm - 1, 0))
        dp2 = jnp.where(rows == tm - 1, h1, jnp.where(rows == tm - 2, h0, pltpu.roll(dv, tm - 2, 0)))
        cw = cw_ref[0]
        du_ref[0] = (cw[2:3] * dv + cw[1:2] * dp1 + cw[0:1] * dp2).astype(BF16)

    blk = pl.BlockSpec((1, tm, GU), lambda g, i: (g, i, 0))
    return _call(
        body, name="conv_transpose", grid=(NDEV, nt),
        in_specs=[blk, pl.BlockSpec((1, HALO, GU),
                                    lambda g, i: (g, jnp.minimum((i + 1) * (tm // HALO), s // HALO - 1), 0)),
                  pl.BlockSpec((1, 3, GU), lambda g, i: (g, 0, 0))],
        out_specs=blk,
        out_shape=_sds((NDEV, s, GU), BF16),
        compiler_params=_params(2),
    )(dy, dy, conv_w)


def _wgrad(a3, b3, *, tk, name):
    ga, s, ka = a3.shape
    gb, _, nb = b3.shape
    groups = max(ga, gb)
    nk = s // tk

    def body(a_ref, b_ref, o_ref, acc):
        k = pl.program_id(1)

        @pl.when(k == 0)
        def _():
            acc[...] = jnp.zeros_like(acc)

        acc[...] += _dot_tn(a_ref[0], b_ref[0])

        @pl.when(k == nk - 1)
        def _():
            o_ref[0] = acc[...].astype(BF16)

    a_spec = pl.BlockSpec((1, tk, ka), (lambda g, k: (g, k, 0)) if ga > 1 else (lambda g, k: (0, k, 0)))
    b_spec = pl.BlockSpec((1, tk, nb), (lambda g, k: (g, k, 0)) if gb > 1 else (lambda g, k: (0, k, 0)))
    return _call(
        body, name=name, grid=(groups, nk),
        in_specs=[a_spec, b_spec], out_specs=pl.BlockSpec((1, ka, nb), lambda g, k: (g, 0, 0)),
        out_shape=_sds((groups, ka, nb), BF16),
        scratch_shapes=[pltpu.VMEM((ka, nb), F32)],
        compiler_params=_params(2),
    )(a3, b3)


def _wgrad_in(h1, dparts, tk, after):
    s = h1.shape[0]
    nk = s // tk

    def body(a_ref, *refs):
        d_refs, o_ref, acc = refs[:NT], refs[NT], refs[NT + 1]
        k = pl.program_id(0)

        @pl.when(k == 0)
        def _():
            acc[...] = jnp.zeros_like(acc)

        for j in range(NT):
            acc[:, WG * j:WG * (j + 1)] += _dot_tn(a_ref[...], d_refs[j][...])

        @pl.when(k == nk - 1)
        def _():
            o_ref[...] = acc[...].astype(BF16)

    kw = dict(
        name="wgrad_in", grid=(nk,),
        in_specs=[pl.BlockSpec((tk, D), lambda k: (k, 0))] + [pl.BlockSpec((tk, WG), lambda k: (k, 0))] * NT,
        out_specs=pl.BlockSpec((D, NT * WG), lambda k: (0, 0)),
        out_shape=_sds((D, NT * WG), BF16),
        scratch_shapes=[pltpu.VMEM((D, NT * WG), F32)],
        compiler_params=_params(1),
    )
    call = _call(body, **kw) if after is None else _call_after(body, after, **kw)
    return call(h1, *dparts)


def _dgrad_in(dparts, w, x_in, scale, dx_up, tm, after):
    s = x_in.shape[0]

    def body(*refs):
        d_refs = refs[:NT]
        w_ref, x_ref, sc_ref, up_ref, dx_ref, dsc_ref, dsh_ref = refs[NT:]

        @pl.when(pl.program_id(0) == 0)
        def _():
            dsc_ref[...] = jnp.zeros_like(dsc_ref)
            dsh_ref[...] = jnp.zeros_like(dsh_ref)

        dh = _dot_nt(d_refs[0][...], w_ref[:, 0:WG])
        for j in range(1, NT):
            dh = dh + _dot_nt(d_refs[j][...], w_ref[:, WG * j:WG * (j + 1)])
        xv = x_ref[...]
        r = _rms(xv)
        nrm = xv * r
        dsh_ref[...] += _colsum(dh)
        dsc_ref[...] += _colsum(dh * nrm)
        dx_ref[...] = up_ref[...] + _rms_bwd(dh * (1.0 + sc_ref[...]), nrm, r)

    row = pl.BlockSpec((tm, D), lambda i: (i, 0))
    vec = pl.BlockSpec((1, D), lambda i: (0, 0))
    return _call_after(
        body, after, name="dgrad_in", grid=(s // tm,),
        in_specs=[pl.BlockSpec((tm, WG), lambda i: (i, 0))] * NT + [pl.BlockSpec((D, NT * WG), lambda i: (0, 0)), row, vec, row],
        out_specs=[row, vec, vec],
        out_shape=[_sds((s, D), F32), _sds((1, D), F32), _sds((1, D), F32)],
        compiler_params=_params(1),
    )(*dparts, w, x_in, scale, dx_up)


def _dgrad_norm_bwd(d3, w, x_in, scale, dx_up, *, tm, name, after):
    groups, s, n = d3.shape

    def body(d_ref, w_ref, x_ref, sc_ref, up_ref, dx_ref, dsc_ref, dsh_ref, acc):
        i, g = pl.program_id(0), pl.program_id(1)

        @pl.when((i == 0) & (g == 0))
        def _():
            dsc_ref[...] = jnp.zeros_like(dsc_ref)
            dsh_ref[...] = jnp.zeros_like(dsh_ref)

        part = _dot(d_ref[0], w_ref[0])

        @pl.when(g == 0)
        def _():
            acc[...] = part

        @pl.when(g > 0)
        def _():
            acc[...] += part

        @pl.when(g == groups - 1)
        def _():
            dh = acc[...]
            xv = x_ref[...]
            r = _rms(xv)
            nrm = xv * r
            dsh_ref[...] += _colsum(dh)
            dsc_ref[...] += _colsum(dh * nrm)
            dx_ref[...] = up_ref[...] + _rms_bwd(dh * (1.0 + sc_ref[...]), nrm, r)

    row = pl.BlockSpec((tm, D), lambda i, g: (i, 0))
    vec = pl.BlockSpec((1, D), lambda i, g: (0, 0))
    return _call_after(
        body, after, name=name, grid=(s // tm, groups),
        in_specs=[pl.BlockSpec((1, tm, n), lambda i, g: (g, i, 0)), pl.BlockSpec((1, n, D), lambda i, g: (g, 0, 0)),
                  row, vec, row],
        out_specs=[row, vec, vec],
        out_shape=[_sds((s, D), F32), _sds((1, D), F32), _sds((1, D), F32)],
        scratch_shapes=[pltpu.VMEM((tm, D), F32)],
        compiler_params=_params(2),
    )(d3, w, x_in, scale, dx_up)


def _mix_out_bwd(dx2, mixed, gate, w_out, oa, ob, g, tm):
    s = dx2.shape[0]

    def body(dx_ref, mixed_ref, gate_ref, w_ref, oa_ref, ob_ref, g_ref, dgate_ref, dm_ref, doa_ref, dob_ref, dg_ref):
        @pl.when(pl.program_id(0) == 0)
        def _():
            dgate_ref[...] = jnp.zeros_like(dgate_ref)
            dg_ref[...] = jnp.zeros_like(dg_ref)
        dxv = dx_ref[...]
        dgate_ref[...] += _colsum(dxv * mixed_ref[...])
        dm_ref[...] = (dxv * gate_ref[...]).astype(BF16)
        dnab = _dot_nt(dm_ref[...], w_ref[...])
        for o_ref, do_ref, sl in ((oa_ref, doa_ref, slice(0, WG)), (ob_ref, dob_ref, slice(WG, D))):
            ov = o_ref[...]
            r = _rms(ov)
            nrm = ov * r
            dn = dnab[:, sl]
            dg_ref[:, sl] += _colsum(dn * nrm)
            do_ref[...] = _rms_bwd(dn * g_ref[:, sl], nrm, r)

    row = pl.BlockSpec((tm, D), lambda i: (i, 0))
    half = pl.BlockSpec((tm, WG), lambda i: (i, 0))
    vec = pl.BlockSpec((1, D), lambda i: (0, 0))
    return _call(
        body, name="mix_out_bwd", grid=(s // tm,),
        in_specs=[row, row, vec, pl.BlockSpec((D, D), lambda i: (0, 0)), half, half, vec],
        out_specs=[vec, row, half, half, vec],
        out_shape=[_sds((1, D), F32), _sds((s, D), BF16), _sds((s, WG), F32), _sds((s, WG), F32), _sds((1, D), F32)],
        compiler_params=_params(1),
    )(dx2, mixed, gate, w_out, oa, ob, g)


def _attn_a_bwd(proj, bias_tab, d_oa, after):
    s = proj.shape[1]
    nq = s // TQA

    def body(q_ref, k_ref, v_ref, b_ref, do_ref, dq_ref, dk_ref, dv_ref, db_ref, dk_acc, dv_acc):
        qi = pl.program_id(1)

        @pl.when(qi == 0)
        def _():
            dk_acc[...] = jnp.zeros_like(dk_acc)
            dv_acc[...] = jnp.zeros_like(dv_acc)
            db_ref[...] = jnp.zeros_like(db_ref)

        def tile(nk, off, kstart):
            for h in range(2):
                sl = slice(HD * h, HD * (h + 1))
                q = q_ref[0, :, sl]
                k = k_ref[0, pl.ds(kstart, nk), sl]
                v = v_ref[0, pl.ds(kstart, nk), sl]
                do = do_ref[:, sl].astype(BF16)
                sc = _dot_nt(q, k) * 0.125 + b_ref[h, :, off:off + nk]
                p = jnp.exp(sc - jnp.max(sc, axis=-1, keepdims=True))
                p = p / jnp.sum(p, axis=-1, keepdims=True)
                dp = _dot_nt(do, v)
                ds = p * (dp - jnp.sum(dp * p, axis=-1, keepdims=True))
                db_ref[h, :, off:off + nk] += ds
                dsb = (ds * 0.125).astype(BF16)
                dq_ref[:, sl] = _dot(dsb, k).astype(BF16)
                dk_acc[pl.ds(kstart, nk), sl] += _dot_tn(dsb, q)
                dv_acc[pl.ds(kstart, nk), sl] += _dot_tn(p.astype(BF16), do)

        _attn_a_cases(qi, tile)

        @pl.when(qi == nq - 1)
        def _():
            dk_ref[...] = dk_acc[...].astype(BF16)
            dv_ref[...] = dv_acc[...].astype(BF16)

    q_spec, k_spec, v_spec, b_spec = _attn_a_specs(s)
    blk = pl.BlockSpec((TQA, 128), lambda hp, qi: (qi, hp))
    col = pl.BlockSpec((s, 128), lambda hp, qi: (0, hp))
    return _call_after(
        body, after, name="attn_a_bwd", grid=(4, nq),
        in_specs=[q_spec, k_spec, v_spec, b_spec, blk],
        out_specs=[blk, col, col, b_spec],
        out_shape=[_sds((s, WG), BF16), _sds((s, WG), BF16), _sds((s, WG), BF16), _sds((NDEV, TQA, KWA), F32)],
        scratch_shapes=[pltpu.VMEM((s, 128), F32), pltpu.VMEM((s, 128), F32)],
        compiler_params=_params(2),
    )(proj, proj, proj, bias_tab, d_oa)


def _attn_b_bwd(proj, lsum, d_ob):
    s = proj.shape[1]
    nq = s // TQB
    nkb = s // TKB
    ndiag = TQB // TKB

    def body(q_ref, k_ref, v_ref, ls_ref, do_ref, dq_ref, dk_ref, dv_ref, dkt_acc, dvt_acc):
        qi = pl.program_id(1)
        q0 = qi * TQB

        @pl.when(qi == 0)
        def _():
            dkt_acc[...] = jnp.zeros_like(dkt_acc)
            dvt_acc[...] = jnp.zeros_like(dvt_acc)

        rows = lax.broadcasted_iota(jnp.int32, (TQB, TKB), 0)
        cols = lax.broadcasted_iota(jnp.int32, (TQB, TKB), 1)
        uj = jnp.bitwise_and(lax.broadcasted_iota(jnp.int32, (2 * TKB, TKB), 0), TKB - 1)
        us = lax.broadcasted_iota(jnp.int32, (2 * TKB, TKB), 1)
        prefix = jnp.where(uj <= us, 1.0, 0.0).astype(BF16)
        heads = [slice(HD * h, HD * (h + 1)) for h in range(2)]
        qs = [(q_ref[0, :, sl].astype(F32) * 0.125).astype(BF16) for sl in heads]
        dos = [do_ref[:, sl].astype(BF16) for sl in heads]
        ei = lax.broadcasted_iota(jnp.int32, (TQB, TQB), 0)
        ej = lax.broadcasted_iota(jnp.int32, (TQB, TQB), 1)
        eye = jnp.where(ei == ej, 1.0, 0.0).astype(BF16)
        qts = [_dot_tn(x, eye).astype(BF16) for x in qs]
        dots = [_dot_tn(x, eye).astype(BF16) for x in dos]
        stots = [ls_ref[0, :, HD * h:HD * h + 1] for h in range(2)]
        qds = [jnp.concatenate([qs[h], dos[h]], axis=1) for h in range(2)]
        zkv = jnp.zeros((TKB, HD), BF16)

        def tile(h, kb, diag, cl, cg, dq):
            sl = heads[h]
            kstart = pl.multiple_of(kb * TKB, TKB)
            k = k_ref[0, pl.ds(kstart, TKB), sl]
            v = v_ref[0, pl.ds(kstart, TKB), sl]
            kv = jnp.concatenate([jnp.concatenate([k, zkv], axis=1), jnp.concatenate([zkv, v], axis=1)], axis=0)
            both = _dot_nt(qds[h], kv)
            lb, sp = _sb_terms(both[:, :TKB])
            if diag is not None:
                strict = rows > cols + diag * TKB
                sp = jnp.where(strict, sp, 0.0)
            pre = _dot2_deep(sp, prefix) + cl
            a = jnp.exp(lb - stots[h] + pre)
            if diag is not None:
                a = jnp.where(strict, a, 0.0)
            gz = both[:, TKB:] * a
            pg = _dot2_deep(gz, prefix) + cg
            dl = gz - pg * jnp.exp(lb)
            if diag is not None:
                dl = jnp.where(strict, dl, 0.0)
            dlb = dl.astype(BF16)
            dkt_acc[h * nkb + kb] += _dot(qts[h], dlb)
            dvt_acc[h * nkb + kb] += _dot(dots[h], a.astype(BF16))
            return pre[:, TKB - 1:TKB], pg[:, TKB - 1:TKB], dq + _dot(dlb, k)

        state = [jnp.zeros((TQB, 1), F32), jnp.zeros((TQB, 1), F32), jnp.zeros((TQB, HD), F32)] * 2

        def step(i, st):
            st = list(st)
            for sub in range(KSTEP):
                for h in range(2):
                    st[3 * h:3 * h + 3] = tile(h, i * KSTEP + sub, None, *st[3 * h:3 * h + 3])
            return tuple(st)

        state = list(lax.fori_loop(0, q0 // (KSTEP * TKB), step, tuple(state)))
        for d in range(ndiag):
            for h in range(2):
                state[3 * h:3 * h + 3] = tile(h, q0 // TKB + d, d, *state[3 * h:3 * h + 3])
        for h in range(2):
            dq_ref[:, heads[h]] = (state[3 * h + 2] * 0.125).astype(BF16)

        @pl.when(qi == nq - 1)
        def _():
            eye64 = eye[:HD, :HD]
            for h in range(2):
                for kb in range(nkb):
                    rows_kb = slice(kb * TKB, (kb + 1) * TKB)
                    dk_ref[rows_kb, heads[h]] = _dot2_tn(dkt_acc[h * nkb + kb], eye64).astype(BF16)
                    dv_ref[rows_kb, heads[h]] = _dot2_tn(dvt_acc[h * nkb + kb], eye64).astype(BF16)

    q_spec, k_spec, v_spec = _attn_b_specs(s)
    blk = pl.BlockSpec((TQB, 128), lambda hp, qi: (qi, hp))
    col = pl.BlockSpec((s, 128), lambda hp, qi: (0, hp))
    return _call(
        body, name="attn_b_bwd", grid=(4, nq),
        in_specs=[q_spec, k_spec, v_spec, pl.BlockSpec((1, TQB, 128), lambda hp, qi: (hp, qi, 0)), blk],
        out_specs=[blk, col, col],
        out_shape=[_sds((s, WG), BF16)] * 3,
        scratch_shapes=[pltpu.VMEM((2 * nkb, HD, TKB), F32), pltpu.VMEM((2 * nkb, HD, TKB), F32)],
        compiler_params=_params(2),
    )(proj, proj, proj, lsum, d_ob)


def _bias_fold(dtab):
    def body(t_ref, d_ref, far_ref):
        acc = jnp.zeros((NDEV, TABW), F32)
        zpad = jnp.zeros((NDEV, TAB0), F32)
        for r in range(TQA):
            row = jnp.concatenate([zpad, t_ref[:, r, :]], axis=1)
            acc = acc + (pltpu.roll(row, TABW - r, 1) if r else row)
        d_ref[...] = acc
        lane = lax.broadcasted_iota(jnp.int32, (NDEV, TABW), 1)
        far = jnp.sum(jnp.where(lane < N_FAR, acc, 0.0), axis=1, keepdims=True)
        far_ref[...] = jnp.broadcast_to(far, (NDEV, 128))

    d_fpad, d_far = _call(
        body, name="bias_fold", grid=(1,),
        in_specs=[pl.BlockSpec((NDEV, TQA, KWA), lambda i: (0, 0, 0))],
        out_specs=[pl.BlockSpec((NDEV, TABW), lambda i: (0, 0)), pl.BlockSpec((NDEV, 128), lambda i: (0, 0))],
        out_shape=[_sds((NDEV, TABW), F32), _sds((NDEV, 128), F32)],
        compiler_params=_params(1),
    )(dtab)
    d_near = d_fpad[:, N_FAR:N_FAR + N_NEAR][:, ::-1]
    return jnp.concatenate([jnp.zeros((NDEV, REL_CLIP - CHUNK + 1), F32), d_near, d_far[:, :1]], axis=1)


def _adamw(w, g, m, v):
    m = B1 * m + (1.0 - B1) * g
    v = B2 * v + (1.0 - B2) * (g * g)
    m_hat = m / (1.0 - B1 ** STEP)
    v_hat = v / (1.0 - B2 ** STEP)
    delta = -LR * (m_hat / (jnp.sqrt(v_hat) + AEPS) + WD * w)
    return delta, m, v


def _adamw_big(recv0, recv1, w, m, v):
    _, rows, cols = recv0.shape
    tr = max(t for t in range(16, 513, 16) if rows % t == 0)
    nt = rows // tr

    def body(r0_ref, r1_ref, w_ref, m_ref, v_ref, g_ref, d_ref, nm_ref, nv_ref):
        def update(r_ref):
            g = r_ref[0].astype(F32)
            for p in range(1, NDEV):
                g = g + r_ref[p].astype(F32)
            delta, nm, nv = _adamw(w_ref[0], g, m_ref[0], v_ref[0])
            g_ref[0], d_ref[0], nm_ref[0], nv_ref[0] = g, delta, nm, nv

        @pl.when(pl.program_id(0) == 0)
        def _():
            update(r0_ref)

        @pl.when(pl.program_id(0) == 1)
        def _():
            update(r1_ref)

    blk = pl.BlockSpec((1, tr, cols), lambda l, i: (l, i, 0))
    r0_spec = pl.BlockSpec((NDEV, tr, cols), lambda l, i: (0, jnp.where(l == 0, i, nt - 1), 0))
    r1_spec = pl.BlockSpec((NDEV, tr, cols), lambda l, i: (0, jnp.where(l == 1, i, 0), 0))
    return _call(
        body, name="adamw_big", grid=(2, nt),
        in_specs=[r0_spec, r1_spec, blk, blk, blk],
        out_specs=[blk] * 4,
        out_shape=[_sds((2, rows, cols), F32)] * 4,
        compiler_params=_params(2),
    )(recv0, recv1, w, m, v)


def _adamw_w_ada(cact_t, dmod, w, m, v):
    tr = 256

    def body(c_ref, dm_ref, w_ref, m_ref, v_ref, g_ref, d_ref, nm_ref, nv_ref):
        g = c_ref[:, 0:1] * dm_ref[0, 0:1, :]
        for b in range(1, NDEV):
            g = g + c_ref[:, b:b + 1] * dm_ref[0, b:b + 1, :]
        delta, nm, nv = _adamw(w_ref[0], g, m_ref[0], v_ref[0])
        g_ref[0], d_ref[0], nm_ref[0], nv_ref[0] = g, delta, nm, nv

    blk = pl.BlockSpec((1, tr, 768), lambda l, i: (l, i, 0))
    return _call(
        body, name="adamw_w_ada", grid=(2, D // tr),
        in_specs=[pl.BlockSpec((tr, NDEV), lambda l, i: (i, 0)), pl.BlockSpec((1, NDEV, 768), lambda l, i: (l, 0, 0)),
                  blk, blk, blk],
        out_specs=[blk] * 4,
        out_shape=[_sds((2, D, 768), F32)] * 4,
        compiler_params=_params(2),
    )(cact_t, dmod, w, m, v)


def _adamw_small(gath, w, m, v):
    rows = gath.shape[1]

    def body(r_ref, w_ref, m_ref, v_ref, g_ref, d_ref, nm_ref, nv_ref):
        g = r_ref[0]
        for p in range(1, NDEV):
            g = g + r_ref[p]
        delta, nm, nv = _adamw(w_ref[...], g, m_ref[...], v_ref[...])
        g_ref[...], d_ref[...], nm_ref[...], nv_ref[...] = g, delta, nm, nv

    blk = pl.BlockSpec((rows, D), lambda i: (0, 0))
    return _call(
        body, name="adamw_small", grid=(1,),
        in_specs=[pl.BlockSpec((NDEV, rows, D), lambda i: (0, 0, 0)), blk, blk, blk],
        out_specs=[blk] * 4,
        out_shape=[_sds((rows, D), F32)] * 4,
        compiler_params=_params(1),
    )(gath, w, m, v)


_PACK = (("b_ada", 2 * 6 * D), ("rel_bias", 2 * 8 * 257), ("g_a", 2 * WG), ("g_b", 2 * WG),
         ("conv_b", 2 * 2 * DFF), ("final_g", D), ("conv_w", 2 * NDEV * 3 * GU))


def _pack(parts):
    rows = []
    for name, size in _PACK:
        flat = parts[name].reshape(-1).astype(F32)
        assert flat.shape[0] == size, (name, flat.shape)
        rows.append(jnp.pad(flat, (0, -size % D)))
    out = jnp.concatenate(rows).reshape(-1, D)
    return jnp.pad(out, ((0, -out.shape[0] % 8), (0, 0)))


def _unpack(packed):
    flat = packed.reshape(-1)
    out, pos = {}, 0
    for name, size in _PACK:
        out[name] = flat[pos:pos + size]
        pos += size + (-size % D)
    return out


def kernel(x, c, w_ada, b_ada, w_in, rel_bias, g_a, g_b, w_out, w_up, conv_w, conv_b, w_down, final_g, loss_target, m_w_ada, m_b_ada, m_w_in, m_rel_bias, m_g_a, m_g_b, m_w_out, m_w_up, m_conv_w, m_conv_b, m_w_down, m_final_g, v_w_ada, v_b_ada, v_w_in, v_rel_bias, v_g_a, v_g_b, v_w_out, v_w_up, v_conv_w, v_conv_b, v_w_down, v_final_g):
    s = x.shape[1]
    assert s % TQA == 0 and s >= KWA and s % 512 == 0
    tm = 512
    tmm = min(1024, s)
    me = 4 * lax.axis_index("x") + 2 * lax.axis_index("y") + lax.axis_index("c")
    xs = x.reshape(s, D)
    target = loss_target.reshape(s, D)

    first = jnp.concatenate([c, jnp.pad(conv_w.reshape(2 * 3, GU), ((0, 1), (0, D - GU)))])
    first_all = _small_allgather(first, "gather_c_conv_w")
    c_all = first_all[:, 0, :]
    cw_all = first_all[:, 1:7, :GU].reshape(NDEV, 2, 3, GU)

    b_sl = lax.dynamic_slice(b_ada, (0, me * 768), (2, 768)).reshape(2, 1, 768)
    mod_part, cact = _mod_fwd(c_all, w_ada, b_sl)
    mod_all = _small_allgather(mod_part.reshape(2 * NDEV, 768), "gather_mod")

    up_t = [jnp.transpose(t, (0, 2, 1)) for t in (w_up, m_w_up, v_w_up)]
    shards = {"in": w_in, "out": w_out, "up": up_t[0], "down": w_down}
    order = [(kind, l) for l in range(2) for kind in _KINDS]
    mod_all, *srcs = lax.optimization_barrier((mod_all, *[shards[kind][l].astype(BF16) for kind, l in order]))
    gather_started = dict(zip(order, _exchange_start([kind for kind, _ in order], True, srcs, "weights_gather_start")))

    def gathered(kind, l, after):
        return _exchange_wait([kind], True, [gather_started[kind, l]], after, f"weights_gather_wait_{kind}{l}")[0]

    mod_all = mod_all.reshape(NDEV, 2, NDEV, 768)
    mod_me = lax.dynamic_index_in_dim(mod_all, me, axis=2, keepdims=False)
    mod = jnp.transpose(mod_me, (1, 0, 2)).reshape(2, 6, 1, D)

    saved = []
    xl = xs
    for l in range(2):
        sh_mix, sc_mix, gt_mix, sh_ffn, sc_ffn, gt_ffn = (mod[l, j] for j in range(6))
        cw = cw_all[:, l].reshape(2, 4, 3, GU)
        cb = conv_b[l].reshape(2, 4, 1, GU)
        gvec = jnp.concatenate([g_a[l], g_b[l]]).reshape(1, D)
        tab = _bias_table(rel_bias[l])

        wi = gathered("in", l, xl if l else mod)
        h1, proj = _nm_matmul(xl, sh_mix, sc_mix, wi, two_d=True, n=WG, groups=NT, out_dtype=BF16, tm=tmm,
                              name="norm_proj")
        oa = _attn_a_fwd(proj, tab)
        ob, lsum = _attn_b_fwd(proj)
        wo = gathered("out", l, ob)
        nab, mixed, x2 = _mix_out(oa, ob, gvec, wo, xl, gt_mix, tm)
        wu = gathered("up", l, x2)
        h2, u = _nm_matmul(x2, sh_ffn, sc_ffn, wu, two_d=False, n=GU, groups=NDEV, out_dtype=BF16, tm=tmm,
                           name="norm_up")
        u = u.reshape(2, 4, s, GU)
        a = _conv_act(u, cw, cb, tm)
        wd4 = gathered("down", l, a).reshape(4, GU, D)
        ffn, x3 = _down(a, wd4, x2, gt_ffn, tmm)
        saved.append(dict(x=xl, h1=h1, proj=proj, oa=oa, ob=ob, lsum=lsum, nab=nab, mixed=mixed, x2=x2, h2=h2, u=u,
                          a=a, ffn=ffn, cw=cw, cb=cb, gvec=gvec, tab=tab, wd4=wd4, wi=wi, wo=wo, wu=wu))
        xl = x3

    loss_part, dx, d_final_g = _final_loss(xl, final_g.reshape(1, D), target, tm)
    loss = lax.psum(loss_part[0, 0], ("x", "y", "c"))

    sent = {}
    small = {"b_ada": [None, None], "rel_bias": [None, None], "g_a": [None, None], "g_b": [None, None],
             "conv_b": [None, None], "conv_w": [None, None]}

    def send(kind, l, grad):
        started, token = _exchange_start([kind], False, [grad], f"grads_start_{kind}{l}", with_token=True)
        sent[kind, l] = started[0]
        return token

    for l in (1, 0):
        sv = saved[l]
        sh_mix, sc_mix, gt_mix, sh_ffn, sc_ffn, gt_ffn = (mod[l, j] for j in range(6))
        d_gt_ffn, dff, da = _down_bwd(dx, gt_ffn, sv["ffn"], sv["wd4"], tmm)
        tok = send("down", l, _wgrad(sv["a"], dff.reshape(1, s, D), tk=tmm, name="wgrad_down").reshape(DFF, D))
        dy, d_cw, d_cb = _conv_act_bwd(sv["u"], sv["cw"], sv["cb"], da, tm, tok)
        du = _conv_transpose(dy.reshape(NDEV, s, GU), sv["cw"].reshape(NDEV, 3, GU), tm)
        tok = send("up", l, _wgrad(du, sv["h2"].reshape(1, s, D), tk=tmm, name="wgrad_up"))
        dx2, d_sc_ffn, d_sh_ffn = _dgrad_norm_bwd(du, sv["wu"], sv["x2"], sc_ffn, dx, tm=tmm, name="dgrad_up", after=tok)
        d_gt_mix, dmixed, d_oa, d_ob, d_g = _mix_out_bwd(dx2, sv["mixed"], gt_mix, sv["wo"], sv["oa"], sv["ob"],
                                                         sv["gvec"], tm)
        tok = send("out", l, _wgrad(sv["nab"].reshape(1, s, D), dmixed.reshape(1, s, D), tk=tmm,
                                    name="wgrad_out").reshape(D, D))
        dqa, dka, dva, d_tab = _attn_a_bwd(sv["proj"], sv["tab"], d_oa, tok)
        dqb, dkb, dvb = _attn_b_bwd(sv["proj"], sv["lsum"], d_ob)
        dparts = (dqa, dka, dva, dqb, dkb, dvb)
        if l:
            tok = send("in", l, _wgrad_in(sv["h1"], dparts, tmm, None))
        dx, d_sc_mix, d_sh_mix = _dgrad_in(dparts, sv["wi"], sv["x"], sc_mix, dx2, tm, tok)
        small["b_ada"][l] = jnp.concatenate([d_sh_mix, d_sc_mix, d_gt_mix, d_sh_ffn, d_sc_ffn, d_gt_ffn], axis=1)
        small["rel_bias"][l] = _bias_fold(d_tab)
        small["g_a"][l], small["g_b"][l] = d_g[:, :WG], d_g[:, WG:]
        small["conv_b"][l] = d_cb
        small["conv_w"][l] = d_cw.reshape(NDEV, 3, GU)
    grad_x = dx.reshape(1, s, D)

    contrib = {k: jnp.stack(vs) for k, vs in small.items()}
    contrib["final_g"] = d_final_g
    gath = _small_allgather(_pack(contrib), "gather_small_grads")
    tok = send("in", 0, _wgrad_in(saved[0]["h1"], dparts, tmm, gath))

    def place_conv_w(t):
        return lax.dynamic_update_slice(jnp.zeros((2, NDEV, 3, GU), F32), t.reshape(2, 1, 3, GU), (0, me, 0, 0))

    def packed_params(b, rb, ga, gb, cb_, fg, cw_):
        return _pack({"b_ada": b, "rel_bias": rb, "g_a": ga, "g_b": gb, "conv_b": cb_, "final_g": fg,
                      "conv_w": place_conv_w(cw_)})

    sm = _adamw_small(gath,
                      packed_params(b_ada, rel_bias, g_a, g_b, conv_b, final_g, conv_w),
                      packed_params(m_b_ada, m_rel_bias, m_g_a, m_g_b, m_conv_b, m_final_g, m_conv_w),
                      packed_params(v_b_ada, v_rel_bias, v_g_a, v_g_b, v_conv_b, v_final_g, v_conv_w))
    sm = [_unpack(t) for t in sm]

    dmod_all = gath[:, :12, :].reshape(NDEV, 2, 6 * D)
    dmod_sl = jnp.transpose(lax.dynamic_slice(dmod_all, (0, 0, me * 768), (NDEV, 2, 768)), (1, 0, 2))
    ada = _adamw_w_ada(cact.T, dmod_sl, w_ada, m_w_ada, v_w_ada)

    big = {}
    for kind, (w, m, v) in (("down", (w_down, m_w_down, v_w_down)), ("up", up_t),
                            ("out", (w_out, m_w_out, v_w_out)), ("in", (w_in, m_w_in, v_w_in))):
        recv0, recv1 = _exchange_wait([kind, kind], False, [sent[kind, 0], sent[kind, 1]], tok, f"grads_wait_{kind}")
        big[kind] = _adamw_big(recv0, recv1, w, m, v)
        tok = big[kind][0]
    big["up"] = [jnp.transpose(t, (0, 2, 1)) for t in big["up"]]

    def small_out(j, name):
        t = sm[j][name]
        if name == "b_ada":
            return t.reshape(2, 6 * D)
        if name == "rel_bias":
            return t.reshape(2, 8, 257)
        if name in ("g_a", "g_b"):
            return t.reshape(2, WG)
        if name == "conv_b":
            return t.reshape(2, 2 * DFF)
        if name == "final_g":
            return t.reshape(D)
        t = t.reshape(2, NDEV, 3, GU)
        return lax.dynamic_index_in_dim(t, me, axis=1, keepdims=False)

    def group(j):
        return (ada[j], small_out(j, "b_ada"), big["in"][j], small_out(j, "rel_bias"), small_out(j, "g_a"),
                small_out(j, "g_b"), big["out"][j], big["up"][j], small_out(j, "conv_w"), small_out(j, "conv_b"),
                big["down"][j], small_out(j, "final_g"))

    return (loss, grad_x, *group(0), *group(1), *group(2), *group(3))
```

```python
import jax
import jax.numpy as jnp
from jax import lax
from jax.experimental import pallas as pl
from jax.experimental.pallas import tpu as pltpu

F32, BF16 = jnp.float32, jnp.bfloat16
MESH_ID = pl.DeviceIdType.MESH
NDEV = 8
D = 1024
HD = 64
WG = 512
NT = 6
GU = 704
DFF = 2816
CHUNK, NPREV, REL_CLIP = 64, 8, 128
BAND = (NPREV + 1) * CHUNK
EPS = 1e-6
NEG = -1e30
TQA = 256
KWA = TQA + NPREV * CHUNK
TABW = 1024
TAB0 = TABW - KWA
N_FAR = TAB0 + NPREV * CHUNK - REL_CLIP + 1
N_NEAR = REL_CLIP + CHUNK - 1
TQB, TKB = 512, 128
KSTEP = 4
LR, B1, B2, AEPS, WD, STEP = 0.001, 0.9, 0.999, 1e-08, 0.01, 10
VMEM_MB = 56


def _call(body, **kw):
    return pl.pallas_call(body, **kw)


def _call_after(body, after, **kw):
    n_in = len(kw["in_specs"])
    kw["in_specs"] = list(kw["in_specs"]) + [pl.BlockSpec(memory_space=pl.ANY)]

    def tied(*refs):
        body(*refs[:n_in], *refs[n_in + 1:])

    call = _call(tied, **kw)
    return lambda *args: call(*args, after)


def _params(n_axes):
    return pltpu.CompilerParams(dimension_semantics=("arbitrary",) * n_axes, vmem_limit_bytes=VMEM_MB << 20)


def _dot(a, b):
    return jnp.dot(a, b, preferred_element_type=F32)


def _dot_nt(a, b):
    return lax.dot_general(a, b, (((1,), (1,)), ((), ())), preferred_element_type=F32)


def _dot_tn(a, b):
    return lax.dot_general(a, b, (((0,), (0,)), ((), ())), preferred_element_type=F32)


def _dot2(x, u):
    hi = x.astype(BF16)
    lo = (x - hi.astype(F32)).astype(BF16)
    return _dot(hi, u) + _dot(lo, u)


def _dot2_deep(x, uu):
    hi = x.astype(BF16)
    lo = (x - hi.astype(F32)).astype(BF16)
    return _dot(jnp.concatenate([hi, lo], axis=1), uu)


def _eye(n):
    i = lax.broadcasted_iota(jnp.int32, (n, n), 0)
    j = lax.broadcasted_iota(jnp.int32, (n, n), 1)
    return jnp.where(i == j, 1.0, 0.0).astype(BF16)


def _transpose_bf16(x, eye):
    return _dot_nt(eye, x).astype(BF16)


def _transpose_f32(x, eye):
    hi = x.astype(BF16)
    lo = (x - hi.astype(F32)).astype(BF16)
    return _dot_nt(eye, hi) + _dot_nt(eye, lo)


def _rms(x):
    return lax.rsqrt(jnp.mean(x * x, axis=-1, keepdims=True) + EPS)


def _rms_bwd(dn, n, r):
    return r * (dn - n * jnp.mean(dn * n, axis=-1, keepdims=True))


def _colsum(x):
    return jnp.sum(x, axis=0, keepdims=True)


def _sigmoid(x):
    return 1.0 / (1.0 + jnp.exp(-x))


def _sds(shape, dtype):
    return jax.ShapeDtypeStruct(shape, dtype)


def _place():
    x, y, c = lax.axis_index("x"), lax.axis_index("y"), lax.axis_index("c")
    return x, y, c, 4 * x + 2 * y + c


def _peer(x, y, c, k):
    px = 1 - x if k & 4 else x
    py = 1 - y if k & 2 else y
    pc = 1 - c if k & 1 else c
    return (px, py, pc), 4 * px + 2 * py + pc


def _small_allgather(v, name):
    rows, cols = v.shape

    def body(v_ref, out_ref, send_sems, recv_sems, local_sem):
        x, y, c, me = _place()
        mine = pltpu.make_async_copy(v_ref, out_ref.at[me], local_sem)
        mine.start()
        sends = []
        for k in range(1, NDEV):
            peer, _ = _peer(x, y, c, k)
            cp = pltpu.make_async_remote_copy(v_ref, out_ref.at[me], send_sems.at[k - 1], recv_sems.at[k - 1],
                                              device_id=peer, device_id_type=MESH_ID)
            cp.start()
            sends.append(cp)
        for k in range(1, NDEV):
            peer, pidx = _peer(x, y, c, k)
            pltpu.make_async_remote_copy(v_ref, out_ref.at[pidx], send_sems.at[k - 1], recv_sems.at[k - 1],
                                         device_id=peer, device_id_type=MESH_ID).wait_recv()
        for cp in sends:
            cp.wait_send()
        mine.wait()

    return _call(
        body, name=name,
        out_shape=_sds((NDEV, rows, cols), F32),
        in_specs=[pl.BlockSpec(memory_space=pltpu.VMEM)],
        out_specs=pl.BlockSpec(memory_space=pltpu.VMEM),
        scratch_shapes=[pltpu.SemaphoreType.DMA((NDEV - 1,)), pltpu.SemaphoreType.DMA((NDEV - 1,)),
                        pltpu.SemaphoreType.DMA],
    )(v)


def _shard_view(ref, kind, p):
    if kind == "in":
        return ref.at[:, pl.ds(pl.multiple_of(p * 384, 128), 384)]
    if kind == "out":
        return ref.at[pl.ds(pl.multiple_of(p * 128, 128), 128), :]
    if kind == "up":
        return ref.at[p]
    if kind == "down":
        return ref.at[pl.ds(pl.multiple_of(p * 352, 16), 352), :]
    raise ValueError(kind)


_KINDS = ("in", "out", "up", "down")
_FULL_SHAPES = {"in": (D, 3 * D), "out": (D, D), "up": (NDEV, GU, D), "down": (DFF, D)}
_SHARD_SHAPES = {"in": (D, 384), "out": (128, D), "up": (GU, D), "down": (352, D)}


_HBM = pl.BlockSpec(memory_space=pltpu.HBM)
_SEM = pl.BlockSpec(memory_space=pltpu.SEMAPHORE)
_EFFECT = pltpu.SideEffectType.DATAFLOW_SIDE_EFFECTING
_SEM_SHAPES = (pltpu.SemaphoreType.DMA((NDEV - 1,)), pltpu.SemaphoreType.DMA((NDEV - 1,)), pltpu.SemaphoreType.DMA(()))


def _hbm(a):
    return pltpu.with_memory_space_constraint(a, pltpu.HBM)


def _exchange_copies(kind, gather, src, land, sems):
    send_sems, recv_sems, local_sem = sems
    x, y, c, me = _place()

    def ends(p_dst, p_from):
        if gather:
            return src, _shard_view(land, kind, me), _shard_view(land, kind, p_from)
        return _shard_view(src, kind, p_dst), land.at[me], land.at[p_from]

    s_me, d_me, _ = ends(me, me)
    local = pltpu.make_async_copy(s_me, d_me, local_sem)
    sends, arrivals = [], []
    for k in range(1, NDEV):
        peer, pidx = _peer(x, y, c, k)
        s_k, d_k, from_k = ends(pidx, pidx)
        sends.append(pltpu.make_async_remote_copy(s_k, d_k, send_sems.at[k - 1], recv_sems.at[k - 1],
                                                  device_id=peer, device_id_type=MESH_ID))
        arrivals.append(pltpu.make_async_remote_copy(s_k, from_k, send_sems.at[k - 1], recv_sems.at[k - 1],
                                                     device_id=peer, device_id_type=MESH_ID))
    return local, sends, arrivals


def _exchange_start(kinds, gather, srcs, name, with_token=False):
    n = len(kinds)
    lands = [lax.empty(_FULL_SHAPES[kd] if gather else (NDEV,) + _SHARD_SHAPES[kd], BF16) for kd in kinds]

    def body(*refs):
        ins, sems = refs[:2 * n], refs[2 * n:5 * n]
        for j, kd in enumerate(kinds):
            local, sends, _ = _exchange_copies(kd, gather, ins[j], ins[n + j], sems[3 * j:3 * j + 3])
            local.start()
            for cp in sends:
                cp.start()
        if with_token:
            token = refs[7 * n]
            token[...] = jnp.zeros_like(token)

    out_shape = list(_SEM_SHAPES) * n
    out_shape += [pltpu.HBM(a.shape, a.dtype) for a in srcs] + [pltpu.HBM(a.shape, a.dtype) for a in lands]
    out_specs = [_SEM] * (3 * n) + [_HBM] * (2 * n)
    if with_token:
        out_shape.append(_sds((8, 128), F32))
        out_specs.append(pl.BlockSpec(memory_space=pltpu.VMEM))
    outs = _call(
        body, name=name, out_shape=out_shape,
        in_specs=[_HBM] * (2 * n), out_specs=out_specs,
        input_output_aliases={i: 3 * n + i for i in range(2 * n)},
        compiler_params=pltpu.CompilerParams(has_side_effects=_EFFECT),
    )(*[_hbm(a) for a in srcs], *[_hbm(a) for a in lands])
    per_tensor = [(tuple(outs[3 * j:3 * j + 3]), outs[3 * n + j], outs[4 * n + j]) for j in range(n)]
    return (per_tensor, outs[5 * n]) if with_token else per_tensor


def _exchange_wait(kinds, gather, started, after, name):
    n = len(kinds)

    def body(*refs):
        ins, sems = refs[:2 * n], refs[2 * n:5 * n]
        for j, kd in enumerate(kinds):
            local, sends, arrivals = _exchange_copies(kd, gather, ins[j], ins[n + j], sems[3 * j:3 * j + 3])
            local.wait()
            for cp in arrivals:
                cp.wait_recv()
            for cp in sends:
                cp.wait_send()

    srcs = [st[1] for st in started]
    lands = [st[2] for st in started]
    sems = [sm for st in started for sm in st[0]]
    outs = _call(
        body, name=name,
        out_shape=[pltpu.HBM(a.shape, a.dtype) for a in srcs] + [pltpu.HBM(a.shape, a.dtype) for a in lands],
        in_specs=[_HBM] * (2 * n) + [_SEM] * (3 * n) + [pl.BlockSpec(memory_space=pl.ANY)],
        out_specs=[_HBM] * (2 * n),
        input_output_aliases={i: i for i in range(2 * n)},
        compiler_params=pltpu.CompilerParams(has_side_effects=_EFFECT),
    )(*srcs, *lands, *sems, after)
    return outs[n:]


def _mod_fwd(c_all, w_ada, b_sl):
    def body(c_ref, w_ref, b_ref, mod_ref, cact_ref):
        cv = c_ref[...]
        ca = cv * _sigmoid(cv)
        cact_ref[...] = ca
        mod_ref[0] = _dot(ca.astype(BF16), w_ref[0].astype(BF16)) + b_ref[0]

    return _call(
        body, name="mod_fwd", grid=(2,),
        in_specs=[pl.BlockSpec((NDEV, D), lambda l: (0, 0)), pl.BlockSpec((1, D, 768), lambda l: (l, 0, 0)),
                  pl.BlockSpec((1, 1, 768), lambda l: (l, 0, 0))],
        out_specs=[pl.BlockSpec((1, NDEV, 768), lambda l: (l, 0, 0)), pl.BlockSpec((NDEV, D), lambda l: (0, 0))],
        out_shape=[_sds((2, NDEV, 768), F32), _sds((NDEV, D), F32)],
        compiler_params=_params(1),
    )(c_all, w_ada, b_sl)


def _nm_matmul(x, shift, scale, w, *, two_d, n, groups, out_dtype, tm, name):
    s = x.shape[0]

    def body(x_ref, sh_ref, sc_ref, w_ref, h_ref, o_ref):
        @pl.when(pl.program_id(1) == 0)
        def _():
            xv = x_ref[...]
            h_ref[...] = ((xv * _rms(xv)) * (1.0 + sc_ref[...]) + sh_ref[...]).astype(BF16)
        if two_d:
            o_ref[0] = _dot(h_ref[...], w_ref[...]).astype(out_dtype)
        else:
            o_ref[0] = _dot_nt(h_ref[...], w_ref[0]).astype(out_dtype)

    vec = pl.BlockSpec((1, D), lambda i, g: (0, 0))
    w_spec = pl.BlockSpec((D, n), lambda i, g: (0, g)) if two_d else pl.BlockSpec((1, n, D), lambda i, g: (g, 0, 0))
    return _call(
        body, name=name, grid=(s // tm, groups),
        in_specs=[pl.BlockSpec((tm, D), lambda i, g: (i, 0)), vec, vec, w_spec],
        out_specs=[pl.BlockSpec((tm, D), lambda i, g: (i, 0)), pl.BlockSpec((1, tm, n), lambda i, g: (g, i, 0))],
        out_shape=[_sds((s, D), BF16), _sds((groups, s, n), out_dtype)],
        compiler_params=_params(2),
    )(x, shift, scale, w)


def _bias_table(rel_bias):
    far = jnp.broadcast_to(rel_bias[:, 2 * REL_CLIP:], (NDEV, N_FAR))
    near = rel_bias[:, 2 * REL_CLIP - 1:REL_CLIP - CHUNK:-1]
    fpad = jnp.concatenate([far, near, jnp.zeros((NDEV, TABW - N_FAR - N_NEAR), F32)], axis=1)

    def body(f_ref, o_ref):
        t = pltpu.roll(jnp.broadcast_to(f_ref[0], (TQA, TABW)), 0, 1, stride=1, stride_axis=0)[:, TAB0:]
        rows = lax.broadcasted_iota(jnp.int32, (TQA, KWA), 0)
        cols = lax.broadcasted_iota(jnp.int32, (TQA, KWA), 1)
        first = jnp.bitwise_and(rows, -CHUNK)
        o_ref[0] = jnp.where((cols >= first) & (cols < first + BAND), t, NEG)

    return _call(
        body, name="bias_table", grid=(NDEV,),
        in_specs=[pl.BlockSpec((1, 1, TABW), lambda h: (h, 0, 0))],
        out_specs=pl.BlockSpec((1, TQA, KWA), lambda h: (h, 0, 0)),
        out_shape=_sds((NDEV, TQA, KWA), F32),
        compiler_params=_params(1),
    )(fpad.reshape(NDEV, 1, TABW))


def _attn_a_cases(qi, tile):
    @pl.when(qi == 0)
    def _():
        tile(TQA, 2 * TQA, 0)

    @pl.when(qi == 1)
    def _():
        tile(2 * TQA, TQA, 0)

    @pl.when(qi >= 2)
    def _():
        tile(KWA, 0, pl.multiple_of((qi - 2) * TQA, TQA))


HA = 4
WA = HA * HD


def _attn_a_specs(s):
    q_spec = pl.BlockSpec((1, TQA, WA), lambda hp, qi: (0, qi, hp))
    k_spec = pl.BlockSpec((1, s, WA), lambda hp, qi: (1, 0, hp))
    v_spec = pl.BlockSpec((1, s, WA), lambda hp, qi: (2, 0, hp))
    b_spec = pl.BlockSpec((HA, TQA, KWA), lambda hp, qi: (hp, 0, 0))
    return q_spec, k_spec, v_spec, b_spec


def _attn_a_fwd(proj, bias_tab):
    s = proj.shape[1]

    def body(q_ref, k_ref, v_ref, b_ref, o_ref):
        def tile(nk, off, kstart):
            for h in range(HA):
                sl = slice(HD * h, HD * (h + 1))
                q = q_ref[0, :, sl]
                k = k_ref[0, pl.ds(kstart, nk), sl]
                v = v_ref[0, pl.ds(kstart, nk), sl]
                sc = _dot_nt(q, k) * 0.125 + b_ref[h, :, off:off + nk]
                p = jnp.exp(sc - jnp.max(sc, axis=-1, keepdims=True))
                den = jnp.sum(p, axis=-1, keepdims=True)
                o_ref[:, sl] = _dot(p.astype(BF16), v) / den

        _attn_a_cases(pl.program_id(1), tile)

    q_spec, k_spec, v_spec, b_spec = _attn_a_specs(s)
    return _call(
        body, name="attn_a_fwd", grid=(WG // WA, s // TQA),
        in_specs=[q_spec, k_spec, v_spec, b_spec],
        out_specs=pl.BlockSpec((TQA, WA), lambda hp, qi: (qi, hp)),
        out_shape=_sds((s, WG), F32),
        compiler_params=_params(2),
    )(proj, proj, proj, bias_tab)


def _sb_terms(lg):
    sp = jnp.maximum(lg, 0.0) + jnp.log(1.0 + jnp.exp(-jnp.abs(lg)))
    return lg - sp, sp


def _attn_b_specs(s):
    q_spec = pl.BlockSpec((1, TQB, 128), lambda hp, qi: (3, qi, hp))
    k_spec = pl.BlockSpec((1, s, 128), lambda hp, qi: (4, 0, hp))
    v_spec = pl.BlockSpec((1, s, 128), lambda hp, qi: (5, 0, hp))
    return q_spec, k_spec, v_spec


def _attn_b_fwd(proj):
    s = proj.shape[1]
    ndiag = TQB // TKB

    def body(q_ref, k_ref, v_ref, o_ref, ls_ref):
        q0 = pl.program_id(1) * TQB
        rows = lax.broadcasted_iota(jnp.int32, (TQB, TKB), 0)
        cols = lax.broadcasted_iota(jnp.int32, (TQB, TKB), 1)
        uj = lax.broadcasted_iota(jnp.int32, (TKB, TKB), 0)
        us = lax.broadcasted_iota(jnp.int32, (TKB, TKB), 1)
        suffix = jnp.where(uj >= us, 1.0, 0.0).astype(BF16)
        heads = [slice(HD * h, HD * (h + 1)) for h in range(2)]
        qs = [(q_ref[0, :, sl].astype(F32) * 0.125).astype(BF16) for sl in heads]

        def tile(h, kstart, diag, carry, acc):
            k = k_ref[0, pl.ds(kstart, TKB), heads[h]]
            v = v_ref[0, pl.ds(kstart, TKB), heads[h]]
            lb, sp = _sb_terms(_dot_nt(qs[h], k))
            if diag is not None:
                strict = rows > cols + diag * TKB
                sp = jnp.where(strict, sp, 0.0)
            csum = _dot2(sp, suffix) + carry
            w = jnp.exp(lb - csum + sp)
            if diag is not None:
                w = jnp.where(strict, w, 0.0)
            return csum[:, 0:1], acc + _dot(w.astype(BF16), v)

        state = [jnp.zeros((TQB, 1), F32), jnp.zeros((TQB, HD), F32)] * 2
        for d in range(ndiag - 1, -1, -1):
            for h in range(2):
                state[2 * h:2 * h + 2] = tile(h, pl.multiple_of(q0 + d * TKB, TKB), d, *state[2 * h:2 * h + 2])
        nsteps = q0 // (KSTEP * TKB)

        def step(i, st):
            st = list(st)
            base = (nsteps - 1 - i) * (KSTEP * TKB)
            for sub in range(KSTEP - 1, -1, -1):
                for h in range(2):
                    st[2 * h:2 * h + 2] = tile(h, pl.multiple_of(base + sub * TKB, TKB), None, *st[2 * h:2 * h + 2])
            return tuple(st)

        state = lax.fori_loop(0, nsteps, step, tuple(state))
        for h in range(2):
            o_ref[:, heads[h]] = state[2 * h + 1]
            ls_ref[0, :, heads[h]] = jnp.broadcast_to(state[2 * h], (TQB, HD))

    q_spec, k_spec, v_spec = _attn_b_specs(s)
    return _call(
        body, name="attn_b_fwd", grid=(4, s // TQB),
        in_specs=[q_spec, k_spec, v_spec],
        out_specs=[pl.BlockSpec((TQB, 128), lambda hp, qi: (qi, hp)),
                   pl.BlockSpec((1, TQB, 128), lambda hp, qi: (hp, qi, 0))],
        out_shape=[_sds((s, WG), F32), _sds((4, s, 128), F32)],
        compiler_params=_params(2),
    )(proj, proj, proj)


def _mix_out(oa, ob, g, w_out, x, gate, tm):
    s = x.shape[0]

    def body(oa_ref, ob_ref, g_ref, w_ref, x_ref, gate_ref, nab_ref, mixed_ref, x2_ref):
        a, b = oa_ref[...], ob_ref[...]
        nab_ref[:, :WG] = (a * _rms(a) * g_ref[:, :WG]).astype(BF16)
        nab_ref[:, WG:] = (b * _rms(b) * g_ref[:, WG:]).astype(BF16)
        mixed = _dot(nab_ref[...], w_ref[...])
        mixed_ref[...] = mixed
        x2_ref[...] = x_ref[...] + gate_ref[...] * mixed

    row = pl.BlockSpec((tm, D), lambda i: (i, 0))
    half = pl.BlockSpec((tm, WG), lambda i: (i, 0))
    vec = pl.BlockSpec((1, D), lambda i: (0, 0))
    return _call(
        body, name="mix_out", grid=(s // tm,),
        in_specs=[half, half, vec, pl.BlockSpec((D, D), lambda i: (0, 0)), row, vec],
        out_specs=[row, row, row],
        out_shape=[_sds((s, D), BF16), _sds((s, D), F32), _sds((s, D), F32)],
        compiler_params=_params(1),
    )(oa, ob, g, w_out, x, gate)


HALO = 16


def _conv_taps(u, halo, first, tm):
    keep = jnp.where(first, 0.0, 1.0)
    window = jnp.concatenate([halo[HALO - 8:HALO, :] * keep, u[0:8, :]], axis=0)
    um1 = jnp.concatenate([pltpu.roll(window, 1, 0)[8:16, :], pltpu.roll(u, 1, 0)[8:, :]], axis=0)
    um2 = jnp.concatenate([pltpu.roll(window, 2, 0)[8:16, :], pltpu.roll(u, 2, 0)[8:, :]], axis=0)
    return um1, um2


def _conv_specs(tm):
    u_spec = pl.BlockSpec((2, 1, tm, GU), lambda p, i: (0, p, i, 0))
    halo_spec = pl.BlockSpec((2, 1, HALO, GU), lambda p, i: (0, p, jnp.maximum(i * (tm // HALO) - 1, 0), 0))
    cw_spec = pl.BlockSpec((2, 1, 3, GU), lambda p, i: (0, p, 0, 0))
    cb_spec = pl.BlockSpec((2, 1, 1, GU), lambda p, i: (0, p, 0, 0))
    return u_spec, halo_spec, cw_spec, cb_spec


def _conv_act(u, conv_w, conv_b, tm):
    s = u.shape[2]

    def body(u_ref, halo_ref, cw_ref, cb_ref, a_ref):
        first = pl.program_id(1) == 0
        ys = []
        for side in range(2):
            uv = u_ref[side, 0].astype(F32)
            um1, um2 = _conv_taps(uv, halo_ref[side, 0].astype(F32), first, tm)
            cw = cw_ref[side, 0]
            ys.append(cw[2:3] * uv + cw[1:2] * um1 + cw[0:1] * um2 + cb_ref[side, 0])
        a_ref[0] = (ys[0] * _sigmoid(ys[0]) * ys[1]).astype(BF16)

    u_spec, halo_spec, cw_spec, cb_spec = _conv_specs(tm)
    return _call(
        body, name="conv_act", grid=(4, s // tm),
        in_specs=[u_spec, halo_spec, cw_spec, cb_spec],
        out_specs=pl.BlockSpec((1, tm, GU), lambda p, i: (p, i, 0)),
        out_shape=_sds((4, s, GU), BF16),
        compiler_params=_params(2),
    )(u, u, conv_w, conv_b)


def _down(a, w_down, x2, gate, tm):
    s = x2.shape[0]

    def body(a_ref, w_ref, x_ref, gate_ref, ffn_ref, x3_ref):
        p = pl.program_id(1)
        part = _dot(a_ref[0], w_ref[0])

        @pl.when(p == 0)
        def _():
            ffn_ref[...] = part

        @pl.when(p > 0)
        def _():
            ffn_ref[...] += part

        @pl.when(p == 3)
        def _():
            x3_ref[...] = x_ref[...] + gate_ref[...] * ffn_ref[...]

    row = pl.BlockSpec((tm, D), lambda i, p: (i, 0))
    return _call(
        body, name="down", grid=(s // tm, 4),
        in_specs=[pl.BlockSpec((1, tm, GU), lambda i, p: (p, i, 0)), pl.BlockSpec((1, GU, D), lambda i, p: (p, 0, 0)),
                  row, pl.BlockSpec((1, D), lambda i, p: (0, 0))],
        out_specs=[row, row],
        out_shape=[_sds((s, D), F32), _sds((s, D), F32)],
        compiler_params=_params(2),
    )(a, w_down, x2, gate)


def _final_loss(x, g, target, tm):
    s = x.shape[0]

    def body(x_ref, g_ref, t_ref, loss_ref, dx_ref, dg_ref):
        @pl.when(pl.program_id(0) == 0)
        def _():
            loss_ref[...] = jnp.zeros_like(loss_ref)
            dg_ref[...] = jnp.zeros_like(dg_ref)
        xv = x_ref[...]
        r = _rms(xv)
        nrm = xv * r
        err = nrm * g_ref[...] - t_ref[...]
        loss_ref[...] += (0.5 / D) * jnp.sum(jnp.sum(err * err, axis=-1, keepdims=True), axis=0, keepdims=True)
        dy = err * (1.0 / D)
        dg_ref[...] += _colsum(dy * nrm)
        dx_ref[...] = _rms_bwd(dy * g_ref[...], nrm, r)

    row = pl.BlockSpec((tm, D), lambda i: (i, 0))
    vec = pl.BlockSpec((1, D), lambda i: (0, 0))
    return _call(
        body, name="final_loss", grid=(s // tm,),
        in_specs=[row, vec, row],
        out_specs=[pl.BlockSpec((1, 1), lambda i: (0, 0)), row, vec],
        out_shape=[_sds((1, 1), F32), _sds((s, D), F32), _sds((1, D), F32)],
        compiler_params=_params(1),
    )(x, g, target)


def _down_bwd(dx3, gate, ffn, w_down, tm):
    s = dx3.shape[0]

    def body(dx_ref, gate_ref, ffn_ref, w_ref, dgate_ref, dff_ref, da_ref):
        i, p = pl.program_id(0), pl.program_id(1)

        @pl.when((i == 0) & (p == 0))
        def _():
            dgate_ref[...] = jnp.zeros_like(dgate_ref)

        @pl.when(p == 0)
        def _():
            dxv = dx_ref[...]
            dgate_ref[...] += _colsum(dxv * ffn_ref[...])
            dff_ref[...] = (dxv * gate_ref[...]).astype(BF16)

        da_ref[0] = _dot_nt(dff_ref[...], w_ref[0]).astype(BF16)

    row = pl.BlockSpec((tm, D), lambda i, p: (i, 0))
    vec = pl.BlockSpec((1, D), lambda i, p: (0, 0))
    return _call(
        body, name="down_bwd", grid=(s // tm, 4),
        in_specs=[row, vec, row, pl.BlockSpec((1, GU, D), lambda i, p: (p, 0, 0))],
        out_specs=[vec, row, pl.BlockSpec((1, tm, GU), lambda i, p: (p, i, 0))],
        out_shape=[_sds((1, D), F32), _sds((s, D), BF16), _sds((4, s, GU), BF16)],
        compiler_params=_params(2),
    )(dx3, gate, ffn, w_down)


def _conv_act_bwd(u, conv_w, conv_b, da, tm, after):
    s = u.shape[2]

    def body(u_ref, halo_ref, cw_ref, cb_ref, da_ref, dy_ref, dcw_ref, dcb_ref):
        first = pl.program_id(1) == 0

        @pl.when(first)
        def _():
            dcw_ref[...] = jnp.zeros_like(dcw_ref)
            dcb_ref[...] = jnp.zeros_like(dcb_ref)

        taps, ys = [], []
        for side in range(2):
            uv = u_ref[side, 0].astype(F32)
            um1, um2 = _conv_taps(uv, halo_ref[side, 0].astype(F32), first, tm)
            cw = cw_ref[side, 0]
            taps.append((um2, um1, uv))
            ys.append(cw[2:3] * uv + cw[1:2] * um1 + cw[0:1] * um2 + cb_ref[side, 0])
        dav = da_ref[0].astype(F32)
        sg = _sigmoid(ys[0])
        dys = (dav * ys[1] * (sg * (1.0 + ys[0] * (1.0 - sg))), dav * (ys[0] * sg))
        for side in range(2):
            dy_ref[side, 0] = dys[side].astype(BF16)
            dcb_ref[side, 0] += _colsum(dys[side])
            for j in range(3):
                dcw_ref[side, 0, j:j + 1, :] += _colsum(dys[side] * taps[side][j])

    u_spec, halo_spec, cw_spec, cb_spec = _conv_specs(tm)
    return _call_after(
        body, after, name="conv_act_bwd", grid=(4, s // tm),
        in_specs=[u_spec, halo_spec, cw_spec, cb_spec, pl.BlockSpec((1, tm, GU), lambda p, i: (p, i, 0))],
        out_specs=[u_spec, cw_spec, cb_spec],
        out_shape=[_sds((2, 4, s, GU), BF16), _sds((2, 4, 3, GU), F32), _sds((2, 4, 1, GU), F32)],
        compiler_params=_params(2),
    )(u, u, conv_w, conv_b, da)


def _conv_transpose(dy, conv_w, tm):
    s = dy.shape[1]
    nt = s // tm

    def body(dy_ref, halo_ref, cw_ref, du_ref):
        keep = jnp.where(pl.program_id(1) == nt - 1, 0.0, 1.0)
        dv = dy_ref[0].astype(F32)
        window = jnp.concatenate([dv[tm - 8:, :], halo_ref[0, 0:8, :].astype(F32) * keep], axis=0)
        dp1 = jnp.concatenate([pltpu.roll(dv, tm - 1, 0)[:tm - 8, :], pltpu.roll(window, 15, 0)[0:8, :]], axis=0)
        dp2 = jnp.concatenate([pltpu.roll(dv, tm - 2, 0)[:tm - 8, :], pltpu.roll(window, 14, 0)[0:8, :]], axis=0)
        cw = cw_ref[0]
        du_ref[0] = (cw[2:3] * dv + cw[1:2] * dp1 + cw[0:1] * dp2).astype(BF16)

    blk = pl.BlockSpec((1, tm, GU), lambda g, i: (g, i, 0))
    return _call(
        body, name="conv_transpose", grid=(NDEV, nt),
        in_specs=[blk, pl.BlockSpec((1, HALO, GU),
                                    lambda g, i: (g, jnp.minimum((i + 1) * (tm // HALO), s // HALO - 1), 0)),
                  pl.BlockSpec((1, 3, GU), lambda g, i: (g, 0, 0))],
        out_specs=blk,
        out_shape=_sds((NDEV, s, GU), BF16),
        compiler_params=_params(2),
    )(dy, dy, conv_w)


def _wgrad(a3, b3, *, tk, name):
    ga, s, ka = a3.shape
    gb, _, nb = b3.shape
    groups = max(ga, gb)
    nk = s // tk

    def body(a_ref, b_ref, o_ref, acc):
        k = pl.program_id(1)

        @pl.when(k == 0)
        def _():
            acc[...] = jnp.zeros_like(acc)

        acc[...] += _dot_tn(a_ref[0], b_ref[0])

        @pl.when(k == nk - 1)
        def _():
            o_ref[0] = acc[...].astype(BF16)

    a_spec = pl.BlockSpec((1, tk, ka), (lambda g, k: (g, k, 0)) if ga > 1 else (lambda g, k: (0, k, 0)))
    b_spec = pl.BlockSpec((1, tk, nb), (lambda g, k: (g, k, 0)) if gb > 1 else (lambda g, k: (0, k, 0)))
    return _call(
        body, name=name, grid=(groups, nk),
        in_specs=[a_spec, b_spec], out_specs=pl.BlockSpec((1, ka, nb), lambda g, k: (g, 0, 0)),
        out_shape=_sds((groups, ka, nb), BF16),
        scratch_shapes=[pltpu.VMEM((ka, nb), F32)],
        compiler_params=_params(2),
    )(a3, b3)


def _wgrad_in(h1, dparts, tk, after):
    s = h1.shape[0]
    nk = s // tk

    def body(a_ref, *refs):
        d_refs, o_ref, acc = refs[:NT], refs[NT], refs[NT + 1]
        k = pl.program_id(0)

        @pl.when(k == 0)
        def _():
            acc[...] = jnp.zeros_like(acc)

        for j in range(NT):
            acc[:, WG * j:WG * (j + 1)] += _dot_tn(a_ref[...], d_refs[j][...])

        @pl.when(k == nk - 1)
        def _():
            o_ref[...] = acc[...].astype(BF16)

    kw = dict(
        name="wgrad_in", grid=(nk,),
        in_specs=[pl.BlockSpec((tk, D), lambda k: (k, 0))] + [pl.BlockSpec((tk, WG), lambda k: (k, 0))] * NT,
        out_specs=pl.BlockSpec((D, NT * WG), lambda k: (0, 0)),
        out_shape=_sds((D, NT * WG), BF16),
        scratch_shapes=[pltpu.VMEM((D, NT * WG), F32)],
        compiler_params=_params(1),
    )
    call = _call(body, **kw) if after is None else _call_after(body, after, **kw)
    return call(h1, *dparts)


def _dgrad_in(dparts, w, x_in, scale, dx_up, tm, after):
    s = x_in.shape[0]

    def body(*refs):
        d_refs = refs[:NT]
        w_ref, x_ref, sc_ref, up_ref, dx_ref, dsc_ref, dsh_ref = refs[NT:]

        @pl.when(pl.program_id(0) == 0)
        def _():
            dsc_ref[...] = jnp.zeros_like(dsc_ref)
            dsh_ref[...] = jnp.zeros_like(dsh_ref)

        dh = _dot_nt(d_refs[0][...], w_ref[:, 0:WG])
        for j in range(1, NT):
            dh = dh + _dot_nt(d_refs[j][...], w_ref[:, WG * j:WG * (j + 1)])
        xv = x_ref[...]
        r = _rms(xv)
        nrm = xv * r
        dsh_ref[...] += _colsum(dh)
        dsc_ref[...] += _colsum(dh * nrm)
        dx_ref[...] = up_ref[...] + _rms_bwd(dh * (1.0 + sc_ref[...]), nrm, r)

    row = pl.BlockSpec((tm, D), lambda i: (i, 0))
    vec = pl.BlockSpec((1, D), lambda i: (0, 0))
    return _call_after(
        body, after, name="dgrad_in", grid=(s // tm,),
        in_specs=[pl.BlockSpec((tm, WG), lambda i: (i, 0))] * NT + [pl.BlockSpec((D, NT * WG), lambda i: (0, 0)), row, vec, row],
        out_specs=[row, vec, vec],
        out_shape=[_sds((s, D), F32), _sds((1, D), F32), _sds((1, D), F32)],
        compiler_params=_params(1),
    )(*dparts, w, x_in, scale, dx_up)


def _dgrad_norm_bwd(d3, w, x_in, scale, dx_up, *, tm, name, after):
    groups, s, n = d3.shape

    def body(d_ref, w_ref, x_ref, sc_ref, up_ref, dx_ref, dsc_ref, dsh_ref, acc):
        i, g = pl.program_id(0), pl.program_id(1)

        @pl.when((i == 0) & (g == 0))
        def _():
            dsc_ref[...] = jnp.zeros_like(dsc_ref)
            dsh_ref[...] = jnp.zeros_like(dsh_ref)

        part = _dot(d_ref[0], w_ref[0])

        @pl.when(g == 0)
        def _():
            acc[...] = part

        @pl.when(g > 0)
        def _():
            acc[...] += part

        @pl.when(g == groups - 1)
        def _():
            dh = acc[...]
            xv = x_ref[...]
            r = _rms(xv)
            nrm = xv * r
            dsh_ref[...] += _colsum(dh)
            dsc_ref[...] += _colsum(dh * nrm)
            dx_ref[...] = up_ref[...] + _rms_bwd(dh * (1.0 + sc_ref[...]), nrm, r)

    row = pl.BlockSpec((tm, D), lambda i, g: (i, 0))
    vec = pl.BlockSpec((1, D), lambda i, g: (0, 0))
    return _call_after(
        body, after, name=name, grid=(s // tm, groups),
        in_specs=[pl.BlockSpec((1, tm, n), lambda i, g: (g, i, 0)), pl.BlockSpec((1, n, D), lambda i, g: (g, 0, 0)),
                  row, vec, row],
        out_specs=[row, vec, vec],
        out_shape=[_sds((s, D), F32), _sds((1, D), F32), _sds((1, D), F32)],
        scratch_shapes=[pltpu.VMEM((tm, D), F32)],
        compiler_params=_params(2),
    )(d3, w, x_in, scale, dx_up)


def _mix_out_bwd(dx2, mixed, gate, w_out, oa, ob, g, tm):
    s = dx2.shape[0]

    def body(dx_ref, mixed_ref, gate_ref, w_ref, oa_ref, ob_ref, g_ref, dgate_ref, dm_ref, doa_ref, dob_ref, dg_ref):
        @pl.when(pl.program_id(0) == 0)
        def _():
            dgate_ref[...] = jnp.zeros_like(dgate_ref)
            dg_ref[...] = jnp.zeros_like(dg_ref)
        dxv = dx_ref[...]
        dgate_ref[...] += _colsum(dxv * mixed_ref[...])
        dm_ref[...] = (dxv * gate_ref[...]).astype(BF16)
        dnab = _dot_nt(dm_ref[...], w_ref[...])
        for o_ref, do_ref, sl in ((oa_ref, doa_ref, slice(0, WG)), (ob_ref, dob_ref, slice(WG, D))):
            ov = o_ref[...]
            r = _rms(ov)
            nrm = ov * r
            dn = dnab[:, sl]
            dg_ref[:, sl] += _colsum(dn * nrm)
            do_ref[...] = _rms_bwd(dn * g_ref[:, sl], nrm, r)

    row = pl.BlockSpec((tm, D), lambda i: (i, 0))
    half = pl.BlockSpec((tm, WG), lambda i: (i, 0))
    vec = pl.BlockSpec((1, D), lambda i: (0, 0))
    return _call(
        body, name="mix_out_bwd", grid=(s // tm,),
        in_specs=[row, row, vec, pl.BlockSpec((D, D), lambda i: (0, 0)), half, half, vec],
        out_specs=[vec, row, half, half, vec],
        out_shape=[_sds((1, D), F32), _sds((s, D), BF16), _sds((s, WG), F32), _sds((s, WG), F32), _sds((1, D), F32)],
        compiler_params=_params(1),
    )(dx2, mixed, gate, w_out, oa, ob, g)


def _attn_a_bwd(proj, bias_tab, d_oa, after):
    s = proj.shape[1]
    nq = s // TQA
    nkb = s // 128

    def body(q_ref, k_ref, v_ref, b_ref, do_ref, dq_ref, dk_ref, dv_ref, db_ref, dkt_acc, dvt_acc):
        qi = pl.program_id(1)

        @pl.when(qi == 0)
        def _():
            dkt_acc[...] = jnp.zeros_like(dkt_acc)
            dvt_acc[...] = jnp.zeros_like(dvt_acc)
            db_ref[...] = jnp.zeros_like(db_ref)

        heads = [slice(HD * h, HD * (h + 1)) for h in range(HA)]
        qs = [q_ref[0, :, sl] for sl in heads]
        dos = [do_ref[:, sl].astype(BF16) for sl in heads]
        qts = [_transpose_bf16(x, _eye(HD)) for x in qs]
        dots = [_transpose_bf16(x, _eye(HD)) for x in dos]

        def tile(nk, off, kstart):
            kb0 = kstart // 128
            for h in range(HA):
                sl = heads[h]
                k = k_ref[0, pl.ds(kstart, nk), sl]
                v = v_ref[0, pl.ds(kstart, nk), sl]
                sc = _dot_nt(qs[h], k) * 0.125 + b_ref[h, :, off:off + nk]
                p = jnp.exp(sc - jnp.max(sc, axis=-1, keepdims=True))
                p = p / jnp.sum(p, axis=-1, keepdims=True)
                dp = _dot_nt(dos[h], v)
                ds = p * (dp - jnp.sum(dp * p, axis=-1, keepdims=True))
                db_ref[h, :, off:off + nk] += ds
                dsb = (ds * 0.125).astype(BF16)
                dq_ref[:, sl] = _dot(dsb, k).astype(BF16)
                dkt = _dot(qts[h], dsb)
                dvt = _dot(dots[h], p.astype(BF16))
                for j in range(nk // 128):
                    dkt_acc[h * nkb + kb0 + j] += dkt[:, 128 * j:128 * (j + 1)]
                    dvt_acc[h * nkb + kb0 + j] += dvt[:, 128 * j:128 * (j + 1)]

        _attn_a_cases(qi, tile)

        @pl.when(qi == nq - 1)
        def _():
            eye = _eye(128)
            for h in range(HA):
                for kb in range(nkb):
                    rows_kb = slice(128 * kb, 128 * (kb + 1))
                    dk_ref[rows_kb, heads[h]] = _transpose_f32(dkt_acc[h * nkb + kb], eye).astype(BF16)
                    dv_ref[rows_kb, heads[h]] = _transpose_f32(dvt_acc[h * nkb + kb], eye).astype(BF16)

    q_spec, k_spec, v_spec, b_spec = _attn_a_specs(s)
    blk = pl.BlockSpec((TQA, WA), lambda hp, qi: (qi, hp))
    col = pl.BlockSpec((s, WA), lambda hp, qi: (0, hp))
    return _call_after(
        body, after, name="attn_a_bwd", grid=(WG // WA, nq),
        in_specs=[q_spec, k_spec, v_spec, b_spec, blk],
        out_specs=[blk, col, col, b_spec],
        out_shape=[_sds((s, WG), BF16), _sds((s, WG), BF16), _sds((s, WG), BF16), _sds((NDEV, TQA, KWA), F32)],
        scratch_shapes=[pltpu.VMEM((HA * nkb, HD, 128), F32), pltpu.VMEM((HA * nkb, HD, 128), F32)],
        compiler_params=_params(2),
    )(proj, proj, proj, bias_tab, d_oa)


def _attn_b_bwd(proj, lsum, d_ob):
    s = proj.shape[1]
    nq = s // TQB
    nkb = s // TKB
    ndiag = TQB // TKB

    def body(q_ref, k_ref, v_ref, ls_ref, do_ref, dq_ref, dk_ref, dv_ref, dkt_acc, dvt_acc):
        qi = pl.program_id(1)
        q0 = qi * TQB

        @pl.when(qi == 0)
        def _():
            dkt_acc[...] = jnp.zeros_like(dkt_acc)
            dvt_acc[...] = jnp.zeros_like(dvt_acc)

        rows = lax.broadcasted_iota(jnp.int32, (TQB, TKB), 0)
        cols = lax.broadcasted_iota(jnp.int32, (TQB, TKB), 1)
        uj = jnp.bitwise_and(lax.broadcasted_iota(jnp.int32, (2 * TKB, TKB), 0), TKB - 1)
        us = lax.broadcasted_iota(jnp.int32, (2 * TKB, TKB), 1)
        prefix = jnp.where(uj <= us, 1.0, 0.0).astype(BF16)
        heads = [slice(HD * h, HD * (h + 1)) for h in range(2)]
        qs = [(q_ref[0, :, sl].astype(F32) * 0.125).astype(BF16) for sl in heads]
        dos = [do_ref[:, sl].astype(BF16) for sl in heads]
        qts = [_transpose_bf16(x, _eye(HD)) for x in qs]
        dots = [_transpose_bf16(x, _eye(HD)) for x in dos]
        stots = [ls_ref[0, :, HD * h:HD * h + 1] for h in range(2)]
        qds = [jnp.concatenate([qs[h], dos[h]], axis=1) for h in range(2)]
        zkv = jnp.zeros((TKB, HD), BF16)

        def tile(h, kb, diag, cl, cg, dq):
            sl = heads[h]
            kstart = pl.multiple_of(kb * TKB, TKB)
            k = k_ref[0, pl.ds(kstart, TKB), sl]
            v = v_ref[0, pl.ds(kstart, TKB), sl]
            kv = jnp.concatenate([jnp.concatenate([k, zkv], axis=1), jnp.concatenate([zkv, v], axis=1)], axis=0)
            both = _dot_nt(qds[h], kv)
            lb, sp = _sb_terms(both[:, :TKB])
            if diag is not None:
                strict = rows > cols + diag * TKB
                sp = jnp.where(strict, sp, 0.0)
            pre = _dot2_deep(sp, prefix) + cl
            a = jnp.exp(lb - stots[h] + pre)
            if diag is not None:
                a = jnp.where(strict, a, 0.0)
            gz = both[:, TKB:] * a
            pg = _dot2_deep(gz, prefix) + cg
            dl = gz - pg * jnp.exp(lb)
            if diag is not None:
                dl = jnp.where(strict, dl, 0.0)
            dlb = dl.astype(BF16)
            dkt_acc[h * nkb + kb] += _dot(qts[h], dlb)
            dvt_acc[h * nkb + kb] += _dot(dots[h], a.astype(BF16))
            return pre[:, TKB - 1:TKB], pg[:, TKB - 1:TKB], dq + _dot(dlb, k)

        state = [jnp.zeros((TQB, 1), F32), jnp.zeros((TQB, 1), F32), jnp.zeros((TQB, HD), F32)] * 2

        def step(i, st):
            st = list(st)
            for sub in range(KSTEP):
                for h in range(2):
                    st[3 * h:3 * h + 3] = tile(h, i * KSTEP + sub, None, *st[3 * h:3 * h + 3])
            return tuple(st)

        state = list(lax.fori_loop(0, q0 // (KSTEP * TKB), step, tuple(state)))
        for d in range(ndiag):
            for h in range(2):
                state[3 * h:3 * h + 3] = tile(h, q0 // TKB + d, d, *state[3 * h:3 * h + 3])
        for h in range(2):
            dq_ref[:, heads[h]] = (state[3 * h + 2] * 0.125).astype(BF16)

        @pl.when(qi == nq - 1)
        def _():
            eye = _eye(TKB)
            for h in range(2):
                for kb in range(nkb):
                    rows_kb = slice(kb * TKB, (kb + 1) * TKB)
                    dk_ref[rows_kb, heads[h]] = _transpose_f32(dkt_acc[h * nkb + kb], eye).astype(BF16)
                    dv_ref[rows_kb, heads[h]] = _transpose_f32(dvt_acc[h * nkb + kb], eye).astype(BF16)

    q_spec, k_spec, v_spec = _attn_b_specs(s)
    blk = pl.BlockSpec((TQB, 128), lambda hp, qi: (qi, hp))
    col = pl.BlockSpec((s, 128), lambda hp, qi: (0, hp))
    return _call(
        body, name="attn_b_bwd", grid=(4, nq),
        in_specs=[q_spec, k_spec, v_spec, pl.BlockSpec((1, TQB, 128), lambda hp, qi: (hp, qi, 0)), blk],
        out_specs=[blk, col, col],
        out_shape=[_sds((s, WG), BF16)] * 3,
        scratch_shapes=[pltpu.VMEM((2 * nkb, HD, TKB), F32), pltpu.VMEM((2 * nkb, HD, TKB), F32)],
        compiler_params=_params(2),
    )(proj, proj, proj, lsum, d_ob)


def _bias_fold(dtab):
    def body(t_ref, d_ref, far_ref):
        acc = jnp.zeros((NDEV, TABW), F32)
        zpad = jnp.zeros((NDEV, TAB0), F32)
        for r in range(TQA):
            row = jnp.concatenate([zpad, t_ref[:, r, :]], axis=1)
            acc = acc + (pltpu.roll(row, TABW - r, 1) if r else row)
        d_ref[...] = acc
        lane = lax.broadcasted_iota(jnp.int32, (NDEV, TABW), 1)
        far = jnp.sum(jnp.where(lane < N_FAR, acc, 0.0), axis=1, keepdims=True)
        far_ref[...] = jnp.broadcast_to(far, (NDEV, 128))

    d_fpad, d_far = _call(
        body, name="bias_fold", grid=(1,),
        in_specs=[pl.BlockSpec((NDEV, TQA, KWA), lambda i: (0, 0, 0))],
        out_specs=[pl.BlockSpec((NDEV, TABW), lambda i: (0, 0)), pl.BlockSpec((NDEV, 128), lambda i: (0, 0))],
        out_shape=[_sds((NDEV, TABW), F32), _sds((NDEV, 128), F32)],
        compiler_params=_params(1),
    )(dtab)
    d_near = d_fpad[:, N_FAR:N_FAR + N_NEAR][:, ::-1]
    return jnp.concatenate([jnp.zeros((NDEV, REL_CLIP - CHUNK + 1), F32), d_near, d_far[:, :1]], axis=1)


def _adamw(w, g, m, v):
    m = B1 * m + (1.0 - B1) * g
    v = B2 * v + (1.0 - B2) * (g * g)
    m_hat = m / (1.0 - B1 ** STEP)
    v_hat = v / (1.0 - B2 ** STEP)
    delta = -LR * (m_hat / (jnp.sqrt(v_hat) + AEPS) + WD * w)
    return delta, m, v


def _adamw_big(recv0, recv1, w, m, v):
    _, rows, cols = recv0.shape
    tr = max(t for t in range(16, 513, 16) if rows % t == 0)
    nt = rows // tr

    def body(r0_ref, r1_ref, w_ref, m_ref, v_ref, g_ref, d_ref, nm_ref, nv_ref):
        def update(r_ref):
            g = r_ref[0].astype(F32)
            for p in range(1, NDEV):
                g = g + r_ref[p].astype(F32)
            delta, nm, nv = _adamw(w_ref[0], g, m_ref[0], v_ref[0])
            g_ref[0], d_ref[0], nm_ref[0], nv_ref[0] = g, delta, nm, nv

        @pl.when(pl.program_id(0) == 0)
        def _():
            update(r0_ref)

        @pl.when(pl.program_id(0) == 1)
        def _():
            update(r1_ref)

    blk = pl.BlockSpec((1, tr, cols), lambda l, i: (l, i, 0))
    r0_spec = pl.BlockSpec((NDEV, tr, cols), lambda l, i: (0, jnp.where(l == 0, i, nt - 1), 0))
    r1_spec = pl.BlockSpec((NDEV, tr, cols), lambda l, i: (0, jnp.where(l == 1, i, 0), 0))
    return _call(
        body, name="adamw_big", grid=(2, nt),
        in_specs=[r0_spec, r1_spec, blk, blk, blk],
        out_specs=[blk] * 4,
        out_shape=[_sds((2, rows, cols), F32)] * 4,
        compiler_params=_params(2),
    )(recv0, recv1, w, m, v)


def _adamw_w_ada(cact_t, dmod, w, m, v):
    tr = 256

    def body(c_ref, dm_ref, w_ref, m_ref, v_ref, g_ref, d_ref, nm_ref, nv_ref):
        g = c_ref[:, 0:1] * dm_ref[0, 0:1, :]
        for b in range(1, NDEV):
            g = g + c_ref[:, b:b + 1] * dm_ref[0, b:b + 1, :]
        delta, nm, nv = _adamw(w_ref[0], g, m_ref[0], v_ref[0])
        g_ref[0], d_ref[0], nm_ref[0], nv_ref[0] = g, delta, nm, nv

    blk = pl.BlockSpec((1, tr, 768), lambda l, i: (l, i, 0))
    return _call(
        body, name="adamw_w_ada", grid=(2, D // tr),
        in_specs=[pl.BlockSpec((tr, NDEV), lambda l, i: (i, 0)), pl.BlockSpec((1, NDEV, 768), lambda l, i: (l, 0, 0)),
                  blk, blk, blk],
        out_specs=[blk] * 4,
        out_shape=[_sds((2, D, 768), F32)] * 4,
        compiler_params=_params(2),
    )(cact_t, dmod, w, m, v)


def _adamw_small(gath, w, m, v):
    rows = gath.shape[1]

    def body(r_ref, w_ref, m_ref, v_ref, g_ref, d_ref, nm_ref, nv_ref):
        g = r_ref[0]
        for p in range(1, NDEV):
            g = g + r_ref[p]
        delta, nm, nv = _adamw(w_ref[...], g, m_ref[...], v_ref[...])
        g_ref[...], d_ref[...], nm_ref[...], nv_ref[...] = g, delta, nm, nv

    blk = pl.BlockSpec((rows, D), lambda i: (0, 0))
    return _call(
        body, name="adamw_small", grid=(1,),
        in_specs=[pl.BlockSpec((NDEV, rows, D), lambda i: (0, 0, 0)), blk, blk, blk],
        out_specs=[blk] * 4,
        out_shape=[_sds((rows, D), F32)] * 4,
        compiler_params=_params(1),
    )(gath, w, m, v)


_PACK = (("b_ada", 2 * 6 * D), ("rel_bias", 2 * 8 * 257), ("g_a", 2 * WG), ("g_b", 2 * WG),
         ("conv_b", 2 * 2 * DFF), ("final_g", D), ("conv_w", 2 * NDEV * 3 * GU))


def _pack(parts):
    rows = []
    for name, size in _PACK:
        flat = parts[name].reshape(-1).astype(F32)
        assert flat.shape[0] == size, (name, flat.shape)
        rows.append(jnp.pad(flat, (0, -size % D)))
    out = jnp.concatenate(rows).reshape(-1, D)
    return jnp.pad(out, ((0, -out.shape[0] % 8), (0, 0)))


def _unpack(packed):
    flat = packed.reshape(-1)
    out, pos = {}, 0
    for name, size in _PACK:
        out[name] = flat[pos:pos + size]
        pos += size + (-size % D)
    return out


def kernel(x, c, w_ada, b_ada, w_in, rel_bias, g_a, g_b, w_out, w_up, conv_w, conv_b, w_down, final_g, loss_target, m_w_ada, m_b_ada, m_w_in, m_rel_bias, m_g_a, m_g_b, m_w_out, m_w_up, m_conv_w, m_conv_b, m_w_down, m_final_g, v_w_ada, v_b_ada, v_w_in, v_rel_bias, v_g_a, v_g_b, v_w_out, v_w_up, v_conv_w, v_conv_b, v_w_down, v_final_g):
    s = x.shape[1]
    assert s % TQA == 0 and s >= KWA and s % 512 == 0
    tm = 512
    tmm = min(1024, s)
    me = 4 * lax.axis_index("x") + 2 * lax.axis_index("y") + lax.axis_index("c")
    xs = x.reshape(s, D)
    target = loss_target.reshape(s, D)

    first = jnp.concatenate([c, jnp.pad(conv_w.reshape(2 * 3, GU), ((0, 1), (0, D - GU)))])
    first_all = _small_allgather(first, "gather_c_conv_w")
    c_all = first_all[:, 0, :]
    cw_all = first_all[:, 1:7, :GU].reshape(NDEV, 2, 3, GU)

    b_sl = lax.dynamic_slice(b_ada, (0, me * 768), (2, 768)).reshape(2, 1, 768)
    mod_part, cact = _mod_fwd(c_all, w_ada, b_sl)
    mod_all = _small_allgather(mod_part.reshape(2 * NDEV, 768), "gather_mod")

    up_t = [jnp.transpose(t, (0, 2, 1)) for t in (w_up, m_w_up, v_w_up)]
    shards = {"in": w_in, "out": w_out, "up": up_t[0], "down": w_down}
    order = [(kind, l) for l in range(2) for kind in _KINDS]
    mod_all, *srcs = lax.optimization_barrier((mod_all, *[shards[kind][l].astype(BF16) for kind, l in order]))
    gather_started = dict(zip(order, _exchange_start([kind for kind, _ in order], True, srcs, "weights_gather_start")))

    def gathered(kind, l, after):
        return _exchange_wait([kind], True, [gather_started[kind, l]], after, f"weights_gather_wait_{kind}{l}")[0]

    mod_all = mod_all.reshape(NDEV, 2, NDEV, 768)
    mod_me = lax.dynamic_index_in_dim(mod_all, me, axis=2, keepdims=False)
    mod = jnp.transpose(mod_me, (1, 0, 2)).reshape(2, 6, 1, D)

    saved = []
    xl = xs
    for l in range(2):
        sh_mix, sc_mix, gt_mix, sh_ffn, sc_ffn, gt_ffn = (mod[l, j] for j in range(6))
        cw = cw_all[:, l].reshape(2, 4, 3, GU)
        cb = conv_b[l].reshape(2, 4, 1, GU)
        gvec = jnp.concatenate([g_a[l], g_b[l]]).reshape(1, D)
        tab = _bias_table(rel_bias[l])

        wi = gathered("in", l, xl if l else mod)
        h1, proj = _nm_matmul(xl, sh_mix, sc_mix, wi, two_d=True, n=WG, groups=NT, out_dtype=BF16, tm=tmm,
                              name="norm_proj")
        oa = _attn_a_fwd(proj, tab)
        ob, lsum = _attn_b_fwd(proj)
        wo = gathered("out", l, ob)
        nab, mixed, x2 = _mix_out(oa, ob, gvec, wo, xl, gt_mix, tm)
        wu = gathered("up", l, x2)
        h2, u = _nm_matmul(x2, sh_ffn, sc_ffn, wu, two_d=False, n=GU, groups=NDEV, out_dtype=BF16, tm=tmm,
                           name="norm_up")
        u = u.reshape(2, 4, s, GU)
        a = _conv_act(u, cw, cb, tm)
        wd4 = gathered("down", l, a).reshape(4, GU, D)
        ffn, x3 = _down(a, wd4, x2, gt_ffn, tmm)
        saved.append(dict(x=xl, h1=h1, proj=proj, oa=oa, ob=ob, lsum=lsum, nab=nab, mixed=mixed, x2=x2, h2=h2, u=u,
                          a=a, ffn=ffn, cw=cw, cb=cb, gvec=gvec, tab=tab, wd4=wd4, wi=wi, wo=wo, wu=wu))
        xl = x3

    loss_part, dx, d_final_g = _final_loss(xl, final_g.reshape(1, D), target, tm)
    loss = lax.psum(loss_part[0, 0], ("x", "y", "c"))

    sent = {}
    small = {"b_ada": [None, None], "rel_bias": [None, None], "g_a": [None, None], "g_b": [None, None],
             "conv_b": [None, None], "conv_w": [None, None]}

    def send(kind, l, grad):
        started, token = _exchange_start([kind], False, [grad], f"grads_start_{kind}{l}", with_token=True)
        sent[kind, l] = started[0]
        return token

    for l in (1, 0):
        sv = saved[l]
        sh_mix, sc_mix, gt_mix, sh_ffn, sc_ffn, gt_ffn = (mod[l, j] for j in range(6))
        d_gt_ffn, dff, da = _down_bwd(dx, gt_ffn, sv["ffn"], sv["wd4"], tmm)
        tok = send("down", l, _wgrad(sv["a"], dff.reshape(1, s, D), tk=tmm, name="wgrad_down").reshape(DFF, D))
        dy, d_cw, d_cb = _conv_act_bwd(sv["u"], sv["cw"], sv["cb"], da, tm, tok)
        du = _conv_transpose(dy.reshape(NDEV, s, GU), sv["cw"].reshape(NDEV, 3, GU), tm)
        tok = send("up", l, _wgrad(du, sv["h2"].reshape(1, s, D), tk=tmm, name="wgrad_up"))
        dx2, d_sc_ffn, d_sh_ffn = _dgrad_norm_bwd(du, sv["wu"], sv["x2"], sc_ffn, dx, tm=tmm, name="dgrad_up", after=tok)
        d_gt_mix, dmixed, d_oa, d_ob, d_g = _mix_out_bwd(dx2, sv["mixed"], gt_mix, sv["wo"], sv["oa"], sv["ob"],
                                                         sv["gvec"], tm)
        tok = send("out", l, _wgrad(sv["nab"].reshape(1, s, D), dmixed.reshape(1, s, D), tk=tmm,
                                    name="wgrad_out").reshape(D, D))
        dqa, dka, dva, d_tab = _attn_a_bwd(sv["proj"], sv["tab"], d_oa, tok)
        dqb, dkb, dvb = _attn_b_bwd(sv["proj"], sv["lsum"], d_ob)
        dparts = (dqa, dka, dva, dqb, dkb, dvb)
        if l:
            tok = send("in", l, _wgrad_in(sv["h1"], dparts, tmm, None))
        dx, d_sc_mix, d_sh_mix = _dgrad_in(dparts, sv["wi"], sv["x"], sc_mix, dx2, tm, tok)
        small["b_ada"][l] = jnp.concatenate([d_sh_mix, d_sc_mix, d_gt_mix, d_sh_ffn, d_sc_ffn, d_gt_ffn], axis=1)
        small["rel_bias"][l] = _bias_fold(d_tab)
        small["g_a"][l], small["g_b"][l] = d_g[:, :WG], d_g[:, WG:]
        small["conv_b"][l] = d_cb
        small["conv_w"][l] = d_cw.reshape(NDEV, 3, GU)
    grad_x = dx.reshape(1, s, D)

    contrib = {k: jnp.stack(vs) for k, vs in small.items()}
    contrib["final_g"] = d_final_g
    gath = _small_allgather(_pack(contrib), "gather_small_grads")
    tok = send("in", 0, _wgrad_in(saved[0]["h1"], dparts, tmm, gath))

    def place_conv_w(t):
        return lax.dynamic_update_slice(jnp.zeros((2, NDEV, 3, GU), F32), t.reshape(2, 1, 3, GU), (0, me, 0, 0))

    def packed_params(b, rb, ga, gb, cb_, fg, cw_):
        return _pack({"b_ada": b, "rel_bias": rb, "g_a": ga, "g_b": gb, "conv_b": cb_, "final_g": fg,
                      "conv_w": place_conv_w(cw_)})

    sm = _adamw_small(gath,
                      packed_params(b_ada, rel_bias, g_a, g_b, conv_b, final_g, conv_w),
                      packed_params(m_b_ada, m_rel_bias, m_g_a, m_g_b, m_conv_b, m_final_g, m_conv_w),
                      packed_params(v_b_ada, v_rel_bias, v_g_a, v_g_b, v_conv_b, v_final_g, v_conv_w))
    sm = [_unpack(t) for t in sm]

    dmod_all = gath[:, :12, :].reshape(NDEV, 2, 6 * D)
    dmod_sl = jnp.transpose(lax.dynamic_slice(dmod_all, (0, 0, me * 768), (NDEV, 2, 768)), (1, 0, 2))
    ada = _adamw_w_ada(cact.T, dmod_sl, w_ada, m_w_ada, v_w_ada)

    big = {}
    for kind, (w, m, v) in (("down", (w_down, m_w_down, v_w_down)), ("up", up_t),
                            ("out", (w_out, m_w_out, v_w_out)), ("in", (w_in, m_w_in, v_w_in))):
        recv0, recv1 = _exchange_wait([kind, kind], False, [sent[kind, 0], sent[kind, 1]], tok, f"grads_wait_{kind}")
        big[kind] = _adamw_big(recv0, recv1, w, m, v)
        tok = big[kind][0]
    big["up"] = [jnp.transpose(t, (0, 2, 1)) for t in big["up"]]

    def small_out(j, name):
        t = sm[j][name]
        if name == "b_ada":
            return t.reshape(2, 6 * D)
        if name == "rel_bias":
            return t.reshape(2, 8, 257)
        if name in ("g_a", "g_b"):
            return t.reshape(2, WG)
        if name == "conv_b":
            return t.reshape(2, 2 * DFF)
        if name == "final_g":
            return t.reshape(D)
        t = t.reshape(2, NDEV, 3, GU)
        return lax.dynamic_index_in_dim(t, me, axis=1, keepdims=False)

    def group(j):
        return (ada[j], small_out(j, "b_ada"), big["in"][j], small_out(j, "rel_bias"), small_out(j, "g_a"),
                small_out(j, "g_b"), big["out"][j], big["up"][j], small_out(j, "conv_w"), small_out(j, "conv_b"),
                big["down"][j], small_out(j, "final_g"))

    return (loss, grad_x, *group(0), *group(1), *group(2), *group(3))
```

```python
import jax
import jax.numpy as jnp
from jax import lax
from jax.experimental import pallas as pl
from jax.experimental.pallas import tpu as pltpu

F32, BF16 = jnp.float32, jnp.bfloat16
MESH_ID = pl.DeviceIdType.MESH
NDEV = 8
D = 1024
HD = 64
WG = 512
NT = 6
GU = 704
DFF = 2816
CHUNK, NPREV, REL_CLIP = 64, 8, 128
BAND = (NPREV + 1) * CHUNK
EPS = 1e-6
NEG = -1e30
TQA = 256
KWA = TQA + NPREV * CHUNK
TABW = 1024
TAB0 = TABW - KWA
N_FAR = TAB0 + NPREV * CHUNK - REL_CLIP + 1
N_NEAR = REL_CLIP + CHUNK - 1
TQB, TKB = 512, 128
KSTEP = 4
LR, B1, B2, AEPS, WD, STEP = 0.001, 0.9, 0.999, 1e-08, 0.01, 10
VMEM_MB = 56


def _call(body, **kw):
    return pl.pallas_call(body, **kw)


def _call_after(body, after, **kw):
    n_in = len(kw["in_specs"])
    kw["in_specs"] = list(kw["in_specs"]) + [pl.BlockSpec(memory_space=pl.ANY)]

    def tied(*refs):
        body(*refs[:n_in], *refs[n_in + 1:])

    call = _call(tied, **kw)
    return lambda *args: call(*args, after)


def _params(n_axes):
    return pltpu.CompilerParams(dimension_semantics=("arbitrary",) * n_axes, vmem_limit_bytes=VMEM_MB << 20)


def _dot(a, b):
    return jnp.dot(a, b, preferred_element_type=F32)


def _dot_nt(a, b):
    return lax.dot_general(a, b, (((1,), (1,)), ((), ())), preferred_element_type=F32)


def _dot_tn(a, b):
    return lax.dot_general(a, b, (((0,), (0,)), ((), ())), preferred_element_type=F32)


def _dot2(x, u):
    hi = x.astype(BF16)
    lo = (x - hi.astype(F32)).astype(BF16)
    return _dot(hi, u) + _dot(lo, u)


def _dot2_deep(x, uu):
    hi = x.astype(BF16)
    lo = (x - hi.astype(F32)).astype(BF16)
    return _dot(jnp.concatenate([hi, lo], axis=1), uu)


def _eye(n):
    i = lax.broadcasted_iota(jnp.int32, (n, n), 0)
    j = lax.broadcasted_iota(jnp.int32, (n, n), 1)
    return jnp.where(i == j, 1.0, 0.0).astype(BF16)


def _transpose_bf16(x, eye):
    return _dot_nt(eye, x).astype(BF16)


def _transpose_f32(x, eye):
    hi = x.astype(BF16)
    lo = (x - hi.astype(F32)).astype(BF16)
    return _dot_nt(eye, hi) + _dot_nt(eye, lo)


def _rms(x):
    return lax.rsqrt(jnp.mean(x * x, axis=-1, keepdims=True) + EPS)


def _rms_bwd(dn, n, r):
    return r * (dn - n * jnp.mean(dn * n, axis=-1, keepdims=True))


def _colsum(x):
    return jnp.sum(x, axis=0, keepdims=True)


def _sigmoid(x):
    return 1.0 / (1.0 + jnp.exp(-x))


def _sds(shape, dtype):
    return jax.ShapeDtypeStruct(shape, dtype)


def _place():
    x, y, c = lax.axis_index("x"), lax.axis_index("y"), lax.axis_index("c")
    return x, y, c, 4 * x + 2 * y + c


def _peer(x, y, c, k):
    px = 1 - x if k & 4 else x
    py = 1 - y if k & 2 else y
    pc = 1 - c if k & 1 else c
    return (px, py, pc), 4 * px + 2 * py + pc


def _small_allgather(v, name):
    rows, cols = v.shape

    def body(v_ref, out_ref, send_sems, recv_sems, local_sem):
        x, y, c, me = _place()
        mine = pltpu.make_async_copy(v_ref, out_ref.at[me], local_sem)
        mine.start()
        sends = []
        for k in range(1, NDEV):
            peer, _ = _peer(x, y, c, k)
            cp = pltpu.make_async_remote_copy(v_ref, out_ref.at[me], send_sems.at[k - 1], recv_sems.at[k - 1],
                                              device_id=peer, device_id_type=MESH_ID)
            cp.start()
            sends.append(cp)
        for k in range(1, NDEV):
            peer, pidx = _peer(x, y, c, k)
            pltpu.make_async_remote_copy(v_ref, out_ref.at[pidx], send_sems.at[k - 1], recv_sems.at[k - 1],
                                         device_id=peer, device_id_type=MESH_ID).wait_recv()
        for cp in sends:
            cp.wait_send()
        mine.wait()

    return _call(
        body, name=name,
        out_shape=_sds((NDEV, rows, cols), F32),
        in_specs=[pl.BlockSpec(memory_space=pltpu.VMEM)],
        out_specs=pl.BlockSpec(memory_space=pltpu.VMEM),
        scratch_shapes=[pltpu.SemaphoreType.DMA((NDEV - 1,)), pltpu.SemaphoreType.DMA((NDEV - 1,)),
                        pltpu.SemaphoreType.DMA],
    )(v)


def _shard_view(ref, kind, p):
    if kind == "in":
        return ref.at[:, pl.ds(pl.multiple_of(p * 384, 128), 384)]
    if kind == "out":
        return ref.at[pl.ds(pl.multiple_of(p * 128, 128), 128), :]
    if kind == "up":
        return ref.at[p]
    if kind == "down":
        return ref.at[pl.ds(pl.multiple_of(p * 352, 16), 352), :]
    raise ValueError(kind)


_KINDS = ("in", "out", "up", "down")
_FULL_SHAPES = {"in": (D, 3 * D), "out": (D, D), "up": (NDEV, GU, D), "down": (DFF, D)}
_SHARD_SHAPES = {"in": (D, 384), "out": (128, D), "up": (GU, D), "down": (352, D)}


_HBM = pl.BlockSpec(memory_space=pltpu.HBM)
_SEM = pl.BlockSpec(memory_space=pltpu.SEMAPHORE)
_EFFECT = pltpu.SideEffectType.DATAFLOW_SIDE_EFFECTING
_SEM_SHAPES = (pltpu.SemaphoreType.DMA((NDEV - 1,)), pltpu.SemaphoreType.DMA((NDEV - 1,)), pltpu.SemaphoreType.DMA(()))


def _hbm(a):
    return pltpu.with_memory_space_constraint(a, pltpu.HBM)


def _exchange_copies(kind, gather, src, land, sems):
    send_sems, recv_sems, local_sem = sems
    x, y, c, me = _place()

    def ends(p_dst, p_from):
        if gather:
            return src, _shard_view(land, kind, me), _shard_view(land, kind, p_from)
        return _shard_view(src, kind, p_dst), land.at[me], land.at[p_from]

    s_me, d_me, _ = ends(me, me)
    local = pltpu.make_async_copy(s_me, d_me, local_sem)
    sends, arrivals = [], []
    for k in range(1, NDEV):
        peer, pidx = _peer(x, y, c, k)
        s_k, d_k, from_k = ends(pidx, pidx)
        sends.append(pltpu.make_async_remote_copy(s_k, d_k, send_sems.at[k - 1], recv_sems.at[k - 1],
                                                  device_id=peer, device_id_type=MESH_ID))
        arrivals.append(pltpu.make_async_remote_copy(s_k, from_k, send_sems.at[k - 1], recv_sems.at[k - 1],
                                                     device_id=peer, device_id_type=MESH_ID))
    return local, sends, arrivals


def _exchange_start(kinds, gather, srcs, name, with_token=False):
    n = len(kinds)
    lands = [lax.empty(_FULL_SHAPES[kd] if gather else (NDEV,) + _SHARD_SHAPES[kd], BF16) for kd in kinds]

    def body(*refs):
        ins, sems = refs[:2 * n], refs[2 * n:5 * n]
        for j, kd in enumerate(kinds):
            local, sends, _ = _exchange_copies(kd, gather, ins[j], ins[n + j], sems[3 * j:3 * j + 3])
            local.start()
            for cp in sends:
                cp.start()
        if with_token:
            token = refs[7 * n]
            token[...] = jnp.zeros_like(token)

    out_shape = list(_SEM_SHAPES) * n
    out_shape += [pltpu.HBM(a.shape, a.dtype) for a in srcs] + [pltpu.HBM(a.shape, a.dtype) for a in lands]
    out_specs = [_SEM] * (3 * n) + [_HBM] * (2 * n)
    if with_token:
        out_shape.append(_sds((8, 128), F32))
        out_specs.append(pl.BlockSpec(memory_space=pltpu.VMEM))
    outs = _call(
        body, name=name, out_shape=out_shape,
        in_specs=[_HBM] * (2 * n), out_specs=out_specs,
        input_output_aliases={i: 3 * n + i for i in range(2 * n)},
        compiler_params=pltpu.CompilerParams(has_side_effects=_EFFECT),
    )(*[_hbm(a) for a in srcs], *[_hbm(a) for a in lands])
    per_tensor = [(tuple(outs[3 * j:3 * j + 3]), outs[3 * n + j], outs[4 * n + j]) for j in range(n)]
    return (per_tensor, outs[5 * n]) if with_token else per_tensor


def _exchange_wait(kinds, gather, started, after, name):
    n = len(kinds)

    def body(*refs):
        ins, sems = refs[:2 * n], refs[2 * n:5 * n]
        for j, kd in enumerate(kinds):
            local, sends, arrivals = _exchange_copies(kd, gather, ins[j], ins[n + j], sems[3 * j:3 * j + 3])
            local.wait()
            for cp in arrivals:
                cp.wait_recv()
            for cp in sends:
                cp.wait_send()

    srcs = [st[1] for st in started]
    lands = [st[2] for st in started]
    sems = [sm for st in started for sm in st[0]]
    outs = _call(
        body, name=name,
        out_shape=[pltpu.HBM(a.shape, a.dtype) for a in srcs] + [pltpu.HBM(a.shape, a.dtype) for a in lands],
        in_specs=[_HBM] * (2 * n) + [_SEM] * (3 * n) + [pl.BlockSpec(memory_space=pl.ANY)],
        out_specs=[_HBM] * (2 * n),
        input_output_aliases={i: i for i in range(2 * n)},
        compiler_params=pltpu.CompilerParams(has_side_effects=_EFFECT),
    )(*srcs, *lands, *sems, after)
    return outs[n:]


def _mod_fwd(c_all, w_ada, b_sl):
    def body(c_ref, w_ref, b_ref, mod_ref, cact_ref):
        cv = c_ref[...]
        ca = cv * _sigmoid(cv)
        cact_ref[...] = ca
        mod_ref[0] = _dot(ca.astype(BF16), w_ref[0].astype(BF16)) + b_ref[0]

    return _call(
        body, name="mod_fwd", grid=(2,),
        in_specs=[pl.BlockSpec((NDEV, D), lambda l: (0, 0)), pl.BlockSpec((1, D, 768), lambda l: (l, 0, 0)),
                  pl.BlockSpec((1, 1, 768), lambda l: (l, 0, 0))],
        out_specs=[pl.BlockSpec((1, NDEV, 768), lambda l: (l, 0, 0)), pl.BlockSpec((NDEV, D), lambda l: (0, 0))],
        out_shape=[_sds((2, NDEV, 768), F32), _sds((NDEV, D), F32)],
        compiler_params=_params(1),
    )(c_all, w_ada, b_sl)


def _nm_matmul(x, shift, scale, w, *, two_d, n, groups, out_dtype, tm, name):
    s = x.shape[0]

    def body(x_ref, sh_ref, sc_ref, w_ref, h_ref, o_ref):
        @pl.when(pl.program_id(1) == 0)
        def _():
            xv = x_ref[...]
            h_ref[...] = ((xv * _rms(xv)) * (1.0 + sc_ref[...]) + sh_ref[...]).astype(BF16)
        if two_d:
            o_ref[0] = _dot(h_ref[...], w_ref[...]).astype(out_dtype)
        else:
            o_ref[0] = _dot_nt(h_ref[...], w_ref[0]).astype(out_dtype)

    vec = pl.BlockSpec((1, D), lambda i, g: (0, 0))
    w_spec = pl.BlockSpec((D, n), lambda i, g: (0, g)) if two_d else pl.BlockSpec((1, n, D), lambda i, g: (g, 0, 0))
    return _call(
        body, name=name, grid=(s // tm, groups),
        in_specs=[pl.BlockSpec((tm, D), lambda i, g: (i, 0)), vec, vec, w_spec],
        out_specs=[pl.BlockSpec((tm, D), lambda i, g: (i, 0)), pl.BlockSpec((1, tm, n), lambda i, g: (g, i, 0))],
        out_shape=[_sds((s, D), BF16), _sds((groups, s, n), out_dtype)],
        compiler_params=_params(2),
    )(x, shift, scale, w)


def _bias_table(rel_bias):
    far = jnp.broadcast_to(rel_bias[:, 2 * REL_CLIP:], (NDEV, N_FAR))
    near = rel_bias[:, 2 * REL_CLIP - 1:REL_CLIP - CHUNK:-1]
    fpad = jnp.concatenate([far, near, jnp.zeros((NDEV, TABW - N_FAR - N_NEAR), F32)], axis=1)

    def body(f_ref, o_ref):
        t = pltpu.roll(jnp.broadcast_to(f_ref[0], (TQA, TABW)), 0, 1, stride=1, stride_axis=0)[:, TAB0:]
        rows = lax.broadcasted_iota(jnp.int32, (TQA, KWA), 0)
        cols = lax.broadcasted_iota(jnp.int32, (TQA, KWA), 1)
        first = jnp.bitwise_and(rows, -CHUNK)
        o_ref[0] = jnp.where((cols >= first) & (cols < first + BAND), t, NEG)

    return _call(
        body, name="bias_table", grid=(NDEV,),
        in_specs=[pl.BlockSpec((1, 1, TABW), lambda h: (h, 0, 0))],
        out_specs=pl.BlockSpec((1, TQA, KWA), lambda h: (h, 0, 0)),
        out_shape=_sds((NDEV, TQA, KWA), F32),
        compiler_params=_params(1),
    )(fpad.reshape(NDEV, 1, TABW))


def _attn_a_cases(qi, tile):
    @pl.when(qi == 0)
    def _():
        tile(TQA, 2 * TQA, 0)

    @pl.when(qi == 1)
    def _():
        tile(2 * TQA, TQA, 0)

    @pl.when(qi >= 2)
    def _():
        tile(KWA, 0, pl.multiple_of((qi - 2) * TQA, TQA))


HA = 8
WA = HA * HD


def _attn_a_specs(s):
    q_spec = pl.BlockSpec((1, TQA, WA), lambda hp, qi: (0, qi, hp))
    k_spec = pl.BlockSpec((1, s, WA), lambda hp, qi: (1, 0, hp))
    v_spec = pl.BlockSpec((1, s, WA), lambda hp, qi: (2, 0, hp))
    b_spec = pl.BlockSpec((HA, TQA, KWA), lambda hp, qi: (hp, 0, 0))
    return q_spec, k_spec, v_spec, b_spec


def _attn_a_fwd(proj, bias_tab):
    s = proj.shape[1]

    def body(q_ref, k_ref, v_ref, b_ref, o_ref):
        def tile(nk, off, kstart):
            for h in range(HA):
                sl = slice(HD * h, HD * (h + 1))
                q = q_ref[0, :, sl]
                k = k_ref[0, pl.ds(kstart, nk), sl]
                v = v_ref[0, pl.ds(kstart, nk), sl]
                sc = _dot_nt(q, k) * 0.125 + b_ref[h, :, off:off + nk]
                p = jnp.exp(sc - jnp.max(sc, axis=-1, keepdims=True))
                den = jnp.sum(p, axis=-1, keepdims=True)
                o_ref[:, sl] = _dot(p.astype(BF16), v) / den

        _attn_a_cases(pl.program_id(1), tile)

    q_spec, k_spec, v_spec, b_spec = _attn_a_specs(s)
    return _call(
        body, name="attn_a_fwd", grid=(WG // WA, s // TQA),
        in_specs=[q_spec, k_spec, v_spec, b_spec],
        out_specs=pl.BlockSpec((TQA, WA), lambda hp, qi: (qi, hp)),
        out_shape=_sds((s, WG), F32),
        compiler_params=_params(2),
    )(proj, proj, proj, bias_tab)


def _sb_terms(lg):
    sp = jnp.maximum(lg, 0.0) + jnp.log(1.0 + jnp.exp(-jnp.abs(lg)))
    return lg - sp, sp


def _attn_b_specs(s):
    q_spec = pl.BlockSpec((1, TQB, 128), lambda hp, qi: (3, qi, hp))
    k_spec = pl.BlockSpec((1, s, 128), lambda hp, qi: (4, 0, hp))
    v_spec = pl.BlockSpec((1, s, 128), lambda hp, qi: (5, 0, hp))
    return q_spec, k_spec, v_spec


def _attn_b_fwd(proj):
    s = proj.shape[1]
    ndiag = TQB // TKB

    def body(q_ref, k_ref, v_ref, o_ref, ls_ref):
        q0 = pl.program_id(1) * TQB
        rows = lax.broadcasted_iota(jnp.int32, (TQB, TKB), 0)
        cols = lax.broadcasted_iota(jnp.int32, (TQB, TKB), 1)
        uj = lax.broadcasted_iota(jnp.int32, (TKB, TKB), 0)
        us = lax.broadcasted_iota(jnp.int32, (TKB, TKB), 1)
        suffix = jnp.where(uj >= us, 1.0, 0.0).astype(BF16)
        heads = [slice(HD * h, HD * (h + 1)) for h in range(2)]
        qs = [(q_ref[0, :, sl].astype(F32) * 0.125).astype(BF16) for sl in heads]

        def tile(h, kstart, diag, carry, acc):
            k = k_ref[0, pl.ds(kstart, TKB), heads[h]]
            v = v_ref[0, pl.ds(kstart, TKB), heads[h]]
            lb, sp = _sb_terms(_dot_nt(qs[h], k))
            if diag is not None:
                strict = rows > cols + diag * TKB
                sp = jnp.where(strict, sp, 0.0)
            csum = _dot2(sp, suffix) + carry
            w = jnp.exp(lb - csum + sp)
            if diag is not None:
                w = jnp.where(strict, w, 0.0)
            return csum[:, 0:1], acc + _dot(w.astype(BF16), v)

        state = [jnp.zeros((TQB, 1), F32), jnp.zeros((TQB, HD), F32)] * 2
        for d in range(ndiag - 1, -1, -1):
            for h in range(2):
                state[2 * h:2 * h + 2] = tile(h, pl.multiple_of(q0 + d * TKB, TKB), d, *state[2 * h:2 * h + 2])
        nsteps = q0 // (KSTEP * TKB)

        def step(i, st):
            st = list(st)
            base = (nsteps - 1 - i) * (KSTEP * TKB)
            for sub in range(KSTEP - 1, -1, -1):
                for h in range(2):
                    st[2 * h:2 * h + 2] = tile(h, pl.multiple_of(base + sub * TKB, TKB), None, *st[2 * h:2 * h + 2])
            return tuple(st)

        state = lax.fori_loop(0, nsteps, step, tuple(state))
        for h in range(2):
            o_ref[:, heads[h]] = state[2 * h + 1]
            ls_ref[0, :, heads[h]] = jnp.broadcast_to(state[2 * h], (TQB, HD))

    q_spec, k_spec, v_spec = _attn_b_specs(s)
    return _call(
        body, name="attn_b_fwd", grid=(4, s // TQB),
        in_specs=[q_spec, k_spec, v_spec],
        out_specs=[pl.BlockSpec((TQB, 128), lambda hp, qi: (qi, hp)),
                   pl.BlockSpec((1, TQB, 128), lambda hp, qi: (hp, qi, 0))],
        out_shape=[_sds((s, WG), F32), _sds((4, s, 128), F32)],
        compiler_params=_params(2),
    )(proj, proj, proj)


def _mix_out(oa, ob, g, w_out, x, gate, tm):
    s = x.shape[0]

    def body(oa_ref, ob_ref, g_ref, w_ref, x_ref, gate_ref, nab_ref, mixed_ref, x2_ref):
        a, b = oa_ref[...], ob_ref[...]
        nab_ref[:, :WG] = (a * _rms(a) * g_ref[:, :WG]).astype(BF16)
        nab_ref[:, WG:] = (b * _rms(b) * g_ref[:, WG:]).astype(BF16)
        mixed = _dot(nab_ref[...], w_ref[...])
        mixed_ref[...] = mixed
        x2_ref[...] = x_ref[...] + gate_ref[...] * mixed

    row = pl.BlockSpec((tm, D), lambda i: (i, 0))
    half = pl.BlockSpec((tm, WG), lambda i: (i, 0))
    vec = pl.BlockSpec((1, D), lambda i: (0, 0))
    return _call(
        body, name="mix_out", grid=(s // tm,),
        in_specs=[half, half, vec, pl.BlockSpec((D, D), lambda i: (0, 0)), row, vec],
        out_specs=[row, row, row],
        out_shape=[_sds((s, D), BF16), _sds((s, D), F32), _sds((s, D), F32)],
        compiler_params=_params(1),
    )(oa, ob, g, w_out, x, gate)


HALO = 16


def _conv_taps(u, halo, first, tm):
    keep = jnp.where(first, 0.0, 1.0)
    window = jnp.concatenate([halo[HALO - 8:HALO, :] * keep, u[0:8, :]], axis=0)
    um1 = jnp.concatenate([pltpu.roll(window, 1, 0)[8:16, :], pltpu.roll(u, 1, 0)[8:, :]], axis=0)
    um2 = jnp.concatenate([pltpu.roll(window, 2, 0)[8:16, :], pltpu.roll(u, 2, 0)[8:, :]], axis=0)
    return um1, um2


def _conv_specs(tm):
    u_spec = pl.BlockSpec((2, 1, tm, GU), lambda p, i: (0, p, i, 0))
    halo_spec = pl.BlockSpec((2, 1, HALO, GU), lambda p, i: (0, p, jnp.maximum(i * (tm // HALO) - 1, 0), 0))
    cw_spec = pl.BlockSpec((2, 1, 3, GU), lambda p, i: (0, p, 0, 0))
    cb_spec = pl.BlockSpec((2, 1, 1, GU), lambda p, i: (0, p, 0, 0))
    return u_spec, halo_spec, cw_spec, cb_spec


def _conv_act(u, conv_w, conv_b, tm):
    s = u.shape[2]

    def body(u_ref, halo_ref, cw_ref, cb_ref, a_ref):
        first = pl.program_id(1) == 0
        ys = []
        for side in range(2):
            uv = u_ref[side, 0].astype(F32)
            um1, um2 = _conv_taps(uv, halo_ref[side, 0].astype(F32), first, tm)
            cw = cw_ref[side, 0]
            ys.append(cw[2:3] * uv + cw[1:2] * um1 + cw[0:1] * um2 + cb_ref[side, 0])
        a_ref[0] = (ys[0] * _sigmoid(ys[0]) * ys[1]).astype(BF16)

    u_spec, halo_spec, cw_spec, cb_spec = _conv_specs(tm)
    return _call(
        body, name="conv_act", grid=(4, s // tm),
        in_specs=[u_spec, halo_spec, cw_spec, cb_spec],
        out_specs=pl.BlockSpec((1, tm, GU), lambda p, i: (p, i, 0)),
        out_shape=_sds((4, s, GU), BF16),
        compiler_params=_params(2),
    )(u, u, conv_w, conv_b)


def _down(a, w_down, x2, gate, tm):
    s = x2.shape[0]

    def body(a_ref, w_ref, x_ref, gate_ref, ffn_ref, x3_ref):
        p = pl.program_id(1)
        part = _dot(a_ref[0], w_ref[0])

        @pl.when(p == 0)
        def _():
            ffn_ref[...] = part

        @pl.when(p > 0)
        def _():
            ffn_ref[...] += part

        @pl.when(p == 3)
        def _():
            x3_ref[...] = x_ref[...] + gate_ref[...] * ffn_ref[...]

    row = pl.BlockSpec((tm, D), lambda i, p: (i, 0))
    return _call(
        body, name="down", grid=(s // tm, 4),
        in_specs=[pl.BlockSpec((1, tm, GU), lambda i, p: (p, i, 0)), pl.BlockSpec((1, GU, D), lambda i, p: (p, 0, 0)),
                  row, pl.BlockSpec((1, D), lambda i, p: (0, 0))],
        out_specs=[row, row],
        out_shape=[_sds((s, D), F32), _sds((s, D), F32)],
        compiler_params=_params(2),
    )(a, w_down, x2, gate)


def _final_loss(x, g, target, tm):
    s = x.shape[0]

    def body(x_ref, g_ref, t_ref, loss_ref, dx_ref, dg_ref):
        @pl.when(pl.program_id(0) == 0)
        def _():
            loss_ref[...] = jnp.zeros_like(loss_ref)
            dg_ref[...] = jnp.zeros_like(dg_ref)
        xv = x_ref[...]
        r = _rms(xv)
        nrm = xv * r
        err = nrm * g_ref[...] - t_ref[...]
        loss_ref[...] += (0.5 / D) * jnp.sum(jnp.sum(err * err, axis=-1, keepdims=True), axis=0, keepdims=True)
        dy = err * (1.0 / D)
        dg_ref[...] += _colsum(dy * nrm)
        dx_ref[...] = _rms_bwd(dy * g_ref[...], nrm, r)

    row = pl.BlockSpec((tm, D), lambda i: (i, 0))
    vec = pl.BlockSpec((1, D), lambda i: (0, 0))
    return _call(
        body, name="final_loss", grid=(s // tm,),
        in_specs=[row, vec, row],
        out_specs=[pl.BlockSpec((1, 1), lambda i: (0, 0)), row, vec],
        out_shape=[_sds((1, 1), F32), _sds((s, D), F32), _sds((1, D), F32)],
        compiler_params=_params(1),
    )(x, g, target)


def _down_bwd(dx3, gate, ffn, w_down, tm):
    s = dx3.shape[0]

    def body(dx_ref, gate_ref, ffn_ref, w_ref, dgate_ref, dff_ref, da_ref):
        i, p = pl.program_id(0), pl.program_id(1)

        @pl.when((i == 0) & (p == 0))
        def _():
            dgate_ref[...] = jnp.zeros_like(dgate_ref)

        @pl.when(p == 0)
        def _():
            dxv = dx_ref[...]
            dgate_ref[...] += _colsum(dxv * ffn_ref[...])
            dff_ref[...] = (dxv * gate_ref[...]).astype(BF16)

        da_ref[0] = _dot_nt(dff_ref[...], w_ref[0]).astype(BF16)

    row = pl.BlockSpec((tm, D), lambda i, p: (i, 0))
    vec = pl.BlockSpec((1, D), lambda i, p: (0, 0))
    return _call(
        body, name="down_bwd", grid=(s // tm, 4),
        in_specs=[row, vec, row, pl.BlockSpec((1, GU, D), lambda i, p: (p, 0, 0))],
        out_specs=[vec, row, pl.BlockSpec((1, tm, GU), lambda i, p: (p, i, 0))],
        out_shape=[_sds((1, D), F32), _sds((s, D), BF16), _sds((4, s, GU), BF16)],
        compiler_params=_params(2),
    )(dx3, gate, ffn, w_down)


def _conv_act_bwd(u, conv_w, conv_b, da, tm, after):
    s = u.shape[2]

    def body(u_ref, halo_ref, cw_ref, cb_ref, da_ref, dy_ref, dcw_ref, dcb_ref):
        first = pl.program_id(1) == 0

        @pl.when(first)
        def _():
            dcw_ref[...] = jnp.zeros_like(dcw_ref)
            dcb_ref[...] = jnp.zeros_like(dcb_ref)

        taps, ys = [], []
        for side in range(2):
            uv = u_ref[side, 0].astype(F32)
            um1, um2 = _conv_taps(uv, halo_ref[side, 0].astype(F32), first, tm)
            cw = cw_ref[side, 0]
            taps.append((um2, um1, uv))
            ys.append(cw[2:3] * uv + cw[1:2] * um1 + cw[0:1] * um2 + cb_ref[side, 0])
        dav = da_ref[0].astype(F32)
        sg = _sigmoid(ys[0])
        dys = (dav * ys[1] * (sg * (1.0 + ys[0] * (1.0 - sg))), dav * (ys[0] * sg))
        for side in range(2):
            dy_ref[side, 0] = dys[side].astype(BF16)
            dcb_ref[side, 0] += _colsum(dys[side])
            for j in range(3):
                dcw_ref[side, 0, j:j + 1, :] += _colsum(dys[side] * taps[side][j])

    u_spec, halo_spec, cw_spec, cb_spec = _conv_specs(tm)
    return _call_after(
        body, after, name="conv_act_bwd", grid=(4, s // tm),
        in_specs=[u_spec, halo_spec, cw_spec, cb_spec, pl.BlockSpec((1, tm, GU), lambda p, i: (p, i, 0))],
        out_specs=[u_spec, cw_spec, cb_spec],
        out_shape=[_sds((2, 4, s, GU), BF16), _sds((2, 4, 3, GU), F32), _sds((2, 4, 1, GU), F32)],
        compiler_params=_params(2),
    )(u, u, conv_w, conv_b, da)


def _conv_transpose(dy, conv_w, tm):
    s = dy.shape[1]
    nt = s // tm

    def body(dy_ref, halo_ref, cw_ref, du_ref):
        keep = jnp.where(pl.program_id(1) == nt - 1, 0.0, 1.0)
        dv = dy_ref[0].astype(F32)
        window = jnp.concatenate([dv[tm - 8:, :], halo_ref[0, 0:8, :].astype(F32) * keep], axis=0)
        dp1 = jnp.concatenate([pltpu.roll(dv, tm - 1, 0)[:tm - 8, :], pltpu.roll(window, 15, 0)[0:8, :]], axis=0)
        dp2 = jnp.concatenate([pltpu.roll(dv, tm - 2, 0)[:tm - 8, :], pltpu.roll(window, 14, 0)[0:8, :]], axis=0)
        cw = cw_ref[0]
        du_ref[0] = (cw[2:3] * dv + cw[1:2] * dp1 + cw[0:1] * dp2).astype(BF16)

    blk = pl.BlockSpec((1, tm, GU), lambda g, i: (g, i, 0))
    return _call(
        body, name="conv_transpose", grid=(NDEV, nt),
        in_specs=[blk, pl.BlockSpec((1, HALO, GU),
                                    lambda g, i: (g, jnp.minimum((i + 1) * (tm // HALO), s // HALO - 1), 0)),
                  pl.BlockSpec((1, 3, GU), lambda g, i: (g, 0, 0))],
        out_specs=blk,
        out_shape=_sds((NDEV, s, GU), BF16),
        compiler_params=_params(2),
    )(dy, dy, conv_w)


def _wgrad(a3, b3, *, tk, name):
    ga, s, ka = a3.shape
    gb, _, nb = b3.shape
    groups = max(ga, gb)
    nk = s // tk

    def body(a_ref, b_ref, o_ref, acc):
        k = pl.program_id(1)

        @pl.when(k == 0)
        def _():
            acc[...] = jnp.zeros_like(acc)

        acc[...] += _dot_tn(a_ref[0], b_ref[0])

        @pl.when(k == nk - 1)
        def _():
            o_ref[0] = acc[...].astype(BF16)

    a_spec = pl.BlockSpec((1, tk, ka), (lambda g, k: (g, k, 0)) if ga > 1 else (lambda g, k: (0, k, 0)))
    b_spec = pl.BlockSpec((1, tk, nb), (lambda g, k: (g, k, 0)) if gb > 1 else (lambda g, k: (0, k, 0)))
    return _call(
        body, name=name, grid=(groups, nk),
        in_specs=[a_spec, b_spec], out_specs=pl.BlockSpec((1, ka, nb), lambda g, k: (g, 0, 0)),
        out_shape=_sds((groups, ka, nb), BF16),
        scratch_shapes=[pltpu.VMEM((ka, nb), F32)],
        compiler_params=_params(2),
    )(a3, b3)


def _wgrad_in(h1, dparts, tk, after):
    s = h1.shape[0]
    nk = s // tk

    def body(a_ref, *refs):
        d_refs, o_ref, acc = refs[:NT], refs[NT], refs[NT + 1]
        k = pl.program_id(0)

        @pl.when(k == 0)
        def _():
            acc[...] = jnp.zeros_like(acc)

        for j in range(NT):
            acc[:, WG * j:WG * (j + 1)] += _dot_tn(a_ref[...], d_refs[j][...])

        @pl.when(k == nk - 1)
        def _():
            o_ref[...] = acc[...].astype(BF16)

    kw = dict(
        name="wgrad_in", grid=(nk,),
        in_specs=[pl.BlockSpec((tk, D), lambda k: (k, 0))] + [pl.BlockSpec((tk, WG), lambda k: (k, 0))] * NT,
        out_specs=pl.BlockSpec((D, NT * WG), lambda k: (0, 0)),
        out_shape=_sds((D, NT * WG), BF16),
        scratch_shapes=[pltpu.VMEM((D, NT * WG), F32)],
        compiler_params=_params(1),
    )
    call = _call(body, **kw) if after is None else _call_after(body, after, **kw)
    return call(h1, *dparts)


def _dgrad_in(dparts, w, x_in, scale, dx_up, tm, after):
    s = x_in.shape[0]

    def body(*refs):
        d_refs = refs[:NT]
        w_ref, x_ref, sc_ref, up_ref, dx_ref, dsc_ref, dsh_ref = refs[NT:]

        @pl.when(pl.program_id(0) == 0)
        def _():
            dsc_ref[...] = jnp.zeros_like(dsc_ref)
            dsh_ref[...] = jnp.zeros_like(dsh_ref)

        dh = _dot_nt(d_refs[0][...], w_ref[:, 0:WG])
        for j in range(1, NT):
            dh = dh + _dot_nt(d_refs[j][...], w_ref[:, WG * j:WG * (j + 1)])
        xv = x_ref[...]
        r = _rms(xv)
        nrm = xv * r
        dsh_ref[...] += _colsum(dh)
        dsc_ref[...] += _colsum(dh * nrm)
        dx_ref[...] = up_ref[...] + _rms_bwd(dh * (1.0 + sc_ref[...]), nrm, r)

    row = pl.BlockSpec((tm, D), lambda i: (i, 0))
    vec = pl.BlockSpec((1, D), lambda i: (0, 0))
    return _call_after(
        body, after, name="dgrad_in", grid=(s // tm,),
        in_specs=[pl.BlockSpec((tm, WG), lambda i: (i, 0))] * NT + [pl.BlockSpec((D, NT * WG), lambda i: (0, 0)), row, vec, row],
        out_specs=[row, vec, vec],
        out_shape=[_sds((s, D), F32), _sds((1, D), F32), _sds((1, D), F32)],
        compiler_params=_params(1),
    )(*dparts, w, x_in, scale, dx_up)


def _dgrad_norm_bwd(d3, w, x_in, scale, dx_up, *, tm, name, after):
    groups, s, n = d3.shape

    def body(d_ref, w_ref, x_ref, sc_ref, up_ref, dx_ref, dsc_ref, dsh_ref, acc):
        i, g = pl.program_id(0), pl.program_id(1)

        @pl.when((i == 0) & (g == 0))
        def _():
            dsc_ref[...] = jnp.zeros_like(dsc_ref)
            dsh_ref[...] = jnp.zeros_like(dsh_ref)

        part = _dot(d_ref[0], w_ref[0])

        @pl.when(g == 0)
        def _():
            acc[...] = part

        @pl.when(g > 0)
        def _():
            acc[...] += part

        @pl.when(g == groups - 1)
        def _():
            dh = acc[...]
            xv = x_ref[...]
            r = _rms(xv)
            nrm = xv * r
            dsh_ref[...] += _colsum(dh)
            dsc_ref[...] += _colsum(dh * nrm)
            dx_ref[...] = up_ref[...] + _rms_bwd(dh * (1.0 + sc_ref[...]), nrm, r)

    row = pl.BlockSpec((tm, D), lambda i, g: (i, 0))
    vec = pl.BlockSpec((1, D), lambda i, g: (0, 0))
    return _call_after(
        body, after, name=name, grid=(s // tm, groups),
        in_specs=[pl.BlockSpec((1, tm, n), lambda i, g: (g, i, 0)), pl.BlockSpec((1, n, D), lambda i, g: (g, 0, 0)),
                  row, vec, row],
        out_specs=[row, vec, vec],
        out_shape=[_sds((s, D), F32), _sds((1, D), F32), _sds((1, D), F32)],
        scratch_shapes=[pltpu.VMEM((tm, D), F32)],
        compiler_params=_params(2),
    )(d3, w, x_in, scale, dx_up)


def _mix_out_bwd(dx2, mixed, gate, w_out, oa, ob, g, tm):
    s = dx2.shape[0]

    def body(dx_ref, mixed_ref, gate_ref, w_ref, oa_ref, ob_ref, g_ref, dgate_ref, dm_ref, doa_ref, dob_ref, dg_ref):
        @pl.when(pl.program_id(0) == 0)
        def _():
            dgate_ref[...] = jnp.zeros_like(dgate_ref)
            dg_ref[...] = jnp.zeros_like(dg_ref)
        dxv = dx_ref[...]
        dgate_ref[...] += _colsum(dxv * mixed_ref[...])
        dm_ref[...] = (dxv * gate_ref[...]).astype(BF16)
        dnab = _dot_nt(dm_ref[...], w_ref[...])
        for o_ref, do_ref, sl in ((oa_ref, doa_ref, slice(0, WG)), (ob_ref, dob_ref, slice(WG, D))):
            ov = o_ref[...]
            r = _rms(ov)
            nrm = ov * r
            dn = dnab[:, sl]
            dg_ref[:, sl] += _colsum(dn * nrm)
            do_ref[...] = _rms_bwd(dn * g_ref[:, sl], nrm, r)

    row = pl.BlockSpec((tm, D), lambda i: (i, 0))
    half = pl.BlockSpec((tm, WG), lambda i: (i, 0))
    vec = pl.BlockSpec((1, D), lambda i: (0, 0))
    return _call(
        body, name="mix_out_bwd", grid=(s // tm,),
        in_specs=[row, row, vec, pl.BlockSpec((D, D), lambda i: (0, 0)), half, half, vec],
        out_specs=[vec, row, half, half, vec],
        out_shape=[_sds((1, D), F32), _sds((s, D), BF16), _sds((s, WG), F32), _sds((s, WG), F32), _sds((1, D), F32)],
        compiler_params=_params(1),
    )(dx2, mixed, gate, w_out, oa, ob, g)


def _attn_a_bwd(proj, bias_tab, d_oa, after):
    s = proj.shape[1]
    nq = s // TQA
    nkb = s // 128

    def body(q_ref, k_ref, v_ref, b_ref, do_ref, dq_ref, dk_ref, dv_ref, db_ref, dkt_acc, dvt_acc):
        qi = pl.program_id(1)

        @pl.when(qi == 0)
        def _():
            dkt_acc[...] = jnp.zeros_like(dkt_acc)
            dvt_acc[...] = jnp.zeros_like(dvt_acc)
            db_ref[...] = jnp.zeros_like(db_ref)

        heads = [slice(HD * h, HD * (h + 1)) for h in range(HA)]
        qs = [q_ref[0, :, sl] for sl in heads]
        dos = [do_ref[:, sl].astype(BF16) for sl in heads]
        qts = [_transpose_bf16(x, _eye(HD)) for x in qs]
        dots = [_transpose_bf16(x, _eye(HD)) for x in dos]

        def tile(nk, off, kstart):
            kb0 = kstart // 128
            for h in range(HA):
                sl = heads[h]
                k = k_ref[0, pl.ds(kstart, nk), sl]
                v = v_ref[0, pl.ds(kstart, nk), sl]
                sc = _dot_nt(qs[h], k) * 0.125 + b_ref[h, :, off:off + nk]
                p = jnp.exp(sc - jnp.max(sc, axis=-1, keepdims=True))
                p = p / jnp.sum(p, axis=-1, keepdims=True)
                dp = _dot_nt(dos[h], v)
                ds = p * (dp - jnp.sum(dp * p, axis=-1, keepdims=True))
                db_ref[h, :, off:off + nk] += ds
                dsb = (ds * 0.125).astype(BF16)
                dq_ref[:, sl] = _dot(dsb, k).astype(BF16)
                dkt = _dot(qts[h], dsb)
                dvt = _dot(dots[h], p.astype(BF16))
                for j in range(nk // 128):
                    dkt_acc[h * nkb + kb0 + j] += dkt[:, 128 * j:128 * (j + 1)]
                    dvt_acc[h * nkb + kb0 + j] += dvt[:, 128 * j:128 * (j + 1)]

        _attn_a_cases(qi, tile)

        @pl.when(qi == nq - 1)
        def _():
            eye = _eye(128)
            for h in range(HA):
                for kb in range(nkb):
                    rows_kb = slice(128 * kb, 128 * (kb + 1))
                    dk_ref[rows_kb, heads[h]] = _transpose_f32(dkt_acc[h * nkb + kb], eye).astype(BF16)
                    dv_ref[rows_kb, heads[h]] = _transpose_f32(dvt_acc[h * nkb + kb], eye).astype(BF16)

    q_spec, k_spec, v_spec, b_spec = _attn_a_specs(s)
    blk = pl.BlockSpec((TQA, WA), lambda hp, qi: (qi, hp))
    col = pl.BlockSpec((s, WA), lambda hp, qi: (0, hp))
    return _call_after(
        body, after, name="attn_a_bwd", grid=(WG // WA, nq),
        in_specs=[q_spec, k_spec, v_spec, b_spec, blk],
        out_specs=[blk, col, col, b_spec],
        out_shape=[_sds((s, WG), BF16), _sds((s, WG), BF16), _sds((s, WG), BF16), _sds((NDEV, TQA, KWA), F32)],
        scratch_shapes=[pltpu.VMEM((HA * nkb, HD, 128), F32), pltpu.VMEM((HA * nkb, HD, 128), F32)],
        compiler_params=_params(2),
    )(proj, proj, proj, bias_tab, d_oa)


def _attn_b_bwd(proj, lsum, d_ob):
    s = proj.shape[1]
    nq = s // TQB
    nkb = s // TKB
    ndiag = TQB // TKB

    def body(q_ref, k_ref, v_ref, ls_ref, do_ref, dq_ref, dk_ref, dv_ref, dkt_acc, dvt_acc):
        qi = pl.program_id(1)
        q0 = qi * TQB

        @pl.when(qi == 0)
        def _():
            dkt_acc[...] = jnp.zeros_like(dkt_acc)
            dvt_acc[...] = jnp.zeros_like(dvt_acc)

        rows = lax.broadcasted_iota(jnp.int32, (TQB, TKB), 0)
        cols = lax.broadcasted_iota(jnp.int32, (TQB, TKB), 1)
        uj = jnp.bitwise_and(lax.broadcasted_iota(jnp.int32, (2 * TKB, TKB), 0), TKB - 1)
        us = lax.broadcasted_iota(jnp.int32, (2 * TKB, TKB), 1)
        prefix = jnp.where(uj <= us, 1.0, 0.0).astype(BF16)
        heads = [slice(HD * h, HD * (h + 1)) for h in range(2)]
        qs = [(q_ref[0, :, sl].astype(F32) * 0.125).astype(BF16) for sl in heads]
        dos = [do_ref[:, sl].astype(BF16) for sl in heads]
        qts = [_transpose_bf16(x, _eye(HD)) for x in qs]
        dots = [_transpose_bf16(x, _eye(HD)) for x in dos]
        stots = [ls_ref[0, :, HD * h:HD * h + 1] for h in range(2)]
        qds = [jnp.concatenate([qs[h], dos[h]], axis=1) for h in range(2)]
        zkv = jnp.zeros((TKB, HD), BF16)

        def tile(h, kb, diag, cl, cg, dq):
            r0 = 0 if diag is None else diag * TKB
            sl = heads[h]
            kstart = pl.multiple_of(kb * TKB, TKB)
            k = k_ref[0, pl.ds(kstart, TKB), sl]
            v = v_ref[0, pl.ds(kstart, TKB), sl]
            kv = jnp.concatenate([jnp.concatenate([k, zkv], axis=1), jnp.concatenate([zkv, v], axis=1)], axis=0)
            both = _dot_nt(qds[h][r0:, :], kv)
            lb, sp = _sb_terms(both[:, :TKB])
            if diag is not None:
                strict = rows[:TQB - r0, :] > cols[:TQB - r0, :]
                sp = jnp.where(strict, sp, 0.0)
            pre = _dot2_deep(sp, prefix) + cl[r0:, :]
            a = jnp.exp(lb - stots[h][r0:, :] + pre)
            if diag is not None:
                a = jnp.where(strict, a, 0.0)
            gz = both[:, TKB:] * a
            pg = _dot2_deep(gz, prefix) + cg[r0:, :]
            dl = gz - pg * jnp.exp(lb)
            if diag is not None:
                dl = jnp.where(strict, dl, 0.0)
            dlb = dl.astype(BF16)
            dkt_acc[h * nkb + kb] += _dot(qts[h][:, r0:], dlb)
            dvt_acc[h * nkb + kb] += _dot(dots[h][:, r0:], a.astype(BF16))
            new = (pre[:, TKB - 1:TKB], pg[:, TKB - 1:TKB], dq[r0:, :] + _dot(dlb, k))
            if r0:
                new = tuple(jnp.concatenate([old[:r0, :], x], axis=0) for old, x in zip((cl, cg, dq), new))
            return new

        state = [jnp.zeros((TQB, 1), F32), jnp.zeros((TQB, 1), F32), jnp.zeros((TQB, HD), F32)] * 2

        def step(i, st):
            st = list(st)
            for sub in range(KSTEP):
                for h in range(2):
                    st[3 * h:3 * h + 3] = tile(h, i * KSTEP + sub, None, *st[3 * h:3 * h + 3])
            return tuple(st)

        state = list(lax.fori_loop(0, q0 // (KSTEP * TKB), step, tuple(state)))
        for d in range(ndiag):
            for h in range(2):
                state[3 * h:3 * h + 3] = tile(h, q0 // TKB + d, d, *state[3 * h:3 * h + 3])
        for h in range(2):
            dq_ref[:, heads[h]] = (state[3 * h + 2] * 0.125).astype(BF16)

        @pl.when(qi == nq - 1)
        def _():
            eye = _eye(TKB)
            for h in range(2):
                for kb in range(nkb):
                    rows_kb = slice(kb * TKB, (kb + 1) * TKB)
                    dk_ref[rows_kb, heads[h]] = _transpose_f32(dkt_acc[h * nkb + kb], eye).astype(BF16)
                    dv_ref[rows_kb, heads[h]] = _transpose_f32(dvt_acc[h * nkb + kb], eye).astype(BF16)

    q_spec, k_spec, v_spec = _attn_b_specs(s)
    blk = pl.BlockSpec((TQB, 128), lambda hp, qi: (qi, hp))
    col = pl.BlockSpec((s, 128), lambda hp, qi: (0, hp))
    return _call(
        body, name="attn_b_bwd", grid=(4, nq),
        in_specs=[q_spec, k_spec, v_spec, pl.BlockSpec((1, TQB, 128), lambda hp, qi: (hp, qi, 0)), blk],
        out_specs=[blk, col, col],
        out_shape=[_sds((s, WG), BF16)] * 3,
        scratch_shapes=[pltpu.VMEM((2 * nkb, HD, TKB), F32), pltpu.VMEM((2 * nkb, HD, TKB), F32)],
        compiler_params=_params(2),
    )(proj, proj, proj, lsum, d_ob)


def _bias_fold(dtab):
    def body(t_ref, d_ref, far_ref):
        acc = jnp.zeros((NDEV, TABW), F32)
        zpad = jnp.zeros((NDEV, TAB0), F32)
        for r in range(TQA):
            row = jnp.concatenate([zpad, t_ref[:, r, :]], axis=1)
            acc = acc + (pltpu.roll(row, TABW - r, 1) if r else row)
        d_ref[...] = acc
        lane = lax.broadcasted_iota(jnp.int32, (NDEV, TABW), 1)
        far = jnp.sum(jnp.where(lane < N_FAR, acc, 0.0), axis=1, keepdims=True)
        far_ref[...] = jnp.broadcast_to(far, (NDEV, 128))

    d_fpad, d_far = _call(
        body, name="bias_fold", grid=(1,),
        in_specs=[pl.BlockSpec((NDEV, TQA, KWA), lambda i: (0, 0, 0))],
        out_specs=[pl.BlockSpec((NDEV, TABW), lambda i: (0, 0)), pl.BlockSpec((NDEV, 128), lambda i: (0, 0))],
        out_shape=[_sds((NDEV, TABW), F32), _sds((NDEV, 128), F32)],
        compiler_params=_params(1),
    )(dtab)
    d_near = d_fpad[:, N_FAR:N_FAR + N_NEAR][:, ::-1]
    return jnp.concatenate([jnp.zeros((NDEV, REL_CLIP - CHUNK + 1), F32), d_near, d_far[:, :1]], axis=1)


def _adamw(w, g, m, v):
    m = B1 * m + (1.0 - B1) * g
    v = B2 * v + (1.0 - B2) * (g * g)
    m_hat = m / (1.0 - B1 ** STEP)
    v_hat = v / (1.0 - B2 ** STEP)
    delta = -LR * (m_hat / (jnp.sqrt(v_hat) + AEPS) + WD * w)
    return delta, m, v


def _adamw_big(recv0, recv1, w, m, v):
    _, rows, cols = recv0.shape
    tr = max(t for t in range(16, 513, 16) if rows % t == 0)
    nt = rows // tr

    def body(r0_ref, r1_ref, w_ref, m_ref, v_ref, g_ref, d_ref, nm_ref, nv_ref):
        def update(r_ref):
            g = r_ref[0].astype(F32)
            for p in range(1, NDEV):
                g = g + r_ref[p].astype(F32)
            delta, nm, nv = _adamw(w_ref[0], g, m_ref[0], v_ref[0])
            g_ref[0], d_ref[0], nm_ref[0], nv_ref[0] = g, delta, nm, nv

        @pl.when(pl.program_id(0) == 0)
        def _():
            update(r0_ref)

        @pl.when(pl.program_id(0) == 1)
        def _():
            update(r1_ref)

    blk = pl.BlockSpec((1, tr, cols), lambda l, i: (l, i, 0))
    r0_spec = pl.BlockSpec((NDEV, tr, cols), lambda l, i: (0, jnp.where(l == 0, i, nt - 1), 0))
    r1_spec = pl.BlockSpec((NDEV, tr, cols), lambda l, i: (0, jnp.where(l == 1, i, 0), 0))
    return _call(
        body, name="adamw_big", grid=(2, nt),
        in_specs=[r0_spec, r1_spec, blk, blk, blk],
        out_specs=[blk] * 4,
        out_shape=[_sds((2, rows, cols), F32)] * 4,
        compiler_params=_params(2),
    )(recv0, recv1, w, m, v)


def _adamw_w_ada(cact_t, dmod, w, m, v):
    tr = 256

    def body(c_ref, dm_ref, w_ref, m_ref, v_ref, g_ref, d_ref, nm_ref, nv_ref):
        g = c_ref[:, 0:1] * dm_ref[0, 0:1, :]
        for b in range(1, NDEV):
            g = g + c_ref[:, b:b + 1] * dm_ref[0, b:b + 1, :]
        delta, nm, nv = _adamw(w_ref[0], g, m_ref[0], v_ref[0])
        g_ref[0], d_ref[0], nm_ref[0], nv_ref[0] = g, delta, nm, nv

    blk = pl.BlockSpec((1, tr, 768), lambda l, i: (l, i, 0))
    return _call(
        body, name="adamw_w_ada", grid=(2, D // tr),
        in_specs=[pl.BlockSpec((tr, NDEV), lambda l, i: (i, 0)), pl.BlockSpec((1, NDEV, 768), lambda l, i: (l, 0, 0)),
                  blk, blk, blk],
        out_specs=[blk] * 4,
        out_shape=[_sds((2, D, 768), F32)] * 4,
        compiler_params=_params(2),
    )(cact_t, dmod, w, m, v)


def _adamw_small(gath, w, m, v):
    rows = gath.shape[1]

    def body(r_ref, w_ref, m_ref, v_ref, g_ref, d_ref, nm_ref, nv_ref):
        g = r_ref[0]
        for p in range(1, NDEV):
            g = g + r_ref[p]
        delta, nm, nv = _adamw(w_ref[...], g, m_ref[...], v_ref[...])
        g_ref[...], d_ref[...], nm_ref[...], nv_ref[...] = g, delta, nm, nv

    blk = pl.BlockSpec((rows, D), lambda i: (0, 0))
    return _call(
        body, name="adamw_small", grid=(1,),
        in_specs=[pl.BlockSpec((NDEV, rows, D), lambda i: (0, 0, 0)), blk, blk, blk],
        out_specs=[blk] * 4,
        out_shape=[_sds((rows, D), F32)] * 4,
        compiler_params=_params(1),
    )(gath, w, m, v)


_PACK = (("b_ada", 2 * 6 * D), ("rel_bias", 2 * 8 * 257), ("g_a", 2 * WG), ("g_b", 2 * WG),
         ("conv_b", 2 * 2 * DFF), ("final_g", D), ("conv_w", 2 * NDEV * 3 * GU))


def _pack(parts):
    rows = []
    for name, size in _PACK:
        flat = parts[name].reshape(-1).astype(F32)
        assert flat.shape[0] == size, (name, flat.shape)
        rows.append(jnp.pad(flat, (0, -size % D)))
    out = jnp.concatenate(rows).reshape(-1, D)
    return jnp.pad(out, ((0, -out.shape[0] % 8), (0, 0)))


def _unpack(packed):
    flat = packed.reshape(-1)
    out, pos = {}, 0
    for name, size in _PACK:
        out[name] = flat[pos:pos + size]
        pos += size + (-size % D)
    return out


def kernel(x, c, w_ada, b_ada, w_in, rel_bias, g_a, g_b, w_out, w_up, conv_w, conv_b, w_down, final_g, loss_target, m_w_ada, m_b_ada, m_w_in, m_rel_bias, m_g_a, m_g_b, m_w_out, m_w_up, m_conv_w, m_conv_b, m_w_down, m_final_g, v_w_ada, v_b_ada, v_w_in, v_rel_bias, v_g_a, v_g_b, v_w_out, v_w_up, v_conv_w, v_conv_b, v_w_down, v_final_g):
    s = x.shape[1]
    assert s % TQA == 0 and s >= KWA and s % 512 == 0
    tm = 512
    tmm = min(1024, s)
    me = 4 * lax.axis_index("x") + 2 * lax.axis_index("y") + lax.axis_index("c")
    xs = x.reshape(s, D)
    target = loss_target.reshape(s, D)

    first = jnp.concatenate([c, jnp.pad(conv_w.reshape(2 * 3, GU), ((0, 1), (0, D - GU)))])
    first_all = _small_allgather(first, "gather_c_conv_w")
    c_all = first_all[:, 0, :]
    cw_all = first_all[:, 1:7, :GU].reshape(NDEV, 2, 3, GU)

    b_sl = lax.dynamic_slice(b_ada, (0, me * 768), (2, 768)).reshape(2, 1, 768)
    mod_part, cact = _mod_fwd(c_all, w_ada, b_sl)
    mod_all = _small_allgather(mod_part.reshape(2 * NDEV, 768), "gather_mod")

    up_t = [jnp.transpose(t, (0, 2, 1)) for t in (w_up, m_w_up, v_w_up)]
    shards = {"in": w_in, "out": w_out, "up": up_t[0], "down": w_down}
    order = [(kind, l) for l in range(2) for kind in _KINDS]
    mod_all, *srcs = lax.optimization_barrier((mod_all, *[shards[kind][l].astype(BF16) for kind, l in order]))
    gather_started = dict(zip(order, _exchange_start([kind for kind, _ in order], True, srcs, "weights_gather_start")))

    def gathered(kind, l, after):
        return _exchange_wait([kind], True, [gather_started[kind, l]], after, f"weights_gather_wait_{kind}{l}")[0]

    mod_all = mod_all.reshape(NDEV, 2, NDEV, 768)
    mod_me = lax.dynamic_index_in_dim(mod_all, me, axis=2, keepdims=False)
    mod = jnp.transpose(mod_me, (1, 0, 2)).reshape(2, 6, 1, D)

    saved = []
    xl = xs
    for l in range(2):
        sh_mix, sc_mix, gt_mix, sh_ffn, sc_ffn, gt_ffn = (mod[l, j] for j in range(6))
        cw = cw_all[:, l].reshape(2, 4, 3, GU)
        cb = conv_b[l].reshape(2, 4, 1, GU)
        gvec = jnp.concatenate([g_a[l], g_b[l]]).reshape(1, D)
        tab = _bias_table(rel_bias[l])

        wi = gathered("in", l, xl if l else mod)
        h1, proj = _nm_matmul(xl, sh_mix, sc_mix, wi, two_d=True, n=WG, groups=NT, out_dtype=BF16, tm=tmm,
                              name="norm_proj")
        oa = _attn_a_fwd(proj, tab)
        ob, lsum = _attn_b_fwd(proj)
        wo = gathered("out", l, ob)
        nab, mixed, x2 = _mix_out(oa, ob, gvec, wo, xl, gt_mix, tm)
        wu = gathered("up", l, x2)
        h2, u = _nm_matmul(x2, sh_ffn, sc_ffn, wu, two_d=False, n=GU, groups=NDEV, out_dtype=BF16, tm=tmm,
                           name="norm_up")
        u = u.reshape(2, 4, s, GU)
        a = _conv_act(u, cw, cb, tm)
        wd4 = gathered("down", l, a).reshape(4, GU, D)
        ffn, x3 = _down(a, wd4, x2, gt_ffn, tmm)
        saved.append(dict(x=xl, h1=h1, proj=proj, oa=oa, ob=ob, lsum=lsum, nab=nab, mixed=mixed, x2=x2, h2=h2, u=u,
                          a=a, ffn=ffn, cw=cw, cb=cb, gvec=gvec, tab=tab, wd4=wd4, wi=wi, wo=wo, wu=wu))
        xl = x3

    loss_part, dx, d_final_g = _final_loss(xl, final_g.reshape(1, D), target, tm)
    loss = lax.psum(loss_part[0, 0], ("x", "y", "c"))

    sent = {}
    small = {"b_ada": [None, None], "rel_bias": [None, None], "g_a": [None, None], "g_b": [None, None],
             "conv_b": [None, None], "conv_w": [None, None]}

    def send(kind, l, grad):
        started, token = _exchange_start([kind], False, [grad], f"grads_start_{kind}{l}", with_token=True)
        sent[kind, l] = started[0]
        return token

    for l in (1, 0):
        sv = saved[l]
        sh_mix, sc_mix, gt_mix, sh_ffn, sc_ffn, gt_ffn = (mod[l, j] for j in range(6))
        d_gt_ffn, dff, da = _down_bwd(dx, gt_ffn, sv["ffn"], sv["wd4"], tmm)
        tok = send("down", l, _wgrad(sv["a"], dff.reshape(1, s, D), tk=tmm, name="wgrad_down").reshape(DFF, D))
        dy, d_cw, d_cb = _conv_act_bwd(sv["u"], sv["cw"], sv["cb"], da, tm, tok)
        du = _conv_transpose(dy.reshape(NDEV, s, GU), sv["cw"].reshape(NDEV, 3, GU), tm)
        tok = send("up", l, _wgrad(du, sv["h2"].reshape(1, s, D), tk=tmm, name="wgrad_up"))
        dx2, d_sc_ffn, d_sh_ffn = _dgrad_norm_bwd(du, sv["wu"], sv["x2"], sc_ffn, dx, tm=tmm, name="dgrad_up", after=tok)
        d_gt_mix, dmixed, d_oa, d_ob, d_g = _mix_out_bwd(dx2, sv["mixed"], gt_mix, sv["wo"], sv["oa"], sv["ob"],
                                                         sv["gvec"], tm)
        tok = send("out", l, _wgrad(sv["nab"].reshape(1, s, D), dmixed.reshape(1, s, D), tk=tmm,
                                    name="wgrad_out").reshape(D, D))
        dqa, dka, dva, d_tab = _attn_a_bwd(sv["proj"], sv["tab"], d_oa, tok)
        dqb, dkb, dvb = _attn_b_bwd(sv["proj"], sv["lsum"], d_ob)
        dparts = (dqa, dka, dva, dqb, dkb, dvb)
        if l:
            tok = send("in", l, _wgrad_in(sv["h1"], dparts, tmm, None))
        dx, d_sc_mix, d_sh_mix = _dgrad_in(dparts, sv["wi"], sv["x"], sc_mix, dx2, tm, tok)
        small["b_ada"][l] = jnp.concatenate([d_sh_mix, d_sc_mix, d_gt_mix, d_sh_ffn, d_sc_ffn, d_gt_ffn], axis=1)
        small["rel_bias"][l] = _bias_fold(d_tab)
        small["g_a"][l], small["g_b"][l] = d_g[:, :WG], d_g[:, WG:]
        small["conv_b"][l] = d_cb
        small["conv_w"][l] = d_cw.reshape(NDEV, 3, GU)
    grad_x = dx.reshape(1, s, D)

    contrib = {k: jnp.stack(vs) for k, vs in small.items()}
    contrib["final_g"] = d_final_g
    gath = _small_allgather(_pack(contrib), "gather_small_grads")
    tok = send("in", 0, _wgrad_in(saved[0]["h1"], dparts, tmm, gath))

    def place_conv_w(t):
        return lax.dynamic_update_slice(jnp.zeros((2, NDEV, 3, GU), F32), t.reshape(2, 1, 3, GU), (0, me, 0, 0))

    def packed_params(b, rb, ga, gb, cb_, fg, cw_):
        return _pack({"b_ada": b, "rel_bias": rb, "g_a": ga, "g_b": gb, "conv_b": cb_, "final_g": fg,
                      "conv_w": place_conv_w(cw_)})

    sm = _adamw_small(gath,
                      packed_params(b_ada, rel_bias, g_a, g_b, conv_b, final_g, conv_w),
                      packed_params(m_b_ada, m_rel_bias, m_g_a, m_g_b, m_conv_b, m_final_g, m_conv_w),
                      packed_params(v_b_ada, v_rel_bias, v_g_a, v_g_b, v_conv_b, v_final_g, v_conv_w))
    sm = [_unpack(t) for t in sm]

    dmod_all = gath[:, :12, :].reshape(NDEV, 2, 6 * D)
    dmod_sl = jnp.transpose(lax.dynamic_slice(dmod_all, (0, 0, me * 768), (NDEV, 2, 768)), (1, 0, 2))
    ada = _adamw_w_ada(cact.T, dmod_sl, w_ada, m_w_ada, v_w_ada)

    big = {}
    for kind, (w, m, v) in (("down", (w_down, m_w_down, v_w_down)), ("up", up_t),
                            ("out", (w_out, m_w_out, v_w_out)), ("in", (w_in, m_w_in, v_w_in))):
        recv0, recv1 = _exchange_wait([kind, kind], False, [sent[kind, 0], sent[kind, 1]], tok, f"grads_wait_{kind}")
        big[kind] = _adamw_big(recv0, recv1, w, m, v)
        tok = big[kind][0]
    big["up"] = [jnp.transpose(t, (0, 2, 1)) for t in big["up"]]

    def small_out(j, name):
        t = sm[j][name]
        if name == "b_ada":
            return t.reshape(2, 6 * D)
        if name == "rel_bias":
            return t.reshape(2, 8, 257)
        if name in ("g_a", "g_b"):
            return t.reshape(2, WG)
        if name == "conv_b":
            return t.reshape(2, 2 * DFF)
        if name == "final_g":
            return t.reshape(D)
        t = t.reshape(2, NDEV, 3, GU)
        return lax.dynamic_index_in_dim(t, me, axis=1, keepdims=False)

    def group(j):
        return (ada[j], small_out(j, "b_ada"), big["in"][j], small_out(j, "rel_bias"), small_out(j, "g_a"),
                small_out(j, "g_b"), big["out"][j], big["up"][j], small_out(j, "conv_w"), small_out(j, "conv_b"),
                big["down"][j], small_out(j, "final_g"))

    return (loss, grad_x, *group(0), *group(1), *group(2), *group(3))
```

```python
import jax
import jax.numpy as jnp
from jax import lax
from jax.experimental import pallas as pl
from jax.experimental.pallas import tpu as pltpu

F32, BF16 = jnp.float32, jnp.bfloat16
MESH_ID = pl.DeviceIdType.MESH
NDEV = 8
D = 1024
HD = 64
WG = 512
NT = 6
GU = 704
DFF = 2816
CHUNK, NPREV, REL_CLIP = 64, 8, 128
BAND = (NPREV + 1) * CHUNK
EPS = 1e-6
NEG = -1e30
TQA = 256
KWA = TQA + NPREV * CHUNK
TABW = 1024
TAB0 = TABW - KWA
N_FAR = TAB0 + NPREV * CHUNK - REL_CLIP + 1
N_NEAR = REL_CLIP + CHUNK - 1
TQB, TKB = 512, 128
KSTEP = 4
LR, B1, B2, AEPS, WD, STEP = 0.001, 0.9, 0.999, 1e-08, 0.01, 10
VMEM_MB = 56


def _call(body, **kw):
    return pl.pallas_call(body, **kw)


def _call_after(body, after, **kw):
    n_in = len(kw["in_specs"])
    kw["in_specs"] = list(kw["in_specs"]) + [pl.BlockSpec(memory_space=pl.ANY)]

    def tied(*refs):
        body(*refs[:n_in], *refs[n_in + 1:])

    call = _call(tied, **kw)
    return lambda *args: call(*args, after)


def _params(n_axes):
    return pltpu.CompilerParams(dimension_semantics=("arbitrary",) * n_axes, vmem_limit_bytes=VMEM_MB << 20)


def _dot(a, b):
    return jnp.dot(a, b, preferred_element_type=F32)


def _dot_nt(a, b):
    return lax.dot_general(a, b, (((1,), (1,)), ((), ())), preferred_element_type=F32)


def _dot_tn(a, b):
    return lax.dot_general(a, b, (((0,), (0,)), ((), ())), preferred_element_type=F32)


def _dot2(x, u):
    hi = x.astype(BF16)
    lo = (x - hi.astype(F32)).astype(BF16)
    return _dot(hi, u) + _dot(lo, u)


def _dot2_deep(x, uu):
    hi = x.astype(BF16)
    lo = (x - hi.astype(F32)).astype(BF16)
    return _dot(jnp.concatenate([hi, lo], axis=1), uu)


def _eye(n):
    i = lax.broadcasted_iota(jnp.int32, (n, n), 0)
    j = lax.broadcasted_iota(jnp.int32, (n, n), 1)
    return jnp.where(i == j, 1.0, 0.0).astype(BF16)


def _transpose_bf16(x, eye):
    return _dot_nt(eye, x).astype(BF16)


def _transpose_f32(x, eye):
    hi = x.astype(BF16)
    lo = (x - hi.astype(F32)).astype(BF16)
    return _dot_nt(eye, hi) + _dot_nt(eye, lo)


def _rms(x):
    return lax.rsqrt(jnp.mean(x * x, axis=-1, keepdims=True) + EPS)


def _rms_bwd(dn, n, r):
    return r * (dn - n * jnp.mean(dn * n, axis=-1, keepdims=True))


def _colsum(x):
    return jnp.sum(x, axis=0, keepdims=True)


def _sigmoid(x):
    return 1.0 / (1.0 + jnp.exp(-x))


def _sds(shape, dtype):
    return jax.ShapeDtypeStruct(shape, dtype)


def _place():
    x, y, c = lax.axis_index("x"), lax.axis_index("y"), lax.axis_index("c")
    return x, y, c, 4 * x + 2 * y + c


def _peer(x, y, c, k):
    px = 1 - x if k & 4 else x
    py = 1 - y if k & 2 else y
    pc = 1 - c if k & 1 else c
    return (px, py, pc), 4 * px + 2 * py + pc


def _small_allgather(v, name):
    rows, cols = v.shape

    def body(v_ref, out_ref, send_sems, recv_sems, local_sem):
        x, y, c, me = _place()
        mine = pltpu.make_async_copy(v_ref, out_ref.at[me], local_sem)
        mine.start()
        sends = []
        for k in range(1, NDEV):
            peer, _ = _peer(x, y, c, k)
            cp = pltpu.make_async_remote_copy(v_ref, out_ref.at[me], send_sems.at[k - 1], recv_sems.at[k - 1],
                                              device_id=peer, device_id_type=MESH_ID)
            cp.start()
            sends.append(cp)
        for k in range(1, NDEV):
            peer, pidx = _peer(x, y, c, k)
            pltpu.make_async_remote_copy(v_ref, out_ref.at[pidx], send_sems.at[k - 1], recv_sems.at[k - 1],
                                         device_id=peer, device_id_type=MESH_ID).wait_recv()
        for cp in sends:
            cp.wait_send()
        mine.wait()

    return _call(
        body, name=name,
        out_shape=_sds((NDEV, rows, cols), F32),
        in_specs=[pl.BlockSpec(memory_space=pltpu.VMEM)],
        out_specs=pl.BlockSpec(memory_space=pltpu.VMEM),
        scratch_shapes=[pltpu.SemaphoreType.DMA((NDEV - 1,)), pltpu.SemaphoreType.DMA((NDEV - 1,)),
                        pltpu.SemaphoreType.DMA],
    )(v)


def _shard_view(ref, kind, p):
    if kind == "in":
        return ref.at[:, pl.ds(pl.multiple_of(p * 384, 128), 384)]
    if kind == "out":
        return ref.at[pl.ds(pl.multiple_of(p * 128, 128), 128), :]
    if kind == "up":
        return ref.at[p]
    if kind == "down":
        return ref.at[pl.ds(pl.multiple_of(p * 352, 16), 352), :]
    raise ValueError(kind)


_KINDS = ("in", "out", "up", "down")
_FULL_SHAPES = {"in": (D, 3 * D), "out": (D, D), "up": (NDEV, GU, D), "down": (DFF, D)}
_SHARD_SHAPES = {"in": (D, 384), "out": (128, D), "up": (GU, D), "down": (352, D)}


_HBM = pl.BlockSpec(memory_space=pltpu.HBM)
_SEM = pl.BlockSpec(memory_space=pltpu.SEMAPHORE)
_EFFECT = pltpu.SideEffectType.DATAFLOW_SIDE_EFFECTING
_SEM_SHAPES = (pltpu.SemaphoreType.DMA((NDEV - 1,)), pltpu.SemaphoreType.DMA((NDEV - 1,)), pltpu.SemaphoreType.DMA(()))


def _hbm(a):
    return pltpu.with_memory_space_constraint(a, pltpu.HBM)


def _exchange_copies(kind, gather, src, land, sems):
    send_sems, recv_sems, local_sem = sems
    x, y, c, me = _place()

    def ends(p_dst, p_from):
        if gather:
            return src, _shard_view(land, kind, me), _shard_view(land, kind, p_from)
        return _shard_view(src, kind, p_dst), land.at[me], land.at[p_from]

    s_me, d_me, _ = ends(me, me)
    local = pltpu.make_async_copy(s_me, d_me, local_sem)
    sends, arrivals = [], []
    for k in range(1, NDEV):
        peer, pidx = _peer(x, y, c, k)
        s_k, d_k, from_k = ends(pidx, pidx)
        sends.append(pltpu.make_async_remote_copy(s_k, d_k, send_sems.at[k - 1], recv_sems.at[k - 1],
                                                  device_id=peer, device_id_type=MESH_ID))
        arrivals.append(pltpu.make_async_remote_copy(s_k, from_k, send_sems.at[k - 1], recv_sems.at[k - 1],
                                                     device_id=peer, device_id_type=MESH_ID))
    return local, sends, arrivals


def _exchange_start(kinds, gather, srcs, name, with_token=False):
    n = len(kinds)
    lands = [lax.empty(_FULL_SHAPES[kd] if gather else (NDEV,) + _SHARD_SHAPES[kd], BF16) for kd in kinds]

    def body(*refs):
        ins, sems = refs[:2 * n], refs[2 * n:5 * n]
        for j, kd in enumerate(kinds):
            local, sends, _ = _exchange_copies(kd, gather, ins[j], ins[n + j], sems[3 * j:3 * j + 3])
            local.start()
            for cp in sends:
                cp.start()
        if with_token:
            token = refs[7 * n]
            token[...] = jnp.zeros_like(token)

    out_shape = list(_SEM_SHAPES) * n
    out_shape += [pltpu.HBM(a.shape, a.dtype) for a in srcs] + [pltpu.HBM(a.shape, a.dtype) for a in lands]
    out_specs = [_SEM] * (3 * n) + [_HBM] * (2 * n)
    if with_token:
        out_shape.append(_sds((8, 128), F32))
        out_specs.append(pl.BlockSpec(memory_space=pltpu.VMEM))
    outs = _call(
        body, name=name, out_shape=out_shape,
        in_specs=[_HBM] * (2 * n), out_specs=out_specs,
        input_output_aliases={i: 3 * n + i for i in range(2 * n)},
        compiler_params=pltpu.CompilerParams(has_side_effects=_EFFECT),
    )(*[_hbm(a) for a in srcs], *[_hbm(a) for a in lands])
    per_tensor = [(tuple(outs[3 * j:3 * j + 3]), outs[3 * n + j], outs[4 * n + j]) for j in range(n)]
    return (per_tensor, outs[5 * n]) if with_token else per_tensor


def _exchange_wait(kinds, gather, started, after, name):
    n = len(kinds)

    def body(*refs):
        ins, sems = refs[:2 * n], refs[2 * n:5 * n]
        for j, kd in enumerate(kinds):
            local, sends, arrivals = _exchange_copies(kd, gather, ins[j], ins[n + j], sems[3 * j:3 * j + 3])
            local.wait()
            for cp in arrivals:
                cp.wait_recv()
            for cp in sends:
                cp.wait_send()

    srcs = [st[1] for st in started]
    lands = [st[2] for st in started]
    sems = [sm for st in started for sm in st[0]]
    outs = _call(
        body, name=name,
        out_shape=[pltpu.HBM(a.shape, a.dtype) for a in srcs] + [pltpu.HBM(a.shape, a.dtype) for a in lands],
        in_specs=[_HBM] * (2 * n) + [_SEM] * (3 * n) + [pl.BlockSpec(memory_space=pl.ANY)],
        out_specs=[_HBM] * (2 * n),
        input_output_aliases={i: i for i in range(2 * n)},
        compiler_params=pltpu.CompilerParams(has_side_effects=_EFFECT),
    )(*srcs, *lands, *sems, after)
    return outs[n:]


def _mod_fwd(c_all, w_ada, b_sl):
    def body(c_ref, w_ref, b_ref, mod_ref, cact_ref):
        cv = c_ref[...]
        ca = cv * _sigmoid(cv)
        cact_ref[...] = ca
        mod_ref[0] = _dot(ca.astype(BF16), w_ref[0].astype(BF16)) + b_ref[0]

    return _call(
        body, name="mod_fwd", grid=(2,),
        in_specs=[pl.BlockSpec((NDEV, D), lambda l: (0, 0)), pl.BlockSpec((1, D, 768), lambda l: (l, 0, 0)),
                  pl.BlockSpec((1, 1, 768), lambda l: (l, 0, 0))],
        out_specs=[pl.BlockSpec((1, NDEV, 768), lambda l: (l, 0, 0)), pl.BlockSpec((NDEV, D), lambda l: (0, 0))],
        out_shape=[_sds((2, NDEV, 768), F32), _sds((NDEV, D), F32)],
        compiler_params=_params(1),
    )(c_all, w_ada, b_sl)


def _nm_matmul(x, shift, scale, w, *, two_d, n, groups, out_dtype, tm, name):
    s = x.shape[0]

    def body(x_ref, sh_ref, sc_ref, w_ref, h_ref, o_ref):
        @pl.when(pl.program_id(1) == 0)
        def _():
            xv = x_ref[...]
            h_ref[...] = ((xv * _rms(xv)) * (1.0 + sc_ref[...]) + sh_ref[...]).astype(BF16)
        if two_d:
            o_ref[0] = _dot(h_ref[...], w_ref[...]).astype(out_dtype)
        else:
            o_ref[0] = _dot_nt(h_ref[...], w_ref[0]).astype(out_dtype)

    vec = pl.BlockSpec((1, D), lambda i, g: (0, 0))
    w_spec = pl.BlockSpec((D, n), lambda i, g: (0, g)) if two_d else pl.BlockSpec((1, n, D), lambda i, g: (g, 0, 0))
    return _call(
        body, name=name, grid=(s // tm, groups),
        in_specs=[pl.BlockSpec((tm, D), lambda i, g: (i, 0)), vec, vec, w_spec],
        out_specs=[pl.BlockSpec((tm, D), lambda i, g: (i, 0)), pl.BlockSpec((1, tm, n), lambda i, g: (g, i, 0))],
        out_shape=[_sds((s, D), BF16), _sds((groups, s, n), out_dtype)],
        compiler_params=_params(2),
    )(x, shift, scale, w)


def _bias_table(rel_bias):
    far = jnp.broadcast_to(rel_bias[:, 2 * REL_CLIP:], (NDEV, N_FAR))
    near = rel_bias[:, 2 * REL_CLIP - 1:REL_CLIP - CHUNK:-1]
    fpad = jnp.concatenate([far, near, jnp.zeros((NDEV, TABW - N_FAR - N_NEAR), F32)], axis=1)

    def body(f_ref, o_ref):
        t = pltpu.roll(jnp.broadcast_to(f_ref[0], (TQA, TABW)), 0, 1, stride=1, stride_axis=0)[:, TAB0:]
        rows = lax.broadcasted_iota(jnp.int32, (TQA, KWA), 0)
        cols = lax.broadcasted_iota(jnp.int32, (TQA, KWA), 1)
        first = jnp.bitwise_and(rows, -CHUNK)
        o_ref[0] = jnp.where((cols >= first) & (cols < first + BAND), t, NEG)

    return _call(
        body, name="bias_table", grid=(NDEV,),
        in_specs=[pl.BlockSpec((1, 1, TABW), lambda h: (h, 0, 0))],
        out_specs=pl.BlockSpec((1, TQA, KWA), lambda h: (h, 0, 0)),
        out_shape=_sds((NDEV, TQA, KWA), F32),
        compiler_params=_params(1),
    )(fpad.reshape(NDEV, 1, TABW))


def _attn_a_cases(qi, tile):
    @pl.when(qi == 0)
    def _():
        tile(TQA, 2 * TQA, 0)

    @pl.when(qi == 1)
    def _():
        tile(2 * TQA, TQA, 0)

    @pl.when(qi >= 2)
    def _():
        tile(KWA, 0, pl.multiple_of((qi - 2) * TQA, TQA))


HA = 8
WA = HA * HD


def _attn_a_specs(s):
    q_spec = pl.BlockSpec((1, TQA, WA), lambda hp, qi: (0, qi, hp))
    k_spec = pl.BlockSpec((1, s, WA), lambda hp, qi: (1, 0, hp))
    v_spec = pl.BlockSpec((1, s, WA), lambda hp, qi: (2, 0, hp))
    b_spec = pl.BlockSpec((HA, TQA, KWA), lambda hp, qi: (hp, 0, 0))
    return q_spec, k_spec, v_spec, b_spec


def _attn_a_fwd(proj, bias_tab):
    s = proj.shape[1]

    def body(q_ref, k_ref, v_ref, b_ref, o_ref):
        def tile(nk, off, kstart):
            for h in range(HA):
                sl = slice(HD * h, HD * (h + 1))
                q = q_ref[0, :, sl]
                k = k_ref[0, pl.ds(kstart, nk), sl]
                v = v_ref[0, pl.ds(kstart, nk), sl]
                sc = _dot_nt(q, k) * 0.125 + b_ref[h, :, off:off + nk]
                p = jnp.exp(sc - jnp.max(sc, axis=-1, keepdims=True))
                den = jnp.sum(p, axis=-1, keepdims=True)
                o_ref[:, sl] = _dot(p.astype(BF16), v) / den

        _attn_a_cases(pl.program_id(1), tile)

    q_spec, k_spec, v_spec, b_spec = _attn_a_specs(s)
    return _call(
        body, name="attn_a_fwd", grid=(WG // WA, s // TQA),
        in_specs=[q_spec, k_spec, v_spec, b_spec],
        out_specs=pl.BlockSpec((TQA, WA), lambda hp, qi: (qi, hp)),
        out_shape=_sds((s, WG), F32),
        compiler_params=_params(2),
    )(proj, proj, proj, bias_tab)


def _sb_terms(lg):
    sp = jnp.maximum(lg, 0.0) + jnp.log(1.0 + jnp.exp(-jnp.abs(lg)))
    return lg - sp, sp


def _attn_b_specs(s):
    q_spec = pl.BlockSpec((1, TQB, 128), lambda hp, qi: (3, qi, hp))
    k_spec = pl.BlockSpec((1, s, 128), lambda hp, qi: (4, 0, hp))
    v_spec = pl.BlockSpec((1, s, 128), lambda hp, qi: (5, 0, hp))
    return q_spec, k_spec, v_spec


def _attn_b_fwd(proj):
    s = proj.shape[1]
    ndiag = TQB // TKB

    def body(q_ref, k_ref, v_ref, o_ref, ls_ref):
        q0 = pl.program_id(1) * TQB
        rows = lax.broadcasted_iota(jnp.int32, (TQB, TKB), 0)
        cols = lax.broadcasted_iota(jnp.int32, (TQB, TKB), 1)
        uj = lax.broadcasted_iota(jnp.int32, (TKB, TKB), 0)
        us = lax.broadcasted_iota(jnp.int32, (TKB, TKB), 1)
        suffix = jnp.where(uj >= us, 1.0, 0.0).astype(BF16)
        heads = [slice(HD * h, HD * (h + 1)) for h in range(2)]
        qs = [(q_ref[0, :, sl].astype(F32) * 0.125).astype(BF16) for sl in heads]

        def tile(h, kstart, diag, carry, acc):
            k = k_ref[0, pl.ds(kstart, TKB), heads[h]]
            v = v_ref[0, pl.ds(kstart, TKB), heads[h]]
            lb, sp = _sb_terms(_dot_nt(qs[h], k))
            if diag is not None:
                strict = rows > cols + diag * TKB
                sp = jnp.where(strict, sp, 0.0)
            csum = _dot2(sp, suffix) + carry
            w = jnp.exp(lb - csum + sp)
            if diag is not None:
                w = jnp.where(strict, w, 0.0)
            return csum[:, 0:1], acc + _dot(w.astype(BF16), v)

        state = [jnp.zeros((TQB, 1), F32), jnp.zeros((TQB, HD), F32)] * 2
        for d in range(ndiag - 1, -1, -1):
            for h in range(2):
                state[2 * h:2 * h + 2] = tile(h, pl.multiple_of(q0 + d * TKB, TKB), d, *state[2 * h:2 * h + 2])
        nsteps = q0 // (KSTEP * TKB)

        def step(i, st):
            st = list(st)
            base = (nsteps - 1 - i) * (KSTEP * TKB)
            for sub in range(KSTEP - 1, -1, -1):
                for h in range(2):
                    st[2 * h:2 * h + 2] = tile(h, pl.multiple_of(base + sub * TKB, TKB), None, *st[2 * h:2 * h + 2])
            return tuple(st)

        state = lax.fori_loop(0, nsteps, step, tuple(state))
        for h in range(2):
            o_ref[:, heads[h]] = state[2 * h + 1]
            ls_ref[0, :, heads[h]] = jnp.broadcast_to(state[2 * h], (TQB, HD))

    q_spec, k_spec, v_spec = _attn_b_specs(s)
    return _call(
        body, name="attn_b_fwd", grid=(4, s // TQB),
        in_specs=[q_spec, k_spec, v_spec],
        out_specs=[pl.BlockSpec((TQB, 128), lambda hp, qi: (qi, hp)),
                   pl.BlockSpec((1, TQB, 128), lambda hp, qi: (hp, qi, 0))],
        out_shape=[_sds((s, WG), F32), _sds((4, s, 128), F32)],
        compiler_params=_params(2),
    )(proj, proj, proj)


def _mix_out(oa, ob, g, w_out, x, gate, tm):
    s = x.shape[0]

    def body(oa_ref, ob_ref, g_ref, w_ref, x_ref, gate_ref, nab_ref, mixed_ref, x2_ref):
        a, b = oa_ref[...], ob_ref[...]
        nab_ref[:, :WG] = (a * _rms(a) * g_ref[:, :WG]).astype(BF16)
        nab_ref[:, WG:] = (b * _rms(b) * g_ref[:, WG:]).astype(BF16)
        mixed = _dot(nab_ref[...], w_ref[...])
        mixed_ref[...] = mixed
        x2_ref[...] = x_ref[...] + gate_ref[...] * mixed

    row = pl.BlockSpec((tm, D), lambda i: (i, 0))
    half = pl.BlockSpec((tm, WG), lambda i: (i, 0))
    vec = pl.BlockSpec((1, D), lambda i: (0, 0))
    return _call(
        body, name="mix_out", grid=(s // tm,),
        in_specs=[half, half, vec, pl.BlockSpec((D, D), lambda i: (0, 0)), row, vec],
        out_specs=[row, row, row],
        out_shape=[_sds((s, D), BF16), _sds((s, D), F32), _sds((s, D), F32)],
        compiler_params=_params(1),
    )(oa, ob, g, w_out, x, gate)


HALO = 16


def _conv_taps(u, halo, first, tm):
    keep = jnp.where(first, 0.0, 1.0)
    window = jnp.concatenate([halo[HALO - 8:HALO, :] * keep, u[0:8, :]], axis=0)
    um1 = jnp.concatenate([pltpu.roll(window, 1, 0)[8:16, :], pltpu.roll(u, 1, 0)[8:, :]], axis=0)
    um2 = jnp.concatenate([pltpu.roll(window, 2, 0)[8:16, :], pltpu.roll(u, 2, 0)[8:, :]], axis=0)
    return um1, um2


def _conv_specs(tm):
    u_spec = pl.BlockSpec((2, 1, tm, GU), lambda p, i: (0, p, i, 0))
    halo_spec = pl.BlockSpec((2, 1, HALO, GU), lambda p, i: (0, p, jnp.maximum(i * (tm // HALO) - 1, 0), 0))
    cw_spec = pl.BlockSpec((2, 1, 3, GU), lambda p, i: (0, p, 0, 0))
    cb_spec = pl.BlockSpec((2, 1, 1, GU), lambda p, i: (0, p, 0, 0))
    return u_spec, halo_spec, cw_spec, cb_spec


def _conv_act(u, conv_w, conv_b, tm):
    s = u.shape[2]

    def body(u_ref, halo_ref, cw_ref, cb_ref, a_ref):
        first = pl.program_id(1) == 0
        ys = []
        for side in range(2):
            uv = u_ref[side, 0].astype(F32)
            um1, um2 = _conv_taps(uv, halo_ref[side, 0].astype(F32), first, tm)
            cw = cw_ref[side, 0]
            ys.append(cw[2:3] * uv + cw[1:2] * um1 + cw[0:1] * um2 + cb_ref[side, 0])
        a_ref[0] = (ys[0] * _sigmoid(ys[0]) * ys[1]).astype(BF16)

    u_spec, halo_spec, cw_spec, cb_spec = _conv_specs(tm)
    return _call(
        body, name="conv_act", grid=(4, s // tm),
        in_specs=[u_spec, halo_spec, cw_spec, cb_spec],
        out_specs=pl.BlockSpec((1, tm, GU), lambda p, i: (p, i, 0)),
        out_shape=_sds((4, s, GU), BF16),
        compiler_params=_params(2),
    )(u, u, conv_w, conv_b)


def _down(a, w_down, x2, gate, tm):
    s = x2.shape[0]

    def body(a_ref, w_ref, x_ref, gate_ref, ffn_ref, x3_ref):
        p = pl.program_id(1)
        part = _dot(a_ref[0], w_ref[0])

        @pl.when(p == 0)
        def _():
            ffn_ref[...] = part

        @pl.when(p > 0)
        def _():
            ffn_ref[...] += part

        @pl.when(p == 3)
        def _():
            x3_ref[...] = x_ref[...] + gate_ref[...] * ffn_ref[...]

    row = pl.BlockSpec((tm, D), lambda i, p: (i, 0))
    return _call(
        body, name="down", grid=(s // tm, 4),
        in_specs=[pl.BlockSpec((1, tm, GU), lambda i, p: (p, i, 0)), pl.BlockSpec((1, GU, D), lambda i, p: (p, 0, 0)),
                  row, pl.BlockSpec((1, D), lambda i, p: (0, 0))],
        out_specs=[row, row],
        out_shape=[_sds((s, D), F32), _sds((s, D), F32)],
        compiler_params=_params(2),
    )(a, w_down, x2, gate)


def _final_loss(x, g, target, tm):
    s = x.shape[0]

    def body(x_ref, g_ref, t_ref, loss_ref, dx_ref, dg_ref):
        @pl.when(pl.program_id(0) == 0)
        def _():
            loss_ref[...] = jnp.zeros_like(loss_ref)
            dg_ref[...] = jnp.zeros_like(dg_ref)
        xv = x_ref[...]
        r = _rms(xv)
        nrm = xv * r
        err = nrm * g_ref[...] - t_ref[...]
        loss_ref[...] += (0.5 / D) * jnp.sum(jnp.sum(err * err, axis=-1, keepdims=True), axis=0, keepdims=True)
        dy = err * (1.0 / D)
        dg_ref[...] += _colsum(dy * nrm)
        dx_ref[...] = _rms_bwd(dy * g_ref[...], nrm, r)

    row = pl.BlockSpec((tm, D), lambda i: (i, 0))
    vec = pl.BlockSpec((1, D), lambda i: (0, 0))
    return _call(
        body, name="final_loss", grid=(s // tm,),
        in_specs=[row, vec, row],
        out_specs=[pl.BlockSpec((1, 1), lambda i: (0, 0)), row, vec],
        out_shape=[_sds((1, 1), F32), _sds((s, D), F32), _sds((1, D), F32)],
        compiler_params=_params(1),
    )(x, g, target)


def _down_bwd(dx3, gate, ffn, w_down, tm):
    s = dx3.shape[0]

    def body(dx_ref, gate_ref, ffn_ref, w_ref, dgate_ref, dff_ref, da_ref):
        i, p = pl.program_id(0), pl.program_id(1)

        @pl.when((i == 0) & (p == 0))
        def _():
            dgate_ref[...] = jnp.zeros_like(dgate_ref)

        @pl.when(p == 0)
        def _():
            dxv = dx_ref[...]
            dgate_ref[...] += _colsum(dxv * ffn_ref[...])
            dff_ref[...] = (dxv * gate_ref[...]).astype(BF16)

        da_ref[0] = _dot_nt(dff_ref[...], w_ref[0]).astype(BF16)

    row = pl.BlockSpec((tm, D), lambda i, p: (i, 0))
    vec = pl.BlockSpec((1, D), lambda i, p: (0, 0))
    return _call(
        body, name="down_bwd", grid=(s // tm, 4),
        in_specs=[row, vec, row, pl.BlockSpec((1, GU, D), lambda i, p: (p, 0, 0))],
        out_specs=[vec, row, pl.BlockSpec((1, tm, GU), lambda i, p: (p, i, 0))],
        out_shape=[_sds((1, D), F32), _sds((s, D), BF16), _sds((4, s, GU), BF16)],
        compiler_params=_params(2),
    )(dx3, gate, ffn, w_down)


def _conv_act_bwd(u, conv_w, conv_b, da, tm, after):
    s = u.shape[2]

    def body(u_ref, halo_ref, cw_ref, cb_ref, da_ref, dy_ref, dcw_ref, dcb_ref):
        first = pl.program_id(1) == 0

        @pl.when(first)
        def _():
            dcw_ref[...] = jnp.zeros_like(dcw_ref)
            dcb_ref[...] = jnp.zeros_like(dcb_ref)

        taps, ys = [], []
        for side in range(2):
            uv = u_ref[side, 0].astype(F32)
            um1, um2 = _conv_taps(uv, halo_ref[side, 0].astype(F32), first, tm)
            cw = cw_ref[side, 0]
            taps.append((um2, um1, uv))
            ys.append(cw[2:3] * uv + cw[1:2] * um1 + cw[0:1] * um2 + cb_ref[side, 0])
        dav = da_ref[0].astype(F32)
        sg = _sigmoid(ys[0])
        dys = (dav * ys[1] * (sg * (1.0 + ys[0] * (1.0 - sg))), dav * (ys[0] * sg))
        for side in range(2):
            dy_ref[side, 0] = dys[side].astype(BF16)
            dcb_ref[side, 0] += _colsum(dys[side])
            for j in range(3):
                dcw_ref[side, 0, j:j + 1, :] += _colsum(dys[side] * taps[side][j])

    u_spec, halo_spec, cw_spec, cb_spec = _conv_specs(tm)
    return _call_after(
        body, after, name="conv_act_bwd", grid=(4, s // tm),
        in_specs=[u_spec, halo_spec, cw_spec, cb_spec, pl.BlockSpec((1, tm, GU), lambda p, i: (p, i, 0))],
        out_specs=[u_spec, cw_spec, cb_spec],
        out_shape=[_sds((2, 4, s, GU), BF16), _sds((2, 4, 3, GU), F32), _sds((2, 4, 1, GU), F32)],
        compiler_params=_params(2),
    )(u, u, conv_w, conv_b, da)


def _conv_transpose(dy, conv_w, tm):
    s = dy.shape[1]
    nt = s // tm

    def body(dy_ref, halo_ref, cw_ref, du_ref):
        keep = jnp.where(pl.program_id(1) == nt - 1, 0.0, 1.0)
        dv = dy_ref[0].astype(F32)
        window = jnp.concatenate([dv[tm - 8:, :], halo_ref[0, 0:8, :].astype(F32) * keep], axis=0)
        dp1 = jnp.concatenate([pltpu.roll(dv, tm - 1, 0)[:tm - 8, :], pltpu.roll(window, 15, 0)[0:8, :]], axis=0)
        dp2 = jnp.concatenate([pltpu.roll(dv, tm - 2, 0)[:tm - 8, :], pltpu.roll(window, 14, 0)[0:8, :]], axis=0)
        cw = cw_ref[0]
        du_ref[0] = (cw[2:3] * dv + cw[1:2] * dp1 + cw[0:1] * dp2).astype(BF16)

    blk = pl.BlockSpec((1, tm, GU), lambda g, i: (g, i, 0))
    return _call(
        body, name="conv_transpose", grid=(NDEV, nt),
        in_specs=[blk, pl.BlockSpec((1, HALO, GU),
                                    lambda g, i: (g, jnp.minimum((i + 1) * (tm // HALO), s // HALO - 1), 0)),
                  pl.BlockSpec((1, 3, GU), lambda g, i: (g, 0, 0))],
        out_specs=blk,
        out_shape=_sds((NDEV, s, GU), BF16),
        compiler_params=_params(2),
    )(dy, dy, conv_w)


def _wgrad(a3, b3, *, tk, name):
    ga, s, ka = a3.shape
    gb, _, nb = b3.shape
    groups = max(ga, gb)
    nk = s // tk

    def body(a_ref, b_ref, o_ref, acc):
        k = pl.program_id(1)

        @pl.when(k == 0)
        def _():
            acc[...] = jnp.zeros_like(acc)

        acc[...] += _dot_tn(a_ref[0], b_ref[0])

        @pl.when(k == nk - 1)
        def _():
            o_ref[0] = acc[...].astype(BF16)

    a_spec = pl.BlockSpec((1, tk, ka), (lambda g, k: (g, k, 0)) if ga > 1 else (lambda g, k: (0, k, 0)))
    b_spec = pl.BlockSpec((1, tk, nb), (lambda g, k: (g, k, 0)) if gb > 1 else (lambda g, k: (0, k, 0)))
    return _call(
        body, name=name, grid=(groups, nk),
        in_specs=[a_spec, b_spec], out_specs=pl.BlockSpec((1, ka, nb), lambda g, k: (g, 0, 0)),
        out_shape=_sds((groups, ka, nb), BF16),
        scratch_shapes=[pltpu.VMEM((ka, nb), F32)],
        compiler_params=_params(2),
    )(a3, b3)


def _wgrad_in(h1, dparts, tk, after):
    s = h1.shape[0]
    nk = s // tk

    def body(a_ref, *refs):
        d_refs, o_ref, acc = refs[:NT], refs[NT], refs[NT + 1]
        k = pl.program_id(0)

        @pl.when(k == 0)
        def _():
            acc[...] = jnp.zeros_like(acc)

        for j in range(NT):
            acc[:, WG * j:WG * (j + 1)] += _dot_tn(a_ref[...], d_refs[j][...])

        @pl.when(k == nk - 1)
        def _():
            o_ref[...] = acc[...].astype(BF16)

    kw = dict(
        name="wgrad_in", grid=(nk,),
        in_specs=[pl.BlockSpec((tk, D), lambda k: (k, 0))] + [pl.BlockSpec((tk, WG), lambda k: (k, 0))] * NT,
        out_specs=pl.BlockSpec((D, NT * WG), lambda k: (0, 0)),
        out_shape=_sds((D, NT * WG), BF16),
        scratch_shapes=[pltpu.VMEM((D, NT * WG), F32)],
        compiler_params=_params(1),
    )
    call = _call(body, **kw) if after is None else _call_after(body, after, **kw)
    return call(h1, *dparts)


def _dgrad_in(dparts, w, x_in, scale, dx_up, tm, after):
    s = x_in.shape[0]

    def body(*refs):
        d_refs = refs[:NT]
        w_ref, x_ref, sc_ref, up_ref, dx_ref, dsc_ref, dsh_ref = refs[NT:]

        @pl.when(pl.program_id(0) == 0)
        def _():
            dsc_ref[...] = jnp.zeros_like(dsc_ref)
            dsh_ref[...] = jnp.zeros_like(dsh_ref)

        dh = _dot_nt(d_refs[0][...], w_ref[:, 0:WG])
        for j in range(1, NT):
            dh = dh + _dot_nt(d_refs[j][...], w_ref[:, WG * j:WG * (j + 1)])
        xv = x_ref[...]
        r = _rms(xv)
        nrm = xv * r
        dsh_ref[...] += _colsum(dh)
        dsc_ref[...] += _colsum(dh * nrm)
        dx_ref[...] = up_ref[...] + _rms_bwd(dh * (1.0 + sc_ref[...]), nrm, r)

    row = pl.BlockSpec((tm, D), lambda i: (i, 0))
    vec = pl.BlockSpec((1, D), lambda i: (0, 0))
    return _call_after(
        body, after, name="dgrad_in", grid=(s // tm,),
        in_specs=[pl.BlockSpec((tm, WG), lambda i: (i, 0))] * NT + [pl.BlockSpec((D, NT * WG), lambda i: (0, 0)), row, vec, row],
        out_specs=[row, vec, vec],
        out_shape=[_sds((s, D), F32), _sds((1, D), F32), _sds((1, D), F32)],
        compiler_params=_params(1),
    )(*dparts, w, x_in, scale, dx_up)


def _dgrad_norm_bwd(d3, w, x_in, scale, dx_up, *, tm, name, after):
    groups, s, n = d3.shape

    def body(d_ref, w_ref, x_ref, sc_ref, up_ref, dx_ref, dsc_ref, dsh_ref, acc):
        i, g = pl.program_id(0), pl.program_id(1)

        @pl.when((i == 0) & (g == 0))
        def _():
            dsc_ref[...] = jnp.zeros_like(dsc_ref)
            dsh_ref[...] = jnp.zeros_like(dsh_ref)

        part = _dot(d_ref[0], w_ref[0])

        @pl.when(g == 0)
        def _():
            acc[...] = part

        @pl.when(g > 0)
        def _():
            acc[...] += part

        @pl.when(g == groups - 1)
        def _():
            dh = acc[...]
            xv = x_ref[...]
            r = _rms(xv)
            nrm = xv * r
            dsh_ref[...] += _colsum(dh)
            dsc_ref[...] += _colsum(dh * nrm)
            dx_ref[...] = up_ref[...] + _rms_bwd(dh * (1.0 + sc_ref[...]), nrm, r)

    row = pl.BlockSpec((tm, D), lambda i, g: (i, 0))
    vec = pl.BlockSpec((1, D), lambda i, g: (0, 0))
    return _call_after(
        body, after, name=name, grid=(s // tm, groups),
        in_specs=[pl.BlockSpec((1, tm, n), lambda i, g: (g, i, 0)), pl.BlockSpec((1, n, D), lambda i, g: (g, 0, 0)),
                  row, vec, row],
        out_specs=[row, vec, vec],
        out_shape=[_sds((s, D), F32), _sds((1, D), F32), _sds((1, D), F32)],
        scratch_shapes=[pltpu.VMEM((tm, D), F32)],
        compiler_params=_params(2),
    )(d3, w, x_in, scale, dx_up)


def _mix_out_bwd(dx2, mixed, gate, w_out, oa, ob, g, tm):
    s = dx2.shape[0]

    def body(dx_ref, mixed_ref, gate_ref, w_ref, oa_ref, ob_ref, g_ref, dgate_ref, dm_ref, doa_ref, dob_ref, dg_ref):
        @pl.when(pl.program_id(0) == 0)
        def _():
            dgate_ref[...] = jnp.zeros_like(dgate_ref)
            dg_ref[...] = jnp.zeros_like(dg_ref)
        dxv = dx_ref[...]
        dgate_ref[...] += _colsum(dxv * mixed_ref[...])
        dm_ref[...] = (dxv * gate_ref[...]).astype(BF16)
        dnab = _dot_nt(dm_ref[...], w_ref[...])
        for o_ref, do_ref, sl in ((oa_ref, doa_ref, slice(0, WG)), (ob_ref, dob_ref, slice(WG, D))):
            ov = o_ref[...]
            r = _rms(ov)
            nrm = ov * r
            dn = dnab[:, sl]
            dg_ref[:, sl] += _colsum(dn * nrm)
            do_ref[...] = _rms_bwd(dn * g_ref[:, sl], nrm, r)

    row = pl.BlockSpec((tm, D), lambda i: (i, 0))
    half = pl.BlockSpec((tm, WG), lambda i: (i, 0))
    vec = pl.BlockSpec((1, D), lambda i: (0, 0))
    return _call(
        body, name="mix_out_bwd", grid=(s // tm,),
        in_specs=[row, row, vec, pl.BlockSpec((D, D), lambda i: (0, 0)), half, half, vec],
        out_specs=[vec, row, half, half, vec],
        out_shape=[_sds((1, D), F32), _sds((s, D), BF16), _sds((s, WG), F32), _sds((s, WG), F32), _sds((1, D), F32)],
        compiler_params=_params(1),
    )(dx2, mixed, gate, w_out, oa, ob, g)


def _attn_a_bwd(proj, bias_tab, d_oa, after):
    s = proj.shape[1]
    nq = s // TQA
    nkb = s // 128

    def body(q_ref, k_ref, v_ref, b_ref, do_ref, dq_ref, dk_ref, dv_ref, db_ref, dkt_acc, dvt_acc):
        qi = pl.program_id(1)

        @pl.when(qi == 0)
        def _():
            dkt_acc[...] = jnp.zeros_like(dkt_acc)
            dvt_acc[...] = jnp.zeros_like(dvt_acc)
            db_ref[...] = jnp.zeros_like(db_ref)

        heads = [slice(HD * h, HD * (h + 1)) for h in range(HA)]
        qs = [q_ref[0, :, sl] for sl in heads]
        dos = [do_ref[:, sl].astype(BF16) for sl in heads]
        qts = [_transpose_bf16(x, _eye(HD)) for x in qs]
        dots = [_transpose_bf16(x, _eye(HD)) for x in dos]

        def tile(nk, off, kstart):
            kb0 = kstart // 128
            for h in range(HA):
                sl = heads[h]
                k = k_ref[0, pl.ds(kstart, nk), sl]
                v = v_ref[0, pl.ds(kstart, nk), sl]
                sc = _dot_nt(qs[h], k) * 0.125 + b_ref[h, :, off:off + nk]
                p = jnp.exp(sc - jnp.max(sc, axis=-1, keepdims=True))
                p = p / jnp.sum(p, axis=-1, keepdims=True)
                dp = _dot_nt(dos[h], v)
                ds = p * (dp - jnp.sum(dp * p, axis=-1, keepdims=True))
                db_ref[h, :, off:off + nk] += ds
                dsb = (ds * 0.125).astype(BF16)
                dq_ref[:, sl] = _dot(dsb, k).astype(BF16)
                dkt = _dot(qts[h], dsb)
                dvt = _dot(dots[h], p.astype(BF16))
                for j in range(nk // 128):
                    dkt_acc[h * nkb + kb0 + j] += dkt[:, 128 * j:128 * (j + 1)]
                    dvt_acc[h * nkb + kb0 + j] += dvt[:, 128 * j:128 * (j + 1)]

        _attn_a_cases(qi, tile)

        @pl.when(qi == nq - 1)
        def _():
            eye = _eye(128)
            for h in range(HA):
                for kb in range(nkb):
                    rows_kb = slice(128 * kb, 128 * (kb + 1))
                    dk_ref[rows_kb, heads[h]] = _transpose_f32(dkt_acc[h * nkb + kb], eye).astype(BF16)
                    dv_ref[rows_kb, heads[h]] = _transpose_f32(dvt_acc[h * nkb + kb], eye).astype(BF16)

    q_spec, k_spec, v_spec, b_spec = _attn_a_specs(s)
    blk = pl.BlockSpec((TQA, WA), lambda hp, qi: (qi, hp))
    col = pl.BlockSpec((s, WA), lambda hp, qi: (0, hp))
    return _call_after(
        body, after, name="attn_a_bwd", grid=(WG // WA, nq),
        in_specs=[q_spec, k_spec, v_spec, b_spec, blk],
        out_specs=[blk, col, col, b_spec],
        out_shape=[_sds((s, WG), BF16), _sds((s, WG), BF16), _sds((s, WG), BF16), _sds((NDEV, TQA, KWA), F32)],
        scratch_shapes=[pltpu.VMEM((HA * nkb, HD, 128), F32), pltpu.VMEM((HA * nkb, HD, 128), F32)],
        compiler_params=_params(2),
    )(proj, proj, proj, bias_tab, d_oa)


def _attn_b_bwd(proj, lsum, d_ob):
    s = proj.shape[1]
    nq = s // TQB
    nkb = s // TKB
    ndiag = TQB // TKB

    def body(q_ref, k_ref, v_ref, ls_ref, do_ref, dq_ref, dk_ref, dv_ref, dkt_acc, dvt_acc):
        qi = pl.program_id(1)
        q0 = qi * TQB

        @pl.when(qi == 0)
        def _():
            dkt_acc[...] = jnp.zeros_like(dkt_acc)
            dvt_acc[...] = jnp.zeros_like(dvt_acc)

        rows = lax.broadcasted_iota(jnp.int32, (TQB, TKB), 0)
        cols = lax.broadcasted_iota(jnp.int32, (TQB, TKB), 1)
        uj = jnp.bitwise_and(lax.broadcasted_iota(jnp.int32, (2 * TKB, TKB), 0), TKB - 1)
        us = lax.broadcasted_iota(jnp.int32, (2 * TKB, TKB), 1)
        prefix = jnp.where(uj <= us, 1.0, 0.0).astype(BF16)
        heads = [slice(HD * h, HD * (h + 1)) for h in range(2)]
        qs = [(q_ref[0, :, sl].astype(F32) * 0.125).astype(BF16) for sl in heads]
        dos = [do_ref[:, sl].astype(BF16) for sl in heads]
        qts = [_transpose_bf16(x, _eye(HD)) for x in qs]
        dots = [_transpose_bf16(x, _eye(HD)) for x in dos]
        stots = [ls_ref[0, :, HD * h:HD * h + 1] for h in range(2)]
        qds = [jnp.concatenate([qs[h], dos[h]], axis=1) for h in range(2)]
        zkv = jnp.zeros((TKB, HD), BF16)

        def tile(h, kb, diag, cl, cg, dq):
            r0 = 0 if diag is None else diag * TKB
            sl = heads[h]
            kstart = pl.multiple_of(kb * TKB, TKB)
            k = k_ref[0, pl.ds(kstart, TKB), sl]
            v = v_ref[0, pl.ds(kstart, TKB), sl]
            kv = jnp.concatenate([jnp.concatenate([k, zkv], axis=1), jnp.concatenate([zkv, v], axis=1)], axis=0)
            both = _dot_nt(qds[h][r0:, :], kv)
            lb, sp = _sb_terms(both[:, :TKB])
            if diag is not None:
                strict = rows[:TQB - r0, :] > cols[:TQB - r0, :]
                sp = jnp.where(strict, sp, 0.0)
            pre = _dot2_deep(sp, prefix) + cl[r0:, :]
            a = jnp.exp(lb - stots[h][r0:, :] + pre)
            if diag is not None:
                a = jnp.where(strict, a, 0.0)
            gz = both[:, TKB:] * a
            pg = _dot2_deep(gz, prefix) + cg[r0:, :]
            dl = gz - pg * jnp.exp(lb)
            if diag is not None:
                dl = jnp.where(strict, dl, 0.0)
            dlb = dl.astype(BF16)
            dkt_acc[h * nkb + kb] += _dot(qts[h][:, r0:], dlb)
            dvt_acc[h * nkb + kb] += _dot(dots[h][:, r0:], a.astype(BF16))
            new = (pre[:, TKB - 1:TKB], pg[:, TKB - 1:TKB], dq[r0:, :] + _dot(dlb, k))
            if r0:
                new = tuple(jnp.concatenate([old[:r0, :], x], axis=0) for old, x in zip((cl, cg, dq), new))
            return new

        state = [jnp.zeros((TQB, 1), F32), jnp.zeros((TQB, 1), F32), jnp.zeros((TQB, HD), F32)] * 2

        def step(i, st):
            st = list(st)
            for sub in range(KSTEP):
                for h in range(2):
                    st[3 * h:3 * h + 3] = tile(h, i * KSTEP + sub, None, *st[3 * h:3 * h + 3])
            return tuple(st)

        state = list(lax.fori_loop(0, q0 // (KSTEP * TKB), step, tuple(state)))
        for d in range(ndiag):
            for h in range(2):
                state[3 * h:3 * h + 3] = tile(h, q0 // TKB + d, d, *state[3 * h:3 * h + 3])
        for h in range(2):
            dq_ref[:, heads[h]] = (state[3 * h + 2] * 0.125).astype(BF16)

        @pl.when(qi == nq - 1)
        def _():
            eye = _eye(TKB)
            for h in range(2):
                for kb in range(nkb):
                    rows_kb = slice(kb * TKB, (kb + 1) * TKB)
                    dk_ref[rows_kb, heads[h]] = _transpose_f32(dkt_acc[h * nkb + kb], eye).astype(BF16)
                    dv_ref[rows_kb, heads[h]] = _transpose_f32(dvt_acc[h * nkb + kb], eye).astype(BF16)

    q_spec, k_spec, v_spec = _attn_b_specs(s)
    blk = pl.BlockSpec((TQB, 128), lambda hp, qi: (qi, hp))
    col = pl.BlockSpec((s, 128), lambda hp, qi: (0, hp))
    return _call(
        body, name="attn_b_bwd", grid=(4, nq),
        in_specs=[q_spec, k_spec, v_spec, pl.BlockSpec((1, TQB, 128), lambda hp, qi: (hp, qi, 0)), blk],
        out_specs=[blk, col, col],
        out_shape=[_sds((s, WG), BF16)] * 3,
        scratch_shapes=[pltpu.VMEM((2 * nkb, HD, TKB), F32), pltpu.VMEM((2 * nkb, HD, TKB), F32)],
        compiler_params=_params(2),
    )(proj, proj, proj, lsum, d_ob)


def _bias_fold(dtab):
    def body(t_ref, d_ref, far_ref):
        acc = jnp.zeros((NDEV, TABW), F32)
        zpad = jnp.zeros((NDEV, TAB0), F32)
        for r in range(TQA):
            row = jnp.concatenate([zpad, t_ref[:, r, :]], axis=1)
            acc = acc + (pltpu.roll(row, TABW - r, 1) if r else row)
        d_ref[...] = acc
        lane = lax.broadcasted_iota(jnp.int32, (NDEV, TABW), 1)
        far = jnp.sum(jnp.where(lane < N_FAR, acc, 0.0), axis=1, keepdims=True)
        far_ref[...] = jnp.broadcast_to(far, (NDEV, 128))

    d_fpad, d_far = _call(
        body, name="bias_fold", grid=(1,),
        in_specs=[pl.BlockSpec((NDEV, TQA, KWA), lambda i: (0, 0, 0))],
        out_specs=[pl.BlockSpec((NDEV, TABW), lambda i: (0, 0)), pl.BlockSpec((NDEV, 128), lambda i: (0, 0))],
        out_shape=[_sds((NDEV, TABW), F32), _sds((NDEV, 128), F32)],
        compiler_params=_params(1),
    )(dtab)
    d_near = d_fpad[:, N_FAR:N_FAR + N_NEAR][:, ::-1]
    return jnp.concatenate([jnp.zeros((NDEV, REL_CLIP - CHUNK + 1), F32), d_near, d_far[:, :1]], axis=1)


def _adamw(w, g, m, v):
    m = B1 * m + (1.0 - B1) * g
    v = B2 * v + (1.0 - B2) * (g * g)
    m_hat = m / (1.0 - B1 ** STEP)
    v_hat = v / (1.0 - B2 ** STEP)
    delta = -LR * (m_hat / (jnp.sqrt(v_hat) + AEPS) + WD * w)
    return delta, m, v


def _adamw_big(recv0, recv1, w, m, v):
    _, rows, cols = recv0.shape
    tr = max(t for t in range(16, 513, 16) if rows % t == 0)
    nt = rows // tr

    def body(r0_ref, r1_ref, w_ref, m_ref, v_ref, g_ref, d_ref, nm_ref, nv_ref):
        def update(r_ref):
            g = r_ref[0].astype(F32)
            for p in range(1, NDEV):
                g = g + r_ref[p].astype(F32)
            delta, nm, nv = _adamw(w_ref[0], g, m_ref[0], v_ref[0])
            g_ref[0], d_ref[0], nm_ref[0], nv_ref[0] = g, delta, nm, nv

        @pl.when(pl.program_id(0) == 0)
        def _():
            update(r0_ref)

        @pl.when(pl.program_id(0) == 1)
        def _():
            update(r1_ref)

    blk = pl.BlockSpec((1, tr, cols), lambda l, i: (l, i, 0))
    r0_spec = pl.BlockSpec((NDEV, tr, cols), lambda l, i: (0, jnp.where(l == 0, i, nt - 1), 0))
    r1_spec = pl.BlockSpec((NDEV, tr, cols), lambda l, i: (0, jnp.where(l == 1, i, 0), 0))
    return _call(
        body, name="adamw_big", grid=(2, nt),
        in_specs=[r0_spec, r1_spec, blk, blk, blk],
        out_specs=[blk] * 4,
        out_shape=[_sds((2, rows, cols), F32)] * 4,
        compiler_params=_params(2),
    )(recv0, recv1, w, m, v)


def _adamw_w_ada(cact_t, dmod, w, m, v):
    tr = 256

    def body(c_ref, dm_ref, w_ref, m_ref, v_ref, g_ref, d_ref, nm_ref, nv_ref):
        g = c_ref[:, 0:1] * dm_ref[0, 0:1, :]
        for b in range(1, NDEV):
            g = g + c_ref[:, b:b + 1] * dm_ref[0, b:b + 1, :]
        delta, nm, nv = _adamw(w_ref[0], g, m_ref[0], v_ref[0])
        g_ref[0], d_ref[0], nm_ref[0], nv_ref[0] = g, delta, nm, nv

    blk = pl.BlockSpec((1, tr, 768), lambda l, i: (l, i, 0))
    return _call(
        body, name="adamw_w_ada", grid=(2, D // tr),
        in_specs=[pl.BlockSpec((tr, NDEV), lambda l, i: (i, 0)), pl.BlockSpec((1, NDEV, 768), lambda l, i: (l, 0, 0)),
                  blk, blk, blk],
        out_specs=[blk] * 4,
        out_shape=[_sds((2, D, 768), F32)] * 4,
        compiler_params=_params(2),
    )(cact_t, dmod, w, m, v)


def _adamw_small(gath, w, m, v):
    rows = gath.shape[1]

    def body(r_ref, w_ref, m_ref, v_ref, g_ref, d_ref, nm_ref, nv_ref):
        g = r_ref[0]
        for p in range(1, NDEV):
            g = g + r_ref[p]
        delta, nm, nv = _adamw(w_ref[...], g, m_ref[...], v_ref[...])
        g_ref[...], d_ref[...], nm_ref[...], nv_ref[...] = g, delta, nm, nv

    blk = pl.BlockSpec((rows, D), lambda i: (0, 0))
    return _call(
        body, name="adamw_small", grid=(1,),
        in_specs=[pl.BlockSpec((NDEV, rows, D), lambda i: (0, 0, 0)), blk, blk, blk],
        out_specs=[blk] * 4,
        out_shape=[_sds((rows, D), F32)] * 4,
        compiler_params=_params(1),
    )(gath, w, m, v)


_PACK = (("b_ada", 2 * 6 * D), ("rel_bias", 2 * 8 * 257), ("g_a", 2 * WG), ("g_b", 2 * WG),
         ("conv_b", 2 * 2 * DFF), ("final_g", D), ("conv_w", 2 * NDEV * 3 * GU))


def _pack(parts):
    rows = []
    for name, size in _PACK:
        flat = parts[name].reshape(-1).astype(F32)
        assert flat.shape[0] == size, (name, flat.shape)
        rows.append(jnp.pad(flat, (0, -size % D)))
    out = jnp.concatenate(rows).reshape(-1, D)
    return jnp.pad(out, ((0, -out.shape[0] % 8), (0, 0)))


def _unpack(packed):
    flat = packed.reshape(-1)
    out, pos = {}, 0
    for name, size in _PACK:
        out[name] = flat[pos:pos + size]
        pos += size + (-size % D)
    return out


def kernel(x, c, w_ada, b_ada, w_in, rel_bias, g_a, g_b, w_out, w_up, conv_w, conv_b, w_down, final_g, loss_target, m_w_ada, m_b_ada, m_w_in, m_rel_bias, m_g_a, m_g_b, m_w_out, m_w_up, m_conv_w, m_conv_b, m_w_down, m_final_g, v_w_ada, v_b_ada, v_w_in, v_rel_bias, v_g_a, v_g_b, v_w_out, v_w_up, v_conv_w, v_conv_b, v_w_down, v_final_g):
    s = x.shape[1]
    assert s % TQA == 0 and s >= KWA and s % 512 == 0
    tm = 512
    tmm = min(1024, s)
    me = 4 * lax.axis_index("x") + 2 * lax.axis_index("y") + lax.axis_index("c")
    xs = x.reshape(s, D)
    target = loss_target.reshape(s, D)

    first = jnp.concatenate([c, jnp.pad(conv_w.reshape(2 * 3, GU), ((0, 1), (0, D - GU)))])
    first_all = _small_allgather(first, "gather_c_conv_w")
    c_all = first_all[:, 0, :]
    cw_all = first_all[:, 1:7, :GU].reshape(NDEV, 2, 3, GU)

    b_sl = lax.dynamic_slice(b_ada, (0, me * 768), (2, 768)).reshape(2, 1, 768)
    mod_part, cact = _mod_fwd(c_all, w_ada, b_sl)
    mod_all = _small_allgather(mod_part.reshape(2 * NDEV, 768), "gather_mod")

    up_t = [jnp.transpose(t, (0, 2, 1)) for t in (w_up, m_w_up, v_w_up)]
    shards = {"in": w_in, "out": w_out, "up": up_t[0], "down": w_down}
    order = [(kind, l) for l in range(2) for kind in _KINDS]
    mod_all, *srcs = lax.optimization_barrier((mod_all, *[shards[kind][l].astype(BF16) for kind, l in order]))
    gather_started = dict(zip(order, _exchange_start([kind for kind, _ in order], True, srcs, "weights_gather_start")))

    def gathered(kind, l, after):
        return _exchange_wait([kind], True, [gather_started[kind, l]], after, f"weights_gather_wait_{kind}{l}")[0]

    mod_all = mod_all.reshape(NDEV, 2, NDEV, 768)
    mod_me = lax.dynamic_index_in_dim(mod_all, me, axis=2, keepdims=False)
    mod = jnp.transpose(mod_me, (1, 0, 2)).reshape(2, 6, 1, D)

    saved = []
    xl = xs
    for l in range(2):
        sh_mix, sc_mix, gt_mix, sh_ffn, sc_ffn, gt_ffn = (mod[l, j] for j in range(6))
        cw = cw_all[:, l].reshape(2, 4, 3, GU)
        cb = conv_b[l].reshape(2, 4, 1, GU)
        gvec = jnp.concatenate([g_a[l], g_b[l]]).reshape(1, D)
        tab = _bias_table(rel_bias[l])

        wi = gathered("in", l, xl if l else mod)
        h1, proj = _nm_matmul(xl, sh_mix, sc_mix, wi, two_d=True, n=WG, groups=NT, out_dtype=BF16, tm=tmm,
                              name="norm_proj")
        oa = _attn_a_fwd(proj, tab)
        ob, lsum = _attn_b_fwd(proj)
        wo = gathered("out", l, ob)
        nab, mixed, x2 = _mix_out(oa, ob, gvec, wo, xl, gt_mix, tm)
        wu = gathered("up", l, x2)
        h2, u = _nm_matmul(x2, sh_ffn, sc_ffn, wu, two_d=False, n=GU, groups=NDEV, out_dtype=BF16, tm=tmm,
                           name="norm_up")
        u = u.reshape(2, 4, s, GU)
        a = _conv_act(u, cw, cb, tmm)
        wd4 = gathered("down", l, a).reshape(4, GU, D)
        ffn, x3 = _down(a, wd4, x2, gt_ffn, tmm)
        saved.append(dict(x=xl, h1=h1, proj=proj, oa=oa, ob=ob, lsum=lsum, nab=nab, mixed=mixed, x2=x2, h2=h2, u=u,
                          a=a, ffn=ffn, cw=cw, cb=cb, gvec=gvec, tab=tab, wd4=wd4, wi=wi, wo=wo, wu=wu))
        xl = x3

    loss_part, dx, d_final_g = _final_loss(xl, final_g.reshape(1, D), target, tm)
    loss = lax.psum(loss_part[0, 0], ("x", "y", "c"))

    sent = {}
    small = {"b_ada": [None, None], "rel_bias": [None, None], "g_a": [None, None], "g_b": [None, None],
             "conv_b": [None, None], "conv_w": [None, None]}

    def send(kind, l, grad):
        started, token = _exchange_start([kind], False, [grad], f"grads_start_{kind}{l}", with_token=True)
        sent[kind, l] = started[0]
        return token

    for l in (1, 0):
        sv = saved[l]
        sh_mix, sc_mix, gt_mix, sh_ffn, sc_ffn, gt_ffn = (mod[l, j] for j in range(6))
        d_gt_ffn, dff, da = _down_bwd(dx, gt_ffn, sv["ffn"], sv["wd4"], tmm)
        tok = send("down", l, _wgrad(sv["a"], dff.reshape(1, s, D), tk=tmm, name="wgrad_down").reshape(DFF, D))
        dy, d_cw, d_cb = _conv_act_bwd(sv["u"], sv["cw"], sv["cb"], da, tm, tok)
        du = _conv_transpose(dy.reshape(NDEV, s, GU), sv["cw"].reshape(NDEV, 3, GU), tmm)
        tok = send("up", l, _wgrad(du, sv["h2"].reshape(1, s, D), tk=tmm, name="wgrad_up"))
        dx2, d_sc_ffn, d_sh_ffn = _dgrad_norm_bwd(du, sv["wu"], sv["x2"], sc_ffn, dx, tm=tmm, name="dgrad_up", after=tok)
        d_gt_mix, dmixed, d_oa, d_ob, d_g = _mix_out_bwd(dx2, sv["mixed"], gt_mix, sv["wo"], sv["oa"], sv["ob"],
                                                         sv["gvec"], tm)
        tok = send("out", l, _wgrad(sv["nab"].reshape(1, s, D), dmixed.reshape(1, s, D), tk=tmm,
                                    name="wgrad_out").reshape(D, D))
        dqa, dka, dva, d_tab = _attn_a_bwd(sv["proj"], sv["tab"], d_oa, tok)
        dqb, dkb, dvb = _attn_b_bwd(sv["proj"], sv["lsum"], d_ob)
        dparts = (dqa, dka, dva, dqb, dkb, dvb)
        if l:
            tok = send("in", l, _wgrad_in(sv["h1"], dparts, tmm, None))
        dx, d_sc_mix, d_sh_mix = _dgrad_in(dparts, sv["wi"], sv["x"], sc_mix, dx2, tm, tok)
        small["b_ada"][l] = jnp.concatenate([d_sh_mix, d_sc_mix, d_gt_mix, d_sh_ffn, d_sc_ffn, d_gt_ffn], axis=1)
        small["rel_bias"][l] = _bias_fold(d_tab)
        small["g_a"][l], small["g_b"][l] = d_g[:, :WG], d_g[:, WG:]
        small["conv_b"][l] = d_cb
        small["conv_w"][l] = d_cw.reshape(NDEV, 3, GU)
    grad_x = dx.reshape(1, s, D)

    contrib = {k: jnp.stack(vs) for k, vs in small.items()}
    contrib["final_g"] = d_final_g
    gath = _small_allgather(_pack(contrib), "gather_small_grads")
    tok = send("in", 0, _wgrad_in(saved[0]["h1"], dparts, tmm, gath))

    def place_conv_w(t):
        return lax.dynamic_update_slice(jnp.zeros((2, NDEV, 3, GU), F32), t.reshape(2, 1, 3, GU), (0, me, 0, 0))

    def packed_params(b, rb, ga, gb, cb_, fg, cw_):
        return _pack({"b_ada": b, "rel_bias": rb, "g_a": ga, "g_b": gb, "conv_b": cb_, "final_g": fg,
                      "conv_w": place_conv_w(cw_)})

    sm = _adamw_small(gath,
                      packed_params(b_ada, rel_bias, g_a, g_b, conv_b, final_g, conv_w),
                      packed_params(m_b_ada, m_rel_bias, m_g_a, m_g_b, m_conv_b, m_final_g, m_conv_w),
                      packed_params(v_b_ada, v_rel_bias, v_g_a, v_g_b, v_conv_b, v_final_g, v_conv_w))
    sm = [_unpack(t) for t in sm]

    dmod_all = gath[:, :12, :].reshape(NDEV, 2, 6 * D)
    dmod_sl = jnp.transpose(lax.dynamic_slice(dmod_all, (0, 0, me * 768), (NDEV, 2, 768)), (1, 0, 2))
    ada = _adamw_w_ada(cact.T, dmod_sl, w_ada, m_w_ada, v_w_ada)

    big = {}
    for kind, (w, m, v) in (("down", (w_down, m_w_down, v_w_down)), ("up", up_t),
                            ("out", (w_out, m_w_out, v_w_out)), ("in", (w_in, m_w_in, v_w_in))):
        recv0, recv1 = _exchange_wait([kind, kind], False, [sent[kind, 0], sent[kind, 1]], tok, f"grads_wait_{kind}")
        big[kind] = _adamw_big(recv0, recv1, w, m, v)
        tok = big[kind][0]
    big["up"] = [jnp.transpose(t, (0, 2, 1)) for t in big["up"]]

    def small_out(j, name):
        t = sm[j][name]
        if name == "b_ada":
            return t.reshape(2, 6 * D)
        if name == "rel_bias":
            return t.reshape(2, 8, 257)
        if name in ("g_a", "g_b"):
            return t.reshape(2, WG)
        if name == "conv_b":
            return t.reshape(2, 2 * DFF)
        if name == "final_g":
            return t.reshape(D)
        t = t.reshape(2, NDEV, 3, GU)
        return lax.dynamic_index_in_dim(t, me, axis=1, keepdims=False)

    def group(j):
        return (ada[j], small_out(j, "b_ada"), big["in"][j], small_out(j, "rel_bias"), small_out(j, "g_a"),
                small_out(j, "g_b"), big["out"][j], big["up"][j], small_out(j, "conv_w"), small_out(j, "conv_b"),
                big["down"][j], small_out(j, "final_g"))

    return (loss, grad_x, *group(0), *group(1), *group(2), *group(3))
```

```python
import jax
import jax.numpy as jnp
from jax import lax
from jax.experimental import pallas as pl
from jax.experimental.pallas import tpu as pltpu

F32, BF16 = jnp.float32, jnp.bfloat16
MESH_ID = pl.DeviceIdType.MESH
NDEV = 8
D = 1024
HD = 64
WG = 512
NT = 6
GU = 704
DFF = 2816
IN_SH = NT * WG // NDEV
OUT_SH = D // NDEV
DOWN_SH = DFF // NDEV
ADA_SH = 6 * D // NDEV
LANES = 128
CHUNK, NPREV, REL_CLIP = 64, 8, 128
BAND = (NPREV + 1) * CHUNK
EPS = 1e-6
NEG = -1e30
TQA = 256
KWA = TQA + NPREV * CHUNK
TABW = 1024
TAB0 = TABW - KWA
N_FAR = TAB0 + NPREV * CHUNK - REL_CLIP + 1
N_NEAR = REL_CLIP + CHUNK - 1
TQB, TKB = 512, 128
KSTEP = 4
LR, B1, B2, AEPS, WD, STEP = 0.001, 0.9, 0.999, 1e-08, 0.01, 10
VMEM_MB = 56


def _call(body, **kw):
    return pl.pallas_call(body, **kw)


def _call_after(body, after, **kw):
    n_in = len(kw["in_specs"])
    kw["in_specs"] = list(kw["in_specs"]) + [pl.BlockSpec(memory_space=pl.ANY)]

    def tied(*refs):
        body(*refs[:n_in], *refs[n_in + 1:])

    call = _call(tied, **kw)
    return lambda *args: call(*args, after)


def _params(n_axes):
    return pltpu.CompilerParams(dimension_semantics=("arbitrary",) * n_axes, vmem_limit_bytes=VMEM_MB << 20)


def _dot(a, b):
    return jnp.dot(a, b, preferred_element_type=F32)


def _dot_nt(a, b):
    return lax.dot_general(a, b, (((1,), (1,)), ((), ())), preferred_element_type=F32)


def _dot_tn(a, b):
    return lax.dot_general(a, b, (((0,), (0,)), ((), ())), preferred_element_type=F32)


def _dot2(x, u):
    hi = x.astype(BF16)
    lo = (x - hi.astype(F32)).astype(BF16)
    return _dot(hi, u) + _dot(lo, u)


def _dot2_deep(x, uu):
    hi = x.astype(BF16)
    lo = (x - hi.astype(F32)).astype(BF16)
    return _dot(jnp.concatenate([hi, lo], axis=1), uu)


def _eye(n):
    i = lax.broadcasted_iota(jnp.int32, (n, n), 0)
    j = lax.broadcasted_iota(jnp.int32, (n, n), 1)
    return jnp.where(i == j, 1.0, 0.0).astype(BF16)


def _transpose_bf16(x, eye):
    return _dot_nt(eye, x).astype(BF16)


def _transpose_f32(x, eye):
    hi = x.astype(BF16)
    lo = (x - hi.astype(F32)).astype(BF16)
    return _dot_nt(eye, hi) + _dot_nt(eye, lo)


def _rms(x):
    return lax.rsqrt(jnp.mean(x * x, axis=-1, keepdims=True) + EPS)


def _rms_bwd(dn, n, r):
    return r * (dn - n * jnp.mean(dn * n, axis=-1, keepdims=True))


def _colsum(x):
    return jnp.sum(x, axis=0, keepdims=True)


def _sigmoid(x):
    return 1.0 / (1.0 + jnp.exp(-x))


def _sds(shape, dtype):
    return jax.ShapeDtypeStruct(shape, dtype)


def _place():
    x, y, c = lax.axis_index("x"), lax.axis_index("y"), lax.axis_index("c")
    return x, y, c, 4 * x + 2 * y + c


def _peer(x, y, c, k):
    px = 1 - x if k & 4 else x
    py = 1 - y if k & 2 else y
    pc = 1 - c if k & 1 else c
    return (px, py, pc), 4 * px + 2 * py + pc


def _small_allgather(v, name):
    rows, cols = v.shape

    def body(v_ref, out_ref, send_sems, recv_sems, local_sem):
        x, y, c, me = _place()
        mine = pltpu.make_async_copy(v_ref, out_ref.at[me], local_sem)
        mine.start()
        sends = []
        for k in range(1, NDEV):
            peer, _ = _peer(x, y, c, k)
            cp = pltpu.make_async_remote_copy(v_ref, out_ref.at[me], send_sems.at[k - 1], recv_sems.at[k - 1],
                                              device_id=peer, device_id_type=MESH_ID)
            cp.start()
            sends.append(cp)
        for k in range(1, NDEV):
            peer, pidx = _peer(x, y, c, k)
            pltpu.make_async_remote_copy(v_ref, out_ref.at[pidx], send_sems.at[k - 1], recv_sems.at[k - 1],
                                         device_id=peer, device_id_type=MESH_ID).wait_recv()
        for cp in sends:
            cp.wait_send()
        mine.wait()

    return _call(
        body, name=name,
        out_shape=_sds((NDEV, rows, cols), F32),
        in_specs=[pl.BlockSpec(memory_space=pltpu.VMEM)],
        out_specs=pl.BlockSpec(memory_space=pltpu.VMEM),
        scratch_shapes=[pltpu.SemaphoreType.DMA((NDEV - 1,)), pltpu.SemaphoreType.DMA((NDEV - 1,)),
                        pltpu.SemaphoreType.DMA],
    )(v)


def _shard_view(ref, kind, p):
    if kind == "in":
        return ref.at[:, pl.ds(pl.multiple_of(p * IN_SH, LANES), IN_SH)]
    if kind == "out":
        return ref.at[pl.ds(pl.multiple_of(p * OUT_SH, OUT_SH), OUT_SH), :]
    if kind == "up":
        return ref.at[p]
    if kind == "down":
        return ref.at[pl.ds(pl.multiple_of(p * DOWN_SH, 16), DOWN_SH), :]
    raise ValueError(kind)


_KINDS = ("in", "out", "up", "down")
_FULL_SHAPES = {"in": (D, 3 * D), "out": (D, D), "up": (NDEV, GU, D), "down": (DFF, D)}
_SHARD_SHAPES = {"in": (D, IN_SH), "out": (OUT_SH, D), "up": (GU, D), "down": (DOWN_SH, D)}


_HBM = pl.BlockSpec(memory_space=pltpu.HBM)
_SEM = pl.BlockSpec(memory_space=pltpu.SEMAPHORE)
_EFFECT = pltpu.SideEffectType.DATAFLOW_SIDE_EFFECTING
_SEM_SHAPES = (pltpu.SemaphoreType.DMA((NDEV - 1,)), pltpu.SemaphoreType.DMA((NDEV - 1,)), pltpu.SemaphoreType.DMA(()))


def _hbm(a):
    return pltpu.with_memory_space_constraint(a, pltpu.HBM)


def _exchange_copies(kind, gather, src, land, sems):
    send_sems, recv_sems, local_sem = sems
    x, y, c, me = _place()

    def ends(p_dst, p_from):
        if gather:
            return src, _shard_view(land, kind, me), _shard_view(land, kind, p_from)
        return _shard_view(src, kind, p_dst), land.at[me], land.at[p_from]

    s_me, d_me, _ = ends(me, me)
    local = pltpu.make_async_copy(s_me, d_me, local_sem)
    sends, arrivals = [], []
    for k in range(1, NDEV):
        peer, pidx = _peer(x, y, c, k)
        s_k, d_k, from_k = ends(pidx, pidx)
        sends.append(pltpu.make_async_remote_copy(s_k, d_k, send_sems.at[k - 1], recv_sems.at[k - 1],
                                                  device_id=peer, device_id_type=MESH_ID))
        arrivals.append(pltpu.make_async_remote_copy(s_k, from_k, send_sems.at[k - 1], recv_sems.at[k - 1],
                                                     device_id=peer, device_id_type=MESH_ID))
    return local, sends, arrivals


def _exchange_start(kinds, gather, srcs, name, with_token=False):
    n = len(kinds)
    lands = [lax.empty(_FULL_SHAPES[kd] if gather else (NDEV,) + _SHARD_SHAPES[kd], BF16) for kd in kinds]

    def body(*refs):
        ins, sems = refs[:2 * n], refs[2 * n:5 * n]
        for j, kd in enumerate(kinds):
            local, sends, _ = _exchange_copies(kd, gather, ins[j], ins[n + j], sems[3 * j:3 * j + 3])
            local.start()
            for cp in sends:
                cp.start()
        if with_token:
            token = refs[7 * n]
            token[...] = jnp.zeros_like(token)

    out_shape = list(_SEM_SHAPES) * n
    out_shape += [pltpu.HBM(a.shape, a.dtype) for a in srcs] + [pltpu.HBM(a.shape, a.dtype) for a in lands]
    out_specs = [_SEM] * (3 * n) + [_HBM] * (2 * n)
    if with_token:
        out_shape.append(_sds((8, 128), F32))
        out_specs.append(pl.BlockSpec(memory_space=pltpu.VMEM))
    outs = _call(
        body, name=name, out_shape=out_shape,
        in_specs=[_HBM] * (2 * n), out_specs=out_specs,
        input_output_aliases={i: 3 * n + i for i in range(2 * n)},
        compiler_params=pltpu.CompilerParams(has_side_effects=_EFFECT),
    )(*[_hbm(a) for a in srcs], *[_hbm(a) for a in lands])
    per_tensor = [(tuple(outs[3 * j:3 * j + 3]), outs[3 * n + j], outs[4 * n + j]) for j in range(n)]
    return (per_tensor, outs[5 * n]) if with_token else per_tensor


def _exchange_wait(kinds, gather, started, after, name):
    n = len(kinds)

    def body(*refs):
        ins, sems = refs[:2 * n], refs[2 * n:5 * n]
        for j, kd in enumerate(kinds):
            local, sends, arrivals = _exchange_copies(kd, gather, ins[j], ins[n + j], sems[3 * j:3 * j + 3])
            local.wait()
            for cp in arrivals:
                cp.wait_recv()
            for cp in sends:
                cp.wait_send()

    srcs = [st[1] for st in started]
    lands = [st[2] for st in started]
    sems = [sm for st in started for sm in st[0]]
    outs = _call(
        body, name=name,
        out_shape=[pltpu.HBM(a.shape, a.dtype) for a in srcs] + [pltpu.HBM(a.shape, a.dtype) for a in lands],
        in_specs=[_HBM] * (2 * n) + [_SEM] * (3 * n) + [pl.BlockSpec(memory_space=pl.ANY)],
        out_specs=[_HBM] * (2 * n),
        input_output_aliases={i: i for i in range(2 * n)},
        compiler_params=pltpu.CompilerParams(has_side_effects=_EFFECT),
    )(*srcs, *lands, *sems, after)
    return outs[n:]


def _mod_fwd(c_all, w_ada, b_sl):
    def body(c_ref, w_ref, b_ref, mod_ref, cact_ref):
        cv = c_ref[...]
        ca = cv * _sigmoid(cv)
        cact_ref[...] = ca
        mod_ref[0] = _dot(ca.astype(BF16), w_ref[0].astype(BF16)) + b_ref[0]

    return _call(
        body, name="mod_fwd", grid=(2,),
        in_specs=[pl.BlockSpec((NDEV, D), lambda l: (0, 0)), pl.BlockSpec((1, D, ADA_SH), lambda l: (l, 0, 0)),
                  pl.BlockSpec((1, 1, ADA_SH), lambda l: (l, 0, 0))],
        out_specs=[pl.BlockSpec((1, NDEV, ADA_SH), lambda l: (l, 0, 0)), pl.BlockSpec((NDEV, D), lambda l: (0, 0))],
        out_shape=[_sds((2, NDEV, ADA_SH), F32), _sds((NDEV, D), F32)],
        compiler_params=_params(1),
    )(c_all, w_ada, b_sl)


def _nm_matmul(x, shift, scale, w, *, two_d, n, groups, out_dtype, tm, name):
    s = x.shape[0]

    def body(x_ref, sh_ref, sc_ref, w_ref, h_ref, o_ref):
        @pl.when(pl.program_id(1) == 0)
        def _():
            xv = x_ref[...]
            h_ref[...] = ((xv * _rms(xv)) * (1.0 + sc_ref[...]) + sh_ref[...]).astype(BF16)
        if two_d:
            o_ref[0] = _dot(h_ref[...], w_ref[...]).astype(out_dtype)
        else:
            o_ref[0] = _dot_nt(h_ref[...], w_ref[0]).astype(out_dtype)

    vec = pl.BlockSpec((1, D), lambda i, g: (0, 0))
    w_spec = pl.BlockSpec((D, n), lambda i, g: (0, g)) if two_d else pl.BlockSpec((1, n, D), lambda i, g: (g, 0, 0))
    return _call(
        body, name=name, grid=(s // tm, groups),
        in_specs=[pl.BlockSpec((tm, D), lambda i, g: (i, 0)), vec, vec, w_spec],
        out_specs=[pl.BlockSpec((tm, D), lambda i, g: (i, 0)), pl.BlockSpec((1, tm, n), lambda i, g: (g, i, 0))],
        out_shape=[_sds((s, D), BF16), _sds((groups, s, n), out_dtype)],
        compiler_params=_params(2),
    )(x, shift, scale, w)


def _bias_table(rel_bias):
    far = jnp.broadcast_to(rel_bias[:, 2 * REL_CLIP:], (NDEV, N_FAR))
    near = rel_bias[:, 2 * REL_CLIP - 1:REL_CLIP - CHUNK:-1]
    fpad = jnp.concatenate([far, near, jnp.zeros((NDEV, TABW - N_FAR - N_NEAR), F32)], axis=1)

    def body(f_ref, o_ref):
        t = pltpu.roll(jnp.broadcast_to(f_ref[0], (TQA, TABW)), 0, 1, stride=1, stride_axis=0)[:, TAB0:]
        rows = lax.broadcasted_iota(jnp.int32, (TQA, KWA), 0)
        cols = lax.broadcasted_iota(jnp.int32, (TQA, KWA), 1)
        first = jnp.bitwise_and(rows, -CHUNK)
        o_ref[0] = jnp.where((cols >= first) & (cols < first + BAND), t, NEG)

    return _call(
        body, name="bias_table", grid=(NDEV,),
        in_specs=[pl.BlockSpec((1, 1, TABW), lambda h: (h, 0, 0))],
        out_specs=pl.BlockSpec((1, TQA, KWA), lambda h: (h, 0, 0)),
        out_shape=_sds((NDEV, TQA, KWA), F32),
        compiler_params=_params(1),
    )(fpad.reshape(NDEV, 1, TABW))


def _attn_a_cases(qi, tile):
    @pl.when(qi == 0)
    def _():
        tile(TQA, 2 * TQA, 0)

    @pl.when(qi == 1)
    def _():
        tile(2 * TQA, TQA, 0)

    @pl.when(qi >= 2)
    def _():
        tile(KWA, 0, pl.multiple_of((qi - 2) * TQA, TQA))


HA = 8
WA = HA * HD


def _attn_a_specs(s):
    q_spec = pl.BlockSpec((1, TQA, WA), lambda hp, qi: (0, qi, hp))
    k_spec = pl.BlockSpec((1, s, WA), lambda hp, qi: (1, 0, hp))
    v_spec = pl.BlockSpec((1, s, WA), lambda hp, qi: (2, 0, hp))
    b_spec = pl.BlockSpec((HA, TQA, KWA), lambda hp, qi: (hp, 0, 0))
    return q_spec, k_spec, v_spec, b_spec


def _attn_a_fwd(proj, bias_tab):
    s = proj.shape[1]

    def body(q_ref, k_ref, v_ref, b_ref, o_ref):
        def tile(nk, off, kstart):
            for h in range(HA):
                sl = slice(HD * h, HD * (h + 1))
                q = q_ref[0, :, sl]
                k = k_ref[0, pl.ds(kstart, nk), sl]
                v = v_ref[0, pl.ds(kstart, nk), sl]
                sc = _dot_nt(q, k) * 0.125 + b_ref[h, :, off:off + nk]
                p = jnp.exp(sc - jnp.max(sc, axis=-1, keepdims=True))
                den = jnp.sum(p, axis=-1, keepdims=True)
                o_ref[:, sl] = _dot(p.astype(BF16), v) / den

        _attn_a_cases(pl.program_id(1), tile)

    q_spec, k_spec, v_spec, b_spec = _attn_a_specs(s)
    return _call(
        body, name="attn_a_fwd", grid=(WG // WA, s // TQA),
        in_specs=[q_spec, k_spec, v_spec, b_spec],
        out_specs=pl.BlockSpec((TQA, WA), lambda hp, qi: (qi, hp)),
        out_shape=_sds((s, WG), F32),
        compiler_params=_params(2),
    )(proj, proj, proj, bias_tab)


def _sb_terms(lg):
    sp = jnp.maximum(lg, 0.0) + jnp.log(1.0 + jnp.exp(-jnp.abs(lg)))
    return lg - sp, sp


def _attn_b_specs(s):
    q_spec = pl.BlockSpec((1, TQB, 128), lambda hp, qi: (3, qi, hp))
    k_spec = pl.BlockSpec((1, s, 128), lambda hp, qi: (4, 0, hp))
    v_spec = pl.BlockSpec((1, s, 128), lambda hp, qi: (5, 0, hp))
    return q_spec, k_spec, v_spec


def _attn_b_fwd(proj):
    s = proj.shape[1]
    ndiag = TQB // TKB

    def body(q_ref, k_ref, v_ref, o_ref, ls_ref):
        q0 = pl.program_id(1) * TQB
        rows = lax.broadcasted_iota(jnp.int32, (TQB, TKB), 0)
        cols = lax.broadcasted_iota(jnp.int32, (TQB, TKB), 1)
        uj = lax.broadcasted_iota(jnp.int32, (TKB, TKB), 0)
        us = lax.broadcasted_iota(jnp.int32, (TKB, TKB), 1)
        suffix = jnp.where(uj >= us, 1.0, 0.0).astype(BF16)
        heads = [slice(HD * h, HD * (h + 1)) for h in range(2)]
        qs = [(q_ref[0, :, sl].astype(F32) * 0.125).astype(BF16) for sl in heads]

        def tile(h, kstart, diag, carry, acc):
            k = k_ref[0, pl.ds(kstart, TKB), heads[h]]
            v = v_ref[0, pl.ds(kstart, TKB), heads[h]]
            lb, sp = _sb_terms(_dot_nt(qs[h], k))
            if diag is not None:
                strict = rows > cols + diag * TKB
                sp = jnp.where(strict, sp, 0.0)
            csum = _dot2(sp, suffix) + carry
            w = jnp.exp(lb - csum + sp)
            if diag is not None:
                w = jnp.where(strict, w, 0.0)
            return csum[:, 0:1], acc + _dot(w.astype(BF16), v)

        state = [jnp.zeros((TQB, 1), F32), jnp.zeros((TQB, HD), F32)] * 2
        for d in range(ndiag - 1, -1, -1):
            for h in range(2):
                state[2 * h:2 * h + 2] = tile(h, pl.multiple_of(q0 + d * TKB, TKB), d, *state[2 * h:2 * h + 2])
        nsteps = q0 // (KSTEP * TKB)

        def step(i, st):
            st = list(st)
            base = (nsteps - 1 - i) * (KSTEP * TKB)
            for sub in range(KSTEP - 1, -1, -1):
                for h in range(2):
                    st[2 * h:2 * h + 2] = tile(h, pl.multiple_of(base + sub * TKB, TKB), None, *st[2 * h:2 * h + 2])
            return tuple(st)

        state = lax.fori_loop(0, nsteps, step, tuple(state))
        for h in range(2):
            o_ref[:, heads[h]] = state[2 * h + 1]
            ls_ref[0, :, heads[h]] = jnp.broadcast_to(state[2 * h], (TQB, HD))

    q_spec, k_spec, v_spec = _attn_b_specs(s)
    return _call(
        body, name="attn_b_fwd", grid=(4, s // TQB),
        in_specs=[q_spec, k_spec, v_spec],
        out_specs=[pl.BlockSpec((TQB, 128), lambda hp, qi: (qi, hp)),
                   pl.BlockSpec((1, TQB, 128), lambda hp, qi: (hp, qi, 0))],
        out_shape=[_sds((s, WG), F32), _sds((4, s, 128), F32)],
        compiler_params=_params(2),
    )(proj, proj, proj)


def _mix_out(oa, ob, g, w_out, x, gate, tm):
    s = x.shape[0]

    def body(oa_ref, ob_ref, g_ref, w_ref, x_ref, gate_ref, nab_ref, mixed_ref, x2_ref):
        a, b = oa_ref[...], ob_ref[...]
        nab_ref[:, :WG] = (a * _rms(a) * g_ref[:, :WG]).astype(BF16)
        nab_ref[:, WG:] = (b * _rms(b) * g_ref[:, WG:]).astype(BF16)
        mixed = _dot(nab_ref[...], w_ref[...])
        mixed_ref[...] = mixed
        x2_ref[...] = x_ref[...] + gate_ref[...] * mixed

    row = pl.BlockSpec((tm, D), lambda i: (i, 0))
    half = pl.BlockSpec((tm, WG), lambda i: (i, 0))
    vec = pl.BlockSpec((1, D), lambda i: (0, 0))
    return _call(
        body, name="mix_out", grid=(s // tm,),
        in_specs=[half, half, vec, pl.BlockSpec((D, D), lambda i: (0, 0)), row, vec],
        out_specs=[row, row, row],
        out_shape=[_sds((s, D), BF16), _sds((s, D), F32), _sds((s, D), F32)],
        compiler_params=_params(1),
    )(oa, ob, g, w_out, x, gate)


HALO = 16


def _conv_taps(u, halo, first, tm):
    keep = jnp.where(first, 0.0, 1.0)
    window = jnp.concatenate([halo[HALO - 8:HALO, :] * keep, u[0:8, :]], axis=0)
    um1 = jnp.concatenate([pltpu.roll(window, 1, 0)[8:16, :], pltpu.roll(u, 1, 0)[8:, :]], axis=0)
    um2 = jnp.concatenate([pltpu.roll(window, 2, 0)[8:16, :], pltpu.roll(u, 2, 0)[8:, :]], axis=0)
    return um1, um2


def _conv_specs(tm):
    u_spec = pl.BlockSpec((2, 1, tm, GU), lambda p, i: (0, p, i, 0))
    halo_spec = pl.BlockSpec((2, 1, HALO, GU), lambda p, i: (0, p, jnp.maximum(i * (tm // HALO) - 1, 0), 0))
    cw_spec = pl.BlockSpec((2, 1, 3, GU), lambda p, i: (0, p, 0, 0))
    cb_spec = pl.BlockSpec((2, 1, 1, GU), lambda p, i: (0, p, 0, 0))
    return u_spec, halo_spec, cw_spec, cb_spec


def _conv_act(u, conv_w, conv_b, tm):
    s = u.shape[2]

    def body(u_ref, halo_ref, cw_ref, cb_ref, a_ref):
        first = pl.program_id(1) == 0
        ys = []
        for side in range(2):
            uv = u_ref[side, 0].astype(F32)
            um1, um2 = _conv_taps(uv, halo_ref[side, 0].astype(F32), first, tm)
            cw = cw_ref[side, 0]
            ys.append(cw[2:3] * uv + cw[1:2] * um1 + cw[0:1] * um2 + cb_ref[side, 0])
        a_ref[0] = (ys[0] * _sigmoid(ys[0]) * ys[1]).astype(BF16)

    u_spec, halo_spec, cw_spec, cb_spec = _conv_specs(tm)
    return _call(
        body, name="conv_act", grid=(4, s // tm),
        in_specs=[u_spec, halo_spec, cw_spec, cb_spec],
        out_specs=pl.BlockSpec((1, tm, GU), lambda p, i: (p, i, 0)),
        out_shape=_sds((4, s, GU), BF16),
        compiler_params=_params(2),
    )(u, u, conv_w, conv_b)


def _down(a, w_down, x2, gate, tm):
    s = x2.shape[0]

    def body(a_ref, w_ref, x_ref, gate_ref, ffn_ref, x3_ref):
        p = pl.program_id(1)
        part = _dot(a_ref[0], w_ref[0])

        @pl.when(p == 0)
        def _():
            ffn_ref[...] = part

        @pl.when(p > 0)
        def _():
            ffn_ref[...] += part

        @pl.when(p == 3)
        def _():
            x3_ref[...] = x_ref[...] + gate_ref[...] * ffn_ref[...]

    row = pl.BlockSpec((tm, D), lambda i, p: (i, 0))
    return _call(
        body, name="down", grid=(s // tm, 4),
        in_specs=[pl.BlockSpec((1, tm, GU), lambda i, p: (p, i, 0)), pl.BlockSpec((1, GU, D), lambda i, p: (p, 0, 0)),
                  row, pl.BlockSpec((1, D), lambda i, p: (0, 0))],
        out_specs=[row, row],
        out_shape=[_sds((s, D), F32), _sds((s, D), F32)],
        compiler_params=_params(2),
    )(a, w_down, x2, gate)


def _final_loss(x, g, target, tm):
    s = x.shape[0]

    def body(x_ref, g_ref, t_ref, loss_ref, dx_ref, dg_ref):
        @pl.when(pl.program_id(0) == 0)
        def _():
            loss_ref[...] = jnp.zeros_like(loss_ref)
            dg_ref[...] = jnp.zeros_like(dg_ref)
        xv = x_ref[...]
        r = _rms(xv)
        nrm = xv * r
        err = nrm * g_ref[...] - t_ref[...]
        loss_ref[...] += (0.5 / D) * jnp.sum(jnp.sum(err * err, axis=-1, keepdims=True), axis=0, keepdims=True)
        dy = err * (1.0 / D)
        dg_ref[...] += _colsum(dy * nrm)
        dx_ref[...] = _rms_bwd(dy * g_ref[...], nrm, r)

    row = pl.BlockSpec((tm, D), lambda i: (i, 0))
    vec = pl.BlockSpec((1, D), lambda i: (0, 0))
    return _call(
        body, name="final_loss", grid=(s // tm,),
        in_specs=[row, vec, row],
        out_specs=[pl.BlockSpec((1, 1), lambda i: (0, 0)), row, vec],
        out_shape=[_sds((1, 1), F32), _sds((s, D), F32), _sds((1, D), F32)],
        compiler_params=_params(1),
    )(x, g, target)


def _down_bwd(dx3, gate, ffn, w_down, tm):
    s = dx3.shape[0]

    def body(dx_ref, gate_ref, ffn_ref, w_ref, dgate_ref, dff_ref, da_ref):
        i, p = pl.program_id(0), pl.program_id(1)

        @pl.when((i == 0) & (p == 0))
        def _():
            dgate_ref[...] = jnp.zeros_like(dgate_ref)

        @pl.when(p == 0)
        def _():
            dxv = dx_ref[...]
            dgate_ref[...] += _colsum(dxv * ffn_ref[...])
            dff_ref[...] = (dxv * gate_ref[...]).astype(BF16)

        da_ref[0] = _dot_nt(dff_ref[...], w_ref[0]).astype(BF16)

    row = pl.BlockSpec((tm, D), lambda i, p: (i, 0))
    vec = pl.BlockSpec((1, D), lambda i, p: (0, 0))
    return _call(
        body, name="down_bwd", grid=(s // tm, 4),
        in_specs=[row, vec, row, pl.BlockSpec((1, GU, D), lambda i, p: (p, 0, 0))],
        out_specs=[vec, row, pl.BlockSpec((1, tm, GU), lambda i, p: (p, i, 0))],
        out_shape=[_sds((1, D), F32), _sds((s, D), BF16), _sds((4, s, GU), BF16)],
        compiler_params=_params(2),
    )(dx3, gate, ffn, w_down)


def _conv_act_bwd(u, conv_w, conv_b, da, tm, after):
    s = u.shape[2]

    def body(u_ref, halo_ref, cw_ref, cb_ref, da_ref, dy_ref, dcw_ref, dcb_ref):
        first = pl.program_id(1) == 0

        @pl.when(first)
        def _():
            dcw_ref[...] = jnp.zeros_like(dcw_ref)
            dcb_ref[...] = jnp.zeros_like(dcb_ref)

        taps, ys = [], []
        for side in range(2):
            uv = u_ref[side, 0].astype(F32)
            um1, um2 = _conv_taps(uv, halo_ref[side, 0].astype(F32), first, tm)
            cw = cw_ref[side, 0]
            taps.append((um2, um1, uv))
            ys.append(cw[2:3] * uv + cw[1:2] * um1 + cw[0:1] * um2 + cb_ref[side, 0])
        dav = da_ref[0].astype(F32)
        sg = _sigmoid(ys[0])
        dys = (dav * ys[1] * (sg * (1.0 + ys[0] * (1.0 - sg))), dav * (ys[0] * sg))
        for side in range(2):
            dy_ref[side, 0] = dys[side].astype(BF16)
            dcb_ref[side, 0] += _colsum(dys[side])
            for j in range(3):
                dcw_ref[side, 0, j:j + 1, :] += _colsum(dys[side] * taps[side][j])

    u_spec, halo_spec, cw_spec, cb_spec = _conv_specs(tm)
    return _call_after(
        body, after, name="conv_act_bwd", grid=(4, s // tm),
        in_specs=[u_spec, halo_spec, cw_spec, cb_spec, pl.BlockSpec((1, tm, GU), lambda p, i: (p, i, 0))],
        out_specs=[u_spec, cw_spec, cb_spec],
        out_shape=[_sds((2, 4, s, GU), BF16), _sds((2, 4, 3, GU), F32), _sds((2, 4, 1, GU), F32)],
        compiler_params=_params(2),
    )(u, u, conv_w, conv_b, da)


def _conv_transpose(dy, conv_w, tm):
    s = dy.shape[1]
    nt = s // tm

    def body(dy_ref, halo_ref, cw_ref, du_ref):
        keep = jnp.where(pl.program_id(1) == nt - 1, 0.0, 1.0)
        dv = dy_ref[0].astype(F32)
        window = jnp.concatenate([dv[tm - 8:, :], halo_ref[0, 0:8, :].astype(F32) * keep], axis=0)
        dp1 = jnp.concatenate([pltpu.roll(dv, tm - 1, 0)[:tm - 8, :], pltpu.roll(window, 15, 0)[0:8, :]], axis=0)
        dp2 = jnp.concatenate([pltpu.roll(dv, tm - 2, 0)[:tm - 8, :], pltpu.roll(window, 14, 0)[0:8, :]], axis=0)
        cw = cw_ref[0]
        du_ref[0] = (cw[2:3] * dv + cw[1:2] * dp1 + cw[0:1] * dp2).astype(BF16)

    blk = pl.BlockSpec((1, tm, GU), lambda g, i: (g, i, 0))
    return _call(
        body, name="conv_transpose", grid=(NDEV, nt),
        in_specs=[blk, pl.BlockSpec((1, HALO, GU),
                                    lambda g, i: (g, jnp.minimum((i + 1) * (tm // HALO), s // HALO - 1), 0)),
                  pl.BlockSpec((1, 3, GU), lambda g, i: (g, 0, 0))],
        out_specs=blk,
        out_shape=_sds((NDEV, s, GU), BF16),
        compiler_params=_params(2),
    )(dy, dy, conv_w)


def _wgrad(a3, b3, *, tk, name):
    ga, s, ka = a3.shape
    gb, _, nb = b3.shape
    groups = max(ga, gb)
    nk = s // tk

    def body(a_ref, b_ref, o_ref, acc):
        k = pl.program_id(1)

        @pl.when(k == 0)
        def _():
            acc[...] = jnp.zeros_like(acc)

        acc[...] += _dot_tn(a_ref[0], b_ref[0])

        @pl.when(k == nk - 1)
        def _():
            o_ref[0] = acc[...].astype(BF16)

    a_spec = pl.BlockSpec((1, tk, ka), (lambda g, k: (g, k, 0)) if ga > 1 else (lambda g, k: (0, k, 0)))
    b_spec = pl.BlockSpec((1, tk, nb), (lambda g, k: (g, k, 0)) if gb > 1 else (lambda g, k: (0, k, 0)))
    return _call(
        body, name=name, grid=(groups, nk),
        in_specs=[a_spec, b_spec], out_specs=pl.BlockSpec((1, ka, nb), lambda g, k: (g, 0, 0)),
        out_shape=_sds((groups, ka, nb), BF16),
        scratch_shapes=[pltpu.VMEM((ka, nb), F32)],
        compiler_params=_params(2),
    )(a3, b3)


def _wgrad_in(h1, dparts, tk, after):
    s = h1.shape[0]
    nk = s // tk

    def body(a_ref, *refs):
        d_refs, o_ref, acc = refs[:NT], refs[NT], refs[NT + 1]
        k = pl.program_id(0)

        @pl.when(k == 0)
        def _():
            acc[...] = jnp.zeros_like(acc)

        for j in range(NT):
            acc[:, WG * j:WG * (j + 1)] += _dot_tn(a_ref[...], d_refs[j][...])

        @pl.when(k == nk - 1)
        def _():
            o_ref[...] = acc[...].astype(BF16)

    kw = dict(
        name="wgrad_in", grid=(nk,),
        in_specs=[pl.BlockSpec((tk, D), lambda k: (k, 0))] + [pl.BlockSpec((tk, WG), lambda k: (k, 0))] * NT,
        out_specs=pl.BlockSpec((D, NT * WG), lambda k: (0, 0)),
        out_shape=_sds((D, NT * WG), BF16),
        scratch_shapes=[pltpu.VMEM((D, NT * WG), F32)],
        compiler_params=_params(1),
    )
    call = _call(body, **kw) if after is None else _call_after(body, after, **kw)
    return call(h1, *dparts)


def _dgrad_in(dparts, w, x_in, scale, dx_up, tm, after):
    s = x_in.shape[0]

    def body(*refs):
        d_refs = refs[:NT]
        w_ref, x_ref, sc_ref, up_ref, dx_ref, dsc_ref, dsh_ref = refs[NT:]

        @pl.when(pl.program_id(0) == 0)
        def _():
            dsc_ref[...] = jnp.zeros_like(dsc_ref)
            dsh_ref[...] = jnp.zeros_like(dsh_ref)

        dh = _dot_nt(d_refs[0][...], w_ref[:, 0:WG])
        for j in range(1, NT):
            dh = dh + _dot_nt(d_refs[j][...], w_ref[:, WG * j:WG * (j + 1)])
        xv = x_ref[...]
        r = _rms(xv)
        nrm = xv * r
        dsh_ref[...] += _colsum(dh)
        dsc_ref[...] += _colsum(dh * nrm)
        dx_ref[...] = up_ref[...] + _rms_bwd(dh * (1.0 + sc_ref[...]), nrm, r)

    row = pl.BlockSpec((tm, D), lambda i: (i, 0))
    vec = pl.BlockSpec((1, D), lambda i: (0, 0))
    return _call_after(
        body, after, name="dgrad_in", grid=(s // tm,),
        in_specs=[pl.BlockSpec((tm, WG), lambda i: (i, 0))] * NT + [pl.BlockSpec((D, NT * WG), lambda i: (0, 0)), row, vec, row],
        out_specs=[row, vec, vec],
        out_shape=[_sds((s, D), F32), _sds((1, D), F32), _sds((1, D), F32)],
        compiler_params=_params(1),
    )(*dparts, w, x_in, scale, dx_up)


def _dgrad_norm_bwd(d3, w, x_in, scale, dx_up, *, tm, name, after):
    groups, s, n = d3.shape

    def body(d_ref, w_ref, x_ref, sc_ref, up_ref, dx_ref, dsc_ref, dsh_ref, acc):
        i, g = pl.program_id(0), pl.program_id(1)

        @pl.when((i == 0) & (g == 0))
        def _():
            dsc_ref[...] = jnp.zeros_like(dsc_ref)
            dsh_ref[...] = jnp.zeros_like(dsh_ref)

        part = _dot(d_ref[0], w_ref[0])

        @pl.when(g == 0)
        def _():
            acc[...] = part

        @pl.when(g > 0)
        def _():
            acc[...] += part

        @pl.when(g == groups - 1)
        def _():
            dh = acc[...]
            xv = x_ref[...]
            r = _rms(xv)
            nrm = xv * r
            dsh_ref[...] += _colsum(dh)
            dsc_ref[...] += _colsum(dh * nrm)
            dx_ref[...] = up_ref[...] + _rms_bwd(dh * (1.0 + sc_ref[...]), nrm, r)

    row = pl.BlockSpec((tm, D), lambda i, g: (i, 0))
    vec = pl.BlockSpec((1, D), lambda i, g: (0, 0))
    return _call_after(
        body, after, name=name, grid=(s // tm, groups),
        in_specs=[pl.BlockSpec((1, tm, n), lambda i, g: (g, i, 0)), pl.BlockSpec((1, n, D), lambda i, g: (g, 0, 0)),
                  row, vec, row],
        out_specs=[row, vec, vec],
        out_shape=[_sds((s, D), F32), _sds((1, D), F32), _sds((1, D), F32)],
        scratch_shapes=[pltpu.VMEM((tm, D), F32)],
        compiler_params=_params(2),
    )(d3, w, x_in, scale, dx_up)


def _mix_out_bwd(dx2, mixed, gate, w_out, oa, ob, g, tm):
    s = dx2.shape[0]

    def body(dx_ref, mixed_ref, gate_ref, w_ref, oa_ref, ob_ref, g_ref, dgate_ref, dm_ref, doa_ref, dob_ref, dg_ref):
        @pl.when(pl.program_id(0) == 0)
        def _():
            dgate_ref[...] = jnp.zeros_like(dgate_ref)
            dg_ref[...] = jnp.zeros_like(dg_ref)
        dxv = dx_ref[...]
        dgate_ref[...] += _colsum(dxv * mixed_ref[...])
        dm_ref[...] = (dxv * gate_ref[...]).astype(BF16)
        dnab = _dot_nt(dm_ref[...], w_ref[...])
        for o_ref, do_ref, sl in ((oa_ref, doa_ref, slice(0, WG)), (ob_ref, dob_ref, slice(WG, D))):
            ov = o_ref[...]
            r = _rms(ov)
            nrm = ov * r
            dn = dnab[:, sl]
            dg_ref[:, sl] += _colsum(dn * nrm)
            do_ref[...] = _rms_bwd(dn * g_ref[:, sl], nrm, r)

    row = pl.BlockSpec((tm, D), lambda i: (i, 0))
    half = pl.BlockSpec((tm, WG), lambda i: (i, 0))
    vec = pl.BlockSpec((1, D), lambda i: (0, 0))
    return _call(
        body, name="mix_out_bwd", grid=(s // tm,),
        in_specs=[row, row, vec, pl.BlockSpec((D, D), lambda i: (0, 0)), half, half, vec],
        out_specs=[vec, row, half, half, vec],
        out_shape=[_sds((1, D), F32), _sds((s, D), BF16), _sds((s, WG), F32), _sds((s, WG), F32), _sds((1, D), F32)],
        compiler_params=_params(1),
    )(dx2, mixed, gate, w_out, oa, ob, g)


def _attn_a_bwd(proj, bias_tab, d_oa, after):
    s = proj.shape[1]
    nq = s // TQA
    nkb = s // 128

    def body(q_ref, k_ref, v_ref, b_ref, do_ref, dq_ref, dk_ref, dv_ref, db_ref, dkt_acc, dvt_acc):
        qi = pl.program_id(1)

        @pl.when(qi == 0)
        def _():
            dkt_acc[...] = jnp.zeros_like(dkt_acc)
            dvt_acc[...] = jnp.zeros_like(dvt_acc)
            db_ref[...] = jnp.zeros_like(db_ref)

        heads = [slice(HD * h, HD * (h + 1)) for h in range(HA)]
        qs = [q_ref[0, :, sl] for sl in heads]
        dos = [do_ref[:, sl].astype(BF16) for sl in heads]
        qts = [_transpose_bf16(x, _eye(HD)) for x in qs]
        dots = [_transpose_bf16(x, _eye(HD)) for x in dos]

        def tile(nk, off, kstart):
            kb0 = kstart // 128
            for h in range(HA):
                sl = heads[h]
                k = k_ref[0, pl.ds(kstart, nk), sl]
                v = v_ref[0, pl.ds(kstart, nk), sl]
                sc = _dot_nt(qs[h], k) * 0.125 + b_ref[h, :, off:off + nk]
                p = jnp.exp(sc - jnp.max(sc, axis=-1, keepdims=True))
                p = p / jnp.sum(p, axis=-1, keepdims=True)
                dp = _dot_nt(dos[h], v)
                ds = p * (dp - jnp.sum(dp * p, axis=-1, keepdims=True))
                db_ref[h, :, off:off + nk] += ds
                dsb = (ds * 0.125).astype(BF16)
                dq_ref[:, sl] = _dot(dsb, k).astype(BF16)
                dkt = _dot(qts[h], dsb)
                dvt = _dot(dots[h], p.astype(BF16))
                for j in range(nk // 128):
                    dkt_acc[h * nkb + kb0 + j] += dkt[:, 128 * j:128 * (j + 1)]
                    dvt_acc[h * nkb + kb0 + j] += dvt[:, 128 * j:128 * (j + 1)]

        _attn_a_cases(qi, tile)

        @pl.when(qi == nq - 1)
        def _():
            eye = _eye(128)
            for h in range(HA):
                for kb in range(nkb):
                    rows_kb = slice(128 * kb, 128 * (kb + 1))
                    dk_ref[rows_kb, heads[h]] = _transpose_f32(dkt_acc[h * nkb + kb], eye).astype(BF16)
                    dv_ref[rows_kb, heads[h]] = _transpose_f32(dvt_acc[h * nkb + kb], eye).astype(BF16)

    q_spec, k_spec, v_spec, b_spec = _attn_a_specs(s)
    blk = pl.BlockSpec((TQA, WA), lambda hp, qi: (qi, hp))
    col = pl.BlockSpec((s, WA), lambda hp, qi: (0, hp))
    return _call_after(
        body, after, name="attn_a_bwd", grid=(WG // WA, nq),
        in_specs=[q_spec, k_spec, v_spec, b_spec, blk],
        out_specs=[blk, col, col, b_spec],
        out_shape=[_sds((s, WG), BF16), _sds((s, WG), BF16), _sds((s, WG), BF16), _sds((NDEV, TQA, KWA), F32)],
        scratch_shapes=[pltpu.VMEM((HA * nkb, HD, 128), F32), pltpu.VMEM((HA * nkb, HD, 128), F32)],
        compiler_params=_params(2),
    )(proj, proj, proj, bias_tab, d_oa)


def _attn_b_bwd(proj, lsum, d_ob):
    s = proj.shape[1]
    nq = s // TQB
    nkb = s // TKB
    ndiag = TQB // TKB

    def body(q_ref, k_ref, v_ref, ls_ref, do_ref, dq_ref, dk_ref, dv_ref, dkt_acc, dvt_acc):
        qi = pl.program_id(1)
        q0 = qi * TQB

        @pl.when(qi == 0)
        def _():
            dkt_acc[...] = jnp.zeros_like(dkt_acc)
            dvt_acc[...] = jnp.zeros_like(dvt_acc)

        rows = lax.broadcasted_iota(jnp.int32, (TQB, TKB), 0)
        cols = lax.broadcasted_iota(jnp.int32, (TQB, TKB), 1)
        uj = jnp.bitwise_and(lax.broadcasted_iota(jnp.int32, (2 * TKB, TKB), 0), TKB - 1)
        us = lax.broadcasted_iota(jnp.int32, (2 * TKB, TKB), 1)
        prefix = jnp.where(uj <= us, 1.0, 0.0).astype(BF16)
        heads = [slice(HD * h, HD * (h + 1)) for h in range(2)]
        qs = [(q_ref[0, :, sl].astype(F32) * 0.125).astype(BF16) for sl in heads]
        dos = [do_ref[:, sl].astype(BF16) for sl in heads]
        qts = [_transpose_bf16(x, _eye(HD)) for x in qs]
        dots = [_transpose_bf16(x, _eye(HD)) for x in dos]
        stots = [ls_ref[0, :, HD * h:HD * h + 1] for h in range(2)]
        qds = [jnp.concatenate([qs[h], dos[h]], axis=1) for h in range(2)]
        zkv = jnp.zeros((TKB, HD), BF16)

        def tile(h, kb, diag, cl, cg, dq):
            r0 = 0 if diag is None else diag * TKB
            sl = heads[h]
            kstart = pl.multiple_of(kb * TKB, TKB)
            k = k_ref[0, pl.ds(kstart, TKB), sl]
            v = v_ref[0, pl.ds(kstart, TKB), sl]
            kv = jnp.concatenate([jnp.concatenate([k, zkv], axis=1), jnp.concatenate([zkv, v], axis=1)], axis=0)
            both = _dot_nt(qds[h][r0:, :], kv)
            lb, sp = _sb_terms(both[:, :TKB])
            if diag is not None:
                strict = rows[:TQB - r0, :] > cols[:TQB - r0, :]
                sp = jnp.where(strict, sp, 0.0)
            pre = _dot2_deep(sp, prefix) + cl[r0:, :]
            a = jnp.exp(lb - stots[h][r0:, :] + pre)
            if diag is not None:
                a = jnp.where(strict, a, 0.0)
            gz = both[:, TKB:] * a
            pg = _dot2_deep(gz, prefix) + cg[r0:, :]
            dl = gz - pg * jnp.exp(lb)
            if diag is not None:
                dl = jnp.where(strict, dl, 0.0)
            dlb = dl.astype(BF16)
            dkt_acc[h * nkb + kb] += _dot(qts[h][:, r0:], dlb)
            dvt_acc[h * nkb + kb] += _dot(dots[h][:, r0:], a.astype(BF16))
            new = (pre[:, TKB - 1:TKB], pg[:, TKB - 1:TKB], dq[r0:, :] + _dot(dlb, k))
            if r0:
                new = tuple(jnp.concatenate([old[:r0, :], x], axis=0) for old, x in zip((cl, cg, dq), new))
            return new

        state = [jnp.zeros((TQB, 1), F32), jnp.zeros((TQB, 1), F32), jnp.zeros((TQB, HD), F32)] * 2

        def step(i, st):
            st = list(st)
            for sub in range(KSTEP):
                for h in range(2):
                    st[3 * h:3 * h + 3] = tile(h, i * KSTEP + sub, None, *st[3 * h:3 * h + 3])
            return tuple(st)

        state = list(lax.fori_loop(0, q0 // (KSTEP * TKB), step, tuple(state)))
        for d in range(ndiag):
            for h in range(2):
                state[3 * h:3 * h + 3] = tile(h, q0 // TKB + d, d, *state[3 * h:3 * h + 3])
        for h in range(2):
            dq_ref[:, heads[h]] = (state[3 * h + 2] * 0.125).astype(BF16)

        @pl.when(qi == nq - 1)
        def _():
            eye = _eye(TKB)
            for h in range(2):
                for kb in range(nkb):
                    rows_kb = slice(kb * TKB, (kb + 1) * TKB)
                    dk_ref[rows_kb, heads[h]] = _transpose_f32(dkt_acc[h * nkb + kb], eye).astype(BF16)
                    dv_ref[rows_kb, heads[h]] = _transpose_f32(dvt_acc[h * nkb + kb], eye).astype(BF16)

    q_spec, k_spec, v_spec = _attn_b_specs(s)
    blk = pl.BlockSpec((TQB, 128), lambda hp, qi: (qi, hp))
    col = pl.BlockSpec((s, 128), lambda hp, qi: (0, hp))
    return _call(
        body, name="attn_b_bwd", grid=(4, nq),
        in_specs=[q_spec, k_spec, v_spec, pl.BlockSpec((1, TQB, 128), lambda hp, qi: (hp, qi, 0)), blk],
        out_specs=[blk, col, col],
        out_shape=[_sds((s, WG), BF16)] * 3,
        scratch_shapes=[pltpu.VMEM((2 * nkb, HD, TKB), F32), pltpu.VMEM((2 * nkb, HD, TKB), F32)],
        compiler_params=_params(2),
    )(proj, proj, proj, lsum, d_ob)


def _bias_fold(dtab):
    def body(t_ref, d_ref, far_ref):
        acc = jnp.zeros((NDEV, TABW), F32)
        zpad = jnp.zeros((NDEV, TAB0), F32)
        for r in range(TQA):
            row = jnp.concatenate([zpad, t_ref[:, r, :]], axis=1)
            acc = acc + (pltpu.roll(row, TABW - r, 1) if r else row)
        d_ref[...] = acc
        lane = lax.broadcasted_iota(jnp.int32, (NDEV, TABW), 1)
        far = jnp.sum(jnp.where(lane < N_FAR, acc, 0.0), axis=1, keepdims=True)
        far_ref[...] = jnp.broadcast_to(far, (NDEV, 128))

    d_fpad, d_far = _call(
        body, name="bias_fold", grid=(1,),
        in_specs=[pl.BlockSpec((NDEV, TQA, KWA), lambda i: (0, 0, 0))],
        out_specs=[pl.BlockSpec((NDEV, TABW), lambda i: (0, 0)), pl.BlockSpec((NDEV, 128), lambda i: (0, 0))],
        out_shape=[_sds((NDEV, TABW), F32), _sds((NDEV, 128), F32)],
        compiler_params=_params(1),
    )(dtab)
    d_near = d_fpad[:, N_FAR:N_FAR + N_NEAR][:, ::-1]
    return jnp.concatenate([jnp.zeros((NDEV, REL_CLIP - CHUNK + 1), F32), d_near, d_far[:, :1]], axis=1)


def _adamw(w, g, m, v):
    m = B1 * m + (1.0 - B1) * g
    v = B2 * v + (1.0 - B2) * (g * g)
    m_hat = m / (1.0 - B1 ** STEP)
    v_hat = v / (1.0 - B2 ** STEP)
    delta = -LR * (m_hat / (jnp.sqrt(v_hat) + AEPS) + WD * w)
    return delta, m, v


def _adamw_big(recv0, recv1, w, m, v):
    _, rows, cols = recv0.shape
    tr = max(t for t in range(16, 513, 16) if rows % t == 0)
    nt = rows // tr

    def body(r0_ref, r1_ref, w_ref, m_ref, v_ref, g_ref, d_ref, nm_ref, nv_ref):
        def update(r_ref):
            g = r_ref[0].astype(F32)
            for p in range(1, NDEV):
                g = g + r_ref[p].astype(F32)
            delta, nm, nv = _adamw(w_ref[0], g, m_ref[0], v_ref[0])
            g_ref[0], d_ref[0], nm_ref[0], nv_ref[0] = g, delta, nm, nv

        @pl.when(pl.program_id(0) == 0)
        def _():
            update(r0_ref)

        @pl.when(pl.program_id(0) == 1)
        def _():
            update(r1_ref)

    blk = pl.BlockSpec((1, tr, cols), lambda l, i: (l, i, 0))
    r0_spec = pl.BlockSpec((NDEV, tr, cols), lambda l, i: (0, jnp.where(l == 0, i, nt - 1), 0))
    r1_spec = pl.BlockSpec((NDEV, tr, cols), lambda l, i: (0, jnp.where(l == 1, i, 0), 0))
    return _call(
        body, name="adamw_big", grid=(2, nt),
        in_specs=[r0_spec, r1_spec, blk, blk, blk],
        out_specs=[blk] * 4,
        out_shape=[_sds((2, rows, cols), F32)] * 4,
        compiler_params=_params(2),
    )(recv0, recv1, w, m, v)


def _adamw_w_ada(cact_t, dmod, w, m, v):
    tr = 256

    def body(c_ref, dm_ref, w_ref, m_ref, v_ref, g_ref, d_ref, nm_ref, nv_ref):
        g = c_ref[:, 0:1] * dm_ref[0, 0:1, :]
        for b in range(1, NDEV):
            g = g + c_ref[:, b:b + 1] * dm_ref[0, b:b + 1, :]
        delta, nm, nv = _adamw(w_ref[0], g, m_ref[0], v_ref[0])
        g_ref[0], d_ref[0], nm_ref[0], nv_ref[0] = g, delta, nm, nv

    blk = pl.BlockSpec((1, tr, ADA_SH), lambda l, i: (l, i, 0))
    return _call(
        body, name="adamw_w_ada", grid=(2, D // tr),
        in_specs=[pl.BlockSpec((tr, NDEV), lambda l, i: (i, 0)), pl.BlockSpec((1, NDEV, ADA_SH), lambda l, i: (l, 0, 0)),
                  blk, blk, blk],
        out_specs=[blk] * 4,
        out_shape=[_sds((2, D, ADA_SH), F32)] * 4,
        compiler_params=_params(2),
    )(cact_t, dmod, w, m, v)


def _adamw_small(gath, w, m, v):
    rows = gath.shape[1]

    def body(r_ref, w_ref, m_ref, v_ref, g_ref, d_ref, nm_ref, nv_ref):
        g = r_ref[0]
        for p in range(1, NDEV):
            g = g + r_ref[p]
        delta, nm, nv = _adamw(w_ref[...], g, m_ref[...], v_ref[...])
        g_ref[...], d_ref[...], nm_ref[...], nv_ref[...] = g, delta, nm, nv

    blk = pl.BlockSpec((rows, D), lambda i: (0, 0))
    return _call(
        body, name="adamw_small", grid=(1,),
        in_specs=[pl.BlockSpec((NDEV, rows, D), lambda i: (0, 0, 0)), blk, blk, blk],
        out_specs=[blk] * 4,
        out_shape=[_sds((rows, D), F32)] * 4,
        compiler_params=_params(1),
    )(gath, w, m, v)


_PACK = (("b_ada", 2 * 6 * D), ("rel_bias", 2 * 8 * 257), ("g_a", 2 * WG), ("g_b", 2 * WG),
         ("conv_b", 2 * 2 * DFF), ("final_g", D), ("conv_w", 2 * NDEV * 3 * GU))


def _pack(parts):
    rows = []
    for name, size in _PACK:
        flat = parts[name].reshape(-1).astype(F32)
        assert flat.shape[0] == size, (name, flat.shape)
        rows.append(jnp.pad(flat, (0, -size % D)))
    out = jnp.concatenate(rows).reshape(-1, D)
    return jnp.pad(out, ((0, -out.shape[0] % 8), (0, 0)))


def _unpack(packed):
    flat = packed.reshape(-1)
    out, pos = {}, 0
    for name, size in _PACK:
        out[name] = flat[pos:pos + size]
        pos += size + (-size % D)
    return out


def kernel(x, c, w_ada, b_ada, w_in, rel_bias, g_a, g_b, w_out, w_up, conv_w, conv_b, w_down, final_g, loss_target, m_w_ada, m_b_ada, m_w_in, m_rel_bias, m_g_a, m_g_b, m_w_out, m_w_up, m_conv_w, m_conv_b, m_w_down, m_final_g, v_w_ada, v_b_ada, v_w_in, v_rel_bias, v_g_a, v_g_b, v_w_out, v_w_up, v_conv_w, v_conv_b, v_w_down, v_final_g):
    s = x.shape[1]
    assert s % TQA == 0 and s >= KWA and s % 512 == 0
    tm = 512
    tmm = min(1024, s)
    me = 4 * lax.axis_index("x") + 2 * lax.axis_index("y") + lax.axis_index("c")
    xs = x.reshape(s, D)
    target = loss_target.reshape(s, D)

    first = jnp.concatenate([c, jnp.pad(conv_w.reshape(2 * 3, GU), ((0, 1), (0, D - GU)))])
    first_all = _small_allgather(first, "gather_c_conv_w")
    c_all = first_all[:, 0, :]
    cw_all = first_all[:, 1:7, :GU].reshape(NDEV, 2, 3, GU)

    b_sl = lax.dynamic_slice(b_ada, (0, me * ADA_SH), (2, ADA_SH)).reshape(2, 1, ADA_SH)
    mod_part, cact = _mod_fwd(c_all, w_ada, b_sl)
    mod_all = _small_allgather(mod_part.reshape(2 * NDEV, ADA_SH), "gather_mod")

    up_t = [jnp.transpose(t, (0, 2, 1)) for t in (w_up, m_w_up, v_w_up)]
    shards = {"in": w_in, "out": w_out, "up": up_t[0], "down": w_down}
    order = [(kind, l) for l in range(2) for kind in _KINDS]
    mod_all, *srcs = lax.optimization_barrier((mod_all, *[shards[kind][l].astype(BF16) for kind, l in order]))
    gather_started = dict(zip(order, _exchange_start([kind for kind, _ in order], True, srcs, "weights_gather_start")))

    def gathered(kind, l, after):
        return _exchange_wait([kind], True, [gather_started[kind, l]], after, f"weights_gather_wait_{kind}{l}")[0]

    mod_all = mod_all.reshape(NDEV, 2, NDEV, ADA_SH)
    mod_me = lax.dynamic_index_in_dim(mod_all, me, axis=2, keepdims=False)
    mod = jnp.transpose(mod_me, (1, 0, 2)).reshape(2, 6, 1, D)

    saved = []
    xl = xs
    for l in range(2):
        sh_mix, sc_mix, gt_mix, sh_ffn, sc_ffn, gt_ffn = (mod[l, j] for j in range(6))
        cw = cw_all[:, l].reshape(2, 4, 3, GU)
        cb = conv_b[l].reshape(2, 4, 1, GU)
        gvec = jnp.concatenate([g_a[l], g_b[l]]).reshape(1, D)
        tab = _bias_table(rel_bias[l])

        wi = gathered("in", l, xl if l else mod)
        h1, proj = _nm_matmul(xl, sh_mix, sc_mix, wi, two_d=True, n=WG, groups=NT, out_dtype=BF16, tm=tmm,
                              name="norm_proj")
        oa = _attn_a_fwd(proj, tab)
        ob, lsum = _attn_b_fwd(proj)
        wo = gathered("out", l, ob)
        nab, mixed, x2 = _mix_out(oa, ob, gvec, wo, xl, gt_mix, tm)
        wu = gathered("up", l, x2)
        h2, u = _nm_matmul(x2, sh_ffn, sc_ffn, wu, two_d=False, n=GU, groups=NDEV, out_dtype=BF16, tm=tmm,
                           name="norm_up")
        u = u.reshape(2, 4, s, GU)
        a = _conv_act(u, cw, cb, tmm)
        wd4 = gathered("down", l, a).reshape(4, GU, D)
        ffn, x3 = _down(a, wd4, x2, gt_ffn, tmm)
        saved.append(dict(x=xl, h1=h1, proj=proj, oa=oa, ob=ob, lsum=lsum, nab=nab, mixed=mixed, x2=x2, h2=h2, u=u,
                          a=a, ffn=ffn, cw=cw, cb=cb, gvec=gvec, tab=tab, wd4=wd4, wi=wi, wo=wo, wu=wu))
        xl = x3

    loss_part, dx, d_final_g = _final_loss(xl, final_g.reshape(1, D), target, tm)
    loss = lax.psum(loss_part[0, 0], ("x", "y", "c"))

    sent = {}
    small = {"b_ada": [None, None], "rel_bias": [None, None], "g_a": [None, None], "g_b": [None, None],
             "conv_b": [None, None], "conv_w": [None, None]}

    def send(kind, l, grad):
        started, token = _exchange_start([kind], False, [grad], f"grads_start_{kind}{l}", with_token=True)
        sent[kind, l] = started[0]
        return token

    for l in (1, 0):
        sv = saved[l]
        sh_mix, sc_mix, gt_mix, sh_ffn, sc_ffn, gt_ffn = (mod[l, j] for j in range(6))
        d_gt_ffn, dff, da = _down_bwd(dx, gt_ffn, sv["ffn"], sv["wd4"], tmm)
        tok = send("down", l, _wgrad(sv["a"], dff.reshape(1, s, D), tk=tmm, name="wgrad_down").reshape(DFF, D))
        dy, d_cw, d_cb = _conv_act_bwd(sv["u"], sv["cw"], sv["cb"], da, tmm, tok)
        du = _conv_transpose(dy.reshape(NDEV, s, GU), sv["cw"].reshape(NDEV, 3, GU), tmm)
        tok = send("up", l, _wgrad(du, sv["h2"].reshape(1, s, D), tk=tmm, name="wgrad_up"))
        dx2, d_sc_ffn, d_sh_ffn = _dgrad_norm_bwd(du, sv["wu"], sv["x2"], sc_ffn, dx, tm=tmm, name="dgrad_up", after=tok)
        d_gt_mix, dmixed, d_oa, d_ob, d_g = _mix_out_bwd(dx2, sv["mixed"], gt_mix, sv["wo"], sv["oa"], sv["ob"],
                                                         sv["gvec"], tm)
        tok = send("out", l, _wgrad(sv["nab"].reshape(1, s, D), dmixed.reshape(1, s, D), tk=tmm,
                                    name="wgrad_out").reshape(D, D))
        dqa, dka, dva, d_tab = _attn_a_bwd(sv["proj"], sv["tab"], d_oa, tok)
        dqb, dkb, dvb = _attn_b_bwd(sv["proj"], sv["lsum"], d_ob)
        dparts = (dqa, dka, dva, dqb, dkb, dvb)
        if l:
            tok = send("in", l, _wgrad_in(sv["h1"], dparts, tmm, None))
        dx, d_sc_mix, d_sh_mix = _dgrad_in(dparts, sv["wi"], sv["x"], sc_mix, dx2, tm, tok)
        small["b_ada"][l] = jnp.concatenate([d_sh_mix, d_sc_mix, d_gt_mix, d_sh_ffn, d_sc_ffn, d_gt_ffn], axis=1)
        small["rel_bias"][l] = _bias_fold(d_tab)
        small["g_a"][l], small["g_b"][l] = d_g[:, :WG], d_g[:, WG:]
        small["conv_b"][l] = d_cb
        small["conv_w"][l] = d_cw.reshape(NDEV, 3, GU)
    grad_x = dx.reshape(1, s, D)

    contrib = {k: jnp.stack(vs) for k, vs in small.items()}
    contrib["final_g"] = d_final_g
    gath = _small_allgather(_pack(contrib), "gather_small_grads")
    tok = send("in", 0, _wgrad_in(saved[0]["h1"], dparts, tmm, gath))

    def place_conv_w(t):
        return lax.dynamic_update_slice(jnp.zeros((2, NDEV, 3, GU), F32), t.reshape(2, 1, 3, GU), (0, me, 0, 0))

    def packed_params(b, rb, ga, gb, cb_, fg, cw_):
        return _pack({"b_ada": b, "rel_bias": rb, "g_a": ga, "g_b": gb, "conv_b": cb_, "final_g": fg,
                      "conv_w": place_conv_w(cw_)})

    sm = _adamw_small(gath,
                      packed_params(b_ada, rel_bias, g_a, g_b, conv_b, final_g, conv_w),
                      packed_params(m_b_ada, m_rel_bias, m_g_a, m_g_b, m_conv_b, m_final_g, m_conv_w),
                      packed_params(v_b_ada, v_rel_bias, v_g_a, v_g_b, v_conv_b, v_final_g, v_conv_w))
    sm = [_unpack(t) for t in sm]

    dmod_all = gath[:, :12, :].reshape(NDEV, 2, 6 * D)
    dmod_sl = jnp.transpose(lax.dynamic_slice(dmod_all, (0, 0, me * ADA_SH), (NDEV, 2, ADA_SH)), (1, 0, 2))
    ada = _adamw_w_ada(cact.T, dmod_sl, w_ada, m_w_ada, v_w_ada)

    big = {}
    for kind, (w, m, v) in (("down", (w_down, m_w_down, v_w_down)), ("up", up_t),
                            ("out", (w_out, m_w_out, v_w_out)), ("in", (w_in, m_w_in, v_w_in))):
        recv0, recv1 = _exchange_wait([kind, kind], False, [sent[kind, 0], sent[kind, 1]], tok, f"grads_wait_{kind}")
        big[kind] = _adamw_big(recv0, recv1, w, m, v)
        tok = big[kind][0]
    big["up"] = [jnp.transpose(t, (0, 2, 1)) for t in big["up"]]

    def small_out(j, name):
        t = sm[j][name]
        if name == "b_ada":
            return t.reshape(2, 6 * D)
        if name == "rel_bias":
            return t.reshape(2, 8, 257)
        if name in ("g_a", "g_b"):
            return t.reshape(2, WG)
        if name == "conv_b":
            return t.reshape(2, 2 * DFF)
        if name == "final_g":
            return t.reshape(D)
        t = t.reshape(2, NDEV, 3, GU)
        return lax.dynamic_index_in_dim(t, me, axis=1, keepdims=False)

    def group(j):
        return (ada[j], small_out(j, "b_ada"), big["in"][j], small_out(j, "rel_bias"), small_out(j, "g_a"),
                small_out(j, "g_b"), big["out"][j], big["up"][j], small_out(j, "conv_w"), small_out(j, "conv_b"),
                big["down"][j], small_out(j, "final_g"))

    return (loss, grad_x, *group(0), *group(1), *group(2), *group(3))
```

```python
import jax
import jax.numpy as jnp
from jax import lax
from jax.experimental import pallas as pl
from jax.experimental.pallas import tpu as pltpu

F32, BF16 = jnp.float32, jnp.bfloat16
MESH_ID = pl.DeviceIdType.MESH
NDEV = 8
D = 1024
HD = 64
WG = 512
NT = 6
GU = 704
DFF = 2816
IN_SH = NT * WG // NDEV
OUT_SH = D // NDEV
DOWN_SH = DFF // NDEV
ADA_SH = 6 * D // NDEV
LANES = 128
CHUNK, NPREV, REL_CLIP = 64, 8, 128
BAND = (NPREV + 1) * CHUNK
EPS = 1e-6
NEG = -1e30
TQA = 256
KWA = TQA + NPREV * CHUNK
TABW = 1024
TAB0 = TABW - KWA
N_FAR = TAB0 + NPREV * CHUNK - REL_CLIP + 1
N_NEAR = REL_CLIP + CHUNK - 1
TQB, TKB = 512, 128
KSTEP = 4
LR, B1, B2, AEPS, WD, STEP = 0.001, 0.9, 0.999, 1e-08, 0.01, 10
VMEM_MB = 56


def _call(body, **kw):
    return pl.pallas_call(body, **kw)


def _call_after(body, after, **kw):
    n_in = len(kw["in_specs"])
    kw["in_specs"] = list(kw["in_specs"]) + [pl.BlockSpec(memory_space=pl.ANY)]

    def tied(*refs):
        body(*refs[:n_in], *refs[n_in + 1:])

    call = _call(tied, **kw)
    return lambda *args: call(*args, after)


def _params(n_axes):
    return pltpu.CompilerParams(dimension_semantics=("arbitrary",) * n_axes, vmem_limit_bytes=VMEM_MB << 20)


def _dot(a, b):
    return jnp.dot(a, b, preferred_element_type=F32)


def _dot_nt(a, b):
    return lax.dot_general(a, b, (((1,), (1,)), ((), ())), preferred_element_type=F32)


def _dot_tn(a, b):
    return lax.dot_general(a, b, (((0,), (0,)), ((), ())), preferred_element_type=F32)


def _dot2(x, u):
    hi = x.astype(BF16)
    lo = (x - hi.astype(F32)).astype(BF16)
    return _dot(hi, u) + _dot(lo, u)


def _dot2_deep(x, uu):
    hi = x.astype(BF16)
    lo = (x - hi.astype(F32)).astype(BF16)
    return _dot(jnp.concatenate([hi, lo], axis=1), uu)


def _eye(n):
    i = lax.broadcasted_iota(jnp.int32, (n, n), 0)
    j = lax.broadcasted_iota(jnp.int32, (n, n), 1)
    return jnp.where(i == j, 1.0, 0.0).astype(BF16)


def _transpose_bf16(x, eye):
    return _dot_nt(eye, x).astype(BF16)


def _transpose_f32(x, eye):
    hi = x.astype(BF16)
    lo = (x - hi.astype(F32)).astype(BF16)
    return _dot_nt(eye, hi) + _dot_nt(eye, lo)


def _rms(x):
    return lax.rsqrt(jnp.mean(x * x, axis=-1, keepdims=True) + EPS)


def _rms_bwd(dn, n, r):
    return r * (dn - n * jnp.mean(dn * n, axis=-1, keepdims=True))


def _colsum(x):
    return jnp.sum(x, axis=0, keepdims=True)


def _sigmoid(x):
    return 1.0 / (1.0 + jnp.exp(-x))


def _sds(shape, dtype):
    return jax.ShapeDtypeStruct(shape, dtype)


def _place():
    x, y, c = lax.axis_index("x"), lax.axis_index("y"), lax.axis_index("c")
    return x, y, c, 4 * x + 2 * y + c


def _peer(x, y, c, k):
    px = 1 - x if k & 4 else x
    py = 1 - y if k & 2 else y
    pc = 1 - c if k & 1 else c
    return (px, py, pc), 4 * px + 2 * py + pc


def _small_allgather(v, name):
    rows, cols = v.shape

    def body(v_ref, out_ref, send_sems, recv_sems, local_sem):
        x, y, c, me = _place()
        mine = pltpu.make_async_copy(v_ref, out_ref.at[me], local_sem)
        mine.start()
        sends = []
        for k in range(1, NDEV):
            peer, _ = _peer(x, y, c, k)
            cp = pltpu.make_async_remote_copy(v_ref, out_ref.at[me], send_sems.at[k - 1], recv_sems.at[k - 1],
                                              device_id=peer, device_id_type=MESH_ID)
            cp.start()
            sends.append(cp)
        for k in range(1, NDEV):
            peer, pidx = _peer(x, y, c, k)
            pltpu.make_async_remote_copy(v_ref, out_ref.at[pidx], send_sems.at[k - 1], recv_sems.at[k - 1],
                                         device_id=peer, device_id_type=MESH_ID).wait_recv()
        for cp in sends:
            cp.wait_send()
        mine.wait()

    return _call(
        body, name=name,
        out_shape=_sds((NDEV, rows, cols), F32),
        in_specs=[pl.BlockSpec(memory_space=pltpu.VMEM)],
        out_specs=pl.BlockSpec(memory_space=pltpu.VMEM),
        scratch_shapes=[pltpu.SemaphoreType.DMA((NDEV - 1,)), pltpu.SemaphoreType.DMA((NDEV - 1,)),
                        pltpu.SemaphoreType.DMA],
    )(v)


def _shard_view(ref, kind, p):
    if kind == "in":
        return ref.at[:, pl.ds(pl.multiple_of(p * IN_SH, LANES), IN_SH)]
    if kind == "out":
        return ref.at[pl.ds(pl.multiple_of(p * OUT_SH, OUT_SH), OUT_SH), :]
    if kind == "up":
        return ref.at[p]
    if kind == "down":
        return ref.at[pl.ds(pl.multiple_of(p * DOWN_SH, 16), DOWN_SH), :]
    raise ValueError(kind)


_KINDS = ("in", "out", "up", "down")
_FULL_SHAPES = {"in": (D, 3 * D), "out": (D, D), "up": (NDEV, GU, D), "down": (DFF, D)}
_SHARD_SHAPES = {"in": (D, IN_SH), "out": (OUT_SH, D), "up": (GU, D), "down": (DOWN_SH, D)}


_HBM = pl.BlockSpec(memory_space=pltpu.HBM)
_SEM = pl.BlockSpec(memory_space=pltpu.SEMAPHORE)
_EFFECT = pltpu.SideEffectType.DATAFLOW_SIDE_EFFECTING
_SEM_SHAPES = (pltpu.SemaphoreType.DMA((NDEV - 1,)), pltpu.SemaphoreType.DMA((NDEV - 1,)), pltpu.SemaphoreType.DMA(()))


def _hbm(a):
    return pltpu.with_memory_space_constraint(a, pltpu.HBM)


def _exchange_copies(kind, gather, src, land, sems):
    send_sems, recv_sems, local_sem = sems
    x, y, c, me = _place()

    def ends(p_dst, p_from):
        if gather:
            return src, _shard_view(land, kind, me), _shard_view(land, kind, p_from)
        return _shard_view(src, kind, p_dst), land.at[me], land.at[p_from]

    s_me, d_me, _ = ends(me, me)
    local = pltpu.make_async_copy(s_me, d_me, local_sem)
    sends, arrivals = [], []
    for k in range(1, NDEV):
        peer, pidx = _peer(x, y, c, k)
        s_k, d_k, from_k = ends(pidx, pidx)
        sends.append(pltpu.make_async_remote_copy(s_k, d_k, send_sems.at[k - 1], recv_sems.at[k - 1],
                                                  device_id=peer, device_id_type=MESH_ID))
        arrivals.append(pltpu.make_async_remote_copy(s_k, from_k, send_sems.at[k - 1], recv_sems.at[k - 1],
                                                     device_id=peer, device_id_type=MESH_ID))
    return local, sends, arrivals


def _exchange_start(kinds, gather, srcs, name, with_token=False):
    n = len(kinds)
    lands = [lax.empty(_FULL_SHAPES[kd] if gather else (NDEV,) + _SHARD_SHAPES[kd], BF16) for kd in kinds]

    def body(*refs):
        ins, sems = refs[:2 * n], refs[2 * n:5 * n]
        for j, kd in enumerate(kinds):
            local, sends, _ = _exchange_copies(kd, gather, ins[j], ins[n + j], sems[3 * j:3 * j + 3])
            local.start()
            for cp in sends:
                cp.start()
        if with_token:
            token = refs[7 * n]
            token[...] = jnp.zeros_like(token)

    out_shape = list(_SEM_SHAPES) * n
    out_shape += [pltpu.HBM(a.shape, a.dtype) for a in srcs] + [pltpu.HBM(a.shape, a.dtype) for a in lands]
    out_specs = [_SEM] * (3 * n) + [_HBM] * (2 * n)
    if with_token:
        out_shape.append(_sds((8, 128), F32))
        out_specs.append(pl.BlockSpec(memory_space=pltpu.VMEM))
    outs = _call(
        body, name=name, out_shape=out_shape,
        in_specs=[_HBM] * (2 * n), out_specs=out_specs,
        input_output_aliases={i: 3 * n + i for i in range(2 * n)},
        compiler_params=pltpu.CompilerParams(has_side_effects=_EFFECT),
    )(*[_hbm(a) for a in srcs], *[_hbm(a) for a in lands])
    per_tensor = [(tuple(outs[3 * j:3 * j + 3]), outs[3 * n + j], outs[4 * n + j]) for j in range(n)]
    return (per_tensor, outs[5 * n]) if with_token else per_tensor


def _exchange_wait(kinds, gather, started, after, name):
    n = len(kinds)

    def body(*refs):
        ins, sems = refs[:2 * n], refs[2 * n:5 * n]
        for j, kd in enumerate(kinds):
            local, sends, arrivals = _exchange_copies(kd, gather, ins[j], ins[n + j], sems[3 * j:3 * j + 3])
            local.wait()
            for cp in arrivals:
                cp.wait_recv()
            for cp in sends:
                cp.wait_send()

    srcs = [st[1] for st in started]
    lands = [st[2] for st in started]
    sems = [sm for st in started for sm in st[0]]
    outs = _call(
        body, name=name,
        out_shape=[pltpu.HBM(a.shape, a.dtype) for a in srcs] + [pltpu.HBM(a.shape, a.dtype) for a in lands],
        in_specs=[_HBM] * (2 * n) + [_SEM] * (3 * n) + [pl.BlockSpec(memory_space=pl.ANY)],
        out_specs=[_HBM] * (2 * n),
        input_output_aliases={i: i for i in range(2 * n)},
        compiler_params=pltpu.CompilerParams(has_side_effects=_EFFECT),
    )(*srcs, *lands, *sems, after)
    return outs[n:]


def _mod_fwd(c_all, w_ada, b_sl):
    def body(c_ref, w_ref, b_ref, mod_ref, cact_ref):
        cv = c_ref[...]
        ca = cv * _sigmoid(cv)
        cact_ref[...] = ca
        mod_ref[0] = _dot(ca.astype(BF16), w_ref[0].astype(BF16)) + b_ref[0]

    return _call(
        body, name="mod_fwd", grid=(2,),
        in_specs=[pl.BlockSpec((NDEV, D), lambda l: (0, 0)), pl.BlockSpec((1, D, ADA_SH), lambda l: (l, 0, 0)),
                  pl.BlockSpec((1, 1, ADA_SH), lambda l: (l, 0, 0))],
        out_specs=[pl.BlockSpec((1, NDEV, ADA_SH), lambda l: (l, 0, 0)), pl.BlockSpec((NDEV, D), lambda l: (0, 0))],
        out_shape=[_sds((2, NDEV, ADA_SH), F32), _sds((NDEV, D), F32)],
        compiler_params=_params(1),
    )(c_all, w_ada, b_sl)


def _nm_matmul(x, shift, scale, w, *, two_d, n, groups, out_dtype, tm, name):
    s = x.shape[0]

    def body(x_ref, sh_ref, sc_ref, w_ref, h_ref, o_ref):
        @pl.when(pl.program_id(1) == 0)
        def _():
            xv = x_ref[...]
            h_ref[...] = ((xv * _rms(xv)) * (1.0 + sc_ref[...]) + sh_ref[...]).astype(BF16)
        if two_d:
            o_ref[0] = _dot(h_ref[...], w_ref[...]).astype(out_dtype)
        else:
            o_ref[0] = _dot_nt(h_ref[...], w_ref[0]).astype(out_dtype)

    vec = pl.BlockSpec((1, D), lambda i, g: (0, 0))
    w_spec = pl.BlockSpec((D, n), lambda i, g: (0, g)) if two_d else pl.BlockSpec((1, n, D), lambda i, g: (g, 0, 0))
    return _call(
        body, name=name, grid=(s // tm, groups),
        in_specs=[pl.BlockSpec((tm, D), lambda i, g: (i, 0)), vec, vec, w_spec],
        out_specs=[pl.BlockSpec((tm, D), lambda i, g: (i, 0)), pl.BlockSpec((1, tm, n), lambda i, g: (g, i, 0))],
        out_shape=[_sds((s, D), BF16), _sds((groups, s, n), out_dtype)],
        compiler_params=_params(2),
    )(x, shift, scale, w)


def _bias_table(rel_bias):
    far = jnp.broadcast_to(rel_bias[:, 2 * REL_CLIP:], (NDEV, N_FAR))
    near = rel_bias[:, 2 * REL_CLIP - 1:REL_CLIP - CHUNK:-1]
    fpad = jnp.concatenate([far, near, jnp.zeros((NDEV, TABW - N_FAR - N_NEAR), F32)], axis=1)

    def body(f_ref, o_ref):
        t = pltpu.roll(jnp.broadcast_to(f_ref[0], (TQA, TABW)), 0, 1, stride=1, stride_axis=0)[:, TAB0:]
        rows = lax.broadcasted_iota(jnp.int32, (TQA, KWA), 0)
        cols = lax.broadcasted_iota(jnp.int32, (TQA, KWA), 1)
        first = jnp.bitwise_and(rows, -CHUNK)
        o_ref[0] = jnp.where((cols >= first) & (cols < first + BAND), t, NEG)

    return _call(
        body, name="bias_table", grid=(NDEV,),
        in_specs=[pl.BlockSpec((1, 1, TABW), lambda h: (h, 0, 0))],
        out_specs=pl.BlockSpec((1, TQA, KWA), lambda h: (h, 0, 0)),
        out_shape=_sds((NDEV, TQA, KWA), F32),
        compiler_params=_params(1),
    )(fpad.reshape(NDEV, 1, TABW))


def _attn_a_cases(qi, tile):
    @pl.when(qi == 0)
    def _():
        tile(TQA, 2 * TQA, 0)

    @pl.when(qi == 1)
    def _():
        tile(2 * TQA, TQA, 0)

    @pl.when(qi >= 2)
    def _():
        tile(KWA, 0, pl.multiple_of((qi - 2) * TQA, TQA))


HA = 8
WA = HA * HD


def _attn_a_specs(s):
    q_spec = pl.BlockSpec((1, TQA, WA), lambda hp, qi: (0, qi, hp))
    k_spec = pl.BlockSpec((1, s, WA), lambda hp, qi: (1, 0, hp))
    v_spec = pl.BlockSpec((1, s, WA), lambda hp, qi: (2, 0, hp))
    b_spec = pl.BlockSpec((HA, TQA, KWA), lambda hp, qi: (hp, 0, 0))
    return q_spec, k_spec, v_spec, b_spec


def _attn_a_fwd(proj, bias_tab):
    s = proj.shape[1]

    def body(q_ref, k_ref, v_ref, b_ref, o_ref):
        def tile(nk, off, kstart):
            for h in range(HA):
                sl = slice(HD * h, HD * (h + 1))
                q = q_ref[0, :, sl]
                k = k_ref[0, pl.ds(kstart, nk), sl]
                v = v_ref[0, pl.ds(kstart, nk), sl]
                sc = _dot_nt(q, k) * 0.125 + b_ref[h, :, off:off + nk]
                p = jnp.exp(sc - jnp.max(sc, axis=-1, keepdims=True))
                den = jnp.sum(p, axis=-1, keepdims=True)
                o_ref[:, sl] = _dot(p.astype(BF16), v) / den

        _attn_a_cases(pl.program_id(1), tile)

    q_spec, k_spec, v_spec, b_spec = _attn_a_specs(s)
    return _call(
        body, name="attn_a_fwd", grid=(WG // WA, s // TQA),
        in_specs=[q_spec, k_spec, v_spec, b_spec],
        out_specs=pl.BlockSpec((TQA, WA), lambda hp, qi: (qi, hp)),
        out_shape=_sds((s, WG), F32),
        compiler_params=_params(2),
    )(proj, proj, proj, bias_tab)


def _sb_terms(lg):
    sp = jnp.maximum(lg, 0.0) + jnp.log(1.0 + jnp.exp(-jnp.abs(lg)))
    return lg - sp, sp


def _attn_b_specs(s):
    q_spec = pl.BlockSpec((1, TQB, 128), lambda hp, qi: (3, qi, hp))
    k_spec = pl.BlockSpec((1, s, 128), lambda hp, qi: (4, 0, hp))
    v_spec = pl.BlockSpec((1, s, 128), lambda hp, qi: (5, 0, hp))
    return q_spec, k_spec, v_spec


def _attn_b_fwd(proj):
    s = proj.shape[1]
    ndiag = TQB // TKB

    def body(q_ref, k_ref, v_ref, o_ref, ls_ref):
        q0 = pl.program_id(1) * TQB
        heads = [slice(HD * h, HD * (h + 1)) for h in range(2)]

        rows = lax.broadcasted_iota(jnp.int32, (TQB, 2 * TKB), 0)
        lane = lax.broadcasted_iota(jnp.int32, (TQB, 2 * TKB), 1)
        cols = jnp.bitwise_and(lane, TKB - 1)
        bj = lax.broadcasted_iota(jnp.int32, (2 * TKB, 2 * TKB), 0)
        bs = lax.broadcasted_iota(jnp.int32, (2 * TKB, 2 * TKB), 1)
        suffix_pair = jnp.where((bj >= bs) & ((bj >= TKB) == (bs >= TKB)), 1.0, 0.0).astype(BF16)
        q_pair = (q_ref[0].astype(F32) * 0.125).astype(BF16)
        zero = jnp.zeros((TKB, HD), BF16)

        def pair(x):
            return jnp.concatenate([jnp.concatenate([x[:, :HD], zero], axis=1),
                                    jnp.concatenate([zero, x[:, HD:]], axis=1)], axis=0)

        def diag_tile(kstart, d, carry, acc):
            k = k_ref[0, pl.ds(kstart, TKB), :]
            v = v_ref[0, pl.ds(kstart, TKB), :]
            lb, sp = _sb_terms(_dot_nt(q_pair, pair(k)))
            strict = rows > cols + d * TKB
            sp = jnp.where(strict, sp, 0.0)
            csum = _dot2(sp, suffix_pair) + carry
            w = jnp.where(strict, jnp.exp(lb - csum + sp), 0.0)
            carry = jnp.where(lane >= TKB, csum[:, TKB:TKB + 1], csum[:, 0:1])
            return carry, acc + _dot(w.astype(BF16), pair(v))

        both = (jnp.zeros((TQB, 2 * TKB), F32), jnp.zeros((TQB, 2 * HD), F32))
        for d in range(ndiag - 1, -1, -1):
            both = diag_tile(pl.multiple_of(q0 + d * TKB, TKB), d, *both)
        state = [both[0][:, 0:1], both[1][:, :HD], both[0][:, TKB:TKB + 1], both[1][:, HD:]]

        uj = lax.broadcasted_iota(jnp.int32, (TKB, TKB), 0)
        us = lax.broadcasted_iota(jnp.int32, (TKB, TKB), 1)
        suffix = jnp.where(uj >= us, 1.0, 0.0).astype(BF16)
        qs = [q_pair[:, sl] for sl in heads]

        def tile(h, kstart, carry, acc):
            k = k_ref[0, pl.ds(kstart, TKB), heads[h]]
            v = v_ref[0, pl.ds(kstart, TKB), heads[h]]
            lb, sp = _sb_terms(_dot_nt(qs[h], k))
            csum = _dot2(sp, suffix) + carry
            w = jnp.exp(lb - csum + sp)
            return csum[:, 0:1], acc + _dot(w.astype(BF16), v)

        nsteps = q0 // (KSTEP * TKB)

        def step(i, st):
            st = list(st)
            base = (nsteps - 1 - i) * (KSTEP * TKB)
            for sub in range(KSTEP - 1, -1, -1):
                for h in range(2):
                    st[2 * h:2 * h + 2] = tile(h, pl.multiple_of(base + sub * TKB, TKB), *st[2 * h:2 * h + 2])
            return tuple(st)

        state = lax.fori_loop(0, nsteps, step, tuple(state))
        for h in range(2):
            o_ref[:, heads[h]] = state[2 * h + 1]
            ls_ref[0, :, heads[h]] = jnp.broadcast_to(state[2 * h], (TQB, HD))

    q_spec, k_spec, v_spec = _attn_b_specs(s)
    return _call(
        body, name="attn_b_fwd", grid=(4, s // TQB),
        in_specs=[q_spec, k_spec, v_spec],
        out_specs=[pl.BlockSpec((TQB, 128), lambda hp, qi: (qi, hp)),
                   pl.BlockSpec((1, TQB, 128), lambda hp, qi: (hp, qi, 0))],
        out_shape=[_sds((s, WG), F32), _sds((4, s, 128), F32)],
        compiler_params=_params(2),
    )(proj, proj, proj)


def _mix_out(oa, ob, g, w_out, x, gate, tm):
    s = x.shape[0]

    def body(oa_ref, ob_ref, g_ref, w_ref, x_ref, gate_ref, nab_ref, mixed_ref, x2_ref):
        a, b = oa_ref[...], ob_ref[...]
        nab_ref[:, :WG] = (a * _rms(a) * g_ref[:, :WG]).astype(BF16)
        nab_ref[:, WG:] = (b * _rms(b) * g_ref[:, WG:]).astype(BF16)
        mixed = _dot(nab_ref[...], w_ref[...])
        mixed_ref[...] = mixed
        x2_ref[...] = x_ref[...] + gate_ref[...] * mixed

    row = pl.BlockSpec((tm, D), lambda i: (i, 0))
    half = pl.BlockSpec((tm, WG), lambda i: (i, 0))
    vec = pl.BlockSpec((1, D), lambda i: (0, 0))
    return _call(
        body, name="mix_out", grid=(s // tm,),
        in_specs=[half, half, vec, pl.BlockSpec((D, D), lambda i: (0, 0)), row, vec],
        out_specs=[row, row, row],
        out_shape=[_sds((s, D), BF16), _sds((s, D), F32), _sds((s, D), F32)],
        compiler_params=_params(1),
    )(oa, ob, g, w_out, x, gate)


HALO = 16


def _conv_taps(u, halo, first, tm):
    keep = jnp.where(first, 0.0, 1.0)
    window = jnp.concatenate([halo[HALO - 8:HALO, :] * keep, u[0:8, :]], axis=0)
    um1 = jnp.concatenate([pltpu.roll(window, 1, 0)[8:16, :], pltpu.roll(u, 1, 0)[8:, :]], axis=0)
    um2 = jnp.concatenate([pltpu.roll(window, 2, 0)[8:16, :], pltpu.roll(u, 2, 0)[8:, :]], axis=0)
    return um1, um2


def _conv_specs(tm):
    u_spec = pl.BlockSpec((2, 1, tm, GU), lambda p, i: (0, p, i, 0))
    halo_spec = pl.BlockSpec((2, 1, HALO, GU), lambda p, i: (0, p, jnp.maximum(i * (tm // HALO) - 1, 0), 0))
    cw_spec = pl.BlockSpec((2, 1, 3, GU), lambda p, i: (0, p, 0, 0))
    cb_spec = pl.BlockSpec((2, 1, 1, GU), lambda p, i: (0, p, 0, 0))
    return u_spec, halo_spec, cw_spec, cb_spec


def _conv_act(u, conv_w, conv_b, tm):
    s = u.shape[2]

    def body(u_ref, halo_ref, cw_ref, cb_ref, a_ref):
        first = pl.program_id(1) == 0
        ys = []
        for side in range(2):
            uv = u_ref[side, 0].astype(F32)
            um1, um2 = _conv_taps(uv, halo_ref[side, 0].astype(F32), first, tm)
            cw = cw_ref[side, 0]
            ys.append(cw[2:3] * uv + cw[1:2] * um1 + cw[0:1] * um2 + cb_ref[side, 0])
        a_ref[0] = (ys[0] * _sigmoid(ys[0]) * ys[1]).astype(BF16)

    u_spec, halo_spec, cw_spec, cb_spec = _conv_specs(tm)
    return _call(
        body, name="conv_act", grid=(4, s // tm),
        in_specs=[u_spec, halo_spec, cw_spec, cb_spec],
        out_specs=pl.BlockSpec((1, tm, GU), lambda p, i: (p, i, 0)),
        out_shape=_sds((4, s, GU), BF16),
        compiler_params=_params(2),
    )(u, u, conv_w, conv_b)


def _down(a, w_down, x2, gate, tm):
    s = x2.shape[0]

    def body(a_ref, w_ref, x_ref, gate_ref, ffn_ref, x3_ref):
        p = pl.program_id(1)
        part = _dot(a_ref[0], w_ref[0])

        @pl.when(p == 0)
        def _():
            ffn_ref[...] = part

        @pl.when(p > 0)
        def _():
            ffn_ref[...] += part

        @pl.when(p == 3)
        def _():
            x3_ref[...] = x_ref[...] + gate_ref[...] * ffn_ref[...]

    row = pl.BlockSpec((tm, D), lambda i, p: (i, 0))
    return _call(
        body, name="down", grid=(s // tm, 4),
        in_specs=[pl.BlockSpec((1, tm, GU), lambda i, p: (p, i, 0)), pl.BlockSpec((1, GU, D), lambda i, p: (p, 0, 0)),
                  row, pl.BlockSpec((1, D), lambda i, p: (0, 0))],
        out_specs=[row, row],
        out_shape=[_sds((s, D), F32), _sds((s, D), F32)],
        compiler_params=_params(2),
    )(a, w_down, x2, gate)


def _final_loss(x, g, target, tm):
    s = x.shape[0]

    def body(x_ref, g_ref, t_ref, loss_ref, dx_ref, dg_ref):
        @pl.when(pl.program_id(0) == 0)
        def _():
            loss_ref[...] = jnp.zeros_like(loss_ref)
            dg_ref[...] = jnp.zeros_like(dg_ref)
        xv = x_ref[...]
        r = _rms(xv)
        nrm = xv * r
        err = nrm * g_ref[...] - t_ref[...]
        loss_ref[...] += (0.5 / D) * jnp.sum(jnp.sum(err * err, axis=-1, keepdims=True), axis=0, keepdims=True)
        dy = err * (1.0 / D)
        dg_ref[...] += _colsum(dy * nrm)
        dx_ref[...] = _rms_bwd(dy * g_ref[...], nrm, r)

    row = pl.BlockSpec((tm, D), lambda i: (i, 0))
    vec = pl.BlockSpec((1, D), lambda i: (0, 0))
    return _call(
        body, name="final_loss", grid=(s // tm,),
        in_specs=[row, vec, row],
        out_specs=[pl.BlockSpec((1, 1), lambda i: (0, 0)), row, vec],
        out_shape=[_sds((1, 1), F32), _sds((s, D), F32), _sds((1, D), F32)],
        compiler_params=_params(1),
    )(x, g, target)


def _down_bwd(dx3, gate, ffn, w_down, tm):
    s = dx3.shape[0]

    def body(dx_ref, gate_ref, ffn_ref, w_ref, dgate_ref, dff_ref, da_ref):
        i, p = pl.program_id(0), pl.program_id(1)

        @pl.when((i == 0) & (p == 0))
        def _():
            dgate_ref[...] = jnp.zeros_like(dgate_ref)

        @pl.when(p == 0)
        def _():
            dxv = dx_ref[...]
            dgate_ref[...] += _colsum(dxv * ffn_ref[...])
            dff_ref[...] = (dxv * gate_ref[...]).astype(BF16)

        da_ref[0] = _dot_nt(dff_ref[...], w_ref[0]).astype(BF16)

    row = pl.BlockSpec((tm, D), lambda i, p: (i, 0))
    vec = pl.BlockSpec((1, D), lambda i, p: (0, 0))
    return _call(
        body, name="down_bwd", grid=(s // tm, 4),
        in_specs=[row, vec, row, pl.BlockSpec((1, GU, D), lambda i, p: (p, 0, 0))],
        out_specs=[vec, row, pl.BlockSpec((1, tm, GU), lambda i, p: (p, i, 0))],
        out_shape=[_sds((1, D), F32), _sds((s, D), BF16), _sds((4, s, GU), BF16)],
        compiler_params=_params(2),
    )(dx3, gate, ffn, w_down)


def _conv_act_bwd(u, conv_w, conv_b, da, tm, after):
    s = u.shape[2]

    def body(u_ref, halo_ref, cw_ref, cb_ref, da_ref, dy_ref, dcw_ref, dcb_ref):
        first = pl.program_id(1) == 0

        @pl.when(first)
        def _():
            dcw_ref[...] = jnp.zeros_like(dcw_ref)
            dcb_ref[...] = jnp.zeros_like(dcb_ref)

        taps, ys = [], []
        for side in range(2):
            uv = u_ref[side, 0].astype(F32)
            um1, um2 = _conv_taps(uv, halo_ref[side, 0].astype(F32), first, tm)
            cw = cw_ref[side, 0]
            taps.append((um2, um1, uv))
            ys.append(cw[2:3] * uv + cw[1:2] * um1 + cw[0:1] * um2 + cb_ref[side, 0])
        dav = da_ref[0].astype(F32)
        sg = _sigmoid(ys[0])
        dys = (dav * ys[1] * (sg * (1.0 + ys[0] * (1.0 - sg))), dav * (ys[0] * sg))
        for side in range(2):
            dy_ref[side, 0] = dys[side].astype(BF16)
            dcb_ref[side, 0] += _colsum(dys[side])
            for j in range(3):
                dcw_ref[side, 0, j:j + 1, :] += _colsum(dys[side] * taps[side][j])

    u_spec, halo_spec, cw_spec, cb_spec = _conv_specs(tm)
    return _call_after(
        body, after, name="conv_act_bwd", grid=(4, s // tm),
        in_specs=[u_spec, halo_spec, cw_spec, cb_spec, pl.BlockSpec((1, tm, GU), lambda p, i: (p, i, 0))],
        out_specs=[u_spec, cw_spec, cb_spec],
        out_shape=[_sds((2, 4, s, GU), BF16), _sds((2, 4, 3, GU), F32), _sds((2, 4, 1, GU), F32)],
        compiler_params=_params(2),
    )(u, u, conv_w, conv_b, da)


def _conv_transpose(dy, conv_w, tm):
    s = dy.shape[1]
    nt = s // tm

    def body(dy_ref, halo_ref, cw_ref, du_ref):
        keep = jnp.where(pl.program_id(1) == nt - 1, 0.0, 1.0)
        dv = dy_ref[0].astype(F32)
        window = jnp.concatenate([dv[tm - 8:, :], halo_ref[0, 0:8, :].astype(F32) * keep], axis=0)
        dp1 = jnp.concatenate([pltpu.roll(dv, tm - 1, 0)[:tm - 8, :], pltpu.roll(window, 15, 0)[0:8, :]], axis=0)
        dp2 = jnp.concatenate([pltpu.roll(dv, tm - 2, 0)[:tm - 8, :], pltpu.roll(window, 14, 0)[0:8, :]], axis=0)
        cw = cw_ref[0]
        du_ref[0] = (cw[2:3] * dv + cw[1:2] * dp1 + cw[0:1] * dp2).astype(BF16)

    blk = pl.BlockSpec((1, tm, GU), lambda g, i: (g, i, 0))
    return _call(
        body, name="conv_transpose", grid=(NDEV, nt),
        in_specs=[blk, pl.BlockSpec((1, HALO, GU),
                                    lambda g, i: (g, jnp.minimum((i + 1) * (tm // HALO), s // HALO - 1), 0)),
                  pl.BlockSpec((1, 3, GU), lambda g, i: (g, 0, 0))],
        out_specs=blk,
        out_shape=_sds((NDEV, s, GU), BF16),
        compiler_params=_params(2),
    )(dy, dy, conv_w)


def _wgrad(a3, b3, *, tk, name):
    ga, s, ka = a3.shape
    gb, _, nb = b3.shape
    groups = max(ga, gb)
    nk = s // tk

    def body(a_ref, b_ref, o_ref, acc):
        k = pl.program_id(1)

        @pl.when(k == 0)
        def _():
            acc[...] = jnp.zeros_like(acc)

        acc[...] += _dot_tn(a_ref[0], b_ref[0])

        @pl.when(k == nk - 1)
        def _():
            o_ref[0] = acc[...].astype(BF16)

    a_spec = pl.BlockSpec((1, tk, ka), (lambda g, k: (g, k, 0)) if ga > 1 else (lambda g, k: (0, k, 0)))
    b_spec = pl.BlockSpec((1, tk, nb), (lambda g, k: (g, k, 0)) if gb > 1 else (lambda g, k: (0, k, 0)))
    return _call(
        body, name=name, grid=(groups, nk),
        in_specs=[a_spec, b_spec], out_specs=pl.BlockSpec((1, ka, nb), lambda g, k: (g, 0, 0)),
        out_shape=_sds((groups, ka, nb), BF16),
        scratch_shapes=[pltpu.VMEM((ka, nb), F32)],
        compiler_params=_params(2),
    )(a3, b3)


def _wgrad_in(h1, dparts, tk, after):
    s = h1.shape[0]
    nk = s // tk

    def body(a_ref, *refs):
        d_refs, o_ref, acc = refs[:NT], refs[NT], refs[NT + 1]
        k = pl.program_id(0)

        @pl.when(k == 0)
        def _():
            acc[...] = jnp.zeros_like(acc)

        for j in range(NT):
            acc[:, WG * j:WG * (j + 1)] += _dot_tn(a_ref[...], d_refs[j][...])

        @pl.when(k == nk - 1)
        def _():
            o_ref[...] = acc[...].astype(BF16)

    kw = dict(
        name="wgrad_in", grid=(nk,),
        in_specs=[pl.BlockSpec((tk, D), lambda k: (k, 0))] + [pl.BlockSpec((tk, WG), lambda k: (k, 0))] * NT,
        out_specs=pl.BlockSpec((D, NT * WG), lambda k: (0, 0)),
        out_shape=_sds((D, NT * WG), BF16),
        scratch_shapes=[pltpu.VMEM((D, NT * WG), F32)],
        compiler_params=_params(1),
    )
    call = _call(body, **kw) if after is None else _call_after(body, after, **kw)
    return call(h1, *dparts)


def _dgrad_in(dparts, w, x_in, scale, dx_up, tm, after):
    s = x_in.shape[0]

    def body(*refs):
        d_refs = refs[:NT]
        w_ref, x_ref, sc_ref, up_ref, dx_ref, dsc_ref, dsh_ref = refs[NT:]

        @pl.when(pl.program_id(0) == 0)
        def _():
            dsc_ref[...] = jnp.zeros_like(dsc_ref)
            dsh_ref[...] = jnp.zeros_like(dsh_ref)

        dh = _dot_nt(d_refs[0][...], w_ref[:, 0:WG])
        for j in range(1, NT):
            dh = dh + _dot_nt(d_refs[j][...], w_ref[:, WG * j:WG * (j + 1)])
        xv = x_ref[...]
        r = _rms(xv)
        nrm = xv * r
        dsh_ref[...] += _colsum(dh)
        dsc_ref[...] += _colsum(dh * nrm)
        dx_ref[...] = up_ref[...] + _rms_bwd(dh * (1.0 + sc_ref[...]), nrm, r)

    row = pl.BlockSpec((tm, D), lambda i: (i, 0))
    vec = pl.BlockSpec((1, D), lambda i: (0, 0))
    return _call_after(
        body, after, name="dgrad_in", grid=(s // tm,),
        in_specs=[pl.BlockSpec((tm, WG), lambda i: (i, 0))] * NT + [pl.BlockSpec((D, NT * WG), lambda i: (0, 0)), row, vec, row],
        out_specs=[row, vec, vec],
        out_shape=[_sds((s, D), F32), _sds((1, D), F32), _sds((1, D), F32)],
        compiler_params=_params(1),
    )(*dparts, w, x_in, scale, dx_up)


def _dgrad_norm_bwd(d3, w, x_in, scale, dx_up, *, tm, name, after):
    groups, s, n = d3.shape

    def body(d_ref, w_ref, x_ref, sc_ref, up_ref, dx_ref, dsc_ref, dsh_ref, acc):
        i, g = pl.program_id(0), pl.program_id(1)

        @pl.when((i == 0) & (g == 0))
        def _():
            dsc_ref[...] = jnp.zeros_like(dsc_ref)
            dsh_ref[...] = jnp.zeros_like(dsh_ref)

        part = _dot(d_ref[0], w_ref[0])

        @pl.when(g == 0)
        def _():
            acc[...] = part

        @pl.when(g > 0)
        def _():
            acc[...] += part

        @pl.when(g == groups - 1)
        def _():
            dh = acc[...]
            xv = x_ref[...]
            r = _rms(xv)
            nrm = xv * r
            dsh_ref[...] += _colsum(dh)
            dsc_ref[...] += _colsum(dh * nrm)
            dx_ref[...] = up_ref[...] + _rms_bwd(dh * (1.0 + sc_ref[...]), nrm, r)

    row = pl.BlockSpec((tm, D), lambda i, g: (i, 0))
    vec = pl.BlockSpec((1, D), lambda i, g: (0, 0))
    return _call_after(
        body, after, name=name, grid=(s // tm, groups),
        in_specs=[pl.BlockSpec((1, tm, n), lambda i, g: (g, i, 0)), pl.BlockSpec((1, n, D), lambda i, g: (g, 0, 0)),
                  row, vec, row],
        out_specs=[row, vec, vec],
        out_shape=[_sds((s, D), F32), _sds((1, D), F32), _sds((1, D), F32)],
        scratch_shapes=[pltpu.VMEM((tm, D), F32)],
        compiler_params=_params(2),
    )(d3, w, x_in, scale, dx_up)


def _mix_out_bwd(dx2, mixed, gate, w_out, oa, ob, g, tm):
    s = dx2.shape[0]

    def body(dx_ref, mixed_ref, gate_ref, w_ref, oa_ref, ob_ref, g_ref, dgate_ref, dm_ref, doa_ref, dob_ref, dg_ref):
        @pl.when(pl.program_id(0) == 0)
        def _():
            dgate_ref[...] = jnp.zeros_like(dgate_ref)
            dg_ref[...] = jnp.zeros_like(dg_ref)
        dxv = dx_ref[...]
        dgate_ref[...] += _colsum(dxv * mixed_ref[...])
        dm_ref[...] = (dxv * gate_ref[...]).astype(BF16)
        dnab = _dot_nt(dm_ref[...], w_ref[...])
        for o_ref, do_ref, sl in ((oa_ref, doa_ref, slice(0, WG)), (ob_ref, dob_ref, slice(WG, D))):
            ov = o_ref[...]
            r = _rms(ov)
            nrm = ov * r
            dn = dnab[:, sl]
            dg_ref[:, sl] += _colsum(dn * nrm)
            do_ref[...] = _rms_bwd(dn * g_ref[:, sl], nrm, r)

    row = pl.BlockSpec((tm, D), lambda i: (i, 0))
    half = pl.BlockSpec((tm, WG), lambda i: (i, 0))
    vec = pl.BlockSpec((1, D), lambda i: (0, 0))
    return _call(
        body, name="mix_out_bwd", grid=(s // tm,),
        in_specs=[row, row, vec, pl.BlockSpec((D, D), lambda i: (0, 0)), half, half, vec],
        out_specs=[vec, row, half, half, vec],
        out_shape=[_sds((1, D), F32), _sds((s, D), BF16), _sds((s, WG), F32), _sds((s, WG), F32), _sds((1, D), F32)],
        compiler_params=_params(1),
    )(dx2, mixed, gate, w_out, oa, ob, g)


def _attn_a_bwd(proj, bias_tab, d_oa, after):
    s = proj.shape[1]
    nq = s // TQA
    nkb = s // 128

    def body(q_ref, k_ref, v_ref, b_ref, do_ref, dq_ref, dk_ref, dv_ref, db_ref, dkt_acc, dvt_acc):
        qi = pl.program_id(1)

        @pl.when(qi == 0)
        def _():
            dkt_acc[...] = jnp.zeros_like(dkt_acc)
            dvt_acc[...] = jnp.zeros_like(dvt_acc)
            db_ref[...] = jnp.zeros_like(db_ref)

        heads = [slice(HD * h, HD * (h + 1)) for h in range(HA)]
        qs = [q_ref[0, :, sl] for sl in heads]
        dos = [do_ref[:, sl].astype(BF16) for sl in heads]
        qts = [_transpose_bf16(x, _eye(HD)) for x in qs]
        dots = [_transpose_bf16(x, _eye(HD)) for x in dos]

        def tile(nk, off, kstart):
            kb0 = kstart // 128
            for h in range(HA):
                sl = heads[h]
                k = k_ref[0, pl.ds(kstart, nk), sl]
                v = v_ref[0, pl.ds(kstart, nk), sl]
                sc = _dot_nt(qs[h], k) * 0.125 + b_ref[h, :, off:off + nk]
                p = jnp.exp(sc - jnp.max(sc, axis=-1, keepdims=True))
                p = p / jnp.sum(p, axis=-1, keepdims=True)
                dp = _dot_nt(dos[h], v)
                ds = p * (dp - jnp.sum(dp * p, axis=-1, keepdims=True))
                db_ref[h, :, off:off + nk] += ds
                dsb = (ds * 0.125).astype(BF16)
                dq_ref[:, sl] = _dot(dsb, k).astype(BF16)
                dkt = _dot(qts[h], dsb)
                dvt = _dot(dots[h], p.astype(BF16))
                for j in range(nk // 128):
                    dkt_acc[h * nkb + kb0 + j] += dkt[:, 128 * j:128 * (j + 1)]
                    dvt_acc[h * nkb + kb0 + j] += dvt[:, 128 * j:128 * (j + 1)]

        _attn_a_cases(qi, tile)

        @pl.when(qi == nq - 1)
        def _():
            eye = _eye(128)
            for h in range(HA):
                for kb in range(nkb):
                    rows_kb = slice(128 * kb, 128 * (kb + 1))
                    dk_ref[rows_kb, heads[h]] = _transpose_f32(dkt_acc[h * nkb + kb], eye).astype(BF16)
                    dv_ref[rows_kb, heads[h]] = _transpose_f32(dvt_acc[h * nkb + kb], eye).astype(BF16)

    q_spec, k_spec, v_spec, b_spec = _attn_a_specs(s)
    blk = pl.BlockSpec((TQA, WA), lambda hp, qi: (qi, hp))
    col = pl.BlockSpec((s, WA), lambda hp, qi: (0, hp))
    return _call_after(
        body, after, name="attn_a_bwd", grid=(WG // WA, nq),
        in_specs=[q_spec, k_spec, v_spec, b_spec, blk],
        out_specs=[blk, col, col, b_spec],
        out_shape=[_sds((s, WG), BF16), _sds((s, WG), BF16), _sds((s, WG), BF16), _sds((NDEV, TQA, KWA), F32)],
        scratch_shapes=[pltpu.VMEM((HA * nkb, HD, 128), F32), pltpu.VMEM((HA * nkb, HD, 128), F32)],
        compiler_params=_params(2),
    )(proj, proj, proj, bias_tab, d_oa)


def _attn_b_bwd(proj, lsum, d_ob):
    s = proj.shape[1]
    nq = s // TQB
    nkb = s // TKB
    ndiag = TQB // TKB

    def body(q_ref, k_ref, v_ref, ls_ref, do_ref, dq_ref, dk_ref, dv_ref, dkt_acc, dvt_acc):
        qi = pl.program_id(1)
        q0 = qi * TQB

        @pl.when(qi == 0)
        def _():
            dkt_acc[...] = jnp.zeros_like(dkt_acc)
            dvt_acc[...] = jnp.zeros_like(dvt_acc)

        rows = lax.broadcasted_iota(jnp.int32, (TQB, TKB), 0)
        cols = lax.broadcasted_iota(jnp.int32, (TQB, TKB), 1)
        uj = jnp.bitwise_and(lax.broadcasted_iota(jnp.int32, (2 * TKB, TKB), 0), TKB - 1)
        us = lax.broadcasted_iota(jnp.int32, (2 * TKB, TKB), 1)
        prefix = jnp.where(uj <= us, 1.0, 0.0).astype(BF16)
        heads = [slice(HD * h, HD * (h + 1)) for h in range(2)]
        qs = [(q_ref[0, :, sl].astype(F32) * 0.125).astype(BF16) for sl in heads]
        dos = [do_ref[:, sl].astype(BF16) for sl in heads]
        qts = [_transpose_bf16(x, _eye(HD)) for x in qs]
        dots = [_transpose_bf16(x, _eye(HD)) for x in dos]
        stots = [ls_ref[0, :, HD * h:HD * h + 1] for h in range(2)]
        qds = [jnp.concatenate([qs[h], dos[h]], axis=1) for h in range(2)]
        zkv = jnp.zeros((TKB, HD), BF16)

        def tile(h, kb, diag, cl, cg, dq):
            r0 = 0 if diag is None else diag * TKB
            sl = heads[h]
            kstart = pl.multiple_of(kb * TKB, TKB)
            k = k_ref[0, pl.ds(kstart, TKB), sl]
            v = v_ref[0, pl.ds(kstart, TKB), sl]
            kv = jnp.concatenate([jnp.concatenate([k, zkv], axis=1), jnp.concatenate([zkv, v], axis=1)], axis=0)
            both = _dot_nt(qds[h][r0:, :], kv)
            lb, sp = _sb_terms(both[:, :TKB])
            if diag is not None:
                strict = rows[:TQB - r0, :] > cols[:TQB - r0, :]
                sp = jnp.where(strict, sp, 0.0)
            pre = _dot2_deep(sp, prefix) + cl[r0:, :]
            a = jnp.exp(lb - stots[h][r0:, :] + pre)
            if diag is not None:
                a = jnp.where(strict, a, 0.0)
            gz = both[:, TKB:] * a
            pg = _dot2_deep(gz, prefix) + cg[r0:, :]
            dl = gz - pg * jnp.exp(lb)
            if diag is not None:
                dl = jnp.where(strict, dl, 0.0)
            dlb = dl.astype(BF16)
            dkt_acc[h * nkb + kb] += _dot(qts[h][:, r0:], dlb)
            dvt_acc[h * nkb + kb] += _dot(dots[h][:, r0:], a.astype(BF16))
            new = (pre[:, TKB - 1:TKB], pg[:, TKB - 1:TKB], dq[r0:, :] + _dot(dlb, k))
            if r0:
                new = tuple(jnp.concatenate([old[:r0, :], x], axis=0) for old, x in zip((cl, cg, dq), new))
            return new

        state = [jnp.zeros((TQB, 1), F32), jnp.zeros((TQB, 1), F32), jnp.zeros((TQB, HD), F32)] * 2

        def step(i, st):
            st = list(st)
            for sub in range(KSTEP):
                for h in range(2):
                    st[3 * h:3 * h + 3] = tile(h, i * KSTEP + sub, None, *st[3 * h:3 * h + 3])
            return tuple(st)

        state = list(lax.fori_loop(0, q0 // (KSTEP * TKB), step, tuple(state)))
        for d in range(ndiag):
            for h in range(2):
                state[3 * h:3 * h + 3] = tile(h, q0 // TKB + d, d, *state[3 * h:3 * h + 3])
        for h in range(2):
            dq_ref[:, heads[h]] = (state[3 * h + 2] * 0.125).astype(BF16)

        @pl.when(qi == nq - 1)
        def _():
            eye = _eye(TKB)
            for h in range(2):
                for kb in range(nkb):
                    rows_kb = slice(kb * TKB, (kb + 1) * TKB)
                    dk_ref[rows_kb, heads[h]] = _transpose_f32(dkt_acc[h * nkb + kb], eye).astype(BF16)
                    dv_ref[rows_kb, heads[h]] = _transpose_f32(dvt_acc[h * nkb + kb], eye).astype(BF16)

    q_spec, k_spec, v_spec = _attn_b_specs(s)
    blk = pl.BlockSpec((TQB, 128), lambda hp, qi: (qi, hp))
    col = pl.BlockSpec((s, 128), lambda hp, qi: (0, hp))
    return _call(
        body, name="attn_b_bwd", grid=(4, nq),
        in_specs=[q_spec, k_spec, v_spec, pl.BlockSpec((1, TQB, 128), lambda hp, qi: (hp, qi, 0)), blk],
        out_specs=[blk, col, col],
        out_shape=[_sds((s, WG), BF16)] * 3,
        scratch_shapes=[pltpu.VMEM((2 * nkb, HD, TKB), F32), pltpu.VMEM((2 * nkb, HD, TKB), F32)],
        compiler_params=_params(2),
    )(proj, proj, proj, lsum, d_ob)


def _bias_fold(dtab):
    def body(t_ref, d_ref, far_ref):
        acc = jnp.zeros((NDEV, TABW), F32)
        zpad = jnp.zeros((NDEV, TAB0), F32)
        for r in range(TQA):
            row = jnp.concatenate([zpad, t_ref[:, r, :]], axis=1)
            acc = acc + (pltpu.roll(row, TABW - r, 1) if r else row)
        d_ref[...] = acc
        lane = lax.broadcasted_iota(jnp.int32, (NDEV, TABW), 1)
        far = jnp.sum(jnp.where(lane < N_FAR, acc, 0.0), axis=1, keepdims=True)
        far_ref[...] = jnp.broadcast_to(far, (NDEV, 128))

    d_fpad, d_far = _call(
        body, name="bias_fold", grid=(1,),
        in_specs=[pl.BlockSpec((NDEV, TQA, KWA), lambda i: (0, 0, 0))],
        out_specs=[pl.BlockSpec((NDEV, TABW), lambda i: (0, 0)), pl.BlockSpec((NDEV, 128), lambda i: (0, 0))],
        out_shape=[_sds((NDEV, TABW), F32), _sds((NDEV, 128), F32)],
        compiler_params=_params(1),
    )(dtab)
    d_near = d_fpad[:, N_FAR:N_FAR + N_NEAR][:, ::-1]
    return jnp.concatenate([jnp.zeros((NDEV, REL_CLIP - CHUNK + 1), F32), d_near, d_far[:, :1]], axis=1)


def _adamw(w, g, m, v):
    m = B1 * m + (1.0 - B1) * g
    v = B2 * v + (1.0 - B2) * (g * g)
    m_hat = m / (1.0 - B1 ** STEP)
    v_hat = v / (1.0 - B2 ** STEP)
    delta = -LR * (m_hat / (jnp.sqrt(v_hat) + AEPS) + WD * w)
    return delta, m, v


def _adamw_big(recv0, recv1, w, m, v):
    _, rows, cols = recv0.shape
    tr = max(t for t in range(16, 513, 16) if rows % t == 0)
    nt = rows // tr

    def body(r0_ref, r1_ref, w_ref, m_ref, v_ref, g_ref, d_ref, nm_ref, nv_ref):
        def update(r_ref):
            g = r_ref[0].astype(F32)
            for p in range(1, NDEV):
                g = g + r_ref[p].astype(F32)
            delta, nm, nv = _adamw(w_ref[0], g, m_ref[0], v_ref[0])
            g_ref[0], d_ref[0], nm_ref[0], nv_ref[0] = g, delta, nm, nv

        @pl.when(pl.program_id(0) == 0)
        def _():
            update(r0_ref)

        @pl.when(pl.program_id(0) == 1)
        def _():
            update(r1_ref)

    blk = pl.BlockSpec((1, tr, cols), lambda l, i: (l, i, 0))
    r0_spec = pl.BlockSpec((NDEV, tr, cols), lambda l, i: (0, jnp.where(l == 0, i, nt - 1), 0))
    r1_spec = pl.BlockSpec((NDEV, tr, cols), lambda l, i: (0, jnp.where(l == 1, i, 0), 0))
    return _call(
        body, name="adamw_big", grid=(2, nt),
        in_specs=[r0_spec, r1_spec, blk, blk, blk],
        out_specs=[blk] * 4,
        out_shape=[_sds((2, rows, cols), F32)] * 4,
        compiler_params=_params(2),
    )(recv0, recv1, w, m, v)


def _adamw_w_ada(cact_t, dmod, w, m, v):
    tr = 256

    def body(c_ref, dm_ref, w_ref, m_ref, v_ref, g_ref, d_ref, nm_ref, nv_ref):
        g = c_ref[:, 0:1] * dm_ref[0, 0:1, :]
        for b in range(1, NDEV):
            g = g + c_ref[:, b:b + 1] * dm_ref[0, b:b + 1, :]
        delta, nm, nv = _adamw(w_ref[0], g, m_ref[0], v_ref[0])
        g_ref[0], d_ref[0], nm_ref[0], nv_ref[0] = g, delta, nm, nv

    blk = pl.BlockSpec((1, tr, ADA_SH), lambda l, i: (l, i, 0))
    return _call(
        body, name="adamw_w_ada", grid=(2, D // tr),
        in_specs=[pl.BlockSpec((tr, NDEV), lambda l, i: (i, 0)), pl.BlockSpec((1, NDEV, ADA_SH), lambda l, i: (l, 0, 0)),
                  blk, blk, blk],
        out_specs=[blk] * 4,
        out_shape=[_sds((2, D, ADA_SH), F32)] * 4,
        compiler_params=_params(2),
    )(cact_t, dmod, w, m, v)


def _adamw_small(gath, w, m, v):
    rows = gath.shape[1]

    def body(r_ref, w_ref, m_ref, v_ref, g_ref, d_ref, nm_ref, nv_ref):
        g = r_ref[0]
        for p in range(1, NDEV):
            g = g + r_ref[p]
        delta, nm, nv = _adamw(w_ref[...], g, m_ref[...], v_ref[...])
        g_ref[...], d_ref[...], nm_ref[...], nv_ref[...] = g, delta, nm, nv

    blk = pl.BlockSpec((rows, D), lambda i: (0, 0))
    return _call(
        body, name="adamw_small", grid=(1,),
        in_specs=[pl.BlockSpec((NDEV, rows, D), lambda i: (0, 0, 0)), blk, blk, blk],
        out_specs=[blk] * 4,
        out_shape=[_sds((rows, D), F32)] * 4,
        compiler_params=_params(1),
    )(gath, w, m, v)


_PACK = (("b_ada", 2 * 6 * D), ("rel_bias", 2 * 8 * 257), ("g_a", 2 * WG), ("g_b", 2 * WG),
         ("conv_b", 2 * 2 * DFF), ("final_g", D), ("conv_w", 2 * NDEV * 3 * GU))


def _pack(parts):
    rows = []
    for name, size in _PACK:
        flat = parts[name].reshape(-1).astype(F32)
        assert flat.shape[0] == size, (name, flat.shape)
        rows.append(jnp.pad(flat, (0, -size % D)))
    out = jnp.concatenate(rows).reshape(-1, D)
    return jnp.pad(out, ((0, -out.shape[0] % 8), (0, 0)))


def _unpack(packed):
    flat = packed.reshape(-1)
    out, pos = {}, 0
    for name, size in _PACK:
        out[name] = flat[pos:pos + size]
        pos += size + (-size % D)
    return out


def kernel(x, c, w_ada, b_ada, w_in, rel_bias, g_a, g_b, w_out, w_up, conv_w, conv_b, w_down, final_g, loss_target, m_w_ada, m_b_ada, m_w_in, m_rel_bias, m_g_a, m_g_b, m_w_out, m_w_up, m_conv_w, m_conv_b, m_w_down, m_final_g, v_w_ada, v_b_ada, v_w_in, v_rel_bias, v_g_a, v_g_b, v_w_out, v_w_up, v_conv_w, v_conv_b, v_w_down, v_final_g):
    s = x.shape[1]
    assert s % TQA == 0 and s >= KWA and s % 512 == 0
    tm = 512
    tmm = min(1024, s)
    me = 4 * lax.axis_index("x") + 2 * lax.axis_index("y") + lax.axis_index("c")
    xs = x.reshape(s, D)
    target = loss_target.reshape(s, D)

    first = jnp.concatenate([c, jnp.pad(conv_w.reshape(2 * 3, GU), ((0, 1), (0, D - GU)))])
    first_all = _small_allgather(first, "gather_c_conv_w")
    c_all = first_all[:, 0, :]
    cw_all = first_all[:, 1:7, :GU].reshape(NDEV, 2, 3, GU)

    b_sl = lax.dynamic_slice(b_ada, (0, me * ADA_SH), (2, ADA_SH)).reshape(2, 1, ADA_SH)
    mod_part, cact = _mod_fwd(c_all, w_ada, b_sl)
    mod_all = _small_allgather(mod_part.reshape(2 * NDEV, ADA_SH), "gather_mod")

    up_t = [jnp.transpose(t, (0, 2, 1)) for t in (w_up, m_w_up, v_w_up)]
    shards = {"in": w_in, "out": w_out, "up": up_t[0], "down": w_down}
    order = [(kind, l) for l in range(2) for kind in _KINDS]
    mod_all, *srcs = lax.optimization_barrier((mod_all, *[shards[kind][l].astype(BF16) for kind, l in order]))
    gather_started = dict(zip(order, _exchange_start([kind for kind, _ in order], True, srcs, "weights_gather_start")))

    def gathered(kind, l, after):
        return _exchange_wait([kind], True, [gather_started[kind, l]], after, f"weights_gather_wait_{kind}{l}")[0]

    mod_all = mod_all.reshape(NDEV, 2, NDEV, ADA_SH)
    mod_me = lax.dynamic_index_in_dim(mod_all, me, axis=2, keepdims=False)
    mod = jnp.transpose(mod_me, (1, 0, 2)).reshape(2, 6, 1, D)

    saved = []
    xl = xs
    for l in range(2):
        sh_mix, sc_mix, gt_mix, sh_ffn, sc_ffn, gt_ffn = (mod[l, j] for j in range(6))
        cw = cw_all[:, l].reshape(2, 4, 3, GU)
        cb = conv_b[l].reshape(2, 4, 1, GU)
        gvec = jnp.concatenate([g_a[l], g_b[l]]).reshape(1, D)
        tab = _bias_table(rel_bias[l])

        wi = gathered("in", l, xl if l else mod)
        h1, proj = _nm_matmul(xl, sh_mix, sc_mix, wi, two_d=True, n=WG, groups=NT, out_dtype=BF16, tm=tmm,
                              name="norm_proj")
        oa = _attn_a_fwd(proj, tab)
        ob, lsum = _attn_b_fwd(proj)
        wo = gathered("out", l, ob)
        nab, mixed, x2 = _mix_out(oa, ob, gvec, wo, xl, gt_mix, tm)
        wu = gathered("up", l, x2)
        h2, u = _nm_matmul(x2, sh_ffn, sc_ffn, wu, two_d=False, n=GU, groups=NDEV, out_dtype=BF16, tm=tmm,
                           name="norm_up")
        u = u.reshape(2, 4, s, GU)
        a = _conv_act(u, cw, cb, tmm)
        wd4 = gathered("down", l, a).reshape(4, GU, D)
        ffn, x3 = _down(a, wd4, x2, gt_ffn, tmm)
        saved.append(dict(x=xl, h1=h1, proj=proj, oa=oa, ob=ob, lsum=lsum, nab=nab, mixed=mixed, x2=x2, h2=h2, u=u,
                          a=a, ffn=ffn, cw=cw, cb=cb, gvec=gvec, tab=tab, wd4=wd4, wi=wi, wo=wo, wu=wu))
        xl = x3

    loss_part, dx, d_final_g = _final_loss(xl, final_g.reshape(1, D), target, tm)
    loss = lax.psum(loss_part[0, 0], ("x", "y", "c"))

    sent = {}
    small = {"b_ada": [None, None], "rel_bias": [None, None], "g_a": [None, None], "g_b": [None, None],
             "conv_b": [None, None], "conv_w": [None, None]}

    def send(kind, l, grad):
        started, token = _exchange_start([kind], False, [grad], f"grads_start_{kind}{l}", with_token=True)
        sent[kind, l] = started[0]
        return token

    for l in (1, 0):
        sv = saved[l]
        sh_mix, sc_mix, gt_mix, sh_ffn, sc_ffn, gt_ffn = (mod[l, j] for j in range(6))
        d_gt_ffn, dff, da = _down_bwd(dx, gt_ffn, sv["ffn"], sv["wd4"], tmm)
        tok = send("down", l, _wgrad(sv["a"], dff.reshape(1, s, D), tk=tmm, name="wgrad_down").reshape(DFF, D))
        dy, d_cw, d_cb = _conv_act_bwd(sv["u"], sv["cw"], sv["cb"], da, tm, tok)
        du = _conv_transpose(dy.reshape(NDEV, s, GU), sv["cw"].reshape(NDEV, 3, GU), tmm)
        tok = send("up", l, _wgrad(du, sv["h2"].reshape(1, s, D), tk=tmm, name="wgrad_up"))
        dx2, d_sc_ffn, d_sh_ffn = _dgrad_norm_bwd(du, sv["wu"], sv["x2"], sc_ffn, dx, tm=tmm, name="dgrad_up", after=tok)
        d_gt_mix, dmixed, d_oa, d_ob, d_g = _mix_out_bwd(dx2, sv["mixed"], gt_mix, sv["wo"], sv["oa"], sv["ob"],
                                                         sv["gvec"], tm)
        tok = send("out", l, _wgrad(sv["nab"].reshape(1, s, D), dmixed.reshape(1, s, D), tk=tmm,
                                    name="wgrad_out").reshape(D, D))
        dqa, dka, dva, d_tab = _attn_a_bwd(sv["proj"], sv["tab"], d_oa, tok)
        dqb, dkb, dvb = _attn_b_bwd(sv["proj"], sv["lsum"], d_ob)
        dparts = (dqa, dka, dva, dqb, dkb, dvb)
        if l:
            tok = send("in", l, _wgrad_in(sv["h1"], dparts, tmm, None))
        dx, d_sc_mix, d_sh_mix = _dgrad_in(dparts, sv["wi"], sv["x"], sc_mix, dx2, tm, tok)
        small["b_ada"][l] = jnp.concatenate([d_sh_mix, d_sc_mix, d_gt_mix, d_sh_ffn, d_sc_ffn, d_gt_ffn], axis=1)
        small["rel_bias"][l] = _bias_fold(d_tab)
        small["g_a"][l], small["g_b"][l] = d_g[:, :WG], d_g[:, WG:]
        small["conv_b"][l] = d_cb
        small["conv_w"][l] = d_cw.reshape(NDEV, 3, GU)
    grad_x = dx.reshape(1, s, D)

    contrib = {k: jnp.stack(vs) for k, vs in small.items()}
    contrib["final_g"] = d_final_g
    gath = _small_allgather(_pack(contrib), "gather_small_grads")
    tok = send("in", 0, _wgrad_in(saved[0]["h1"], dparts, tmm, gath))

    def place_conv_w(t):
        return lax.dynamic_update_slice(jnp.zeros((2, NDEV, 3, GU), F32), t.reshape(2, 1, 3, GU), (0, me, 0, 0))

    def packed_params(b, rb, ga, gb, cb_, fg, cw_):
        return _pack({"b_ada": b, "rel_bias": rb, "g_a": ga, "g_b": gb, "conv_b": cb_, "final_g": fg,
                      "conv_w": place_conv_w(cw_)})

    sm = _adamw_small(gath,
                      packed_params(b_ada, rel_bias, g_a, g_b, conv_b, final_g, conv_w),
                      packed_params(m_b_ada, m_rel_bias, m_g_a, m_g_b, m_conv_b, m_final_g, m_conv_w),
                      packed_params(v_b_ada, v_rel_bias, v_g_a, v_g_b, v_conv_b, v_final_g, v_conv_w))
    sm = [_unpack(t) for t in sm]

    dmod_all = gath[:, :12, :].reshape(NDEV, 2, 6 * D)
    dmod_sl = jnp.transpose(lax.dynamic_slice(dmod_all, (0, 0, me * ADA_SH), (NDEV, 2, ADA_SH)), (1, 0, 2))
    ada = _adamw_w_ada(cact.T, dmod_sl, w_ada, m_w_ada, v_w_ada)

    big = {}
    for kind, (w, m, v) in (("down", (w_down, m_w_down, v_w_down)), ("up", up_t),
                            ("out", (w_out, m_w_out, v_w_out)), ("in", (w_in, m_w_in, v_w_in))):
        recv0, recv1 = _exchange_wait([kind, kind], False, [sent[kind, 0], sent[kind, 1]], tok, f"grads_wait_{kind}")
        big[kind] = _adamw_big(recv0, recv1, w, m, v)
        tok = big[kind][0]
    big["up"] = [jnp.transpose(t, (0, 2, 1)) for t in big["up"]]

    def small_out(j, name):
        t = sm[j][name]
        if name == "b_ada":
            return t.reshape(2, 6 * D)
        if name == "rel_bias":
            return t.reshape(2, 8, 257)
        if name in ("g_a", "g_b"):
            return t.reshape(2, WG)
        if name == "conv_b":
            return t.reshape(2, 2 * DFF)
        if name == "final_g":
            return t.reshape(D)
        t = t.reshape(2, NDEV, 3, GU)
        return lax.dynamic_index_in_dim(t, me, axis=1, keepdims=False)

    def group(j):
        return (ada[j], small_out(j, "b_ada"), big["in"][j], small_out(j, "rel_bias"), small_out(j, "g_a"),
                small_out(j, "g_b"), big["out"][j], big["up"][j], small_out(j, "conv_w"), small_out(j, "conv_b"),
                big["down"][j], small_out(j, "final_g"))

    return (loss, grad_x, *group(0), *group(1), *group(2), *group(3))
```

```python
import jax
import jax.numpy as jnp
from jax import lax
from jax.experimental import pallas as pl
from jax.experimental.pallas import tpu as pltpu

F32, BF16 = jnp.float32, jnp.bfloat16
MESH_ID = pl.DeviceIdType.MESH
NDEV = 8
D = 1024
HD = 64
WG = 512
NT = 6
GU = 704
DFF = 2816
IN_SH = NT * WG // NDEV
OUT_SH = D // NDEV
DOWN_SH = DFF // NDEV
ADA_SH = 6 * D // NDEV
LANES = 128
CHUNK, NPREV, REL_CLIP = 64, 8, 128
BAND = (NPREV + 1) * CHUNK
EPS = 1e-6
NEG = -1e30
TQA = 256
KWA = TQA + NPREV * CHUNK
TABW = 1024
TAB0 = TABW - KWA
N_FAR = TAB0 + NPREV * CHUNK - REL_CLIP + 1
N_NEAR = REL_CLIP + CHUNK - 1
TQB, TKB = 512, 128
KSTEP = 4
LR, B1, B2, AEPS, WD, STEP = 0.001, 0.9, 0.999, 1e-08, 0.01, 10
VMEM_MB = 56


def _call(body, **kw):
    return pl.pallas_call(body, **kw)


def _call_after(body, after, **kw):
    n_in = len(kw["in_specs"])
    kw["in_specs"] = list(kw["in_specs"]) + [pl.BlockSpec(memory_space=pl.ANY)]

    def tied(*refs):
        body(*refs[:n_in], *refs[n_in + 1:])

    call = _call(tied, **kw)
    return lambda *args: call(*args, after)


def _params(n_axes):
    return pltpu.CompilerParams(dimension_semantics=("arbitrary",) * n_axes, vmem_limit_bytes=VMEM_MB << 20)


def _dot(a, b):
    return jnp.dot(a, b, preferred_element_type=F32)


def _dot_nt(a, b):
    return lax.dot_general(a, b, (((1,), (1,)), ((), ())), preferred_element_type=F32)


def _dot_tn(a, b):
    return lax.dot_general(a, b, (((0,), (0,)), ((), ())), preferred_element_type=F32)


def _dot2(x, u):
    hi = x.astype(BF16)
    lo = (x - hi.astype(F32)).astype(BF16)
    return _dot(hi, u) + _dot(lo, u)


def _dot2_deep(x, uu):
    hi = x.astype(BF16)
    lo = (x - hi.astype(F32)).astype(BF16)
    return _dot(jnp.concatenate([hi, lo], axis=1), uu)


def _eye(n):
    i = lax.broadcasted_iota(jnp.int32, (n, n), 0)
    j = lax.broadcasted_iota(jnp.int32, (n, n), 1)
    return jnp.where(i == j, 1.0, 0.0).astype(BF16)


def _transpose_bf16(x, eye):
    return _dot_nt(eye, x).astype(BF16)


def _transpose_f32(x, eye):
    hi = x.astype(BF16)
    lo = (x - hi.astype(F32)).astype(BF16)
    return _dot_nt(eye, hi) + _dot_nt(eye, lo)


def _rms(x):
    return lax.rsqrt(jnp.mean(x * x, axis=-1, keepdims=True) + EPS)


def _rms_bwd(dn, n, r):
    return r * (dn - n * jnp.mean(dn * n, axis=-1, keepdims=True))


def _colsum(x):
    return jnp.sum(x, axis=0, keepdims=True)


def _sigmoid(x):
    return 0.5 * jnp.tanh(0.5 * x) + 0.5


def _sds(shape, dtype):
    return jax.ShapeDtypeStruct(shape, dtype)


def _place():
    x, y, c = lax.axis_index("x"), lax.axis_index("y"), lax.axis_index("c")
    return x, y, c, 4 * x + 2 * y + c


def _peer(x, y, c, k):
    px = 1 - x if k & 4 else x
    py = 1 - y if k & 2 else y
    pc = 1 - c if k & 1 else c
    return (px, py, pc), 4 * px + 2 * py + pc


def _small_allgather(v, name):
    rows, cols = v.shape

    def body(v_ref, out_ref, send_sems, recv_sems, local_sem):
        x, y, c, me = _place()
        mine = pltpu.make_async_copy(v_ref, out_ref.at[me], local_sem)
        mine.start()
        sends = []
        for k in range(1, NDEV):
            peer, _ = _peer(x, y, c, k)
            cp = pltpu.make_async_remote_copy(v_ref, out_ref.at[me], send_sems.at[k - 1], recv_sems.at[k - 1],
                                              device_id=peer, device_id_type=MESH_ID)
            cp.start()
            sends.append(cp)
        for k in range(1, NDEV):
            peer, pidx = _peer(x, y, c, k)
            pltpu.make_async_remote_copy(v_ref, out_ref.at[pidx], send_sems.at[k - 1], recv_sems.at[k - 1],
                                         device_id=peer, device_id_type=MESH_ID).wait_recv()
        for cp in sends:
            cp.wait_send()
        mine.wait()

    return _call(
        body, name=name,
        out_shape=_sds((NDEV, rows, cols), F32),
        in_specs=[pl.BlockSpec(memory_space=pltpu.VMEM)],
        out_specs=pl.BlockSpec(memory_space=pltpu.VMEM),
        scratch_shapes=[pltpu.SemaphoreType.DMA((NDEV - 1,)), pltpu.SemaphoreType.DMA((NDEV - 1,)),
                        pltpu.SemaphoreType.DMA],
    )(v)


def _shard_view(ref, kind, p):
    if kind == "in":
        return ref.at[:, pl.ds(pl.multiple_of(p * IN_SH, LANES), IN_SH)]
    if kind == "out":
        return ref.at[pl.ds(pl.multiple_of(p * OUT_SH, OUT_SH), OUT_SH), :]
    if kind == "up":
        return ref.at[p]
    if kind == "down":
        return ref.at[pl.ds(pl.multiple_of(p * DOWN_SH, 16), DOWN_SH), :]
    raise ValueError(kind)


_KINDS = ("in", "out", "up", "down")
_FULL_SHAPES = {"in": (D, 3 * D), "out": (D, D), "up": (NDEV, GU, D), "down": (DFF, D)}
_SHARD_SHAPES = {"in": (D, IN_SH), "out": (OUT_SH, D), "up": (GU, D), "down": (DOWN_SH, D)}


_HBM = pl.BlockSpec(memory_space=pltpu.HBM)
_SEM = pl.BlockSpec(memory_space=pltpu.SEMAPHORE)
_EFFECT = pltpu.SideEffectType.DATAFLOW_SIDE_EFFECTING
_SEM_SHAPES = (pltpu.SemaphoreType.DMA((NDEV - 1,)), pltpu.SemaphoreType.DMA((NDEV - 1,)), pltpu.SemaphoreType.DMA(()))


def _hbm(a):
    return pltpu.with_memory_space_constraint(a, pltpu.HBM)


def _exchange_copies(kind, gather, src, land, sems):
    send_sems, recv_sems, local_sem = sems
    x, y, c, me = _place()

    def ends(p_dst, p_from):
        if gather:
            return src, _shard_view(land, kind, me), _shard_view(land, kind, p_from)
        return _shard_view(src, kind, p_dst), land.at[me], land.at[p_from]

    s_me, d_me, _ = ends(me, me)
    local = pltpu.make_async_copy(s_me, d_me, local_sem)
    sends, arrivals = [], []
    for k in range(1, NDEV):
        peer, pidx = _peer(x, y, c, k)
        s_k, d_k, from_k = ends(pidx, pidx)
        sends.append(pltpu.make_async_remote_copy(s_k, d_k, send_sems.at[k - 1], recv_sems.at[k - 1],
                                                  device_id=peer, device_id_type=MESH_ID))
        arrivals.append(pltpu.make_async_remote_copy(s_k, from_k, send_sems.at[k - 1], recv_sems.at[k - 1],
                                                     device_id=peer, device_id_type=MESH_ID))
    return local, sends, arrivals


def _exchange_start(kinds, gather, srcs, name, with_token=False):
    n = len(kinds)
    lands = [lax.empty(_FULL_SHAPES[kd] if gather else (NDEV,) + _SHARD_SHAPES[kd], BF16) for kd in kinds]

    def body(*refs):
        ins, sems = refs[:2 * n], refs[2 * n:5 * n]
        for j, kd in enumerate(kinds):
            local, sends, _ = _exchange_copies(kd, gather, ins[j], ins[n + j], sems[3 * j:3 * j + 3])
            local.start()
            for cp in sends:
                cp.start()
        if with_token:
            token = refs[7 * n]
            token[...] = jnp.zeros_like(token)

    out_shape = list(_SEM_SHAPES) * n
    out_shape += [pltpu.HBM(a.shape, a.dtype) for a in srcs] + [pltpu.HBM(a.shape, a.dtype) for a in lands]
    out_specs = [_SEM] * (3 * n) + [_HBM] * (2 * n)
    if with_token:
        out_shape.append(_sds((8, 128), F32))
        out_specs.append(pl.BlockSpec(memory_space=pltpu.VMEM))
    outs = _call(
        body, name=name, out_shape=out_shape,
        in_specs=[_HBM] * (2 * n), out_specs=out_specs,
        input_output_aliases={i: 3 * n + i for i in range(2 * n)},
        compiler_params=pltpu.CompilerParams(has_side_effects=_EFFECT),
    )(*[_hbm(a) for a in srcs], *[_hbm(a) for a in lands])
    per_tensor = [(tuple(outs[3 * j:3 * j + 3]), outs[3 * n + j], outs[4 * n + j]) for j in range(n)]
    return (per_tensor, outs[5 * n]) if with_token else per_tensor


def _exchange_wait(kinds, gather, started, after, name):
    n = len(kinds)

    def body(*refs):
        ins, sems = refs[:2 * n], refs[2 * n:5 * n]
        for j, kd in enumerate(kinds):
            local, sends, arrivals = _exchange_copies(kd, gather, ins[j], ins[n + j], sems[3 * j:3 * j + 3])
            local.wait()
            for cp in arrivals:
                cp.wait_recv()
            for cp in sends:
                cp.wait_send()

    srcs = [st[1] for st in started]
    lands = [st[2] for st in started]
    sems = [sm for st in started for sm in st[0]]
    outs = _call(
        body, name=name,
        out_shape=[pltpu.HBM(a.shape, a.dtype) for a in srcs] + [pltpu.HBM(a.shape, a.dtype) for a in lands],
        in_specs=[_HBM] * (2 * n) + [_SEM] * (3 * n) + [pl.BlockSpec(memory_space=pl.ANY)],
        out_specs=[_HBM] * (2 * n),
        input_output_aliases={i: i for i in range(2 * n)},
        compiler_params=pltpu.CompilerParams(has_side_effects=_EFFECT),
    )(*srcs, *lands, *sems, after)
    return outs[n:]


def _mod_fwd(c_all, w_ada, b_sl):
    def body(c_ref, w_ref, b_ref, mod_ref, cact_ref):
        cv = c_ref[...]
        ca = cv * _sigmoid(cv)
        cact_ref[...] = ca
        mod_ref[0] = _dot(ca.astype(BF16), w_ref[0].astype(BF16)) + b_ref[0]

    return _call(
        body, name="mod_fwd", grid=(2,),
        in_specs=[pl.BlockSpec((NDEV, D), lambda l: (0, 0)), pl.BlockSpec((1, D, ADA_SH), lambda l: (l, 0, 0)),
                  pl.BlockSpec((1, 1, ADA_SH), lambda l: (l, 0, 0))],
        out_specs=[pl.BlockSpec((1, NDEV, ADA_SH), lambda l: (l, 0, 0)), pl.BlockSpec((NDEV, D), lambda l: (0, 0))],
        out_shape=[_sds((2, NDEV, ADA_SH), F32), _sds((NDEV, D), F32)],
        compiler_params=_params(1),
    )(c_all, w_ada, b_sl)


def _nm_matmul(x, shift, scale, w, *, two_d, n, groups, out_dtype, tm, name):
    s = x.shape[0]

    def body(x_ref, sh_ref, sc_ref, w_ref, h_ref, o_ref):
        @pl.when(pl.program_id(1) == 0)
        def _():
            xv = x_ref[...]
            h_ref[...] = ((xv * _rms(xv)) * (1.0 + sc_ref[...]) + sh_ref[...]).astype(BF16)
        if two_d:
            o_ref[0] = _dot(h_ref[...], w_ref[...]).astype(out_dtype)
        else:
            o_ref[0] = _dot_nt(h_ref[...], w_ref[0]).astype(out_dtype)

    vec = pl.BlockSpec((1, D), lambda i, g: (0, 0))
    w_spec = pl.BlockSpec((D, n), lambda i, g: (0, g)) if two_d else pl.BlockSpec((1, n, D), lambda i, g: (g, 0, 0))
    return _call(
        body, name=name, grid=(s // tm, groups),
        in_specs=[pl.BlockSpec((tm, D), lambda i, g: (i, 0)), vec, vec, w_spec],
        out_specs=[pl.BlockSpec((tm, D), lambda i, g: (i, 0)), pl.BlockSpec((1, tm, n), lambda i, g: (g, i, 0))],
        out_shape=[_sds((s, D), BF16), _sds((groups, s, n), out_dtype)],
        compiler_params=_params(2),
    )(x, shift, scale, w)


def _bias_table(rel_bias):
    far = jnp.broadcast_to(rel_bias[:, 2 * REL_CLIP:], (NDEV, N_FAR))
    near = rel_bias[:, 2 * REL_CLIP - 1:REL_CLIP - CHUNK:-1]
    fpad = jnp.concatenate([far, near, jnp.zeros((NDEV, TABW - N_FAR - N_NEAR), F32)], axis=1)

    def body(f_ref, o_ref):
        t = pltpu.roll(jnp.broadcast_to(f_ref[0], (TQA, TABW)), 0, 1, stride=1, stride_axis=0)[:, TAB0:]
        rows = lax.broadcasted_iota(jnp.int32, (TQA, KWA), 0)
        cols = lax.broadcasted_iota(jnp.int32, (TQA, KWA), 1)
        first = jnp.bitwise_and(rows, -CHUNK)
        o_ref[0] = jnp.where((cols >= first) & (cols < first + BAND), t, NEG)

    return _call(
        body, name="bias_table", grid=(NDEV,),
        in_specs=[pl.BlockSpec((1, 1, TABW), lambda h: (h, 0, 0))],
        out_specs=pl.BlockSpec((1, TQA, KWA), lambda h: (h, 0, 0)),
        out_shape=_sds((NDEV, TQA, KWA), F32),
        compiler_params=_params(1),
    )(fpad.reshape(NDEV, 1, TABW))


def _attn_a_cases(qi, tile):
    @pl.when(qi == 0)
    def _():
        tile(TQA, 2 * TQA, 0)

    @pl.when(qi == 1)
    def _():
        tile(2 * TQA, TQA, 0)

    @pl.when(qi >= 2)
    def _():
        tile(KWA, 0, pl.multiple_of((qi - 2) * TQA, TQA))


HA = 8
WA = HA * HD


def _attn_a_specs(s):
    q_spec = pl.BlockSpec((1, TQA, WA), lambda hp, qi: (0, qi, hp))
    k_spec = pl.BlockSpec((1, s, WA), lambda hp, qi: (1, 0, hp))
    v_spec = pl.BlockSpec((1, s, WA), lambda hp, qi: (2, 0, hp))
    b_spec = pl.BlockSpec((HA, TQA, KWA), lambda hp, qi: (hp, 0, 0))
    return q_spec, k_spec, v_spec, b_spec


def _attn_a_fwd(proj, bias_tab):
    s = proj.shape[1]

    def body(q_ref, k_ref, v_ref, b_ref, o_ref):
        def tile(nk, off, kstart):
            for h in range(HA):
                sl = slice(HD * h, HD * (h + 1))
                q = q_ref[0, :, sl]
                k = k_ref[0, pl.ds(kstart, nk), sl]
                v = v_ref[0, pl.ds(kstart, nk), sl]
                sc = _dot_nt(q, k) * 0.125 + b_ref[h, :, off:off + nk]
                p = jnp.exp(sc - jnp.max(sc, axis=-1, keepdims=True))
                den = jnp.sum(p, axis=-1, keepdims=True)
                o_ref[:, sl] = _dot(p.astype(BF16), v) / den

        _attn_a_cases(pl.program_id(1), tile)

    q_spec, k_spec, v_spec, b_spec = _attn_a_specs(s)
    return _call(
        body, name="attn_a_fwd", grid=(WG // WA, s // TQA),
        in_specs=[q_spec, k_spec, v_spec, b_spec],
        out_specs=pl.BlockSpec((TQA, WA), lambda hp, qi: (qi, hp)),
        out_shape=_sds((s, WG), F32),
        compiler_params=_params(2),
    )(proj, proj, proj, bias_tab)


def _sb_terms(lg):
    sp = jnp.maximum(lg, 0.0) + jnp.log(1.0 + jnp.exp(-jnp.abs(lg)))
    return lg - sp, sp


def _attn_b_specs(s):
    q_spec = pl.BlockSpec((1, TQB, 128), lambda hp, qi: (3, qi, hp))
    k_spec = pl.BlockSpec((1, s, 128), lambda hp, qi: (4, 0, hp))
    v_spec = pl.BlockSpec((1, s, 128), lambda hp, qi: (5, 0, hp))
    return q_spec, k_spec, v_spec


def _attn_b_fwd(proj):
    s = proj.shape[1]
    ndiag = TQB // TKB

    def body(q_ref, k_ref, v_ref, o_ref, ls_ref):
        q0 = pl.program_id(1) * TQB
        heads = [slice(HD * h, HD * (h + 1)) for h in range(2)]

        rows = lax.broadcasted_iota(jnp.int32, (TQB, 2 * TKB), 0)
        lane = lax.broadcasted_iota(jnp.int32, (TQB, 2 * TKB), 1)
        cols = jnp.bitwise_and(lane, TKB - 1)
        bj = lax.broadcasted_iota(jnp.int32, (2 * TKB, 2 * TKB), 0)
        bs = lax.broadcasted_iota(jnp.int32, (2 * TKB, 2 * TKB), 1)
        suffix_pair = jnp.where((bj >= bs) & ((bj >= TKB) == (bs >= TKB)), 1.0, 0.0).astype(BF16)
        q_pair = (q_ref[0].astype(F32) * 0.125).astype(BF16)
        zero = jnp.zeros((TKB, HD), BF16)

        def pair(x):
            return jnp.concatenate([jnp.concatenate([x[:, :HD], zero], axis=1),
                                    jnp.concatenate([zero, x[:, HD:]], axis=1)], axis=0)

        def diag_tile(kstart, d, carry, acc):
            k = k_ref[0, pl.ds(kstart, TKB), :]
            v = v_ref[0, pl.ds(kstart, TKB), :]
            lb, sp = _sb_terms(_dot_nt(q_pair, pair(k)))
            strict = rows > cols + d * TKB
            sp = jnp.where(strict, sp, 0.0)
            csum = _dot2(sp, suffix_pair) + carry
            w = jnp.where(strict, jnp.exp(lb - csum + sp), 0.0)
            carry = jnp.where(lane >= TKB, csum[:, TKB:TKB + 1], csum[:, 0:1])
            return carry, acc + _dot(w.astype(BF16), pair(v))

        both = (jnp.zeros((TQB, 2 * TKB), F32), jnp.zeros((TQB, 2 * HD), F32))
        for d in range(ndiag - 1, -1, -1):
            both = diag_tile(pl.multiple_of(q0 + d * TKB, TKB), d, *both)
        state = [both[0][:, 0:1], both[1][:, :HD], both[0][:, TKB:TKB + 1], both[1][:, HD:]]

        uj = lax.broadcasted_iota(jnp.int32, (TKB, TKB), 0)
        us = lax.broadcasted_iota(jnp.int32, (TKB, TKB), 1)
        suffix = jnp.where(uj >= us, 1.0, 0.0).astype(BF16)
        qs = [q_pair[:, sl] for sl in heads]

        def tile(h, kstart, carry, acc):
            k = k_ref[0, pl.ds(kstart, TKB), heads[h]]
            v = v_ref[0, pl.ds(kstart, TKB), heads[h]]
            lb, sp = _sb_terms(_dot_nt(qs[h], k))
            csum = _dot2(sp, suffix) + carry
            w = jnp.exp(lb - csum + sp)
            return csum[:, 0:1], acc + _dot(w.astype(BF16), v)

        nsteps = q0 // (KSTEP * TKB)

        def step(i, st):
            st = list(st)
            base = (nsteps - 1 - i) * (KSTEP * TKB)
            for sub in range(KSTEP - 1, -1, -1):
                for h in range(2):
                    st[2 * h:2 * h + 2] = tile(h, pl.multiple_of(base + sub * TKB, TKB), *st[2 * h:2 * h + 2])
            return tuple(st)

        state = lax.fori_loop(0, nsteps, step, tuple(state))
        for h in range(2):
            o_ref[:, heads[h]] = state[2 * h + 1]
            ls_ref[0, :, heads[h]] = jnp.broadcast_to(state[2 * h], (TQB, HD))

    q_spec, k_spec, v_spec = _attn_b_specs(s)
    return _call(
        body, name="attn_b_fwd", grid=(4, s // TQB),
        in_specs=[q_spec, k_spec, v_spec],
        out_specs=[pl.BlockSpec((TQB, 128), lambda hp, qi: (qi, hp)),
                   pl.BlockSpec((1, TQB, 128), lambda hp, qi: (hp, qi, 0))],
        out_shape=[_sds((s, WG), F32), _sds((4, s, 128), F32)],
        compiler_params=_params(2),
    )(proj, proj, proj)


def _mix_out(oa, ob, g, w_out, x, gate, tm):
    s = x.shape[0]

    def body(oa_ref, ob_ref, g_ref, w_ref, x_ref, gate_ref, nab_ref, mixed_ref, x2_ref):
        a, b = oa_ref[...], ob_ref[...]
        nab_ref[:, :WG] = (a * _rms(a) * g_ref[:, :WG]).astype(BF16)
        nab_ref[:, WG:] = (b * _rms(b) * g_ref[:, WG:]).astype(BF16)
        mixed = _dot(nab_ref[...], w_ref[...])
        mixed_ref[...] = mixed
        x2_ref[...] = x_ref[...] + gate_ref[...] * mixed

    row = pl.BlockSpec((tm, D), lambda i: (i, 0))
    half = pl.BlockSpec((tm, WG), lambda i: (i, 0))
    vec = pl.BlockSpec((1, D), lambda i: (0, 0))
    return _call(
        body, name="mix_out", grid=(s // tm,),
        in_specs=[half, half, vec, pl.BlockSpec((D, D), lambda i: (0, 0)), row, vec],
        out_specs=[row, row, row],
        out_shape=[_sds((s, D), BF16), _sds((s, D), F32), _sds((s, D), F32)],
        compiler_params=_params(1),
    )(oa, ob, g, w_out, x, gate)


HALO = 16


def _conv_taps(u, halo, first, tm):
    keep = jnp.where(first, 0.0, 1.0)
    window = jnp.concatenate([halo[HALO - 8:HALO, :] * keep, u[0:8, :]], axis=0)
    um1 = jnp.concatenate([pltpu.roll(window, 1, 0)[8:16, :], pltpu.roll(u, 1, 0)[8:, :]], axis=0)
    um2 = jnp.concatenate([pltpu.roll(window, 2, 0)[8:16, :], pltpu.roll(u, 2, 0)[8:, :]], axis=0)
    return um1, um2


def _conv_specs(tm):
    u_spec = pl.BlockSpec((2, 1, tm, GU), lambda p, i: (0, p, i, 0))
    halo_spec = pl.BlockSpec((2, 1, HALO, GU), lambda p, i: (0, p, jnp.maximum(i * (tm // HALO) - 1, 0), 0))
    cw_spec = pl.BlockSpec((2, 1, 3, GU), lambda p, i: (0, p, 0, 0))
    cb_spec = pl.BlockSpec((2, 1, 1, GU), lambda p, i: (0, p, 0, 0))
    return u_spec, halo_spec, cw_spec, cb_spec


def _conv_act(u, conv_w, conv_b, tm):
    s = u.shape[2]

    def body(u_ref, halo_ref, cw_ref, cb_ref, a_ref):
        first = pl.program_id(1) == 0
        ys = []
        for side in range(2):
            uv = u_ref[side, 0].astype(F32)
            um1, um2 = _conv_taps(uv, halo_ref[side, 0].astype(F32), first, tm)
            cw = cw_ref[side, 0]
            ys.append(cw[2:3] * uv + cw[1:2] * um1 + cw[0:1] * um2 + cb_ref[side, 0])
        a_ref[0] = (ys[0] * _sigmoid(ys[0]) * ys[1]).astype(BF16)

    u_spec, halo_spec, cw_spec, cb_spec = _conv_specs(tm)
    return _call(
        body, name="conv_act", grid=(4, s // tm),
        in_specs=[u_spec, halo_spec, cw_spec, cb_spec],
        out_specs=pl.BlockSpec((1, tm, GU), lambda p, i: (p, i, 0)),
        out_shape=_sds((4, s, GU), BF16),
        compiler_params=_params(2),
    )(u, u, conv_w, conv_b)


def _down(a, w_down, x2, gate, tm):
    s = x2.shape[0]

    def body(a_ref, w_ref, x_ref, gate_ref, ffn_ref, x3_ref):
        p = pl.program_id(1)
        part = _dot(a_ref[0], w_ref[0])

        @pl.when(p == 0)
        def _():
            ffn_ref[...] = part

        @pl.when(p > 0)
        def _():
            ffn_ref[...] += part

        @pl.when(p == 3)
        def _():
            x3_ref[...] = x_ref[...] + gate_ref[...] * ffn_ref[...]

    row = pl.BlockSpec((tm, D), lambda i, p: (i, 0))
    return _call(
        body, name="down", grid=(s // tm, 4),
        in_specs=[pl.BlockSpec((1, tm, GU), lambda i, p: (p, i, 0)), pl.BlockSpec((1, GU, D), lambda i, p: (p, 0, 0)),
                  row, pl.BlockSpec((1, D), lambda i, p: (0, 0))],
        out_specs=[row, row],
        out_shape=[_sds((s, D), F32), _sds((s, D), F32)],
        compiler_params=_params(2),
    )(a, w_down, x2, gate)


def _final_loss(x, g, target, tm):
    s = x.shape[0]

    def body(x_ref, g_ref, t_ref, loss_ref, dx_ref, dg_ref):
        @pl.when(pl.program_id(0) == 0)
        def _():
            loss_ref[...] = jnp.zeros_like(loss_ref)
            dg_ref[...] = jnp.zeros_like(dg_ref)
        xv = x_ref[...]
        r = _rms(xv)
        nrm = xv * r
        err = nrm * g_ref[...] - t_ref[...]
        loss_ref[...] += (0.5 / D) * jnp.sum(jnp.sum(err * err, axis=-1, keepdims=True), axis=0, keepdims=True)
        dy = err * (1.0 / D)
        dg_ref[...] += _colsum(dy * nrm)
        dx_ref[...] = _rms_bwd(dy * g_ref[...], nrm, r)

    row = pl.BlockSpec((tm, D), lambda i: (i, 0))
    vec = pl.BlockSpec((1, D), lambda i: (0, 0))
    return _call(
        body, name="final_loss", grid=(s // tm,),
        in_specs=[row, vec, row],
        out_specs=[pl.BlockSpec((1, 1), lambda i: (0, 0)), row, vec],
        out_shape=[_sds((1, 1), F32), _sds((s, D), F32), _sds((1, D), F32)],
        compiler_params=_params(1),
    )(x, g, target)


def _down_bwd(dx3, gate, ffn, w_down, tm):
    s = dx3.shape[0]

    def body(dx_ref, gate_ref, ffn_ref, w_ref, dgate_ref, dff_ref, da_ref):
        i, p = pl.program_id(0), pl.program_id(1)

        @pl.when((i == 0) & (p == 0))
        def _():
            dgate_ref[...] = jnp.zeros_like(dgate_ref)

        @pl.when(p == 0)
        def _():
            dxv = dx_ref[...]
            dgate_ref[...] += _colsum(dxv * ffn_ref[...])
            dff_ref[...] = (dxv * gate_ref[...]).astype(BF16)

        da_ref[0] = _dot_nt(dff_ref[...], w_ref[0]).astype(BF16)

    row = pl.BlockSpec((tm, D), lambda i, p: (i, 0))
    vec = pl.BlockSpec((1, D), lambda i, p: (0, 0))
    return _call(
        body, name="down_bwd", grid=(s // tm, 4),
        in_specs=[row, vec, row, pl.BlockSpec((1, GU, D), lambda i, p: (p, 0, 0))],
        out_specs=[vec, row, pl.BlockSpec((1, tm, GU), lambda i, p: (p, i, 0))],
        out_shape=[_sds((1, D), F32), _sds((s, D), BF16), _sds((4, s, GU), BF16)],
        compiler_params=_params(2),
    )(dx3, gate, ffn, w_down)


def _conv_act_bwd(u, conv_w, conv_b, da, tm, after):
    s = u.shape[2]

    def body(u_ref, halo_ref, cw_ref, cb_ref, da_ref, dy_ref, dcw_ref, dcb_ref):
        first = pl.program_id(1) == 0

        @pl.when(first)
        def _():
            dcw_ref[...] = jnp.zeros_like(dcw_ref)
            dcb_ref[...] = jnp.zeros_like(dcb_ref)

        taps, ys = [], []
        for side in range(2):
            uv = u_ref[side, 0].astype(F32)
            um1, um2 = _conv_taps(uv, halo_ref[side, 0].astype(F32), first, tm)
            cw = cw_ref[side, 0]
            taps.append((um2, um1, uv))
            ys.append(cw[2:3] * uv + cw[1:2] * um1 + cw[0:1] * um2 + cb_ref[side, 0])
        dav = da_ref[0].astype(F32)
        sg = _sigmoid(ys[0])
        dys = (dav * ys[1] * (sg * (1.0 + ys[0] * (1.0 - sg))), dav * (ys[0] * sg))
        for side in range(2):
            dy_ref[side, 0] = dys[side].astype(BF16)
            dcb_ref[side, 0] += _colsum(dys[side])
            for j in range(3):
                dcw_ref[side, 0, j:j + 1, :] += _colsum(dys[side] * taps[side][j])

    u_spec, halo_spec, cw_spec, cb_spec = _conv_specs(tm)
    return _call_after(
        body, after, name="conv_act_bwd", grid=(4, s // tm),
        in_specs=[u_spec, halo_spec, cw_spec, cb_spec, pl.BlockSpec((1, tm, GU), lambda p, i: (p, i, 0))],
        out_specs=[u_spec, cw_spec, cb_spec],
        out_shape=[_sds((2, 4, s, GU), BF16), _sds((2, 4, 3, GU), F32), _sds((2, 4, 1, GU), F32)],
        compiler_params=_params(2),
    )(u, u, conv_w, conv_b, da)


def _conv_transpose(dy, conv_w, tm):
    s = dy.shape[1]
    nt = s // tm

    def body(dy_ref, halo_ref, cw_ref, du_ref):
        keep = jnp.where(pl.program_id(1) == nt - 1, 0.0, 1.0)
        dv = dy_ref[0].astype(F32)
        window = jnp.concatenate([dv[tm - 8:, :], halo_ref[0, 0:8, :].astype(F32) * keep], axis=0)
        dp1 = jnp.concatenate([pltpu.roll(dv, tm - 1, 0)[:tm - 8, :], pltpu.roll(window, 15, 0)[0:8, :]], axis=0)
        dp2 = jnp.concatenate([pltpu.roll(dv, tm - 2, 0)[:tm - 8, :], pltpu.roll(window, 14, 0)[0:8, :]], axis=0)
        cw = cw_ref[0]
        du_ref[0] = (cw[2:3] * dv + cw[1:2] * dp1 + cw[0:1] * dp2).astype(BF16)

    blk = pl.BlockSpec((1, tm, GU), lambda g, i: (g, i, 0))
    return _call(
        body, name="conv_transpose", grid=(NDEV, nt),
        in_specs=[blk, pl.BlockSpec((1, HALO, GU),
                                    lambda g, i: (g, jnp.minimum((i + 1) * (tm // HALO), s // HALO - 1), 0)),
                  pl.BlockSpec((1, 3, GU), lambda g, i: (g, 0, 0))],
        out_specs=blk,
        out_shape=_sds((NDEV, s, GU), BF16),
        compiler_params=_params(2),
    )(dy, dy, conv_w)


def _wgrad(a3, b3, *, tk, name):
    ga, s, ka = a3.shape
    gb, _, nb = b3.shape
    groups = max(ga, gb)
    nk = s // tk

    def body(a_ref, b_ref, o_ref, acc):
        k = pl.program_id(1)

        @pl.when(k == 0)
        def _():
            acc[...] = jnp.zeros_like(acc)

        acc[...] += _dot_tn(a_ref[0], b_ref[0])

        @pl.when(k == nk - 1)
        def _():
            o_ref[0] = acc[...].astype(BF16)

    a_spec = pl.BlockSpec((1, tk, ka), (lambda g, k: (g, k, 0)) if ga > 1 else (lambda g, k: (0, k, 0)))
    b_spec = pl.BlockSpec((1, tk, nb), (lambda g, k: (g, k, 0)) if gb > 1 else (lambda g, k: (0, k, 0)))
    return _call(
        body, name=name, grid=(groups, nk),
        in_specs=[a_spec, b_spec], out_specs=pl.BlockSpec((1, ka, nb), lambda g, k: (g, 0, 0)),
        out_shape=_sds((groups, ka, nb), BF16),
        scratch_shapes=[pltpu.VMEM((ka, nb), F32)],
        compiler_params=_params(2),
    )(a3, b3)


def _wgrad_in(h1, dparts, tk, after):
    s = h1.shape[0]
    nk = s // tk

    def body(a_ref, *refs):
        d_refs, o_ref, acc = refs[:NT], refs[NT], refs[NT + 1]
        k = pl.program_id(0)

        @pl.when(k == 0)
        def _():
            acc[...] = jnp.zeros_like(acc)

        for j in range(NT):
            acc[:, WG * j:WG * (j + 1)] += _dot_tn(a_ref[...], d_refs[j][...])

        @pl.when(k == nk - 1)
        def _():
            o_ref[...] = acc[...].astype(BF16)

    kw = dict(
        name="wgrad_in", grid=(nk,),
        in_specs=[pl.BlockSpec((tk, D), lambda k: (k, 0))] + [pl.BlockSpec((tk, WG), lambda k: (k, 0))] * NT,
        out_specs=pl.BlockSpec((D, NT * WG), lambda k: (0, 0)),
        out_shape=_sds((D, NT * WG), BF16),
        scratch_shapes=[pltpu.VMEM((D, NT * WG), F32)],
        compiler_params=_params(1),
    )
    call = _call(body, **kw) if after is None else _call_after(body, after, **kw)
    return call(h1, *dparts)


def _dgrad_in(dparts, w, x_in, scale, dx_up, tm, after):
    s = x_in.shape[0]

    def body(*refs):
        d_refs = refs[:NT]
        w_ref, x_ref, sc_ref, up_ref, dx_ref, dsc_ref, dsh_ref = refs[NT:]

        @pl.when(pl.program_id(0) == 0)
        def _():
            dsc_ref[...] = jnp.zeros_like(dsc_ref)
            dsh_ref[...] = jnp.zeros_like(dsh_ref)

        dh = _dot_nt(d_refs[0][...], w_ref[:, 0:WG])
        for j in range(1, NT):
            dh = dh + _dot_nt(d_refs[j][...], w_ref[:, WG * j:WG * (j + 1)])
        xv = x_ref[...]
        r = _rms(xv)
        nrm = xv * r
        dsh_ref[...] += _colsum(dh)
        dsc_ref[...] += _colsum(dh * nrm)
        dx_ref[...] = up_ref[...] + _rms_bwd(dh * (1.0 + sc_ref[...]), nrm, r)

    row = pl.BlockSpec((tm, D), lambda i: (i, 0))
    vec = pl.BlockSpec((1, D), lambda i: (0, 0))
    return _call_after(
        body, after, name="dgrad_in", grid=(s // tm,),
        in_specs=[pl.BlockSpec((tm, WG), lambda i: (i, 0))] * NT + [pl.BlockSpec((D, NT * WG), lambda i: (0, 0)), row, vec, row],
        out_specs=[row, vec, vec],
        out_shape=[_sds((s, D), F32), _sds((1, D), F32), _sds((1, D), F32)],
        compiler_params=_params(1),
    )(*dparts, w, x_in, scale, dx_up)


def _dgrad_norm_bwd(d3, w, x_in, scale, dx_up, *, tm, name, after):
    groups, s, n = d3.shape

    def body(d_ref, w_ref, x_ref, sc_ref, up_ref, dx_ref, dsc_ref, dsh_ref, acc):
        i, g = pl.program_id(0), pl.program_id(1)

        @pl.when((i == 0) & (g == 0))
        def _():
            dsc_ref[...] = jnp.zeros_like(dsc_ref)
            dsh_ref[...] = jnp.zeros_like(dsh_ref)

        part = _dot(d_ref[0], w_ref[0])

        @pl.when(g == 0)
        def _():
            acc[...] = part

        @pl.when(g > 0)
        def _():
            acc[...] += part

        @pl.when(g == groups - 1)
        def _():
            dh = acc[...]
            xv = x_ref[...]
            r = _rms(xv)
            nrm = xv * r
            dsh_ref[...] += _colsum(dh)
            dsc_ref[...] += _colsum(dh * nrm)
            dx_ref[...] = up_ref[...] + _rms_bwd(dh * (1.0 + sc_ref[...]), nrm, r)

    row = pl.BlockSpec((tm, D), lambda i, g: (i, 0))
    vec = pl.BlockSpec((1, D), lambda i, g: (0, 0))
    return _call_after(
        body, after, name=name, grid=(s // tm, groups),
        in_specs=[pl.BlockSpec((1, tm, n), lambda i, g: (g, i, 0)), pl.BlockSpec((1, n, D), lambda i, g: (g, 0, 0)),
                  row, vec, row],
        out_specs=[row, vec, vec],
        out_shape=[_sds((s, D), F32), _sds((1, D), F32), _sds((1, D), F32)],
        scratch_shapes=[pltpu.VMEM((tm, D), F32)],
        compiler_params=_params(2),
    )(d3, w, x_in, scale, dx_up)


def _mix_out_bwd(dx2, mixed, gate, w_out, oa, ob, g, tm):
    s = dx2.shape[0]

    def body(dx_ref, mixed_ref, gate_ref, w_ref, oa_ref, ob_ref, g_ref, dgate_ref, dm_ref, doa_ref, dob_ref, dg_ref):
        @pl.when(pl.program_id(0) == 0)
        def _():
            dgate_ref[...] = jnp.zeros_like(dgate_ref)
            dg_ref[...] = jnp.zeros_like(dg_ref)
        dxv = dx_ref[...]
        dgate_ref[...] += _colsum(dxv * mixed_ref[...])
        dm_ref[...] = (dxv * gate_ref[...]).astype(BF16)
        dnab = _dot_nt(dm_ref[...], w_ref[...])
        for o_ref, do_ref, sl in ((oa_ref, doa_ref, slice(0, WG)), (ob_ref, dob_ref, slice(WG, D))):
            ov = o_ref[...]
            r = _rms(ov)
            nrm = ov * r
            dn = dnab[:, sl]
            dg_ref[:, sl] += _colsum(dn * nrm)
            do_ref[...] = _rms_bwd(dn * g_ref[:, sl], nrm, r)

    row = pl.BlockSpec((tm, D), lambda i: (i, 0))
    half = pl.BlockSpec((tm, WG), lambda i: (i, 0))
    vec = pl.BlockSpec((1, D), lambda i: (0, 0))
    return _call(
        body, name="mix_out_bwd", grid=(s // tm,),
        in_specs=[row, row, vec, pl.BlockSpec((D, D), lambda i: (0, 0)), half, half, vec],
        out_specs=[vec, row, half, half, vec],
        out_shape=[_sds((1, D), F32), _sds((s, D), BF16), _sds((s, WG), F32), _sds((s, WG), F32), _sds((1, D), F32)],
        compiler_params=_params(1),
    )(dx2, mixed, gate, w_out, oa, ob, g)


def _attn_a_bwd(proj, bias_tab, d_oa, after):
    s = proj.shape[1]
    nq = s // TQA
    nkb = s // 128

    def body(q_ref, k_ref, v_ref, b_ref, do_ref, dq_ref, dk_ref, dv_ref, db_ref, dkt_acc, dvt_acc):
        qi = pl.program_id(1)

        @pl.when(qi == 0)
        def _():
            dkt_acc[...] = jnp.zeros_like(dkt_acc)
            dvt_acc[...] = jnp.zeros_like(dvt_acc)
            db_ref[...] = jnp.zeros_like(db_ref)

        heads = [slice(HD * h, HD * (h + 1)) for h in range(HA)]
        qs = [q_ref[0, :, sl] for sl in heads]
        dos = [do_ref[:, sl].astype(BF16) for sl in heads]
        qts = [_transpose_bf16(x, _eye(HD)) for x in qs]
        dots = [_transpose_bf16(x, _eye(HD)) for x in dos]

        def tile(nk, off, kstart):
            kb0 = kstart // 128
            for h in range(HA):
                sl = heads[h]
                k = k_ref[0, pl.ds(kstart, nk), sl]
                v = v_ref[0, pl.ds(kstart, nk), sl]
                sc = _dot_nt(qs[h], k) * 0.125 + b_ref[h, :, off:off + nk]
                p = jnp.exp(sc - jnp.max(sc, axis=-1, keepdims=True))
                p = p / jnp.sum(p, axis=-1, keepdims=True)
                dp = _dot_nt(dos[h], v)
                ds = p * (dp - jnp.sum(dp * p, axis=-1, keepdims=True))
                db_ref[h, :, off:off + nk] += ds
                dsb = (ds * 0.125).astype(BF16)
                dq_ref[:, sl] = _dot(dsb, k).astype(BF16)
                dkt = _dot(qts[h], dsb)
                dvt = _dot(dots[h], p.astype(BF16))
                for j in range(nk // 128):
                    dkt_acc[h * nkb + kb0 + j] += dkt[:, 128 * j:128 * (j + 1)]
                    dvt_acc[h * nkb + kb0 + j] += dvt[:, 128 * j:128 * (j + 1)]

        _attn_a_cases(qi, tile)

        @pl.when(qi == nq - 1)
        def _():
            eye = _eye(128)
            for h in range(HA):
                for kb in range(nkb):
                    rows_kb = slice(128 * kb, 128 * (kb + 1))
                    dk_ref[rows_kb, heads[h]] = _transpose_f32(dkt_acc[h * nkb + kb], eye).astype(BF16)
                    dv_ref[rows_kb, heads[h]] = _transpose_f32(dvt_acc[h * nkb + kb], eye).astype(BF16)

    q_spec, k_spec, v_spec, b_spec = _attn_a_specs(s)
    blk = pl.BlockSpec((TQA, WA), lambda hp, qi: (qi, hp))
    col = pl.BlockSpec((s, WA), lambda hp, qi: (0, hp))
    return _call_after(
        body, after, name="attn_a_bwd", grid=(WG // WA, nq),
        in_specs=[q_spec, k_spec, v_spec, b_spec, blk],
        out_specs=[blk, col, col, b_spec],
        out_shape=[_sds((s, WG), BF16), _sds((s, WG), BF16), _sds((s, WG), BF16), _sds((NDEV, TQA, KWA), F32)],
        scratch_shapes=[pltpu.VMEM((HA * nkb, HD, 128), F32), pltpu.VMEM((HA * nkb, HD, 128), F32)],
        compiler_params=_params(2),
    )(proj, proj, proj, bias_tab, d_oa)


def _attn_b_bwd(proj, lsum, d_ob):
    s = proj.shape[1]
    nq = s // TQB
    nkb = s // TKB
    ndiag = TQB // TKB

    def body(q_ref, k_ref, v_ref, ls_ref, do_ref, dq_ref, dk_ref, dv_ref, dkt_acc, dvt_acc):
        qi = pl.program_id(1)
        q0 = qi * TQB

        @pl.when(qi == 0)
        def _():
            dkt_acc[...] = jnp.zeros_like(dkt_acc)
            dvt_acc[...] = jnp.zeros_like(dvt_acc)

        rows = lax.broadcasted_iota(jnp.int32, (TQB, TKB), 0)
        cols = lax.broadcasted_iota(jnp.int32, (TQB, TKB), 1)
        uj = jnp.bitwise_and(lax.broadcasted_iota(jnp.int32, (2 * TKB, TKB), 0), TKB - 1)
        us = lax.broadcasted_iota(jnp.int32, (2 * TKB, TKB), 1)
        prefix = jnp.where(uj <= us, 1.0, 0.0).astype(BF16)
        heads = [slice(HD * h, HD * (h + 1)) for h in range(2)]
        qs = [(q_ref[0, :, sl].astype(F32) * 0.125).astype(BF16) for sl in heads]
        dos = [do_ref[:, sl].astype(BF16) for sl in heads]
        qts = [_transpose_bf16(x, _eye(HD)) for x in qs]
        dots = [_transpose_bf16(x, _eye(HD)) for x in dos]
        stots = [ls_ref[0, :, HD * h:HD * h + 1] for h in range(2)]
        qds = [jnp.concatenate([qs[h], dos[h]], axis=1) for h in range(2)]
        zkv = jnp.zeros((TKB, HD), BF16)

        def tile(h, kb, diag, cl, cg, dq):
            r0 = 0 if diag is None else diag * TKB
            sl = heads[h]
            kstart = pl.multiple_of(kb * TKB, TKB)
            k = k_ref[0, pl.ds(kstart, TKB), sl]
            v = v_ref[0, pl.ds(kstart, TKB), sl]
            kv = jnp.concatenate([jnp.concatenate([k, zkv], axis=1), jnp.concatenate([zkv, v], axis=1)], axis=0)
            both = _dot_nt(qds[h][r0:, :], kv)
            lb, sp = _sb_terms(both[:, :TKB])
            if diag is not None:
                strict = rows[:TQB - r0, :] > cols[:TQB - r0, :]
                sp = jnp.where(strict, sp, 0.0)
            pre = _dot2_deep(sp, prefix) + cl[r0:, :]
            a = jnp.exp(lb - stots[h][r0:, :] + pre)
            if diag is not None:
                a = jnp.where(strict, a, 0.0)
            gz = both[:, TKB:] * a
            pg = _dot2_deep(gz, prefix) + cg[r0:, :]
            dl = gz - pg * jnp.exp(lb)
            if diag is not None:
                dl = jnp.where(strict, dl, 0.0)
            dlb = dl.astype(BF16)
            dkt_acc[h * nkb + kb] += _dot(qts[h][:, r0:], dlb)
            dvt_acc[h * nkb + kb] += _dot(dots[h][:, r0:], a.astype(BF16))
            new = (pre[:, TKB - 1:TKB], pg[:, TKB - 1:TKB], dq[r0:, :] + _dot(dlb, k))
            if r0:
                new = tuple(jnp.concatenate([old[:r0, :], x], axis=0) for old, x in zip((cl, cg, dq), new))
            return new

        state = [jnp.zeros((TQB, 1), F32), jnp.zeros((TQB, 1), F32), jnp.zeros((TQB, HD), F32)] * 2

        def step(i, st):
            st = list(st)
            for sub in range(KSTEP):
                for h in range(2):
                    st[3 * h:3 * h + 3] = tile(h, i * KSTEP + sub, None, *st[3 * h:3 * h + 3])
            return tuple(st)

        state = list(lax.fori_loop(0, q0 // (KSTEP * TKB), step, tuple(state)))
        for d in range(ndiag):
            for h in range(2):
                state[3 * h:3 * h + 3] = tile(h, q0 // TKB + d, d, *state[3 * h:3 * h + 3])
        for h in range(2):
            dq_ref[:, heads[h]] = (state[3 * h + 2] * 0.125).astype(BF16)

        @pl.when(qi == nq - 1)
        def _():
            eye = _eye(TKB)
            for h in range(2):
                for kb in range(nkb):
                    rows_kb = slice(kb * TKB, (kb + 1) * TKB)
                    dk_ref[rows_kb, heads[h]] = _transpose_f32(dkt_acc[h * nkb + kb], eye).astype(BF16)
                    dv_ref[rows_kb, heads[h]] = _transpose_f32(dvt_acc[h * nkb + kb], eye).astype(BF16)

    q_spec, k_spec, v_spec = _attn_b_specs(s)
    blk = pl.BlockSpec((TQB, 128), lambda hp, qi: (qi, hp))
    col = pl.BlockSpec((s, 128), lambda hp, qi: (0, hp))
    return _call(
        body, name="attn_b_bwd", grid=(4, nq),
        in_specs=[q_spec, k_spec, v_spec, pl.BlockSpec((1, TQB, 128), lambda hp, qi: (hp, qi, 0)), blk],
        out_specs=[blk, col, col],
        out_shape=[_sds((s, WG), BF16)] * 3,
        scratch_shapes=[pltpu.VMEM((2 * nkb, HD, TKB), F32), pltpu.VMEM((2 * nkb, HD, TKB), F32)],
        compiler_params=_params(2),
    )(proj, proj, proj, lsum, d_ob)


def _bias_fold(dtab):
    def body(t_ref, d_ref, far_ref):
        acc = jnp.zeros((NDEV, TABW), F32)
        zpad = jnp.zeros((NDEV, TAB0), F32)
        for r in range(TQA):
            row = jnp.concatenate([zpad, t_ref[:, r, :]], axis=1)
            acc = acc + (pltpu.roll(row, TABW - r, 1) if r else row)
        d_ref[...] = acc
        lane = lax.broadcasted_iota(jnp.int32, (NDEV, TABW), 1)
        far = jnp.sum(jnp.where(lane < N_FAR, acc, 0.0), axis=1, keepdims=True)
        far_ref[...] = jnp.broadcast_to(far, (NDEV, 128))

    d_fpad, d_far = _call(
        body, name="bias_fold", grid=(1,),
        in_specs=[pl.BlockSpec((NDEV, TQA, KWA), lambda i: (0, 0, 0))],
        out_specs=[pl.BlockSpec((NDEV, TABW), lambda i: (0, 0)), pl.BlockSpec((NDEV, 128), lambda i: (0, 0))],
        out_shape=[_sds((NDEV, TABW), F32), _sds((NDEV, 128), F32)],
        compiler_params=_params(1),
    )(dtab)
    d_near = d_fpad[:, N_FAR:N_FAR + N_NEAR][:, ::-1]
    return jnp.concatenate([jnp.zeros((NDEV, REL_CLIP - CHUNK + 1), F32), d_near, d_far[:, :1]], axis=1)


def _adamw(w, g, m, v):
    m = B1 * m + (1.0 - B1) * g
    v = B2 * v + (1.0 - B2) * (g * g)
    m_hat = m / (1.0 - B1 ** STEP)
    v_hat = v / (1.0 - B2 ** STEP)
    delta = -LR * (m_hat / (jnp.sqrt(v_hat) + AEPS) + WD * w)
    return delta, m, v


def _adamw_big(recv0, recv1, w, m, v):
    _, rows, cols = recv0.shape
    tr = max(t for t in range(16, 513, 16) if rows % t == 0)
    nt = rows // tr

    def body(r0_ref, r1_ref, w_ref, m_ref, v_ref, g_ref, d_ref, nm_ref, nv_ref):
        def update(r_ref):
            g = r_ref[0].astype(F32)
            for p in range(1, NDEV):
                g = g + r_ref[p].astype(F32)
            delta, nm, nv = _adamw(w_ref[0], g, m_ref[0], v_ref[0])
            g_ref[0], d_ref[0], nm_ref[0], nv_ref[0] = g, delta, nm, nv

        @pl.when(pl.program_id(0) == 0)
        def _():
            update(r0_ref)

        @pl.when(pl.program_id(0) == 1)
        def _():
            update(r1_ref)

    blk = pl.BlockSpec((1, tr, cols), lambda l, i: (l, i, 0))
    r0_spec = pl.BlockSpec((NDEV, tr, cols), lambda l, i: (0, jnp.where(l == 0, i, nt - 1), 0))
    r1_spec = pl.BlockSpec((NDEV, tr, cols), lambda l, i: (0, jnp.where(l == 1, i, 0), 0))
    return _call(
        body, name="adamw_big", grid=(2, nt),
        in_specs=[r0_spec, r1_spec, blk, blk, blk],
        out_specs=[blk] * 4,
        out_shape=[_sds((2, rows, cols), F32)] * 4,
        compiler_params=_params(2),
    )(recv0, recv1, w, m, v)


def _adamw_w_ada(cact_t, dmod, w, m, v):
    tr = 256

    def body(c_ref, dm_ref, w_ref, m_ref, v_ref, g_ref, d_ref, nm_ref, nv_ref):
        g = c_ref[:, 0:1] * dm_ref[0, 0:1, :]
        for b in range(1, NDEV):
            g = g + c_ref[:, b:b + 1] * dm_ref[0, b:b + 1, :]
        delta, nm, nv = _adamw(w_ref[0], g, m_ref[0], v_ref[0])
        g_ref[0], d_ref[0], nm_ref[0], nv_ref[0] = g, delta, nm, nv

    blk = pl.BlockSpec((1, tr, ADA_SH), lambda l, i: (l, i, 0))
    return _call(
        body, name="adamw_w_ada", grid=(2, D // tr),
        in_specs=[pl.BlockSpec((tr, NDEV), lambda l, i: (i, 0)), pl.BlockSpec((1, NDEV, ADA_SH), lambda l, i: (l, 0, 0)),
                  blk, blk, blk],
        out_specs=[blk] * 4,
        out_shape=[_sds((2, D, ADA_SH), F32)] * 4,
        compiler_params=_params(2),
    )(cact_t, dmod, w, m, v)


def _adamw_small(gath, w, m, v):
    rows = gath.shape[1]

    def body(r_ref, w_ref, m_ref, v_ref, g_ref, d_ref, nm_ref, nv_ref):
        g = r_ref[0]
        for p in range(1, NDEV):
            g = g + r_ref[p]
        delta, nm, nv = _adamw(w_ref[...], g, m_ref[...], v_ref[...])
        g_ref[...], d_ref[...], nm_ref[...], nv_ref[...] = g, delta, nm, nv

    blk = pl.BlockSpec((rows, D), lambda i: (0, 0))
    return _call(
        body, name="adamw_small", grid=(1,),
        in_specs=[pl.BlockSpec((NDEV, rows, D), lambda i: (0, 0, 0)), blk, blk, blk],
        out_specs=[blk] * 4,
        out_shape=[_sds((rows, D), F32)] * 4,
        compiler_params=_params(1),
    )(gath, w, m, v)


_PACK = (("b_ada", 2 * 6 * D), ("rel_bias", 2 * 8 * 257), ("g_a", 2 * WG), ("g_b", 2 * WG),
         ("conv_b", 2 * 2 * DFF), ("final_g", D), ("conv_w", 2 * NDEV * 3 * GU))


def _pack(parts):
    rows = []
    for name, size in _PACK:
        flat = parts[name].reshape(-1).astype(F32)
        assert flat.shape[0] == size, (name, flat.shape)
        rows.append(jnp.pad(flat, (0, -size % D)))
    out = jnp.concatenate(rows).reshape(-1, D)
    return jnp.pad(out, ((0, -out.shape[0] % 8), (0, 0)))


def _unpack(packed):
    flat = packed.reshape(-1)
    out, pos = {}, 0
    for name, size in _PACK:
        out[name] = flat[pos:pos + size]
        pos += size + (-size % D)
    return out


def kernel(x, c, w_ada, b_ada, w_in, rel_bias, g_a, g_b, w_out, w_up, conv_w, conv_b, w_down, final_g, loss_target, m_w_ada, m_b_ada, m_w_in, m_rel_bias, m_g_a, m_g_b, m_w_out, m_w_up, m_conv_w, m_conv_b, m_w_down, m_final_g, v_w_ada, v_b_ada, v_w_in, v_rel_bias, v_g_a, v_g_b, v_w_out, v_w_up, v_conv_w, v_conv_b, v_w_down, v_final_g):
    s = x.shape[1]
    assert s % TQA == 0 and s >= KWA and s % 512 == 0
    tm = 512
    tmm = min(1024, s)
    me = 4 * lax.axis_index("x") + 2 * lax.axis_index("y") + lax.axis_index("c")
    xs = x.reshape(s, D)
    target = loss_target.reshape(s, D)

    first = jnp.concatenate([c, jnp.pad(conv_w.reshape(2 * 3, GU), ((0, 1), (0, D - GU)))])
    first_all = _small_allgather(first, "gather_c_conv_w")
    c_all = first_all[:, 0, :]
    cw_all = first_all[:, 1:7, :GU].reshape(NDEV, 2, 3, GU)

    b_sl = lax.dynamic_slice(b_ada, (0, me * ADA_SH), (2, ADA_SH)).reshape(2, 1, ADA_SH)
    mod_part, cact = _mod_fwd(c_all, w_ada, b_sl)
    mod_all = _small_allgather(mod_part.reshape(2 * NDEV, ADA_SH), "gather_mod")

    up_t = [jnp.transpose(t, (0, 2, 1)) for t in (w_up, m_w_up, v_w_up)]
    shards = {"in": w_in, "out": w_out, "up": up_t[0], "down": w_down}
    order = [(kind, l) for l in range(2) for kind in _KINDS]
    mod_all, *srcs = lax.optimization_barrier((mod_all, *[shards[kind][l].astype(BF16) for kind, l in order]))
    gather_started = dict(zip(order, _exchange_start([kind for kind, _ in order], True, srcs, "weights_gather_start")))

    def gathered(kind, l, after):
        return _exchange_wait([kind], True, [gather_started[kind, l]], after, f"weights_gather_wait_{kind}{l}")[0]

    mod_all = mod_all.reshape(NDEV, 2, NDEV, ADA_SH)
    mod_me = lax.dynamic_index_in_dim(mod_all, me, axis=2, keepdims=False)
    mod = jnp.transpose(mod_me, (1, 0, 2)).reshape(2, 6, 1, D)

    saved = []
    xl = xs
    for l in range(2):
        sh_mix, sc_mix, gt_mix, sh_ffn, sc_ffn, gt_ffn = (mod[l, j] for j in range(6))
        cw = cw_all[:, l].reshape(2, 4, 3, GU)
        cb = conv_b[l].reshape(2, 4, 1, GU)
        gvec = jnp.concatenate([g_a[l], g_b[l]]).reshape(1, D)
        tab = _bias_table(rel_bias[l])

        wi = gathered("in", l, xl if l else mod)
        h1, proj = _nm_matmul(xl, sh_mix, sc_mix, wi, two_d=True, n=WG, groups=NT, out_dtype=BF16, tm=tmm,
                              name="norm_proj")
        oa = _attn_a_fwd(proj, tab)
        ob, lsum = _attn_b_fwd(proj)
        wo = gathered("out", l, ob)
        nab, mixed, x2 = _mix_out(oa, ob, gvec, wo, xl, gt_mix, tm)
        wu = gathered("up", l, x2)
        h2, u = _nm_matmul(x2, sh_ffn, sc_ffn, wu, two_d=False, n=GU, groups=NDEV, out_dtype=BF16, tm=tmm,
                           name="norm_up")
        u = u.reshape(2, 4, s, GU)
        a = _conv_act(u, cw, cb, tmm)
        wd4 = gathered("down", l, a).reshape(4, GU, D)
        ffn, x3 = _down(a, wd4, x2, gt_ffn, tmm)
        saved.append(dict(x=xl, h1=h1, proj=proj, oa=oa, ob=ob, lsum=lsum, nab=nab, mixed=mixed, x2=x2, h2=h2, u=u,
                          a=a, ffn=ffn, cw=cw, cb=cb, gvec=gvec, tab=tab, wd4=wd4, wi=wi, wo=wo, wu=wu))
        xl = x3

    loss_part, dx, d_final_g = _final_loss(xl, final_g.reshape(1, D), target, tm)
    loss = lax.psum(loss_part[0, 0], ("x", "y", "c"))

    sent = {}
    small = {"b_ada": [None, None], "rel_bias": [None, None], "g_a": [None, None], "g_b": [None, None],
             "conv_b": [None, None], "conv_w": [None, None]}

    def send(kind, l, grad):
        started, token = _exchange_start([kind], False, [grad], f"grads_start_{kind}{l}", with_token=True)
        sent[kind, l] = started[0]
        return token

    for l in (1, 0):
        sv = saved[l]
        sh_mix, sc_mix, gt_mix, sh_ffn, sc_ffn, gt_ffn = (mod[l, j] for j in range(6))
        d_gt_ffn, dff, da = _down_bwd(dx, gt_ffn, sv["ffn"], sv["wd4"], tmm)
        tok = send("down", l, _wgrad(sv["a"], dff.reshape(1, s, D), tk=tmm, name="wgrad_down").reshape(DFF, D))
        dy, d_cw, d_cb = _conv_act_bwd(sv["u"], sv["cw"], sv["cb"], da, tm, tok)
        du = _conv_transpose(dy.reshape(NDEV, s, GU), sv["cw"].reshape(NDEV, 3, GU), tmm)
        tok = send("up", l, _wgrad(du, sv["h2"].reshape(1, s, D), tk=tmm, name="wgrad_up"))
        dx2, d_sc_ffn, d_sh_ffn = _dgrad_norm_bwd(du, sv["wu"], sv["x2"], sc_ffn, dx, tm=tmm, name="dgrad_up", after=tok)
        d_gt_mix, dmixed, d_oa, d_ob, d_g = _mix_out_bwd(dx2, sv["mixed"], gt_mix, sv["wo"], sv["oa"], sv["ob"],
                                                         sv["gvec"], tm)
        tok = send("out", l, _wgrad(sv["nab"].reshape(1, s, D), dmixed.reshape(1, s, D), tk=tmm,
                                    name="wgrad_out").reshape(D, D))
        dqa, dka, dva, d_tab = _attn_a_bwd(sv["proj"], sv["tab"], d_oa, tok)
        dqb, dkb, dvb = _attn_b_bwd(sv["proj"], sv["lsum"], d_ob)
        dparts = (dqa, dka, dva, dqb, dkb, dvb)
        if l:
            tok = send("in", l, _wgrad_in(sv["h1"], dparts, tmm, None))
        dx, d_sc_mix, d_sh_mix = _dgrad_in(dparts, sv["wi"], sv["x"], sc_mix, dx2, tm, tok)
        small["b_ada"][l] = jnp.concatenate([d_sh_mix, d_sc_mix, d_gt_mix, d_sh_ffn, d_sc_ffn, d_gt_ffn], axis=1)
        small["rel_bias"][l] = _bias_fold(d_tab)
        small["g_a"][l], small["g_b"][l] = d_g[:, :WG], d_g[:, WG:]
        small["conv_b"][l] = d_cb
        small["conv_w"][l] = d_cw.reshape(NDEV, 3, GU)
    grad_x = dx.reshape(1, s, D)

    contrib = {k: jnp.stack(vs) for k, vs in small.items()}
    contrib["final_g"] = d_final_g
    gath = _small_allgather(_pack(contrib), "gather_small_grads")
    tok = send("in", 0, _wgrad_in(saved[0]["h1"], dparts, tmm, gath))

    def place_conv_w(t):
        return lax.dynamic_update_slice(jnp.zeros((2, NDEV, 3, GU), F32), t.reshape(2, 1, 3, GU), (0, me, 0, 0))

    def packed_params(b, rb, ga, gb, cb_, fg, cw_):
        return _pack({"b_ada": b, "rel_bias": rb, "g_a": ga, "g_b": gb, "conv_b": cb_, "final_g": fg,
                      "conv_w": place_conv_w(cw_)})

    sm = _adamw_small(gath,
                      packed_params(b_ada, rel_bias, g_a, g_b, conv_b, final_g, conv_w),
                      packed_params(m_b_ada, m_rel_bias, m_g_a, m_g_b, m_conv_b, m_final_g, m_conv_w),
                      packed_params(v_b_ada, v_rel_bias, v_g_a, v_g_b, v_conv_b, v_final_g, v_conv_w))
    sm = [_unpack(t) for t in sm]

    dmod_all = gath[:, :12, :].reshape(NDEV, 2, 6 * D)
    dmod_sl = jnp.transpose(lax.dynamic_slice(dmod_all, (0, 0, me * ADA_SH), (NDEV, 2, ADA_SH)), (1, 0, 2))
    ada = _adamw_w_ada(cact.T, dmod_sl, w_ada, m_w_ada, v_w_ada)

    big = {}
    for kind, (w, m, v) in (("down", (w_down, m_w_down, v_w_down)), ("up", up_t),
                            ("out", (w_out, m_w_out, v_w_out)), ("in", (w_in, m_w_in, v_w_in))):
        recv0, recv1 = _exchange_wait([kind, kind], False, [sent[kind, 0], sent[kind, 1]], tok, f"grads_wait_{kind}")
        big[kind] = _adamw_big(recv0, recv1, w, m, v)
        tok = big[kind][0]
    big["up"] = [jnp.transpose(t, (0, 2, 1)) for t in big["up"]]

    def small_out(j, name):
        t = sm[j][name]
        if name == "b_ada":
            return t.reshape(2, 6 * D)
        if name == "rel_bias":
            return t.reshape(2, 8, 257)
        if name in ("g_a", "g_b"):
            return t.reshape(2, WG)
        if name == "conv_b":
            return t.reshape(2, 2 * DFF)
        if name == "final_g":
            return t.reshape(D)
        t = t.reshape(2, NDEV, 3, GU)
        return lax.dynamic_index_in_dim(t, me, axis=1, keepdims=False)

    def group(j):
        return (ada[j], small_out(j, "b_ada"), big["in"][j], small_out(j, "rel_bias"), small_out(j, "g_a"),
                small_out(j, "g_b"), big["out"][j], big["up"][j], small_out(j, "conv_w"), small_out(j, "conv_b"),
                big["down"][j], small_out(j, "final_g"))

    return (loss, grad_x, *group(0), *group(1), *group(2), *group(3))
```
